```python
import math
import jax
import jax.numpy as jnp
from jax import lax
import numpy as np

D_MODEL = 1024
BATCH = 2
SEQ = 8192
DEPTH = 2
DEC_BATCH = 32
DEC_SEQ = 8
PAST_LEN = 16384
PAGE_SIZE = 128

N_EVEN = (DEPTH + 1) // 2
N_ODD = DEPTH // 2
ALPHA = (2.0 * DEPTH) ** 0.25
BETA = (8.0 * DEPTH) ** -0.25
LN_EPS = 1e-5
RMS_EPS = 1e-5
NEG = -1e30

S5_DIM = D_MODEL // 2
S5_GROUP = 16
S5_GROUPS = S5_DIM // S5_GROUP
S5_STATE = 64
DT_MIN = 1e-3
DT_MAX = 1e-1

HEAD_DIM = 64
N_HEADS = (D_MODEL // 2) // HEAD_DIM
KV_GROUPS = 2
HEADS_PER_GROUP = N_HEADS // KV_GROUPS
CMP_STRIDE = 16
CMP_LEN = 2 * CMP_STRIDE
SEL_BLOCK = 64
N_SEL = 16
WINDOW = 512
Q_BLOCK = 128
ROPE_THETA = 500000.0
ROT_DIM = HEAD_DIM // 4
FORCE = 1e4
NSA_Q = N_HEADS * HEAD_DIM
NSA_KV = 2 * KV_GROUPS * HEAD_DIM
IN_EVEN = S5_DIM + NSA_Q + 3 * NSA_KV + 3 * N_HEADS
MIX_EVEN = S5_DIM + NSA_Q

SC_DIM = D_MODEL // 2
SC_WIDTH = 3
SSD_HEAD_DIM = 64
SSD_HEADS = 16
SSD_INNER = SSD_HEADS * SSD_HEAD_DIM
SSD_GROUPS = 4
SSD_STATE = 128
SSD_CONV = 4
SSD_CONV_DIM = SSD_INNER + 2 * SSD_GROUPS * SSD_STATE
SSD_CHUNK = 128
IN_ODD = 3 * SC_DIM + SSD_INNER + SSD_CONV_DIM + SSD_HEADS
MIX_ODD = SC_DIM + SSD_INNER

D_FF = 2816
N_EXPERTS = 8
TOP_K = 2

kernel_name = 'hybrid_s5_nsa_shortconv_ssd_deepnorm_step'


def layer_norm(x, g, b):
    x = x.astype(jnp.float32)
    mu = jnp.mean(x, -1, keepdims=True)
    xc = x - mu
    var = jnp.mean(xc * xc, -1, keepdims=True)
    return xc * lax.rsqrt(var + LN_EPS) * g + b


def rope(x, pos):
    half = ROT_DIM // 2
    inv = ROPE_THETA ** (-jnp.arange(half, dtype=jnp.float32) * 2.0 / ROT_DIM)
    ang = pos.astype(jnp.float32)[:, None] * inv[None, :]
    cos = jnp.cos(ang)[:, None, :]
    sin = jnp.sin(ang)[:, None, :]
    x1 = x[..., :half]
    x2 = x[..., half:ROT_DIM]
    return jnp.concatenate([x1 * cos - x2 * sin, x2 * cos + x1 * sin, x[..., ROT_DIM:]], axis=-1)


def masked_softmax(s, mask):
    s = jnp.where(mask, s.astype(jnp.float32), NEG)
    m = jnp.max(s, -1, keepdims=True)
    p = jnp.where(mask, jnp.exp(s - m), 0.0)
    return p / jnp.maximum(jnp.sum(p, -1, keepdims=True), 1e-30)


def last_rows(x, n):
    t = x.shape[1]
    if t < n:
        x = jnp.pad(x, [(0, 0), (n - t, 0)] + [(0, 0)] * (x.ndim - 2))
    return x[:, x.shape[1] - n:]


def causal_conv(x, buf, w, b):
    t = x.shape[1]
    width = w.shape[0]
    xp = jnp.concatenate([buf, x], axis=1)
    y = b + sum(xp[:, j:j + t] * w[j] for j in range(width))
    return y, xp[:, xp.shape[1] - (width - 1):]


def swiglu(x, w_gu, w_down):
    gu = x @ w_gu
    return (jax.nn.silu(gu[..., :D_FF]) * gu[..., D_FF:]) @ w_down


def moe_ffn(x, w_r, b_r, w_gu, w_down):
    logits = (x @ w_r + b_r).astype(jnp.float32)
    top_v, top_i = lax.top_k(logits, TOP_K)
    gate = jax.nn.softmax(top_v, axis=-1)
    comb = jnp.sum(jax.nn.one_hot(top_i, N_EXPERTS, dtype=jnp.float32) * gate[..., None], axis=-2)
    out = jnp.zeros_like(x)
    for e in range(N_EXPERTS):
        out = out + comb[..., e:e + 1] * swiglu(x, w_gu[e], w_down[e])
    return out


def s5_mixer(u, h0, lam_re, lam_im, log_dt, b, c, d, w_glu):
    f32 = jnp.float32
    bt, t, _ = u.shape
    ug = u.astype(f32).reshape(bt, t, S5_GROUPS, S5_GROUP)
    lam_re = lam_re.astype(f32)
    lam_im = lam_im.astype(f32)
    dt = jnp.exp(log_dt.astype(f32))[:, None]
    mag = jnp.exp(lam_re * dt)
    ang = lam_im * dt
    ab_re = mag * jnp.cos(ang)
    ab_im = mag * jnp.sin(ang)
    den = lam_re * lam_re + lam_im * lam_im
    nr = ab_re - 1.0
    coef_re = (nr * lam_re + ab_im * lam_im) / den
    coef_im = (ab_im * lam_re - nr * lam_im) / den
    b_re = b[..., 0].astype(f32)
    b_im = b[..., 1].astype(f32)
    bb_re = coef_re[..., None] * b_re - coef_im[..., None] * b_im
    bb_im = coef_re[..., None] * b_im + coef_im[..., None] * b_re
    bu_re = jnp.einsum('gnk,btgk->btgn', bb_re, ug)
    bu_im = jnp.einsum('gnk,btgk->btgn', bb_im, ug)
    a_re = jnp.broadcast_to(ab_re, bu_re.shape)
    a_im = jnp.broadcast_to(ab_im, bu_re.shape)

    def combine(e1, e2):
        a1r, a1i, x1r, x1i = e1
        a2r, a2i, x2r, x2i = e2
        return (a2r * a1r - a2i * a1i, a2r * a1i + a2i * a1r,
                a2r * x1r - a2i * x1i + x2r, a2r * x1i + a2i * x1r + x2i)

    cr, ci, xr, xi = lax.associative_scan(combine, (a_re, a_im, bu_re, bu_im), axis=1)
    h0 = h0.astype(f32)
    h_re = h0[..., 0][:, None]
    h_im = h0[..., 1][:, None]
    s_re = xr + cr * h_re - ci * h_im
    s_im = xi + cr * h_im + ci * h_re
    y = (jnp.einsum('gkn,btgn->btgk', c[..., 0].astype(f32), s_re)
         - jnp.einsum('gkn,btgn->btgk', c[..., 1].astype(f32), s_im)
         + d.astype(f32) * ug).reshape(bt, t, S5_DIM)
    z = jax.nn.gelu(y)
    out = z * jax.nn.sigmoid(z @ w_glu)
    new_state = jnp.stack([s_re[:, -1], s_im[:, -1]], axis=-1)
    return out, new_state


def compress(k, w1, w2, pe):
    bt, t, g, dh = k.shape
    n_ch = -(-t // CMP_STRIDE)
    k = jnp.pad(k, ((0, 0), (0, n_ch * CMP_STRIDE - t), (0, 0), (0, 0)))
    ch = k.reshape(bt, n_ch, CMP_STRIDE, g, dh)
    blocks = jnp.concatenate([ch[:, :-1], ch[:, 1:]], axis=2) + pe[None, None, :, None, :]
    flat = blocks.transpose(0, 1, 3, 2, 4).reshape(bt, n_ch - 1, g, CMP_LEN * dh)
    return jax.nn.gelu(flat @ w1) @ w2


def sel_blocks(k):
    bt, t, g, dh = k.shape
    n = -(-t // SEL_BLOCK)
    k = jnp.pad(k, ((0, 0), (0, n * SEL_BLOCK - t), (0, 0), (0, 0)))
    return k.reshape(bt, n, SEL_BLOCK, g, dh)


def nsa_block(q, qpos, kc, vc, kb, vb, kw, vw, wpos, gates):
    bt, qb = q.shape[0], q.shape[1]
    scale = HEAD_DIM ** -0.5
    qg = q.reshape(bt, qb, KV_GROUPS, HEADS_PER_GROUP, HEAD_DIM)
    n_cmp = kc.shape[1]
    cend = jnp.arange(n_cmp) * CMP_STRIDE + (CMP_LEN - 1)
    cmask = cend[None, :] <= qpos[:, None]
    p_c = masked_softmax(jnp.einsum('bqgrd,bngd->bgrqn', qg, kc) * scale, cmask)
    o_c = jnp.einsum('bgrqn,bngd->bqgrd', p_c, vc)
    n_sel = kb.shape[1]
    ratio = SEL_BLOCK // CMP_STRIDE
    imp = jnp.sum(p_c, axis=2)
    imp = jnp.pad(imp, ((0, 0), (0, 0), (0, 0), (0, n_sel * ratio - n_cmp)))
    imp = imp.reshape(bt, KV_GROUPS, qb, n_sel, ratio).sum(-1)
    blk = jnp.arange(n_sel)[None, :]
    jq = (qpos // SEL_BLOCK)[:, None]
    forced = (blk == 0) | (blk == jq) | (blk == jq - 1)
    score = jnp.where(blk <= jq, imp + jnp.where(forced, FORCE, 0.0), NEG)
    k_eff = min(N_SEL, n_sel)
    top_s, idx = lax.top_k(score, k_eff)
    valid = top_s > 0.5 * NEG
    bi = jnp.arange(bt)[:, None, None, None]
    gi = jnp.arange(KV_GROUPS)[None, :, None, None]
    ks = kb.transpose(0, 3, 1, 2, 4)[bi, gi, idx]
    vs = vb.transpose(0, 3, 1, 2, 4)[bi, gi, idx]
    kpos = idx[..., None] * SEL_BLOCK + jnp.arange(SEL_BLOCK)
    smask = (valid[..., None] & (kpos <= qpos[None, None, :, None, None]))
    smask = smask.reshape(bt, KV_GROUPS, 1, qb, k_eff * SEL_BLOCK)
    s_s = jnp.einsum('bqgrd,bgqksd->bgrqks', qg, ks).reshape(bt, KV_GROUPS, HEADS_PER_GROUP, qb, k_eff * SEL_BLOCK)
    p_s = masked_softmax(s_s * scale, smask)
    o_s = jnp.einsum('bgrqm,bgqmd->bqgrd', p_s, vs.reshape(bt, KV_GROUPS, qb, k_eff * SEL_BLOCK, HEAD_DIM))
    wmask = (wpos[None, :] <= qpos[:, None]) & (wpos[None, :] > qpos[:, None] - WINDOW) & (wpos[None, :] >= 0)
    p_w = masked_softmax(jnp.einsum('bqgrd,blgd->bgrql', qg, kw) * scale, wmask)
    o_w = jnp.einsum('bgrql,blgd->bqgrd', p_w, vw)
    g = gates.reshape(bt, qb, KV_GROUPS, HEADS_PER_GROUP, 3)
    o = g[..., 0:1] * o_c + g[..., 1:2] * o_s + g[..., 2:3] * o_w
    return o.reshape(bt, qb, NSA_Q)


def even_project(x, pos, w_in):
    bt, t, _ = x.shape
    z = x @ w_in
    o = [0]
    for w in (S5_DIM, NSA_Q, NSA_KV, NSA_KV, NSA_KV, 3 * N_HEADS):
        o.append(o[-1] + w)
    u = z[..., o[0]:o[1]]
    q = rope(z[..., o[1]:o[2]].reshape(bt, t, N_HEADS, HEAD_DIM), pos)

    def kv(a, b):
        r = z[..., a:b].reshape(bt, t, 2, KV_GROUPS, HEAD_DIM)
        return jnp.stack([rope(r[:, :, 0], pos), r[:, :, 1]], axis=2)

    kvc = kv(o[2], o[3])
    kvs = kv(o[3], o[4])
    kvw = kv(o[4], o[5])
    gates = jax.nn.sigmoid(z[..., o[5]:o[6]]).reshape(bt, t, N_HEADS, 3)
    return u, q, kvc, kvs, kvw, gates


def even_prompt(x, w_in, s5p, cmpp, w_out, w_buf):
    bt, t, _ = x.shape
    pos = jnp.arange(t)
    u, q, kvc, kvs, kvw, gates = even_project(x, pos, w_in)
    y_s5, s5_state = s5_mixer(u, jnp.zeros((bt, S5_GROUPS, S5_STATE, 2), jnp.float32), *s5p)
    kc = compress(kvc[:, :, 0], cmpp[0], cmpp[1], cmpp[2])
    vc = compress(kvc[:, :, 1], cmpp[3], cmpp[4], cmpp[5])
    kb = sel_blocks(kvs[:, :, 0])
    vb = sel_blocks(kvs[:, :, 1])
    kw_pad = jnp.pad(kvw, ((0, 0), (WINDOW, 0), (0, 0), (0, 0), (0, 0)))

    def q_block(i):
        start = i * Q_BLOCK
        qpos = start + jnp.arange(Q_BLOCK)
        qs = lax.dynamic_slice_in_dim(q, start, Q_BLOCK, axis=1)
        gs = lax.dynamic_slice_in_dim(gates, start, Q_BLOCK, axis=1)
        win = lax.dynamic_slice_in_dim(kw_pad, start, WINDOW + Q_BLOCK, axis=1)
        wpos = start - WINDOW + jnp.arange(WINDOW + Q_BLOCK)
        return nsa_block(qs, qpos, kc, vc, kb, vb, win[:, :, 0], win[:, :, 1], wpos, gs)

    o = lax.map(q_block, jnp.arange(t // Q_BLOCK))
    y_nsa = o.transpose(1, 0, 2, 3).reshape(bt, t, NSA_Q)
    out = jnp.concatenate([y_s5, y_nsa], axis=-1) @ w_out
    new_rows = jnp.concatenate([kvc, kvs], axis=2)
    return out, s5_state, new_rows, last_rows(kvw, w_buf)


def even_sample(x, s5_h0, pool, page_table, win_buf, w_in, s5p, cmpp, w_out):
    bt, t, _ = x.shape
    pos = PAST_LEN + jnp.arange(t)
    u, q, kvc, kvs, kvw, gates = even_project(x, pos, w_in)
    y_s5, s5_state = s5_mixer(u, s5_h0, *s5p)
    past = pool[page_table].astype(jnp.float32)
    past = past.reshape(bt, past.shape[1] * past.shape[2], 4, KV_GROUPS, HEAD_DIM)
    new_rows = jnp.concatenate([kvc, kvs], axis=2)
    full = jnp.concatenate([past, new_rows], axis=1)
    kc = compress(full[:, :, 0], cmpp[0], cmpp[1], cmpp[2])
    vc = compress(full[:, :, 1], cmpp[3], cmpp[4], cmpp[5])
    kb = sel_blocks(full[:, :, 2])
    vb = sel_blocks(full[:, :, 3])
    w_buf = win_buf.shape[1]
    win = jnp.concatenate([win_buf.astype(jnp.float32), kvw], axis=1)
    wpos = PAST_LEN - w_buf + jnp.arange(w_buf + t)
    y_nsa = nsa_block(q, pos, kc, vc, kb, vb, win[:, :, 0], win[:, :, 1], wpos, gates)
    out = jnp.concatenate([y_s5, y_nsa], axis=-1) @ w_out
    return out, s5_state, new_rows, win[:, t:]


def ssd_scan(x, dt, a, bm, cm, h0, chunk):
    bt, t, nh, p = x.shape
    nch = t // chunk
    r = nh // SSD_GROUPS
    tri = jnp.arange(chunk)[:, None] >= jnp.arange(chunk)[None, :]

    def to_chunks(v):
        return jnp.moveaxis(v.reshape((bt, nch, chunk) + v.shape[2:]), 1, 0)

    def step(h, inp):
        xc, dtc, bc, cc = inp
        cum = jnp.cumsum(dtc * a, axis=1)
        seg = cum[:, :, None, :] - cum[:, None, :, :]
        decay = jnp.exp(jnp.where(tri[None, :, :, None], seg, NEG)).reshape(bt, chunk, chunk, SSD_GROUPS, r)
        cb = jnp.einsum('btgn,bsgn->btsg', cc, bc)
        xg = xc.reshape(bt, chunk, SSD_GROUPS, r, p)
        dg = dtc.reshape(bt, chunk, SSD_GROUPS, r)
        w = cb[..., None] * decay * dg[:, None]
        y_intra = jnp.einsum('btsgr,bsgrp->btgrp', w, xg)
        hg = h.reshape(bt, SSD_GROUPS, r, p, SSD_STATE)
        y_inter = jnp.einsum('btgn,bgrpn->btgrp', cc, hg) * jnp.exp(cum).reshape(bt, chunk, SSD_GROUPS, r)[..., None]
        wt = (jnp.exp(cum[:, -1:, :] - cum) * dtc).reshape(bt, chunk, SSD_GROUPS, r)
        h_new = (hg * jnp.exp(cum[:, -1]).reshape(bt, SSD_GROUPS, r)[..., None, None]
                 + jnp.einsum('bsgr,bsgrp,bsgn->bgrpn', wt, xg, bc))
        return h_new.reshape(bt, nh, p, SSD_STATE), (y_intra + y_inter).reshape(bt, chunk, nh, p)

    h_fin, ys = lax.scan(step, h0, (to_chunks(x), to_chunks(dt), to_chunks(bm), to_chunks(cm)))
    return jnp.moveaxis(ys, 0, 1).reshape(bt, t, nh, p), h_fin


def gated_rmsnorm(y, z, g):
    v = y * jax.nn.silu(z)
    bt, t, _ = v.shape
    vg = v.reshape(bt, t, SSD_GROUPS, SSD_INNER // SSD_GROUPS)
    vg = vg * lax.rsqrt(jnp.mean(vg * vg, -1, keepdims=True) + RMS_EPS)
    return vg.reshape(bt, t, SSD_INNER) * g


def odd_mixer(x, sc_buf, conv_buf, h0, chunk, w_in, sc_w, sc_b, cv_w, cv_b, dt_bias, a_log, d_skip, norm_g, w_out):
    f32 = jnp.float32
    bt, t, _ = x.shape
    z = x @ w_in
    o1 = SC_DIM
    o2 = 2 * SC_DIM
    o3 = 3 * SC_DIM
    o4 = o3 + SSD_INNER
    o5 = o4 + SSD_CONV_DIM
    sc_h = z[..., :o1]
    sc_bg = z[..., o1:o2]
    sc_cg = z[..., o2:o3]
    zg = z[..., o3:o4]
    xbc = z[..., o4:o5]
    dt_raw = z[..., o5:]
    conv_sc, new_sc = causal_conv(sc_cg * sc_h, sc_buf.astype(f32), sc_w, sc_b)
    y_sc = sc_bg * conv_sc
    xbc_c, new_conv = causal_conv(xbc, conv_buf.astype(f32), cv_w, cv_b)
    xbc_c = jax.nn.silu(xbc_c)
    gn = SSD_GROUPS * SSD_STATE
    xs = xbc_c[..., :SSD_INNER].reshape(bt, t, SSD_HEADS, SSD_HEAD_DIM)
    bm = xbc_c[..., SSD_INNER:SSD_INNER + gn].reshape(bt, t, SSD_GROUPS, SSD_STATE)
    cm = xbc_c[..., SSD_INNER + gn:].reshape(bt, t, SSD_GROUPS, SSD_STATE)
    dt = jax.nn.softplus((dt_raw + dt_bias).astype(f32))
    a = -jnp.exp(a_log.astype(f32))
    y, h_new = ssd_scan(xs.astype(f32), dt, a, bm.astype(f32), cm.astype(f32), h0.astype(f32), chunk)
    y = (y + d_skip[:, None] * xs).reshape(bt, t, SSD_INNER)
    y = gated_rmsnorm(y, zg, norm_g)
    out = jnp.concatenate([y_sc, y], axis=-1) @ w_out
    return out, new_sc, new_conv, h_new


def setup_inputs(seed: int = 0) -> dict:
    key = jax.random.key(seed)
    ks = iter(jax.random.split(key, 64))
    f32 = jnp.float32

    def nrm(shape, s):
        return jax.random.normal(next(ks), shape, f32) * s

    def unif(shape, lo, hi):
        return jax.random.uniform(next(ks), shape, f32, lo, hi)

    n_pages = PAST_LEN // PAGE_SIZE
    n_used = DEC_BATCH * n_pages
    n_pool = n_used + max(1, n_used // 4)
    w_buf = min(WINDOW, PAST_LEN)
    D = D_MODEL
    inp = {}
    inp['x_prompt'] = nrm((BATCH, SEQ, D), 1.0)
    inp['x_sample'] = nrm((DEC_BATCH, DEC_SEQ, D), 1.0)
    inp['state_s5'] = nrm((N_EVEN, DEC_BATCH, S5_GROUPS, S5_STATE, 2), 0.5)
    inp['cache_nsa_kv'] = nrm((N_EVEN, n_pool, PAGE_SIZE, 4, KV_GROUPS, HEAD_DIM), 1.0)
    inp['state_win_kv'] = nrm((N_EVEN, DEC_BATCH, w_buf, 2, KV_GROUPS, HEAD_DIM), 1.0)
    inp['state_sc_conv'] = nrm((N_ODD, DEC_BATCH, SC_WIDTH - 1, SC_DIM), 1.0)
    inp['state_ssd_conv'] = nrm((N_ODD, DEC_BATCH, SSD_CONV - 1, SSD_CONV_DIM), 1.0)
    inp['state_ssd'] = nrm((N_ODD, DEC_BATCH, SSD_HEADS, SSD_HEAD_DIM, SSD_STATE), 0.1)
    inp['page_table'] = jax.random.permutation(next(ks), n_pool)[:n_used].reshape(DEC_BATCH, n_pages).astype(jnp.int32)
    inp['ln_g'] = 1.0 + nrm((DEPTH, 2, D), 0.02)
    inp['ln_b'] = nrm((DEPTH, 2, D), 0.02)
    inp['w_in_even'] = nrm((N_EVEN, D, IN_EVEN), D ** -0.5)
    inp['s5_lam_re'] = -0.5 + nrm((N_EVEN, S5_GROUPS, S5_STATE), 0.01)
    inp['s5_lam_im'] = math.pi * jnp.arange(S5_STATE, dtype=f32) + nrm((N_EVEN, S5_GROUPS, S5_STATE), 0.01)
    inp['s5_log_dt'] = unif((N_EVEN, S5_GROUPS), math.log(DT_MIN), math.log(DT_MAX))
    inp['s5_b'] = nrm((N_EVEN, S5_GROUPS, S5_STATE, S5_GROUP, 2), (2 * S5_GROUP) ** -0.5)
    inp['s5_c'] = nrm((N_EVEN, S5_GROUPS, S5_GROUP, S5_STATE, 2), (2 * S5_STATE) ** -0.5)
    inp['s5_d'] = nrm((N_EVEN, S5_GROUPS, S5_GROUP), 1.0)
    inp['s5_w_glu'] = nrm((N_EVEN, S5_DIM, S5_DIM), S5_DIM ** -0.5)
    inp['nsa_wk1'] = nrm((N_EVEN, CMP_LEN * HEAD_DIM, HEAD_DIM), (CMP_LEN * HEAD_DIM) ** -0.5)
    inp['nsa_wk2'] = nrm((N_EVEN, HEAD_DIM, HEAD_DIM), HEAD_DIM ** -0.5)
    inp['nsa_pe_k'] = nrm((N_EVEN, CMP_LEN, HEAD_DIM), 0.02)
    inp['nsa_wv1'] = nrm((N_EVEN, CMP_LEN * HEAD_DIM, HEAD_DIM), (CMP_LEN * HEAD_DIM) ** -0.5)
    inp['nsa_wv2'] = nrm((N_EVEN, HEAD_DIM, HEAD_DIM), HEAD_DIM ** -0.5)
    inp['nsa_pe_v'] = nrm((N_EVEN, CMP_LEN, HEAD_DIM), 0.02)
    inp['w_out_even'] = nrm((N_EVEN, MIX_EVEN, D), MIX_EVEN ** -0.5 * BETA)
    inp['ffn_w_gu'] = nrm((N_EVEN, D, 2 * D_FF), D ** -0.5)
    inp['ffn_w_down'] = nrm((N_EVEN, D_FF, D), D_FF ** -0.5 * BETA)
    inp['w_in_odd'] = nrm((N_ODD, D, IN_ODD), D ** -0.5)
    inp['sc_conv_w'] = nrm((N_ODD, SC_WIDTH, SC_DIM), SC_WIDTH ** -0.5)
    inp['sc_conv_b'] = nrm((N_ODD, SC_DIM), 0.02)
    inp['ssd_conv_w'] = nrm((N_ODD, SSD_CONV, SSD_CONV_DIM), SSD_CONV ** -0.5)
    inp['ssd_conv_b'] = nrm((N_ODD, SSD_CONV_DIM), 0.02)
    dt0 = jnp.exp(unif((N_ODD, SSD_HEADS), math.log(DT_MIN), math.log(DT_MAX)))
    inp['ssd_dt_bias'] = dt0 + jnp.log(-jnp.expm1(-dt0))
    inp['ssd_a_log'] = jnp.log(unif((N_ODD, SSD_HEADS), 1.0, 16.0))
    inp['ssd_d'] = 1.0 + nrm((N_ODD, SSD_HEADS), 0.1)
    inp['ssd_norm_g'] = 1.0 + nrm((N_ODD, SSD_INNER), 0.02)
    inp['w_out_odd'] = nrm((N_ODD, MIX_ODD, D), MIX_ODD ** -0.5 * BETA)
    inp['moe_router'] = nrm((N_ODD, D, N_EXPERTS), D ** -0.5)
    inp['moe_router_b'] = nrm((N_ODD, N_EXPERTS), 0.01)
    inp['moe_w_gu'] = nrm((N_ODD, N_EXPERTS, D, 2 * D_FF), D ** -0.5)
    inp['moe_w_down'] = nrm((N_ODD, N_EXPERTS, D_FF, D), D_FF ** -0.5 * BETA)
    return inp


def reference(x_prompt, x_sample, state_s5, cache_nsa_kv, state_win_kv, state_sc_conv, state_ssd_conv, state_ssd,
              page_table, ln_g, ln_b, w_in_even, s5_lam_re, s5_lam_im, s5_log_dt, s5_b, s5_c, s5_d, s5_w_glu,
              nsa_wk1, nsa_wk2, nsa_pe_k, nsa_wv1, nsa_wv2, nsa_pe_v, w_out_even, ffn_w_gu, ffn_w_down,
              w_in_odd, sc_conv_w, sc_conv_b, ssd_conv_w, ssd_conv_b, ssd_dt_bias, ssd_a_log, ssd_d, ssd_norm_g,
              w_out_odd, moe_router, moe_router_b, moe_w_gu, moe_w_down):
    f32 = jnp.float32
    hp = x_prompt.astype(f32)
    hs = x_sample.astype(f32)
    bp = hp.shape[0]
    w_buf = state_win_kv.shape[2]
    s5_p, s5_s, kv_p, kv_s, win_p, win_s = [], [], [], [], [], []
    scc_p, scc_s, sdc_p, sdc_s, ssd_p, ssd_s = [], [], [], [], [], []
    for l in range(DEPTH):
        i = l // 2
        if l % 2 == 0:
            s5p = (s5_lam_re[i], s5_lam_im[i], s5_log_dt[i], s5_b[i], s5_c[i], s5_d[i], s5_w_glu[i])
            cmpp = (nsa_wk1[i], nsa_wk2[i], nsa_pe_k[i], nsa_wv1[i], nsa_wv2[i], nsa_pe_v[i])
            mp, a1, a2, a3 = even_prompt(hp, w_in_even[i], s5p, cmpp, w_out_even[i], w_buf)
            ms, b1, b2, b3 = even_sample(hs, state_s5[i], cache_nsa_kv[i], page_table, state_win_kv[i],
                                         w_in_even[i], s5p, cmpp, w_out_even[i])
            s5_p.append(a1)
            kv_p.append(a2)
            win_p.append(a3)
            s5_s.append(b1)
            kv_s.append(b2)
            win_s.append(b3)
        else:
            oddp = (w_in_odd[i], sc_conv_w[i], sc_conv_b[i], ssd_conv_w[i], ssd_conv_b[i], ssd_dt_bias[i],
                    ssd_a_log[i], ssd_d[i], ssd_norm_g[i], w_out_odd[i])
            mp, a1, a2, a3 = odd_mixer(hp, jnp.zeros((bp, SC_WIDTH - 1, SC_DIM), f32),
                                       jnp.zeros((bp, SSD_CONV - 1, SSD_CONV_DIM), f32),
                                       jnp.zeros((bp, SSD_HEADS, SSD_HEAD_DIM, SSD_STATE), f32),
                                       SSD_CHUNK, *oddp)
            ms, b1, b2, b3 = odd_mixer(hs, state_sc_conv[i], state_ssd_conv[i], state_ssd[i], hs.shape[1], *oddp)
            scc_p.append(a1)
            sdc_p.append(a2)
            ssd_p.append(a3)
            scc_s.append(b1)
            sdc_s.append(b2)
            ssd_s.append(b3)
        hp = layer_norm(ALPHA * hp + mp, ln_g[l, 0], ln_b[l, 0])
        hs = layer_norm(ALPHA * hs + ms, ln_g[l, 0], ln_b[l, 0])
        if l % 2 == 0:
            fp = swiglu(hp, ffn_w_gu[i], ffn_w_down[i])
            fs = swiglu(hs, ffn_w_gu[i], ffn_w_down[i])
        else:
            fp = moe_ffn(hp, moe_router[i], moe_router_b[i], moe_w_gu[i], moe_w_down[i])
            fs = moe_ffn(hs, moe_router[i], moe_router_b[i], moe_w_gu[i], moe_w_down[i])
        hp = layer_norm(ALPHA * hp + fp, ln_g[l, 1], ln_b[l, 1])
        hs = layer_norm(ALPHA * hs + fs, ln_g[l, 1], ln_b[l, 1])
    return (hp.astype(x_prompt.dtype), hs.astype(x_sample.dtype),
            jnp.stack(s5_p).astype(state_s5.dtype), jnp.stack(s5_s).astype(state_s5.dtype),
            jnp.stack(kv_p).astype(cache_nsa_kv.dtype), jnp.stack(kv_s).astype(cache_nsa_kv.dtype),
            jnp.stack(win_p).astype(state_win_kv.dtype), jnp.stack(win_s).astype(state_win_kv.dtype),
            jnp.stack(scc_p).astype(state_sc_conv.dtype), jnp.stack(scc_s).astype(state_sc_conv.dtype),
            jnp.stack(sdc_p).astype(state_ssd_conv.dtype), jnp.stack(sdc_s).astype(state_ssd_conv.dtype),
            jnp.stack(ssd_p).astype(state_ssd.dtype), jnp.stack(ssd_s).astype(state_ssd.dtype))
```

```python
import functools
import math

import jax
import jax.numpy as jnp
from jax import lax
from jax.experimental import pallas as pl
from jax.experimental.pallas import tpu as pltpu

D_MODEL = 1024
SEQ = 8192
DEPTH = 2
DEC_SEQ = 8
PAST_LEN = 16384
ALPHA = (2.0 * DEPTH) ** 0.25
LN_EPS = 1e-5
RMS_EPS = 1e-5
NEG = -1e30

S5_DIM = D_MODEL // 2
S5_GROUP = 16
S5_GROUPS = S5_DIM // S5_GROUP
S5_STATE = 64

HEAD_DIM = 64
N_HEADS = (D_MODEL // 2) // HEAD_DIM
KV_GROUPS = 2
HEADS_PER_GROUP = N_HEADS // KV_GROUPS
CMP_STRIDE = 16
CMP_LEN = 2 * CMP_STRIDE
SEL_BLOCK = 64
N_SEL = 16
WINDOW = 512
Q_BLOCK = 128
ROPE_THETA = 500000.0
ROT_DIM = HEAD_DIM // 4
FORCE = 1e4
NSA_Q = N_HEADS * HEAD_DIM
NSA_KV = 2 * KV_GROUPS * HEAD_DIM

SC_DIM = D_MODEL // 2
SC_WIDTH = 3
SSD_HEAD_DIM = 64
SSD_HEADS = 16
SSD_INNER = SSD_HEADS * SSD_HEAD_DIM
SSD_GROUPS = 4
SSD_STATE = 128
SSD_CONV = 4
SSD_CONV_DIM = SSD_INNER + 2 * SSD_GROUPS * SSD_STATE
SSD_CHUNK = 128

D_FF = 2816
N_EXPERTS = 8
TOP_K = 2

VMEM_LIMIT_BYTES = 56 * 1024 * 1024
ROW_TILE = 512
FF_TILE = D_FF // 2


def _cparams(*sem):
    return pltpu.CompilerParams(dimension_semantics=sem, vmem_limit_bytes=VMEM_LIMIT_BYTES)


def _mm_kernel(x_ref, w_ref, o_ref):
    o_ref[...] = jnp.dot(x_ref[...].astype(jnp.bfloat16), w_ref[...],
                         preferred_element_type=jnp.float32)


COL_TILE = 512


def matmul(x, w_bf16):
    m, k = x.shape
    n = w_bf16.shape[1]
    return pl.pallas_call(
        _mm_kernel,
        grid=(pl.cdiv(m, ROW_TILE), pl.cdiv(n, COL_TILE)),
        in_specs=[pl.BlockSpec((ROW_TILE, k), lambda i, j: (i, 0)),
                  pl.BlockSpec((k, COL_TILE), lambda i, j: (0, j))],
        out_specs=pl.BlockSpec((ROW_TILE, COL_TILE), lambda i, j: (i, j)),
        out_shape=jax.ShapeDtypeStruct((m, n), jnp.float32),
        compiler_params=_cparams("parallel", "parallel"),
        name="matmul",
    )(x, w_bf16)


def _ffn_kernel(te_ref, nt_ref, x_ref, wg_ref, wu_ref, wd_ref, o_ref):
    t = pl.program_id(0)
    j = pl.program_id(1)

    @pl.when(t < nt_ref[0])
    def _():
        x = x_ref[...].astype(jnp.bfloat16)
        g = jnp.dot(x, wg_ref[...], preferred_element_type=jnp.float32)
        u = jnp.dot(x, wu_ref[...], preferred_element_type=jnp.float32)
        h = (g * jax.nn.sigmoid(g) * u).astype(jnp.bfloat16)
        part = jnp.dot(h, wd_ref[...], preferred_element_type=jnp.float32)

        @pl.when(j == 0)
        def _():
            o_ref[...] = part

        @pl.when(j > 0)
        def _():
            o_ref[...] += part

    @pl.when(jnp.logical_and(t >= nt_ref[0], j == 0))
    def _():
        o_ref[...] = jnp.zeros_like(o_ref)


def grouped_ffn(x, w_gu_bf16, w_down_bf16, tile_expert, n_tiles_used):
    r, d = x.shape
    nf = D_FF // FF_TILE
    assert nf == 2
    n_tiles = pl.cdiv(r, ROW_TILE)
    grid_spec = pltpu.PrefetchScalarGridSpec(
        num_scalar_prefetch=2,
        grid=(n_tiles, nf),
        in_specs=[
            pl.BlockSpec((ROW_TILE, d), lambda t, j, te, nt: (t, 0)),
            pl.BlockSpec((None, d, FF_TILE), lambda t, j, te, nt: (te[t], 0, j)),
            pl.BlockSpec((None, d, FF_TILE), lambda t, j, te, nt: (te[t], 0, nf + j)),
            pl.BlockSpec((None, FF_TILE, d), lambda t, j, te, nt: (te[t], j, 0)),
        ],
        out_specs=pl.BlockSpec((ROW_TILE, d), lambda t, j, te, nt: (t, 0)),
    )
    return pl.pallas_call(
        _ffn_kernel,
        grid_spec=grid_spec,
        out_shape=jax.ShapeDtypeStruct((r, d), jnp.float32),
        compiler_params=_cparams("parallel", "arbitrary"),
        name="grouped_ffn",
    )(tile_expert, n_tiles_used, x, w_gu_bf16, w_gu_bf16, w_down_bf16)


def layer_norm(x, g, b):
    mu = jnp.mean(x, -1, keepdims=True)
    xc = x - mu
    var = jnp.mean(xc * xc, -1, keepdims=True)
    return xc * lax.rsqrt(var + LN_EPS) * g + b


def rope(x, pos):
    half = ROT_DIM // 2
    inv = ROPE_THETA ** (-jnp.arange(half, dtype=jnp.float32) * 2.0 / ROT_DIM)
    ang = pos.astype(jnp.float32)[:, None] * inv[None, :]
    cos = jnp.cos(ang)[:, None, :]
    sin = jnp.sin(ang)[:, None, :]
    x1 = x[..., :half]
    x2 = x[..., half:ROT_DIM]
    return jnp.concatenate([x1 * cos - x2 * sin, x2 * cos + x1 * sin, x[..., ROT_DIM:]], axis=-1)


def masked_softmax(s, mask):
    s = jnp.where(mask, s.astype(jnp.float32), NEG)
    m = jnp.max(s, -1, keepdims=True)
    p = jnp.where(mask, jnp.exp(s - m), 0.0)
    return p / jnp.maximum(jnp.sum(p, -1, keepdims=True), 1e-30)


def last_rows(x, n):
    t = x.shape[1]
    if t < n:
        x = jnp.pad(x, [(0, 0), (n - t, 0)] + [(0, 0)] * (x.ndim - 2))
    return x[:, x.shape[1] - n:]


def causal_conv(x, buf, w, b):
    t = x.shape[1]
    width = w.shape[0]
    xp = jnp.concatenate([buf, x], axis=1)
    y = b + sum(xp[:, j:j + t] * w[j] for j in range(width))
    return y, xp[:, xp.shape[1] - (width - 1):]


def s5_mixer(u, h0, lam_re, lam_im, log_dt, b, c, d, w_glu):
    f32 = jnp.float32
    bt, t, _ = u.shape
    ug = u.astype(f32).reshape(bt, t, S5_GROUPS, S5_GROUP)
    dt = jnp.exp(log_dt.astype(f32))[:, None]
    mag = jnp.exp(lam_re * dt)
    ang = lam_im * dt
    ab_re = mag * jnp.cos(ang)
    ab_im = mag * jnp.sin(ang)
    den = lam_re * lam_re + lam_im * lam_im
    nr = ab_re - 1.0
    coef_re = (nr * lam_re + ab_im * lam_im) / den
    coef_im = (ab_im * lam_re - nr * lam_im) / den
    b_re = b[..., 0].astype(f32)
    b_im = b[..., 1].astype(f32)
    bb_re = coef_re[..., None] * b_re - coef_im[..., None] * b_im
    bb_im = coef_re[..., None] * b_im + coef_im[..., None] * b_re
    bu_re = jnp.einsum('gnk,btgk->btgn', bb_re, ug)
    bu_im = jnp.einsum('gnk,btgk->btgn', bb_im, ug)
    a_re = jnp.broadcast_to(ab_re, bu_re.shape)
    a_im = jnp.broadcast_to(ab_im, bu_re.shape)

    def combine(e1, e2):
        a1r, a1i, x1r, x1i = e1
        a2r, a2i, x2r, x2i = e2
        return (a2r * a1r - a2i * a1i, a2r * a1i + a2i * a1r,
                a2r * x1r - a2i * x1i + x2r, a2r * x1i + a2i * x1r + x2i)

    cr, ci, xr, xi = lax.associative_scan(combine, (a_re, a_im, bu_re, bu_im), axis=1)
    h0 = h0.astype(f32)
    h_re = h0[..., 0][:, None]
    h_im = h0[..., 1][:, None]
    s_re = xr + cr * h_re - ci * h_im
    s_im = xi + cr * h_im + ci * h_re
    y = (jnp.einsum('gkn,btgn->btgk', c[..., 0].astype(f32), s_re)
         - jnp.einsum('gkn,btgn->btgk', c[..., 1].astype(f32), s_im)
         + d.astype(f32) * ug).reshape(bt, t, S5_DIM)
    z = jax.nn.gelu(y)
    out = z * jax.nn.sigmoid(z @ w_glu)
    new_state = jnp.stack([s_re[:, -1], s_im[:, -1]], axis=-1)
    return out, new_state


def compress(k, w1, w2, pe):
    bt, t, g, dh = k.shape
    n_ch = -(-t // CMP_STRIDE)
    k = jnp.pad(k, ((0, 0), (0, n_ch * CMP_STRIDE - t), (0, 0), (0, 0)))
    ch = k.reshape(bt, n_ch, CMP_STRIDE, g, dh)
    blocks = jnp.concatenate([ch[:, :-1], ch[:, 1:]], axis=2) + pe[None, None, :, None, :]
    flat = blocks.transpose(0, 1, 3, 2, 4).reshape(bt, n_ch - 1, g, CMP_LEN * dh)
    return jax.nn.gelu(flat @ w1) @ w2


def sel_blocks(k):
    bt, t, g, dh = k.shape
    n = -(-t // SEL_BLOCK)
    k = jnp.pad(k, ((0, 0), (0, n * SEL_BLOCK - t), (0, 0), (0, 0)))
    return k.reshape(bt, n, SEL_BLOCK, g, dh)


def nsa_block(q, qpos, kc, vc, kb, vb, kw, vw, wpos, gates):
    bt, qb = q.shape[0], q.shape[1]
    scale = HEAD_DIM ** -0.5
    qg = q.reshape(bt, qb, KV_GROUPS, HEADS_PER_GROUP, HEAD_DIM)
    n_cmp = kc.shape[1]
    cend = jnp.arange(n_cmp) * CMP_STRIDE + (CMP_LEN - 1)
    cmask = cend[None, :] <= qpos[:, None]
    p_c = masked_softmax(jnp.einsum('bqgrd,bngd->bgrqn', qg, kc) * scale, cmask)
    o_c = jnp.einsum('bgrqn,bngd->bqgrd', p_c, vc)
    n_sel = kb.shape[1]
    ratio = SEL_BLOCK // CMP_STRIDE
    imp = jnp.sum(p_c, axis=2)
    imp = jnp.pad(imp, ((0, 0), (0, 0), (0, 0), (0, n_sel * ratio - n_cmp)))
    imp = imp.reshape(bt, KV_GROUPS, qb, n_sel, ratio).sum(-1)
    blk = jnp.arange(n_sel)[None, :]
    jq = (qpos // SEL_BLOCK)[:, None]
    forced = (blk == 0) | (blk == jq) | (blk == jq - 1)
    score = jnp.where(blk <= jq, imp + jnp.where(forced, FORCE, 0.0), NEG)
    k_eff = min(N_SEL, n_sel)
    top_s, idx = lax.top_k(score, k_eff)
    valid = top_s > 0.5 * NEG
    bi = jnp.arange(bt)[:, None, None, None]
    gi = jnp.arange(KV_GROUPS)[None, :, None, None]
    ks = kb.transpose(0, 3, 1, 2, 4)[bi, gi, idx]
    vs = vb.transpose(0, 3, 1, 2, 4)[bi, gi, idx]
    kpos = idx[..., None] * SEL_BLOCK + jnp.arange(SEL_BLOCK)
    smask = (valid[..., None] & (kpos <= qpos[None, None, :, None, None]))
    smask = smask.reshape(bt, KV_GROUPS, 1, qb, k_eff * SEL_BLOCK)
    s_s = jnp.einsum('bqgrd,bgqksd->bgrqks', qg, ks).reshape(bt, KV_GROUPS, HEADS_PER_GROUP, qb, k_eff * SEL_BLOCK)
    p_s = masked_softmax(s_s * scale, smask)
    o_s = jnp.einsum('bgrqm,bgqmd->bqgrd', p_s, vs.reshape(bt, KV_GROUPS, qb, k_eff * SEL_BLOCK, HEAD_DIM))
    wmask = (wpos[None, :] <= qpos[:, None]) & (wpos[None, :] > qpos[:, None] - WINDOW) & (wpos[None, :] >= 0)
    p_w = masked_softmax(jnp.einsum('bqgrd,blgd->bgrql', qg, kw) * scale, wmask)
    o_w = jnp.einsum('bgrql,blgd->bqgrd', p_w, vw)
    g = gates.reshape(bt, qb, KV_GROUPS, HEADS_PER_GROUP, 3)
    o = g[..., 0:1] * o_c + g[..., 1:2] * o_s + g[..., 2:3] * o_w
    return o.reshape(bt, qb, NSA_Q)


def even_split(z, pos):
    bt, t, _ = z.shape
    o = [0]
    for w in (S5_DIM, NSA_Q, NSA_KV, NSA_KV, NSA_KV, 3 * N_HEADS):
        o.append(o[-1] + w)
    u = z[..., o[0]:o[1]]
    q = rope(z[..., o[1]:o[2]].reshape(bt, t, N_HEADS, HEAD_DIM), pos)

    def kv(a, b):
        r = z[..., a:b].reshape(bt, t, 2, KV_GROUPS, HEAD_DIM)
        return jnp.stack([rope(r[:, :, 0], pos), r[:, :, 1]], axis=2)

    kvc = kv(o[2], o[3])
    kvs = kv(o[3], o[4])
    kvw = kv(o[4], o[5])
    gates = jax.nn.sigmoid(z[..., o[5]:o[6]]).reshape(bt, t, N_HEADS, 3)
    return u, q, kvc, kvs, kvw, gates


def even_prompt_mix(z, s5p, cmpp, w_buf):
    bt, t, _ = z.shape
    pos = jnp.arange(t)
    u, q, kvc, kvs, kvw, gates = even_split(z, pos)
    y_s5, s5_state = s5_mixer(u, jnp.zeros((bt, S5_GROUPS, S5_STATE, 2), jnp.float32), *s5p)
    kc = compress(kvc[:, :, 0], cmpp[0], cmpp[1], cmpp[2])
    vc = compress(kvc[:, :, 1], cmpp[3], cmpp[4], cmpp[5])
    kb = sel_blocks(kvs[:, :, 0])
    vb = sel_blocks(kvs[:, :, 1])
    kw_pad = jnp.pad(kvw, ((0, 0), (WINDOW, 0), (0, 0), (0, 0), (0, 0)))

    def q_block(i):
        start = i * Q_BLOCK
        qpos = start + jnp.arange(Q_BLOCK)
        qs = lax.dynamic_slice_in_dim(q, start, Q_BLOCK, axis=1)
        gs = lax.dynamic_slice_in_dim(gates, start, Q_BLOCK, axis=1)
        win = lax.dynamic_slice_in_dim(kw_pad, start, WINDOW + Q_BLOCK, axis=1)
        wpos = start - WINDOW + jnp.arange(WINDOW + Q_BLOCK)
        return nsa_block(qs, qpos, kc, vc, kb, vb, win[:, :, 0], win[:, :, 1], wpos, gs)

    o = lax.map(q_block, jnp.arange(t // Q_BLOCK))
    y_nsa = o.transpose(1, 0, 2, 3).reshape(bt, t, NSA_Q)
    mix = jnp.concatenate([y_s5, y_nsa], axis=-1)
    new_rows = jnp.concatenate([kvc, kvs], axis=2)
    return mix, s5_state, new_rows, last_rows(kvw, w_buf)


def even_sample_mix(z, s5_h0, pool, page_table, win_buf, s5p, cmpp):
    bt, t, _ = z.shape
    pos = PAST_LEN + jnp.arange(t)
    u, q, kvc, kvs, kvw, gates = even_split(z, pos)
    y_s5, s5_state = s5_mixer(u, s5_h0, *s5p)
    past = pool[page_table].astype(jnp.float32)
    past = past.reshape(bt, past.shape[1] * past.shape[2], 4, KV_GROUPS, HEAD_DIM)
    new_rows = jnp.concatenate([kvc, kvs], axis=2)
    full = jnp.concatenate([past, new_rows], axis=1)
    kc = compress(full[:, :, 0], cmpp[0], cmpp[1], cmpp[2])
    vc = compress(full[:, :, 1], cmpp[3], cmpp[4], cmpp[5])
    kb = sel_blocks(full[:, :, 2])
    vb = sel_blocks(full[:, :, 3])
    w_buf = win_buf.shape[1]
    win = jnp.concatenate([win_buf.astype(jnp.float32), kvw], axis=1)
    wpos = PAST_LEN - w_buf + jnp.arange(w_buf + t)
    y_nsa = nsa_block(q, pos, kc, vc, kb, vb, win[:, :, 0], win[:, :, 1], wpos, gates)
    mix = jnp.concatenate([y_s5, y_nsa], axis=-1)
    return mix, s5_state, new_rows, win[:, t:]


def ssd_scan(x, dt, a, bm, cm, h0, chunk):
    bt, t, nh, p = x.shape
    nch = t // chunk
    r = nh // SSD_GROUPS
    tri = jnp.arange(chunk)[:, None] >= jnp.arange(chunk)[None, :]

    def to_chunks(v):
        return jnp.moveaxis(v.reshape((bt, nch, chunk) + v.shape[2:]), 1, 0)

    def step(h, inp):
        xc, dtc, bc, cc = inp
        cum = jnp.cumsum(dtc * a, axis=1)
        seg = cum[:, :, None, :] - cum[:, None, :, :]
        decay = jnp.exp(jnp.where(tri[None, :, :, None], seg, NEG)).reshape(bt, chunk, chunk, SSD_GROUPS, r)
        cb = jnp.einsum('btgn,bsgn->btsg', cc, bc)
        xg = xc.reshape(bt, chunk, SSD_GROUPS, r, p)
        dg = dtc.reshape(bt, chunk, SSD_GROUPS, r)
        w = cb[..., None] * decay * dg[:, None]
        y_intra = jnp.einsum('btsgr,bsgrp->btgrp', w, xg)
        hg = h.reshape(bt, SSD_GROUPS, r, p, SSD_STATE)
        y_inter = jnp.einsum('btgn,bgrpn->btgrp', cc, hg) * jnp.exp(cum).reshape(bt, chunk, SSD_GROUPS, r)[..., None]
        wt = (jnp.exp(cum[:, -1:, :] - cum) * dtc).reshape(bt, chunk, SSD_GROUPS, r)
        h_new = (hg * jnp.exp(cum[:, -1]).reshape(bt, SSD_GROUPS, r)[..., None, None]
                 + jnp.einsum('bsgr,bsgrp,bsgn->bgrpn', wt, xg, bc))
        return h_new.reshape(bt, nh, p, SSD_STATE), (y_intra + y_inter).reshape(bt, chunk, nh, p)

    h_fin, ys = lax.scan(step, h0, (to_chunks(x), to_chunks(dt), to_chunks(bm), to_chunks(cm)))
    return jnp.moveaxis(ys, 0, 1).reshape(bt, t, nh, p), h_fin


def gated_rmsnorm(y, z, g):
    v = y * jax.nn.silu(z)
    bt, t, _ = v.shape
    vg = v.reshape(bt, t, SSD_GROUPS, SSD_INNER // SSD_GROUPS)
    vg = vg * lax.rsqrt(jnp.mean(vg * vg, -1, keepdims=True) + RMS_EPS)
    return vg.reshape(bt, t, SSD_INNER) * g


def odd_mix(z, sc_buf, conv_buf, h0, chunk, sc_w, sc_b, cv_w, cv_b, dt_bias, a_log, d_skip, norm_g):
    f32 = jnp.float32
    bt, t, _ = z.shape
    o1 = SC_DIM
    o2 = 2 * SC_DIM
    o3 = 3 * SC_DIM
    o4 = o3 + SSD_INNER
    o5 = o4 + SSD_CONV_DIM
    sc_h = z[..., :o1]
    sc_bg = z[..., o1:o2]
    sc_cg = z[..., o2:o3]
    zg = z[..., o3:o4]
    xbc = z[..., o4:o5]
    dt_raw = z[..., o5:]
    conv_sc, new_sc = causal_conv(sc_cg * sc_h, sc_buf.astype(f32), sc_w, sc_b)
    y_sc = sc_bg * conv_sc
    xbc_c, new_conv = causal_conv(xbc, conv_buf.astype(f32), cv_w, cv_b)
    xbc_c = jax.nn.silu(xbc_c)
    gn = SSD_GROUPS * SSD_STATE
    xs = xbc_c[..., :SSD_INNER].reshape(bt, t, SSD_HEADS, SSD_HEAD_DIM)
    bm = xbc_c[..., SSD_INNER:SSD_INNER + gn].reshape(bt, t, SSD_GROUPS, SSD_STATE)
    cm = xbc_c[..., SSD_INNER + gn:].reshape(bt, t, SSD_GROUPS, SSD_STATE)
    dt = jax.nn.softplus((dt_raw + dt_bias).astype(f32))
    a = -jnp.exp(a_log.astype(f32))
    y, h_new = ssd_scan(xs, dt, a, bm, cm, h0.astype(f32), chunk)
    y = (y + d_skip[:, None] * xs).reshape(bt, t, SSD_INNER)
    y = gated_rmsnorm(y, zg, norm_g)
    mix = jnp.concatenate([y_sc, y], axis=-1)
    return mix, new_sc, new_conv, h_new


def moe_ffn(x, w_r, b_r, w_gu_bf16, w_down_bf16):
    n, d = x.shape
    logits = jnp.dot(x, w_r, precision=lax.Precision.HIGHEST) + b_r
    top_v, top_i = lax.top_k(logits, TOP_K)
    gate = jax.nn.softmax(top_v, axis=-1)
    flat_e = top_i.reshape(-1)
    order = jnp.argsort(flat_e, stable=True)
    sorted_e = flat_e[order]
    counts = jnp.bincount(flat_e, length=N_EXPERTS)
    padded = ((counts + ROW_TILE - 1) // ROW_TILE) * ROW_TILE
    pad_start = jnp.cumsum(padded) - padded
    start = jnp.cumsum(counts) - counts
    rank = jnp.arange(2 * n) - start[sorted_e]
    dest_sorted = (pad_start[sorted_e] + rank).astype(jnp.int32)
    n_tiles = (2 * n) // ROW_TILE + N_EXPERTS
    rows = n_tiles * ROW_TILE
    row_token = jnp.zeros((rows,), jnp.int32).at[dest_sorted].set((order // TOP_K).astype(jnp.int32))
    dest = jnp.zeros((2 * n,), jnp.int32).at[order].set(dest_sorted)
    tile_end = jnp.cumsum(padded) // ROW_TILE
    tile_expert = jnp.minimum(jnp.searchsorted(tile_end, jnp.arange(n_tiles), side='right'),
                              N_EXPERTS - 1).astype(jnp.int32)
    n_used = tile_end[-1:].astype(jnp.int32)
    xs = jnp.take(x, row_token, axis=0)
    ys = grouped_ffn(xs, w_gu_bf16, w_down_bf16, tile_expert, n_used)
    yk = jnp.take(ys, dest, axis=0).reshape(n, TOP_K, d)
    return jnp.sum(yk * gate[..., None], axis=1)


def kernel(x_prompt, x_sample, state_s5, cache_nsa_kv, state_win_kv, state_sc_conv, state_ssd_conv, state_ssd,
           page_table, ln_g, ln_b, w_in_even, s5_lam_re, s5_lam_im, s5_log_dt, s5_b, s5_c, s5_d, s5_w_glu,
           nsa_wk1, nsa_wk2, nsa_pe_k, nsa_wv1, nsa_wv2, nsa_pe_v, w_out_even, ffn_w_gu, ffn_w_down,
           w_in_odd, sc_conv_w, sc_conv_b, ssd_conv_w, ssd_conv_b, ssd_dt_bias, ssd_a_log, ssd_d, ssd_norm_g,
           w_out_odd, moe_router, moe_router_b, moe_w_gu, moe_w_down):
    f32 = jnp.float32
    bf16 = jnp.bfloat16
    bp, tp, d = x_prompt.shape
    bs, ts, _ = x_sample.shape
    n_p = bp * tp
    n_s = bs * ts
    w_buf = state_win_kv.shape[2]
    h = jnp.concatenate([x_prompt.astype(f32).reshape(n_p, d), x_sample.astype(f32).reshape(n_s, d)], axis=0)
    n = n_p + n_s
    one_tile = jnp.zeros((pl.cdiv(n, ROW_TILE),), jnp.int32)
    all_tiles = jnp.full((1,), pl.cdiv(n, ROW_TILE), jnp.int32)

    s5p = (s5_lam_re[0], s5_lam_im[0], s5_log_dt[0], s5_b[0], s5_c[0], s5_d[0], s5_w_glu[0])
    cmpp = (nsa_wk1[0], nsa_wk2[0], nsa_pe_k[0], nsa_wv1[0], nsa_wv2[0], nsa_pe_v[0])
    z = matmul(h, w_in_even[0].astype(bf16))
    zp = z[:n_p].reshape(bp, tp, -1)
    zs = z[n_p:].reshape(bs, ts, -1)
    mix_p, s5_p, kv_p, win_p = even_prompt_mix(zp, s5p, cmpp, w_buf)
    mix_s, s5_s, kv_s, win_s = even_sample_mix(zs, state_s5[0], cache_nsa_kv[0], page_table, state_win_kv[0],
                                               s5p, cmpp)
    mix = jnp.concatenate([mix_p.reshape(n_p, -1), mix_s.reshape(n_s, -1)], axis=0)
    m = matmul(mix, w_out_even[0].astype(bf16))
    h = layer_norm(ALPHA * h + m, ln_g[0, 0], ln_b[0, 0])
    f = grouped_ffn(h, ffn_w_gu.astype(bf16), ffn_w_down.astype(bf16), one_tile, all_tiles)
    h = layer_norm(ALPHA * h + f, ln_g[0, 1], ln_b[0, 1])

    oddp = (sc_conv_w[0], sc_conv_b[0], ssd_conv_w[0], ssd_conv_b[0], ssd_dt_bias[0],
            ssd_a_log[0], ssd_d[0], ssd_norm_g[0])
    z = matmul(h, w_in_odd[0].astype(bf16))
    zp = z[:n_p].reshape(bp, tp, -1)
    zs = z[n_p:].reshape(bs, ts, -1)
    mix_p, scc_p, sdc_p, ssd_p = odd_mix(zp, jnp.zeros((bp, SC_WIDTH - 1, SC_DIM), f32),
                                         jnp.zeros((bp, SSD_CONV - 1, SSD_CONV_DIM), f32),
                                         jnp.zeros((bp, SSD_HEADS, SSD_HEAD_DIM, SSD_STATE), f32),
                                         SSD_CHUNK, *oddp)
    mix_s, scc_s, sdc_s, ssd_s = odd_mix(zs, state_sc_conv[0], state_ssd_conv[0], state_ssd[0], ts, *oddp)
    mix = jnp.concatenate([mix_p.reshape(n_p, -1), mix_s.reshape(n_s, -1)], axis=0)
    m = matmul(mix, w_out_odd[0].astype(bf16))
    h = layer_norm(ALPHA * h + m, ln_g[1, 0], ln_b[1, 0])
    f = moe_ffn(h, moe_router[0], moe_router_b[0], moe_w_gu[0].astype(bf16), moe_w_down[0].astype(bf16))
    h = layer_norm(ALPHA * h + f, ln_g[1, 1], ln_b[1, 1])

    hp = h[:n_p].reshape(bp, tp, d)
    hs = h[n_p:].reshape(bs, ts, d)
    st = lambda a, ref: a[None].astype(ref.dtype)
    return (hp.astype(x_prompt.dtype), hs.astype(x_sample.dtype),
            st(s5_p, state_s5), st(s5_s, state_s5),
            st(kv_p, cache_nsa_kv), st(kv_s, cache_nsa_kv),
            st(win_p, state_win_kv), st(win_s, state_win_kv),
            st(scc_p, state_sc_conv), st(scc_s, state_sc_conv),
            st(sdc_p, state_ssd_conv), st(sdc_s, state_ssd_conv),
            st(ssd_p, state_ssd), st(ssd_s, state_ssd))
```

```python
import functools
import math

import jax
import jax.numpy as jnp
from jax import lax
from jax.experimental import pallas as pl
from jax.experimental.pallas import tpu as pltpu

D_MODEL = 1024
SEQ = 8192
DEPTH = 2
DEC_SEQ = 8
PAST_LEN = 16384
ALPHA = (2.0 * DEPTH) ** 0.25
LN_EPS = 1e-5
RMS_EPS = 1e-5
NEG = -1e30

S5_DIM = D_MODEL // 2
S5_GROUP = 16
S5_GROUPS = S5_DIM // S5_GROUP
S5_STATE = 64

HEAD_DIM = 64
N_HEADS = (D_MODEL // 2) // HEAD_DIM
KV_GROUPS = 2
HEADS_PER_GROUP = N_HEADS // KV_GROUPS
CMP_STRIDE = 16
CMP_LEN = 2 * CMP_STRIDE
SEL_BLOCK = 64
N_SEL = 16
WINDOW = 512
Q_BLOCK = 128
ROPE_THETA = 500000.0
ROT_DIM = HEAD_DIM // 4
FORCE = 1e4
NSA_Q = N_HEADS * HEAD_DIM
NSA_KV = 2 * KV_GROUPS * HEAD_DIM

SC_DIM = D_MODEL // 2
SC_WIDTH = 3
SSD_HEAD_DIM = 64
SSD_HEADS = 16
SSD_INNER = SSD_HEADS * SSD_HEAD_DIM
SSD_GROUPS = 4
SSD_STATE = 128
SSD_CONV = 4
SSD_CONV_DIM = SSD_INNER + 2 * SSD_GROUPS * SSD_STATE
SSD_CHUNK = 128

D_FF = 2816
N_EXPERTS = 8
TOP_K = 2

VMEM_LIMIT_BYTES = 56 * 1024 * 1024
LANES = 128
S5_N = S5_GROUPS * S5_STATE
S5_LT = S5_N // LANES
S5_CHUNK = 256
SEL_TILE = 512
REMOVED = -3e38
ROW_TILE = 512
FF_TILE = D_FF // 2


def _cparams(*sem):
    return pltpu.CompilerParams(dimension_semantics=sem, vmem_limit_bytes=VMEM_LIMIT_BYTES)


def _mm_kernel(x_ref, w_ref, o_ref):
    o_ref[...] = jnp.dot(x_ref[...].astype(jnp.bfloat16), w_ref[...],
                         preferred_element_type=jnp.float32)


COL_TILE = 512


def matmul(x, w_bf16):
    m, k = x.shape
    n = w_bf16.shape[1]
    return pl.pallas_call(
        _mm_kernel,
        grid=(pl.cdiv(m, ROW_TILE), pl.cdiv(n, COL_TILE)),
        in_specs=[pl.BlockSpec((ROW_TILE, k), lambda i, j: (i, 0)),
                  pl.BlockSpec((k, COL_TILE), lambda i, j: (0, j))],
        out_specs=pl.BlockSpec((ROW_TILE, COL_TILE), lambda i, j: (i, j)),
        out_shape=jax.ShapeDtypeStruct((m, n), jnp.float32),
        compiler_params=_cparams("parallel", "parallel"),
        name="matmul",
    )(x, w_bf16)


def _ffn_kernel(te_ref, nt_ref, x_ref, wg_ref, wu_ref, wd_ref, o_ref):
    t = pl.program_id(0)
    j = pl.program_id(1)

    @pl.when(t < nt_ref[0])
    def _():
        x = x_ref[...].astype(jnp.bfloat16)
        g = jnp.dot(x, wg_ref[...], preferred_element_type=jnp.float32)
        u = jnp.dot(x, wu_ref[...], preferred_element_type=jnp.float32)
        h = (g * jax.nn.sigmoid(g) * u).astype(jnp.bfloat16)
        part = jnp.dot(h, wd_ref[...], preferred_element_type=jnp.float32)

        @pl.when(j == 0)
        def _():
            o_ref[...] = part

        @pl.when(j > 0)
        def _():
            o_ref[...] += part

    @pl.when(jnp.logical_and(t >= nt_ref[0], j == 0))
    def _():
        o_ref[...] = jnp.zeros_like(o_ref)


def grouped_ffn(x, w_gu_bf16, w_down_bf16, tile_expert, n_tiles_used):
    r, d = x.shape
    nf = D_FF // FF_TILE
    assert nf == 2
    n_tiles = pl.cdiv(r, ROW_TILE)
    grid_spec = pltpu.PrefetchScalarGridSpec(
        num_scalar_prefetch=2,
        grid=(n_tiles, nf),
        in_specs=[
            pl.BlockSpec((ROW_TILE, d), lambda t, j, te, nt: (t, 0)),
            pl.BlockSpec((None, d, FF_TILE), lambda t, j, te, nt: (te[t], 0, j)),
            pl.BlockSpec((None, d, FF_TILE), lambda t, j, te, nt: (te[t], 0, nf + j)),
            pl.BlockSpec((None, FF_TILE, d), lambda t, j, te, nt: (te[t], j, 0)),
        ],
        out_specs=pl.BlockSpec((ROW_TILE, d), lambda t, j, te, nt: (t, 0)),
    )
    return pl.pallas_call(
        _ffn_kernel,
        grid_spec=grid_spec,
        out_shape=jax.ShapeDtypeStruct((r, d), jnp.float32),
        compiler_params=_cparams("parallel", "arbitrary"),
        name="grouped_ffn",
    )(tile_expert, n_tiles_used, x, w_gu_bf16, w_gu_bf16, w_down_bf16)


def _s5_kernel(u_ref, h0r_ref, h0i_ref, ar_ref, ai_ref, bbr_ref, bbi_ref, cr_ref, ci_ref, d_ref, wglu_ref,
               y_ref, hro_ref, hio_ref, bur, bui, sr, si, hr, hi, *, chains, chunk):
    j = pl.program_id(0)

    @pl.when(j == 0)
    def _():
        hr[...] = h0r_ref[...]
        hi[...] = h0i_ref[...]

    u = u_ref[...].reshape(chains * chunk, S5_DIM)
    ub = u.astype(jnp.bfloat16)
    bu_r = jnp.dot(ub, bbr_ref[...], preferred_element_type=jnp.float32)
    bu_i = jnp.dot(ub, bbi_ref[...], preferred_element_type=jnp.float32)
    for k in range(S5_LT):
        bur[k] = bu_r[:, k * LANES:(k + 1) * LANES]
        bui[k] = bu_i[:, k * LANES:(k + 1) * LANES]
    ar = [jnp.broadcast_to(ar_ref[:, k * LANES:(k + 1) * LANES], (chains, LANES)) for k in range(S5_LT)]
    ai = [jnp.broadcast_to(ai_ref[:, k * LANES:(k + 1) * LANES], (chains, LANES)) for k in range(S5_LT)]

    def body(t, carry):
        rows = pl.ds(t, chains, stride=chunk)
        out = []
        for k in range(S5_LT):
            xr, xi = carry[2 * k], carry[2 * k + 1]
            nr = ar[k] * xr - ai[k] * xi + bur[k, rows, :]
            ni = ar[k] * xi + ai[k] * xr + bui[k, rows, :]
            sr[k, rows, :] = nr
            si[k, rows, :] = ni
            out += [nr, ni]
        return tuple(out)

    init = []
    for k in range(S5_LT):
        init += [hr[:, k * LANES:(k + 1) * LANES], hi[:, k * LANES:(k + 1) * LANES]]
    fin = lax.fori_loop(0, chunk, body, tuple(init))
    xr = jnp.concatenate(fin[0::2], axis=1)
    xi = jnp.concatenate(fin[1::2], axis=1)
    hr[...] = xr
    hi[...] = xi
    hro_ref[...] = xr
    hio_ref[...] = xi
    s_r = jnp.concatenate([sr[k] for k in range(S5_LT)], axis=1).astype(jnp.bfloat16)
    s_i = jnp.concatenate([si[k] for k in range(S5_LT)], axis=1).astype(jnp.bfloat16)
    y = (jnp.dot(s_r, cr_ref[...], preferred_element_type=jnp.float32)
         - jnp.dot(s_i, ci_ref[...], preferred_element_type=jnp.float32)
         + d_ref[...] * u)
    z = jax.nn.gelu(y)
    gate = jax.nn.sigmoid(jnp.dot(z.astype(jnp.bfloat16), wglu_ref[...], preferred_element_type=jnp.float32))
    y_ref[...] = (z * gate).reshape(chains, chunk, S5_DIM)


def s5_params(lam_re, lam_im, log_dt, b, c, d, w_glu):
    f32 = jnp.float32
    dt = jnp.exp(log_dt.astype(f32))[:, None]
    mag = jnp.exp(lam_re * dt)
    ang = lam_im * dt
    ab_re = mag * jnp.cos(ang)
    ab_im = mag * jnp.sin(ang)
    den = lam_re * lam_re + lam_im * lam_im
    nr = ab_re - 1.0
    coef_re = (nr * lam_re + ab_im * lam_im) / den
    coef_im = (ab_im * lam_re - nr * lam_im) / den
    b_re = b[..., 0].astype(f32)
    b_im = b[..., 1].astype(f32)
    bb_re = coef_re[..., None] * b_re - coef_im[..., None] * b_im
    bb_im = coef_re[..., None] * b_im + coef_im[..., None] * b_re
    eye = jnp.eye(S5_GROUPS, dtype=f32)
    bbr = jnp.einsum('gnk,gh->gkhn', bb_re, eye).reshape(S5_DIM, S5_N).astype(jnp.bfloat16)
    bbi = jnp.einsum('gnk,gh->gkhn', bb_im, eye).reshape(S5_DIM, S5_N).astype(jnp.bfloat16)
    cr = jnp.einsum('gkn,gh->gnhk', c[..., 0].astype(f32), eye).reshape(S5_N, S5_DIM).astype(jnp.bfloat16)
    ci = jnp.einsum('gkn,gh->gnhk', c[..., 1].astype(f32), eye).reshape(S5_N, S5_DIM).astype(jnp.bfloat16)
    return (ab_re.reshape(1, S5_N), ab_im.reshape(1, S5_N), bbr, bbi, cr, ci,
            d.astype(f32).reshape(1, S5_DIM), w_glu.astype(jnp.bfloat16))


def s5_scan(u, h0, params, chunk):
    chains, t, _ = u.shape
    ar, ai, bbr, bbi, cr, ci, d, wglu = params
    h0r = h0[..., 0].reshape(chains, S5_N)
    h0i = h0[..., 1].reshape(chains, S5_N)
    full = lambda shape: pl.BlockSpec(shape, lambda j: (0,) * len(shape))
    rows = chains * chunk
    y, hr, hi = pl.pallas_call(
        functools.partial(_s5_kernel, chains=chains, chunk=chunk),
        grid=(t // chunk,),
        in_specs=[pl.BlockSpec((chains, chunk, S5_DIM), lambda j: (0, j, 0)),
                  full((chains, S5_N)), full((chains, S5_N)), full((1, S5_N)), full((1, S5_N)),
                  full((S5_DIM, S5_N)), full((S5_DIM, S5_N)), full((S5_N, S5_DIM)), full((S5_N, S5_DIM)),
                  full((1, S5_DIM)), full((S5_DIM, S5_DIM))],
        out_specs=[pl.BlockSpec((chains, chunk, S5_DIM), lambda j: (0, j, 0)),
                   full((chains, S5_N)), full((chains, S5_N))],
        out_shape=[jax.ShapeDtypeStruct((chains, t, S5_DIM), jnp.float32),
                   jax.ShapeDtypeStruct((chains, S5_N), jnp.float32),
                   jax.ShapeDtypeStruct((chains, S5_N), jnp.float32)],
        scratch_shapes=[pltpu.VMEM((S5_LT, rows, LANES), jnp.float32)] * 4
                       + [pltpu.VMEM((chains, S5_N), jnp.float32)] * 2,
        compiler_params=_cparams("arbitrary"),
        name="s5_scan",
    )(u, h0r, h0i, ar, ai, bbr, bbi, cr, ci, d, wglu)
    new_state = jnp.stack([hr.reshape(chains, S5_GROUPS, S5_STATE), hi.reshape(chains, S5_GROUPS, S5_STATE)],
                          axis=-1)
    return y, new_state


def _dot_nt(a, b):
    return lax.dot_general(a, b, (((1,), (1,)), ((), ())), preferred_element_type=jnp.float32)


def _softmax_rows(s, mask):
    s = jnp.where(mask, s, NEG)
    m = jnp.max(s, axis=-1, keepdims=True)
    p = jnp.where(mask, jnp.exp(s - m), 0.0)
    return p * (1.0 / jnp.maximum(jnp.sum(p, axis=-1, keepdims=True), 1e-30))


def _nsa_prompt_kernel(q_ref, gate_ref, kc_ref, vc_ref, ks_ref, vs_ref, kw_ref, vw_ref, o_ref, *, n_cmp, n_blk):
    f32, bf16 = jnp.float32, jnp.bfloat16
    r4 = HEADS_PER_GROUP
    n_cpad = kc_ref.shape[0]
    start = pl.program_id(1) * Q_BLOCK
    q = q_ref[...] * (HEAD_DIM ** -0.5)
    gate = gate_ref[...]
    lane = lax.broadcasted_iota(jnp.int32, (Q_BLOCK, LANES), 1)
    qpos = start + lax.broadcasted_iota(jnp.int32, (Q_BLOCK, 1), 0)
    heads = [None] * N_HEADS
    for g in range(KV_GROUPS):
        keep = (lane < HEAD_DIM) if g == 0 else (lane >= HEAD_DIM)
        parts = []
        for r in range(r4):
            h = r4 * g + r
            tile = q[:, (h // 2) * LANES:(h // 2 + 1) * LANES]
            if h % 2 != g:
                tile = pltpu.roll(tile, HEAD_DIM, axis=1)
            parts.append(jnp.where(keep, tile, 0.0))
        qg = jnp.concatenate(parts, axis=0).astype(bf16)

        s = _dot_nt(qg, kc_ref[...]).reshape(r4, Q_BLOCK, n_cpad)
        n_idx = lax.broadcasted_iota(jnp.int32, (Q_BLOCK, n_cpad), 1)
        cmask = ((n_idx * CMP_STRIDE + (CMP_LEN - 1)) <= qpos) & (n_idx < n_cmp)
        p_c = _softmax_rows(s, cmask[None])
        o_c = jnp.dot(p_c.reshape(r4 * Q_BLOCK, n_cpad).astype(bf16), vc_ref[...],
                      preferred_element_type=f32).reshape(r4, Q_BLOCK, LANES)
        psum = p_c[0] + p_c[1] + p_c[2] + p_c[3]
        p_hi = psum.astype(bf16)
        rem = psum - p_hi.astype(f32)
        p_mid = rem.astype(bf16)
        p_lo = (rem - p_mid.astype(f32)).astype(bf16)
        ratio = SEL_BLOCK // CMP_STRIDE
        gsum = (lax.broadcasted_iota(jnp.int32, (n_blk, n_cpad), 1) // ratio
                == lax.broadcasted_iota(jnp.int32, (n_blk, n_cpad), 0)).astype(bf16)
        imp_t = _dot_nt(gsum, p_hi) + _dot_nt(gsum, p_mid) + _dot_nt(gsum, p_lo)

        blk = lax.broadcasted_iota(jnp.int32, (n_blk, Q_BLOCK), 0)
        jq = (start + lax.broadcasted_iota(jnp.int32, (n_blk, Q_BLOCK), 1)) // SEL_BLOCK
        forced = (blk == 0) | (blk == jq) | (blk == jq - 1)
        score = jnp.where(blk <= jq, imp_t + jnp.where(forced, FORCE, 0.0), NEG)
        blk_f = blk.astype(f32)
        sel_t = jnp.zeros((n_blk, Q_BLOCK), f32)
        for _ in range(min(N_SEL, n_blk)):
            m = jnp.max(score, axis=0, keepdims=True)
            idx = jnp.min(jnp.where(score == m, blk_f, float(n_blk)), axis=0, keepdims=True)
            hit = blk_f == idx
            sel_t = jnp.where(hit & (m > 0.5 * NEG), 1.0, sel_t)
            score = jnp.where(hit, REMOVED, score)
        sel = sel_t.T.astype(bf16)

        n_tiles = (start + Q_BLOCK + SEL_TILE - 1) // SEL_TILE

        def tile_body(i, carry):
            m_run, l_run, acc = carry
            off = pl.multiple_of(i * SEL_TILE, SEL_TILE)
            k = ks_ref[pl.ds(off, SEL_TILE), :]
            v = vs_ref[pl.ds(off, SEL_TILE), :]
            s_t = _dot_nt(qg, k).reshape(r4, Q_BLOCK, SEL_TILE)
            key = lax.broadcasted_iota(jnp.int32, (n_blk, SEL_TILE), 1)
            expand = (lax.broadcasted_iota(jnp.int32, (n_blk, SEL_TILE), 0)
                      == i * (SEL_TILE // SEL_BLOCK) + key // SEL_BLOCK).astype(bf16)
            picked = jnp.dot(sel, expand, preferred_element_type=f32)
            kpos = i * SEL_TILE + lax.broadcasted_iota(jnp.int32, (Q_BLOCK, SEL_TILE), 1)
            mk = ((picked > 0.5) & (kpos <= qpos))[None]
            s_t = jnp.where(mk, s_t, NEG)
            m_new = jnp.maximum(m_run, jnp.max(s_t, axis=-1, keepdims=True))
            alpha = jnp.exp(m_run - m_new)
            p = jnp.where(mk, jnp.exp(s_t - m_new), 0.0)
            l_new = alpha * l_run + jnp.sum(p, axis=-1, keepdims=True)
            pv = jnp.dot(p.reshape(r4 * Q_BLOCK, SEL_TILE).astype(bf16), v, preferred_element_type=f32)
            return m_new, l_new, alpha * acc + pv.reshape(r4, Q_BLOCK, LANES)

        init = (jnp.full((r4, Q_BLOCK, 1), NEG, f32), jnp.zeros((r4, Q_BLOCK, 1), f32),
                jnp.zeros((r4, Q_BLOCK, LANES), f32))
        _, l_fin, acc = lax.fori_loop(0, n_tiles, tile_body, init)
        o_s = acc * (1.0 / jnp.maximum(l_fin, 1e-30))

        n_win = WINDOW + Q_BLOCK
        woff = pl.multiple_of(start, Q_BLOCK)
        kwin = kw_ref[pl.ds(woff, n_win), :]
        vwin = vw_ref[pl.ds(woff, n_win), :]
        s_w = _dot_nt(qg, kwin).reshape(r4, Q_BLOCK, n_win)
        wpos = start - WINDOW + lax.broadcasted_iota(jnp.int32, (Q_BLOCK, n_win), 1)
        wmask = (wpos <= qpos) & (wpos > qpos - WINDOW) & (wpos >= 0)
        p_w = _softmax_rows(s_w, wmask[None])
        o_w = jnp.dot(p_w.reshape(r4 * Q_BLOCK, n_win).astype(bf16), vwin,
                      preferred_element_type=f32).reshape(r4, Q_BLOCK, LANES)

        for r in range(r4):
            h = r4 * g + r
            heads[h] = (gate[:, 3 * h:3 * h + 1] * o_c[r] + gate[:, 3 * h + 1:3 * h + 2] * o_s[r]
                        + gate[:, 3 * h + 2:3 * h + 3] * o_w[r])

    tiles = []
    for j in range(N_HEADS // 2):
        even, odd = heads[2 * j], heads[2 * j + 1]
        if j // 2 == 0:
            tiles.append(jnp.where(lane < HEAD_DIM, even, pltpu.roll(odd, HEAD_DIM, axis=1)))
        else:
            tiles.append(jnp.where(lane < HEAD_DIM, pltpu.roll(even, HEAD_DIM, axis=1), odd))
    o_ref[...] = jnp.concatenate(tiles, axis=1)


def nsa_prompt(q, gates, kc, vc, ks, vs, kw_pad, vw_pad):
    b, t, _ = q.shape
    n_cpad = kc.shape[1]
    kern = functools.partial(_nsa_prompt_kernel, n_cmp=t // CMP_STRIDE - 1, n_blk=t // SEL_BLOCK)
    whole = lambda rows: pl.BlockSpec((None, rows, LANES), lambda i, j: (i, 0, 0))
    return pl.pallas_call(
        kern,
        grid=(b, t // Q_BLOCK),
        in_specs=[pl.BlockSpec((None, Q_BLOCK, NSA_Q), lambda i, j: (i, j, 0)),
                  pl.BlockSpec((None, Q_BLOCK, 3 * N_HEADS), lambda i, j: (i, j, 0)),
                  whole(n_cpad), whole(n_cpad), whole(t), whole(t), whole(t + WINDOW), whole(t + WINDOW)],
        out_specs=pl.BlockSpec((None, Q_BLOCK, NSA_Q), lambda i, j: (i, j, 0)),
        out_shape=jax.ShapeDtypeStruct((b, t, NSA_Q), jnp.float32),
        compiler_params=_cparams("parallel", "arbitrary"),
        name="nsa_prompt",
    )(q, gates, kc, vc, ks, vs, kw_pad, vw_pad)


def _compress_kernel(ch_ref, pet_ref, peb_ref, w1t_ref, w1b_ref, w2_ref, o_ref):
    bf16 = jnp.bfloat16
    ch = ch_ref[...]
    n_ch = ch.shape[0]
    a = jnp.dot((ch + pet_ref[...]).astype(bf16), w1t_ref[...], preferred_element_type=jnp.float32)
    b = jnp.dot((ch + peb_ref[...]).astype(bf16), w1b_ref[...], preferred_element_type=jnp.float32)
    pre = a + pltpu.roll(b, n_ch - 1, axis=0)
    o_ref[...] = jnp.dot(jax.nn.gelu(pre).astype(bf16), w2_ref[...],
                         preferred_element_type=jnp.float32).astype(o_ref.dtype)


def compress_params(w1, w2, pe):
    f32 = jnp.float32
    eye = jnp.eye(KV_GROUPS, dtype=f32)
    w1r = w1.astype(f32).reshape(2, CMP_STRIDE, HEAD_DIM, HEAD_DIM)
    big = jnp.einsum('hjde,gk->hjgdke', w1r, eye).reshape(2, CMP_STRIDE * LANES, LANES).astype(jnp.bfloat16)
    w2bd = jnp.einsum('de,gk->gdke', w2.astype(f32), eye).reshape(LANES, LANES).astype(jnp.bfloat16)
    per = pe.astype(f32).reshape(2, CMP_STRIDE, 1, HEAD_DIM)
    pe_rows = jnp.broadcast_to(per, (2, CMP_STRIDE, KV_GROUPS, HEAD_DIM)).reshape(2, 1, CMP_STRIDE * LANES)
    return pe_rows[0], pe_rows[1], big[0], big[1], w2bd


def compress_prompt(x, params):
    b, t, _ = x.shape
    n_ch = t // CMP_STRIDE
    ch = x.reshape(b, n_ch, CMP_STRIDE * LANES)
    pet, peb, w1t, w1b, w2bd = params
    full = lambda shape: pl.BlockSpec(shape, lambda i: (0,) * len(shape))
    return pl.pallas_call(
        _compress_kernel,
        grid=(b,),
        in_specs=[pl.BlockSpec((None, n_ch, CMP_STRIDE * LANES), lambda i: (i, 0, 0)),
                  full((1, CMP_STRIDE * LANES)), full((1, CMP_STRIDE * LANES)),
                  full((CMP_STRIDE * LANES, LANES)), full((CMP_STRIDE * LANES, LANES)), full((LANES, LANES))],
        out_specs=pl.BlockSpec((None, n_ch, LANES), lambda i: (i, 0, 0)),
        out_shape=jax.ShapeDtypeStruct((b, n_ch, LANES), jnp.bfloat16),
        compiler_params=_cparams("parallel"),
        name="compress_prompt",
    )(ch, pet, peb, w1t, w1b, w2bd)


def layer_norm(x, g, b):
    mu = jnp.mean(x, -1, keepdims=True)
    xc = x - mu
    var = jnp.mean(xc * xc, -1, keepdims=True)
    return xc * lax.rsqrt(var + LN_EPS) * g + b


def rope(x, pos):
    half = ROT_DIM // 2
    inv = ROPE_THETA ** (-jnp.arange(half, dtype=jnp.float32) * 2.0 / ROT_DIM)
    ang = pos.astype(jnp.float32)[:, None] * inv[None, :]
    cos = jnp.cos(ang)[:, None, :]
    sin = jnp.sin(ang)[:, None, :]
    x1 = x[..., :half]
    x2 = x[..., half:ROT_DIM]
    return jnp.concatenate([x1 * cos - x2 * sin, x2 * cos + x1 * sin, x[..., ROT_DIM:]], axis=-1)


def masked_softmax(s, mask):
    s = jnp.where(mask, s.astype(jnp.float32), NEG)
    m = jnp.max(s, -1, keepdims=True)
    p = jnp.where(mask, jnp.exp(s - m), 0.0)
    return p / jnp.maximum(jnp.sum(p, -1, keepdims=True), 1e-30)


def last_rows(x, n):
    t = x.shape[1]
    if t < n:
        x = jnp.pad(x, [(0, 0), (n - t, 0)] + [(0, 0)] * (x.ndim - 2))
    return x[:, x.shape[1] - n:]


def causal_conv(x, buf, w, b):
    t = x.shape[1]
    width = w.shape[0]
    xp = jnp.concatenate([buf, x], axis=1)
    y = b + sum(xp[:, j:j + t] * w[j] for j in range(width))
    return y, xp[:, xp.shape[1] - (width - 1):]


def compress(k, w1, w2, pe):
    bt, t, g, dh = k.shape
    n_ch = -(-t // CMP_STRIDE)
    k = jnp.pad(k, ((0, 0), (0, n_ch * CMP_STRIDE - t), (0, 0), (0, 0)))
    ch = k.reshape(bt, n_ch, CMP_STRIDE, g, dh)
    blocks = jnp.concatenate([ch[:, :-1], ch[:, 1:]], axis=2) + pe[None, None, :, None, :]
    flat = blocks.transpose(0, 1, 3, 2, 4).reshape(bt, n_ch - 1, g, CMP_LEN * dh)
    return jax.nn.gelu(flat @ w1) @ w2


def sel_blocks(k):
    bt, t, g, dh = k.shape
    n = -(-t // SEL_BLOCK)
    k = jnp.pad(k, ((0, 0), (0, n * SEL_BLOCK - t), (0, 0), (0, 0)))
    return k.reshape(bt, n, SEL_BLOCK, g, dh)


def nsa_block(q, qpos, kc, vc, kb, vb, kw, vw, wpos, gates):
    bt, qb = q.shape[0], q.shape[1]
    scale = HEAD_DIM ** -0.5
    qg = q.reshape(bt, qb, KV_GROUPS, HEADS_PER_GROUP, HEAD_DIM)
    n_cmp = kc.shape[1]
    cend = jnp.arange(n_cmp) * CMP_STRIDE + (CMP_LEN - 1)
    cmask = cend[None, :] <= qpos[:, None]
    p_c = masked_softmax(jnp.einsum('bqgrd,bngd->bgrqn', qg, kc) * scale, cmask)
    o_c = jnp.einsum('bgrqn,bngd->bqgrd', p_c, vc)
    n_sel = kb.shape[1]
    ratio = SEL_BLOCK // CMP_STRIDE
    imp = jnp.sum(p_c, axis=2)
    imp = jnp.pad(imp, ((0, 0), (0, 0), (0, 0), (0, n_sel * ratio - n_cmp)))
    imp = imp.reshape(bt, KV_GROUPS, qb, n_sel, ratio).sum(-1)
    blk = jnp.arange(n_sel)[None, :]
    jq = (qpos // SEL_BLOCK)[:, None]
    forced = (blk == 0) | (blk == jq) | (blk == jq - 1)
    score = jnp.where(blk <= jq, imp + jnp.where(forced, FORCE, 0.0), NEG)
    k_eff = min(N_SEL, n_sel)
    top_s, idx = lax.top_k(score, k_eff)
    valid = top_s > 0.5 * NEG
    bi = jnp.arange(bt)[:, None, None, None]
    gi = jnp.arange(KV_GROUPS)[None, :, None, None]
    ks = kb.transpose(0, 3, 1, 2, 4)[bi, gi, idx]
    vs = vb.transpose(0, 3, 1, 2, 4)[bi, gi, idx]
    kpos = idx[..., None] * SEL_BLOCK + jnp.arange(SEL_BLOCK)
    smask = (valid[..., None] & (kpos <= qpos[None, None, :, None, None]))
    smask = smask.reshape(bt, KV_GROUPS, 1, qb, k_eff * SEL_BLOCK)
    s_s = jnp.einsum('bqgrd,bgqksd->bgrqks', qg, ks).reshape(bt, KV_GROUPS, HEADS_PER_GROUP, qb, k_eff * SEL_BLOCK)
    p_s = masked_softmax(s_s * scale, smask)
    o_s = jnp.einsum('bgrqm,bgqmd->bqgrd', p_s, vs.reshape(bt, KV_GROUPS, qb, k_eff * SEL_BLOCK, HEAD_DIM))
    wmask = (wpos[None, :] <= qpos[:, None]) & (wpos[None, :] > qpos[:, None] - WINDOW) & (wpos[None, :] >= 0)
    p_w = masked_softmax(jnp.einsum('bqgrd,blgd->bgrql', qg, kw) * scale, wmask)
    o_w = jnp.einsum('bgrql,blgd->bqgrd', p_w, vw)
    g = gates.reshape(bt, qb, KV_GROUPS, HEADS_PER_GROUP, 3)
    o = g[..., 0:1] * o_c + g[..., 1:2] * o_s + g[..., 2:3] * o_w
    return o.reshape(bt, qb, NSA_Q)


def even_split(z, pos):
    bt, t, _ = z.shape
    o = [0]
    for w in (S5_DIM, NSA_Q, NSA_KV, NSA_KV, NSA_KV, 3 * N_HEADS):
        o.append(o[-1] + w)
    u = z[..., o[0]:o[1]]
    q = rope(z[..., o[1]:o[2]].reshape(bt, t, N_HEADS, HEAD_DIM), pos)

    def kv(a, b):
        r = z[..., a:b].reshape(bt, t, 2, KV_GROUPS, HEAD_DIM)
        return jnp.stack([rope(r[:, :, 0], pos), r[:, :, 1]], axis=2)

    kvc = kv(o[2], o[3])
    kvs = kv(o[3], o[4])
    kvw = kv(o[4], o[5])
    gates = jax.nn.sigmoid(z[..., o[5]:o[6]]).reshape(bt, t, N_HEADS, 3)
    return u, q, kvc, kvs, kvw, gates


def even_prompt_mix(z, s5p, cmpp, w_buf):
    bt, t, _ = z.shape
    pos = jnp.arange(t)
    u, q, kvc, kvs, kvw, gates = even_split(z, pos)
    y_s5, s5_state = s5_scan(u, jnp.zeros((bt, S5_GROUPS, S5_STATE, 2), jnp.float32), s5p, S5_CHUNK)
    bf16 = jnp.bfloat16
    lanes = lambda a: a.reshape(bt, t, KV_GROUPS * HEAD_DIM)
    kc = compress_prompt(lanes(kvc[:, :, 0]), compress_params(cmpp[0], cmpp[1], cmpp[2]))
    vc = compress_prompt(lanes(kvc[:, :, 1]), compress_params(cmpp[3], cmpp[4], cmpp[5]))
    front = lambda a: jnp.pad(lanes(a).astype(bf16), ((0, 0), (WINDOW, 0), (0, 0)))
    y_nsa = nsa_prompt(q.reshape(bt, t, NSA_Q), gates.reshape(bt, t, 3 * N_HEADS), kc, vc,
                       lanes(kvs[:, :, 0]).astype(bf16), lanes(kvs[:, :, 1]).astype(bf16),
                       front(kvw[:, :, 0]), front(kvw[:, :, 1]))
    mix = jnp.concatenate([y_s5, y_nsa], axis=-1)
    new_rows = jnp.concatenate([kvc, kvs], axis=2)
    return mix, s5_state, new_rows, last_rows(kvw, w_buf)


def even_sample_mix(z, s5_h0, pool, page_table, win_buf, s5p, cmpp):
    bt, t, _ = z.shape
    pos = PAST_LEN + jnp.arange(t)
    u, q, kvc, kvs, kvw, gates = even_split(z, pos)
    y_s5, s5_state = s5_scan(u, s5_h0.astype(jnp.float32), s5p, t)
    past = pool[page_table].astype(jnp.float32)
    past = past.reshape(bt, past.shape[1] * past.shape[2], 4, KV_GROUPS, HEAD_DIM)
    new_rows = jnp.concatenate([kvc, kvs], axis=2)
    full = jnp.concatenate([past, new_rows], axis=1)
    kc = compress(full[:, :, 0], cmpp[0], cmpp[1], cmpp[2])
    vc = compress(full[:, :, 1], cmpp[3], cmpp[4], cmpp[5])
    kb = sel_blocks(full[:, :, 2])
    vb = sel_blocks(full[:, :, 3])
    w_buf = win_buf.shape[1]
    win = jnp.concatenate([win_buf.astype(jnp.float32), kvw], axis=1)
    wpos = PAST_LEN - w_buf + jnp.arange(w_buf + t)
    y_nsa = nsa_block(q, pos, kc, vc, kb, vb, win[:, :, 0], win[:, :, 1], wpos, gates)
    mix = jnp.concatenate([y_s5, y_nsa], axis=-1)
    return mix, s5_state, new_rows, win[:, t:]


def ssd_scan(x, dt, a, bm, cm, h0, chunk):
    bt, t, nh, p = x.shape
    nch = t // chunk
    r = nh // SSD_GROUPS
    tri = jnp.arange(chunk)[:, None] >= jnp.arange(chunk)[None, :]

    def to_chunks(v):
        return jnp.moveaxis(v.reshape((bt, nch, chunk) + v.shape[2:]), 1, 0)

    def step(h, inp):
        xc, dtc, bc, cc = inp
        cum = jnp.cumsum(dtc * a, axis=1)
        seg = cum[:, :, None, :] - cum[:, None, :, :]
        decay = jnp.exp(jnp.where(tri[None, :, :, None], seg, NEG)).reshape(bt, chunk, chunk, SSD_GROUPS, r)
        cb = jnp.einsum('btgn,bsgn->btsg', cc, bc)
        xg = xc.reshape(bt, chunk, SSD_GROUPS, r, p)
        dg = dtc.reshape(bt, chunk, SSD_GROUPS, r)
        w = cb[..., None] * decay * dg[:, None]
        y_intra = jnp.einsum('btsgr,bsgrp->btgrp', w, xg)
        hg = h.reshape(bt, SSD_GROUPS, r, p, SSD_STATE)
        y_inter = jnp.einsum('btgn,bgrpn->btgrp', cc, hg) * jnp.exp(cum).reshape(bt, chunk, SSD_GROUPS, r)[..., None]
        wt = (jnp.exp(cum[:, -1:, :] - cum) * dtc).reshape(bt, chunk, SSD_GROUPS, r)
        h_new = (hg * jnp.exp(cum[:, -1]).reshape(bt, SSD_GROUPS, r)[..., None, None]
                 + jnp.einsum('bsgr,bsgrp,bsgn->bgrpn', wt, xg, bc))
        return h_new.reshape(bt, nh, p, SSD_STATE), (y_intra + y_inter).reshape(bt, chunk, nh, p)

    h_fin, ys = lax.scan(step, h0, (to_chunks(x), to_chunks(dt), to_chunks(bm), to_chunks(cm)))
    return jnp.moveaxis(ys, 0, 1).reshape(bt, t, nh, p), h_fin


def gated_rmsnorm(y, z, g):
    v = y * jax.nn.silu(z)
    bt, t, _ = v.shape
    vg = v.reshape(bt, t, SSD_GROUPS, SSD_INNER // SSD_GROUPS)
    vg = vg * lax.rsqrt(jnp.mean(vg * vg, -1, keepdims=True) + RMS_EPS)
    return vg.reshape(bt, t, SSD_INNER) * g


def odd_mix(z, sc_buf, conv_buf, h0, chunk, sc_w, sc_b, cv_w, cv_b, dt_bias, a_log, d_skip, norm_g):
    f32 = jnp.float32
    bt, t, _ = z.shape
    o1 = SC_DIM
    o2 = 2 * SC_DIM
    o3 = 3 * SC_DIM
    o4 = o3 + SSD_INNER
    o5 = o4 + SSD_CONV_DIM
    sc_h = z[..., :o1]
    sc_bg = z[..., o1:o2]
    sc_cg = z[..., o2:o3]
    zg = z[..., o3:o4]
    xbc = z[..., o4:o5]
    dt_raw = z[..., o5:]
    conv_sc, new_sc = causal_conv(sc_cg * sc_h, sc_buf.astype(f32), sc_w, sc_b)
    y_sc = sc_bg * conv_sc
    xbc_c, new_conv = causal_conv(xbc, conv_buf.astype(f32), cv_w, cv_b)
    xbc_c = jax.nn.silu(xbc_c)
    gn = SSD_GROUPS * SSD_STATE
    xs = xbc_c[..., :SSD_INNER].reshape(bt, t, SSD_HEADS, SSD_HEAD_DIM)
    bm = xbc_c[..., SSD_INNER:SSD_INNER + gn].reshape(bt, t, SSD_GROUPS, SSD_STATE)
    cm = xbc_c[..., SSD_INNER + gn:].reshape(bt, t, SSD_GROUPS, SSD_STATE)
    dt = jax.nn.softplus((dt_raw + dt_bias).astype(f32))
    a = -jnp.exp(a_log.astype(f32))
    y, h_new = ssd_scan(xs, dt, a, bm, cm, h0.astype(f32), chunk)
    y = (y + d_skip[:, None] * xs).reshape(bt, t, SSD_INNER)
    y = gated_rmsnorm(y, zg, norm_g)
    mix = jnp.concatenate([y_sc, y], axis=-1)
    return mix, new_sc, new_conv, h_new


def moe_ffn(x, w_r, b_r, w_gu_bf16, w_down_bf16):
    n, d = x.shape
    logits = jnp.dot(x, w_r, precision=lax.Precision.HIGHEST) + b_r
    top_v, top_i = lax.top_k(logits, TOP_K)
    gate = jax.nn.softmax(top_v, axis=-1)
    flat_e = top_i.reshape(-1)
    order = jnp.argsort(flat_e, stable=True)
    sorted_e = flat_e[order]
    counts = jnp.bincount(flat_e, length=N_EXPERTS)
    padded = ((counts + ROW_TILE - 1) // ROW_TILE) * ROW_TILE
    pad_start = jnp.cumsum(padded) - padded
    start = jnp.cumsum(counts) - counts
    rank = jnp.arange(2 * n) - start[sorted_e]
    dest_sorted = (pad_start[sorted_e] + rank).astype(jnp.int32)
    n_tiles = (2 * n) // ROW_TILE + N_EXPERTS
    rows = n_tiles * ROW_TILE
    row_token = jnp.zeros((rows,), jnp.int32).at[dest_sorted].set((order // TOP_K).astype(jnp.int32))
    dest = jnp.zeros((2 * n,), jnp.int32).at[order].set(dest_sorted)
    tile_end = jnp.cumsum(padded) // ROW_TILE
    tile_expert = jnp.minimum(jnp.searchsorted(tile_end, jnp.arange(n_tiles), side='right'),
                              N_EXPERTS - 1).astype(jnp.int32)
    n_used = tile_end[-1:].astype(jnp.int32)
    xs = jnp.take(x, row_token, axis=0)
    ys = grouped_ffn(xs, w_gu_bf16, w_down_bf16, tile_expert, n_used)
    yk = jnp.take(ys, dest, axis=0).reshape(n, TOP_K, d)
    return jnp.sum(yk * gate[..., None], axis=1)


def kernel(x_prompt, x_sample, state_s5, cache_nsa_kv, state_win_kv, state_sc_conv, state_ssd_conv, state_ssd,
           page_table, ln_g, ln_b, w_in_even, s5_lam_re, s5_lam_im, s5_log_dt, s5_b, s5_c, s5_d, s5_w_glu,
           nsa_wk1, nsa_wk2, nsa_pe_k, nsa_wv1, nsa_wv2, nsa_pe_v, w_out_even, ffn_w_gu, ffn_w_down,
           w_in_odd, sc_conv_w, sc_conv_b, ssd_conv_w, ssd_conv_b, ssd_dt_bias, ssd_a_log, ssd_d, ssd_norm_g,
           w_out_odd, moe_router, moe_router_b, moe_w_gu, moe_w_down):
    f32 = jnp.float32
    bf16 = jnp.bfloat16
    bp, tp, d = x_prompt.shape
    bs, ts, _ = x_sample.shape
    n_p = bp * tp
    n_s = bs * ts
    w_buf = state_win_kv.shape[2]
    h = jnp.concatenate([x_prompt.astype(f32).reshape(n_p, d), x_sample.astype(f32).reshape(n_s, d)], axis=0)
    n = n_p + n_s
    one_tile = jnp.zeros((pl.cdiv(n, ROW_TILE),), jnp.int32)
    all_tiles = jnp.full((1,), pl.cdiv(n, ROW_TILE), jnp.int32)

    s5p = s5_params(s5_lam_re[0], s5_lam_im[0], s5_log_dt[0], s5_b[0], s5_c[0], s5_d[0], s5_w_glu[0])
    cmpp = (nsa_wk1[0], nsa_wk2[0], nsa_pe_k[0], nsa_wv1[0], nsa_wv2[0], nsa_pe_v[0])
    z = matmul(h, w_in_even[0].astype(bf16))
    zp = z[:n_p].reshape(bp, tp, -1)
    zs = z[n_p:].reshape(bs, ts, -1)
    mix_p, s5_p, kv_p, win_p = even_prompt_mix(zp, s5p, cmpp, w_buf)
    mix_s, s5_s, kv_s, win_s = even_sample_mix(zs, state_s5[0], cache_nsa_kv[0], page_table, state_win_kv[0],
                                               s5p, cmpp)
    mix = jnp.concatenate([mix_p.reshape(n_p, -1), mix_s.reshape(n_s, -1)], axis=0)
    m = matmul(mix, w_out_even[0].astype(bf16))
    h = layer_norm(ALPHA * h + m, ln_g[0, 0], ln_b[0, 0])
    f = grouped_ffn(h, ffn_w_gu.astype(bf16), ffn_w_down.astype(bf16), one_tile, all_tiles)
    h = layer_norm(ALPHA * h + f, ln_g[0, 1], ln_b[0, 1])

    oddp = (sc_conv_w[0], sc_conv_b[0], ssd_conv_w[0], ssd_conv_b[0], ssd_dt_bias[0],
            ssd_a_log[0], ssd_d[0], ssd_norm_g[0])
    z = matmul(h, w_in_odd[0].astype(bf16))
    zp = z[:n_p].reshape(bp, tp, -1)
    zs = z[n_p:].reshape(bs, ts, -1)
    mix_p, scc_p, sdc_p, ssd_p = odd_mix(zp, jnp.zeros((bp, SC_WIDTH - 1, SC_DIM), f32),
                                         jnp.zeros((bp, SSD_CONV - 1, SSD_CONV_DIM), f32),
                                         jnp.zeros((bp, SSD_HEADS, SSD_HEAD_DIM, SSD_STATE), f32),
                                         SSD_CHUNK, *oddp)
    mix_s, scc_s, sdc_s, ssd_s = odd_mix(zs, state_sc_conv[0], state_ssd_conv[0], state_ssd[0], ts, *oddp)
    mix = jnp.concatenate([mix_p.reshape(n_p, -1), mix_s.reshape(n_s, -1)], axis=0)
    m = matmul(mix, w_out_odd[0].astype(bf16))
    h = layer_norm(ALPHA * h + m, ln_g[1, 0], ln_b[1, 0])
    f = moe_ffn(h, moe_router[0], moe_router_b[0], moe_w_gu[0].astype(bf16), moe_w_down[0].astype(bf16))
    h = layer_norm(ALPHA * h + f, ln_g[1, 1], ln_b[1, 1])

    hp = h[:n_p].reshape(bp, tp, d)
    hs = h[n_p:].reshape(bs, ts, d)
    st = lambda a, ref: a[None].astype(ref.dtype)
    return (hp.astype(x_prompt.dtype), hs.astype(x_sample.dtype),
            st(s5_p, state_s5), st(s5_s, state_s5),
            st(kv_p, cache_nsa_kv), st(kv_s, cache_nsa_kv),
            st(win_p, state_win_kv), st(win_s, state_win_kv),
            st(scc_p, state_sc_conv), st(scc_s, state_sc_conv),
            st(sdc_p, state_ssd_conv), st(sdc_s, state_ssd_conv),
            st(ssd_p, state_ssd), st(ssd_s, state_ssd))
```

```python
import functools
import math

import jax
import jax.numpy as jnp
from jax import lax
from jax.experimental import pallas as pl
from jax.experimental.pallas import tpu as pltpu

D_MODEL = 1024
SEQ = 8192
DEPTH = 2
DEC_SEQ = 8
PAST_LEN = 16384
ALPHA = (2.0 * DEPTH) ** 0.25
LN_EPS = 1e-5
RMS_EPS = 1e-5
NEG = -1e30

S5_DIM = D_MODEL // 2
S5_GROUP = 16
S5_GROUPS = S5_DIM // S5_GROUP
S5_STATE = 64

HEAD_DIM = 64
N_HEADS = (D_MODEL // 2) // HEAD_DIM
KV_GROUPS = 2
HEADS_PER_GROUP = N_HEADS // KV_GROUPS
CMP_STRIDE = 16
CMP_LEN = 2 * CMP_STRIDE
SEL_BLOCK = 64
N_SEL = 16
WINDOW = 512
Q_BLOCK = 128
ROPE_THETA = 500000.0
ROT_DIM = HEAD_DIM // 4
FORCE = 1e4
NSA_Q = N_HEADS * HEAD_DIM
NSA_KV = 2 * KV_GROUPS * HEAD_DIM

SC_DIM = D_MODEL // 2
SC_WIDTH = 3
SSD_HEAD_DIM = 64
SSD_HEADS = 16
SSD_INNER = SSD_HEADS * SSD_HEAD_DIM
SSD_GROUPS = 4
SSD_STATE = 128
SSD_CONV = 4
SSD_CONV_DIM = SSD_INNER + 2 * SSD_GROUPS * SSD_STATE
SSD_CHUNK = 128

D_FF = 2816
N_EXPERTS = 8
TOP_K = 2

VMEM_LIMIT_BYTES = 56 * 1024 * 1024
LANES = 128
S5_N = S5_GROUPS * S5_STATE
S5_LT = S5_N // LANES
S5_CHUNK = 256
SEL_TILE = 512
REMOVED = -3e38
PAGE_SIZE = 128
PAGES_PER_STEP = 8
NEW_PAD = 128
ROW_TILE = 512
FF_TILE = D_FF // 2


def _cparams(*sem):
    return pltpu.CompilerParams(dimension_semantics=sem, vmem_limit_bytes=VMEM_LIMIT_BYTES)


def _mm_kernel(x_ref, w_ref, o_ref):
    o_ref[...] = jnp.dot(x_ref[...].astype(jnp.bfloat16), w_ref[...],
                         preferred_element_type=jnp.float32)


COL_TILE = 512


def matmul(x, w_bf16):
    m, k = x.shape
    n = w_bf16.shape[1]
    return pl.pallas_call(
        _mm_kernel,
        grid=(pl.cdiv(m, ROW_TILE), pl.cdiv(n, COL_TILE)),
        in_specs=[pl.BlockSpec((ROW_TILE, k), lambda i, j: (i, 0)),
                  pl.BlockSpec((k, COL_TILE), lambda i, j: (0, j))],
        out_specs=pl.BlockSpec((ROW_TILE, COL_TILE), lambda i, j: (i, j)),
        out_shape=jax.ShapeDtypeStruct((m, n), jnp.float32),
        compiler_params=_cparams("parallel", "parallel"),
        name="matmul",
    )(x, w_bf16)


def _ffn_kernel(te_ref, nt_ref, x_ref, wg_ref, wu_ref, wd_ref, o_ref):
    t = pl.program_id(0)
    j = pl.program_id(1)

    @pl.when(t < nt_ref[0])
    def _():
        x = x_ref[...].astype(jnp.bfloat16)
        g = jnp.dot(x, wg_ref[...], preferred_element_type=jnp.float32)
        u = jnp.dot(x, wu_ref[...], preferred_element_type=jnp.float32)
        h = (g * jax.nn.sigmoid(g) * u).astype(jnp.bfloat16)
        part = jnp.dot(h, wd_ref[...], preferred_element_type=jnp.float32)

        @pl.when(j == 0)
        def _():
            o_ref[...] = part

        @pl.when(j > 0)
        def _():
            o_ref[...] += part

    @pl.when(jnp.logical_and(t >= nt_ref[0], j == 0))
    def _():
        o_ref[...] = jnp.zeros_like(o_ref)


def grouped_ffn(x, w_gu_bf16, w_down_bf16, tile_expert, n_tiles_used):
    r, d = x.shape
    nf = D_FF // FF_TILE
    assert nf == 2
    n_tiles = pl.cdiv(r, ROW_TILE)
    grid_spec = pltpu.PrefetchScalarGridSpec(
        num_scalar_prefetch=2,
        grid=(n_tiles, nf),
        in_specs=[
            pl.BlockSpec((ROW_TILE, d), lambda t, j, te, nt: (t, 0)),
            pl.BlockSpec((None, d, FF_TILE), lambda t, j, te, nt: (te[t], 0, j)),
            pl.BlockSpec((None, d, FF_TILE), lambda t, j, te, nt: (te[t], 0, nf + j)),
            pl.BlockSpec((None, FF_TILE, d), lambda t, j, te, nt: (te[t], j, 0)),
        ],
        out_specs=pl.BlockSpec((ROW_TILE, d), lambda t, j, te, nt: (t, 0)),
    )
    return pl.pallas_call(
        _ffn_kernel,
        grid_spec=grid_spec,
        out_shape=jax.ShapeDtypeStruct((r, d), jnp.float32),
        compiler_params=_cparams("parallel", "arbitrary"),
        name="grouped_ffn",
    )(tile_expert, n_tiles_used, x, w_gu_bf16, w_gu_bf16, w_down_bf16)


def _s5_kernel(u_ref, h0r_ref, h0i_ref, ar_ref, ai_ref, bbr_ref, bbi_ref, cr_ref, ci_ref, d_ref, wglu_ref,
               y_ref, hro_ref, hio_ref, bur, bui, sr, si, hr, hi, *, chains, chunk):
    j = pl.program_id(0)

    @pl.when(j == 0)
    def _():
        hr[...] = h0r_ref[...]
        hi[...] = h0i_ref[...]

    u = u_ref[...].reshape(chains * chunk, S5_DIM)
    ub = u.astype(jnp.bfloat16)
    bu_r = jnp.dot(ub, bbr_ref[...], preferred_element_type=jnp.float32)
    bu_i = jnp.dot(ub, bbi_ref[...], preferred_element_type=jnp.float32)
    for k in range(S5_LT):
        bur[k] = bu_r[:, k * LANES:(k + 1) * LANES]
        bui[k] = bu_i[:, k * LANES:(k + 1) * LANES]
    ar = [jnp.broadcast_to(ar_ref[:, k * LANES:(k + 1) * LANES], (chains, LANES)) for k in range(S5_LT)]
    ai = [jnp.broadcast_to(ai_ref[:, k * LANES:(k + 1) * LANES], (chains, LANES)) for k in range(S5_LT)]

    def body(t, carry):
        rows = pl.ds(t, chains, stride=chunk)
        out = []
        for k in range(S5_LT):
            xr, xi = carry[2 * k], carry[2 * k + 1]
            nr = ar[k] * xr - ai[k] * xi + bur[k, rows, :]
            ni = ar[k] * xi + ai[k] * xr + bui[k, rows, :]
            sr[k, rows, :] = nr
            si[k, rows, :] = ni
            out += [nr, ni]
        return tuple(out)

    init = []
    for k in range(S5_LT):
        init += [hr[:, k * LANES:(k + 1) * LANES], hi[:, k * LANES:(k + 1) * LANES]]
    fin = lax.fori_loop(0, chunk, body, tuple(init))
    xr = jnp.concatenate(fin[0::2], axis=1)
    xi = jnp.concatenate(fin[1::2], axis=1)
    hr[...] = xr
    hi[...] = xi
    hro_ref[...] = xr
    hio_ref[...] = xi
    s_r = jnp.concatenate([sr[k] for k in range(S5_LT)], axis=1).astype(jnp.bfloat16)
    s_i = jnp.concatenate([si[k] for k in range(S5_LT)], axis=1).astype(jnp.bfloat16)
    y = (jnp.dot(s_r, cr_ref[...], preferred_element_type=jnp.float32)
         - jnp.dot(s_i, ci_ref[...], preferred_element_type=jnp.float32)
         + d_ref[...] * u)
    z = jax.nn.gelu(y)
    gate = jax.nn.sigmoid(jnp.dot(z.astype(jnp.bfloat16), wglu_ref[...], preferred_element_type=jnp.float32))
    y_ref[...] = (z * gate).reshape(chains, chunk, S5_DIM)


def s5_params(lam_re, lam_im, log_dt, b, c, d, w_glu):
    f32 = jnp.float32
    dt = jnp.exp(log_dt.astype(f32))[:, None]
    mag = jnp.exp(lam_re * dt)
    ang = lam_im * dt
    ab_re = mag * jnp.cos(ang)
    ab_im = mag * jnp.sin(ang)
    den = lam_re * lam_re + lam_im * lam_im
    nr = ab_re - 1.0
    coef_re = (nr * lam_re + ab_im * lam_im) / den
    coef_im = (ab_im * lam_re - nr * lam_im) / den
    b_re = b[..., 0].astype(f32)
    b_im = b[..., 1].astype(f32)
    bb_re = coef_re[..., None] * b_re - coef_im[..., None] * b_im
    bb_im = coef_re[..., None] * b_im + coef_im[..., None] * b_re
    eye = jnp.eye(S5_GROUPS, dtype=f32)
    bbr = jnp.einsum('gnk,gh->gkhn', bb_re, eye).reshape(S5_DIM, S5_N).astype(jnp.bfloat16)
    bbi = jnp.einsum('gnk,gh->gkhn', bb_im, eye).reshape(S5_DIM, S5_N).astype(jnp.bfloat16)
    cr = jnp.einsum('gkn,gh->gnhk', c[..., 0].astype(f32), eye).reshape(S5_N, S5_DIM).astype(jnp.bfloat16)
    ci = jnp.einsum('gkn,gh->gnhk', c[..., 1].astype(f32), eye).reshape(S5_N, S5_DIM).astype(jnp.bfloat16)
    return (ab_re.reshape(1, S5_N), ab_im.reshape(1, S5_N), bbr, bbi, cr, ci,
            d.astype(f32).reshape(1, S5_DIM), w_glu.astype(jnp.bfloat16))


def s5_scan(u, h0, params, chunk):
    chains, t, _ = u.shape
    ar, ai, bbr, bbi, cr, ci, d, wglu = params
    h0r = h0[..., 0].reshape(chains, S5_N)
    h0i = h0[..., 1].reshape(chains, S5_N)
    full = lambda shape: pl.BlockSpec(shape, lambda j: (0,) * len(shape))
    rows = chains * chunk
    y, hr, hi = pl.pallas_call(
        functools.partial(_s5_kernel, chains=chains, chunk=chunk),
        grid=(t // chunk,),
        in_specs=[pl.BlockSpec((chains, chunk, S5_DIM), lambda j: (0, j, 0)),
                  full((chains, S5_N)), full((chains, S5_N)), full((1, S5_N)), full((1, S5_N)),
                  full((S5_DIM, S5_N)), full((S5_DIM, S5_N)), full((S5_N, S5_DIM)), full((S5_N, S5_DIM)),
                  full((1, S5_DIM)), full((S5_DIM, S5_DIM))],
        out_specs=[pl.BlockSpec((chains, chunk, S5_DIM), lambda j: (0, j, 0)),
                   full((chains, S5_N)), full((chains, S5_N))],
        out_shape=[jax.ShapeDtypeStruct((chains, t, S5_DIM), jnp.float32),
                   jax.ShapeDtypeStruct((chains, S5_N), jnp.float32),
                   jax.ShapeDtypeStruct((chains, S5_N), jnp.float32)],
        scratch_shapes=[pltpu.VMEM((S5_LT, rows, LANES), jnp.float32)] * 4
                       + [pltpu.VMEM((chains, S5_N), jnp.float32)] * 2,
        compiler_params=_cparams("arbitrary"),
        name="s5_scan",
    )(u, h0r, h0i, ar, ai, bbr, bbi, cr, ci, d, wglu)
    new_state = jnp.stack([hr.reshape(chains, S5_GROUPS, S5_STATE), hi.reshape(chains, S5_GROUPS, S5_STATE)],
                          axis=-1)
    return y, new_state


def _dot_nt(a, b):
    return lax.dot_general(a, b, (((1,), (1,)), ((), ())), preferred_element_type=jnp.float32)


def _softmax_rows(s, mask):
    s = jnp.where(mask, s, NEG)
    m = jnp.max(s, axis=-1, keepdims=True)
    p = jnp.where(mask, jnp.exp(s - m), 0.0)
    return p * (1.0 / jnp.maximum(jnp.sum(p, axis=-1, keepdims=True), 1e-30))


def _nsa_prompt_kernel(q_ref, gate_ref, kc_ref, vc_ref, ks_ref, vs_ref, kw_ref, vw_ref, o_ref, *, n_cmp, n_blk):
    f32, bf16 = jnp.float32, jnp.bfloat16
    r4 = HEADS_PER_GROUP
    n_cpad = kc_ref.shape[0]
    start = pl.program_id(1) * Q_BLOCK
    q = q_ref[...] * (HEAD_DIM ** -0.5)
    gate = gate_ref[...]
    lane = lax.broadcasted_iota(jnp.int32, (Q_BLOCK, LANES), 1)
    qpos = start + lax.broadcasted_iota(jnp.int32, (Q_BLOCK, 1), 0)
    heads = [None] * N_HEADS
    for g in range(KV_GROUPS):
        keep = (lane < HEAD_DIM) if g == 0 else (lane >= HEAD_DIM)
        parts = []
        for r in range(r4):
            h = r4 * g + r
            tile = q[:, (h // 2) * LANES:(h // 2 + 1) * LANES]
            if h % 2 != g:
                tile = pltpu.roll(tile, HEAD_DIM, axis=1)
            parts.append(jnp.where(keep, tile, 0.0))
        qg = jnp.concatenate(parts, axis=0).astype(bf16)

        s = _dot_nt(qg, kc_ref[...]).reshape(r4, Q_BLOCK, n_cpad)
        n_idx = lax.broadcasted_iota(jnp.int32, (Q_BLOCK, n_cpad), 1)
        cmask = ((n_idx * CMP_STRIDE + (CMP_LEN - 1)) <= qpos) & (n_idx < n_cmp)
        p_c = _softmax_rows(s, cmask[None])
        o_c = jnp.dot(p_c.reshape(r4 * Q_BLOCK, n_cpad).astype(bf16), vc_ref[...],
                      preferred_element_type=f32).reshape(r4, Q_BLOCK, LANES)
        psum = p_c[0] + p_c[1] + p_c[2] + p_c[3]
        p_hi = psum.astype(bf16)
        rem = psum - p_hi.astype(f32)
        p_mid = rem.astype(bf16)
        p_lo = (rem - p_mid.astype(f32)).astype(bf16)
        ratio = SEL_BLOCK // CMP_STRIDE
        gsum = (lax.broadcasted_iota(jnp.int32, (n_blk, n_cpad), 1) // ratio
                == lax.broadcasted_iota(jnp.int32, (n_blk, n_cpad), 0)).astype(bf16)
        imp_t = _dot_nt(gsum, p_hi) + _dot_nt(gsum, p_mid) + _dot_nt(gsum, p_lo)

        blk = lax.broadcasted_iota(jnp.int32, (n_blk, Q_BLOCK), 0)
        jq = (start + lax.broadcasted_iota(jnp.int32, (n_blk, Q_BLOCK), 1)) // SEL_BLOCK
        forced = (blk == 0) | (blk == jq) | (blk == jq - 1)
        score = jnp.where(blk <= jq, imp_t + jnp.where(forced, FORCE, 0.0), NEG)
        blk_f = blk.astype(f32)
        sel_t = jnp.zeros((n_blk, Q_BLOCK), f32)
        for _ in range(min(N_SEL, n_blk)):
            m = jnp.max(score, axis=0, keepdims=True)
            idx = jnp.min(jnp.where(score == m, blk_f, float(n_blk)), axis=0, keepdims=True)
            hit = blk_f == idx
            sel_t = jnp.where(hit & (m > 0.5 * NEG), 1.0, sel_t)
            score = jnp.where(hit, REMOVED, score)
        sel = sel_t.T.astype(bf16)

        n_tiles = (start + Q_BLOCK + SEL_TILE - 1) // SEL_TILE

        def tile_body(i, carry):
            m_run, l_run, acc = carry
            off = pl.multiple_of(i * SEL_TILE, SEL_TILE)
            k = ks_ref[pl.ds(off, SEL_TILE), :]
            v = vs_ref[pl.ds(off, SEL_TILE), :]
            s_t = _dot_nt(qg, k).reshape(r4, Q_BLOCK, SEL_TILE)
            key = lax.broadcasted_iota(jnp.int32, (n_blk, SEL_TILE), 1)
            expand = (lax.broadcasted_iota(jnp.int32, (n_blk, SEL_TILE), 0)
                      == i * (SEL_TILE // SEL_BLOCK) + key // SEL_BLOCK).astype(bf16)
            picked = jnp.dot(sel, expand, preferred_element_type=f32)
            kpos = i * SEL_TILE + lax.broadcasted_iota(jnp.int32, (Q_BLOCK, SEL_TILE), 1)
            mk = ((picked > 0.5) & (kpos <= qpos))[None]
            s_t = jnp.where(mk, s_t, NEG)
            m_new = jnp.maximum(m_run, jnp.max(s_t, axis=-1, keepdims=True))
            alpha = jnp.exp(m_run - m_new)
            p = jnp.where(mk, jnp.exp(s_t - m_new), 0.0)
            l_new = alpha * l_run + jnp.sum(p, axis=-1, keepdims=True)
            pv = jnp.dot(p.reshape(r4 * Q_BLOCK, SEL_TILE).astype(bf16), v, preferred_element_type=f32)
            return m_new, l_new, alpha * acc + pv.reshape(r4, Q_BLOCK, LANES)

        init = (jnp.full((r4, Q_BLOCK, 1), NEG, f32), jnp.zeros((r4, Q_BLOCK, 1), f32),
                jnp.zeros((r4, Q_BLOCK, LANES), f32))
        _, l_fin, acc = lax.fori_loop(0, n_tiles, tile_body, init)
        o_s = acc * (1.0 / jnp.maximum(l_fin, 1e-30))

        n_win = WINDOW + Q_BLOCK
        woff = pl.multiple_of(start, Q_BLOCK)
        kwin = kw_ref[pl.ds(woff, n_win), :]
        vwin = vw_ref[pl.ds(woff, n_win), :]
        s_w = _dot_nt(qg, kwin).reshape(r4, Q_BLOCK, n_win)
        wpos = start - WINDOW + lax.broadcasted_iota(jnp.int32, (Q_BLOCK, n_win), 1)
        wmask = (wpos <= qpos) & (wpos > qpos - WINDOW) & (wpos >= 0)
        p_w = _softmax_rows(s_w, wmask[None])
        o_w = jnp.dot(p_w.reshape(r4 * Q_BLOCK, n_win).astype(bf16), vwin,
                      preferred_element_type=f32).reshape(r4, Q_BLOCK, LANES)

        for r in range(r4):
            h = r4 * g + r
            heads[h] = (gate[:, 3 * h:3 * h + 1] * o_c[r] + gate[:, 3 * h + 1:3 * h + 2] * o_s[r]
                        + gate[:, 3 * h + 2:3 * h + 3] * o_w[r])

    tiles = []
    for j in range(N_HEADS // 2):
        even, odd = heads[2 * j], heads[2 * j + 1]
        if j // 2 == 0:
            tiles.append(jnp.where(lane < HEAD_DIM, even, pltpu.roll(odd, HEAD_DIM, axis=1)))
        else:
            tiles.append(jnp.where(lane < HEAD_DIM, pltpu.roll(even, HEAD_DIM, axis=1), odd))
    o_ref[...] = jnp.concatenate(tiles, axis=1)


def nsa_prompt(q, gates, kc, vc, ks, vs, kw_pad, vw_pad):
    b, t, _ = q.shape
    n_cpad = kc.shape[1]
    kern = functools.partial(_nsa_prompt_kernel, n_cmp=t // CMP_STRIDE - 1, n_blk=t // SEL_BLOCK)
    whole = lambda rows: pl.BlockSpec((None, rows, LANES), lambda i, j: (i, 0, 0))
    return pl.pallas_call(
        kern,
        grid=(b, t // Q_BLOCK),
        in_specs=[pl.BlockSpec((None, Q_BLOCK, NSA_Q), lambda i, j: (i, j, 0)),
                  pl.BlockSpec((None, Q_BLOCK, 3 * N_HEADS), lambda i, j: (i, j, 0)),
                  whole(n_cpad), whole(n_cpad), whole(t), whole(t), whole(t + WINDOW), whole(t + WINDOW)],
        out_specs=pl.BlockSpec((None, Q_BLOCK, NSA_Q), lambda i, j: (i, j, 0)),
        out_shape=jax.ShapeDtypeStruct((b, t, NSA_Q), jnp.float32),
        compiler_params=_cparams("parallel", "arbitrary"),
        name="nsa_prompt",
    )(q, gates, kc, vc, ks, vs, kw_pad, vw_pad)


def _compress_kernel(ch_ref, pet_ref, peb_ref, w1t_ref, w1b_ref, w2_ref, o_ref):
    bf16 = jnp.bfloat16
    ch = ch_ref[...]
    n_ch = ch.shape[0]
    a = jnp.dot((ch + pet_ref[...]).astype(bf16), w1t_ref[...], preferred_element_type=jnp.float32)
    b = jnp.dot((ch + peb_ref[...]).astype(bf16), w1b_ref[...], preferred_element_type=jnp.float32)
    pre = a + pltpu.roll(b, n_ch - 1, axis=0)
    o_ref[...] = jnp.dot(jax.nn.gelu(pre).astype(bf16), w2_ref[...],
                         preferred_element_type=jnp.float32).astype(o_ref.dtype)


def compress_params(w1, w2, pe):
    f32 = jnp.float32
    eye = jnp.eye(KV_GROUPS, dtype=f32)
    w1r = w1.astype(f32).reshape(2, CMP_STRIDE, HEAD_DIM, HEAD_DIM)
    big = jnp.einsum('hjde,gk->hjgdke', w1r, eye).reshape(2, CMP_STRIDE * LANES, LANES).astype(jnp.bfloat16)
    w2bd = jnp.einsum('de,gk->gdke', w2.astype(f32), eye).reshape(LANES, LANES).astype(jnp.bfloat16)
    per = pe.astype(f32).reshape(2, CMP_STRIDE, 1, HEAD_DIM)
    pe_rows = jnp.broadcast_to(per, (2, CMP_STRIDE, KV_GROUPS, HEAD_DIM)).reshape(2, 1, CMP_STRIDE * LANES)
    return pe_rows[0], pe_rows[1], big[0], big[1], w2bd


def compress_prompt(x, params):
    b, t, _ = x.shape
    n_ch = t // CMP_STRIDE
    ch = x.reshape(b, n_ch, CMP_STRIDE * LANES)
    pet, peb, w1t, w1b, w2bd = params
    full = lambda shape: pl.BlockSpec(shape, lambda i: (0,) * len(shape))
    return pl.pallas_call(
        _compress_kernel,
        grid=(b,),
        in_specs=[pl.BlockSpec((None, n_ch, CMP_STRIDE * LANES), lambda i: (i, 0, 0)),
                  full((1, CMP_STRIDE * LANES)), full((1, CMP_STRIDE * LANES)),
                  full((CMP_STRIDE * LANES, LANES)), full((CMP_STRIDE * LANES, LANES)), full((LANES, LANES))],
        out_specs=pl.BlockSpec((None, n_ch, LANES), lambda i: (i, 0, 0)),
        out_shape=jax.ShapeDtypeStruct((b, n_ch, LANES), jnp.bfloat16),
        compiler_params=_cparams("parallel"),
        name="compress_prompt",
    )(ch, pet, peb, w1t, w1b, w2bd)


def _cmp_sample_kernel(pt_ref, *refs, n_pages):
    f32, bf16 = jnp.float32, jnp.bfloat16
    pp = PAGES_PER_STEP
    kpages, vpages = refs[0:pp], refs[pp:2 * pp]
    (newk_ref, newv_ref, wk_ref, wv_ref, ck_ref, cv_ref, w2k_ref, w2v_ref,
     kc_ref, vc_ref, slab_k, slab_v) = refs[2 * pp:]
    s = pl.program_id(1)
    base = pl.multiple_of(s * (pp * PAGE_SIZE), pp * PAGE_SIZE)
    for i in range(pp):
        slab_k[pl.ds(base + i * PAGE_SIZE, PAGE_SIZE), :] = kpages[i][...]
        slab_v[pl.ds(base + i * PAGE_SIZE, PAGE_SIZE), :] = vpages[i][...]

    @pl.when(s == pl.num_programs(1) - 1)
    def _():
        n_ch = n_pages * (PAGE_SIZE // CMP_STRIDE)
        row = lax.broadcasted_iota(jnp.int32, (n_ch, LANES), 0)
        for slab, new_ref, w_ref, c_ref, w2_ref, o_ref in ((slab_k, newk_ref, wk_ref, ck_ref, w2k_ref, kc_ref),
                                                           (slab_v, newv_ref, wv_ref, cv_ref, w2v_ref, vc_ref)):
            ch = jnp.concatenate([slab[pl.ds(j, n_ch, stride=CMP_STRIDE), :] for j in range(CMP_STRIDE)],
                                 axis=1).astype(bf16)
            ab = jnp.dot(ch, w_ref[...], preferred_element_type=f32)
            b_new = jnp.dot(new_ref[...].astype(bf16), w_ref[...], preferred_element_type=f32)[0:1, LANES:]
            nxt = pltpu.roll(ab[:, LANES:], n_ch - 1, axis=0)
            nxt = jnp.where(row == n_ch - 1, b_new, nxt)
            pre = ab[:, :LANES] + nxt + c_ref[...]
            o_ref[...] = jnp.dot(jax.nn.gelu(pre).astype(bf16), w2_ref[...],
                                 preferred_element_type=f32).astype(o_ref.dtype)


def compress_sample_params(w1, w2, pe):
    pet, peb, w1t, w1b, w2bd = compress_params(w1, w2, pe)
    hp = lax.Precision.HIGHEST
    const = (jnp.dot(pet, w1t.astype(jnp.float32), precision=hp)
             + jnp.dot(peb, w1b.astype(jnp.float32), precision=hp))
    return jnp.concatenate([w1t, w1b], axis=1), const, w2bd


def _page_spec(i, lane_block):
    return pl.BlockSpec((None, PAGE_SIZE, LANES),
                        lambda b, s, pt: (pt[b, PAGES_PER_STEP * s + i], 0, lane_block))


def _per_seq(shape):
    return pl.BlockSpec((None,) + shape, lambda b, s, pt: (b, 0, 0))


def compress_sample(pool3, page_table, new_k, new_v, pk, pv):
    bsz, n_pages = page_table.shape
    pp = PAGES_PER_STEP
    n_ch = n_pages * (PAGE_SIZE // CMP_STRIDE)
    t_new = new_k.shape[1]

    def chunk_rows(x):
        x = jnp.pad(x, ((0, 0), (0, CMP_STRIDE - t_new), (0, 0))).reshape(bsz, 1, CMP_STRIDE * LANES)
        return jnp.pad(x, ((0, 0), (0, 7), (0, 0)))

    full = lambda shape: pl.BlockSpec(shape, lambda b, s, pt: (0,) * len(shape))
    wk, ck, w2k = pk
    wv, cv, w2v = pv
    grid_spec = pltpu.PrefetchScalarGridSpec(
        num_scalar_prefetch=1,
        grid=(bsz, n_pages // pp),
        in_specs=[_page_spec(i, 0) for i in range(pp)] + [_page_spec(i, 1) for i in range(pp)]
                 + [_per_seq((8, CMP_STRIDE * LANES)), _per_seq((8, CMP_STRIDE * LANES)),
                    full((CMP_STRIDE * LANES, 2 * LANES)), full((CMP_STRIDE * LANES, 2 * LANES)),
                    full((1, LANES)), full((1, LANES)), full((LANES, LANES)), full((LANES, LANES))],
        out_specs=[_per_seq((n_ch, LANES)), _per_seq((n_ch, LANES))],
        scratch_shapes=[pltpu.VMEM((n_pages * PAGE_SIZE, LANES), jnp.float32)] * 2,
    )
    return pl.pallas_call(
        functools.partial(_cmp_sample_kernel, n_pages=n_pages),
        grid_spec=grid_spec,
        out_shape=[jax.ShapeDtypeStruct((bsz, n_ch, LANES), jnp.bfloat16)] * 2,
        compiler_params=_cparams("parallel", "arbitrary"),
        name="compress_sample",
    )(page_table, *([pool3] * (2 * pp)), chunk_rows(new_k), chunk_rows(new_v), wk, wv, ck, cv, w2k, w2v)


def _nsa_sample_kernel(pt_ref, *refs, n_pages, t_new, w_buf):
    f32, bf16 = jnp.float32, jnp.bfloat16
    pp = PAGES_PER_STEP
    q_ref, gate_ref, kc_ref, vc_ref = refs[0:4]
    kpages, vpages = refs[4:4 + pp], refs[4 + pp:4 + 2 * pp]
    (ksn_ref, vsn_ref, win_ref, kwn_ref, vwn_ref, o_ref,
     sel_scr, oc_scr, m_scr, l_scr, acc_scr) = refs[4 + 2 * pp:]
    r4, g2 = HEADS_PER_GROUP, KV_GROUPS
    n_rows = g2 * r4 * t_new
    past_len = n_pages * PAGE_SIZE
    n_cmp = kc_ref.shape[0]
    n_bpad = sel_scr.shape[1]
    tile = pp * PAGE_SIZE
    s = pl.program_id(1)
    qall = q_ref[...]
    qpos = past_len + lax.broadcasted_iota(jnp.int32, (n_rows, 1), 0) % t_new

    def grouped(x):
        return x.reshape(g2, 1, t_new, x.shape[-1])

    @pl.when(s == 0)
    def _():
        s_c = _dot_nt(qall, kc_ref[...])
        n_idx = lax.broadcasted_iota(jnp.int32, (n_rows, n_cmp), 1)
        p_c = _softmax_rows(s_c, (n_idx * CMP_STRIDE + (CMP_LEN - 1)) <= qpos)
        oc_scr[...] = jnp.dot(p_c.astype(bf16), vc_ref[...], preferred_element_type=f32)
        psum = jnp.sum(p_c.reshape(g2, r4, t_new, n_cmp), axis=1).reshape(g2 * t_new, n_cmp)
        psum = jnp.concatenate([psum, jnp.zeros((LANES - g2 * t_new, n_cmp), f32)], axis=0)
        p_hi = psum.astype(bf16)
        rem = psum - p_hi.astype(f32)
        p_mid = rem.astype(bf16)
        p_lo = (rem - p_mid.astype(f32)).astype(bf16)
        ratio = SEL_BLOCK // CMP_STRIDE
        gsum = (lax.broadcasted_iota(jnp.int32, (n_bpad, n_cmp), 1) // ratio
                == lax.broadcasted_iota(jnp.int32, (n_bpad, n_cmp), 0)).astype(bf16)
        imp_t = _dot_nt(gsum, p_hi) + _dot_nt(gsum, p_mid) + _dot_nt(gsum, p_lo)
        blk = lax.broadcasted_iota(jnp.int32, (n_bpad, LANES), 0)
        jq = (past_len + lax.broadcasted_iota(jnp.int32, (n_bpad, LANES), 1) % t_new) // SEL_BLOCK
        forced = (blk == 0) | (blk == jq) | (blk == jq - 1)
        score = jnp.where(blk <= jq, imp_t + jnp.where(forced, FORCE, 0.0), NEG)
        blk_f = blk.astype(f32)
        sel_t = jnp.zeros((n_bpad, LANES), f32)
        for _ in range(N_SEL):
            m = jnp.max(score, axis=0, keepdims=True)
            idx = jnp.min(jnp.where(score == m, blk_f, float(n_bpad)), axis=0, keepdims=True)
            hit = blk_f == idx
            sel_t = jnp.where(hit & (m > 0.5 * NEG), 1.0, sel_t)
            score = jnp.where(hit, REMOVED, score)
        sel = jnp.concatenate([sel_t[k * LANES:(k + 1) * LANES].T for k in range(n_bpad // LANES)], axis=1)
        sel_scr[...] = sel[0:g2 * t_new].astype(bf16)
        m_scr[...] = jnp.full(m_scr.shape, NEG, f32)
        l_scr[...] = jnp.zeros(l_scr.shape, f32)
        acc_scr[...] = jnp.zeros(acc_scr.shape, f32)

    def online_update(s_t, mk, v):
        n = s_t.shape[-1]
        s4 = jnp.where(mk, s_t.reshape(g2, r4, t_new, n), NEG)
        m_run = m_scr[...].reshape(g2, r4, t_new, 1)
        m_new = jnp.maximum(m_run, jnp.max(s4, axis=-1, keepdims=True))
        alpha = jnp.exp(m_run - m_new)
        p = jnp.where(mk, jnp.exp(s4 - m_new), 0.0)
        l_new = alpha * l_scr[...].reshape(g2, r4, t_new, 1) + jnp.sum(p, axis=-1, keepdims=True)
        pv = jnp.dot(p.reshape(n_rows, n).astype(bf16), v, preferred_element_type=f32)
        m_scr[...] = m_new.reshape(n_rows, 1)
        l_scr[...] = l_new.reshape(n_rows, 1)
        acc_scr[...] = alpha.reshape(n_rows, 1) * acc_scr[...] + pv

    kt = jnp.concatenate([r[...] for r in kpages], axis=0).astype(bf16)
    vt = jnp.concatenate([r[...] for r in vpages], axis=0).astype(bf16)
    key = lax.broadcasted_iota(jnp.int32, (n_bpad, tile), 1)
    expand = (lax.broadcasted_iota(jnp.int32, (n_bpad, tile), 0)
              == s * (tile // SEL_BLOCK) + key // SEL_BLOCK).astype(bf16)
    picked = jnp.dot(sel_scr[...], expand, preferred_element_type=f32)
    online_update(_dot_nt(qall, kt), grouped(picked) > 0.5, vt)

    @pl.when(s == pl.num_programs(1) - 1)
    def _():
        new_blk = past_len // SEL_BLOCK
        kidx = lax.broadcasted_iota(jnp.int32, (n_rows, NEW_PAD), 1)
        causal = ((past_len + kidx) <= qpos) & (kidx < t_new)
        picked_new = sel_scr[:, new_blk:new_blk + 1].astype(f32)
        mk = (grouped(picked_new) > 0.5) & causal.reshape(g2, r4, t_new, NEW_PAD)
        online_update(_dot_nt(qall, ksn_ref[...]), mk, vsn_ref[...])
        o_s = acc_scr[...] * (1.0 / jnp.maximum(l_scr[...], 1e-30))

        n_win = w_buf + NEW_PAD
        kw = jnp.concatenate([win_ref[:, 0:LANES].astype(bf16), kwn_ref[...]], axis=0)
        vw = jnp.concatenate([win_ref[:, LANES:2 * LANES].astype(bf16), vwn_ref[...]], axis=0)
        widx = lax.broadcasted_iota(jnp.int32, (n_rows, n_win), 1)
        wpos = past_len - w_buf + widx
        wmask = (wpos <= qpos) & (wpos > qpos - WINDOW) & (wpos >= 0) & (widx < w_buf + t_new)
        p_w = _softmax_rows(_dot_nt(qall, kw), wmask)
        o_w = jnp.dot(p_w.astype(bf16), vw, preferred_element_type=f32)
        gate = gate_ref[...]
        o_ref[...] = gate[:, 0:1] * oc_scr[...] + gate[:, 1:2] * o_s + gate[:, 2:3] * o_w


def nsa_sample(q, gates, kc, vc, pool3, page_table, ks_new, vs_new, win, kw_new, vw_new):
    f32, bf16 = jnp.float32, jnp.bfloat16
    bsz, t_new = q.shape[0], q.shape[1]
    n_pages = page_table.shape[1]
    pp = PAGES_PER_STEP
    w_buf = win.shape[1]
    r4, g2 = HEADS_PER_GROUP, KV_GROUPS
    n_rows = g2 * r4 * t_new
    past_len = n_pages * PAGE_SIZE
    assert past_len % SEL_BLOCK == 0 and t_new <= SEL_BLOCK and past_len >= w_buf and n_pages % pp == 0
    n_sel = past_len // SEL_BLOCK + 1
    n_bpad = -(-n_sel // LANES) * LANES
    eye = jnp.eye(g2, dtype=f32)
    qg = q.reshape(bsz, t_new, g2, r4, HEAD_DIM).transpose(0, 2, 3, 1, 4) * (HEAD_DIM ** -0.5)
    qall = jnp.einsum('bgrqd,gk->bgrqkd', qg, eye).reshape(bsz, n_rows, LANES).astype(bf16)
    gall = gates.reshape(bsz, t_new, g2, r4, 3).transpose(0, 2, 3, 1, 4).reshape(bsz, n_rows, 3)
    pad_rows = lambda x: jnp.pad(x, ((0, 0), (0, NEW_PAD - t_new), (0, 0))).astype(bf16)
    n_cmp = kc.shape[1]
    grid_spec = pltpu.PrefetchScalarGridSpec(
        num_scalar_prefetch=1,
        grid=(bsz, n_pages // pp),
        in_specs=[_per_seq((n_rows, LANES)), _per_seq((n_rows, 3)), _per_seq((n_cmp, LANES)),
                  _per_seq((n_cmp, LANES))]
                 + [_page_spec(i, 2) for i in range(pp)] + [_page_spec(i, 3) for i in range(pp)]
                 + [_per_seq((NEW_PAD, LANES)), _per_seq((NEW_PAD, LANES)), _per_seq((w_buf, 2 * LANES)),
                    _per_seq((NEW_PAD, LANES)), _per_seq((NEW_PAD, LANES))],
        out_specs=_per_seq((n_rows, LANES)),
        scratch_shapes=[pltpu.VMEM((g2 * t_new, n_bpad), bf16), pltpu.VMEM((n_rows, LANES), f32),
                        pltpu.VMEM((n_rows, 1), f32), pltpu.VMEM((n_rows, 1), f32),
                        pltpu.VMEM((n_rows, LANES), f32)],
    )
    o = pl.pallas_call(
        functools.partial(_nsa_sample_kernel, n_pages=n_pages, t_new=t_new, w_buf=w_buf),
        grid_spec=grid_spec,
        out_shape=jax.ShapeDtypeStruct((bsz, n_rows, LANES), f32),
        compiler_params=_cparams("parallel", "arbitrary"),
        name="nsa_sample",
    )(page_table, qall, gall, kc, vc, *([pool3] * (2 * pp)), pad_rows(ks_new), pad_rows(vs_new), win,
      pad_rows(kw_new), pad_rows(vw_new))
    o = jnp.einsum('bgrqkd,gk->bqgrd', o.reshape(bsz, g2, r4, t_new, g2, HEAD_DIM), eye)
    return o.reshape(bsz, t_new, NSA_Q)


def layer_norm(x, g, b):
    mu = jnp.mean(x, -1, keepdims=True)
    xc = x - mu
    var = jnp.mean(xc * xc, -1, keepdims=True)
    return xc * lax.rsqrt(var + LN_EPS) * g + b


def rope(x, pos):
    half = ROT_DIM // 2
    inv = ROPE_THETA ** (-jnp.arange(half, dtype=jnp.float32) * 2.0 / ROT_DIM)
    ang = pos.astype(jnp.float32)[:, None] * inv[None, :]
    cos = jnp.cos(ang)[:, None, :]
    sin = jnp.sin(ang)[:, None, :]
    x1 = x[..., :half]
    x2 = x[..., half:ROT_DIM]
    return jnp.concatenate([x1 * cos - x2 * sin, x2 * cos + x1 * sin, x[..., ROT_DIM:]], axis=-1)


def last_rows(x, n):
    t = x.shape[1]
    if t < n:
        x = jnp.pad(x, [(0, 0), (n - t, 0)] + [(0, 0)] * (x.ndim - 2))
    return x[:, x.shape[1] - n:]


def causal_conv(x, buf, w, b):
    t = x.shape[1]
    width = w.shape[0]
    xp = jnp.concatenate([buf, x], axis=1)
    y = b + sum(xp[:, j:j + t] * w[j] for j in range(width))
    return y, xp[:, xp.shape[1] - (width - 1):]


def even_split(z, pos):
    bt, t, _ = z.shape
    o = [0]
    for w in (S5_DIM, NSA_Q, NSA_KV, NSA_KV, NSA_KV, 3 * N_HEADS):
        o.append(o[-1] + w)
    u = z[..., o[0]:o[1]]
    q = rope(z[..., o[1]:o[2]].reshape(bt, t, N_HEADS, HEAD_DIM), pos)

    def kv(a, b):
        r = z[..., a:b].reshape(bt, t, 2, KV_GROUPS, HEAD_DIM)
        return jnp.stack([rope(r[:, :, 0], pos), r[:, :, 1]], axis=2)

    kvc = kv(o[2], o[3])
    kvs = kv(o[3], o[4])
    kvw = kv(o[4], o[5])
    gates = jax.nn.sigmoid(z[..., o[5]:o[6]]).reshape(bt, t, N_HEADS, 3)
    return u, q, kvc, kvs, kvw, gates


def even_prompt_mix(z, s5p, cmpp, w_buf):
    bt, t, _ = z.shape
    pos = jnp.arange(t)
    u, q, kvc, kvs, kvw, gates = even_split(z, pos)
    y_s5, s5_state = s5_scan(u, jnp.zeros((bt, S5_GROUPS, S5_STATE, 2), jnp.float32), s5p, S5_CHUNK)
    bf16 = jnp.bfloat16
    lanes = lambda a: a.reshape(bt, t, KV_GROUPS * HEAD_DIM)
    kc = compress_prompt(lanes(kvc[:, :, 0]), compress_params(cmpp[0], cmpp[1], cmpp[2]))
    vc = compress_prompt(lanes(kvc[:, :, 1]), compress_params(cmpp[3], cmpp[4], cmpp[5]))
    front = lambda a: jnp.pad(lanes(a).astype(bf16), ((0, 0), (WINDOW, 0), (0, 0)))
    y_nsa = nsa_prompt(q.reshape(bt, t, NSA_Q), gates.reshape(bt, t, 3 * N_HEADS), kc, vc,
                       lanes(kvs[:, :, 0]).astype(bf16), lanes(kvs[:, :, 1]).astype(bf16),
                       front(kvw[:, :, 0]), front(kvw[:, :, 1]))
    mix = jnp.concatenate([y_s5, y_nsa], axis=-1)
    new_rows = jnp.concatenate([kvc, kvs], axis=2)
    return mix, s5_state, new_rows, last_rows(kvw, w_buf)


def even_sample_mix(z, s5_h0, pool, page_table, win_buf, s5p, cmpp):
    bt, t, _ = z.shape
    f32 = jnp.float32
    pos = page_table.shape[1] * PAGE_SIZE + jnp.arange(t)
    u, q, kvc, kvs, kvw, gates = even_split(z, pos)
    y_s5, s5_state = s5_scan(u, s5_h0.astype(f32), s5p, t)
    lanes = lambda a: a.reshape(bt, t, KV_GROUPS * HEAD_DIM)
    pool3 = pool.astype(f32).reshape(pool.shape[0], PAGE_SIZE, 4 * KV_GROUPS * HEAD_DIM)
    kc, vc = compress_sample(pool3, page_table, lanes(kvc[:, :, 0]), lanes(kvc[:, :, 1]),
                             compress_sample_params(cmpp[0], cmpp[1], cmpp[2]),
                             compress_sample_params(cmpp[3], cmpp[4], cmpp[5]))
    w_buf = win_buf.shape[1]
    win_f = win_buf.astype(f32)
    y_nsa = nsa_sample(q, gates, kc, vc, pool3, page_table, lanes(kvs[:, :, 0]), lanes(kvs[:, :, 1]),
                       win_f.reshape(bt, w_buf, 2 * KV_GROUPS * HEAD_DIM), lanes(kvw[:, :, 0]), lanes(kvw[:, :, 1]))
    new_rows = jnp.concatenate([kvc, kvs], axis=2)
    win = jnp.concatenate([win_f, kvw], axis=1)
    mix = jnp.concatenate([y_s5, y_nsa], axis=-1)
    return mix, s5_state, new_rows, win[:, t:]


def ssd_scan(x, dt, a, bm, cm, h0, chunk):
    bt, t, nh, p = x.shape
    nch = t // chunk
    r = nh // SSD_GROUPS
    tri = jnp.arange(chunk)[:, None] >= jnp.arange(chunk)[None, :]

    def to_chunks(v):
        return jnp.moveaxis(v.reshape((bt, nch, chunk) + v.shape[2:]), 1, 0)

    def step(h, inp):
        xc, dtc, bc, cc = inp
        cum = jnp.cumsum(dtc * a, axis=1)
        seg = cum[:, :, None, :] - cum[:, None, :, :]
        decay = jnp.exp(jnp.where(tri[None, :, :, None], seg, NEG)).reshape(bt, chunk, chunk, SSD_GROUPS, r)
        cb = jnp.einsum('btgn,bsgn->btsg', cc, bc)
        xg = xc.reshape(bt, chunk, SSD_GROUPS, r, p)
        dg = dtc.reshape(bt, chunk, SSD_GROUPS, r)
        w = cb[..., None] * decay * dg[:, None]
        y_intra = jnp.einsum('btsgr,bsgrp->btgrp', w, xg)
        hg = h.reshape(bt, SSD_GROUPS, r, p, SSD_STATE)
        y_inter = jnp.einsum('btgn,bgrpn->btgrp', cc, hg) * jnp.exp(cum).reshape(bt, chunk, SSD_GROUPS, r)[..., None]
        wt = (jnp.exp(cum[:, -1:, :] - cum) * dtc).reshape(bt, chunk, SSD_GROUPS, r)
        h_new = (hg * jnp.exp(cum[:, -1]).reshape(bt, SSD_GROUPS, r)[..., None, None]
                 + jnp.einsum('bsgr,bsgrp,bsgn->bgrpn', wt, xg, bc))
        return h_new.reshape(bt, nh, p, SSD_STATE), (y_intra + y_inter).reshape(bt, chunk, nh, p)

    h_fin, ys = lax.scan(step, h0, (to_chunks(x), to_chunks(dt), to_chunks(bm), to_chunks(cm)))
    return jnp.moveaxis(ys, 0, 1).reshape(bt, t, nh, p), h_fin


def gated_rmsnorm(y, z, g):
    v = y * jax.nn.silu(z)
    bt, t, _ = v.shape
    vg = v.reshape(bt, t, SSD_GROUPS, SSD_INNER // SSD_GROUPS)
    vg = vg * lax.rsqrt(jnp.mean(vg * vg, -1, keepdims=True) + RMS_EPS)
    return vg.reshape(bt, t, SSD_INNER) * g


def odd_mix(z, sc_buf, conv_buf, h0, chunk, sc_w, sc_b, cv_w, cv_b, dt_bias, a_log, d_skip, norm_g):
    f32 = jnp.float32
    bt, t, _ = z.shape
    o1 = SC_DIM
    o2 = 2 * SC_DIM
    o3 = 3 * SC_DIM
    o4 = o3 + SSD_INNER
    o5 = o4 + SSD_CONV_DIM
    sc_h = z[..., :o1]
    sc_bg = z[..., o1:o2]
    sc_cg = z[..., o2:o3]
    zg = z[..., o3:o4]
    xbc = z[..., o4:o5]
    dt_raw = z[..., o5:]
    conv_sc, new_sc = causal_conv(sc_cg * sc_h, sc_buf.astype(f32), sc_w, sc_b)
    y_sc = sc_bg * conv_sc
    xbc_c, new_conv = causal_conv(xbc, conv_buf.astype(f32), cv_w, cv_b)
    xbc_c = jax.nn.silu(xbc_c)
    gn = SSD_GROUPS * SSD_STATE
    xs = xbc_c[..., :SSD_INNER].reshape(bt, t, SSD_HEADS, SSD_HEAD_DIM)
    bm = xbc_c[..., SSD_INNER:SSD_INNER + gn].reshape(bt, t, SSD_GROUPS, SSD_STATE)
    cm = xbc_c[..., SSD_INNER + gn:].reshape(bt, t, SSD_GROUPS, SSD_STATE)
    dt = jax.nn.softplus((dt_raw + dt_bias).astype(f32))
    a = -jnp.exp(a_log.astype(f32))
    y, h_new = ssd_scan(xs, dt, a, bm, cm, h0.astype(f32), chunk)
    y = (y + d_skip[:, None] * xs).reshape(bt, t, SSD_INNER)
    y = gated_rmsnorm(y, zg, norm_g)
    mix = jnp.concatenate([y_sc, y], axis=-1)
    return mix, new_sc, new_conv, h_new


def moe_ffn(x, w_r, b_r, w_gu_bf16, w_down_bf16):
    n, d = x.shape
    logits = jnp.dot(x, w_r, precision=lax.Precision.HIGHEST) + b_r
    top_v, top_i = lax.top_k(logits, TOP_K)
    gate = jax.nn.softmax(top_v, axis=-1)
    flat_e = top_i.reshape(-1)
    order = jnp.argsort(flat_e, stable=True)
    sorted_e = flat_e[order]
    counts = jnp.bincount(flat_e, length=N_EXPERTS)
    padded = ((counts + ROW_TILE - 1) // ROW_TILE) * ROW_TILE
    pad_start = jnp.cumsum(padded) - padded
    start = jnp.cumsum(counts) - counts
    rank = jnp.arange(2 * n) - start[sorted_e]
    dest_sorted = (pad_start[sorted_e] + rank).astype(jnp.int32)
    n_tiles = (2 * n) // ROW_TILE + N_EXPERTS
    rows = n_tiles * ROW_TILE
    row_token = jnp.zeros((rows,), jnp.int32).at[dest_sorted].set((order // TOP_K).astype(jnp.int32))
    dest = jnp.zeros((2 * n,), jnp.int32).at[order].set(dest_sorted)
    tile_end = jnp.cumsum(padded) // ROW_TILE
    tile_expert = jnp.minimum(jnp.searchsorted(tile_end, jnp.arange(n_tiles), side='right'),
                              N_EXPERTS - 1).astype(jnp.int32)
    n_used = tile_end[-1:].astype(jnp.int32)
    xs = jnp.take(x, row_token, axis=0)
    ys = grouped_ffn(xs, w_gu_bf16, w_down_bf16, tile_expert, n_used)
    yk = jnp.take(ys, dest, axis=0).reshape(n, TOP_K, d)
    return jnp.sum(yk * gate[..., None], axis=1)


def kernel(x_prompt, x_sample, state_s5, cache_nsa_kv, state_win_kv, state_sc_conv, state_ssd_conv, state_ssd,
           page_table, ln_g, ln_b, w_in_even, s5_lam_re, s5_lam_im, s5_log_dt, s5_b, s5_c, s5_d, s5_w_glu,
           nsa_wk1, nsa_wk2, nsa_pe_k, nsa_wv1, nsa_wv2, nsa_pe_v, w_out_even, ffn_w_gu, ffn_w_down,
           w_in_odd, sc_conv_w, sc_conv_b, ssd_conv_w, ssd_conv_b, ssd_dt_bias, ssd_a_log, ssd_d, ssd_norm_g,
           w_out_odd, moe_router, moe_router_b, moe_w_gu, moe_w_down):
    f32 = jnp.float32
    bf16 = jnp.bfloat16
    bp, tp, d = x_prompt.shape
    bs, ts, _ = x_sample.shape
    n_p = bp * tp
    n_s = bs * ts
    w_buf = state_win_kv.shape[2]
    h = jnp.concatenate([x_prompt.astype(f32).reshape(n_p, d), x_sample.astype(f32).reshape(n_s, d)], axis=0)
    n = n_p + n_s
    one_tile = jnp.zeros((pl.cdiv(n, ROW_TILE),), jnp.int32)
    all_tiles = jnp.full((1,), pl.cdiv(n, ROW_TILE), jnp.int32)

    s5p = s5_params(s5_lam_re[0], s5_lam_im[0], s5_log_dt[0], s5_b[0], s5_c[0], s5_d[0], s5_w_glu[0])
    cmpp = (nsa_wk1[0], nsa_wk2[0], nsa_pe_k[0], nsa_wv1[0], nsa_wv2[0], nsa_pe_v[0])
    z = matmul(h, w_in_even[0].astype(bf16))
    zp = z[:n_p].reshape(bp, tp, -1)
    zs = z[n_p:].reshape(bs, ts, -1)
    mix_p, s5_p, kv_p, win_p = even_prompt_mix(zp, s5p, cmpp, w_buf)
    mix_s, s5_s, kv_s, win_s = even_sample_mix(zs, state_s5[0], cache_nsa_kv[0], page_table, state_win_kv[0],
                                               s5p, cmpp)
    mix = jnp.concatenate([mix_p.reshape(n_p, -1), mix_s.reshape(n_s, -1)], axis=0)
    m = matmul(mix, w_out_even[0].astype(bf16))
    h = layer_norm(ALPHA * h + m, ln_g[0, 0], ln_b[0, 0])
    f = grouped_ffn(h, ffn_w_gu.astype(bf16), ffn_w_down.astype(bf16), one_tile, all_tiles)
    h = layer_norm(ALPHA * h + f, ln_g[0, 1], ln_b[0, 1])

    oddp = (sc_conv_w[0], sc_conv_b[0], ssd_conv_w[0], ssd_conv_b[0], ssd_dt_bias[0],
            ssd_a_log[0], ssd_d[0], ssd_norm_g[0])
    z = matmul(h, w_in_odd[0].astype(bf16))
    zp = z[:n_p].reshape(bp, tp, -1)
    zs = z[n_p:].reshape(bs, ts, -1)
    mix_p, scc_p, sdc_p, ssd_p = odd_mix(zp, jnp.zeros((bp, SC_WIDTH - 1, SC_DIM), f32),
                                         jnp.zeros((bp, SSD_CONV - 1, SSD_CONV_DIM), f32),
                                         jnp.zeros((bp, SSD_HEADS, SSD_HEAD_DIM, SSD_STATE), f32),
                                         SSD_CHUNK, *oddp)
    mix_s, scc_s, sdc_s, ssd_s = odd_mix(zs, state_sc_conv[0], state_ssd_conv[0], state_ssd[0], ts, *oddp)
    mix = jnp.concatenate([mix_p.reshape(n_p, -1), mix_s.reshape(n_s, -1)], axis=0)
    m = matmul(mix, w_out_odd[0].astype(bf16))
    h = layer_norm(ALPHA * h + m, ln_g[1, 0], ln_b[1, 0])
    f = moe_ffn(h, moe_router[0], moe_router_b[0], moe_w_gu[0].astype(bf16), moe_w_down[0].astype(bf16))
    h = layer_norm(ALPHA * h + f, ln_g[1, 1], ln_b[1, 1])

    hp = h[:n_p].reshape(bp, tp, d)
    hs = h[n_p:].reshape(bs, ts, d)
    st = lambda a, ref: a[None].astype(ref.dtype)
    return (hp.astype(x_prompt.dtype), hs.astype(x_sample.dtype),
            st(s5_p, state_s5), st(s5_s, state_s5),
            st(kv_p, cache_nsa_kv), st(kv_s, cache_nsa_kv),
            st(win_p, state_win_kv), st(win_s, state_win_kv),
            st(scc_p, state_sc_conv), st(scc_s, state_sc_conv),
            st(sdc_p, state_ssd_conv), st(sdc_s, state_ssd_conv),
            st(ssd_p, state_ssd), st(ssd_s, state_ssd))
```

```python
import functools
import math

import jax
import jax.numpy as jnp
from jax import lax
from jax.experimental import pallas as pl
from jax.experimental.pallas import tpu as pltpu

D_MODEL = 1024
SEQ = 8192
DEPTH = 2
DEC_SEQ = 8
PAST_LEN = 16384
ALPHA = (2.0 * DEPTH) ** 0.25
LN_EPS = 1e-5
RMS_EPS = 1e-5
NEG = -1e30

S5_DIM = D_MODEL // 2
S5_GROUP = 16
S5_GROUPS = S5_DIM // S5_GROUP
S5_STATE = 64

HEAD_DIM = 64
N_HEADS = (D_MODEL // 2) // HEAD_DIM
KV_GROUPS = 2
HEADS_PER_GROUP = N_HEADS // KV_GROUPS
CMP_STRIDE = 16
CMP_LEN = 2 * CMP_STRIDE
SEL_BLOCK = 64
N_SEL = 16
WINDOW = 512
Q_BLOCK = 128
ROPE_THETA = 500000.0
ROT_DIM = HEAD_DIM // 4
FORCE = 1e4
NSA_Q = N_HEADS * HEAD_DIM
NSA_KV = 2 * KV_GROUPS * HEAD_DIM

SC_DIM = D_MODEL // 2
SC_WIDTH = 3
SSD_HEAD_DIM = 64
SSD_HEADS = 16
SSD_INNER = SSD_HEADS * SSD_HEAD_DIM
SSD_GROUPS = 4
SSD_STATE = 128
SSD_CONV = 4
SSD_CONV_DIM = SSD_INNER + 2 * SSD_GROUPS * SSD_STATE
SSD_CHUNK = 128

D_FF = 2816
N_EXPERTS = 8
TOP_K = 2

VMEM_LIMIT_BYTES = 56 * 1024 * 1024
LANES = 128
S5_N = S5_GROUPS * S5_STATE
S5_LT = S5_N // LANES
S5_CHUNK = 256
SEL_TILE = 1024
QK_SCALE = HEAD_DIM ** -0.5 * math.log2(math.e)
REMOVED = -3e38
PAGE_SIZE = 128
PAGES_PER_STEP = 8
NEW_PAD = 128
ROW_TILE = 512
FF_TILE = D_FF // 2


def _cparams(*sem):
    return pltpu.CompilerParams(dimension_semantics=sem, vmem_limit_bytes=VMEM_LIMIT_BYTES)


def _mm_kernel(x_ref, w_ref, o_ref):
    o_ref[...] = jnp.dot(x_ref[...].astype(jnp.bfloat16), w_ref[...],
                         preferred_element_type=jnp.float32)


def matmul(x, w_bf16):
    m, k = x.shape
    n = w_bf16.shape[1]
    return pl.pallas_call(
        _mm_kernel,
        grid=(pl.cdiv(m, ROW_TILE),),
        in_specs=[pl.BlockSpec((ROW_TILE, k), lambda i: (i, 0)),
                  pl.BlockSpec((k, n), lambda i: (0, 0), pipeline_mode=pl.Buffered(1))],
        out_specs=pl.BlockSpec((ROW_TILE, n), lambda i: (i, 0)),
        out_shape=jax.ShapeDtypeStruct((m, n), jnp.float32),
        compiler_params=_cparams("parallel"),
        name="matmul",
    )(x, w_bf16)


def _ffn_kernel(te_ref, nt_ref, x_ref, wg_ref, wu_ref, wd_ref, o_ref):
    t = pl.program_id(0)
    j = pl.program_id(1)

    @pl.when(t < nt_ref[0])
    def _():
        x = x_ref[...].astype(jnp.bfloat16)
        g = jnp.dot(x, wg_ref[...], preferred_element_type=jnp.float32)
        u = jnp.dot(x, wu_ref[...], preferred_element_type=jnp.float32)
        h = (g * jax.nn.sigmoid(g) * u).astype(jnp.bfloat16)
        part = jnp.dot(h, wd_ref[...], preferred_element_type=jnp.float32)

        @pl.when(j == 0)
        def _():
            o_ref[...] = part

        @pl.when(j > 0)
        def _():
            o_ref[...] += part

    @pl.when(jnp.logical_and(t >= nt_ref[0], j == 0))
    def _():
        o_ref[...] = jnp.zeros_like(o_ref)


def grouped_ffn(x, w_gu_bf16, w_down_bf16, tile_expert, n_tiles_used):
    r, d = x.shape
    nf = D_FF // FF_TILE
    assert nf == 2
    n_tiles = pl.cdiv(r, ROW_TILE)
    grid_spec = pltpu.PrefetchScalarGridSpec(
        num_scalar_prefetch=2,
        grid=(n_tiles, nf),
        in_specs=[
            pl.BlockSpec((ROW_TILE, d), lambda t, j, te, nt: (t, 0)),
            pl.BlockSpec((None, d, FF_TILE), lambda t, j, te, nt: (te[t], 0, j)),
            pl.BlockSpec((None, d, FF_TILE), lambda t, j, te, nt: (te[t], 0, nf + j)),
            pl.BlockSpec((None, FF_TILE, d), lambda t, j, te, nt: (te[t], j, 0)),
        ],
        out_specs=pl.BlockSpec((ROW_TILE, d), lambda t, j, te, nt: (t, 0)),
    )
    return pl.pallas_call(
        _ffn_kernel,
        grid_spec=grid_spec,
        out_shape=jax.ShapeDtypeStruct((r, d), jnp.float32),
        compiler_params=_cparams("parallel", "arbitrary"),
        name="grouped_ffn",
    )(tile_expert, n_tiles_used, x, w_gu_bf16, w_gu_bf16, w_down_bf16)


def _s5_kernel(u_ref, h0r_ref, h0i_ref, ar_ref, ai_ref, bbr_ref, bbi_ref, cr_ref, ci_ref, d_ref, wglu_ref,
               y_ref, hro_ref, hio_ref, bur, bui, sr, si, hr, hi, *, chains, chunk):
    j = pl.program_id(0)

    @pl.when(j == 0)
    def _():
        hr[...] = h0r_ref[...]
        hi[...] = h0i_ref[...]

    u = u_ref[...].reshape(chains * chunk, S5_DIM)
    ub = u.astype(jnp.bfloat16)
    bu_r = jnp.dot(ub, bbr_ref[...], preferred_element_type=jnp.float32)
    bu_i = jnp.dot(ub, bbi_ref[...], preferred_element_type=jnp.float32)
    for k in range(S5_LT):
        bur[k] = bu_r[:, k * LANES:(k + 1) * LANES]
        bui[k] = bu_i[:, k * LANES:(k + 1) * LANES]
    ar = [jnp.broadcast_to(ar_ref[:, k * LANES:(k + 1) * LANES], (chains, LANES)) for k in range(S5_LT)]
    ai = [jnp.broadcast_to(ai_ref[:, k * LANES:(k + 1) * LANES], (chains, LANES)) for k in range(S5_LT)]

    def body(t, carry):
        rows = pl.ds(t, chains, stride=chunk)
        out = []
        for k in range(S5_LT):
            xr, xi = carry[2 * k], carry[2 * k + 1]
            nr = ar[k] * xr - ai[k] * xi + bur[k, rows, :]
            ni = ar[k] * xi + ai[k] * xr + bui[k, rows, :]
            sr[k, rows, :] = nr
            si[k, rows, :] = ni
            out += [nr, ni]
        return tuple(out)

    init = []
    for k in range(S5_LT):
        init += [hr[:, k * LANES:(k + 1) * LANES], hi[:, k * LANES:(k + 1) * LANES]]
    fin = lax.fori_loop(0, chunk, body, tuple(init))
    xr = jnp.concatenate(fin[0::2], axis=1)
    xi = jnp.concatenate(fin[1::2], axis=1)
    hr[...] = xr
    hi[...] = xi
    hro_ref[...] = xr
    hio_ref[...] = xi
    s_r = jnp.concatenate([sr[k] for k in range(S5_LT)], axis=1).astype(jnp.bfloat16)
    s_i = jnp.concatenate([si[k] for k in range(S5_LT)], axis=1).astype(jnp.bfloat16)
    y = (jnp.dot(s_r, cr_ref[...], preferred_element_type=jnp.float32)
         - jnp.dot(s_i, ci_ref[...], preferred_element_type=jnp.float32)
         + d_ref[...] * u)
    z = jax.nn.gelu(y)
    gate = jax.nn.sigmoid(jnp.dot(z.astype(jnp.bfloat16), wglu_ref[...], preferred_element_type=jnp.float32))
    y_ref[...] = (z * gate).reshape(chains, chunk, S5_DIM)


def s5_params(lam_re, lam_im, log_dt, b, c, d, w_glu):
    f32 = jnp.float32
    dt = jnp.exp(log_dt.astype(f32))[:, None]
    mag = jnp.exp(lam_re * dt)
    ang = lam_im * dt
    ab_re = mag * jnp.cos(ang)
    ab_im = mag * jnp.sin(ang)
    den = lam_re * lam_re + lam_im * lam_im
    nr = ab_re - 1.0
    coef_re = (nr * lam_re + ab_im * lam_im) / den
    coef_im = (ab_im * lam_re - nr * lam_im) / den
    b_re = b[..., 0].astype(f32)
    b_im = b[..., 1].astype(f32)
    bb_re = coef_re[..., None] * b_re - coef_im[..., None] * b_im
    bb_im = coef_re[..., None] * b_im + coef_im[..., None] * b_re
    eye = jnp.eye(S5_GROUPS, dtype=f32)
    bbr = jnp.einsum('gnk,gh->gkhn', bb_re, eye).reshape(S5_DIM, S5_N).astype(jnp.bfloat16)
    bbi = jnp.einsum('gnk,gh->gkhn', bb_im, eye).reshape(S5_DIM, S5_N).astype(jnp.bfloat16)
    cr = jnp.einsum('gkn,gh->gnhk', c[..., 0].astype(f32), eye).reshape(S5_N, S5_DIM).astype(jnp.bfloat16)
    ci = jnp.einsum('gkn,gh->gnhk', c[..., 1].astype(f32), eye).reshape(S5_N, S5_DIM).astype(jnp.bfloat16)
    return (ab_re.reshape(1, S5_N), ab_im.reshape(1, S5_N), bbr, bbi, cr, ci,
            d.astype(f32).reshape(1, S5_DIM), w_glu.astype(jnp.bfloat16))


def s5_scan(u, h0, params, chunk):
    chains, t, _ = u.shape
    ar, ai, bbr, bbi, cr, ci, d, wglu = params
    h0r = h0[..., 0].reshape(chains, S5_N)
    h0i = h0[..., 1].reshape(chains, S5_N)
    full = lambda shape: pl.BlockSpec(shape, lambda j: (0,) * len(shape))
    rows = chains * chunk
    y, hr, hi = pl.pallas_call(
        functools.partial(_s5_kernel, chains=chains, chunk=chunk),
        grid=(t // chunk,),
        in_specs=[pl.BlockSpec((chains, chunk, S5_DIM), lambda j: (0, j, 0)),
                  full((chains, S5_N)), full((chains, S5_N)), full((1, S5_N)), full((1, S5_N)),
                  full((S5_DIM, S5_N)), full((S5_DIM, S5_N)), full((S5_N, S5_DIM)), full((S5_N, S5_DIM)),
                  full((1, S5_DIM)), full((S5_DIM, S5_DIM))],
        out_specs=[pl.BlockSpec((chains, chunk, S5_DIM), lambda j: (0, j, 0)),
                   full((chains, S5_N)), full((chains, S5_N))],
        out_shape=[jax.ShapeDtypeStruct((chains, t, S5_DIM), jnp.float32),
                   jax.ShapeDtypeStruct((chains, S5_N), jnp.float32),
                   jax.ShapeDtypeStruct((chains, S5_N), jnp.float32)],
        scratch_shapes=[pltpu.VMEM((S5_LT, rows, LANES), jnp.float32)] * 4
                       + [pltpu.VMEM((chains, S5_N), jnp.float32)] * 2,
        compiler_params=_cparams("arbitrary"),
        name="s5_scan",
    )(u, h0r, h0i, ar, ai, bbr, bbi, cr, ci, d, wglu)
    new_state = jnp.stack([hr.reshape(chains, S5_GROUPS, S5_STATE), hi.reshape(chains, S5_GROUPS, S5_STATE)],
                          axis=-1)
    return y, new_state


def _dot_nt(a, b):
    return lax.dot_general(a, b, (((1,), (1,)), ((), ())), preferred_element_type=jnp.float32)


def _split3(x):
    hi = x.astype(jnp.bfloat16)
    rem = x - hi.astype(jnp.float32)
    mid = rem.astype(jnp.bfloat16)
    lo = (rem - mid.astype(jnp.float32)).astype(jnp.bfloat16)
    return hi, mid, lo


def _softmax_rows(s, mask):
    s = jnp.where(mask, s, NEG)
    m = jnp.max(s, axis=-1, keepdims=True)
    p = jnp.exp2(s - m)
    inv = jnp.where(m > 0.5 * NEG, 1.0 / jnp.sum(p, axis=-1, keepdims=True), 0.0)
    return p * inv


def _nsa_prompt_kernel(q_ref, gate_ref, kc_ref, vc_ref, ks_ref, vs_ref, kw_ref, vw_ref, o_ref, *, n_cmp, n_blk):
    f32, bf16 = jnp.float32, jnp.bfloat16
    r4 = HEADS_PER_GROUP
    n_cpad = kc_ref.shape[0]
    start = pl.program_id(1) * Q_BLOCK
    q = q_ref[...] * QK_SCALE
    gate = gate_ref[...]
    lane = lax.broadcasted_iota(jnp.int32, (Q_BLOCK, LANES), 1)
    qpos = start + lax.broadcasted_iota(jnp.int32, (Q_BLOCK, 1), 0)
    n_idx = lax.broadcasted_iota(jnp.int32, (Q_BLOCK, n_cpad), 1)
    cmask = (((n_idx * CMP_STRIDE + (CMP_LEN - 1)) <= qpos) & (n_idx < n_cmp))[None]
    ratio = SEL_BLOCK // CMP_STRIDE
    gsum = (lax.broadcasted_iota(jnp.int32, (n_blk, n_cpad), 1) // ratio
            == lax.broadcasted_iota(jnp.int32, (n_blk, n_cpad), 0)).astype(bf16)
    blk = lax.broadcasted_iota(jnp.int32, (n_blk, Q_BLOCK), 0)
    blk_f = blk.astype(f32)
    jq = (start + lax.broadcasted_iota(jnp.int32, (n_blk, Q_BLOCK), 1)) // SEL_BLOCK
    force = jnp.where((blk == 0) | (blk == jq) | (blk == jq - 1), FORCE, 0.0)
    qgs, o_cs, sels = [], [], []
    for g in range(KV_GROUPS):
        keep = (lane < HEAD_DIM) if g == 0 else (lane >= HEAD_DIM)
        parts = []
        for r in range(r4):
            h = r4 * g + r
            tile = q[:, (h // 2) * LANES:(h // 2 + 1) * LANES]
            if h % 2 != g:
                tile = pltpu.roll(tile, HEAD_DIM, axis=1)
            parts.append(jnp.where(keep, tile, 0.0))
        qg = jnp.concatenate(parts, axis=0).astype(bf16)
        qgs.append(qg)

        p_c = _softmax_rows(_dot_nt(qg, kc_ref[...]).reshape(r4, Q_BLOCK, n_cpad), cmask)
        o_cs.append(jnp.dot(p_c.reshape(r4 * Q_BLOCK, n_cpad).astype(bf16), vc_ref[...],
                            preferred_element_type=f32).reshape(r4, Q_BLOCK, LANES))
        psum = p_c[0] + p_c[1] + p_c[2] + p_c[3]
        imp_t = sum(_dot_nt(gsum, part) for part in _split3(psum))

        score = jnp.where(blk <= jq, imp_t + force, NEG)
        sel_t = jnp.zeros((n_blk, Q_BLOCK), f32)
        for _ in range(min(N_SEL, n_blk)):
            m = jnp.max(score, axis=0, keepdims=True)
            idx = jnp.min(jnp.where(score == m, blk_f, float(n_blk)), axis=0, keepdims=True)
            hit = blk_f == idx
            sel_t = jnp.where(hit & (m > 0.5 * NEG), 1.0, sel_t)
            score = jnp.where(hit, REMOVED, score)
        sels.append(sel_t.T)

    n_full = start // SEL_TILE
    expand0 = (lax.broadcasted_iota(jnp.int32, (n_blk, SEL_TILE), 0)
               == lax.broadcasted_iota(jnp.int32, (n_blk, SEL_TILE), 1) // SEL_BLOCK).astype(bf16)

    def tile_update(i, carry, causal):
        off = pl.multiple_of(i * SEL_TILE, SEL_TILE)
        k = ks_ref[pl.ds(off, SEL_TILE), :]
        v = vs_ref[pl.ds(off, SEL_TILE), :]
        out = []
        for g in range(KV_GROUPS):
            m_run, l_run, acc = carry[g]
            s_t = _dot_nt(qgs[g], k).reshape(r4, Q_BLOCK, SEL_TILE)
            shifted = pltpu.roll(sels[g], (n_blk - i * (SEL_TILE // SEL_BLOCK)) % n_blk, axis=1).astype(bf16)
            mk = jnp.dot(shifted, expand0, preferred_element_type=f32) > 0.5
            if causal:
                kpos = i * SEL_TILE + lax.broadcasted_iota(jnp.int32, (Q_BLOCK, SEL_TILE), 1)
                mk = mk & (kpos <= qpos)
            s_t = jnp.where(mk[None], s_t, NEG)
            m_new = jnp.maximum(m_run, jnp.max(s_t, axis=-1, keepdims=True))
            alpha = jnp.exp2(m_run - m_new)
            p = jnp.exp2(s_t - m_new)
            l_new = alpha * l_run + jnp.sum(p, axis=-1, keepdims=True)
            pv = jnp.dot(p.reshape(r4 * Q_BLOCK, SEL_TILE).astype(bf16), v, preferred_element_type=f32)
            out.append((m_new, l_new, alpha * acc + pv.reshape(r4, Q_BLOCK, LANES)))
        return tuple(out)

    init = (jnp.full((r4, Q_BLOCK, 1), NEG, f32), jnp.zeros((r4, Q_BLOCK, 1), f32),
            jnp.zeros((r4, Q_BLOCK, LANES), f32))
    carry = lax.fori_loop(0, n_full, lambda i, c: tile_update(i, c, False), (init, init))
    fin = tile_update(n_full, carry, True)

    n_win = WINDOW + Q_BLOCK
    woff = pl.multiple_of(start, Q_BLOCK)
    kwin = kw_ref[pl.ds(woff, n_win), :]
    vwin = vw_ref[pl.ds(woff, n_win), :]
    wpos = start - WINDOW + lax.broadcasted_iota(jnp.int32, (Q_BLOCK, n_win), 1)
    wmask = ((wpos <= qpos) & (wpos > qpos - WINDOW) & (wpos >= 0))[None]
    heads = [None] * N_HEADS
    for g in range(KV_GROUPS):
        m_fin, l_fin, acc = fin[g]
        o_s = acc * jnp.where(m_fin > 0.5 * NEG, 1.0 / l_fin, 0.0)
        p_w = _softmax_rows(_dot_nt(qgs[g], kwin).reshape(r4, Q_BLOCK, n_win), wmask)
        o_w = jnp.dot(p_w.reshape(r4 * Q_BLOCK, n_win).astype(bf16), vwin,
                      preferred_element_type=f32).reshape(r4, Q_BLOCK, LANES)
        for r in range(r4):
            h = r4 * g + r
            heads[h] = (gate[:, 3 * h:3 * h + 1] * o_cs[g][r] + gate[:, 3 * h + 1:3 * h + 2] * o_s[r]
                        + gate[:, 3 * h + 2:3 * h + 3] * o_w[r])

    tiles = []
    for j in range(N_HEADS // 2):
        even, odd = heads[2 * j], heads[2 * j + 1]
        if j // 2 == 0:
            tiles.append(jnp.where(lane < HEAD_DIM, even, pltpu.roll(odd, HEAD_DIM, axis=1)))
        else:
            tiles.append(jnp.where(lane < HEAD_DIM, pltpu.roll(even, HEAD_DIM, axis=1), odd))
    o_ref[...] = jnp.concatenate(tiles, axis=1)


def nsa_prompt(q, gates, kc, vc, ks, vs, kw_pad, vw_pad):
    b, t, _ = q.shape
    n_cpad = kc.shape[1]
    kern = functools.partial(_nsa_prompt_kernel, n_cmp=t // CMP_STRIDE - 1, n_blk=t // SEL_BLOCK)
    whole = lambda rows: pl.BlockSpec((None, rows, LANES), lambda i, j: (i, 0, 0))
    return pl.pallas_call(
        kern,
        grid=(b, t // Q_BLOCK),
        in_specs=[pl.BlockSpec((None, Q_BLOCK, NSA_Q), lambda i, j: (i, j, 0)),
                  pl.BlockSpec((None, Q_BLOCK, 3 * N_HEADS), lambda i, j: (i, j, 0)),
                  whole(n_cpad), whole(n_cpad), whole(t), whole(t), whole(t + WINDOW), whole(t + WINDOW)],
        out_specs=pl.BlockSpec((None, Q_BLOCK, NSA_Q), lambda i, j: (i, j, 0)),
        out_shape=jax.ShapeDtypeStruct((b, t, NSA_Q), jnp.float32),
        compiler_params=_cparams("parallel", "arbitrary"),
        name="nsa_prompt",
    )(q, gates, kc, vc, ks, vs, kw_pad, vw_pad)


def _compress_kernel(ch_ref, pet_ref, peb_ref, w1t_ref, w1b_ref, w2_ref, o_ref):
    bf16 = jnp.bfloat16
    ch = ch_ref[...]
    n_ch = ch.shape[0]
    a = jnp.dot((ch + pet_ref[...]).astype(bf16), w1t_ref[...], preferred_element_type=jnp.float32)
    b = jnp.dot((ch + peb_ref[...]).astype(bf16), w1b_ref[...], preferred_element_type=jnp.float32)
    pre = a + pltpu.roll(b, n_ch - 1, axis=0)
    o_ref[...] = jnp.dot(jax.nn.gelu(pre).astype(bf16), w2_ref[...],
                         preferred_element_type=jnp.float32).astype(o_ref.dtype)


def compress_params(w1, w2, pe):
    f32 = jnp.float32
    eye = jnp.eye(KV_GROUPS, dtype=f32)
    w1r = w1.astype(f32).reshape(2, CMP_STRIDE, HEAD_DIM, HEAD_DIM)
    big = jnp.einsum('hjde,gk->hjgdke', w1r, eye).reshape(2, CMP_STRIDE * LANES, LANES).astype(jnp.bfloat16)
    w2bd = jnp.einsum('de,gk->gdke', w2.astype(f32), eye).reshape(LANES, LANES).astype(jnp.bfloat16)
    per = pe.astype(f32).reshape(2, CMP_STRIDE, 1, HEAD_DIM)
    pe_rows = jnp.broadcast_to(per, (2, CMP_STRIDE, KV_GROUPS, HEAD_DIM)).reshape(2, 1, CMP_STRIDE * LANES)
    return pe_rows[0], pe_rows[1], big[0], big[1], w2bd


def compress_prompt(x, params):
    b, t, _ = x.shape
    n_ch = t // CMP_STRIDE
    ch = x.reshape(b, n_ch, CMP_STRIDE * LANES)
    pet, peb, w1t, w1b, w2bd = params
    full = lambda shape: pl.BlockSpec(shape, lambda i: (0,) * len(shape))
    return pl.pallas_call(
        _compress_kernel,
        grid=(b,),
        in_specs=[pl.BlockSpec((None, n_ch, CMP_STRIDE * LANES), lambda i: (i, 0, 0)),
                  full((1, CMP_STRIDE * LANES)), full((1, CMP_STRIDE * LANES)),
                  full((CMP_STRIDE * LANES, LANES)), full((CMP_STRIDE * LANES, LANES)), full((LANES, LANES))],
        out_specs=pl.BlockSpec((None, n_ch, LANES), lambda i: (i, 0, 0)),
        out_shape=jax.ShapeDtypeStruct((b, n_ch, LANES), jnp.bfloat16),
        compiler_params=_cparams("parallel"),
        name="compress_prompt",
    )(ch, pet, peb, w1t, w1b, w2bd)


def _cmp_sample_kernel(pt_ref, *refs, n_pages):
    f32, bf16 = jnp.float32, jnp.bfloat16
    pp = PAGES_PER_STEP
    kpages, vpages = refs[0:pp], refs[pp:2 * pp]
    (newk_ref, newv_ref, wk_ref, wv_ref, ck_ref, cv_ref, w2k_ref, w2v_ref,
     kc_ref, vc_ref, slab_k, slab_v) = refs[2 * pp:]
    s = pl.program_id(1)
    base = pl.multiple_of(s * (pp * PAGE_SIZE), pp * PAGE_SIZE)
    for i in range(pp):
        slab_k[pl.ds(base + i * PAGE_SIZE, PAGE_SIZE), :] = kpages[i][...]
        slab_v[pl.ds(base + i * PAGE_SIZE, PAGE_SIZE), :] = vpages[i][...]

    @pl.when(s == pl.num_programs(1) - 1)
    def _():
        n_ch = n_pages * (PAGE_SIZE // CMP_STRIDE)
        row = lax.broadcasted_iota(jnp.int32, (n_ch, LANES), 0)
        for slab, new_ref, w_ref, c_ref, w2_ref, o_ref in ((slab_k, newk_ref, wk_ref, ck_ref, w2k_ref, kc_ref),
                                                           (slab_v, newv_ref, wv_ref, cv_ref, w2v_ref, vc_ref)):
            ch = jnp.concatenate([slab[pl.ds(j, n_ch, stride=CMP_STRIDE), :] for j in range(CMP_STRIDE)],
                                 axis=1).astype(bf16)
            ab = jnp.dot(ch, w_ref[...], preferred_element_type=f32)
            b_new = jnp.dot(new_ref[...].astype(bf16), w_ref[...], preferred_element_type=f32)[0:1, LANES:]
            nxt = pltpu.roll(ab[:, LANES:], n_ch - 1, axis=0)
            nxt = jnp.where(row == n_ch - 1, b_new, nxt)
            pre = ab[:, :LANES] + nxt + c_ref[...]
            o_ref[...] = jnp.dot(jax.nn.gelu(pre).astype(bf16), w2_ref[...],
                                 preferred_element_type=f32).astype(o_ref.dtype)


def compress_sample_params(w1, w2, pe):
    pet, peb, w1t, w1b, w2bd = compress_params(w1, w2, pe)
    hp = lax.Precision.HIGHEST
    const = (jnp.dot(pet, w1t.astype(jnp.float32), precision=hp)
             + jnp.dot(peb, w1b.astype(jnp.float32), precision=hp))
    return jnp.concatenate([w1t, w1b], axis=1), const, w2bd


def _page_spec(i, lane_block):
    return pl.BlockSpec((None, PAGE_SIZE, LANES),
                        lambda b, s, pt: (pt[b, PAGES_PER_STEP * s + i], 0, lane_block))


def _per_seq(shape):
    return pl.BlockSpec((None,) + shape, lambda b, s, pt: (b, 0, 0))


def compress_sample(pool3, page_table, new_k, new_v, pk, pv):
    bsz, n_pages = page_table.shape
    pp = PAGES_PER_STEP
    n_ch = n_pages * (PAGE_SIZE // CMP_STRIDE)
    t_new = new_k.shape[1]

    def chunk_rows(x):
        x = jnp.pad(x, ((0, 0), (0, CMP_STRIDE - t_new), (0, 0))).reshape(bsz, 1, CMP_STRIDE * LANES)
        return jnp.pad(x, ((0, 0), (0, 7), (0, 0)))

    full = lambda shape: pl.BlockSpec(shape, lambda b, s, pt: (0,) * len(shape))
    wk, ck, w2k = pk
    wv, cv, w2v = pv
    grid_spec = pltpu.PrefetchScalarGridSpec(
        num_scalar_prefetch=1,
        grid=(bsz, n_pages // pp),
        in_specs=[_page_spec(i, 0) for i in range(pp)] + [_page_spec(i, 1) for i in range(pp)]
                 + [_per_seq((8, CMP_STRIDE * LANES)), _per_seq((8, CMP_STRIDE * LANES)),
                    full((CMP_STRIDE * LANES, 2 * LANES)), full((CMP_STRIDE * LANES, 2 * LANES)),
                    full((1, LANES)), full((1, LANES)), full((LANES, LANES)), full((LANES, LANES))],
        out_specs=[_per_seq((n_ch, LANES)), _per_seq((n_ch, LANES))],
        scratch_shapes=[pltpu.VMEM((n_pages * PAGE_SIZE, LANES), jnp.float32)] * 2,
    )
    return pl.pallas_call(
        functools.partial(_cmp_sample_kernel, n_pages=n_pages),
        grid_spec=grid_spec,
        out_shape=[jax.ShapeDtypeStruct((bsz, n_ch, LANES), jnp.bfloat16)] * 2,
        compiler_params=_cparams("parallel", "arbitrary"),
        name="compress_sample",
    )(page_table, *([pool3] * (2 * pp)), chunk_rows(new_k), chunk_rows(new_v), wk, wv, ck, cv, w2k, w2v)


def _nsa_sample_kernel(pt_ref, *refs, n_pages, t_new, w_buf):
    f32, bf16 = jnp.float32, jnp.bfloat16
    pp = PAGES_PER_STEP
    q_ref, gate_ref, kc_ref, vc_ref = refs[0:4]
    kpages, vpages = refs[4:4 + pp], refs[4 + pp:4 + 2 * pp]
    (ksn_ref, vsn_ref, win_ref, kwn_ref, vwn_ref, o_ref,
     sel_scr, oc_scr, m_scr, l_scr, acc_scr) = refs[4 + 2 * pp:]
    r4, g2 = HEADS_PER_GROUP, KV_GROUPS
    n_rows = g2 * r4 * t_new
    past_len = n_pages * PAGE_SIZE
    n_cmp = kc_ref.shape[0]
    n_bpad = sel_scr.shape[1]
    tile = pp * PAGE_SIZE
    s = pl.program_id(1)
    qall = q_ref[...]
    qpos = past_len + lax.broadcasted_iota(jnp.int32, (n_rows, 1), 0) % t_new

    def grouped(x):
        return x.reshape(g2, 1, t_new, x.shape[-1])

    @pl.when(s == 0)
    def _():
        s_c = _dot_nt(qall, kc_ref[...])
        n_idx = lax.broadcasted_iota(jnp.int32, (n_rows, n_cmp), 1)
        p_c = _softmax_rows(s_c, (n_idx * CMP_STRIDE + (CMP_LEN - 1)) <= qpos)
        oc_scr[...] = jnp.dot(p_c.astype(bf16), vc_ref[...], preferred_element_type=f32)
        psum = jnp.sum(p_c.reshape(g2, r4, t_new, n_cmp), axis=1).reshape(g2 * t_new, n_cmp)
        psum = jnp.concatenate([psum, jnp.zeros((LANES - g2 * t_new, n_cmp), f32)], axis=0)
        p_hi = psum.astype(bf16)
        rem = psum - p_hi.astype(f32)
        p_mid = rem.astype(bf16)
        p_lo = (rem - p_mid.astype(f32)).astype(bf16)
        ratio = SEL_BLOCK // CMP_STRIDE
        gsum = (lax.broadcasted_iota(jnp.int32, (n_bpad, n_cmp), 1) // ratio
                == lax.broadcasted_iota(jnp.int32, (n_bpad, n_cmp), 0)).astype(bf16)
        imp_t = _dot_nt(gsum, p_hi) + _dot_nt(gsum, p_mid) + _dot_nt(gsum, p_lo)
        blk = lax.broadcasted_iota(jnp.int32, (n_bpad, LANES), 0)
        jq = (past_len + lax.broadcasted_iota(jnp.int32, (n_bpad, LANES), 1) % t_new) // SEL_BLOCK
        forced = (blk == 0) | (blk == jq) | (blk == jq - 1)
        score = jnp.where(blk <= jq, imp_t + jnp.where(forced, FORCE, 0.0), NEG)
        blk_f = blk.astype(f32)
        sel_t = jnp.zeros((n_bpad, LANES), f32)
        for _ in range(N_SEL):
            m = jnp.max(score, axis=0, keepdims=True)
            idx = jnp.min(jnp.where(score == m, blk_f, float(n_bpad)), axis=0, keepdims=True)
            hit = blk_f == idx
            sel_t = jnp.where(hit & (m > 0.5 * NEG), 1.0, sel_t)
            score = jnp.where(hit, REMOVED, score)
        sel = jnp.concatenate([sel_t[k * LANES:(k + 1) * LANES].T for k in range(n_bpad // LANES)], axis=1)
        sel_scr[...] = sel[0:g2 * t_new].astype(bf16)
        m_scr[...] = jnp.full(m_scr.shape, NEG, f32)
        l_scr[...] = jnp.zeros(l_scr.shape, f32)
        acc_scr[...] = jnp.zeros(acc_scr.shape, f32)

    def online_update(s_t, mk, v):
        n = s_t.shape[-1]
        s4 = jnp.where(mk, s_t.reshape(g2, r4, t_new, n), NEG)
        m_run = m_scr[...].reshape(g2, r4, t_new, 1)
        m_new = jnp.maximum(m_run, jnp.max(s4, axis=-1, keepdims=True))
        alpha = jnp.exp2(m_run - m_new)
        p = jnp.exp2(s4 - m_new)
        l_new = alpha * l_scr[...].reshape(g2, r4, t_new, 1) + jnp.sum(p, axis=-1, keepdims=True)
        pv = jnp.dot(p.reshape(n_rows, n).astype(bf16), v, preferred_element_type=f32)
        m_scr[...] = m_new.reshape(n_rows, 1)
        l_scr[...] = l_new.reshape(n_rows, 1)
        acc_scr[...] = alpha.reshape(n_rows, 1) * acc_scr[...] + pv

    kt = jnp.concatenate([r[...] for r in kpages], axis=0).astype(bf16)
    vt = jnp.concatenate([r[...] for r in vpages], axis=0).astype(bf16)
    key = lax.broadcasted_iota(jnp.int32, (n_bpad, tile), 1)
    expand = (lax.broadcasted_iota(jnp.int32, (n_bpad, tile), 0)
              == s * (tile // SEL_BLOCK) + key // SEL_BLOCK).astype(bf16)
    picked = jnp.dot(sel_scr[...], expand, preferred_element_type=f32)
    online_update(_dot_nt(qall, kt), grouped(picked) > 0.5, vt)

    @pl.when(s == pl.num_programs(1) - 1)
    def _():
        new_blk = past_len // SEL_BLOCK
        kidx = lax.broadcasted_iota(jnp.int32, (n_rows, NEW_PAD), 1)
        causal = ((past_len + kidx) <= qpos) & (kidx < t_new)
        picked_new = sel_scr[:, new_blk:new_blk + 1].astype(f32)
        mk = (grouped(picked_new) > 0.5) & causal.reshape(g2, r4, t_new, NEW_PAD)
        online_update(_dot_nt(qall, ksn_ref[...]), mk, vsn_ref[...])
        o_s = acc_scr[...] * jnp.where(m_scr[...] > 0.5 * NEG, 1.0 / l_scr[...], 0.0)

        n_win = w_buf + NEW_PAD
        kw = jnp.concatenate([win_ref[:, 0:LANES].astype(bf16), kwn_ref[...]], axis=0)
        vw = jnp.concatenate([win_ref[:, LANES:2 * LANES].astype(bf16), vwn_ref[...]], axis=0)
        widx = lax.broadcasted_iota(jnp.int32, (n_rows, n_win), 1)
        wpos = past_len - w_buf + widx
        wmask = (wpos <= qpos) & (wpos > qpos - WINDOW) & (wpos >= 0) & (widx < w_buf + t_new)
        p_w = _softmax_rows(_dot_nt(qall, kw), wmask)
        o_w = jnp.dot(p_w.astype(bf16), vw, preferred_element_type=f32)
        gate = gate_ref[...]
        o_ref[...] = gate[:, 0:1] * oc_scr[...] + gate[:, 1:2] * o_s + gate[:, 2:3] * o_w


def nsa_sample(q, gates, kc, vc, pool3, page_table, ks_new, vs_new, win, kw_new, vw_new):
    f32, bf16 = jnp.float32, jnp.bfloat16
    bsz, t_new = q.shape[0], q.shape[1]
    n_pages = page_table.shape[1]
    pp = PAGES_PER_STEP
    w_buf = win.shape[1]
    r4, g2 = HEADS_PER_GROUP, KV_GROUPS
    n_rows = g2 * r4 * t_new
    past_len = n_pages * PAGE_SIZE
    assert past_len % SEL_BLOCK == 0 and t_new <= SEL_BLOCK and past_len >= w_buf and n_pages % pp == 0
    n_sel = past_len // SEL_BLOCK + 1
    n_bpad = -(-n_sel // LANES) * LANES
    eye = jnp.eye(g2, dtype=f32)
    qg = q.reshape(bsz, t_new, g2, r4, HEAD_DIM).transpose(0, 2, 3, 1, 4) * QK_SCALE
    qall = jnp.einsum('bgrqd,gk->bgrqkd', qg, eye).reshape(bsz, n_rows, LANES).astype(bf16)
    gall = gates.reshape(bsz, t_new, g2, r4, 3).transpose(0, 2, 3, 1, 4).reshape(bsz, n_rows, 3)
    pad_rows = lambda x: jnp.pad(x, ((0, 0), (0, NEW_PAD - t_new), (0, 0))).astype(bf16)
    n_cmp = kc.shape[1]
    grid_spec = pltpu.PrefetchScalarGridSpec(
        num_scalar_prefetch=1,
        grid=(bsz, n_pages // pp),
        in_specs=[_per_seq((n_rows, LANES)), _per_seq((n_rows, 3)), _per_seq((n_cmp, LANES)),
                  _per_seq((n_cmp, LANES))]
                 + [_page_spec(i, 2) for i in range(pp)] + [_page_spec(i, 3) for i in range(pp)]
                 + [_per_seq((NEW_PAD, LANES)), _per_seq((NEW_PAD, LANES)), _per_seq((w_buf, 2 * LANES)),
                    _per_seq((NEW_PAD, LANES)), _per_seq((NEW_PAD, LANES))],
        out_specs=_per_seq((n_rows, LANES)),
        scratch_shapes=[pltpu.VMEM((g2 * t_new, n_bpad), bf16), pltpu.VMEM((n_rows, LANES), f32),
                        pltpu.VMEM((n_rows, 1), f32), pltpu.VMEM((n_rows, 1), f32),
                        pltpu.VMEM((n_rows, LANES), f32)],
    )
    o = pl.pallas_call(
        functools.partial(_nsa_sample_kernel, n_pages=n_pages, t_new=t_new, w_buf=w_buf),
        grid_spec=grid_spec,
        out_shape=jax.ShapeDtypeStruct((bsz, n_rows, LANES), f32),
        compiler_params=_cparams("parallel", "arbitrary"),
        name="nsa_sample",
    )(page_table, qall, gall, kc, vc, *([pool3] * (2 * pp)), pad_rows(ks_new), pad_rows(vs_new), win,
      pad_rows(kw_new), pad_rows(vw_new))
    o = jnp.einsum('bgrqkd,gk->bqgrd', o.reshape(bsz, g2, r4, t_new, g2, HEAD_DIM), eye)
    return o.reshape(bsz, t_new, NSA_Q)


def _ssd_kernel(x_ref, b_ref, c_ref, dt_ref, a_ref, z_ref, dskip_ref, ng_ref, y_ref, hout_ref, h_scr, *, chunk):
    f32, bf16 = jnp.float32, jnp.bfloat16
    n_l = chunk
    hpg = SSD_HEADS // SSD_GROUPS
    gw = hpg * SSD_HEAD_DIM
    j = pl.program_id(1)

    @pl.when(j == 0)
    def _():
        h_scr[...] = jnp.zeros(h_scr.shape, f32)

    x = x_ref[...]
    dt = dt_ref[...]
    tri_b = (lax.broadcasted_iota(jnp.int32, (n_l, n_l), 0) >= lax.broadcasted_iota(jnp.int32, (n_l, n_l), 1))
    tri = tri_b.astype(bf16)
    cum = sum(jnp.dot(tri, part, preferred_element_type=f32) for part in _split3(dt * a_ref[...]))
    cum_t = cum.T
    dt_t = dt.T
    ecum = jnp.exp(cum)
    clast = cum[n_l - 1:n_l, :]
    wt = jnp.exp(clast - cum) * dt
    elast = jnp.exp(clast)
    lane = lax.broadcasted_iota(jnp.int32, (n_l, LANES), 1)
    low = lane < SSD_HEAD_DIM

    def pair(v, h0):
        return jnp.where(low[:v.shape[0]], v[:, h0:h0 + 1], v[:, h0 + 1:h0 + 2])

    tiles = []
    for g in range(SSD_GROUPS):
        bg = b_ref[:, g * SSD_STATE:(g + 1) * SSD_STATE]
        cgb = c_ref[:, g * SSD_STATE:(g + 1) * SSD_STATE].astype(bf16)
        bgt = bg.T.astype(bf16)
        cb = jnp.dot(cgb, bgt, preferred_element_type=f32)
        hg = h_scr[g]
        y_inter = jnp.dot(cgb, hg.astype(bf16), preferred_element_type=f32)
        xw, dec = [], []
        for pr in range(hpg // 2):
            h0 = hpg * g + 2 * pr
            xt = x[:, (h0 // 2) * LANES:(h0 // 2 + 1) * LANES]
            acc = None
            for k in range(2):
                h = h0 + k
                seg = cum[:, h:h + 1] - cum_t[h:h + 1, :]
                w = cb * jnp.exp(jnp.where(tri_b, seg, NEG)) * dt_t[h:h + 1, :]
                xm = jnp.where(low if k == 0 else jnp.logical_not(low), xt, 0.0).astype(bf16)
                part = jnp.dot(w.astype(bf16), xm, preferred_element_type=f32)
                acc = part if acc is None else acc + part
            tiles.append(acc + y_inter[:, pr * LANES:(pr + 1) * LANES] * pair(ecum, h0))
            xw.append((xt * pair(wt, h0)).astype(bf16))
            dec.append(pair(elast, h0))
        h_scr[g] = (hg * jnp.concatenate(dec, axis=1)
                    + jnp.dot(bgt, jnp.concatenate(xw, axis=1), preferred_element_type=f32))
    y = jnp.concatenate(tiles, axis=1) + dskip_ref[...] * x
    zg = z_ref[...]
    v = y * (zg * jax.nn.sigmoid(zg))
    outs = []
    for g in range(SSD_GROUPS):
        vg = v[:, g * gw:(g + 1) * gw]
        outs.append(vg * lax.rsqrt(jnp.mean(vg * vg, axis=-1, keepdims=True) + RMS_EPS))
    y_ref[...] = jnp.concatenate(outs, axis=1) * ng_ref[...]

    @pl.when(j == pl.num_programs(1) - 1)
    def _():
        hout_ref[...] = h_scr[...]


def ssd_prompt(xbc, dt, a, zg, d_skip, norm_g):
    f32 = jnp.float32
    bsz, t, _ = xbc.shape
    hpg = SSD_HEADS // SSD_GROUPS
    gn = SSD_GROUPS * SSD_STATE
    dt_p = jnp.pad(dt, ((0, 0), (0, 0), (0, LANES - SSD_HEADS)))
    a_p = jnp.pad(a.astype(f32), (0, LANES - SSD_HEADS)).reshape(1, LANES)
    dsk = jnp.repeat(d_skip.astype(f32), SSD_HEAD_DIM).reshape(1, SSD_INNER)
    blk = lambda w, c: pl.BlockSpec((None, SSD_CHUNK, w), lambda b, j: (b, j, c))
    full = lambda shape: pl.BlockSpec(shape, lambda b, j: (0,) * len(shape))
    state_spec = pl.BlockSpec((None, SSD_GROUPS, SSD_STATE, hpg * SSD_HEAD_DIM), lambda b, j: (b, 0, 0, 0))
    y, h = pl.pallas_call(
        functools.partial(_ssd_kernel, chunk=SSD_CHUNK),
        grid=(bsz, t // SSD_CHUNK),
        in_specs=[blk(SSD_INNER, 0), blk(gn, SSD_INNER // gn), blk(gn, SSD_INNER // gn + 1), blk(LANES, 0),
                  full((1, LANES)), blk(SSD_INNER, 0), full((1, SSD_INNER)), full((1, SSD_INNER))],
        out_specs=[blk(SSD_INNER, 0), state_spec],
        out_shape=[jax.ShapeDtypeStruct((bsz, t, SSD_INNER), f32),
                   jax.ShapeDtypeStruct((bsz, SSD_GROUPS, SSD_STATE, hpg * SSD_HEAD_DIM), f32)],
        scratch_shapes=[pltpu.VMEM((SSD_GROUPS, SSD_STATE, hpg * SSD_HEAD_DIM), f32)],
        compiler_params=_cparams("parallel", "arbitrary"),
        name="ssd_prompt",
    )(xbc, xbc, xbc, dt_p, a_p, zg, dsk, norm_g.astype(f32).reshape(1, SSD_INNER))
    h = h.reshape(bsz, SSD_GROUPS, SSD_STATE, hpg, SSD_HEAD_DIM).transpose(0, 1, 3, 4, 2)
    return y, h.reshape(bsz, SSD_HEADS, SSD_HEAD_DIM, SSD_STATE)


def layer_norm(x, g, b):
    mu = jnp.mean(x, -1, keepdims=True)
    xc = x - mu
    var = jnp.mean(xc * xc, -1, keepdims=True)
    return xc * lax.rsqrt(var + LN_EPS) * g + b


def rope(x, pos):
    half = ROT_DIM // 2
    inv = ROPE_THETA ** (-jnp.arange(half, dtype=jnp.float32) * 2.0 / ROT_DIM)
    ang = pos.astype(jnp.float32)[:, None] * inv[None, :]
    cos = jnp.cos(ang)[:, None, :]
    sin = jnp.sin(ang)[:, None, :]
    x1 = x[..., :half]
    x2 = x[..., half:ROT_DIM]
    return jnp.concatenate([x1 * cos - x2 * sin, x2 * cos + x1 * sin, x[..., ROT_DIM:]], axis=-1)


def last_rows(x, n):
    t = x.shape[1]
    if t < n:
        x = jnp.pad(x, [(0, 0), (n - t, 0)] + [(0, 0)] * (x.ndim - 2))
    return x[:, x.shape[1] - n:]


def causal_conv(x, buf, w, b):
    t = x.shape[1]
    width = w.shape[0]
    xp = jnp.concatenate([buf, x], axis=1)
    y = b + sum(xp[:, j:j + t] * w[j] for j in range(width))
    return y, xp[:, xp.shape[1] - (width - 1):]


def even_split(z, pos):
    bt, t, _ = z.shape
    o = [0]
    for w in (S5_DIM, NSA_Q, NSA_KV, NSA_KV, NSA_KV, 3 * N_HEADS):
        o.append(o[-1] + w)
    u = z[..., o[0]:o[1]]
    q = rope(z[..., o[1]:o[2]].reshape(bt, t, N_HEADS, HEAD_DIM), pos)

    def kv(a, b):
        r = z[..., a:b].reshape(bt, t, 2, KV_GROUPS, HEAD_DIM)
        return jnp.stack([rope(r[:, :, 0], pos), r[:, :, 1]], axis=2)

    kvc = kv(o[2], o[3])
    kvs = kv(o[3], o[4])
    kvw = kv(o[4], o[5])
    gates = jax.nn.sigmoid(z[..., o[5]:o[6]]).reshape(bt, t, N_HEADS, 3)
    return u, q, kvc, kvs, kvw, gates


def even_prompt_mix(z, s5p, cmpp, w_buf):
    bt, t, _ = z.shape
    pos = jnp.arange(t)
    u, q, kvc, kvs, kvw, gates = even_split(z, pos)
    y_s5, s5_state = s5_scan(u, jnp.zeros((bt, S5_GROUPS, S5_STATE, 2), jnp.float32), s5p, S5_CHUNK)
    bf16 = jnp.bfloat16
    lanes = lambda a: a.reshape(bt, t, KV_GROUPS * HEAD_DIM)
    kc = compress_prompt(lanes(kvc[:, :, 0]), compress_params(cmpp[0], cmpp[1], cmpp[2]))
    vc = compress_prompt(lanes(kvc[:, :, 1]), compress_params(cmpp[3], cmpp[4], cmpp[5]))
    front = lambda a: jnp.pad(lanes(a).astype(bf16), ((0, 0), (WINDOW, 0), (0, 0)))
    y_nsa = nsa_prompt(q.reshape(bt, t, NSA_Q), gates.reshape(bt, t, 3 * N_HEADS), kc, vc,
                       lanes(kvs[:, :, 0]).astype(bf16), lanes(kvs[:, :, 1]).astype(bf16),
                       front(kvw[:, :, 0]), front(kvw[:, :, 1]))
    mix = jnp.concatenate([y_s5, y_nsa], axis=-1)
    new_rows = jnp.concatenate([kvc, kvs], axis=2)
    return mix, s5_state, new_rows, last_rows(kvw, w_buf)


def even_sample_mix(z, s5_h0, pool, page_table, win_buf, s5p, cmpp):
    bt, t, _ = z.shape
    f32 = jnp.float32
    pos = page_table.shape[1] * PAGE_SIZE + jnp.arange(t)
    u, q, kvc, kvs, kvw, gates = even_split(z, pos)
    y_s5, s5_state = s5_scan(u, s5_h0.astype(f32), s5p, t)
    lanes = lambda a: a.reshape(bt, t, KV_GROUPS * HEAD_DIM)
    pool3 = pool.astype(f32).reshape(pool.shape[0], PAGE_SIZE, 4 * KV_GROUPS * HEAD_DIM)
    kc, vc = compress_sample(pool3, page_table, lanes(kvc[:, :, 0]), lanes(kvc[:, :, 1]),
                             compress_sample_params(cmpp[0], cmpp[1], cmpp[2]),
                             compress_sample_params(cmpp[3], cmpp[4], cmpp[5]))
    w_buf = win_buf.shape[1]
    win_f = win_buf.astype(f32)
    y_nsa = nsa_sample(q, gates, kc, vc, pool3, page_table, lanes(kvs[:, :, 0]), lanes(kvs[:, :, 1]),
                       win_f.reshape(bt, w_buf, 2 * KV_GROUPS * HEAD_DIM), lanes(kvw[:, :, 0]), lanes(kvw[:, :, 1]))
    new_rows = jnp.concatenate([kvc, kvs], axis=2)
    win = jnp.concatenate([win_f, kvw], axis=1)
    mix = jnp.concatenate([y_s5, y_nsa], axis=-1)
    return mix, s5_state, new_rows, win[:, t:]


def ssd_scan(x, dt, a, bm, cm, h0, chunk):
    bt, t, nh, p = x.shape
    nch = t // chunk
    r = nh // SSD_GROUPS
    tri = jnp.arange(chunk)[:, None] >= jnp.arange(chunk)[None, :]

    def to_chunks(v):
        return jnp.moveaxis(v.reshape((bt, nch, chunk) + v.shape[2:]), 1, 0)

    def step(h, inp):
        xc, dtc, bc, cc = inp
        cum = jnp.cumsum(dtc * a, axis=1)
        seg = cum[:, :, None, :] - cum[:, None, :, :]
        decay = jnp.exp(jnp.where(tri[None, :, :, None], seg, NEG)).reshape(bt, chunk, chunk, SSD_GROUPS, r)
        cb = jnp.einsum('btgn,bsgn->btsg', cc, bc)
        xg = xc.reshape(bt, chunk, SSD_GROUPS, r, p)
        dg = dtc.reshape(bt, chunk, SSD_GROUPS, r)
        w = cb[..., None] * decay * dg[:, None]
        y_intra = jnp.einsum('btsgr,bsgrp->btgrp', w, xg)
        hg = h.reshape(bt, SSD_GROUPS, r, p, SSD_STATE)
        y_inter = jnp.einsum('btgn,bgrpn->btgrp', cc, hg) * jnp.exp(cum).reshape(bt, chunk, SSD_GROUPS, r)[..., None]
        wt = (jnp.exp(cum[:, -1:, :] - cum) * dtc).reshape(bt, chunk, SSD_GROUPS, r)
        h_new = (hg * jnp.exp(cum[:, -1]).reshape(bt, SSD_GROUPS, r)[..., None, None]
                 + jnp.einsum('bsgr,bsgrp,bsgn->bgrpn', wt, xg, bc))
        return h_new.reshape(bt, nh, p, SSD_STATE), (y_intra + y_inter).reshape(bt, chunk, nh, p)

    h_fin, ys = lax.scan(step, h0, (to_chunks(x), to_chunks(dt), to_chunks(bm), to_chunks(cm)))
    return jnp.moveaxis(ys, 0, 1).reshape(bt, t, nh, p), h_fin


def gated_rmsnorm(y, z, g):
    v = y * jax.nn.silu(z)
    bt, t, _ = v.shape
    vg = v.reshape(bt, t, SSD_GROUPS, SSD_INNER // SSD_GROUPS)
    vg = vg * lax.rsqrt(jnp.mean(vg * vg, -1, keepdims=True) + RMS_EPS)
    return vg.reshape(bt, t, SSD_INNER) * g


def odd_mix(z, sc_buf, conv_buf, h0, chunk, sc_w, sc_b, cv_w, cv_b, dt_bias, a_log, d_skip, norm_g):
    f32 = jnp.float32
    bt, t, _ = z.shape
    o1 = SC_DIM
    o2 = 2 * SC_DIM
    o3 = 3 * SC_DIM
    o4 = o3 + SSD_INNER
    o5 = o4 + SSD_CONV_DIM
    sc_h = z[..., :o1]
    sc_bg = z[..., o1:o2]
    sc_cg = z[..., o2:o3]
    zg = z[..., o3:o4]
    xbc = z[..., o4:o5]
    dt_raw = z[..., o5:]
    conv_sc, new_sc = causal_conv(sc_cg * sc_h, sc_buf.astype(f32), sc_w, sc_b)
    y_sc = sc_bg * conv_sc
    xbc_c, new_conv = causal_conv(xbc, conv_buf.astype(f32), cv_w, cv_b)
    xbc_c = jax.nn.silu(xbc_c)
    gn = SSD_GROUPS * SSD_STATE
    xs = xbc_c[..., :SSD_INNER].reshape(bt, t, SSD_HEADS, SSD_HEAD_DIM)
    bm = xbc_c[..., SSD_INNER:SSD_INNER + gn].reshape(bt, t, SSD_GROUPS, SSD_STATE)
    cm = xbc_c[..., SSD_INNER + gn:].reshape(bt, t, SSD_GROUPS, SSD_STATE)
    dt = jax.nn.softplus((dt_raw + dt_bias).astype(f32))
    a = -jnp.exp(a_log.astype(f32))
    if h0 is None:
        y, h_new = ssd_prompt(xbc_c, dt, a, zg, d_skip, norm_g)
    else:
        y, h_new = ssd_scan(xs, dt, a, bm, cm, h0.astype(f32), chunk)
        y = (y + d_skip[:, None] * xs).reshape(bt, t, SSD_INNER)
        y = gated_rmsnorm(y, zg, norm_g)
    mix = jnp.concatenate([y_sc, y], axis=-1)
    return mix, new_sc, new_conv, h_new


def moe_ffn(x, w_r, b_r, w_gu_bf16, w_down_bf16):
    n, d = x.shape
    logits = jnp.dot(x, w_r, precision=lax.Precision.HIGHEST) + b_r
    top_v, top_i = lax.top_k(logits, TOP_K)
    gate = jax.nn.softmax(top_v, axis=-1)
    flat_e = top_i.reshape(-1)
    order = jnp.argsort(flat_e, stable=True)
    sorted_e = flat_e[order]
    counts = jnp.bincount(flat_e, length=N_EXPERTS)
    padded = ((counts + ROW_TILE - 1) // ROW_TILE) * ROW_TILE
    pad_start = jnp.cumsum(padded) - padded
    start = jnp.cumsum(counts) - counts
    rank = jnp.arange(2 * n) - start[sorted_e]
    dest_sorted = (pad_start[sorted_e] + rank).astype(jnp.int32)
    n_tiles = (2 * n) // ROW_TILE + N_EXPERTS
    rows = n_tiles * ROW_TILE
    row_token = jnp.zeros((rows,), jnp.int32).at[dest_sorted].set((order // TOP_K).astype(jnp.int32))
    dest = jnp.zeros((2 * n,), jnp.int32).at[order].set(dest_sorted)
    tile_end = jnp.cumsum(padded) // ROW_TILE
    tile_expert = jnp.minimum(jnp.searchsorted(tile_end, jnp.arange(n_tiles), side='right'),
                              N_EXPERTS - 1).astype(jnp.int32)
    n_used = tile_end[-1:].astype(jnp.int32)
    xs = jnp.take(x, row_token, axis=0)
    ys = grouped_ffn(xs, w_gu_bf16, w_down_bf16, tile_expert, n_used)
    yk = jnp.take(ys, dest, axis=0).reshape(n, TOP_K, d)
    return jnp.sum(yk * gate[..., None], axis=1)


def kernel(x_prompt, x_sample, state_s5, cache_nsa_kv, state_win_kv, state_sc_conv, state_ssd_conv, state_ssd,
           page_table, ln_g, ln_b, w_in_even, s5_lam_re, s5_lam_im, s5_log_dt, s5_b, s5_c, s5_d, s5_w_glu,
           nsa_wk1, nsa_wk2, nsa_pe_k, nsa_wv1, nsa_wv2, nsa_pe_v, w_out_even, ffn_w_gu, ffn_w_down,
           w_in_odd, sc_conv_w, sc_conv_b, ssd_conv_w, ssd_conv_b, ssd_dt_bias, ssd_a_log, ssd_d, ssd_norm_g,
           w_out_odd, moe_router, moe_router_b, moe_w_gu, moe_w_down):
    f32 = jnp.float32
    bf16 = jnp.bfloat16
    bp, tp, d = x_prompt.shape
    bs, ts, _ = x_sample.shape
    n_p = bp * tp
    n_s = bs * ts
    w_buf = state_win_kv.shape[2]
    h = jnp.concatenate([x_prompt.astype(f32).reshape(n_p, d), x_sample.astype(f32).reshape(n_s, d)], axis=0)
    n = n_p + n_s
    one_tile = jnp.zeros((pl.cdiv(n, ROW_TILE),), jnp.int32)
    all_tiles = jnp.full((1,), pl.cdiv(n, ROW_TILE), jnp.int32)

    s5p = s5_params(s5_lam_re[0], s5_lam_im[0], s5_log_dt[0], s5_b[0], s5_c[0], s5_d[0], s5_w_glu[0])
    cmpp = (nsa_wk1[0], nsa_wk2[0], nsa_pe_k[0], nsa_wv1[0], nsa_wv2[0], nsa_pe_v[0])
    z = matmul(h, w_in_even[0].astype(bf16))
    zp = z[:n_p].reshape(bp, tp, -1)
    zs = z[n_p:].reshape(bs, ts, -1)
    mix_p, s5_p, kv_p, win_p = even_prompt_mix(zp, s5p, cmpp, w_buf)
    mix_s, s5_s, kv_s, win_s = even_sample_mix(zs, state_s5[0], cache_nsa_kv[0], page_table, state_win_kv[0],
                                               s5p, cmpp)
    mix = jnp.concatenate([mix_p.reshape(n_p, -1), mix_s.reshape(n_s, -1)], axis=0)
    m = matmul(mix, w_out_even[0].astype(bf16))
    h = layer_norm(ALPHA * h + m, ln_g[0, 0], ln_b[0, 0])
    f = grouped_ffn(h, ffn_w_gu.astype(bf16), ffn_w_down.astype(bf16), one_tile, all_tiles)
    h = layer_norm(ALPHA * h + f, ln_g[0, 1], ln_b[0, 1])

    oddp = (sc_conv_w[0], sc_conv_b[0], ssd_conv_w[0], ssd_conv_b[0], ssd_dt_bias[0],
            ssd_a_log[0], ssd_d[0], ssd_norm_g[0])
    z = matmul(h, w_in_odd[0].astype(bf16))
    zp = z[:n_p].reshape(bp, tp, -1)
    zs = z[n_p:].reshape(bs, ts, -1)
    mix_p, scc_p, sdc_p, ssd_p = odd_mix(zp, jnp.zeros((bp, SC_WIDTH - 1, SC_DIM), f32),
                                         jnp.zeros((bp, SSD_CONV - 1, SSD_CONV_DIM), f32),
                                         None, SSD_CHUNK, *oddp)
    mix_s, scc_s, sdc_s, ssd_s = odd_mix(zs, state_sc_conv[0], state_ssd_conv[0], state_ssd[0], ts, *oddp)
    mix = jnp.concatenate([mix_p.reshape(n_p, -1), mix_s.reshape(n_s, -1)], axis=0)
    m = matmul(mix, w_out_odd[0].astype(bf16))
    h = layer_norm(ALPHA * h + m, ln_g[1, 0], ln_b[1, 0])
    f = moe_ffn(h, moe_router[0], moe_router_b[0], moe_w_gu[0].astype(bf16), moe_w_down[0].astype(bf16))
    h = layer_norm(ALPHA * h + f, ln_g[1, 1], ln_b[1, 1])

    hp = h[:n_p].reshape(bp, tp, d)
    hs = h[n_p:].reshape(bs, ts, d)
    st = lambda a, ref: a[None].astype(ref.dtype)
    return (hp.astype(x_prompt.dtype), hs.astype(x_sample.dtype),
            st(s5_p, state_s5), st(s5_s, state_s5),
            st(kv_p, cache_nsa_kv), st(kv_s, cache_nsa_kv),
            st(win_p, state_win_kv), st(win_s, state_win_kv),
            st(scc_p, state_sc_conv), st(scc_s, state_sc_conv),
            st(sdc_p, state_ssd_conv), st(sdc_s, state_ssd_conv),
            st(ssd_p, state_ssd), st(ssd_s, state_ssd))
```

```python
import functools
import math

import jax
import jax.numpy as jnp
from jax import lax
from jax.experimental import pallas as pl
from jax.experimental.pallas import tpu as pltpu

D_MODEL = 1024
SEQ = 8192
DEPTH = 2
DEC_SEQ = 8
PAST_LEN = 16384
ALPHA = (2.0 * DEPTH) ** 0.25
LN_EPS = 1e-5
RMS_EPS = 1e-5
NEG = -1e30

S5_DIM = D_MODEL // 2
S5_GROUP = 16
S5_GROUPS = S5_DIM // S5_GROUP
S5_STATE = 64

HEAD_DIM = 64
N_HEADS = (D_MODEL // 2) // HEAD_DIM
KV_GROUPS = 2
HEADS_PER_GROUP = N_HEADS // KV_GROUPS
CMP_STRIDE = 16
CMP_LEN = 2 * CMP_STRIDE
SEL_BLOCK = 64
N_SEL = 16
WINDOW = 512
Q_BLOCK = 128
ROPE_THETA = 500000.0
ROT_DIM = HEAD_DIM // 4
FORCE = 1e4
NSA_Q = N_HEADS * HEAD_DIM
NSA_KV = 2 * KV_GROUPS * HEAD_DIM

SC_DIM = D_MODEL // 2
SC_WIDTH = 3
SSD_HEAD_DIM = 64
SSD_HEADS = 16
SSD_INNER = SSD_HEADS * SSD_HEAD_DIM
SSD_GROUPS = 4
SSD_STATE = 128
SSD_CONV = 4
SSD_CONV_DIM = SSD_INNER + 2 * SSD_GROUPS * SSD_STATE
SSD_CHUNK = 128

D_FF = 2816
N_EXPERTS = 8
TOP_K = 2

VMEM_LIMIT_BYTES = 56 * 1024 * 1024
LANES = 128
S5_N = S5_GROUPS * S5_STATE
S5_LT = S5_N // LANES
S5_CHUNK = 256
SEL_TILE = 1024
QK_SCALE = HEAD_DIM ** -0.5 * math.log2(math.e)
REMOVED = -3e38
PAGE_SIZE = 128
PAGES_PER_STEP = 8
NEW_PAD = 128
CAST_ROWS = 256
ROW_TILE = 512
FF_TILE = D_FF // 2


def _cparams(*sem):
    return pltpu.CompilerParams(dimension_semantics=sem, vmem_limit_bytes=VMEM_LIMIT_BYTES)


def _deepnorm(resid, update, g, b):
    y = ALPHA * resid + update
    mu = jnp.mean(y, axis=-1, keepdims=True)
    yc = y - mu
    var = jnp.mean(yc * yc, axis=-1, keepdims=True)
    return yc * lax.rsqrt(var + LN_EPS) * g + b


def _mm_kernel(*refs, n_in, fuse_ln):
    xs, ws = refs[0:n_in], refs[n_in:2 * n_in]
    o_ref = refs[-1]
    acc = None
    for x_ref, w_ref in zip(xs, ws):
        part = jnp.dot(x_ref[...].astype(jnp.bfloat16), w_ref[...], preferred_element_type=jnp.float32)
        acc = part if acc is None else acc + part
    if fuse_ln:
        r_ref, g_ref, b_ref = refs[2 * n_in:2 * n_in + 3]
        acc = _deepnorm(r_ref[...], acc, g_ref[...], b_ref[...])
    o_ref[...] = acc


def matmul(xs, ws_bf16, ln=None):
    m = xs[0].shape[0]
    n = ws_bf16[0].shape[1]
    tile = min(ROW_TILE, m)
    row = lambda width: pl.BlockSpec((tile, width), lambda i: (i, 0))
    fixed = lambda shape: pl.BlockSpec(shape, lambda i: (0, 0), pipeline_mode=pl.Buffered(1))
    in_specs = [row(x.shape[1]) for x in xs] + [fixed(w.shape) for w in ws_bf16]
    args = list(xs) + list(ws_bf16)
    if ln is not None:
        resid, g, b = ln
        in_specs += [row(n), fixed((1, n)), fixed((1, n))]
        args += [resid, g.reshape(1, n), b.reshape(1, n)]
    return pl.pallas_call(
        functools.partial(_mm_kernel, n_in=len(xs), fuse_ln=ln is not None),
        grid=(pl.cdiv(m, tile),),
        in_specs=in_specs,
        out_specs=row(n),
        out_shape=jax.ShapeDtypeStruct((m, n), jnp.float32),
        compiler_params=_cparams("parallel"),
        name="matmul",
    )(*args)


def _cast_kernel(x_ref, o_ref):
    o_ref[...] = x_ref[...].astype(o_ref.dtype)


def to_bf16(w):
    shape = w.shape
    w2 = w.reshape(-1, shape[-1])
    rows, cols = w2.shape
    out = pl.pallas_call(
        _cast_kernel,
        grid=(pl.cdiv(rows, CAST_ROWS),),
        in_specs=[pl.BlockSpec((CAST_ROWS, cols), lambda i: (i, 0))],
        out_specs=pl.BlockSpec((CAST_ROWS, cols), lambda i: (i, 0)),
        out_shape=jax.ShapeDtypeStruct((rows, cols), jnp.bfloat16),
        compiler_params=_cparams("parallel"),
        name="to_bf16",
    )(w2)
    return out.reshape(shape)


def _ffn_kernel(te_ref, nt_ref, x_ref, wg_ref, wu_ref, wd_ref, *rest, fuse_ln):
    o_ref = rest[-1]
    t = pl.program_id(0)
    j = pl.program_id(1)

    @pl.when(t < nt_ref[0])
    def _():
        x = x_ref[...].astype(jnp.bfloat16)
        g = jnp.dot(x, wg_ref[...], preferred_element_type=jnp.float32)
        u = jnp.dot(x, wu_ref[...], preferred_element_type=jnp.float32)
        h = (g * jax.nn.sigmoid(g) * u).astype(jnp.bfloat16)
        part = jnp.dot(h, wd_ref[...], preferred_element_type=jnp.float32)

        @pl.when(j == 0)
        def _():
            o_ref[...] = part

        @pl.when(j > 0)
        def _():
            if fuse_ln:
                o_ref[...] = _deepnorm(x_ref[...], o_ref[...] + part, rest[0][...], rest[1][...])
            else:
                o_ref[...] += part

    @pl.when(jnp.logical_and(t >= nt_ref[0], j == 0))
    def _():
        o_ref[...] = jnp.zeros_like(o_ref)


def grouped_ffn(x, w_gu_bf16, w_down_bf16, tile_expert, n_tiles_used, ln=None):
    r, d = x.shape
    nf = D_FF // FF_TILE
    assert nf == 2
    tile = min(ROW_TILE, r)
    n_tiles = pl.cdiv(r, tile)
    in_specs = [
        pl.BlockSpec((tile, d), lambda t, j, te, nt: (t, 0)),
        pl.BlockSpec((None, d, FF_TILE), lambda t, j, te, nt: (te[t], 0, j)),
        pl.BlockSpec((None, d, FF_TILE), lambda t, j, te, nt: (te[t], 0, nf + j)),
        pl.BlockSpec((None, FF_TILE, d), lambda t, j, te, nt: (te[t], j, 0)),
    ]
    args = [tile_expert, n_tiles_used, x, w_gu_bf16, w_gu_bf16, w_down_bf16]
    if ln is not None:
        in_specs += [pl.BlockSpec((1, d), lambda t, j, te, nt: (0, 0))] * 2
        args += [ln[0].reshape(1, d), ln[1].reshape(1, d)]
    grid_spec = pltpu.PrefetchScalarGridSpec(
        num_scalar_prefetch=2,
        grid=(n_tiles, nf),
        in_specs=in_specs,
        out_specs=pl.BlockSpec((tile, d), lambda t, j, te, nt: (t, 0)),
    )
    return pl.pallas_call(
        functools.partial(_ffn_kernel, fuse_ln=ln is not None),
        grid_spec=grid_spec,
        out_shape=jax.ShapeDtypeStruct((r, d), jnp.float32),
        compiler_params=_cparams("parallel", "arbitrary"),
        name="grouped_ffn",
    )(*args)


def _s5_kernel(u_ref, h0r_ref, h0i_ref, ar_ref, ai_ref, bbr_ref, bbi_ref, cr_ref, ci_ref, d_ref, wglu_ref,
               y_ref, hro_ref, hio_ref, bur, bui, sr, si, hr, hi, *, chains, chunk):
    j = pl.program_id(0)

    @pl.when(j == 0)
    def _():
        hr[...] = h0r_ref[...]
        hi[...] = h0i_ref[...]

    u = u_ref[...].reshape(chains * chunk, S5_DIM)
    ub = u.astype(jnp.bfloat16)
    bu_r = jnp.dot(ub, bbr_ref[...], preferred_element_type=jnp.float32)
    bu_i = jnp.dot(ub, bbi_ref[...], preferred_element_type=jnp.float32)
    for k in range(S5_LT):
        bur[k] = bu_r[:, k * LANES:(k + 1) * LANES]
        bui[k] = bu_i[:, k * LANES:(k + 1) * LANES]
    ar = [jnp.broadcast_to(ar_ref[:, k * LANES:(k + 1) * LANES], (chains, LANES)) for k in range(S5_LT)]
    ai = [jnp.broadcast_to(ai_ref[:, k * LANES:(k + 1) * LANES], (chains, LANES)) for k in range(S5_LT)]

    def body(t, carry):
        rows = pl.ds(t, chains, stride=chunk)
        out = []
        for k in range(S5_LT):
            xr, xi = carry[2 * k], carry[2 * k + 1]
            nr = ar[k] * xr - ai[k] * xi + bur[k, rows, :]
            ni = ar[k] * xi + ai[k] * xr + bui[k, rows, :]
            sr[k, rows, :] = nr
            si[k, rows, :] = ni
            out += [nr, ni]
        return tuple(out)

    init = []
    for k in range(S5_LT):
        init += [hr[:, k * LANES:(k + 1) * LANES], hi[:, k * LANES:(k + 1) * LANES]]
    fin = lax.fori_loop(0, chunk, body, tuple(init))
    xr = jnp.concatenate(fin[0::2], axis=1)
    xi = jnp.concatenate(fin[1::2], axis=1)
    hr[...] = xr
    hi[...] = xi
    hro_ref[...] = xr
    hio_ref[...] = xi
    s_r = jnp.concatenate([sr[k] for k in range(S5_LT)], axis=1).astype(jnp.bfloat16)
    s_i = jnp.concatenate([si[k] for k in range(S5_LT)], axis=1).astype(jnp.bfloat16)
    y = (jnp.dot(s_r, cr_ref[...], preferred_element_type=jnp.float32)
         - jnp.dot(s_i, ci_ref[...], preferred_element_type=jnp.float32)
         + d_ref[...] * u)
    z = jax.nn.gelu(y)
    gate = jax.nn.sigmoid(jnp.dot(z.astype(jnp.bfloat16), wglu_ref[...], preferred_element_type=jnp.float32))
    y_ref[...] = (z * gate).reshape(chains, chunk, S5_DIM)


def s5_params(lam_re, lam_im, log_dt, b, c, d, w_glu):
    f32 = jnp.float32
    dt = jnp.exp(log_dt.astype(f32))[:, None]
    mag = jnp.exp(lam_re * dt)
    ang = lam_im * dt
    ab_re = mag * jnp.cos(ang)
    ab_im = mag * jnp.sin(ang)
    den = lam_re * lam_re + lam_im * lam_im
    nr = ab_re - 1.0
    coef_re = (nr * lam_re + ab_im * lam_im) / den
    coef_im = (ab_im * lam_re - nr * lam_im) / den
    b_re = b[..., 0].astype(f32)
    b_im = b[..., 1].astype(f32)
    bb_re = coef_re[..., None] * b_re - coef_im[..., None] * b_im
    bb_im = coef_re[..., None] * b_im + coef_im[..., None] * b_re
    eye = jnp.eye(S5_GROUPS, dtype=f32)
    bbr = jnp.einsum('gnk,gh->gkhn', bb_re, eye).reshape(S5_DIM, S5_N).astype(jnp.bfloat16)
    bbi = jnp.einsum('gnk,gh->gkhn', bb_im, eye).reshape(S5_DIM, S5_N).astype(jnp.bfloat16)
    cr = jnp.einsum('gkn,gh->gnhk', c[..., 0].astype(f32), eye).reshape(S5_N, S5_DIM).astype(jnp.bfloat16)
    ci = jnp.einsum('gkn,gh->gnhk', c[..., 1].astype(f32), eye).reshape(S5_N, S5_DIM).astype(jnp.bfloat16)
    return (ab_re.reshape(1, S5_N), ab_im.reshape(1, S5_N), bbr, bbi, cr, ci,
            d.astype(f32).reshape(1, S5_DIM), w_glu.astype(jnp.bfloat16))


def s5_scan(u, h0, params, chunk):
    chains, t, _ = u.shape
    ar, ai, bbr, bbi, cr, ci, d, wglu = params
    h0r = h0[..., 0].reshape(chains, S5_N)
    h0i = h0[..., 1].reshape(chains, S5_N)
    full = lambda shape: pl.BlockSpec(shape, lambda j: (0,) * len(shape))
    rows = chains * chunk
    y, hr, hi = pl.pallas_call(
        functools.partial(_s5_kernel, chains=chains, chunk=chunk),
        grid=(t // chunk,),
        in_specs=[pl.BlockSpec((chains, chunk, S5_DIM), lambda j: (0, j, 0)),
                  full((chains, S5_N)), full((chains, S5_N)), full((1, S5_N)), full((1, S5_N)),
                  full((S5_DIM, S5_N)), full((S5_DIM, S5_N)), full((S5_N, S5_DIM)), full((S5_N, S5_DIM)),
                  full((1, S5_DIM)), full((S5_DIM, S5_DIM))],
        out_specs=[pl.BlockSpec((chains, chunk, S5_DIM), lambda j: (0, j, 0)),
                   full((chains, S5_N)), full((chains, S5_N))],
        out_shape=[jax.ShapeDtypeStruct((chains, t, S5_DIM), jnp.float32),
                   jax.ShapeDtypeStruct((chains, S5_N), jnp.float32),
                   jax.ShapeDtypeStruct((chains, S5_N), jnp.float32)],
        scratch_shapes=[pltpu.VMEM((S5_LT, rows, LANES), jnp.float32)] * 4
                       + [pltpu.VMEM((chains, S5_N), jnp.float32)] * 2,
        compiler_params=_cparams("arbitrary"),
        name="s5_scan",
    )(u, h0r, h0i, ar, ai, bbr, bbi, cr, ci, d, wglu)
    new_state = jnp.stack([hr.reshape(chains, S5_GROUPS, S5_STATE), hi.reshape(chains, S5_GROUPS, S5_STATE)],
                          axis=-1)
    return y, new_state


def _dot_nt(a, b):
    return lax.dot_general(a, b, (((1,), (1,)), ((), ())), preferred_element_type=jnp.float32)


def _split3(x):
    hi = x.astype(jnp.bfloat16)
    rem = x - hi.astype(jnp.float32)
    mid = rem.astype(jnp.bfloat16)
    lo = (rem - mid.astype(jnp.float32)).astype(jnp.bfloat16)
    return hi, mid, lo


def _softmax_rows(s, mask):
    s = jnp.where(mask, s, NEG)
    m = jnp.max(s, axis=-1, keepdims=True)
    p = jnp.exp2(s - m)
    inv = jnp.where(m > 0.5 * NEG, 1.0 / jnp.sum(p, axis=-1, keepdims=True), 0.0)
    return p * inv


def _nsa_prompt_kernel(q_ref, gate_ref, kc_ref, vc_ref, ks_ref, vs_ref, kw_ref, vw_ref, o_ref, *, n_cmp, n_blk):
    f32, bf16 = jnp.float32, jnp.bfloat16
    r4 = HEADS_PER_GROUP
    n_cpad = kc_ref.shape[0]
    start = pl.program_id(1) * Q_BLOCK
    q = q_ref[...] * QK_SCALE
    gate = gate_ref[...]
    lane = lax.broadcasted_iota(jnp.int32, (Q_BLOCK, LANES), 1)
    qpos = start + lax.broadcasted_iota(jnp.int32, (Q_BLOCK, 1), 0)
    n_idx = lax.broadcasted_iota(jnp.int32, (Q_BLOCK, n_cpad), 1)
    cmask = (((n_idx * CMP_STRIDE + (CMP_LEN - 1)) <= qpos) & (n_idx < n_cmp))[None]
    ratio = SEL_BLOCK // CMP_STRIDE
    gsum = (lax.broadcasted_iota(jnp.int32, (n_blk, n_cpad), 1) // ratio
            == lax.broadcasted_iota(jnp.int32, (n_blk, n_cpad), 0)).astype(bf16)
    blk = lax.broadcasted_iota(jnp.int32, (n_blk, Q_BLOCK), 0)
    blk_f = blk.astype(f32)
    jq = (start + lax.broadcasted_iota(jnp.int32, (n_blk, Q_BLOCK), 1)) // SEL_BLOCK
    force = jnp.where((blk == 0) | (blk == jq) | (blk == jq - 1), FORCE, 0.0)
    qgs, o_cs, sels = [], [], []
    for g in range(KV_GROUPS):
        keep = (lane < HEAD_DIM) if g == 0 else (lane >= HEAD_DIM)
        parts = []
        for r in range(r4):
            h = r4 * g + r
            tile = q[:, (h // 2) * LANES:(h // 2 + 1) * LANES]
            if h % 2 != g:
                tile = pltpu.roll(tile, HEAD_DIM, axis=1)
            parts.append(jnp.where(keep, tile, 0.0))
        qg = jnp.concatenate(parts, axis=0).astype(bf16)
        qgs.append(qg)

        p_c = _softmax_rows(_dot_nt(qg, kc_ref[...]).reshape(r4, Q_BLOCK, n_cpad), cmask)
        o_cs.append(jnp.dot(p_c.reshape(r4 * Q_BLOCK, n_cpad).astype(bf16), vc_ref[...],
                            preferred_element_type=f32).reshape(r4, Q_BLOCK, LANES))
        psum = p_c[0] + p_c[1] + p_c[2] + p_c[3]
        imp_t = sum(_dot_nt(gsum, part) for part in _split3(psum))

        score = jnp.where(blk <= jq, imp_t + force, NEG)
        sel_t = jnp.zeros((n_blk, Q_BLOCK), f32)
        for _ in range(min(N_SEL, n_blk)):
            m = jnp.max(score, axis=0, keepdims=True)
            idx = jnp.min(jnp.where(score == m, blk_f, float(n_blk)), axis=0, keepdims=True)
            hit = blk_f == idx
            sel_t = jnp.where(hit & (m > 0.5 * NEG), 1.0, sel_t)
            score = jnp.where(hit, REMOVED, score)
        sels.append(sel_t.T)

    n_full = start // SEL_TILE
    expand0 = (lax.broadcasted_iota(jnp.int32, (n_blk, SEL_TILE), 0)
               == lax.broadcasted_iota(jnp.int32, (n_blk, SEL_TILE), 1) // SEL_BLOCK).astype(bf16)

    def tile_update(i, carry, causal):
        off = pl.multiple_of(i * SEL_TILE, SEL_TILE)
        k = ks_ref[pl.ds(off, SEL_TILE), :]
        v = vs_ref[pl.ds(off, SEL_TILE), :]
        out = []
        for g in range(KV_GROUPS):
            m_run, l_run, acc = carry[g]
            s_t = _dot_nt(qgs[g], k).reshape(r4, Q_BLOCK, SEL_TILE)
            shifted = pltpu.roll(sels[g], (n_blk - i * (SEL_TILE // SEL_BLOCK)) % n_blk, axis=1).astype(bf16)
            mk = jnp.dot(shifted, expand0, preferred_element_type=f32) > 0.5
            if causal:
                kpos = i * SEL_TILE + lax.broadcasted_iota(jnp.int32, (Q_BLOCK, SEL_TILE), 1)
                mk = mk & (kpos <= qpos)
            s_t = jnp.where(mk[None], s_t, NEG)
            m_new = jnp.maximum(m_run, jnp.max(s_t, axis=-1, keepdims=True))
            alpha = jnp.exp2(m_run - m_new)
            p = jnp.exp2(s_t - m_new)
            l_new = alpha * l_run + jnp.sum(p, axis=-1, keepdims=True)
            pv = jnp.dot(p.reshape(r4 * Q_BLOCK, SEL_TILE).astype(bf16), v, preferred_element_type=f32)
            out.append((m_new, l_new, alpha * acc + pv.reshape(r4, Q_BLOCK, LANES)))
        return tuple(out)

    init = (jnp.full((r4, Q_BLOCK, 1), NEG, f32), jnp.zeros((r4, Q_BLOCK, 1), f32),
            jnp.zeros((r4, Q_BLOCK, LANES), f32))
    carry = lax.fori_loop(0, n_full, lambda i, c: tile_update(i, c, False), (init, init))
    fin = tile_update(n_full, carry, True)

    n_win = WINDOW + Q_BLOCK
    woff = pl.multiple_of(start, Q_BLOCK)
    kwin = kw_ref[pl.ds(woff, n_win), :]
    vwin = vw_ref[pl.ds(woff, n_win), :]
    wpos = start - WINDOW + lax.broadcasted_iota(jnp.int32, (Q_BLOCK, n_win), 1)
    wmask = ((wpos <= qpos) & (wpos > qpos - WINDOW) & (wpos >= 0))[None]
    heads = [None] * N_HEADS
    for g in range(KV_GROUPS):
        m_fin, l_fin, acc = fin[g]
        o_s = acc * jnp.where(m_fin > 0.5 * NEG, 1.0 / l_fin, 0.0)
        p_w = _softmax_rows(_dot_nt(qgs[g], kwin).reshape(r4, Q_BLOCK, n_win), wmask)
        o_w = jnp.dot(p_w.reshape(r4 * Q_BLOCK, n_win).astype(bf16), vwin,
                      preferred_element_type=f32).reshape(r4, Q_BLOCK, LANES)
        for r in range(r4):
            h = r4 * g + r
            heads[h] = (gate[:, 3 * h:3 * h + 1] * o_cs[g][r] + gate[:, 3 * h + 1:3 * h + 2] * o_s[r]
                        + gate[:, 3 * h + 2:3 * h + 3] * o_w[r])

    tiles = []
    for j in range(N_HEADS // 2):
        even, odd = heads[2 * j], heads[2 * j + 1]
        if j // 2 == 0:
            tiles.append(jnp.where(lane < HEAD_DIM, even, pltpu.roll(odd, HEAD_DIM, axis=1)))
        else:
            tiles.append(jnp.where(lane < HEAD_DIM, pltpu.roll(even, HEAD_DIM, axis=1), odd))
    o_ref[...] = jnp.concatenate(tiles, axis=1)


def nsa_prompt(q, gates, kc, vc, ks, vs, kw_pad, vw_pad):
    b, t, _ = q.shape
    n_cpad = kc.shape[1]
    kern = functools.partial(_nsa_prompt_kernel, n_cmp=t // CMP_STRIDE - 1, n_blk=t // SEL_BLOCK)
    whole = lambda rows: pl.BlockSpec((None, rows, LANES), lambda i, j: (i, 0, 0))
    return pl.pallas_call(
        kern,
        grid=(b, t // Q_BLOCK),
        in_specs=[pl.BlockSpec((None, Q_BLOCK, NSA_Q), lambda i, j: (i, j, 0)),
                  pl.BlockSpec((None, Q_BLOCK, 3 * N_HEADS), lambda i, j: (i, j, 0)),
                  whole(n_cpad), whole(n_cpad), whole(t), whole(t), whole(t + WINDOW), whole(t + WINDOW)],
        out_specs=pl.BlockSpec((None, Q_BLOCK, NSA_Q), lambda i, j: (i, j, 0)),
        out_shape=jax.ShapeDtypeStruct((b, t, NSA_Q), jnp.float32),
        compiler_params=_cparams("parallel", "arbitrary"),
        name="nsa_prompt",
    )(q, gates, kc, vc, ks, vs, kw_pad, vw_pad)


def _compress_kernel(ch_ref, pet_ref, peb_ref, w1t_ref, w1b_ref, w2_ref, o_ref):
    bf16 = jnp.bfloat16
    ch = ch_ref[...]
    n_ch = ch.shape[0]
    a = jnp.dot((ch + pet_ref[...]).astype(bf16), w1t_ref[...], preferred_element_type=jnp.float32)
    b = jnp.dot((ch + peb_ref[...]).astype(bf16), w1b_ref[...], preferred_element_type=jnp.float32)
    pre = a + pltpu.roll(b, n_ch - 1, axis=0)
    o_ref[...] = jnp.dot(jax.nn.gelu(pre).astype(bf16), w2_ref[...],
                         preferred_element_type=jnp.float32).astype(o_ref.dtype)


def compress_params(w1, w2, pe):
    f32 = jnp.float32
    eye = jnp.eye(KV_GROUPS, dtype=f32)
    w1r = w1.astype(f32).reshape(2, CMP_STRIDE, HEAD_DIM, HEAD_DIM)
    big = jnp.einsum('hjde,gk->hjgdke', w1r, eye).reshape(2, CMP_STRIDE * LANES, LANES).astype(jnp.bfloat16)
    w2bd = jnp.einsum('de,gk->gdke', w2.astype(f32), eye).reshape(LANES, LANES).astype(jnp.bfloat16)
    per = pe.astype(f32).reshape(2, CMP_STRIDE, 1, HEAD_DIM)
    pe_rows = jnp.broadcast_to(per, (2, CMP_STRIDE, KV_GROUPS, HEAD_DIM)).reshape(2, 1, CMP_STRIDE * LANES)
    return pe_rows[0], pe_rows[1], big[0], big[1], w2bd


def compress_prompt(x, params):
    b, t, _ = x.shape
    n_ch = t // CMP_STRIDE
    ch = x.reshape(b, n_ch, CMP_STRIDE * LANES)
    pet, peb, w1t, w1b, w2bd = params
    full = lambda shape: pl.BlockSpec(shape, lambda i: (0,) * len(shape))
    return pl.pallas_call(
        _compress_kernel,
        grid=(b,),
        in_specs=[pl.BlockSpec((None, n_ch, CMP_STRIDE * LANES), lambda i: (i, 0, 0)),
                  full((1, CMP_STRIDE * LANES)), full((1, CMP_STRIDE * LANES)),
                  full((CMP_STRIDE * LANES, LANES)), full((CMP_STRIDE * LANES, LANES)), full((LANES, LANES))],
        out_specs=pl.BlockSpec((None, n_ch, LANES), lambda i: (i, 0, 0)),
        out_shape=jax.ShapeDtypeStruct((b, n_ch, LANES), jnp.bfloat16),
        compiler_params=_cparams("parallel"),
        name="compress_prompt",
    )(ch, pet, peb, w1t, w1b, w2bd)


def _cmp_sample_kernel(pt_ref, *refs, n_pages):
    f32, bf16 = jnp.float32, jnp.bfloat16
    pp = PAGES_PER_STEP
    kpages, vpages = refs[0:pp], refs[pp:2 * pp]
    (newk_ref, newv_ref, wk_ref, wv_ref, ck_ref, cv_ref, w2k_ref, w2v_ref,
     kc_ref, vc_ref, slab_k, slab_v) = refs[2 * pp:]
    s = pl.program_id(1)
    base = pl.multiple_of(s * (pp * PAGE_SIZE), pp * PAGE_SIZE)
    for i in range(pp):
        slab_k[pl.ds(base + i * PAGE_SIZE, PAGE_SIZE), :] = kpages[i][...].T
        slab_v[pl.ds(base + i * PAGE_SIZE, PAGE_SIZE), :] = vpages[i][...].T

    @pl.when(s == pl.num_programs(1) - 1)
    def _():
        n_ch = n_pages * (PAGE_SIZE // CMP_STRIDE)
        row = lax.broadcasted_iota(jnp.int32, (n_ch, LANES), 0)
        for slab, new_ref, w_ref, c_ref, w2_ref, o_ref in ((slab_k, newk_ref, wk_ref, ck_ref, w2k_ref, kc_ref),
                                                           (slab_v, newv_ref, wv_ref, cv_ref, w2v_ref, vc_ref)):
            ch = jnp.concatenate([slab[pl.ds(j, n_ch, stride=CMP_STRIDE), :] for j in range(CMP_STRIDE)],
                                 axis=1).astype(bf16)
            ab = jnp.dot(ch, w_ref[...], preferred_element_type=f32)
            b_new = jnp.dot(new_ref[...].astype(bf16), w_ref[...], preferred_element_type=f32)[0:1, LANES:]
            nxt = pltpu.roll(ab[:, LANES:], n_ch - 1, axis=0)
            nxt = jnp.where(row == n_ch - 1, b_new, nxt)
            pre = ab[:, :LANES] + nxt + c_ref[...]
            o_ref[...] = jnp.dot(jax.nn.gelu(pre).astype(bf16), w2_ref[...],
                                 preferred_element_type=f32).astype(o_ref.dtype)


def compress_sample_params(w1, w2, pe):
    pet, peb, w1t, w1b, w2bd = compress_params(w1, w2, pe)
    hp = lax.Precision.HIGHEST
    const = (jnp.dot(pet, w1t.astype(jnp.float32), precision=hp)
             + jnp.dot(peb, w1b.astype(jnp.float32), precision=hp))
    return jnp.concatenate([w1t, w1b], axis=1), const, w2bd


def _page_spec(i, slot):
    return pl.BlockSpec((None, LANES, PAGE_SIZE),
                        lambda b, s, pt: (pt[b, PAGES_PER_STEP * s + i], slot, 0))


def _per_seq(shape):
    return pl.BlockSpec((None,) + shape, lambda b, s, pt: (b, 0, 0))


def compress_sample(pool_t, page_table, new_k, new_v, pk, pv):
    bsz, n_pages = page_table.shape
    pp = PAGES_PER_STEP
    n_ch = n_pages * (PAGE_SIZE // CMP_STRIDE)
    t_new = new_k.shape[1]

    def chunk_rows(x):
        x = jnp.pad(x, ((0, 0), (0, CMP_STRIDE - t_new), (0, 0))).reshape(bsz, 1, CMP_STRIDE * LANES)
        return jnp.pad(x, ((0, 0), (0, 7), (0, 0)))

    full = lambda shape: pl.BlockSpec(shape, lambda b, s, pt: (0,) * len(shape))
    wk, ck, w2k = pk
    wv, cv, w2v = pv
    grid_spec = pltpu.PrefetchScalarGridSpec(
        num_scalar_prefetch=1,
        grid=(bsz, n_pages // pp),
        in_specs=[_page_spec(i, 0) for i in range(pp)] + [_page_spec(i, 1) for i in range(pp)]
                 + [_per_seq((8, CMP_STRIDE * LANES)), _per_seq((8, CMP_STRIDE * LANES)),
                    full((CMP_STRIDE * LANES, 2 * LANES)), full((CMP_STRIDE * LANES, 2 * LANES)),
                    full((1, LANES)), full((1, LANES)), full((LANES, LANES)), full((LANES, LANES))],
        out_specs=[_per_seq((n_ch, LANES)), _per_seq((n_ch, LANES))],
        scratch_shapes=[pltpu.VMEM((n_pages * PAGE_SIZE, LANES), jnp.float32)] * 2,
    )
    return pl.pallas_call(
        functools.partial(_cmp_sample_kernel, n_pages=n_pages),
        grid_spec=grid_spec,
        out_shape=[jax.ShapeDtypeStruct((bsz, n_ch, LANES), jnp.bfloat16)] * 2,
        compiler_params=_cparams("parallel", "arbitrary"),
        name="compress_sample",
    )(page_table, *([pool_t] * (2 * pp)), chunk_rows(new_k), chunk_rows(new_v), wk, wv, ck, cv, w2k, w2v)


def _nsa_sample_kernel(pt_ref, *refs, n_pages, t_new, w_buf):
    f32, bf16 = jnp.float32, jnp.bfloat16
    pp = PAGES_PER_STEP
    q_ref, gate_ref, kc_ref, vc_ref = refs[0:4]
    kpages, vpages = refs[4:4 + pp], refs[4 + pp:4 + 2 * pp]
    (ksn_ref, vsn_ref, win_ref, kwn_ref, vwn_ref, o_ref,
     sel_scr, oc_scr, m_scr, l_scr, acc_scr) = refs[4 + 2 * pp:]
    r4, g2 = HEADS_PER_GROUP, KV_GROUPS
    n_rows = g2 * r4 * t_new
    past_len = n_pages * PAGE_SIZE
    n_cmp = kc_ref.shape[0]
    n_bpad = sel_scr.shape[1]
    tile = pp * PAGE_SIZE
    s = pl.program_id(1)
    qall = q_ref[...]
    qpos = past_len + lax.broadcasted_iota(jnp.int32, (n_rows, 1), 0) % t_new

    def grouped(x):
        return x.reshape(g2, 1, t_new, x.shape[-1])

    @pl.when(s == 0)
    def _():
        s_c = _dot_nt(qall, kc_ref[...])
        n_idx = lax.broadcasted_iota(jnp.int32, (n_rows, n_cmp), 1)
        p_c = _softmax_rows(s_c, (n_idx * CMP_STRIDE + (CMP_LEN - 1)) <= qpos)
        oc_scr[...] = jnp.dot(p_c.astype(bf16), vc_ref[...], preferred_element_type=f32)
        psum = jnp.sum(p_c.reshape(g2, r4, t_new, n_cmp), axis=1).reshape(g2 * t_new, n_cmp)
        psum = jnp.concatenate([psum, jnp.zeros((LANES - g2 * t_new, n_cmp), f32)], axis=0)
        p_hi = psum.astype(bf16)
        rem = psum - p_hi.astype(f32)
        p_mid = rem.astype(bf16)
        p_lo = (rem - p_mid.astype(f32)).astype(bf16)
        ratio = SEL_BLOCK // CMP_STRIDE
        gsum = (lax.broadcasted_iota(jnp.int32, (n_bpad, n_cmp), 1) // ratio
                == lax.broadcasted_iota(jnp.int32, (n_bpad, n_cmp), 0)).astype(bf16)
        imp_t = _dot_nt(gsum, p_hi) + _dot_nt(gsum, p_mid) + _dot_nt(gsum, p_lo)
        blk = lax.broadcasted_iota(jnp.int32, (n_bpad, LANES), 0)
        jq = (past_len + lax.broadcasted_iota(jnp.int32, (n_bpad, LANES), 1) % t_new) // SEL_BLOCK
        forced = (blk == 0) | (blk == jq) | (blk == jq - 1)
        score = jnp.where(blk <= jq, imp_t + jnp.where(forced, FORCE, 0.0), NEG)
        blk_f = blk.astype(f32)
        sel_t = jnp.zeros((n_bpad, LANES), f32)
        for _ in range(N_SEL):
            m = jnp.max(score, axis=0, keepdims=True)
            idx = jnp.min(jnp.where(score == m, blk_f, float(n_bpad)), axis=0, keepdims=True)
            hit = blk_f == idx
            sel_t = jnp.where(hit & (m > 0.5 * NEG), 1.0, sel_t)
            score = jnp.where(hit, REMOVED, score)
        sel = jnp.concatenate([sel_t[k * LANES:(k + 1) * LANES].T for k in range(n_bpad // LANES)], axis=1)
        sel_scr[...] = sel[0:g2 * t_new].astype(bf16)
        m_scr[...] = jnp.full(m_scr.shape, NEG, f32)
        l_scr[...] = jnp.zeros(l_scr.shape, f32)
        acc_scr[...] = jnp.zeros(acc_scr.shape, f32)

    def online_update(s_t, mk, v, v_feature_major):
        n = s_t.shape[-1]
        s4 = jnp.where(mk, s_t.reshape(g2, r4, t_new, n), NEG)
        m_run = m_scr[...].reshape(g2, r4, t_new, 1)
        m_new = jnp.maximum(m_run, jnp.max(s4, axis=-1, keepdims=True))
        alpha = jnp.exp2(m_run - m_new)
        p = jnp.exp2(s4 - m_new)
        l_new = alpha * l_scr[...].reshape(g2, r4, t_new, 1) + jnp.sum(p, axis=-1, keepdims=True)
        pb = p.reshape(n_rows, n).astype(bf16)
        pv = _dot_nt(pb, v) if v_feature_major else jnp.dot(pb, v, preferred_element_type=f32)
        m_scr[...] = m_new.reshape(n_rows, 1)
        l_scr[...] = l_new.reshape(n_rows, 1)
        acc_scr[...] = alpha.reshape(n_rows, 1) * acc_scr[...] + pv

    kt = jnp.concatenate([r[...] for r in kpages], axis=1).astype(bf16)
    vt = jnp.concatenate([r[...] for r in vpages], axis=1).astype(bf16)
    key = lax.broadcasted_iota(jnp.int32, (n_bpad, tile), 1)
    expand = (lax.broadcasted_iota(jnp.int32, (n_bpad, tile), 0)
              == s * (tile // SEL_BLOCK) + key // SEL_BLOCK).astype(bf16)
    picked = jnp.dot(sel_scr[...], expand, preferred_element_type=f32)
    online_update(jnp.dot(qall, kt, preferred_element_type=f32), grouped(picked) > 0.5, vt, True)

    @pl.when(s == pl.num_programs(1) - 1)
    def _():
        new_blk = past_len // SEL_BLOCK
        kidx = lax.broadcasted_iota(jnp.int32, (n_rows, NEW_PAD), 1)
        causal = ((past_len + kidx) <= qpos) & (kidx < t_new)
        picked_new = sel_scr[:, new_blk:new_blk + 1].astype(f32)
        mk = (grouped(picked_new) > 0.5) & causal.reshape(g2, r4, t_new, NEW_PAD)
        online_update(_dot_nt(qall, ksn_ref[...]), mk, vsn_ref[...], False)
        o_s = acc_scr[...] * jnp.where(m_scr[...] > 0.5 * NEG, 1.0 / l_scr[...], 0.0)

        n_win = w_buf + NEW_PAD
        kw_t = win_ref[0:LANES, :].astype(bf16)
        vw_t = win_ref[LANES:2 * LANES, :].astype(bf16)
        widx = lax.broadcasted_iota(jnp.int32, (n_rows, n_win), 1)
        wpos = past_len - w_buf + widx
        wmask = (wpos <= qpos) & (wpos > qpos - WINDOW) & (wpos >= 0) & (widx < w_buf + t_new)
        s_w = jnp.concatenate([jnp.dot(qall, kw_t, preferred_element_type=f32), _dot_nt(qall, kwn_ref[...])],
                              axis=1)
        p_w = _softmax_rows(s_w, wmask).astype(bf16)
        o_w = (_dot_nt(p_w[:, 0:w_buf], vw_t)
               + jnp.dot(p_w[:, w_buf:], vwn_ref[...], preferred_element_type=f32))
        gate = gate_ref[...]
        o_ref[...] = gate[:, 0:1] * oc_scr[...] + gate[:, 1:2] * o_s + gate[:, 2:3] * o_w


def nsa_sample(q, gates, kc, vc, pool_t, page_table, ks_new, vs_new, win, kw_new, vw_new):
    f32, bf16 = jnp.float32, jnp.bfloat16
    bsz, t_new = q.shape[0], q.shape[1]
    n_pages = page_table.shape[1]
    pp = PAGES_PER_STEP
    w_buf = win.shape[2]
    r4, g2 = HEADS_PER_GROUP, KV_GROUPS
    n_rows = g2 * r4 * t_new
    past_len = n_pages * PAGE_SIZE
    assert past_len % SEL_BLOCK == 0 and t_new <= SEL_BLOCK and past_len >= w_buf and n_pages % pp == 0
    n_sel = past_len // SEL_BLOCK + 1
    n_bpad = -(-n_sel // LANES) * LANES
    eye = jnp.eye(g2, dtype=f32)
    qg = q.reshape(bsz, t_new, g2, r4, HEAD_DIM).transpose(0, 2, 3, 1, 4) * QK_SCALE
    qall = jnp.einsum('bgrqd,gk->bgrqkd', qg, eye).reshape(bsz, n_rows, LANES).astype(bf16)
    gall = gates.reshape(bsz, t_new, g2, r4, 3).transpose(0, 2, 3, 1, 4).reshape(bsz, n_rows, 3)
    pad_rows = lambda x: jnp.pad(x, ((0, 0), (0, NEW_PAD - t_new), (0, 0))).astype(bf16)
    n_cmp = kc.shape[1]
    grid_spec = pltpu.PrefetchScalarGridSpec(
        num_scalar_prefetch=1,
        grid=(bsz, n_pages // pp),
        in_specs=[_per_seq((n_rows, LANES)), _per_seq((n_rows, 3)), _per_seq((n_cmp, LANES)),
                  _per_seq((n_cmp, LANES))]
                 + [_page_spec(i, 2) for i in range(pp)] + [_page_spec(i, 3) for i in range(pp)]
                 + [_per_seq((NEW_PAD, LANES)), _per_seq((NEW_PAD, LANES)), _per_seq((2 * LANES, w_buf)),
                    _per_seq((NEW_PAD, LANES)), _per_seq((NEW_PAD, LANES))],
        out_specs=_per_seq((n_rows, LANES)),
        scratch_shapes=[pltpu.VMEM((g2 * t_new, n_bpad), bf16), pltpu.VMEM((n_rows, LANES), f32),
                        pltpu.VMEM((n_rows, 1), f32), pltpu.VMEM((n_rows, 1), f32),
                        pltpu.VMEM((n_rows, LANES), f32)],
    )
    o = pl.pallas_call(
        functools.partial(_nsa_sample_kernel, n_pages=n_pages, t_new=t_new, w_buf=w_buf),
        grid_spec=grid_spec,
        out_shape=jax.ShapeDtypeStruct((bsz, n_rows, LANES), f32),
        compiler_params=_cparams("parallel", "arbitrary"),
        name="nsa_sample",
    )(page_table, qall, gall, kc, vc, *([pool_t] * (2 * pp)), pad_rows(ks_new), pad_rows(vs_new), win,
      pad_rows(kw_new), pad_rows(vw_new))
    o = jnp.einsum('bgrqkd,gk->bqgrd', o.reshape(bsz, g2, r4, t_new, g2, HEAD_DIM), eye)
    return o.reshape(bsz, t_new, NSA_Q)


def _ssd_kernel(x_ref, b_ref, c_ref, dt_ref, a_ref, z_ref, dskip_ref, ng_ref, y_ref, hout_ref, h_scr, *, chunk):
    f32, bf16 = jnp.float32, jnp.bfloat16
    n_l = chunk
    hpg = SSD_HEADS // SSD_GROUPS
    gw = hpg * SSD_HEAD_DIM
    j = pl.program_id(1)

    @pl.when(j == 0)
    def _():
        h_scr[...] = jnp.zeros(h_scr.shape, f32)

    x = x_ref[...]
    dt = dt_ref[...]
    tri_b = (lax.broadcasted_iota(jnp.int32, (n_l, n_l), 0) >= lax.broadcasted_iota(jnp.int32, (n_l, n_l), 1))
    tri = tri_b.astype(bf16)
    cum = sum(jnp.dot(tri, part, preferred_element_type=f32) for part in _split3(dt * a_ref[...]))
    cum_t = cum.T
    dt_t = dt.T
    ecum = jnp.exp(cum)
    clast = cum[n_l - 1:n_l, :]
    wt = jnp.exp(clast - cum) * dt
    elast = jnp.exp(clast)
    lane = lax.broadcasted_iota(jnp.int32, (n_l, LANES), 1)
    low = lane < SSD_HEAD_DIM

    def pair(v, h0):
        return jnp.where(low[:v.shape[0]], v[:, h0:h0 + 1], v[:, h0 + 1:h0 + 2])

    tiles = []
    for g in range(SSD_GROUPS):
        bg = b_ref[:, g * SSD_STATE:(g + 1) * SSD_STATE]
        cgb = c_ref[:, g * SSD_STATE:(g + 1) * SSD_STATE].astype(bf16)
        bgt = bg.T.astype(bf16)
        cb = jnp.dot(cgb, bgt, preferred_element_type=f32)
        hg = h_scr[g]
        y_inter = jnp.dot(cgb, hg.astype(bf16), preferred_element_type=f32)
        xw, dec = [], []
        for pr in range(hpg // 2):
            h0 = hpg * g + 2 * pr
            xt = x[:, (h0 // 2) * LANES:(h0 // 2 + 1) * LANES]
            acc = None
            for k in range(2):
                h = h0 + k
                seg = cum[:, h:h + 1] - cum_t[h:h + 1, :]
                w = cb * jnp.exp(jnp.where(tri_b, seg, NEG)) * dt_t[h:h + 1, :]
                xm = jnp.where(low if k == 0 else jnp.logical_not(low), xt, 0.0).astype(bf16)
                part = jnp.dot(w.astype(bf16), xm, preferred_element_type=f32)
                acc = part if acc is None else acc + part
            tiles.append(acc + y_inter[:, pr * LANES:(pr + 1) * LANES] * pair(ecum, h0))
            xw.append((xt * pair(wt, h0)).astype(bf16))
            dec.append(pair(elast, h0))
        h_scr[g] = (hg * jnp.concatenate(dec, axis=1)
                    + jnp.dot(bgt, jnp.concatenate(xw, axis=1), preferred_element_type=f32))
    y = jnp.concatenate(tiles, axis=1) + dskip_ref[...] * x
    zg = z_ref[...]
    v = y * (zg * jax.nn.sigmoid(zg))
    outs = []
    for g in range(SSD_GROUPS):
        vg = v[:, g * gw:(g + 1) * gw]
        outs.append(vg * lax.rsqrt(jnp.mean(vg * vg, axis=-1, keepdims=True) + RMS_EPS))
    y_ref[...] = jnp.concatenate(outs, axis=1) * ng_ref[...]

    @pl.when(j == pl.num_programs(1) - 1)
    def _():
        hout_ref[...] = h_scr[...]


def ssd_prompt(xbc, dt, a, zg, d_skip, norm_g):
    f32 = jnp.float32
    bsz, t, _ = xbc.shape
    hpg = SSD_HEADS // SSD_GROUPS
    gn = SSD_GROUPS * SSD_STATE
    dt_p = jnp.pad(dt, ((0, 0), (0, 0), (0, LANES - SSD_HEADS)))
    a_p = jnp.pad(a.astype(f32), (0, LANES - SSD_HEADS)).reshape(1, LANES)
    dsk = jnp.repeat(d_skip.astype(f32), SSD_HEAD_DIM).reshape(1, SSD_INNER)
    blk = lambda w, c: pl.BlockSpec((None, SSD_CHUNK, w), lambda b, j: (b, j, c))
    full = lambda shape: pl.BlockSpec(shape, lambda b, j: (0,) * len(shape))
    state_spec = pl.BlockSpec((None, SSD_GROUPS, SSD_STATE, hpg * SSD_HEAD_DIM), lambda b, j: (b, 0, 0, 0))
    y, h = pl.pallas_call(
        functools.partial(_ssd_kernel, chunk=SSD_CHUNK),
        grid=(bsz, t // SSD_CHUNK),
        in_specs=[blk(SSD_INNER, 0), blk(gn, SSD_INNER // gn), blk(gn, SSD_INNER // gn + 1), blk(LANES, 0),
                  full((1, LANES)), blk(SSD_INNER, 0), full((1, SSD_INNER)), full((1, SSD_INNER))],
        out_specs=[blk(SSD_INNER, 0), state_spec],
        out_shape=[jax.ShapeDtypeStruct((bsz, t, SSD_INNER), f32),
                   jax.ShapeDtypeStruct((bsz, SSD_GROUPS, SSD_STATE, hpg * SSD_HEAD_DIM), f32)],
        scratch_shapes=[pltpu.VMEM((SSD_GROUPS, SSD_STATE, hpg * SSD_HEAD_DIM), f32)],
        compiler_params=_cparams("parallel", "arbitrary"),
        name="ssd_prompt",
    )(xbc, xbc, xbc, dt_p, a_p, zg, dsk, norm_g.astype(f32).reshape(1, SSD_INNER))
    h = h.reshape(bsz, SSD_GROUPS, SSD_STATE, hpg, SSD_HEAD_DIM).transpose(0, 1, 3, 4, 2)
    return y, h.reshape(bsz, SSD_HEADS, SSD_HEAD_DIM, SSD_STATE)


def layer_norm(x, g, b):
    mu = jnp.mean(x, -1, keepdims=True)
    xc = x - mu
    var = jnp.mean(xc * xc, -1, keepdims=True)
    return xc * lax.rsqrt(var + LN_EPS) * g + b


def rope(x, pos):
    half = ROT_DIM // 2
    inv = ROPE_THETA ** (-jnp.arange(half, dtype=jnp.float32) * 2.0 / ROT_DIM)
    ang = pos.astype(jnp.float32)[:, None] * inv[None, :]
    cos = jnp.cos(ang)[:, None, :]
    sin = jnp.sin(ang)[:, None, :]
    x1 = x[..., :half]
    x2 = x[..., half:ROT_DIM]
    return jnp.concatenate([x1 * cos - x2 * sin, x2 * cos + x1 * sin, x[..., ROT_DIM:]], axis=-1)


def last_rows(x, n):
    t = x.shape[1]
    if t < n:
        x = jnp.pad(x, [(0, 0), (n - t, 0)] + [(0, 0)] * (x.ndim - 2))
    return x[:, x.shape[1] - n:]


def causal_conv(x, buf, w, b):
    t = x.shape[1]
    width = w.shape[0]
    xp = jnp.concatenate([buf, x], axis=1)
    y = b + sum(xp[:, j:j + t] * w[j] for j in range(width))
    return y, xp[:, xp.shape[1] - (width - 1):]


def even_split(z, pos):
    bt, t, _ = z.shape
    o = [0]
    for w in (S5_DIM, NSA_Q, NSA_KV, NSA_KV, NSA_KV, 3 * N_HEADS):
        o.append(o[-1] + w)
    u = z[..., o[0]:o[1]]
    q = rope(z[..., o[1]:o[2]].reshape(bt, t, N_HEADS, HEAD_DIM), pos)

    def kv(a, b):
        r = z[..., a:b].reshape(bt, t, 2, KV_GROUPS, HEAD_DIM)
        return jnp.stack([rope(r[:, :, 0], pos), r[:, :, 1]], axis=2)

    kvc = kv(o[2], o[3])
    kvs = kv(o[3], o[4])
    kvw = kv(o[4], o[5])
    gates = jax.nn.sigmoid(z[..., o[5]:o[6]]).reshape(bt, t, N_HEADS, 3)
    return u, q, kvc, kvs, kvw, gates


def even_prompt_mix(z, s5p, cmpp, w_buf):
    bt, t, _ = z.shape
    pos = jnp.arange(t)
    u, q, kvc, kvs, kvw, gates = even_split(z, pos)
    y_s5, s5_state = s5_scan(u, jnp.zeros((bt, S5_GROUPS, S5_STATE, 2), jnp.float32), s5p, S5_CHUNK)
    bf16 = jnp.bfloat16
    lanes = lambda a: a.reshape(bt, t, KV_GROUPS * HEAD_DIM)
    kc = compress_prompt(lanes(kvc[:, :, 0]), compress_params(cmpp[0], cmpp[1], cmpp[2]))
    vc = compress_prompt(lanes(kvc[:, :, 1]), compress_params(cmpp[3], cmpp[4], cmpp[5]))
    front = lambda a: jnp.pad(lanes(a).astype(bf16), ((0, 0), (WINDOW, 0), (0, 0)))
    y_nsa = nsa_prompt(q.reshape(bt, t, NSA_Q), gates.reshape(bt, t, 3 * N_HEADS), kc, vc,
                       lanes(kvs[:, :, 0]).astype(bf16), lanes(kvs[:, :, 1]).astype(bf16),
                       front(kvw[:, :, 0]), front(kvw[:, :, 1]))
    new_rows = jnp.concatenate([kvc, kvs], axis=2)
    return (y_s5, y_nsa), s5_state, new_rows, last_rows(kvw, w_buf)


def even_sample_mix(z, s5_h0, pool, page_table, win_buf, s5p, cmpp):
    bt, t, _ = z.shape
    f32 = jnp.float32
    pos = page_table.shape[1] * PAGE_SIZE + jnp.arange(t)
    u, q, kvc, kvs, kvw, gates = even_split(z, pos)
    y_s5, s5_state = s5_scan(u, s5_h0.astype(f32), s5p, t)
    lanes = lambda a: a.reshape(bt, t, KV_GROUPS * HEAD_DIM)
    feat = KV_GROUPS * HEAD_DIM
    pool_t = pool.astype(f32).transpose(0, 2, 3, 4, 1).reshape(pool.shape[0], 4 * feat, PAGE_SIZE)
    kc, vc = compress_sample(pool_t, page_table, lanes(kvc[:, :, 0]), lanes(kvc[:, :, 1]),
                             compress_sample_params(cmpp[0], cmpp[1], cmpp[2]),
                             compress_sample_params(cmpp[3], cmpp[4], cmpp[5]))
    w_buf = win_buf.shape[1]
    win_f = win_buf.astype(f32)
    y_nsa = nsa_sample(q, gates, kc, vc, pool_t, page_table, lanes(kvs[:, :, 0]), lanes(kvs[:, :, 1]),
                       win_f.transpose(0, 2, 3, 4, 1).reshape(bt, 2 * feat, w_buf), lanes(kvw[:, :, 0]),
                       lanes(kvw[:, :, 1]))
    new_rows = jnp.concatenate([kvc, kvs], axis=2)
    win = jnp.concatenate([win_f, kvw], axis=1)
    return (y_s5, y_nsa), s5_state, new_rows, win[:, t:]


def ssd_scan(x, dt, a, bm, cm, h0, chunk):
    bt, t, nh, p = x.shape
    nch = t // chunk
    r = nh // SSD_GROUPS
    tri = jnp.arange(chunk)[:, None] >= jnp.arange(chunk)[None, :]

    def to_chunks(v):
        return jnp.moveaxis(v.reshape((bt, nch, chunk) + v.shape[2:]), 1, 0)

    def step(h, inp):
        xc, dtc, bc, cc = inp
        cum = jnp.cumsum(dtc * a, axis=1)
        seg = cum[:, :, None, :] - cum[:, None, :, :]
        decay = jnp.exp(jnp.where(tri[None, :, :, None], seg, NEG)).reshape(bt, chunk, chunk, SSD_GROUPS, r)
        cb = jnp.einsum('btgn,bsgn->btsg', cc, bc)
        xg = xc.reshape(bt, chunk, SSD_GROUPS, r, p)
        dg = dtc.reshape(bt, chunk, SSD_GROUPS, r)
        w = cb[..., None] * decay * dg[:, None]
        y_intra = jnp.einsum('btsgr,bsgrp->btgrp', w, xg)
        hg = h.reshape(bt, SSD_GROUPS, r, p, SSD_STATE)
        y_inter = jnp.einsum('btgn,bgrpn->btgrp', cc, hg) * jnp.exp(cum).reshape(bt, chunk, SSD_GROUPS, r)[..., None]
        wt = (jnp.exp(cum[:, -1:, :] - cum) * dtc).reshape(bt, chunk, SSD_GROUPS, r)
        h_new = (hg * jnp.exp(cum[:, -1]).reshape(bt, SSD_GROUPS, r)[..., None, None]
                 + jnp.einsum('bsgr,bsgrp,bsgn->bgrpn', wt, xg, bc))
        return h_new.reshape(bt, nh, p, SSD_STATE), (y_intra + y_inter).reshape(bt, chunk, nh, p)

    h_fin, ys = lax.scan(step, h0, (to_chunks(x), to_chunks(dt), to_chunks(bm), to_chunks(cm)))
    return jnp.moveaxis(ys, 0, 1).reshape(bt, t, nh, p), h_fin


def gated_rmsnorm(y, z, g):
    v = y * jax.nn.silu(z)
    bt, t, _ = v.shape
    vg = v.reshape(bt, t, SSD_GROUPS, SSD_INNER // SSD_GROUPS)
    vg = vg * lax.rsqrt(jnp.mean(vg * vg, -1, keepdims=True) + RMS_EPS)
    return vg.reshape(bt, t, SSD_INNER) * g


def odd_mix(z, sc_buf, conv_buf, h0, chunk, sc_w, sc_b, cv_w, cv_b, dt_bias, a_log, d_skip, norm_g):
    f32 = jnp.float32
    bt, t, _ = z.shape
    o1 = SC_DIM
    o2 = 2 * SC_DIM
    o3 = 3 * SC_DIM
    o4 = o3 + SSD_INNER
    o5 = o4 + SSD_CONV_DIM
    sc_h = z[..., :o1]
    sc_bg = z[..., o1:o2]
    sc_cg = z[..., o2:o3]
    zg = z[..., o3:o4]
    xbc = z[..., o4:o5]
    dt_raw = z[..., o5:]
    conv_sc, new_sc = causal_conv(sc_cg * sc_h, sc_buf.astype(f32), sc_w, sc_b)
    y_sc = sc_bg * conv_sc
    xbc_c, new_conv = causal_conv(xbc, conv_buf.astype(f32), cv_w, cv_b)
    xbc_c = jax.nn.silu(xbc_c)
    gn = SSD_GROUPS * SSD_STATE
    xs = xbc_c[..., :SSD_INNER].reshape(bt, t, SSD_HEADS, SSD_HEAD_DIM)
    bm = xbc_c[..., SSD_INNER:SSD_INNER + gn].reshape(bt, t, SSD_GROUPS, SSD_STATE)
    cm = xbc_c[..., SSD_INNER + gn:].reshape(bt, t, SSD_GROUPS, SSD_STATE)
    dt = jax.nn.softplus((dt_raw + dt_bias).astype(f32))
    a = -jnp.exp(a_log.astype(f32))
    if h0 is None:
        y, h_new = ssd_prompt(xbc_c, dt, a, zg, d_skip, norm_g)
    else:
        y, h_new = ssd_scan(xs, dt, a, bm, cm, h0.astype(f32), chunk)
        y = (y + d_skip[:, None] * xs).reshape(bt, t, SSD_INNER)
        y = gated_rmsnorm(y, zg, norm_g)
    return (y_sc, y), new_sc, new_conv, h_new


def moe_ffn(x, w_r, b_r, w_gu_bf16, w_down_bf16):
    n, d = x.shape
    logits = jnp.dot(x, w_r, precision=lax.Precision.HIGHEST) + b_r
    top_v, top_i = lax.top_k(logits, TOP_K)
    gate = jax.nn.softmax(top_v, axis=-1)
    flat_e = top_i.reshape(-1)
    order = jnp.argsort(flat_e, stable=True)
    sorted_e = flat_e[order]
    counts = jnp.bincount(flat_e, length=N_EXPERTS)
    padded = ((counts + ROW_TILE - 1) // ROW_TILE) * ROW_TILE
    pad_start = jnp.cumsum(padded) - padded
    start = jnp.cumsum(counts) - counts
    rank = jnp.arange(2 * n) - start[sorted_e]
    dest_sorted = (pad_start[sorted_e] + rank).astype(jnp.int32)
    n_tiles = (2 * n) // ROW_TILE + N_EXPERTS
    rows = n_tiles * ROW_TILE
    row_token = jnp.zeros((rows,), jnp.int32).at[dest_sorted].set((order // TOP_K).astype(jnp.int32))
    dest = jnp.zeros((2 * n,), jnp.int32).at[order].set(dest_sorted)
    tile_end = jnp.cumsum(padded) // ROW_TILE
    tile_expert = jnp.minimum(jnp.searchsorted(tile_end, jnp.arange(n_tiles), side='right'),
                              N_EXPERTS - 1).astype(jnp.int32)
    n_used = tile_end[-1:].astype(jnp.int32)
    xs = jnp.take(x, row_token, axis=0)
    ys = grouped_ffn(xs, w_gu_bf16, w_down_bf16, tile_expert, n_used)
    yk = jnp.take(ys, dest, axis=0).reshape(n, TOP_K, d)
    return jnp.sum(yk * gate[..., None], axis=1)


def kernel(x_prompt, x_sample, state_s5, cache_nsa_kv, state_win_kv, state_sc_conv, state_ssd_conv, state_ssd,
           page_table, ln_g, ln_b, w_in_even, s5_lam_re, s5_lam_im, s5_log_dt, s5_b, s5_c, s5_d, s5_w_glu,
           nsa_wk1, nsa_wk2, nsa_pe_k, nsa_wv1, nsa_wv2, nsa_pe_v, w_out_even, ffn_w_gu, ffn_w_down,
           w_in_odd, sc_conv_w, sc_conv_b, ssd_conv_w, ssd_conv_b, ssd_dt_bias, ssd_a_log, ssd_d, ssd_norm_g,
           w_out_odd, moe_router, moe_router_b, moe_w_gu, moe_w_down):
    f32 = jnp.float32
    bf16 = jnp.bfloat16
    bp, tp, d = x_prompt.shape
    bs, ts, _ = x_sample.shape
    n_p = bp * tp
    n_s = bs * ts
    w_buf = state_win_kv.shape[2]
    streams = [x_prompt.astype(f32).reshape(n_p, d), x_sample.astype(f32).reshape(n_s, d)]
    shapes = [(bp, tp), (bs, ts)]

    def flat(parts, n_rows):
        return [p.reshape(n_rows, p.shape[-1]) for p in parts]

    def out_proj(parts, w_out, width, h, g, b):
        w = w_out.astype(bf16)
        return matmul(flat(parts, h.shape[0]), [w[:width], w[width:]], ln=(h, g, b))

    def single_expert(n_rows):
        n_tiles = pl.cdiv(n_rows, min(ROW_TILE, n_rows))
        return jnp.zeros((n_tiles,), jnp.int32), jnp.full((1,), n_tiles, jnp.int32)

    s5p = s5_params(s5_lam_re[0], s5_lam_im[0], s5_log_dt[0], s5_b[0], s5_c[0], s5_d[0], s5_w_glu[0])
    cmpp = (nsa_wk1[0], nsa_wk2[0], nsa_pe_k[0], nsa_wv1[0], nsa_wv2[0], nsa_pe_v[0])
    w_in = w_in_even[0].astype(bf16)
    zp, zs = [matmul([h], [w_in]).reshape(sh + (-1,)) for h, sh in zip(streams, shapes)]
    mix_p, s5_p, kv_p, win_p = even_prompt_mix(zp, s5p, cmpp, w_buf)
    mix_s, s5_s, kv_s, win_s = even_sample_mix(zs, state_s5[0], cache_nsa_kv[0], page_table, state_win_kv[0],
                                               s5p, cmpp)
    streams = [out_proj(mix, w_out_even[0], S5_DIM, h, ln_g[0, 0], ln_b[0, 0])
               for mix, h in zip((mix_p, mix_s), streams)]
    w_gu, w_down = to_bf16(ffn_w_gu), to_bf16(ffn_w_down)
    streams = [grouped_ffn(h, w_gu, w_down, *single_expert(h.shape[0]), ln=(ln_g[0, 1], ln_b[0, 1]))
               for h in streams]

    oddp = (sc_conv_w[0], sc_conv_b[0], ssd_conv_w[0], ssd_conv_b[0], ssd_dt_bias[0],
            ssd_a_log[0], ssd_d[0], ssd_norm_g[0])
    w_in = w_in_odd[0].astype(bf16)
    zp, zs = [matmul([h], [w_in]).reshape(sh + (-1,)) for h, sh in zip(streams, shapes)]
    mix_p, scc_p, sdc_p, ssd_p = odd_mix(zp, jnp.zeros((bp, SC_WIDTH - 1, SC_DIM), f32),
                                         jnp.zeros((bp, SSD_CONV - 1, SSD_CONV_DIM), f32),
                                         None, SSD_CHUNK, *oddp)
    mix_s, scc_s, sdc_s, ssd_s = odd_mix(zs, state_sc_conv[0], state_ssd_conv[0], state_ssd[0], ts, *oddp)
    streams = [out_proj(mix, w_out_odd[0], SC_DIM, h, ln_g[1, 0], ln_b[1, 0])
               for mix, h in zip((mix_p, mix_s), streams)]
    h = jnp.concatenate(streams, axis=0)
    f = moe_ffn(h, moe_router[0], moe_router_b[0], to_bf16(moe_w_gu[0]), to_bf16(moe_w_down[0]))
    h = layer_norm(ALPHA * h + f, ln_g[1, 1], ln_b[1, 1])

    hp = h[:n_p].reshape(bp, tp, d)
    hs = h[n_p:].reshape(bs, ts, d)
    st = lambda a, ref: a[None].astype(ref.dtype)
    return (hp.astype(x_prompt.dtype), hs.astype(x_sample.dtype),
            st(s5_p, state_s5), st(s5_s, state_s5),
            st(kv_p, cache_nsa_kv), st(kv_s, cache_nsa_kv),
            st(win_p, state_win_kv), st(win_s, state_win_kv),
            st(scc_p, state_sc_conv), st(scc_s, state_sc_conv),
            st(sdc_p, state_ssd_conv), st(sdc_s, state_ssd_conv),
            st(ssd_p, state_ssd), st(ssd_s, state_ssd))
```

```python
import functools
import math

import jax
import jax.numpy as jnp
from jax import lax
from jax.experimental import pallas as pl
from jax.experimental.pallas import tpu as pltpu

D_MODEL = 1024
SEQ = 8192
DEPTH = 2
DEC_SEQ = 8
PAST_LEN = 16384
ALPHA = (2.0 * DEPTH) ** 0.25
LN_EPS = 1e-5
RMS_EPS = 1e-5
NEG = -1e30

S5_DIM = D_MODEL // 2
S5_GROUP = 16
S5_GROUPS = S5_DIM // S5_GROUP
S5_STATE = 64

HEAD_DIM = 64
N_HEADS = (D_MODEL // 2) // HEAD_DIM
KV_GROUPS = 2
HEADS_PER_GROUP = N_HEADS // KV_GROUPS
CMP_STRIDE = 16
CMP_LEN = 2 * CMP_STRIDE
SEL_BLOCK = 64
N_SEL = 16
WINDOW = 512
Q_BLOCK = 128
ROPE_THETA = 500000.0
ROT_DIM = HEAD_DIM // 4
FORCE = 1e4
NSA_Q = N_HEADS * HEAD_DIM
NSA_KV = 2 * KV_GROUPS * HEAD_DIM

SC_DIM = D_MODEL // 2
SC_WIDTH = 3
SSD_HEAD_DIM = 64
SSD_HEADS = 16
SSD_INNER = SSD_HEADS * SSD_HEAD_DIM
SSD_GROUPS = 4
SSD_STATE = 128
SSD_CONV = 4
SSD_CONV_DIM = SSD_INNER + 2 * SSD_GROUPS * SSD_STATE
SSD_CHUNK = 128

D_FF = 2816
N_EXPERTS = 8
TOP_K = 2

VMEM_LIMIT_BYTES = 56 * 1024 * 1024
LANES = 128
S5_N = S5_GROUPS * S5_STATE
S5_LT = S5_N // LANES
S5_CHUNK = 256
SEL_TILE = 1024
QK_SCALE = HEAD_DIM ** -0.5 * math.log2(math.e)
REMOVED = -3e38
PAGE_SIZE = 128
PAGES_PER_STEP = 8
NEW_PAD = 128
CAST_ROWS = 256
CAST_SPLIT = 4
ROW_TILE = 512
FF_TILE = D_FF // 2


def _cparams(*sem):
    return pltpu.CompilerParams(dimension_semantics=sem, vmem_limit_bytes=VMEM_LIMIT_BYTES)


def _deepnorm(resid, update, g, b):
    y = ALPHA * resid + update
    mu = jnp.mean(y, axis=-1, keepdims=True)
    yc = y - mu
    var = jnp.mean(yc * yc, axis=-1, keepdims=True)
    return yc * lax.rsqrt(var + LN_EPS) * g + b


def _mm_kernel(*refs, n_in, fuse_ln):
    xs, ws = refs[0:n_in], refs[n_in:2 * n_in]
    o_ref = refs[-1]
    acc = None
    for x_ref, w_ref in zip(xs, ws):
        part = jnp.dot(x_ref[...].astype(jnp.bfloat16), w_ref[...], preferred_element_type=jnp.float32)
        acc = part if acc is None else acc + part
    if fuse_ln:
        r_ref, g_ref, b_ref = refs[2 * n_in:2 * n_in + 3]
        acc = _deepnorm(r_ref[...], acc, g_ref[...], b_ref[...])
    o_ref[...] = acc


def matmul(xs, ws_bf16, ln=None):
    m = xs[0].shape[0]
    n = ws_bf16[0].shape[1]
    tile = min(ROW_TILE, m)
    row = lambda width: pl.BlockSpec((tile, width), lambda i: (i, 0))
    fixed = lambda shape: pl.BlockSpec(shape, lambda i: (0, 0), pipeline_mode=pl.Buffered(1))
    in_specs = [row(x.shape[1]) for x in xs] + [fixed(w.shape) for w in ws_bf16]
    args = list(xs) + list(ws_bf16)
    if ln is not None:
        resid, g, b = ln
        in_specs += [row(n), fixed((1, n)), fixed((1, n))]
        args += [resid, g.reshape(1, n), b.reshape(1, n)]
    return pl.pallas_call(
        functools.partial(_mm_kernel, n_in=len(xs), fuse_ln=ln is not None),
        grid=(pl.cdiv(m, tile),),
        in_specs=in_specs,
        out_specs=row(n),
        out_shape=jax.ShapeDtypeStruct((m, n), jnp.float32),
        compiler_params=_cparams("parallel"),
        name="matmul",
    )(*args)


def _cast_kernel(*refs):
    o_ref = refs[-1]
    o_ref[...] = jnp.concatenate([r[...].astype(o_ref.dtype) for r in refs[:-1]], axis=1)


def to_bf16(w):
    shape = w.shape
    w2 = w.reshape(-1, shape[-1])
    rows, cols = w2.shape
    split = CAST_SPLIT if cols % (CAST_SPLIT * LANES) == 0 else 1
    out = pl.pallas_call(
        _cast_kernel,
        grid=(pl.cdiv(rows, CAST_ROWS),),
        in_specs=[pl.BlockSpec((CAST_ROWS, cols // split), lambda i, c=c: (i, c)) for c in range(split)],
        out_specs=pl.BlockSpec((CAST_ROWS, cols), lambda i: (i, 0)),
        out_shape=jax.ShapeDtypeStruct((rows, cols), jnp.bfloat16),
        compiler_params=_cparams("parallel"),
        name="to_bf16",
    )(*([w2] * split))
    return out.reshape(shape)


def _ffn_kernel(te_ref, nt_ref, x_ref, wg_ref, wu_ref, wd_ref, *rest, fuse_ln):
    o_ref = rest[-1]
    t = pl.program_id(0)
    j = pl.program_id(1)

    @pl.when(t < nt_ref[0])
    def _():
        x = x_ref[...].astype(jnp.bfloat16)
        g = jnp.dot(x, wg_ref[...], preferred_element_type=jnp.float32)
        u = jnp.dot(x, wu_ref[...], preferred_element_type=jnp.float32)
        h = (g * jax.nn.sigmoid(g) * u).astype(jnp.bfloat16)
        part = jnp.dot(h, wd_ref[...], preferred_element_type=jnp.float32)

        @pl.when(j == 0)
        def _():
            o_ref[...] = part

        @pl.when(j > 0)
        def _():
            if fuse_ln:
                o_ref[...] = _deepnorm(x_ref[...], o_ref[...] + part, rest[0][...], rest[1][...])
            else:
                o_ref[...] += part

    @pl.when(jnp.logical_and(t >= nt_ref[0], j == 0))
    def _():
        o_ref[...] = jnp.zeros_like(o_ref)


def grouped_ffn(x, w_gu_bf16, w_down_bf16, tile_expert, n_tiles_used, ln=None):
    r, d = x.shape
    nf = D_FF // FF_TILE
    assert nf == 2
    tile = min(ROW_TILE, r)
    n_tiles = pl.cdiv(r, tile)
    in_specs = [
        pl.BlockSpec((tile, d), lambda t, j, te, nt: (t, 0)),
        pl.BlockSpec((None, d, FF_TILE), lambda t, j, te, nt: (te[t], 0, j)),
        pl.BlockSpec((None, d, FF_TILE), lambda t, j, te, nt: (te[t], 0, nf + j)),
        pl.BlockSpec((None, FF_TILE, d), lambda t, j, te, nt: (te[t], j, 0)),
    ]
    args = [tile_expert, n_tiles_used, x, w_gu_bf16, w_gu_bf16, w_down_bf16]
    if ln is not None:
        in_specs += [pl.BlockSpec((1, d), lambda t, j, te, nt: (0, 0))] * 2
        args += [ln[0].reshape(1, d), ln[1].reshape(1, d)]
    grid_spec = pltpu.PrefetchScalarGridSpec(
        num_scalar_prefetch=2,
        grid=(n_tiles, nf),
        in_specs=in_specs,
        out_specs=pl.BlockSpec((tile, d), lambda t, j, te, nt: (t, 0)),
    )
    return pl.pallas_call(
        functools.partial(_ffn_kernel, fuse_ln=ln is not None),
        grid_spec=grid_spec,
        out_shape=jax.ShapeDtypeStruct((r, d), jnp.float32),
        compiler_params=_cparams("parallel", "arbitrary"),
        name="grouped_ffn",
    )(*args)


def _s5_kernel(u_ref, h0r_ref, h0i_ref, ar_ref, ai_ref, bbr_ref, bbi_ref, cr_ref, ci_ref, d_ref, wglu_ref,
               y_ref, hro_ref, hio_ref, bur, bui, sr, si, hr, hi, *, chains, chunk):
    j = pl.program_id(0)

    @pl.when(j == 0)
    def _():
        hr[...] = h0r_ref[...]
        hi[...] = h0i_ref[...]

    u = u_ref[...].reshape(chains * chunk, S5_DIM)
    ub = u.astype(jnp.bfloat16)
    bu_r = jnp.dot(ub, bbr_ref[...], preferred_element_type=jnp.float32)
    bu_i = jnp.dot(ub, bbi_ref[...], preferred_element_type=jnp.float32)
    for k in range(S5_LT):
        bur[k] = bu_r[:, k * LANES:(k + 1) * LANES]
        bui[k] = bu_i[:, k * LANES:(k + 1) * LANES]
    ar = [jnp.broadcast_to(ar_ref[:, k * LANES:(k + 1) * LANES], (chains, LANES)) for k in range(S5_LT)]
    ai = [jnp.broadcast_to(ai_ref[:, k * LANES:(k + 1) * LANES], (chains, LANES)) for k in range(S5_LT)]

    def body(t, carry):
        rows = pl.ds(t, chains, stride=chunk)
        out = []
        for k in range(S5_LT):
            xr, xi = carry[2 * k], carry[2 * k + 1]
            nr = ar[k] * xr - ai[k] * xi + bur[k, rows, :]
            ni = ar[k] * xi + ai[k] * xr + bui[k, rows, :]
            sr[k, rows, :] = nr
            si[k, rows, :] = ni
            out += [nr, ni]
        return tuple(out)

    init = []
    for k in range(S5_LT):
        init += [hr[:, k * LANES:(k + 1) * LANES], hi[:, k * LANES:(k + 1) * LANES]]
    fin = lax.fori_loop(0, chunk, body, tuple(init))
    xr = jnp.concatenate(fin[0::2], axis=1)
    xi = jnp.concatenate(fin[1::2], axis=1)
    hr[...] = xr
    hi[...] = xi
    hro_ref[...] = xr
    hio_ref[...] = xi
    s_r = jnp.concatenate([sr[k] for k in range(S5_LT)], axis=1).astype(jnp.bfloat16)
    s_i = jnp.concatenate([si[k] for k in range(S5_LT)], axis=1).astype(jnp.bfloat16)
    y = (jnp.dot(s_r, cr_ref[...], preferred_element_type=jnp.float32)
         - jnp.dot(s_i, ci_ref[...], preferred_element_type=jnp.float32)
         + d_ref[...] * u)
    z = jax.nn.gelu(y)
    gate = jax.nn.sigmoid(jnp.dot(z.astype(jnp.bfloat16), wglu_ref[...], preferred_element_type=jnp.float32))
    y_ref[...] = (z * gate).reshape(chains, chunk, S5_DIM)


def s5_params(lam_re, lam_im, log_dt, b, c, d, w_glu):
    f32 = jnp.float32
    dt = jnp.exp(log_dt.astype(f32))[:, None]
    mag = jnp.exp(lam_re * dt)
    ang = lam_im * dt
    ab_re = mag * jnp.cos(ang)
    ab_im = mag * jnp.sin(ang)
    den = lam_re * lam_re + lam_im * lam_im
    nr = ab_re - 1.0
    coef_re = (nr * lam_re + ab_im * lam_im) / den
    coef_im = (ab_im * lam_re - nr * lam_im) / den
    b_re = b[..., 0].astype(f32)
    b_im = b[..., 1].astype(f32)
    bb_re = coef_re[..., None] * b_re - coef_im[..., None] * b_im
    bb_im = coef_re[..., None] * b_im + coef_im[..., None] * b_re
    eye = jnp.eye(S5_GROUPS, dtype=f32)
    bbr = jnp.einsum('gnk,gh->gkhn', bb_re, eye).reshape(S5_DIM, S5_N).astype(jnp.bfloat16)
    bbi = jnp.einsum('gnk,gh->gkhn', bb_im, eye).reshape(S5_DIM, S5_N).astype(jnp.bfloat16)
    cr = jnp.einsum('gkn,gh->gnhk', c[..., 0].astype(f32), eye).reshape(S5_N, S5_DIM).astype(jnp.bfloat16)
    ci = jnp.einsum('gkn,gh->gnhk', c[..., 1].astype(f32), eye).reshape(S5_N, S5_DIM).astype(jnp.bfloat16)
    return (ab_re.reshape(1, S5_N), ab_im.reshape(1, S5_N), bbr, bbi, cr, ci,
            d.astype(f32).reshape(1, S5_DIM), w_glu.astype(jnp.bfloat16))


def s5_scan(u, h0, params, chunk):
    chains, t, _ = u.shape
    ar, ai, bbr, bbi, cr, ci, d, wglu = params
    h0r = h0[..., 0].reshape(chains, S5_N)
    h0i = h0[..., 1].reshape(chains, S5_N)
    full = lambda shape: pl.BlockSpec(shape, lambda j: (0,) * len(shape))
    rows = chains * chunk
    y, hr, hi = pl.pallas_call(
        functools.partial(_s5_kernel, chains=chains, chunk=chunk),
        grid=(t // chunk,),
        in_specs=[pl.BlockSpec((chains, chunk, S5_DIM), lambda j: (0, j, 0)),
                  full((chains, S5_N)), full((chains, S5_N)), full((1, S5_N)), full((1, S5_N)),
                  full((S5_DIM, S5_N)), full((S5_DIM, S5_N)), full((S5_N, S5_DIM)), full((S5_N, S5_DIM)),
                  full((1, S5_DIM)), full((S5_DIM, S5_DIM))],
        out_specs=[pl.BlockSpec((chains, chunk, S5_DIM), lambda j: (0, j, 0)),
                   full((chains, S5_N)), full((chains, S5_N))],
        out_shape=[jax.ShapeDtypeStruct((chains, t, S5_DIM), jnp.float32),
                   jax.ShapeDtypeStruct((chains, S5_N), jnp.float32),
                   jax.ShapeDtypeStruct((chains, S5_N), jnp.float32)],
        scratch_shapes=[pltpu.VMEM((S5_LT, rows, LANES), jnp.float32)] * 4
                       + [pltpu.VMEM((chains, S5_N), jnp.float32)] * 2,
        compiler_params=_cparams("arbitrary"),
        name="s5_scan",
    )(u, h0r, h0i, ar, ai, bbr, bbi, cr, ci, d, wglu)
    new_state = jnp.stack([hr.reshape(chains, S5_GROUPS, S5_STATE), hi.reshape(chains, S5_GROUPS, S5_STATE)],
                          axis=-1)
    return y, new_state


def _dot_nt(a, b):
    return lax.dot_general(a, b, (((1,), (1,)), ((), ())), preferred_element_type=jnp.float32)


def _split3(x):
    hi = x.astype(jnp.bfloat16)
    rem = x - hi.astype(jnp.float32)
    mid = rem.astype(jnp.bfloat16)
    lo = (rem - mid.astype(jnp.float32)).astype(jnp.bfloat16)
    return hi, mid, lo


def _softmax_rows(s, mask):
    s = jnp.where(mask, s, NEG)
    m = jnp.max(s, axis=-1, keepdims=True)
    p = jnp.exp2(s - m)
    inv = jnp.where(m > 0.5 * NEG, 1.0 / jnp.sum(p, axis=-1, keepdims=True), 0.0)
    return p * inv


def _nsa_prompt_kernel(q_ref, gate_ref, kc_ref, vc_ref, ks_ref, vs_ref, kw_ref, vw_ref, o_ref, *, n_cmp, n_blk):
    f32, bf16 = jnp.float32, jnp.bfloat16
    r4 = HEADS_PER_GROUP
    n_cpad = kc_ref.shape[0]
    start = pl.program_id(1) * Q_BLOCK
    q = q_ref[...] * QK_SCALE
    gate = gate_ref[...]
    lane = lax.broadcasted_iota(jnp.int32, (Q_BLOCK, LANES), 1)
    qpos = start + lax.broadcasted_iota(jnp.int32, (Q_BLOCK, 1), 0)
    n_idx = lax.broadcasted_iota(jnp.int32, (Q_BLOCK, n_cpad), 1)
    cmask = (((n_idx * CMP_STRIDE + (CMP_LEN - 1)) <= qpos) & (n_idx < n_cmp))[None]
    ratio = SEL_BLOCK // CMP_STRIDE
    gsum = (lax.broadcasted_iota(jnp.int32, (n_blk, n_cpad), 1) // ratio
            == lax.broadcasted_iota(jnp.int32, (n_blk, n_cpad), 0)).astype(bf16)
    blk = lax.broadcasted_iota(jnp.int32, (n_blk, Q_BLOCK), 0)
    blk_f = blk.astype(f32)
    jq = (start + lax.broadcasted_iota(jnp.int32, (n_blk, Q_BLOCK), 1)) // SEL_BLOCK
    force = jnp.where((blk == 0) | (blk == jq) | (blk == jq - 1), FORCE, 0.0)
    qgs, o_cs, sels = [], [], []
    for g in range(KV_GROUPS):
        keep = (lane < HEAD_DIM) if g == 0 else (lane >= HEAD_DIM)
        parts = []
        for r in range(r4):
            h = r4 * g + r
            tile = q[:, (h // 2) * LANES:(h // 2 + 1) * LANES]
            if h % 2 != g:
                tile = pltpu.roll(tile, HEAD_DIM, axis=1)
            parts.append(jnp.where(keep, tile, 0.0))
        qg = jnp.concatenate(parts, axis=0).astype(bf16)
        qgs.append(qg)

        p_c = _softmax_rows(_dot_nt(qg, kc_ref[...]).reshape(r4, Q_BLOCK, n_cpad), cmask)
        o_cs.append(jnp.dot(p_c.reshape(r4 * Q_BLOCK, n_cpad).astype(bf16), vc_ref[...],
                            preferred_element_type=f32).reshape(r4, Q_BLOCK, LANES))
        psum = p_c[0] + p_c[1] + p_c[2] + p_c[3]
        imp_t = sum(_dot_nt(gsum, part) for part in _split3(psum))

        score = jnp.where(blk <= jq, imp_t + force, NEG)
        sel_t = jnp.zeros((n_blk, Q_BLOCK), f32)
        for _ in range(min(N_SEL, n_blk)):
            m = jnp.max(score, axis=0, keepdims=True)
            idx = jnp.min(jnp.where(score == m, blk_f, float(n_blk)), axis=0, keepdims=True)
            hit = blk_f == idx
            sel_t = jnp.where(hit & (m > 0.5 * NEG), 1.0, sel_t)
            score = jnp.where(hit, REMOVED, score)
        sels.append(sel_t.T)

    n_full = start // SEL_TILE
    expand0 = (lax.broadcasted_iota(jnp.int32, (n_blk, SEL_TILE), 0)
               == lax.broadcasted_iota(jnp.int32, (n_blk, SEL_TILE), 1) // SEL_BLOCK).astype(bf16)

    def tile_update(i, carry, causal):
        off = pl.multiple_of(i * SEL_TILE, SEL_TILE)
        k = ks_ref[pl.ds(off, SEL_TILE), :]
        v = vs_ref[pl.ds(off, SEL_TILE), :]
        out = []
        for g in range(KV_GROUPS):
            m_run, l_run, acc = carry[g]
            s_t = _dot_nt(qgs[g], k).reshape(r4, Q_BLOCK, SEL_TILE)
            shifted = pltpu.roll(sels[g], (n_blk - i * (SEL_TILE // SEL_BLOCK)) % n_blk, axis=1).astype(bf16)
            mk = jnp.dot(shifted, expand0, preferred_element_type=f32) > 0.5
            if causal:
                kpos = i * SEL_TILE + lax.broadcasted_iota(jnp.int32, (Q_BLOCK, SEL_TILE), 1)
                mk = mk & (kpos <= qpos)
            s_t = jnp.where(mk[None], s_t, NEG)
            m_new = jnp.maximum(m_run, jnp.max(s_t, axis=-1, keepdims=True))
            alpha = jnp.exp2(m_run - m_new)
            p = jnp.exp2(s_t - m_new)
            l_new = alpha * l_run + jnp.sum(p, axis=-1, keepdims=True)
            pv = jnp.dot(p.reshape(r4 * Q_BLOCK, SEL_TILE).astype(bf16), v, preferred_element_type=f32)
            out.append((m_new, l_new, alpha * acc + pv.reshape(r4, Q_BLOCK, LANES)))
        return tuple(out)

    init = (jnp.full((r4, Q_BLOCK, 1), NEG, f32), jnp.zeros((r4, Q_BLOCK, 1), f32),
            jnp.zeros((r4, Q_BLOCK, LANES), f32))
    carry = lax.fori_loop(0, n_full, lambda i, c: tile_update(i, c, False), (init, init))
    fin = tile_update(n_full, carry, True)

    n_win = WINDOW + Q_BLOCK
    woff = pl.multiple_of(start, Q_BLOCK)
    kwin = kw_ref[pl.ds(woff, n_win), :]
    vwin = vw_ref[pl.ds(woff, n_win), :]
    wpos = start - WINDOW + lax.broadcasted_iota(jnp.int32, (Q_BLOCK, n_win), 1)
    wmask = ((wpos <= qpos) & (wpos > qpos - WINDOW) & (wpos >= 0))[None]
    heads = [None] * N_HEADS
    for g in range(KV_GROUPS):
        m_fin, l_fin, acc = fin[g]
        o_s = acc * jnp.where(m_fin > 0.5 * NEG, 1.0 / l_fin, 0.0)
        p_w = _softmax_rows(_dot_nt(qgs[g], kwin).reshape(r4, Q_BLOCK, n_win), wmask)
        o_w = jnp.dot(p_w.reshape(r4 * Q_BLOCK, n_win).astype(bf16), vwin,
                      preferred_element_type=f32).reshape(r4, Q_BLOCK, LANES)
        for r in range(r4):
            h = r4 * g + r
            heads[h] = (gate[:, 3 * h:3 * h + 1] * o_cs[g][r] + gate[:, 3 * h + 1:3 * h + 2] * o_s[r]
                        + gate[:, 3 * h + 2:3 * h + 3] * o_w[r])

    tiles = []
    for j in range(N_HEADS // 2):
        even, odd = heads[2 * j], heads[2 * j + 1]
        if j // 2 == 0:
            tiles.append(jnp.where(lane < HEAD_DIM, even, pltpu.roll(odd, HEAD_DIM, axis=1)))
        else:
            tiles.append(jnp.where(lane < HEAD_DIM, pltpu.roll(even, HEAD_DIM, axis=1), odd))
    o_ref[...] = jnp.concatenate(tiles, axis=1)


def nsa_prompt(q, gates, kc, vc, ks, vs, kw_pad, vw_pad):
    b, t, _ = q.shape
    n_cpad = kc.shape[1]
    kern = functools.partial(_nsa_prompt_kernel, n_cmp=t // CMP_STRIDE - 1, n_blk=t // SEL_BLOCK)
    whole = lambda rows: pl.BlockSpec((None, rows, LANES), lambda i, j: (i, 0, 0))
    return pl.pallas_call(
        kern,
        grid=(b, t // Q_BLOCK),
        in_specs=[pl.BlockSpec((None, Q_BLOCK, NSA_Q), lambda i, j: (i, j, 0)),
                  pl.BlockSpec((None, Q_BLOCK, 3 * N_HEADS), lambda i, j: (i, j, 0)),
                  whole(n_cpad), whole(n_cpad), whole(t), whole(t), whole(t + WINDOW), whole(t + WINDOW)],
        out_specs=pl.BlockSpec((None, Q_BLOCK, NSA_Q), lambda i, j: (i, j, 0)),
        out_shape=jax.ShapeDtypeStruct((b, t, NSA_Q), jnp.float32),
        compiler_params=_cparams("parallel", "arbitrary"),
        name="nsa_prompt",
    )(q, gates, kc, vc, ks, vs, kw_pad, vw_pad)


def _compress_kernel(ch_ref, pet_ref, peb_ref, w1t_ref, w1b_ref, w2_ref, o_ref):
    bf16 = jnp.bfloat16
    ch = ch_ref[...]
    n_ch = ch.shape[0]
    a = jnp.dot((ch + pet_ref[...]).astype(bf16), w1t_ref[...], preferred_element_type=jnp.float32)
    b = jnp.dot((ch + peb_ref[...]).astype(bf16), w1b_ref[...], preferred_element_type=jnp.float32)
    pre = a + pltpu.roll(b, n_ch - 1, axis=0)
    o_ref[...] = jnp.dot(jax.nn.gelu(pre).astype(bf16), w2_ref[...],
                         preferred_element_type=jnp.float32).astype(o_ref.dtype)


def compress_params(w1, w2, pe):
    f32 = jnp.float32
    eye = jnp.eye(KV_GROUPS, dtype=f32)
    w1r = w1.astype(f32).reshape(2, CMP_STRIDE, HEAD_DIM, HEAD_DIM)
    big = jnp.einsum('hjde,gk->hjgdke', w1r, eye).reshape(2, CMP_STRIDE * LANES, LANES).astype(jnp.bfloat16)
    w2bd = jnp.einsum('de,gk->gdke', w2.astype(f32), eye).reshape(LANES, LANES).astype(jnp.bfloat16)
    per = pe.astype(f32).reshape(2, CMP_STRIDE, 1, HEAD_DIM)
    pe_rows = jnp.broadcast_to(per, (2, CMP_STRIDE, KV_GROUPS, HEAD_DIM)).reshape(2, 1, CMP_STRIDE * LANES)
    return pe_rows[0], pe_rows[1], big[0], big[1], w2bd


def compress_prompt(x, params):
    b, t, _ = x.shape
    n_ch = t // CMP_STRIDE
    ch = x.reshape(b, n_ch, CMP_STRIDE * LANES)
    pet, peb, w1t, w1b, w2bd = params
    full = lambda shape: pl.BlockSpec(shape, lambda i: (0,) * len(shape))
    return pl.pallas_call(
        _compress_kernel,
        grid=(b,),
        in_specs=[pl.BlockSpec((None, n_ch, CMP_STRIDE * LANES), lambda i: (i, 0, 0)),
                  full((1, CMP_STRIDE * LANES)), full((1, CMP_STRIDE * LANES)),
                  full((CMP_STRIDE * LANES, LANES)), full((CMP_STRIDE * LANES, LANES)), full((LANES, LANES))],
        out_specs=pl.BlockSpec((None, n_ch, LANES), lambda i: (i, 0, 0)),
        out_shape=jax.ShapeDtypeStruct((b, n_ch, LANES), jnp.bfloat16),
        compiler_params=_cparams("parallel"),
        name="compress_prompt",
    )(ch, pet, peb, w1t, w1b, w2bd)


def _cmp_sample_kernel(pt_ref, *refs, n_pages):
    f32, bf16 = jnp.float32, jnp.bfloat16
    pp = PAGES_PER_STEP
    kpages, vpages = refs[0:pp], refs[pp:2 * pp]
    (perm_ref, newk_ref, newv_ref, wk_ref, wv_ref, ck_ref, cv_ref, w2k_ref, w2v_ref,
     kc_ref, vc_ref, slab_k, slab_v) = refs[2 * pp:]
    s = pl.program_id(1)
    cpp = PAGE_SIZE // CMP_STRIDE
    base = pl.multiple_of(s * (pp * cpp), pp * cpp)
    for pages, slab in ((kpages, slab_k), (vpages, slab_v)):
        for i in range(pp):
            rows = _dot_nt(perm_ref[...], pages[i][...].astype(bf16))
            for j in range(CMP_STRIDE):
                slab[j, pl.ds(base + i * cpp, cpp), :] = rows[j * cpp:(j + 1) * cpp, :]

    @pl.when(s == pl.num_programs(1) - 1)
    def _():
        n_ch = n_pages * (PAGE_SIZE // CMP_STRIDE)
        row = lax.broadcasted_iota(jnp.int32, (n_ch, LANES), 0)
        for slab, new_ref, w_ref, c_ref, w2_ref, o_ref in ((slab_k, newk_ref, wk_ref, ck_ref, w2k_ref, kc_ref),
                                                           (slab_v, newv_ref, wv_ref, cv_ref, w2v_ref, vc_ref)):
            ch = jnp.concatenate([slab[j] for j in range(CMP_STRIDE)], axis=1).astype(bf16)
            ab = jnp.dot(ch, w_ref[...], preferred_element_type=f32)
            b_new = jnp.dot(new_ref[...].astype(bf16), w_ref[...], preferred_element_type=f32)[0:1, LANES:]
            nxt = pltpu.roll(ab[:, LANES:], n_ch - 1, axis=0)
            nxt = jnp.where(row == n_ch - 1, b_new, nxt)
            pre = ab[:, :LANES] + nxt + c_ref[...]
            o_ref[...] = jnp.dot(jax.nn.gelu(pre).astype(bf16), w2_ref[...],
                                 preferred_element_type=f32).astype(o_ref.dtype)


def compress_sample_params(w1, w2, pe):
    pet, peb, w1t, w1b, w2bd = compress_params(w1, w2, pe)
    hp = lax.Precision.HIGHEST
    const = (jnp.dot(pet, w1t.astype(jnp.float32), precision=hp)
             + jnp.dot(peb, w1b.astype(jnp.float32), precision=hp))
    return jnp.concatenate([w1t, w1b], axis=1), const, w2bd


def _page_spec(i, slot):
    return pl.BlockSpec((None, LANES, PAGE_SIZE),
                        lambda b, s, pt: (pt[b, PAGES_PER_STEP * s + i], slot, 0))


def _per_seq(shape):
    return pl.BlockSpec((None,) + shape, lambda b, s, pt: (b, 0, 0))


def compress_sample(pool_t, page_table, new_k, new_v, pk, pv):
    bsz, n_pages = page_table.shape
    pp = PAGES_PER_STEP
    n_ch = n_pages * (PAGE_SIZE // CMP_STRIDE)
    t_new = new_k.shape[1]

    def chunk_rows(x):
        x = jnp.pad(x, ((0, 0), (0, CMP_STRIDE - t_new), (0, 0))).reshape(bsz, 1, CMP_STRIDE * LANES)
        return jnp.pad(x, ((0, 0), (0, 7), (0, 0)))

    full = lambda shape: pl.BlockSpec(shape, lambda b, s, pt: (0,) * len(shape))
    r = jnp.arange(PAGE_SIZE)
    cpp = PAGE_SIZE // CMP_STRIDE
    perm = (r[None, :] == (r[:, None] % cpp) * CMP_STRIDE + r[:, None] // cpp).astype(jnp.bfloat16)
    wk, ck, w2k = pk
    wv, cv, w2v = pv
    grid_spec = pltpu.PrefetchScalarGridSpec(
        num_scalar_prefetch=1,
        grid=(bsz, n_pages // pp),
        in_specs=[_page_spec(i, 0) for i in range(pp)] + [_page_spec(i, 1) for i in range(pp)]
                 + [full((PAGE_SIZE, PAGE_SIZE)), _per_seq((8, CMP_STRIDE * LANES)),
                    _per_seq((8, CMP_STRIDE * LANES)),
                    full((CMP_STRIDE * LANES, 2 * LANES)), full((CMP_STRIDE * LANES, 2 * LANES)),
                    full((1, LANES)), full((1, LANES)), full((LANES, LANES)), full((LANES, LANES))],
        out_specs=[_per_seq((n_ch, LANES)), _per_seq((n_ch, LANES))],
        scratch_shapes=[pltpu.VMEM((CMP_STRIDE, n_ch, LANES), jnp.float32)] * 2,
    )
    return pl.pallas_call(
        functools.partial(_cmp_sample_kernel, n_pages=n_pages),
        grid_spec=grid_spec,
        out_shape=[jax.ShapeDtypeStruct((bsz, n_ch, LANES), jnp.bfloat16)] * 2,
        compiler_params=_cparams("parallel", "arbitrary"),
        name="compress_sample",
    )(page_table, *([pool_t] * (2 * pp)), perm, chunk_rows(new_k), chunk_rows(new_v), wk, wv, ck, cv, w2k, w2v)


def _nsa_sample_kernel(pt_ref, *refs, n_pages, t_new, w_buf):
    f32, bf16 = jnp.float32, jnp.bfloat16
    pp = PAGES_PER_STEP
    q_ref, gate_ref, kc_ref, vc_ref = refs[0:4]
    kpages, vpages = refs[4:4 + pp], refs[4 + pp:4 + 2 * pp]
    (ksn_ref, vsn_ref, win_ref, kwn_ref, vwn_ref, o_ref,
     sel_scr, exp_scr, oc_scr, m_scr, l_scr, acc_scr) = refs[4 + 2 * pp:]
    r4, g2 = HEADS_PER_GROUP, KV_GROUPS
    n_rows = g2 * r4 * t_new
    past_len = n_pages * PAGE_SIZE
    n_cmp = kc_ref.shape[0]
    n_bpad = sel_scr.shape[1]
    tile = pp * PAGE_SIZE
    s = pl.program_id(1)
    qall = q_ref[...]
    qpos = past_len + lax.broadcasted_iota(jnp.int32, (n_rows, 1), 0) % t_new

    def grouped(x):
        return x.reshape(g2, 1, t_new, x.shape[-1])

    @pl.when(s == 0)
    def _():
        s_c = _dot_nt(qall, kc_ref[...])
        n_idx = lax.broadcasted_iota(jnp.int32, (n_rows, n_cmp), 1)
        p_c = _softmax_rows(s_c, (n_idx * CMP_STRIDE + (CMP_LEN - 1)) <= qpos)
        oc_scr[...] = jnp.dot(p_c.astype(bf16), vc_ref[...], preferred_element_type=f32)
        psum = jnp.sum(p_c.reshape(g2, r4, t_new, n_cmp), axis=1).reshape(g2 * t_new, n_cmp)
        psum = jnp.concatenate([psum, jnp.zeros((LANES - g2 * t_new, n_cmp), f32)], axis=0)
        p_hi = psum.astype(bf16)
        rem = psum - p_hi.astype(f32)
        p_mid = rem.astype(bf16)
        p_lo = (rem - p_mid.astype(f32)).astype(bf16)
        ratio = SEL_BLOCK // CMP_STRIDE
        gsum = (lax.broadcasted_iota(jnp.int32, (n_bpad, n_cmp), 1) // ratio
                == lax.broadcasted_iota(jnp.int32, (n_bpad, n_cmp), 0)).astype(bf16)
        imp_t = _dot_nt(gsum, p_hi) + _dot_nt(gsum, p_mid) + _dot_nt(gsum, p_lo)
        blk = lax.broadcasted_iota(jnp.int32, (n_bpad, LANES), 0)
        jq = (past_len + lax.broadcasted_iota(jnp.int32, (n_bpad, LANES), 1) % t_new) // SEL_BLOCK
        forced = (blk == 0) | (blk == jq) | (blk == jq - 1)
        score = jnp.where(blk <= jq, imp_t + jnp.where(forced, FORCE, 0.0), NEG)
        blk_f = blk.astype(f32)
        sel_t = jnp.zeros((n_bpad, LANES), f32)
        for _ in range(N_SEL):
            m = jnp.max(score, axis=0, keepdims=True)
            idx = jnp.min(jnp.where(score == m, blk_f, float(n_bpad)), axis=0, keepdims=True)
            hit = blk_f == idx
            sel_t = jnp.where(hit & (m > 0.5 * NEG), 1.0, sel_t)
            score = jnp.where(hit, REMOVED, score)
        sel = jnp.concatenate([sel_t[k * LANES:(k + 1) * LANES].T for k in range(n_bpad // LANES)], axis=1)
        sel_scr[...] = sel[0:g2 * t_new]
        exp_scr[...] = (lax.broadcasted_iota(jnp.int32, (LANES, tile), 0)
                        == lax.broadcasted_iota(jnp.int32, (LANES, tile), 1) // SEL_BLOCK).astype(bf16)
        m_scr[...] = jnp.full(m_scr.shape, NEG, f32)
        l_scr[...] = jnp.zeros(l_scr.shape, f32)
        acc_scr[...] = jnp.zeros(acc_scr.shape, f32)

    def online_update(s_t, mk, v, v_feature_major):
        n = s_t.shape[-1]
        s4 = jnp.where(mk, s_t.reshape(g2, r4, t_new, n), NEG)
        m_run = m_scr[...].reshape(g2, r4, t_new, 1)
        m_new = jnp.maximum(m_run, jnp.max(s4, axis=-1, keepdims=True))
        alpha = jnp.exp2(m_run - m_new)
        p = jnp.exp2(s4 - m_new)
        l_new = alpha * l_scr[...].reshape(g2, r4, t_new, 1) + jnp.sum(p, axis=-1, keepdims=True)
        pb = p.reshape(n_rows, n).astype(bf16)
        pv = _dot_nt(pb, v) if v_feature_major else jnp.dot(pb, v, preferred_element_type=f32)
        m_scr[...] = m_new.reshape(n_rows, 1)
        l_scr[...] = l_new.reshape(n_rows, 1)
        acc_scr[...] = alpha.reshape(n_rows, 1) * acc_scr[...] + pv

    kt = jnp.concatenate([r[...] for r in kpages], axis=1).astype(bf16)
    vt = jnp.concatenate([r[...] for r in vpages], axis=1).astype(bf16)
    shifted = pltpu.roll(sel_scr[...], (n_bpad - s * (tile // SEL_BLOCK)) % n_bpad, axis=1)
    picked = jnp.dot(shifted[:, 0:LANES].astype(bf16), exp_scr[...], preferred_element_type=f32)
    online_update(jnp.dot(qall, kt, preferred_element_type=f32), grouped(picked) > 0.5, vt, True)

    @pl.when(s == pl.num_programs(1) - 1)
    def _():
        new_blk = past_len // SEL_BLOCK
        kidx = lax.broadcasted_iota(jnp.int32, (n_rows, NEW_PAD), 1)
        causal = ((past_len + kidx) <= qpos) & (kidx < t_new)
        picked_new = sel_scr[:, new_blk:new_blk + 1]
        mk = (grouped(picked_new) > 0.5) & causal.reshape(g2, r4, t_new, NEW_PAD)
        online_update(_dot_nt(qall, ksn_ref[...]), mk, vsn_ref[...], False)
        o_s = acc_scr[...] * jnp.where(m_scr[...] > 0.5 * NEG, 1.0 / l_scr[...], 0.0)

        n_win = w_buf + NEW_PAD
        kw_t = win_ref[0:LANES, :].astype(bf16)
        vw_t = win_ref[LANES:2 * LANES, :].astype(bf16)
        widx = lax.broadcasted_iota(jnp.int32, (n_rows, n_win), 1)
        wpos = past_len - w_buf + widx
        wmask = (wpos <= qpos) & (wpos > qpos - WINDOW) & (wpos >= 0) & (widx < w_buf + t_new)
        s_w = jnp.concatenate([jnp.dot(qall, kw_t, preferred_element_type=f32), _dot_nt(qall, kwn_ref[...])],
                              axis=1)
        p_w = _softmax_rows(s_w, wmask).astype(bf16)
        o_w = (_dot_nt(p_w[:, 0:w_buf], vw_t)
               + jnp.dot(p_w[:, w_buf:], vwn_ref[...], preferred_element_type=f32))
        gate = gate_ref[...]
        o_ref[...] = gate[:, 0:1] * oc_scr[...] + gate[:, 1:2] * o_s + gate[:, 2:3] * o_w


def nsa_sample(q, gates, kc, vc, pool_t, page_table, ks_new, vs_new, win, kw_new, vw_new):
    f32, bf16 = jnp.float32, jnp.bfloat16
    bsz, t_new = q.shape[0], q.shape[1]
    n_pages = page_table.shape[1]
    pp = PAGES_PER_STEP
    w_buf = win.shape[2]
    r4, g2 = HEADS_PER_GROUP, KV_GROUPS
    n_rows = g2 * r4 * t_new
    past_len = n_pages * PAGE_SIZE
    assert past_len % SEL_BLOCK == 0 and t_new <= SEL_BLOCK and past_len >= w_buf and n_pages % pp == 0
    n_sel = past_len // SEL_BLOCK + 1
    n_bpad = -(-n_sel // LANES) * LANES
    eye = jnp.eye(g2, dtype=f32)
    qg = q.reshape(bsz, t_new, g2, r4, HEAD_DIM).transpose(0, 2, 3, 1, 4) * QK_SCALE
    qall = jnp.einsum('bgrqd,gk->bgrqkd', qg, eye).reshape(bsz, n_rows, LANES).astype(bf16)
    gall = gates.reshape(bsz, t_new, g2, r4, 3).transpose(0, 2, 3, 1, 4).reshape(bsz, n_rows, 3)
    pad_rows = lambda x: jnp.pad(x, ((0, 0), (0, NEW_PAD - t_new), (0, 0))).astype(bf16)
    n_cmp = kc.shape[1]
    grid_spec = pltpu.PrefetchScalarGridSpec(
        num_scalar_prefetch=1,
        grid=(bsz, n_pages // pp),
        in_specs=[_per_seq((n_rows, LANES)), _per_seq((n_rows, 3)), _per_seq((n_cmp, LANES)),
                  _per_seq((n_cmp, LANES))]
                 + [_page_spec(i, 2) for i in range(pp)] + [_page_spec(i, 3) for i in range(pp)]
                 + [_per_seq((NEW_PAD, LANES)), _per_seq((NEW_PAD, LANES)), _per_seq((2 * LANES, w_buf)),
                    _per_seq((NEW_PAD, LANES)), _per_seq((NEW_PAD, LANES))],
        out_specs=_per_seq((n_rows, LANES)),
        scratch_shapes=[pltpu.VMEM((g2 * t_new, n_bpad), f32), pltpu.VMEM((LANES, pp * PAGE_SIZE), bf16),
                        pltpu.VMEM((n_rows, LANES), f32),
                        pltpu.VMEM((n_rows, 1), f32), pltpu.VMEM((n_rows, 1), f32),
                        pltpu.VMEM((n_rows, LANES), f32)],
    )
    o = pl.pallas_call(
        functools.partial(_nsa_sample_kernel, n_pages=n_pages, t_new=t_new, w_buf=w_buf),
        grid_spec=grid_spec,
        out_shape=jax.ShapeDtypeStruct((bsz, n_rows, LANES), f32),
        compiler_params=_cparams("parallel", "arbitrary"),
        name="nsa_sample",
    )(page_table, qall, gall, kc, vc, *([pool_t] * (2 * pp)), pad_rows(ks_new), pad_rows(vs_new), win,
      pad_rows(kw_new), pad_rows(vw_new))
    o = jnp.einsum('bgrqkd,gk->bqgrd', o.reshape(bsz, g2, r4, t_new, g2, HEAD_DIM), eye)
    return o.reshape(bsz, t_new, NSA_Q)


def _ssd_kernel(x_ref, b_ref, c_ref, dt_ref, a_ref, z_ref, dskip_ref, ng_ref, y_ref, hout_ref, h_scr, *, chunk):
    f32, bf16 = jnp.float32, jnp.bfloat16
    n_l = chunk
    hpg = SSD_HEADS // SSD_GROUPS
    gw = hpg * SSD_HEAD_DIM
    j = pl.program_id(1)

    @pl.when(j == 0)
    def _():
        h_scr[...] = jnp.zeros(h_scr.shape, f32)

    x = x_ref[...]
    dt = dt_ref[...]
    tri_b = (lax.broadcasted_iota(jnp.int32, (n_l, n_l), 0) >= lax.broadcasted_iota(jnp.int32, (n_l, n_l), 1))
    tri = tri_b.astype(bf16)
    cum = sum(jnp.dot(tri, part, preferred_element_type=f32) for part in _split3(dt * a_ref[...]))
    cum_t = cum.T
    dt_t = dt.T
    ecum = jnp.exp(cum)
    clast = cum[n_l - 1:n_l, :]
    wt = jnp.exp(clast - cum) * dt
    elast = jnp.exp(clast)
    lane = lax.broadcasted_iota(jnp.int32, (n_l, LANES), 1)
    low = lane < SSD_HEAD_DIM

    def pair(v, h0):
        return jnp.where(low[:v.shape[0]], v[:, h0:h0 + 1], v[:, h0 + 1:h0 + 2])

    tiles = []
    for g in range(SSD_GROUPS):
        bg = b_ref[:, g * SSD_STATE:(g + 1) * SSD_STATE]
        cgb = c_ref[:, g * SSD_STATE:(g + 1) * SSD_STATE].astype(bf16)
        bgt = bg.T.astype(bf16)
        cb = jnp.dot(cgb, bgt, preferred_element_type=f32)
        hg = h_scr[g]
        y_inter = jnp.dot(cgb, hg.astype(bf16), preferred_element_type=f32)
        xw, dec = [], []
        for pr in range(hpg // 2):
            h0 = hpg * g + 2 * pr
            xt = x[:, (h0 // 2) * LANES:(h0 // 2 + 1) * LANES]
            acc = None
            for k in range(2):
                h = h0 + k
                seg = cum[:, h:h + 1] - cum_t[h:h + 1, :]
                w = cb * jnp.exp(jnp.where(tri_b, seg, NEG)) * dt_t[h:h + 1, :]
                xm = jnp.where(low if k == 0 else jnp.logical_not(low), xt, 0.0).astype(bf16)
                part = jnp.dot(w.astype(bf16), xm, preferred_element_type=f32)
                acc = part if acc is None else acc + part
            tiles.append(acc + y_inter[:, pr * LANES:(pr + 1) * LANES] * pair(ecum, h0))
            xw.append((xt * pair(wt, h0)).astype(bf16))
            dec.append(pair(elast, h0))
        h_scr[g] = (hg * jnp.concatenate(dec, axis=1)
                    + jnp.dot(bgt, jnp.concatenate(xw, axis=1), preferred_element_type=f32))
    y = jnp.concatenate(tiles, axis=1) + dskip_ref[...] * x
    zg = z_ref[...]
    v = y * (zg * jax.nn.sigmoid(zg))
    outs = []
    for g in range(SSD_GROUPS):
        vg = v[:, g * gw:(g + 1) * gw]
        outs.append(vg * lax.rsqrt(jnp.mean(vg * vg, axis=-1, keepdims=True) + RMS_EPS))
    y_ref[...] = jnp.concatenate(outs, axis=1) * ng_ref[...]

    @pl.when(j == pl.num_programs(1) - 1)
    def _():
        hout_ref[...] = h_scr[...]


def ssd_prompt(xbc, dt, a, zg, d_skip, norm_g):
    f32 = jnp.float32
    bsz, t, _ = xbc.shape
    hpg = SSD_HEADS // SSD_GROUPS
    gn = SSD_GROUPS * SSD_STATE
    dt_p = jnp.pad(dt, ((0, 0), (0, 0), (0, LANES - SSD_HEADS)))
    a_p = jnp.pad(a.astype(f32), (0, LANES - SSD_HEADS)).reshape(1, LANES)
    dsk = jnp.repeat(d_skip.astype(f32), SSD_HEAD_DIM).reshape(1, SSD_INNER)
    blk = lambda w, c: pl.BlockSpec((None, SSD_CHUNK, w), lambda b, j: (b, j, c))
    full = lambda shape: pl.BlockSpec(shape, lambda b, j: (0,) * len(shape))
    state_spec = pl.BlockSpec((None, SSD_GROUPS, SSD_STATE, hpg * SSD_HEAD_DIM), lambda b, j: (b, 0, 0, 0))
    y, h = pl.pallas_call(
        functools.partial(_ssd_kernel, chunk=SSD_CHUNK),
        grid=(bsz, t // SSD_CHUNK),
        in_specs=[blk(SSD_INNER, 0), blk(gn, SSD_INNER // gn), blk(gn, SSD_INNER // gn + 1), blk(LANES, 0),
                  full((1, LANES)), blk(SSD_INNER, 0), full((1, SSD_INNER)), full((1, SSD_INNER))],
        out_specs=[blk(SSD_INNER, 0), state_spec],
        out_shape=[jax.ShapeDtypeStruct((bsz, t, SSD_INNER), f32),
                   jax.ShapeDtypeStruct((bsz, SSD_GROUPS, SSD_STATE, hpg * SSD_HEAD_DIM), f32)],
        scratch_shapes=[pltpu.VMEM((SSD_GROUPS, SSD_STATE, hpg * SSD_HEAD_DIM), f32)],
        compiler_params=_cparams("parallel", "arbitrary"),
        name="ssd_prompt",
    )(xbc, xbc, xbc, dt_p, a_p, zg, dsk, norm_g.astype(f32).reshape(1, SSD_INNER))
    h = h.reshape(bsz, SSD_GROUPS, SSD_STATE, hpg, SSD_HEAD_DIM).transpose(0, 1, 3, 4, 2)
    return y, h.reshape(bsz, SSD_HEADS, SSD_HEAD_DIM, SSD_STATE)


def layer_norm(x, g, b):
    mu = jnp.mean(x, -1, keepdims=True)
    xc = x - mu
    var = jnp.mean(xc * xc, -1, keepdims=True)
    return xc * lax.rsqrt(var + LN_EPS) * g + b


def rope(x, pos):
    half = ROT_DIM // 2
    inv = ROPE_THETA ** (-jnp.arange(half, dtype=jnp.float32) * 2.0 / ROT_DIM)
    ang = pos.astype(jnp.float32)[:, None] * inv[None, :]
    cos = jnp.cos(ang)[:, None, :]
    sin = jnp.sin(ang)[:, None, :]
    x1 = x[..., :half]
    x2 = x[..., half:ROT_DIM]
    return jnp.concatenate([x1 * cos - x2 * sin, x2 * cos + x1 * sin, x[..., ROT_DIM:]], axis=-1)


def last_rows(x, n):
    t = x.shape[1]
    if t < n:
        x = jnp.pad(x, [(0, 0), (n - t, 0)] + [(0, 0)] * (x.ndim - 2))
    return x[:, x.shape[1] - n:]


def causal_conv(x, buf, w, b):
    t = x.shape[1]
    width = w.shape[0]
    xp = jnp.concatenate([buf, x], axis=1)
    y = b + sum(xp[:, j:j + t] * w[j] for j in range(width))
    return y, xp[:, xp.shape[1] - (width - 1):]


def even_split(z, pos):
    bt, t, _ = z.shape
    o = [0]
    for w in (S5_DIM, NSA_Q, NSA_KV, NSA_KV, NSA_KV, 3 * N_HEADS):
        o.append(o[-1] + w)
    u = z[..., o[0]:o[1]]
    q = rope(z[..., o[1]:o[2]].reshape(bt, t, N_HEADS, HEAD_DIM), pos)

    def kv(a, b):
        r = z[..., a:b].reshape(bt, t, 2, KV_GROUPS, HEAD_DIM)
        return jnp.stack([rope(r[:, :, 0], pos), r[:, :, 1]], axis=2)

    kvc = kv(o[2], o[3])
    kvs = kv(o[3], o[4])
    kvw = kv(o[4], o[5])
    gates = jax.nn.sigmoid(z[..., o[5]:o[6]]).reshape(bt, t, N_HEADS, 3)
    return u, q, kvc, kvs, kvw, gates


def even_prompt_mix(z, s5p, cmpp, w_buf):
    bt, t, _ = z.shape
    pos = jnp.arange(t)
    u, q, kvc, kvs, kvw, gates = even_split(z, pos)
    y_s5, s5_state = s5_scan(u, jnp.zeros((bt, S5_GROUPS, S5_STATE, 2), jnp.float32), s5p, S5_CHUNK)
    bf16 = jnp.bfloat16
    lanes = lambda a: a.reshape(bt, t, KV_GROUPS * HEAD_DIM)
    kc = compress_prompt(lanes(kvc[:, :, 0]), compress_params(cmpp[0], cmpp[1], cmpp[2]))
    vc = compress_prompt(lanes(kvc[:, :, 1]), compress_params(cmpp[3], cmpp[4], cmpp[5]))
    front = lambda a: jnp.pad(lanes(a).astype(bf16), ((0, 0), (WINDOW, 0), (0, 0)))
    y_nsa = nsa_prompt(q.reshape(bt, t, NSA_Q), gates.reshape(bt, t, 3 * N_HEADS), kc, vc,
                       lanes(kvs[:, :, 0]).astype(bf16), lanes(kvs[:, :, 1]).astype(bf16),
                       front(kvw[:, :, 0]), front(kvw[:, :, 1]))
    new_rows = jnp.concatenate([kvc, kvs], axis=2)
    return (y_s5, y_nsa), s5_state, new_rows, last_rows(kvw, w_buf)


def even_sample_mix(z, s5_h0, pool, page_table, win_buf, s5p, cmpp):
    bt, t, _ = z.shape
    f32 = jnp.float32
    pos = page_table.shape[1] * PAGE_SIZE + jnp.arange(t)
    u, q, kvc, kvs, kvw, gates = even_split(z, pos)
    y_s5, s5_state = s5_scan(u, s5_h0.astype(f32), s5p, t)
    lanes = lambda a: a.reshape(bt, t, KV_GROUPS * HEAD_DIM)
    feat = KV_GROUPS * HEAD_DIM
    pool_t = pool.astype(f32).transpose(0, 2, 3, 4, 1).reshape(pool.shape[0], 4 * feat, PAGE_SIZE)
    kc, vc = compress_sample(pool_t, page_table, lanes(kvc[:, :, 0]), lanes(kvc[:, :, 1]),
                             compress_sample_params(cmpp[0], cmpp[1], cmpp[2]),
                             compress_sample_params(cmpp[3], cmpp[4], cmpp[5]))
    w_buf = win_buf.shape[1]
    win_f = win_buf.astype(f32)
    y_nsa = nsa_sample(q, gates, kc, vc, pool_t, page_table, lanes(kvs[:, :, 0]), lanes(kvs[:, :, 1]),
                       win_f.transpose(0, 2, 3, 4, 1).reshape(bt, 2 * feat, w_buf), lanes(kvw[:, :, 0]),
                       lanes(kvw[:, :, 1]))
    new_rows = jnp.concatenate([kvc, kvs], axis=2)
    win = jnp.concatenate([win_f, kvw], axis=1)
    return (y_s5, y_nsa), s5_state, new_rows, win[:, t:]


def ssd_scan(x, dt, a, bm, cm, h0, chunk):
    bt, t, nh, p = x.shape
    nch = t // chunk
    r = nh // SSD_GROUPS
    tri = jnp.arange(chunk)[:, None] >= jnp.arange(chunk)[None, :]

    def to_chunks(v):
        return jnp.moveaxis(v.reshape((bt, nch, chunk) + v.shape[2:]), 1, 0)

    def step(h, inp):
        xc, dtc, bc, cc = inp
        cum = jnp.cumsum(dtc * a, axis=1)
        seg = cum[:, :, None, :] - cum[:, None, :, :]
        decay = jnp.exp(jnp.where(tri[None, :, :, None], seg, NEG)).reshape(bt, chunk, chunk, SSD_GROUPS, r)
        cb = jnp.einsum('btgn,bsgn->btsg', cc, bc)
        xg = xc.reshape(bt, chunk, SSD_GROUPS, r, p)
        dg = dtc.reshape(bt, chunk, SSD_GROUPS, r)
        w = cb[..., None] * decay * dg[:, None]
        y_intra = jnp.einsum('btsgr,bsgrp->btgrp', w, xg)
        hg = h.reshape(bt, SSD_GROUPS, r, p, SSD_STATE)
        y_inter = jnp.einsum('btgn,bgrpn->btgrp', cc, hg) * jnp.exp(cum).reshape(bt, chunk, SSD_GROUPS, r)[..., None]
        wt = (jnp.exp(cum[:, -1:, :] - cum) * dtc).reshape(bt, chunk, SSD_GROUPS, r)
        h_new = (hg * jnp.exp(cum[:, -1]).reshape(bt, SSD_GROUPS, r)[..., None, None]
                 + jnp.einsum('bsgr,bsgrp,bsgn->bgrpn', wt, xg, bc))
        return h_new.reshape(bt, nh, p, SSD_STATE), (y_intra + y_inter).reshape(bt, chunk, nh, p)

    h_fin, ys = lax.scan(step, h0, (to_chunks(x), to_chunks(dt), to_chunks(bm), to_chunks(cm)))
    return jnp.moveaxis(ys, 0, 1).reshape(bt, t, nh, p), h_fin


def gated_rmsnorm(y, z, g):
    v = y * jax.nn.silu(z)
    bt, t, _ = v.shape
    vg = v.reshape(bt, t, SSD_GROUPS, SSD_INNER // SSD_GROUPS)
    vg = vg * lax.rsqrt(jnp.mean(vg * vg, -1, keepdims=True) + RMS_EPS)
    return vg.reshape(bt, t, SSD_INNER) * g


def odd_mix(z, sc_buf, conv_buf, h0, chunk, sc_w, sc_b, cv_w, cv_b, dt_bias, a_log, d_skip, norm_g):
    f32 = jnp.float32
    bt, t, _ = z.shape
    o1 = SC_DIM
    o2 = 2 * SC_DIM
    o3 = 3 * SC_DIM
    o4 = o3 + SSD_INNER
    o5 = o4 + SSD_CONV_DIM
    sc_h = z[..., :o1]
    sc_bg = z[..., o1:o2]
    sc_cg = z[..., o2:o3]
    zg = z[..., o3:o4]
    xbc = z[..., o4:o5]
    dt_raw = z[..., o5:]
    conv_sc, new_sc = causal_conv(sc_cg * sc_h, sc_buf.astype(f32), sc_w, sc_b)
    y_sc = sc_bg * conv_sc
    xbc_c, new_conv = causal_conv(xbc, conv_buf.astype(f32), cv_w, cv_b)
    xbc_c = jax.nn.silu(xbc_c)
    gn = SSD_GROUPS * SSD_STATE
    xs = xbc_c[..., :SSD_INNER].reshape(bt, t, SSD_HEADS, SSD_HEAD_DIM)
    bm = xbc_c[..., SSD_INNER:SSD_INNER + gn].reshape(bt, t, SSD_GROUPS, SSD_STATE)
    cm = xbc_c[..., SSD_INNER + gn:].reshape(bt, t, SSD_GROUPS, SSD_STATE)
    dt = jax.nn.softplus((dt_raw + dt_bias).astype(f32))
    a = -jnp.exp(a_log.astype(f32))
    if h0 is None:
        y, h_new = ssd_prompt(xbc_c, dt, a, zg, d_skip, norm_g)
    else:
        y, h_new = ssd_scan(xs, dt, a, bm, cm, h0.astype(f32), chunk)
        y = (y + d_skip[:, None] * xs).reshape(bt, t, SSD_INNER)
        y = gated_rmsnorm(y, zg, norm_g)
    return (y_sc, y), new_sc, new_conv, h_new


def moe_ffn(x, w_r, b_r, w_gu_bf16, w_down_bf16):
    n, d = x.shape
    logits = jnp.dot(x, w_r, precision=lax.Precision.HIGHEST) + b_r
    top_v, top_i = lax.top_k(logits, TOP_K)
    gate = jax.nn.softmax(top_v, axis=-1)
    flat_e = top_i.reshape(-1)
    order = jnp.argsort(flat_e, stable=True)
    sorted_e = flat_e[order]
    counts = jnp.bincount(flat_e, length=N_EXPERTS)
    padded = ((counts + ROW_TILE - 1) // ROW_TILE) * ROW_TILE
    pad_start = jnp.cumsum(padded) - padded
    start = jnp.cumsum(counts) - counts
    rank = jnp.arange(2 * n) - start[sorted_e]
    dest_sorted = (pad_start[sorted_e] + rank).astype(jnp.int32)
    n_tiles = (2 * n) // ROW_TILE + N_EXPERTS
    rows = n_tiles * ROW_TILE
    row_token = jnp.zeros((rows,), jnp.int32).at[dest_sorted].set((order // TOP_K).astype(jnp.int32))
    dest = jnp.zeros((2 * n,), jnp.int32).at[order].set(dest_sorted)
    tile_end = jnp.cumsum(padded) // ROW_TILE
    tile_expert = jnp.minimum(jnp.searchsorted(tile_end, jnp.arange(n_tiles), side='right'),
                              N_EXPERTS - 1).astype(jnp.int32)
    n_used = tile_end[-1:].astype(jnp.int32)
    xs = jnp.take(x, row_token, axis=0)
    ys = grouped_ffn(xs, w_gu_bf16, w_down_bf16, tile_expert, n_used)
    yk = jnp.take(ys, dest, axis=0).reshape(n, TOP_K, d)
    return jnp.sum(yk * gate[..., None], axis=1)


def kernel(x_prompt, x_sample, state_s5, cache_nsa_kv, state_win_kv, state_sc_conv, state_ssd_conv, state_ssd,
           page_table, ln_g, ln_b, w_in_even, s5_lam_re, s5_lam_im, s5_log_dt, s5_b, s5_c, s5_d, s5_w_glu,
           nsa_wk1, nsa_wk2, nsa_pe_k, nsa_wv1, nsa_wv2, nsa_pe_v, w_out_even, ffn_w_gu, ffn_w_down,
           w_in_odd, sc_conv_w, sc_conv_b, ssd_conv_w, ssd_conv_b, ssd_dt_bias, ssd_a_log, ssd_d, ssd_norm_g,
           w_out_odd, moe_router, moe_router_b, moe_w_gu, moe_w_down):
    f32 = jnp.float32
    bf16 = jnp.bfloat16
    bp, tp, d = x_prompt.shape
    bs, ts, _ = x_sample.shape
    n_p = bp * tp
    n_s = bs * ts
    w_buf = state_win_kv.shape[2]
    streams = [x_prompt.astype(f32).reshape(n_p, d), x_sample.astype(f32).reshape(n_s, d)]
    shapes = [(bp, tp), (bs, ts)]

    def flat(parts, n_rows):
        return [p.reshape(n_rows, p.shape[-1]) for p in parts]

    def out_proj(parts, w_out, width, h, g, b):
        w = w_out.astype(bf16)
        return matmul(flat(parts, h.shape[0]), [w[:width], w[width:]], ln=(h, g, b))

    def single_expert(n_rows):
        n_tiles = pl.cdiv(n_rows, min(ROW_TILE, n_rows))
        return jnp.zeros((n_tiles,), jnp.int32), jnp.full((1,), n_tiles, jnp.int32)

    s5p = s5_params(s5_lam_re[0], s5_lam_im[0], s5_log_dt[0], s5_b[0], s5_c[0], s5_d[0], s5_w_glu[0])
    cmpp = (nsa_wk1[0], nsa_wk2[0], nsa_pe_k[0], nsa_wv1[0], nsa_wv2[0], nsa_pe_v[0])
    w_in = w_in_even[0].astype(bf16)
    zp, zs = [matmul([h], [w_in]).reshape(sh + (-1,)) for h, sh in zip(streams, shapes)]
    mix_p, s5_p, kv_p, win_p = even_prompt_mix(zp, s5p, cmpp, w_buf)
    mix_s, s5_s, kv_s, win_s = even_sample_mix(zs, state_s5[0], cache_nsa_kv[0], page_table, state_win_kv[0],
                                               s5p, cmpp)
    streams = [out_proj(mix, w_out_even[0], S5_DIM, h, ln_g[0, 0], ln_b[0, 0])
               for mix, h in zip((mix_p, mix_s), streams)]
    w_gu, w_down = to_bf16(ffn_w_gu), to_bf16(ffn_w_down)
    streams = [grouped_ffn(h, w_gu, w_down, *single_expert(h.shape[0]), ln=(ln_g[0, 1], ln_b[0, 1]))
               for h in streams]

    oddp = (sc_conv_w[0], sc_conv_b[0], ssd_conv_w[0], ssd_conv_b[0], ssd_dt_bias[0],
            ssd_a_log[0], ssd_d[0], ssd_norm_g[0])
    w_in = w_in_odd[0].astype(bf16)
    zp, zs = [matmul([h], [w_in]).reshape(sh + (-1,)) for h, sh in zip(streams, shapes)]
    mix_p, scc_p, sdc_p, ssd_p = odd_mix(zp, jnp.zeros((bp, SC_WIDTH - 1, SC_DIM), f32),
                                         jnp.zeros((bp, SSD_CONV - 1, SSD_CONV_DIM), f32),
                                         None, SSD_CHUNK, *oddp)
    mix_s, scc_s, sdc_s, ssd_s = odd_mix(zs, state_sc_conv[0], state_ssd_conv[0], state_ssd[0], ts, *oddp)
    streams = [out_proj(mix, w_out_odd[0], SC_DIM, h, ln_g[1, 0], ln_b[1, 0])
               for mix, h in zip((mix_p, mix_s), streams)]
    h = jnp.concatenate(streams, axis=0)
    f = moe_ffn(h, moe_router[0], moe_router_b[0], to_bf16(moe_w_gu[0]), to_bf16(moe_w_down[0]))
    h = layer_norm(ALPHA * h + f, ln_g[1, 1], ln_b[1, 1])

    hp = h[:n_p].reshape(bp, tp, d)
    hs = h[n_p:].reshape(bs, ts, d)
    st = lambda a, ref: a[None].astype(ref.dtype)
    return (hp.astype(x_prompt.dtype), hs.astype(x_sample.dtype),
            st(s5_p, state_s5), st(s5_s, state_s5),
            st(kv_p, cache_nsa_kv), st(kv_s, cache_nsa_kv),
            st(win_p, state_win_kv), st(win_s, state_win_kv),
            st(scc_p, state_sc_conv), st(scc_s, state_sc_conv),
            st(sdc_p, state_ssd_conv), st(sdc_s, state_ssd_conv),
            st(ssd_p, state_ssd), st(ssd_s, state_ssd))
```

```python
import functools
import math

import jax
import jax.numpy as jnp
from jax import lax
from jax.experimental import pallas as pl
from jax.experimental.pallas import tpu as pltpu

D_MODEL = 1024
SEQ = 8192
DEPTH = 2
DEC_SEQ = 8
PAST_LEN = 16384
ALPHA = (2.0 * DEPTH) ** 0.25
LN_EPS = 1e-5
RMS_EPS = 1e-5
NEG = -1e30

S5_DIM = D_MODEL // 2
S5_GROUP = 16
S5_GROUPS = S5_DIM // S5_GROUP
S5_STATE = 64

HEAD_DIM = 64
N_HEADS = (D_MODEL // 2) // HEAD_DIM
KV_GROUPS = 2
HEADS_PER_GROUP = N_HEADS // KV_GROUPS
CMP_STRIDE = 16
CMP_LEN = 2 * CMP_STRIDE
SEL_BLOCK = 64
N_SEL = 16
WINDOW = 512
Q_BLOCK = 128
ROPE_THETA = 500000.0
ROT_DIM = HEAD_DIM // 4
FORCE = 1e4
NSA_Q = N_HEADS * HEAD_DIM
NSA_KV = 2 * KV_GROUPS * HEAD_DIM

SC_DIM = D_MODEL // 2
SC_WIDTH = 3
SSD_HEAD_DIM = 64
SSD_HEADS = 16
SSD_INNER = SSD_HEADS * SSD_HEAD_DIM
SSD_GROUPS = 4
SSD_STATE = 128
SSD_CONV = 4
SSD_CONV_DIM = SSD_INNER + 2 * SSD_GROUPS * SSD_STATE
SSD_CHUNK = 128

D_FF = 2816
N_EXPERTS = 8
TOP_K = 2

VMEM_LIMIT_BYTES = 56 * 1024 * 1024
LANES = 128
S5_N = S5_GROUPS * S5_STATE
S5_LT = S5_N // LANES
S5_CHUNK = 256
SEL_TILE = 1024
QK_SCALE = HEAD_DIM ** -0.5 * math.log2(math.e)
REMOVED = -3e38
PAGE_SIZE = 128
PAGES_PER_STEP = 16
NEW_PAD = 128
CAST_ROWS = 256
CAST_SPLIT = 4
ROW_TILE = 512
FF_TILE = D_FF // 2


def _cparams(*sem):
    return pltpu.CompilerParams(dimension_semantics=sem, vmem_limit_bytes=VMEM_LIMIT_BYTES)


def _deepnorm(resid, update, g, b):
    y = ALPHA * resid + update
    mu = jnp.mean(y, axis=-1, keepdims=True)
    yc = y - mu
    var = jnp.mean(yc * yc, axis=-1, keepdims=True)
    return yc * lax.rsqrt(var + LN_EPS) * g + b


def _mm_kernel(*refs, n_in, fuse_ln):
    xs, ws = refs[0:n_in], refs[n_in:2 * n_in]
    o_ref = refs[-1]
    acc = None
    for x_ref, w_ref in zip(xs, ws):
        part = jnp.dot(x_ref[...].astype(jnp.bfloat16), w_ref[...], preferred_element_type=jnp.float32)
        acc = part if acc is None else acc + part
    if fuse_ln:
        r_ref, g_ref, b_ref = refs[2 * n_in:2 * n_in + 3]
        acc = _deepnorm(r_ref[...], acc, g_ref[...], b_ref[...])
    o_ref[...] = acc


def matmul(xs, ws_bf16, ln=None):
    m = xs[0].shape[0]
    n = ws_bf16[0].shape[1]
    tile = min(ROW_TILE, m)
    row = lambda width: pl.BlockSpec((tile, width), lambda i: (i, 0))
    fixed = lambda shape: pl.BlockSpec(shape, lambda i: (0, 0), pipeline_mode=pl.Buffered(1))
    in_specs = [row(x.shape[1]) for x in xs] + [fixed(w.shape) for w in ws_bf16]
    args = list(xs) + list(ws_bf16)
    if ln is not None:
        resid, g, b = ln
        in_specs += [row(n), fixed((1, n)), fixed((1, n))]
        args += [resid, g.reshape(1, n), b.reshape(1, n)]
    return pl.pallas_call(
        functools.partial(_mm_kernel, n_in=len(xs), fuse_ln=ln is not None),
        grid=(pl.cdiv(m, tile),),
        in_specs=in_specs,
        out_specs=row(n),
        out_shape=jax.ShapeDtypeStruct((m, n), jnp.float32),
        compiler_params=_cparams("parallel"),
        name="matmul",
    )(*args)


def _cast_kernel(*refs):
    o_ref = refs[-1]
    o_ref[...] = jnp.concatenate([r[...].astype(o_ref.dtype) for r in refs[:-1]], axis=1)


def to_bf16(w):
    shape = w.shape
    w2 = w.reshape(-1, shape[-1])
    rows, cols = w2.shape
    split = CAST_SPLIT if cols % (CAST_SPLIT * LANES) == 0 else 1
    out = pl.pallas_call(
        _cast_kernel,
        grid=(pl.cdiv(rows, CAST_ROWS),),
        in_specs=[pl.BlockSpec((CAST_ROWS, cols // split), lambda i, c=c: (i, c)) for c in range(split)],
        out_specs=pl.BlockSpec((CAST_ROWS, cols), lambda i: (i, 0)),
        out_shape=jax.ShapeDtypeStruct((rows, cols), jnp.bfloat16),
        compiler_params=_cparams("parallel"),
        name="to_bf16",
    )(*([w2] * split))
    return out.reshape(shape)


def _ffn_kernel(te_ref, nt_ref, x_ref, wg_ref, wu_ref, wd_ref, *rest, fuse_ln):
    o_ref = rest[-1]
    t = pl.program_id(0)
    j = pl.program_id(1)

    @pl.when(t < nt_ref[0])
    def _():
        x = x_ref[...].astype(jnp.bfloat16)
        g = jnp.dot(x, wg_ref[...], preferred_element_type=jnp.float32)
        u = jnp.dot(x, wu_ref[...], preferred_element_type=jnp.float32)
        h = (g * jax.nn.sigmoid(g) * u).astype(jnp.bfloat16)
        part = jnp.dot(h, wd_ref[...], preferred_element_type=jnp.float32)

        @pl.when(j == 0)
        def _():
            o_ref[...] = part

        @pl.when(j > 0)
        def _():
            if fuse_ln:
                o_ref[...] = _deepnorm(x_ref[...], o_ref[...] + part, rest[0][...], rest[1][...])
            else:
                o_ref[...] += part

    @pl.when(jnp.logical_and(t >= nt_ref[0], j == 0))
    def _():
        o_ref[...] = jnp.zeros_like(o_ref)


def grouped_ffn(x, w_gu_bf16, w_down_bf16, tile_expert, n_tiles_used, ln=None):
    r, d = x.shape
    nf = D_FF // FF_TILE
    assert nf == 2
    tile = min(ROW_TILE, r)
    n_tiles = pl.cdiv(r, tile)
    in_specs = [
        pl.BlockSpec((tile, d), lambda t, j, te, nt: (t, 0)),
        pl.BlockSpec((None, d, FF_TILE), lambda t, j, te, nt: (te[t], 0, j)),
        pl.BlockSpec((None, d, FF_TILE), lambda t, j, te, nt: (te[t], 0, nf + j)),
        pl.BlockSpec((None, FF_TILE, d), lambda t, j, te, nt: (te[t], j, 0)),
    ]
    args = [tile_expert, n_tiles_used, x, w_gu_bf16, w_gu_bf16, w_down_bf16]
    if ln is not None:
        in_specs += [pl.BlockSpec((1, d), lambda t, j, te, nt: (0, 0))] * 2
        args += [ln[0].reshape(1, d), ln[1].reshape(1, d)]
    grid_spec = pltpu.PrefetchScalarGridSpec(
        num_scalar_prefetch=2,
        grid=(n_tiles, nf),
        in_specs=in_specs,
        out_specs=pl.BlockSpec((tile, d), lambda t, j, te, nt: (t, 0)),
    )
    return pl.pallas_call(
        functools.partial(_ffn_kernel, fuse_ln=ln is not None),
        grid_spec=grid_spec,
        out_shape=jax.ShapeDtypeStruct((r, d), jnp.float32),
        compiler_params=_cparams("parallel", "arbitrary"),
        name="grouped_ffn",
    )(*args)


def _s5_kernel(u_ref, h0r_ref, h0i_ref, ar_ref, ai_ref, bbr_ref, bbi_ref, cr_ref, ci_ref, d_ref, wglu_ref,
               y_ref, hro_ref, hio_ref, bur, bui, sr, si, hr, hi, *, chains, chunk):
    j = pl.program_id(0)

    @pl.when(j == 0)
    def _():
        hr[...] = h0r_ref[...]
        hi[...] = h0i_ref[...]

    u = u_ref[...].reshape(chains * chunk, S5_DIM)
    ub = u.astype(jnp.bfloat16)
    bu_r = jnp.dot(ub, bbr_ref[...], preferred_element_type=jnp.float32)
    bu_i = jnp.dot(ub, bbi_ref[...], preferred_element_type=jnp.float32)
    for k in range(S5_LT):
        bur[k] = bu_r[:, k * LANES:(k + 1) * LANES]
        bui[k] = bu_i[:, k * LANES:(k + 1) * LANES]
    ar = [jnp.broadcast_to(ar_ref[:, k * LANES:(k + 1) * LANES], (chains, LANES)) for k in range(S5_LT)]
    ai = [jnp.broadcast_to(ai_ref[:, k * LANES:(k + 1) * LANES], (chains, LANES)) for k in range(S5_LT)]

    def body(t, carry):
        rows = pl.ds(t, chains, stride=chunk)
        out = []
        for k in range(S5_LT):
            xr, xi = carry[2 * k], carry[2 * k + 1]
            nr = ar[k] * xr - ai[k] * xi + bur[k, rows, :]
            ni = ar[k] * xi + ai[k] * xr + bui[k, rows, :]
            sr[k, rows, :] = nr
            si[k, rows, :] = ni
            out += [nr, ni]
        return tuple(out)

    init = []
    for k in range(S5_LT):
        init += [hr[:, k * LANES:(k + 1) * LANES], hi[:, k * LANES:(k + 1) * LANES]]
    fin = lax.fori_loop(0, chunk, body, tuple(init))
    xr = jnp.concatenate(fin[0::2], axis=1)
    xi = jnp.concatenate(fin[1::2], axis=1)
    hr[...] = xr
    hi[...] = xi
    hro_ref[...] = xr
    hio_ref[...] = xi
    s_r = jnp.concatenate([sr[k] for k in range(S5_LT)], axis=1).astype(jnp.bfloat16)
    s_i = jnp.concatenate([si[k] for k in range(S5_LT)], axis=1).astype(jnp.bfloat16)
    y = (jnp.dot(s_r, cr_ref[...], preferred_element_type=jnp.float32)
         - jnp.dot(s_i, ci_ref[...], preferred_element_type=jnp.float32)
         + d_ref[...] * u)
    z = jax.nn.gelu(y)
    gate = jax.nn.sigmoid(jnp.dot(z.astype(jnp.bfloat16), wglu_ref[...], preferred_element_type=jnp.float32))
    y_ref[...] = (z * gate).reshape(chains, chunk, S5_DIM)


def s5_params(lam_re, lam_im, log_dt, b, c, d, w_glu):
    f32 = jnp.float32
    dt = jnp.exp(log_dt.astype(f32))[:, None]
    mag = jnp.exp(lam_re * dt)
    ang = lam_im * dt
    ab_re = mag * jnp.cos(ang)
    ab_im = mag * jnp.sin(ang)
    den = lam_re * lam_re + lam_im * lam_im
    nr = ab_re - 1.0
    coef_re = (nr * lam_re + ab_im * lam_im) / den
    coef_im = (ab_im * lam_re - nr * lam_im) / den
    b_re = b[..., 0].astype(f32)
    b_im = b[..., 1].astype(f32)
    bb_re = coef_re[..., None] * b_re - coef_im[..., None] * b_im
    bb_im = coef_re[..., None] * b_im + coef_im[..., None] * b_re
    eye = jnp.eye(S5_GROUPS, dtype=f32)
    bbr = jnp.einsum('gnk,gh->gkhn', bb_re, eye).reshape(S5_DIM, S5_N).astype(jnp.bfloat16)
    bbi = jnp.einsum('gnk,gh->gkhn', bb_im, eye).reshape(S5_DIM, S5_N).astype(jnp.bfloat16)
    cr = jnp.einsum('gkn,gh->gnhk', c[..., 0].astype(f32), eye).reshape(S5_N, S5_DIM).astype(jnp.bfloat16)
    ci = jnp.einsum('gkn,gh->gnhk', c[..., 1].astype(f32), eye).reshape(S5_N, S5_DIM).astype(jnp.bfloat16)
    return (ab_re.reshape(1, S5_N), ab_im.reshape(1, S5_N), bbr, bbi, cr, ci,
            d.astype(f32).reshape(1, S5_DIM), w_glu.astype(jnp.bfloat16))


def s5_scan(u, h0, params, chunk):
    chains, t, _ = u.shape
    ar, ai, bbr, bbi, cr, ci, d, wglu = params
    h0r = h0[..., 0].reshape(chains, S5_N)
    h0i = h0[..., 1].reshape(chains, S5_N)
    full = lambda shape: pl.BlockSpec(shape, lambda j: (0,) * len(shape))
    rows = chains * chunk
    y, hr, hi = pl.pallas_call(
        functools.partial(_s5_kernel, chains=chains, chunk=chunk),
        grid=(t // chunk,),
        in_specs=[pl.BlockSpec((chains, chunk, S5_DIM), lambda j: (0, j, 0)),
                  full((chains, S5_N)), full((chains, S5_N)), full((1, S5_N)), full((1, S5_N)),
                  full((S5_DIM, S5_N)), full((S5_DIM, S5_N)), full((S5_N, S5_DIM)), full((S5_N, S5_DIM)),
                  full((1, S5_DIM)), full((S5_DIM, S5_DIM))],
        out_specs=[pl.BlockSpec((chains, chunk, S5_DIM), lambda j: (0, j, 0)),
                   full((chains, S5_N)), full((chains, S5_N))],
        out_shape=[jax.ShapeDtypeStruct((chains, t, S5_DIM), jnp.float32),
                   jax.ShapeDtypeStruct((chains, S5_N), jnp.float32),
                   jax.ShapeDtypeStruct((chains, S5_N), jnp.float32)],
        scratch_shapes=[pltpu.VMEM((S5_LT, rows, LANES), jnp.float32)] * 4
                       + [pltpu.VMEM((chains, S5_N), jnp.float32)] * 2,
        compiler_params=_cparams("arbitrary"),
        name="s5_scan",
    )(u, h0r, h0i, ar, ai, bbr, bbi, cr, ci, d, wglu)
    new_state = jnp.stack([hr.reshape(chains, S5_GROUPS, S5_STATE), hi.reshape(chains, S5_GROUPS, S5_STATE)],
                          axis=-1)
    return y, new_state


def _dot_nt(a, b):
    return lax.dot_general(a, b, (((1,), (1,)), ((), ())), preferred_element_type=jnp.float32)


def _split3(x):
    hi = x.astype(jnp.bfloat16)
    rem = x - hi.astype(jnp.float32)
    mid = rem.astype(jnp.bfloat16)
    lo = (rem - mid.astype(jnp.float32)).astype(jnp.bfloat16)
    return hi, mid, lo


def _softmax_rows(s, mask):
    s = jnp.where(mask, s, NEG)
    m = jnp.max(s, axis=-1, keepdims=True)
    p = jnp.exp2(s - m)
    inv = jnp.where(m > 0.5 * NEG, 1.0 / jnp.sum(p, axis=-1, keepdims=True), 0.0)
    return p * inv


def _nsa_prompt_kernel(q_ref, gate_ref, kc_ref, vc_ref, ks_ref, vs_ref, kw_ref, vw_ref, o_ref, *, n_cmp, n_blk):
    f32, bf16 = jnp.float32, jnp.bfloat16
    r4 = HEADS_PER_GROUP
    n_cpad = kc_ref.shape[0]
    start = pl.program_id(1) * Q_BLOCK
    q = q_ref[...] * QK_SCALE
    gate = gate_ref[...]
    lane = lax.broadcasted_iota(jnp.int32, (Q_BLOCK, LANES), 1)
    qpos = start + lax.broadcasted_iota(jnp.int32, (Q_BLOCK, 1), 0)
    n_idx = lax.broadcasted_iota(jnp.int32, (Q_BLOCK, n_cpad), 1)
    cmask = (((n_idx * CMP_STRIDE + (CMP_LEN - 1)) <= qpos) & (n_idx < n_cmp))[None]
    ratio = SEL_BLOCK // CMP_STRIDE
    gsum = (lax.broadcasted_iota(jnp.int32, (n_blk, n_cpad), 1) // ratio
            == lax.broadcasted_iota(jnp.int32, (n_blk, n_cpad), 0)).astype(bf16)
    blk = lax.broadcasted_iota(jnp.int32, (n_blk, Q_BLOCK), 0)
    blk_f = blk.astype(f32)
    jq = (start + lax.broadcasted_iota(jnp.int32, (n_blk, Q_BLOCK), 1)) // SEL_BLOCK
    force = jnp.where((blk == 0) | (blk == jq) | (blk == jq - 1), FORCE, 0.0)
    qgs, o_cs, sels = [], [], []
    for g in range(KV_GROUPS):
        keep = (lane < HEAD_DIM) if g == 0 else (lane >= HEAD_DIM)
        parts = []
        for r in range(r4):
            h = r4 * g + r
            tile = q[:, (h // 2) * LANES:(h // 2 + 1) * LANES]
            if h % 2 != g:
                tile = pltpu.roll(tile, HEAD_DIM, axis=1)
            parts.append(jnp.where(keep, tile, 0.0))
        qg = jnp.concatenate(parts, axis=0).astype(bf16)
        qgs.append(qg)

        p_c = _softmax_rows(_dot_nt(qg, kc_ref[...]).reshape(r4, Q_BLOCK, n_cpad), cmask)
        o_cs.append(jnp.dot(p_c.reshape(r4 * Q_BLOCK, n_cpad).astype(bf16), vc_ref[...],
                            preferred_element_type=f32).reshape(r4, Q_BLOCK, LANES))
        psum = p_c[0] + p_c[1] + p_c[2] + p_c[3]
        imp_t = sum(_dot_nt(gsum, part) for part in _split3(psum))

        score = jnp.where(blk <= jq, imp_t + force, NEG)
        sel_t = jnp.zeros((n_blk, Q_BLOCK), f32)
        for _ in range(min(N_SEL, n_blk)):
            m = jnp.max(score, axis=0, keepdims=True)
            idx = jnp.min(jnp.where(score == m, blk_f, float(n_blk)), axis=0, keepdims=True)
            hit = blk_f == idx
            sel_t = jnp.where(hit & (m > 0.5 * NEG), 1.0, sel_t)
            score = jnp.where(hit, REMOVED, score)
        sels.append(sel_t.T)

    n_full = start // SEL_TILE
    expand0 = (lax.broadcasted_iota(jnp.int32, (n_blk, SEL_TILE), 0)
               == lax.broadcasted_iota(jnp.int32, (n_blk, SEL_TILE), 1) // SEL_BLOCK).astype(bf16)

    def tile_update(i, carry, causal):
        off = pl.multiple_of(i * SEL_TILE, SEL_TILE)
        k = ks_ref[pl.ds(off, SEL_TILE), :]
        v = vs_ref[pl.ds(off, SEL_TILE), :]
        out = []
        for g in range(KV_GROUPS):
            m_run, l_run, acc = carry[g]
            s_t = _dot_nt(qgs[g], k).reshape(r4, Q_BLOCK, SEL_TILE)
            shifted = pltpu.roll(sels[g], (n_blk - i * (SEL_TILE // SEL_BLOCK)) % n_blk, axis=1).astype(bf16)
            mk = jnp.dot(shifted, expand0, preferred_element_type=f32) > 0.5
            if causal:
                kpos = i * SEL_TILE + lax.broadcasted_iota(jnp.int32, (Q_BLOCK, SEL_TILE), 1)
                mk = mk & (kpos <= qpos)
            s_t = jnp.where(mk[None], s_t, NEG)
            m_new = jnp.maximum(m_run, jnp.max(s_t, axis=-1, keepdims=True))
            alpha = jnp.exp2(m_run - m_new)
            p = jnp.exp2(s_t - m_new)
            l_new = alpha * l_run + jnp.sum(p, axis=-1, keepdims=True)
            pv = jnp.dot(p.reshape(r4 * Q_BLOCK, SEL_TILE).astype(bf16), v, preferred_element_type=f32)
            out.append((m_new, l_new, alpha * acc + pv.reshape(r4, Q_BLOCK, LANES)))
        return tuple(out)

    init = (jnp.full((r4, Q_BLOCK, 1), NEG, f32), jnp.zeros((r4, Q_BLOCK, 1), f32),
            jnp.zeros((r4, Q_BLOCK, LANES), f32))
    carry = lax.fori_loop(0, n_full, lambda i, c: tile_update(i, c, False), (init, init))
    fin = tile_update(n_full, carry, True)

    n_win = WINDOW + Q_BLOCK
    woff = pl.multiple_of(start, Q_BLOCK)
    kwin = kw_ref[pl.ds(woff, n_win), :]
    vwin = vw_ref[pl.ds(woff, n_win), :]
    wpos = start - WINDOW + lax.broadcasted_iota(jnp.int32, (Q_BLOCK, n_win), 1)
    wmask = ((wpos <= qpos) & (wpos > qpos - WINDOW) & (wpos >= 0))[None]
    heads = [None] * N_HEADS
    for g in range(KV_GROUPS):
        m_fin, l_fin, acc = fin[g]
        o_s = acc * jnp.where(m_fin > 0.5 * NEG, 1.0 / l_fin, 0.0)
        p_w = _softmax_rows(_dot_nt(qgs[g], kwin).reshape(r4, Q_BLOCK, n_win), wmask)
        o_w = jnp.dot(p_w.reshape(r4 * Q_BLOCK, n_win).astype(bf16), vwin,
                      preferred_element_type=f32).reshape(r4, Q_BLOCK, LANES)
        for r in range(r4):
            h = r4 * g + r
            heads[h] = (gate[:, 3 * h:3 * h + 1] * o_cs[g][r] + gate[:, 3 * h + 1:3 * h + 2] * o_s[r]
                        + gate[:, 3 * h + 2:3 * h + 3] * o_w[r])

    tiles = []
    for j in range(N_HEADS // 2):
        even, odd = heads[2 * j], heads[2 * j + 1]
        if j // 2 == 0:
            tiles.append(jnp.where(lane < HEAD_DIM, even, pltpu.roll(odd, HEAD_DIM, axis=1)))
        else:
            tiles.append(jnp.where(lane < HEAD_DIM, pltpu.roll(even, HEAD_DIM, axis=1), odd))
    o_ref[...] = jnp.concatenate(tiles, axis=1)


def nsa_prompt(q, gates, kc, vc, ks, vs, kw_pad, vw_pad):
    b, t, _ = q.shape
    n_cpad = kc.shape[1]
    kern = functools.partial(_nsa_prompt_kernel, n_cmp=t // CMP_STRIDE - 1, n_blk=t // SEL_BLOCK)
    whole = lambda rows: pl.BlockSpec((None, rows, LANES), lambda i, j: (i, 0, 0))
    return pl.pallas_call(
        kern,
        grid=(b, t // Q_BLOCK),
        in_specs=[pl.BlockSpec((None, Q_BLOCK, NSA_Q), lambda i, j: (i, j, 0)),
                  pl.BlockSpec((None, Q_BLOCK, 3 * N_HEADS), lambda i, j: (i, j, 0)),
                  whole(n_cpad), whole(n_cpad), whole(t), whole(t), whole(t + WINDOW), whole(t + WINDOW)],
        out_specs=pl.BlockSpec((None, Q_BLOCK, NSA_Q), lambda i, j: (i, j, 0)),
        out_shape=jax.ShapeDtypeStruct((b, t, NSA_Q), jnp.float32),
        compiler_params=_cparams("parallel", "arbitrary"),
        name="nsa_prompt",
    )(q, gates, kc, vc, ks, vs, kw_pad, vw_pad)


def _compress_kernel(ch_ref, pet_ref, peb_ref, w1t_ref, w1b_ref, w2_ref, o_ref):
    bf16 = jnp.bfloat16
    ch = ch_ref[...]
    n_ch = ch.shape[0]
    a = jnp.dot((ch + pet_ref[...]).astype(bf16), w1t_ref[...], preferred_element_type=jnp.float32)
    b = jnp.dot((ch + peb_ref[...]).astype(bf16), w1b_ref[...], preferred_element_type=jnp.float32)
    pre = a + pltpu.roll(b, n_ch - 1, axis=0)
    o_ref[...] = jnp.dot(jax.nn.gelu(pre).astype(bf16), w2_ref[...],
                         preferred_element_type=jnp.float32).astype(o_ref.dtype)


def compress_params(w1, w2, pe):
    f32 = jnp.float32
    eye = jnp.eye(KV_GROUPS, dtype=f32)
    w1r = w1.astype(f32).reshape(2, CMP_STRIDE, HEAD_DIM, HEAD_DIM)
    big = jnp.einsum('hjde,gk->hjgdke', w1r, eye).reshape(2, CMP_STRIDE * LANES, LANES).astype(jnp.bfloat16)
    w2bd = jnp.einsum('de,gk->gdke', w2.astype(f32), eye).reshape(LANES, LANES).astype(jnp.bfloat16)
    per = pe.astype(f32).reshape(2, CMP_STRIDE, 1, HEAD_DIM)
    pe_rows = jnp.broadcast_to(per, (2, CMP_STRIDE, KV_GROUPS, HEAD_DIM)).reshape(2, 1, CMP_STRIDE * LANES)
    return pe_rows[0], pe_rows[1], big[0], big[1], w2bd


def compress_prompt(x, params):
    b, t, _ = x.shape
    n_ch = t // CMP_STRIDE
    ch = x.reshape(b, n_ch, CMP_STRIDE * LANES)
    pet, peb, w1t, w1b, w2bd = params
    full = lambda shape: pl.BlockSpec(shape, lambda i: (0,) * len(shape))
    return pl.pallas_call(
        _compress_kernel,
        grid=(b,),
        in_specs=[pl.BlockSpec((None, n_ch, CMP_STRIDE * LANES), lambda i: (i, 0, 0)),
                  full((1, CMP_STRIDE * LANES)), full((1, CMP_STRIDE * LANES)),
                  full((CMP_STRIDE * LANES, LANES)), full((CMP_STRIDE * LANES, LANES)), full((LANES, LANES))],
        out_specs=pl.BlockSpec((None, n_ch, LANES), lambda i: (i, 0, 0)),
        out_shape=jax.ShapeDtypeStruct((b, n_ch, LANES), jnp.bfloat16),
        compiler_params=_cparams("parallel"),
        name="compress_prompt",
    )(ch, pet, peb, w1t, w1b, w2bd)


def _cmp_sample_kernel(pt_ref, *refs, n_pages):
    f32, bf16 = jnp.float32, jnp.bfloat16
    pp = PAGES_PER_STEP
    pages = refs[0:pp]
    (perm_ref, newk_ref, newv_ref, wk_ref, wv_ref, ck_ref, cv_ref, w2k_ref, w2v_ref,
     kc_ref, vc_ref, slab_k, slab_v) = refs[pp:]
    s = pl.program_id(1)
    cpp = PAGE_SIZE // CMP_STRIDE
    base = pl.multiple_of(s * (pp * cpp), pp * cpp)
    for half, slab in enumerate((slab_k, slab_v)):
        for i in range(pp):
            page = pages[i][half * LANES:(half + 1) * LANES, :].astype(bf16)
            rows = _dot_nt(perm_ref[...], page)
            for j in range(CMP_STRIDE):
                slab[j, pl.ds(base + i * cpp, cpp), :] = rows[j * cpp:(j + 1) * cpp, :]

    @pl.when(s == pl.num_programs(1) - 1)
    def _():
        n_ch = n_pages * (PAGE_SIZE // CMP_STRIDE)
        row = lax.broadcasted_iota(jnp.int32, (n_ch, LANES), 0)
        for slab, new_ref, w_ref, c_ref, w2_ref, o_ref in ((slab_k, newk_ref, wk_ref, ck_ref, w2k_ref, kc_ref),
                                                           (slab_v, newv_ref, wv_ref, cv_ref, w2v_ref, vc_ref)):
            ch = jnp.concatenate([slab[j] for j in range(CMP_STRIDE)], axis=1).astype(bf16)
            ab = jnp.dot(ch, w_ref[...], preferred_element_type=f32)
            b_new = jnp.dot(new_ref[...].astype(bf16), w_ref[...], preferred_element_type=f32)[0:1, LANES:]
            nxt = pltpu.roll(ab[:, LANES:], n_ch - 1, axis=0)
            nxt = jnp.where(row == n_ch - 1, b_new, nxt)
            pre = ab[:, :LANES] + nxt + c_ref[...]
            o_ref[...] = jnp.dot(jax.nn.gelu(pre).astype(bf16), w2_ref[...],
                                 preferred_element_type=f32).astype(o_ref.dtype)


def compress_sample_params(w1, w2, pe):
    pet, peb, w1t, w1b, w2bd = compress_params(w1, w2, pe)
    hp = lax.Precision.HIGHEST
    const = (jnp.dot(pet, w1t.astype(jnp.float32), precision=hp)
             + jnp.dot(peb, w1b.astype(jnp.float32), precision=hp))
    return jnp.concatenate([w1t, w1b], axis=1), const, w2bd


def _page_spec(i, pair):
    return pl.BlockSpec((None, 2 * LANES, PAGE_SIZE),
                        lambda b, s, pt: (pt[b, PAGES_PER_STEP * s + i], pair, 0))


def _per_seq(shape):
    return pl.BlockSpec((None,) + shape, lambda b, s, pt: (b, 0, 0))


def compress_sample(pool_t, page_table, new_k, new_v, pk, pv):
    bsz, n_pages = page_table.shape
    pp = PAGES_PER_STEP
    n_ch = n_pages * (PAGE_SIZE // CMP_STRIDE)
    t_new = new_k.shape[1]

    def chunk_rows(x):
        x = jnp.pad(x, ((0, 0), (0, CMP_STRIDE - t_new), (0, 0))).reshape(bsz, 1, CMP_STRIDE * LANES)
        return jnp.pad(x, ((0, 0), (0, 7), (0, 0)))

    full = lambda shape: pl.BlockSpec(shape, lambda b, s, pt: (0,) * len(shape))
    r = jnp.arange(PAGE_SIZE)
    cpp = PAGE_SIZE // CMP_STRIDE
    perm = (r[None, :] == (r[:, None] % cpp) * CMP_STRIDE + r[:, None] // cpp).astype(jnp.bfloat16)
    wk, ck, w2k = pk
    wv, cv, w2v = pv
    grid_spec = pltpu.PrefetchScalarGridSpec(
        num_scalar_prefetch=1,
        grid=(bsz, n_pages // pp),
        in_specs=[_page_spec(i, 0) for i in range(pp)]
                 + [full((PAGE_SIZE, PAGE_SIZE)), _per_seq((8, CMP_STRIDE * LANES)),
                    _per_seq((8, CMP_STRIDE * LANES)),
                    full((CMP_STRIDE * LANES, 2 * LANES)), full((CMP_STRIDE * LANES, 2 * LANES)),
                    full((1, LANES)), full((1, LANES)), full((LANES, LANES)), full((LANES, LANES))],
        out_specs=[_per_seq((n_ch, LANES)), _per_seq((n_ch, LANES))],
        scratch_shapes=[pltpu.VMEM((CMP_STRIDE, n_ch, LANES), jnp.float32)] * 2,
    )
    return pl.pallas_call(
        functools.partial(_cmp_sample_kernel, n_pages=n_pages),
        grid_spec=grid_spec,
        out_shape=[jax.ShapeDtypeStruct((bsz, n_ch, LANES), jnp.bfloat16)] * 2,
        compiler_params=_cparams("parallel", "arbitrary"),
        name="compress_sample",
    )(page_table, *([pool_t] * pp), perm, chunk_rows(new_k), chunk_rows(new_v), wk, wv, ck, cv, w2k, w2v)


def _nsa_sample_kernel(pt_ref, *refs, n_pages, t_new, w_buf):
    f32, bf16 = jnp.float32, jnp.bfloat16
    pp = PAGES_PER_STEP
    q_ref, gate_ref, kc_ref, vc_ref = refs[0:4]
    pages = refs[4:4 + pp]
    (ksn_ref, vsn_ref, win_ref, kwn_ref, vwn_ref, o_ref,
     sel_scr, exp_scr, oc_scr, m_scr, l_scr, acc_scr) = refs[4 + pp:]
    r4, g2 = HEADS_PER_GROUP, KV_GROUPS
    n_rows = g2 * r4 * t_new
    past_len = n_pages * PAGE_SIZE
    n_cmp = kc_ref.shape[0]
    n_bpad = sel_scr.shape[1]
    tile = pp * PAGE_SIZE
    s = pl.program_id(1)
    qall = q_ref[...]
    qpos = past_len + lax.broadcasted_iota(jnp.int32, (n_rows, 1), 0) % t_new

    def grouped(x):
        return x.reshape(g2, 1, t_new, x.shape[-1])

    @pl.when(s == 0)
    def _():
        s_c = _dot_nt(qall, kc_ref[...])
        n_idx = lax.broadcasted_iota(jnp.int32, (n_rows, n_cmp), 1)
        p_c = _softmax_rows(s_c, (n_idx * CMP_STRIDE + (CMP_LEN - 1)) <= qpos)
        oc_scr[...] = jnp.dot(p_c.astype(bf16), vc_ref[...], preferred_element_type=f32)
        psum = jnp.sum(p_c.reshape(g2, r4, t_new, n_cmp), axis=1).reshape(g2 * t_new, n_cmp)
        psum = jnp.concatenate([psum, jnp.zeros((LANES - g2 * t_new, n_cmp), f32)], axis=0)
        p_hi = psum.astype(bf16)
        rem = psum - p_hi.astype(f32)
        p_mid = rem.astype(bf16)
        p_lo = (rem - p_mid.astype(f32)).astype(bf16)
        ratio = SEL_BLOCK // CMP_STRIDE
        gsum = (lax.broadcasted_iota(jnp.int32, (n_bpad, n_cmp), 1) // ratio
                == lax.broadcasted_iota(jnp.int32, (n_bpad, n_cmp), 0)).astype(bf16)
        imp_t = _dot_nt(gsum, p_hi) + _dot_nt(gsum, p_mid) + _dot_nt(gsum, p_lo)
        blk = lax.broadcasted_iota(jnp.int32, (n_bpad, LANES), 0)
        jq = (past_len + lax.broadcasted_iota(jnp.int32, (n_bpad, LANES), 1) % t_new) // SEL_BLOCK
        forced = (blk == 0) | (blk == jq) | (blk == jq - 1)
        score = jnp.where(blk <= jq, imp_t + jnp.where(forced, FORCE, 0.0), NEG)
        blk_f = blk.astype(f32)
        sel_t = jnp.zeros((n_bpad, LANES), f32)
        for _ in range(N_SEL):
            m = jnp.max(score, axis=0, keepdims=True)
            idx = jnp.min(jnp.where(score == m, blk_f, float(n_bpad)), axis=0, keepdims=True)
            hit = blk_f == idx
            sel_t = jnp.where(hit & (m > 0.5 * NEG), 1.0, sel_t)
            score = jnp.where(hit, REMOVED, score)
        sel = jnp.concatenate([sel_t[k * LANES:(k + 1) * LANES].T for k in range(n_bpad // LANES)], axis=1)
        sel_scr[...] = sel[0:g2 * t_new]
        exp_scr[...] = (lax.broadcasted_iota(jnp.int32, (LANES, tile), 0)
                        == lax.broadcasted_iota(jnp.int32, (LANES, tile), 1) // SEL_BLOCK).astype(bf16)
        m_scr[...] = jnp.full(m_scr.shape, NEG, f32)
        l_scr[...] = jnp.zeros(l_scr.shape, f32)
        acc_scr[...] = jnp.zeros(acc_scr.shape, f32)

    def online_update(s_t, mk, v, v_feature_major):
        n = s_t.shape[-1]
        s4 = jnp.where(mk, s_t.reshape(g2, r4, t_new, n), NEG)
        m_run = m_scr[...].reshape(g2, r4, t_new, 1)
        m_new = jnp.maximum(m_run, jnp.max(s4, axis=-1, keepdims=True))
        alpha = jnp.exp2(m_run - m_new)
        p = jnp.exp2(s4 - m_new)
        l_new = alpha * l_scr[...].reshape(g2, r4, t_new, 1) + jnp.sum(p, axis=-1, keepdims=True)
        pb = p.reshape(n_rows, n).astype(bf16)
        pv = _dot_nt(pb, v) if v_feature_major else jnp.dot(pb, v, preferred_element_type=f32)
        m_scr[...] = m_new.reshape(n_rows, 1)
        l_scr[...] = l_new.reshape(n_rows, 1)
        acc_scr[...] = alpha.reshape(n_rows, 1) * acc_scr[...] + pv

    kt = jnp.concatenate([r[0:LANES, :] for r in pages], axis=1).astype(bf16)
    vt = jnp.concatenate([r[LANES:2 * LANES, :] for r in pages], axis=1).astype(bf16)
    shifted = pltpu.roll(sel_scr[...], (n_bpad - s * (tile // SEL_BLOCK)) % n_bpad, axis=1)
    picked = jnp.dot(shifted[:, 0:LANES].astype(bf16), exp_scr[...], preferred_element_type=f32)
    online_update(jnp.dot(qall, kt, preferred_element_type=f32), grouped(picked) > 0.5, vt, True)

    @pl.when(s == pl.num_programs(1) - 1)
    def _():
        new_blk = past_len // SEL_BLOCK
        kidx = lax.broadcasted_iota(jnp.int32, (n_rows, NEW_PAD), 1)
        causal = ((past_len + kidx) <= qpos) & (kidx < t_new)
        picked_new = sel_scr[:, new_blk:new_blk + 1]
        mk = (grouped(picked_new) > 0.5) & causal.reshape(g2, r4, t_new, NEW_PAD)
        online_update(_dot_nt(qall, ksn_ref[...]), mk, vsn_ref[...], False)
        o_s = acc_scr[...] * jnp.where(m_scr[...] > 0.5 * NEG, 1.0 / l_scr[...], 0.0)

        n_win = w_buf + NEW_PAD
        kw_t = win_ref[0:LANES, :].astype(bf16)
        vw_t = win_ref[LANES:2 * LANES, :].astype(bf16)
        widx = lax.broadcasted_iota(jnp.int32, (n_rows, n_win), 1)
        wpos = past_len - w_buf + widx
        wmask = (wpos <= qpos) & (wpos > qpos - WINDOW) & (wpos >= 0) & (widx < w_buf + t_new)
        s_w = jnp.concatenate([jnp.dot(qall, kw_t, preferred_element_type=f32), _dot_nt(qall, kwn_ref[...])],
                              axis=1)
        p_w = _softmax_rows(s_w, wmask).astype(bf16)
        o_w = (_dot_nt(p_w[:, 0:w_buf], vw_t)
               + jnp.dot(p_w[:, w_buf:], vwn_ref[...], preferred_element_type=f32))
        gate = gate_ref[...]
        o_ref[...] = gate[:, 0:1] * oc_scr[...] + gate[:, 1:2] * o_s + gate[:, 2:3] * o_w


def nsa_sample(q, gates, kc, vc, pool_t, page_table, ks_new, vs_new, win, kw_new, vw_new):
    f32, bf16 = jnp.float32, jnp.bfloat16
    bsz, t_new = q.shape[0], q.shape[1]
    n_pages = page_table.shape[1]
    pp = PAGES_PER_STEP
    w_buf = win.shape[2]
    r4, g2 = HEADS_PER_GROUP, KV_GROUPS
    n_rows = g2 * r4 * t_new
    past_len = n_pages * PAGE_SIZE
    assert past_len % SEL_BLOCK == 0 and t_new <= SEL_BLOCK and past_len >= w_buf and n_pages % pp == 0
    n_sel = past_len // SEL_BLOCK + 1
    n_bpad = -(-n_sel // LANES) * LANES
    eye = jnp.eye(g2, dtype=f32)
    qg = q.reshape(bsz, t_new, g2, r4, HEAD_DIM).transpose(0, 2, 3, 1, 4) * QK_SCALE
    qall = jnp.einsum('bgrqd,gk->bgrqkd', qg, eye).reshape(bsz, n_rows, LANES).astype(bf16)
    gall = gates.reshape(bsz, t_new, g2, r4, 3).transpose(0, 2, 3, 1, 4).reshape(bsz, n_rows, 3)
    pad_rows = lambda x: jnp.pad(x, ((0, 0), (0, NEW_PAD - t_new), (0, 0))).astype(bf16)
    n_cmp = kc.shape[1]
    grid_spec = pltpu.PrefetchScalarGridSpec(
        num_scalar_prefetch=1,
        grid=(bsz, n_pages // pp),
        in_specs=[_per_seq((n_rows, LANES)), _per_seq((n_rows, 3)), _per_seq((n_cmp, LANES)),
                  _per_seq((n_cmp, LANES))]
                 + [_page_spec(i, 1) for i in range(pp)]
                 + [_per_seq((NEW_PAD, LANES)), _per_seq((NEW_PAD, LANES)), _per_seq((2 * LANES, w_buf)),
                    _per_seq((NEW_PAD, LANES)), _per_seq((NEW_PAD, LANES))],
        out_specs=_per_seq((n_rows, LANES)),
        scratch_shapes=[pltpu.VMEM((g2 * t_new, n_bpad), f32), pltpu.VMEM((LANES, pp * PAGE_SIZE), bf16),
                        pltpu.VMEM((n_rows, LANES), f32),
                        pltpu.VMEM((n_rows, 1), f32), pltpu.VMEM((n_rows, 1), f32),
                        pltpu.VMEM((n_rows, LANES), f32)],
    )
    o = pl.pallas_call(
        functools.partial(_nsa_sample_kernel, n_pages=n_pages, t_new=t_new, w_buf=w_buf),
        grid_spec=grid_spec,
        out_shape=jax.ShapeDtypeStruct((bsz, n_rows, LANES), f32),
        compiler_params=_cparams("parallel", "arbitrary"),
        name="nsa_sample",
    )(page_table, qall, gall, kc, vc, *([pool_t] * pp), pad_rows(ks_new), pad_rows(vs_new), win,
      pad_rows(kw_new), pad_rows(vw_new))
    o = jnp.einsum('bgrqkd,gk->bqgrd', o.reshape(bsz, g2, r4, t_new, g2, HEAD_DIM), eye)
    return o.reshape(bsz, t_new, NSA_Q)


def _ssd_kernel(x_ref, b_ref, c_ref, dt_ref, a_ref, z_ref, dskip_ref, ng_ref, y_ref, hout_ref, h_scr, *, chunk):
    f32, bf16 = jnp.float32, jnp.bfloat16
    n_l = chunk
    hpg = SSD_HEADS // SSD_GROUPS
    gw = hpg * SSD_HEAD_DIM
    j = pl.program_id(1)

    @pl.when(j == 0)
    def _():
        h_scr[...] = jnp.zeros(h_scr.shape, f32)

    x = x_ref[...]
    dt = dt_ref[...]
    tri_b = (lax.broadcasted_iota(jnp.int32, (n_l, n_l), 0) >= lax.broadcasted_iota(jnp.int32, (n_l, n_l), 1))
    tri = tri_b.astype(bf16)
    cum = sum(jnp.dot(tri, part, preferred_element_type=f32) for part in _split3(dt * a_ref[...]))
    cum_t = cum.T
    dt_t = dt.T
    ecum = jnp.exp(cum)
    clast = cum[n_l - 1:n_l, :]
    wt = jnp.exp(clast - cum) * dt
    elast = jnp.exp(clast)
    lane = lax.broadcasted_iota(jnp.int32, (n_l, LANES), 1)
    low = lane < SSD_HEAD_DIM

    def pair(v, h0):
        return jnp.where(low[:v.shape[0]], v[:, h0:h0 + 1], v[:, h0 + 1:h0 + 2])

    tiles = []
    for g in range(SSD_GROUPS):
        bg = b_ref[:, g * SSD_STATE:(g + 1) * SSD_STATE]
        cgb = c_ref[:, g * SSD_STATE:(g + 1) * SSD_STATE].astype(bf16)
        bgt = bg.T.astype(bf16)
        cb = jnp.dot(cgb, bgt, preferred_element_type=f32)
        hg = h_scr[g]
        y_inter = jnp.dot(cgb, hg.astype(bf16), preferred_element_type=f32)
        xw, dec = [], []
        for pr in range(hpg // 2):
            h0 = hpg * g + 2 * pr
            xt = x[:, (h0 // 2) * LANES:(h0 // 2 + 1) * LANES]
            acc = None
            for k in range(2):
                h = h0 + k
                seg = cum[:, h:h + 1] - cum_t[h:h + 1, :]
                w = cb * jnp.exp(jnp.where(tri_b, seg, NEG)) * dt_t[h:h + 1, :]
                xm = jnp.where(low if k == 0 else jnp.logical_not(low), xt, 0.0).astype(bf16)
                part = jnp.dot(w.astype(bf16), xm, preferred_element_type=f32)
                acc = part if acc is None else acc + part
            tiles.append(acc + y_inter[:, pr * LANES:(pr + 1) * LANES] * pair(ecum, h0))
            xw.append((xt * pair(wt, h0)).astype(bf16))
            dec.append(pair(elast, h0))
        h_scr[g] = (hg * jnp.concatenate(dec, axis=1)
                    + jnp.dot(bgt, jnp.concatenate(xw, axis=1), preferred_element_type=f32))
    y = jnp.concatenate(tiles, axis=1) + dskip_ref[...] * x
    zg = z_ref[...]
    v = y * (zg * jax.nn.sigmoid(zg))
    outs = []
    for g in range(SSD_GROUPS):
        vg = v[:, g * gw:(g + 1) * gw]
        outs.append(vg * lax.rsqrt(jnp.mean(vg * vg, axis=-1, keepdims=True) + RMS_EPS))
    y_ref[...] = jnp.concatenate(outs, axis=1) * ng_ref[...]

    @pl.when(j == pl.num_programs(1) - 1)
    def _():
        hout_ref[...] = h_scr[...]


def ssd_prompt(xbc, dt, a, zg, d_skip, norm_g):
    f32 = jnp.float32
    bsz, t, _ = xbc.shape
    hpg = SSD_HEADS // SSD_GROUPS
    gn = SSD_GROUPS * SSD_STATE
    dt_p = jnp.pad(dt, ((0, 0), (0, 0), (0, LANES - SSD_HEADS)))
    a_p = jnp.pad(a.astype(f32), (0, LANES - SSD_HEADS)).reshape(1, LANES)
    dsk = jnp.repeat(d_skip.astype(f32), SSD_HEAD_DIM).reshape(1, SSD_INNER)
    blk = lambda w, c: pl.BlockSpec((None, SSD_CHUNK, w), lambda b, j: (b, j, c))
    full = lambda shape: pl.BlockSpec(shape, lambda b, j: (0,) * len(shape))
    state_spec = pl.BlockSpec((None, SSD_GROUPS, SSD_STATE, hpg * SSD_HEAD_DIM), lambda b, j: (b, 0, 0, 0))
    y, h = pl.pallas_call(
        functools.partial(_ssd_kernel, chunk=SSD_CHUNK),
        grid=(bsz, t // SSD_CHUNK),
        in_specs=[blk(SSD_INNER, 0), blk(gn, SSD_INNER // gn), blk(gn, SSD_INNER // gn + 1), blk(LANES, 0),
                  full((1, LANES)), blk(SSD_INNER, 0), full((1, SSD_INNER)), full((1, SSD_INNER))],
        out_specs=[blk(SSD_INNER, 0), state_spec],
        out_shape=[jax.ShapeDtypeStruct((bsz, t, SSD_INNER), f32),
                   jax.ShapeDtypeStruct((bsz, SSD_GROUPS, SSD_STATE, hpg * SSD_HEAD_DIM), f32)],
        scratch_shapes=[pltpu.VMEM((SSD_GROUPS, SSD_STATE, hpg * SSD_HEAD_DIM), f32)],
        compiler_params=_cparams("parallel", "arbitrary"),
        name="ssd_prompt",
    )(xbc, xbc, xbc, dt_p, a_p, zg, dsk, norm_g.astype(f32).reshape(1, SSD_INNER))
    h = h.reshape(bsz, SSD_GROUPS, SSD_STATE, hpg, SSD_HEAD_DIM).transpose(0, 1, 3, 4, 2)
    return y, h.reshape(bsz, SSD_HEADS, SSD_HEAD_DIM, SSD_STATE)


def layer_norm(x, g, b):
    mu = jnp.mean(x, -1, keepdims=True)
    xc = x - mu
    var = jnp.mean(xc * xc, -1, keepdims=True)
    return xc * lax.rsqrt(var + LN_EPS) * g + b


def rope(x, pos):
    half = ROT_DIM // 2
    inv = ROPE_THETA ** (-jnp.arange(half, dtype=jnp.float32) * 2.0 / ROT_DIM)
    ang = pos.astype(jnp.float32)[:, None] * inv[None, :]
    cos = jnp.cos(ang)[:, None, :]
    sin = jnp.sin(ang)[:, None, :]
    x1 = x[..., :half]
    x2 = x[..., half:ROT_DIM]
    return jnp.concatenate([x1 * cos - x2 * sin, x2 * cos + x1 * sin, x[..., ROT_DIM:]], axis=-1)


def last_rows(x, n):
    t = x.shape[1]
    if t < n:
        x = jnp.pad(x, [(0, 0), (n - t, 0)] + [(0, 0)] * (x.ndim - 2))
    return x[:, x.shape[1] - n:]


def causal_conv(x, buf, w, b):
    t = x.shape[1]
    width = w.shape[0]
    xp = jnp.concatenate([buf, x], axis=1)
    y = b + sum(xp[:, j:j + t] * w[j] for j in range(width))
    return y, xp[:, xp.shape[1] - (width - 1):]


def even_split(z, pos):
    bt, t, _ = z.shape
    o = [0]
    for w in (S5_DIM, NSA_Q, NSA_KV, NSA_KV, NSA_KV, 3 * N_HEADS):
        o.append(o[-1] + w)
    u = z[..., o[0]:o[1]]
    q = rope(z[..., o[1]:o[2]].reshape(bt, t, N_HEADS, HEAD_DIM), pos)

    def kv(a, b):
        r = z[..., a:b].reshape(bt, t, 2, KV_GROUPS, HEAD_DIM)
        return jnp.stack([rope(r[:, :, 0], pos), r[:, :, 1]], axis=2)

    kvc = kv(o[2], o[3])
    kvs = kv(o[3], o[4])
    kvw = kv(o[4], o[5])
    gates = jax.nn.sigmoid(z[..., o[5]:o[6]]).reshape(bt, t, N_HEADS, 3)
    return u, q, kvc, kvs, kvw, gates


def even_prompt_mix(z, s5p, cmpp, w_buf):
    bt, t, _ = z.shape
    pos = jnp.arange(t)
    u, q, kvc, kvs, kvw, gates = even_split(z, pos)
    y_s5, s5_state = s5_scan(u, jnp.zeros((bt, S5_GROUPS, S5_STATE, 2), jnp.float32), s5p, S5_CHUNK)
    bf16 = jnp.bfloat16
    lanes = lambda a: a.reshape(bt, t, KV_GROUPS * HEAD_DIM)
    kc = compress_prompt(lanes(kvc[:, :, 0]), compress_params(cmpp[0], cmpp[1], cmpp[2]))
    vc = compress_prompt(lanes(kvc[:, :, 1]), compress_params(cmpp[3], cmpp[4], cmpp[5]))
    front = lambda a: jnp.pad(lanes(a).astype(bf16), ((0, 0), (WINDOW, 0), (0, 0)))
    y_nsa = nsa_prompt(q.reshape(bt, t, NSA_Q), gates.reshape(bt, t, 3 * N_HEADS), kc, vc,
                       lanes(kvs[:, :, 0]).astype(bf16), lanes(kvs[:, :, 1]).astype(bf16),
                       front(kvw[:, :, 0]), front(kvw[:, :, 1]))
    new_rows = jnp.concatenate([kvc, kvs], axis=2)
    return (y_s5, y_nsa), s5_state, new_rows, last_rows(kvw, w_buf)


def even_sample_mix(z, s5_h0, pool, page_table, win_buf, s5p, cmpp):
    bt, t, _ = z.shape
    f32 = jnp.float32
    pos = page_table.shape[1] * PAGE_SIZE + jnp.arange(t)
    u, q, kvc, kvs, kvw, gates = even_split(z, pos)
    y_s5, s5_state = s5_scan(u, s5_h0.astype(f32), s5p, t)
    lanes = lambda a: a.reshape(bt, t, KV_GROUPS * HEAD_DIM)
    feat = KV_GROUPS * HEAD_DIM
    pool_t = pool.astype(f32).transpose(0, 2, 3, 4, 1).reshape(pool.shape[0], 4 * feat, PAGE_SIZE)
    kc, vc = compress_sample(pool_t, page_table, lanes(kvc[:, :, 0]), lanes(kvc[:, :, 1]),
                             compress_sample_params(cmpp[0], cmpp[1], cmpp[2]),
                             compress_sample_params(cmpp[3], cmpp[4], cmpp[5]))
    w_buf = win_buf.shape[1]
    win_f = win_buf.astype(f32)
    y_nsa = nsa_sample(q, gates, kc, vc, pool_t, page_table, lanes(kvs[:, :, 0]), lanes(kvs[:, :, 1]),
                       win_f.transpose(0, 2, 3, 4, 1).reshape(bt, 2 * feat, w_buf), lanes(kvw[:, :, 0]),
                       lanes(kvw[:, :, 1]))
    new_rows = jnp.concatenate([kvc, kvs], axis=2)
    win = jnp.concatenate([win_f, kvw], axis=1)
    return (y_s5, y_nsa), s5_state, new_rows, win[:, t:]


def ssd_scan(x, dt, a, bm, cm, h0, chunk):
    bt, t, nh, p = x.shape
    nch = t // chunk
    r = nh // SSD_GROUPS
    tri = jnp.arange(chunk)[:, None] >= jnp.arange(chunk)[None, :]

    def to_chunks(v):
        return jnp.moveaxis(v.reshape((bt, nch, chunk) + v.shape[2:]), 1, 0)

    def step(h, inp):
        xc, dtc, bc, cc = inp
        cum = jnp.cumsum(dtc * a, axis=1)
        seg = cum[:, :, None, :] - cum[:, None, :, :]
        decay = jnp.exp(jnp.where(tri[None, :, :, None], seg, NEG)).reshape(bt, chunk, chunk, SSD_GROUPS, r)
        cb = jnp.einsum('btgn,bsgn->btsg', cc, bc)
        xg = xc.reshape(bt, chunk, SSD_GROUPS, r, p)
        dg = dtc.reshape(bt, chunk, SSD_GROUPS, r)
        w = cb[..., None] * decay * dg[:, None]
        y_intra = jnp.einsum('btsgr,bsgrp->btgrp', w, xg)
        hg = h.reshape(bt, SSD_GROUPS, r, p, SSD_STATE)
        y_inter = jnp.einsum('btgn,bgrpn->btgrp', cc, hg) * jnp.exp(cum).reshape(bt, chunk, SSD_GROUPS, r)[..., None]
        wt = (jnp.exp(cum[:, -1:, :] - cum) * dtc).reshape(bt, chunk, SSD_GROUPS, r)
        h_new = (hg * jnp.exp(cum[:, -1]).reshape(bt, SSD_GROUPS, r)[..., None, None]
                 + jnp.einsum('bsgr,bsgrp,bsgn->bgrpn', wt, xg, bc))
        return h_new.reshape(bt, nh, p, SSD_STATE), (y_intra + y_inter).reshape(bt, chunk, nh, p)

    h_fin, ys = lax.scan(step, h0, (to_chunks(x), to_chunks(dt), to_chunks(bm), to_chunks(cm)))
    return jnp.moveaxis(ys, 0, 1).reshape(bt, t, nh, p), h_fin


def gated_rmsnorm(y, z, g):
    v = y * jax.nn.silu(z)
    bt, t, _ = v.shape
    vg = v.reshape(bt, t, SSD_GROUPS, SSD_INNER // SSD_GROUPS)
    vg = vg * lax.rsqrt(jnp.mean(vg * vg, -1, keepdims=True) + RMS_EPS)
    return vg.reshape(bt, t, SSD_INNER) * g


def odd_mix(z, sc_buf, conv_buf, h0, chunk, sc_w, sc_b, cv_w, cv_b, dt_bias, a_log, d_skip, norm_g):
    f32 = jnp.float32
    bt, t, _ = z.shape
    o1 = SC_DIM
    o2 = 2 * SC_DIM
    o3 = 3 * SC_DIM
    o4 = o3 + SSD_INNER
    o5 = o4 + SSD_CONV_DIM
    sc_h = z[..., :o1]
    sc_bg = z[..., o1:o2]
    sc_cg = z[..., o2:o3]
    zg = z[..., o3:o4]
    xbc = z[..., o4:o5]
    dt_raw = z[..., o5:]
    conv_sc, new_sc = causal_conv(sc_cg * sc_h, sc_buf.astype(f32), sc_w, sc_b)
    y_sc = sc_bg * conv_sc
    xbc_c, new_conv = causal_conv(xbc, conv_buf.astype(f32), cv_w, cv_b)
    xbc_c = jax.nn.silu(xbc_c)
    gn = SSD_GROUPS * SSD_STATE
    xs = xbc_c[..., :SSD_INNER].reshape(bt, t, SSD_HEADS, SSD_HEAD_DIM)
    bm = xbc_c[..., SSD_INNER:SSD_INNER + gn].reshape(bt, t, SSD_GROUPS, SSD_STATE)
    cm = xbc_c[..., SSD_INNER + gn:].reshape(bt, t, SSD_GROUPS, SSD_STATE)
    dt = jax.nn.softplus((dt_raw + dt_bias).astype(f32))
    a = -jnp.exp(a_log.astype(f32))
    if h0 is None:
        y, h_new = ssd_prompt(xbc_c, dt, a, zg, d_skip, norm_g)
    else:
        y, h_new = ssd_scan(xs, dt, a, bm, cm, h0.astype(f32), chunk)
        y = (y + d_skip[:, None] * xs).reshape(bt, t, SSD_INNER)
        y = gated_rmsnorm(y, zg, norm_g)
    return (y_sc, y), new_sc, new_conv, h_new


def moe_ffn(x, w_r, b_r, w_gu_bf16, w_down_bf16):
    n, d = x.shape
    logits = jnp.dot(x, w_r, precision=lax.Precision.HIGHEST) + b_r
    top_v, top_i = lax.top_k(logits, TOP_K)
    gate = jax.nn.softmax(top_v, axis=-1)
    flat_e = top_i.reshape(-1)
    blk = 128
    assert (TOP_K * n) % blk == 0
    onehot = jax.nn.one_hot(flat_e, N_EXPERTS, dtype=jnp.float32).reshape(-1, blk, N_EXPERTS)
    tri = (jnp.arange(blk)[:, None] >= jnp.arange(blk)[None, :]).astype(jnp.float32)
    local = jnp.einsum('ij,bjk->bik', tri, onehot)
    block_total = local[:, -1, :]
    block_off = jnp.cumsum(block_total, axis=0) - block_total
    incl = (local + block_off[:, None, :]).reshape(-1, N_EXPERTS)
    rank = jnp.take_along_axis(incl, flat_e[:, None], axis=1)[:, 0].astype(jnp.int32) - 1
    counts = jnp.sum(block_total, axis=0).astype(jnp.int32)
    padded = ((counts + ROW_TILE - 1) // ROW_TILE) * ROW_TILE
    pad_start = jnp.cumsum(padded) - padded
    dest = (pad_start[flat_e] + rank).astype(jnp.int32)
    n_tiles = (TOP_K * n) // ROW_TILE + N_EXPERTS
    rows = n_tiles * ROW_TILE
    row_token = jnp.zeros((rows,), jnp.int32).at[dest].set(jnp.arange(TOP_K * n, dtype=jnp.int32) // TOP_K,
                                                           unique_indices=True, mode='promise_in_bounds')
    tile_end = jnp.cumsum(padded) // ROW_TILE
    tile_expert = jnp.minimum(jnp.searchsorted(tile_end, jnp.arange(n_tiles), side='right'),
                              N_EXPERTS - 1).astype(jnp.int32)
    n_used = tile_end[-1:].astype(jnp.int32)
    xs = x.at[row_token].get(mode='promise_in_bounds')
    ys = grouped_ffn(xs, w_gu_bf16, w_down_bf16, tile_expert, n_used)
    dest = dest.reshape(n, TOP_K)
    y0 = ys.at[dest[:, 0]].get(mode='promise_in_bounds')
    y1 = ys.at[dest[:, 1]].get(mode='promise_in_bounds')
    return gate[:, 0:1] * y0 + gate[:, 1:2] * y1


def kernel(x_prompt, x_sample, state_s5, cache_nsa_kv, state_win_kv, state_sc_conv, state_ssd_conv, state_ssd,
           page_table, ln_g, ln_b, w_in_even, s5_lam_re, s5_lam_im, s5_log_dt, s5_b, s5_c, s5_d, s5_w_glu,
           nsa_wk1, nsa_wk2, nsa_pe_k, nsa_wv1, nsa_wv2, nsa_pe_v, w_out_even, ffn_w_gu, ffn_w_down,
           w_in_odd, sc_conv_w, sc_conv_b, ssd_conv_w, ssd_conv_b, ssd_dt_bias, ssd_a_log, ssd_d, ssd_norm_g,
           w_out_odd, moe_router, moe_router_b, moe_w_gu, moe_w_down):
    f32 = jnp.float32
    bf16 = jnp.bfloat16
    bp, tp, d = x_prompt.shape
    bs, ts, _ = x_sample.shape
    n_p = bp * tp
    n_s = bs * ts
    w_buf = state_win_kv.shape[2]
    streams = [x_prompt.astype(f32).reshape(n_p, d), x_sample.astype(f32).reshape(n_s, d)]
    shapes = [(bp, tp), (bs, ts)]

    def flat(parts, n_rows):
        return [p.reshape(n_rows, p.shape[-1]) for p in parts]

    def out_proj(parts, w_out, width, h, g, b):
        w = w_out.astype(bf16)
        return matmul(flat(parts, h.shape[0]), [w[:width], w[width:]], ln=(h, g, b))

    def single_expert(n_rows):
        n_tiles = pl.cdiv(n_rows, min(ROW_TILE, n_rows))
        return jnp.zeros((n_tiles,), jnp.int32), jnp.full((1,), n_tiles, jnp.int32)

    s5p = s5_params(s5_lam_re[0], s5_lam_im[0], s5_log_dt[0], s5_b[0], s5_c[0], s5_d[0], s5_w_glu[0])
    cmpp = (nsa_wk1[0], nsa_wk2[0], nsa_pe_k[0], nsa_wv1[0], nsa_wv2[0], nsa_pe_v[0])
    w_in = w_in_even[0].astype(bf16)
    zp, zs = [matmul([h], [w_in]).reshape(sh + (-1,)) for h, sh in zip(streams, shapes)]
    mix_p, s5_p, kv_p, win_p = even_prompt_mix(zp, s5p, cmpp, w_buf)
    mix_s, s5_s, kv_s, win_s = even_sample_mix(zs, state_s5[0], cache_nsa_kv[0], page_table, state_win_kv[0],
                                               s5p, cmpp)
    streams = [out_proj(mix, w_out_even[0], S5_DIM, h, ln_g[0, 0], ln_b[0, 0])
               for mix, h in zip((mix_p, mix_s), streams)]
    w_gu, w_down = to_bf16(ffn_w_gu), to_bf16(ffn_w_down)
    streams = [grouped_ffn(h, w_gu, w_down, *single_expert(h.shape[0]), ln=(ln_g[0, 1], ln_b[0, 1]))
               for h in streams]

    oddp = (sc_conv_w[0], sc_conv_b[0], ssd_conv_w[0], ssd_conv_b[0], ssd_dt_bias[0],
            ssd_a_log[0], ssd_d[0], ssd_norm_g[0])
    w_in = w_in_odd[0].astype(bf16)
    zp, zs = [matmul([h], [w_in]).reshape(sh + (-1,)) for h, sh in zip(streams, shapes)]
    mix_p, scc_p, sdc_p, ssd_p = odd_mix(zp, jnp.zeros((bp, SC_WIDTH - 1, SC_DIM), f32),
                                         jnp.zeros((bp, SSD_CONV - 1, SSD_CONV_DIM), f32),
                                         None, SSD_CHUNK, *oddp)
    mix_s, scc_s, sdc_s, ssd_s = odd_mix(zs, state_sc_conv[0], state_ssd_conv[0], state_ssd[0], ts, *oddp)
    streams = [out_proj(mix, w_out_odd[0], SC_DIM, h, ln_g[1, 0], ln_b[1, 0])
               for mix, h in zip((mix_p, mix_s), streams)]
    h = jnp.concatenate(streams, axis=0)
    f = moe_ffn(h, moe_router[0], moe_router_b[0], to_bf16(moe_w_gu[0]), to_bf16(moe_w_down[0]))
    h = layer_norm(ALPHA * h + f, ln_g[1, 1], ln_b[1, 1])

    hp = h[:n_p].reshape(bp, tp, d)
    hs = h[n_p:].reshape(bs, ts, d)
    st = lambda a, ref: a[None].astype(ref.dtype)
    return (hp.astype(x_prompt.dtype), hs.astype(x_sample.dtype),
            st(s5_p, state_s5), st(s5_s, state_s5),
            st(kv_p, cache_nsa_kv), st(kv_s, cache_nsa_kv),
            st(win_p, state_win_kv), st(win_s, state_win_kv),
            st(scc_p, state_sc_conv), st(scc_s, state_sc_conv),
            st(sdc_p, state_ssd_conv), st(sdc_s, state_ssd_conv),
            st(ssd_p, state_ssd), st(ssd_s, state_ssd))
```

```python
import functools
import math

import jax
import jax.numpy as jnp
from jax import lax
from jax.experimental import pallas as pl
from jax.experimental.pallas import tpu as pltpu

D_MODEL = 1024
SEQ = 8192
DEPTH = 2
DEC_SEQ = 8
PAST_LEN = 16384
ALPHA = (2.0 * DEPTH) ** 0.25
LN_EPS = 1e-5
RMS_EPS = 1e-5
NEG = -1e30

S5_DIM = D_MODEL // 2
S5_GROUP = 16
S5_GROUPS = S5_DIM // S5_GROUP
S5_STATE = 64

HEAD_DIM = 64
N_HEADS = (D_MODEL // 2) // HEAD_DIM
KV_GROUPS = 2
HEADS_PER_GROUP = N_HEADS // KV_GROUPS
CMP_STRIDE = 16
CMP_LEN = 2 * CMP_STRIDE
SEL_BLOCK = 64
N_SEL = 16
WINDOW = 512
Q_BLOCK = 128
ROPE_THETA = 500000.0
ROT_DIM = HEAD_DIM // 4
FORCE = 1e4
NSA_Q = N_HEADS * HEAD_DIM
NSA_KV = 2 * KV_GROUPS * HEAD_DIM

SC_DIM = D_MODEL // 2
SC_WIDTH = 3
SSD_HEAD_DIM = 64
SSD_HEADS = 16
SSD_INNER = SSD_HEADS * SSD_HEAD_DIM
SSD_GROUPS = 4
SSD_STATE = 128
SSD_CONV = 4
SSD_CONV_DIM = SSD_INNER + 2 * SSD_GROUPS * SSD_STATE
SSD_CHUNK = 128

D_FF = 2816
N_EXPERTS = 8
TOP_K = 2

VMEM_LIMIT_BYTES = 56 * 1024 * 1024
LANES = 128
S5_N = S5_GROUPS * S5_STATE
S5_LT = S5_N // LANES
S5_CHUNK = 256
SEL_TILE = 1024
QK_SCALE = HEAD_DIM ** -0.5 * math.log2(math.e)
REMOVED = -3e38
PAGE_SIZE = 128
PAGES_PER_STEP = 32
NEW_PAD = 128
CAST_ROWS = 256
CAST_SPLIT = 4
PREP_ROWS = 256
TAIL = 8
ROW_TILE = 512
FF_TILE = D_FF // 2


def _cparams(*sem):
    return pltpu.CompilerParams(dimension_semantics=sem, vmem_limit_bytes=VMEM_LIMIT_BYTES)


def _deepnorm(resid, update, g, b):
    y = ALPHA * resid + update
    mu = jnp.mean(y, axis=-1, keepdims=True)
    yc = y - mu
    var = jnp.mean(yc * yc, axis=-1, keepdims=True)
    return yc * lax.rsqrt(var + LN_EPS) * g + b


def _mm_kernel(*refs, n_in, fuse_ln):
    xs, ws = refs[0:n_in], refs[n_in:2 * n_in]
    o_ref = refs[-1]
    acc = None
    for x_ref, w_ref in zip(xs, ws):
        part = jnp.dot(x_ref[...].astype(jnp.bfloat16), w_ref[...], preferred_element_type=jnp.float32)
        acc = part if acc is None else acc + part
    if fuse_ln:
        r_ref, g_ref, b_ref = refs[2 * n_in:2 * n_in + 3]
        acc = _deepnorm(r_ref[...], acc, g_ref[...], b_ref[...])
    o_ref[...] = acc


def matmul(xs, ws_bf16, ln=None):
    m = xs[0].shape[0]
    n = ws_bf16[0].shape[1]
    tile = min(ROW_TILE, m)
    row = lambda width: pl.BlockSpec((tile, width), lambda i: (i, 0))
    fixed = lambda shape: pl.BlockSpec(shape, lambda i: (0, 0), pipeline_mode=pl.Buffered(1))
    in_specs = [row(x.shape[1]) for x in xs] + [fixed(w.shape) for w in ws_bf16]
    args = list(xs) + list(ws_bf16)
    if ln is not None:
        resid, g, b = ln
        in_specs += [row(n), fixed((1, n)), fixed((1, n))]
        args += [resid, g.reshape(1, n), b.reshape(1, n)]
    return pl.pallas_call(
        functools.partial(_mm_kernel, n_in=len(xs), fuse_ln=ln is not None),
        grid=(pl.cdiv(m, tile),),
        in_specs=in_specs,
        out_specs=row(n),
        out_shape=jax.ShapeDtypeStruct((m, n), jnp.float32),
        compiler_params=_cparams("parallel"),
        name="matmul",
    )(*args)


def _cast_kernel(*refs):
    o_ref = refs[-1]
    o_ref[...] = jnp.concatenate([r[...].astype(o_ref.dtype) for r in refs[:-1]], axis=1)


def to_bf16(w):
    shape = w.shape
    w2 = w.reshape(-1, shape[-1])
    rows, cols = w2.shape
    split = CAST_SPLIT if cols % (CAST_SPLIT * LANES) == 0 else 1
    out = pl.pallas_call(
        _cast_kernel,
        grid=(pl.cdiv(rows, CAST_ROWS),),
        in_specs=[pl.BlockSpec((CAST_ROWS, cols // split), lambda i, c=c: (i, c)) for c in range(split)],
        out_specs=pl.BlockSpec((CAST_ROWS, cols), lambda i: (i, 0)),
        out_shape=jax.ShapeDtypeStruct((rows, cols), jnp.bfloat16),
        compiler_params=_cparams("parallel"),
        name="to_bf16",
    )(*([w2] * split))
    return out.reshape(shape)


def _ffn_kernel(te_ref, nt_ref, x_ref, wg_ref, wu_ref, wd_ref, *rest, fuse_ln):
    o_ref = rest[-1]
    t = pl.program_id(0)
    j = pl.program_id(1)

    @pl.when(t < nt_ref[0])
    def _():
        x = x_ref[...].astype(jnp.bfloat16)
        g = jnp.dot(x, wg_ref[...], preferred_element_type=jnp.float32)
        u = jnp.dot(x, wu_ref[...], preferred_element_type=jnp.float32)
        h = (g * jax.nn.sigmoid(g) * u).astype(jnp.bfloat16)
        part = jnp.dot(h, wd_ref[...], preferred_element_type=jnp.float32)

        @pl.when(j == 0)
        def _():
            o_ref[...] = part

        @pl.when(j > 0)
        def _():
            if fuse_ln:
                o_ref[...] = _deepnorm(x_ref[...], o_ref[...] + part, rest[0][...], rest[1][...])
            else:
                o_ref[...] += part

    @pl.when(jnp.logical_and(t >= nt_ref[0], j == 0))
    def _():
        o_ref[...] = jnp.zeros_like(o_ref)


def grouped_ffn(x, w_gu_bf16, w_down_bf16, tile_expert, n_tiles_used, ln=None):
    r, d = x.shape
    nf = D_FF // FF_TILE
    assert nf == 2
    tile = min(ROW_TILE, r)
    n_tiles = pl.cdiv(r, tile)
    in_specs = [
        pl.BlockSpec((tile, d), lambda t, j, te, nt: (t, 0)),
        pl.BlockSpec((None, d, FF_TILE), lambda t, j, te, nt: (te[t], 0, j)),
        pl.BlockSpec((None, d, FF_TILE), lambda t, j, te, nt: (te[t], 0, nf + j)),
        pl.BlockSpec((None, FF_TILE, d), lambda t, j, te, nt: (te[t], j, 0)),
    ]
    args = [tile_expert, n_tiles_used, x, w_gu_bf16, w_gu_bf16, w_down_bf16]
    if ln is not None:
        in_specs += [pl.BlockSpec((1, d), lambda t, j, te, nt: (0, 0))] * 2
        args += [ln[0].reshape(1, d), ln[1].reshape(1, d)]
    grid_spec = pltpu.PrefetchScalarGridSpec(
        num_scalar_prefetch=2,
        grid=(n_tiles, nf),
        in_specs=in_specs,
        out_specs=pl.BlockSpec((tile, d), lambda t, j, te, nt: (t, 0)),
    )
    return pl.pallas_call(
        functools.partial(_ffn_kernel, fuse_ln=ln is not None),
        grid_spec=grid_spec,
        out_shape=jax.ShapeDtypeStruct((r, d), jnp.float32),
        compiler_params=_cparams("parallel", "arbitrary"),
        name="grouped_ffn",
    )(*args)


def _s5_kernel(u_ref, h0r_ref, h0i_ref, ar_ref, ai_ref, bbr_ref, bbi_ref, cr_ref, ci_ref, d_ref, wglu_ref,
               y_ref, hro_ref, hio_ref, bur, bui, sr, si, hr, hi, *, chains, chunk):
    j = pl.program_id(0)

    @pl.when(j == 0)
    def _():
        hr[...] = h0r_ref[...]
        hi[...] = h0i_ref[...]

    u = u_ref[...].reshape(chains * chunk, S5_DIM)
    ub = u.astype(jnp.bfloat16)
    hd, hn = S5_DIM // 2, S5_N // 2

    def b_proj(w_ref):
        return jnp.concatenate([jnp.dot(ub[:, h * hd:(h + 1) * hd], w_ref[h * hd:(h + 1) * hd, h * hn:(h + 1) * hn],
                                        preferred_element_type=jnp.float32) for h in range(2)], axis=1)

    bu_r = b_proj(bbr_ref)
    bu_i = b_proj(bbi_ref)
    for k in range(S5_LT):
        bur[k] = bu_r[:, k * LANES:(k + 1) * LANES]
        bui[k] = bu_i[:, k * LANES:(k + 1) * LANES]
    ar = [jnp.broadcast_to(ar_ref[:, k * LANES:(k + 1) * LANES], (chains, LANES)) for k in range(S5_LT)]
    ai = [jnp.broadcast_to(ai_ref[:, k * LANES:(k + 1) * LANES], (chains, LANES)) for k in range(S5_LT)]

    def body(t, carry):
        rows = pl.ds(t, chains, stride=chunk)
        out = []
        for k in range(S5_LT):
            xr, xi = carry[2 * k], carry[2 * k + 1]
            nr = ar[k] * xr - ai[k] * xi + bur[k, rows, :]
            ni = ar[k] * xi + ai[k] * xr + bui[k, rows, :]
            sr[k, rows, :] = nr
            si[k, rows, :] = ni
            out += [nr, ni]
        return tuple(out)

    init = []
    for k in range(S5_LT):
        init += [hr[:, k * LANES:(k + 1) * LANES], hi[:, k * LANES:(k + 1) * LANES]]
    fin = lax.fori_loop(0, chunk, body, tuple(init), unroll=2)
    xr = jnp.concatenate(fin[0::2], axis=1)
    xi = jnp.concatenate(fin[1::2], axis=1)
    hr[...] = xr
    hi[...] = xi
    hro_ref[...] = xr
    hio_ref[...] = xi
    s_r = jnp.concatenate([sr[k] for k in range(S5_LT)], axis=1).astype(jnp.bfloat16)
    s_i = jnp.concatenate([si[k] for k in range(S5_LT)], axis=1).astype(jnp.bfloat16)
    y = jnp.concatenate(
        [jnp.dot(s_r[:, h * hn:(h + 1) * hn], cr_ref[h * hn:(h + 1) * hn, h * hd:(h + 1) * hd],
                 preferred_element_type=jnp.float32)
         - jnp.dot(s_i[:, h * hn:(h + 1) * hn], ci_ref[h * hn:(h + 1) * hn, h * hd:(h + 1) * hd],
                   preferred_element_type=jnp.float32) for h in range(2)], axis=1) + d_ref[...] * u
    z = jax.nn.gelu(y)
    gate = jax.nn.sigmoid(jnp.dot(z.astype(jnp.bfloat16), wglu_ref[...], preferred_element_type=jnp.float32))
    y_ref[...] = (z * gate).reshape(chains, chunk, S5_DIM)


def s5_params(lam_re, lam_im, log_dt, b, c, d, w_glu):
    f32 = jnp.float32
    dt = jnp.exp(log_dt.astype(f32))[:, None]
    mag = jnp.exp(lam_re * dt)
    ang = lam_im * dt
    ab_re = mag * jnp.cos(ang)
    ab_im = mag * jnp.sin(ang)
    den = lam_re * lam_re + lam_im * lam_im
    nr = ab_re - 1.0
    coef_re = (nr * lam_re + ab_im * lam_im) / den
    coef_im = (ab_im * lam_re - nr * lam_im) / den
    b_re = b[..., 0].astype(f32)
    b_im = b[..., 1].astype(f32)
    bb_re = coef_re[..., None] * b_re - coef_im[..., None] * b_im
    bb_im = coef_re[..., None] * b_im + coef_im[..., None] * b_re
    eye = jnp.eye(S5_GROUPS, dtype=f32)
    bbr = jnp.einsum('gnk,gh->gkhn', bb_re, eye).reshape(S5_DIM, S5_N).astype(jnp.bfloat16)
    bbi = jnp.einsum('gnk,gh->gkhn', bb_im, eye).reshape(S5_DIM, S5_N).astype(jnp.bfloat16)
    cr = jnp.einsum('gkn,gh->gnhk', c[..., 0].astype(f32), eye).reshape(S5_N, S5_DIM).astype(jnp.bfloat16)
    ci = jnp.einsum('gkn,gh->gnhk', c[..., 1].astype(f32), eye).reshape(S5_N, S5_DIM).astype(jnp.bfloat16)
    return (ab_re.reshape(1, S5_N), ab_im.reshape(1, S5_N), bbr, bbi, cr, ci,
            d.astype(f32).reshape(1, S5_DIM), w_glu.astype(jnp.bfloat16))


def s5_scan(u, h0, params, chunk):
    chains, t, _ = u.shape
    ar, ai, bbr, bbi, cr, ci, d, wglu = params
    h0r = h0[..., 0].reshape(chains, S5_N)
    h0i = h0[..., 1].reshape(chains, S5_N)
    full = lambda shape: pl.BlockSpec(shape, lambda j: (0,) * len(shape))
    rows = chains * chunk
    y, hr, hi = pl.pallas_call(
        functools.partial(_s5_kernel, chains=chains, chunk=chunk),
        grid=(t // chunk,),
        in_specs=[pl.BlockSpec((chains, chunk, S5_DIM), lambda j: (0, j, 0)),
                  full((chains, S5_N)), full((chains, S5_N)), full((1, S5_N)), full((1, S5_N)),
                  full((S5_DIM, S5_N)), full((S5_DIM, S5_N)), full((S5_N, S5_DIM)), full((S5_N, S5_DIM)),
                  full((1, S5_DIM)), full((S5_DIM, S5_DIM))],
        out_specs=[pl.BlockSpec((chains, chunk, S5_DIM), lambda j: (0, j, 0)),
                   full((chains, S5_N)), full((chains, S5_N))],
        out_shape=[jax.ShapeDtypeStruct((chains, t, S5_DIM), jnp.float32),
                   jax.ShapeDtypeStruct((chains, S5_N), jnp.float32),
                   jax.ShapeDtypeStruct((chains, S5_N), jnp.float32)],
        scratch_shapes=[pltpu.VMEM((S5_LT, rows, LANES), jnp.float32)] * 4
                       + [pltpu.VMEM((chains, S5_N), jnp.float32)] * 2,
        compiler_params=_cparams("arbitrary"),
        name="s5_scan",
    )(u, h0r, h0i, ar, ai, bbr, bbi, cr, ci, d, wglu)
    new_state = jnp.stack([hr.reshape(chains, S5_GROUPS, S5_STATE), hi.reshape(chains, S5_GROUPS, S5_STATE)],
                          axis=-1)
    return y, new_state


def _dot_nt(a, b):
    return lax.dot_general(a, b, (((1,), (1,)), ((), ())), preferred_element_type=jnp.float32)


def _split3(x):
    hi = x.astype(jnp.bfloat16)
    rem = x - hi.astype(jnp.float32)
    mid = rem.astype(jnp.bfloat16)
    lo = (rem - mid.astype(jnp.float32)).astype(jnp.bfloat16)
    return hi, mid, lo


def _softmax_rows(s, mask):
    s = jnp.where(mask, s, NEG)
    m = jnp.max(s, axis=-1, keepdims=True)
    p = jnp.exp2(s - m)
    inv = jnp.where(m > 0.5 * NEG, 1.0 / jnp.sum(p, axis=-1, keepdims=True), 0.0)
    return p * inv


def _nsa_prompt_kernel(q_ref, gate_ref, kc_ref, vc_ref, ks_ref, vs_ref, kw_ref, vw_ref, o_ref, *, n_cmp, n_blk):
    f32, bf16 = jnp.float32, jnp.bfloat16
    r4 = HEADS_PER_GROUP
    n_cpad = kc_ref.shape[0]
    start = pl.program_id(1) * Q_BLOCK
    q = q_ref[...] * QK_SCALE
    gate = gate_ref[...]
    lane = lax.broadcasted_iota(jnp.int32, (Q_BLOCK, LANES), 1)
    qpos = start + lax.broadcasted_iota(jnp.int32, (Q_BLOCK, 1), 0)
    n_idx = lax.broadcasted_iota(jnp.int32, (Q_BLOCK, n_cpad), 1)
    cmask = (((n_idx * CMP_STRIDE + (CMP_LEN - 1)) <= qpos) & (n_idx < n_cmp))[None]
    ratio = SEL_BLOCK // CMP_STRIDE
    gsum = (lax.broadcasted_iota(jnp.int32, (n_blk, n_cpad), 1) // ratio
            == lax.broadcasted_iota(jnp.int32, (n_blk, n_cpad), 0)).astype(bf16)
    blk = lax.broadcasted_iota(jnp.int32, (n_blk, Q_BLOCK), 0)
    blk_f = blk.astype(f32)
    jq = (start + lax.broadcasted_iota(jnp.int32, (n_blk, Q_BLOCK), 1)) // SEL_BLOCK
    force = jnp.where((blk == 0) | (blk == jq) | (blk == jq - 1), FORCE, 0.0)
    qgs, o_cs, sels = [], [], []
    for g in range(KV_GROUPS):
        keep = (lane < HEAD_DIM) if g == 0 else (lane >= HEAD_DIM)
        parts = []
        for r in range(r4):
            h = r4 * g + r
            tile = q[:, (h // 2) * LANES:(h // 2 + 1) * LANES]
            if h % 2 != g:
                tile = pltpu.roll(tile, HEAD_DIM, axis=1)
            parts.append(jnp.where(keep, tile, 0.0))
        qg = jnp.concatenate(parts, axis=0).astype(bf16)
        qgs.append(qg)

        p_c = _softmax_rows(_dot_nt(qg, kc_ref[...]).reshape(r4, Q_BLOCK, n_cpad), cmask)
        o_cs.append(jnp.dot(p_c.reshape(r4 * Q_BLOCK, n_cpad).astype(bf16), vc_ref[...],
                            preferred_element_type=f32).reshape(r4, Q_BLOCK, LANES))
        psum = p_c[0] + p_c[1] + p_c[2] + p_c[3]
        imp_t = sum(_dot_nt(gsum, part) for part in _split3(psum))

        score = jnp.where(blk <= jq, imp_t + force, NEG)
        sel_t = jnp.zeros((n_blk, Q_BLOCK), f32)
        for _ in range(min(N_SEL, n_blk)):
            m = jnp.max(score, axis=0, keepdims=True)
            idx = jnp.min(jnp.where(score == m, blk_f, float(n_blk)), axis=0, keepdims=True)
            hit = blk_f == idx
            sel_t = jnp.where(hit & (m > 0.5 * NEG), 1.0, sel_t)
            score = jnp.where(hit, REMOVED, score)
        sels.append(sel_t.T)

    n_full = start // SEL_TILE
    expand0 = (lax.broadcasted_iota(jnp.int32, (n_blk, SEL_TILE), 0)
               == lax.broadcasted_iota(jnp.int32, (n_blk, SEL_TILE), 1) // SEL_BLOCK).astype(bf16)

    def tile_update(i, carry, causal):
        off = pl.multiple_of(i * SEL_TILE, SEL_TILE)
        k = ks_ref[pl.ds(off, SEL_TILE), :]
        v = vs_ref[pl.ds(off, SEL_TILE), :]
        out = []
        for g in range(KV_GROUPS):
            m_run, l_run, acc = carry[g]
            s_t = _dot_nt(qgs[g], k).reshape(r4, Q_BLOCK, SEL_TILE)
            shifted = pltpu.roll(sels[g], (n_blk - i * (SEL_TILE // SEL_BLOCK)) % n_blk, axis=1).astype(bf16)
            mk = jnp.dot(shifted, expand0, preferred_element_type=f32) > 0.5
            if causal:
                kpos = i * SEL_TILE + lax.broadcasted_iota(jnp.int32, (Q_BLOCK, SEL_TILE), 1)
                mk = mk & (kpos <= qpos)
            s_t = jnp.where(mk[None], s_t, NEG)
            m_new = jnp.maximum(m_run, jnp.max(s_t, axis=-1, keepdims=True))
            alpha = jnp.exp2(m_run - m_new)
            p = jnp.exp2(s_t - m_new)
            l_new = alpha * l_run + jnp.sum(p, axis=-1, keepdims=True)
            pv = jnp.dot(p.reshape(r4 * Q_BLOCK, SEL_TILE).astype(bf16), v, preferred_element_type=f32)
            out.append((m_new, l_new, alpha * acc + pv.reshape(r4, Q_BLOCK, LANES)))
        return tuple(out)

    init = (jnp.full((r4, Q_BLOCK, 1), NEG, f32), jnp.zeros((r4, Q_BLOCK, 1), f32),
            jnp.zeros((r4, Q_BLOCK, LANES), f32))
    carry = lax.fori_loop(0, n_full, lambda i, c: tile_update(i, c, False), (init, init))
    fin = tile_update(n_full, carry, True)

    n_win = WINDOW + Q_BLOCK
    woff = pl.multiple_of(start, Q_BLOCK)
    kwin = kw_ref[pl.ds(woff, n_win), :]
    vwin = vw_ref[pl.ds(woff, n_win), :]
    wpos = start - WINDOW + lax.broadcasted_iota(jnp.int32, (Q_BLOCK, n_win), 1)
    wmask = ((wpos <= qpos) & (wpos > qpos - WINDOW) & (wpos >= 0))[None]
    heads = [None] * N_HEADS
    for g in range(KV_GROUPS):
        m_fin, l_fin, acc = fin[g]
        o_s = acc * jnp.where(m_fin > 0.5 * NEG, 1.0 / l_fin, 0.0)
        p_w = _softmax_rows(_dot_nt(qgs[g], kwin).reshape(r4, Q_BLOCK, n_win), wmask)
        o_w = jnp.dot(p_w.reshape(r4 * Q_BLOCK, n_win).astype(bf16), vwin,
                      preferred_element_type=f32).reshape(r4, Q_BLOCK, LANES)
        for r in range(r4):
            h = r4 * g + r
            heads[h] = (gate[:, 3 * h:3 * h + 1] * o_cs[g][r] + gate[:, 3 * h + 1:3 * h + 2] * o_s[r]
                        + gate[:, 3 * h + 2:3 * h + 3] * o_w[r])

    tiles = []
    for j in range(N_HEADS // 2):
        even, odd = heads[2 * j], heads[2 * j + 1]
        if j // 2 == 0:
            tiles.append(jnp.where(lane < HEAD_DIM, even, pltpu.roll(odd, HEAD_DIM, axis=1)))
        else:
            tiles.append(jnp.where(lane < HEAD_DIM, pltpu.roll(even, HEAD_DIM, axis=1), odd))
    o_ref[...] = jnp.concatenate(tiles, axis=1)


def nsa_prompt(q, gates, kc, vc, ks, vs, kw_pad, vw_pad):
    b, t, _ = q.shape
    n_cpad = kc.shape[1]
    kern = functools.partial(_nsa_prompt_kernel, n_cmp=t // CMP_STRIDE - 1, n_blk=t // SEL_BLOCK)
    whole = lambda rows: pl.BlockSpec((None, rows, LANES), lambda i, j: (i, 0, 0))
    return pl.pallas_call(
        kern,
        grid=(b, t // Q_BLOCK),
        in_specs=[pl.BlockSpec((None, Q_BLOCK, NSA_Q), lambda i, j: (i, j, 0)),
                  pl.BlockSpec((None, Q_BLOCK, 3 * N_HEADS), lambda i, j: (i, j, 0)),
                  whole(n_cpad), whole(n_cpad), whole(t), whole(t), whole(t + WINDOW), whole(t + WINDOW)],
        out_specs=pl.BlockSpec((None, Q_BLOCK, NSA_Q), lambda i, j: (i, j, 0)),
        out_shape=jax.ShapeDtypeStruct((b, t, NSA_Q), jnp.float32),
        compiler_params=_cparams("parallel", "arbitrary"),
        name="nsa_prompt",
    )(q, gates, kc, vc, ks, vs, kw_pad, vw_pad)


def _compress_kernel(ch_ref, pet_ref, peb_ref, w1t_ref, w1b_ref, w2_ref, o_ref):
    bf16 = jnp.bfloat16
    ch = ch_ref[...]
    n_ch = ch.shape[0]
    a = jnp.dot((ch + pet_ref[...]).astype(bf16), w1t_ref[...], preferred_element_type=jnp.float32)
    b = jnp.dot((ch + peb_ref[...]).astype(bf16), w1b_ref[...], preferred_element_type=jnp.float32)
    pre = a + pltpu.roll(b, n_ch - 1, axis=0)
    o_ref[...] = jnp.dot(jax.nn.gelu(pre).astype(bf16), w2_ref[...],
                         preferred_element_type=jnp.float32).astype(o_ref.dtype)


def compress_params(w1, w2, pe):
    f32 = jnp.float32
    eye = jnp.eye(KV_GROUPS, dtype=f32)
    w1r = w1.astype(f32).reshape(2, CMP_STRIDE, HEAD_DIM, HEAD_DIM)
    big = jnp.einsum('hjde,gk->hjgdke', w1r, eye).reshape(2, CMP_STRIDE * LANES, LANES).astype(jnp.bfloat16)
    w2bd = jnp.einsum('de,gk->gdke', w2.astype(f32), eye).reshape(LANES, LANES).astype(jnp.bfloat16)
    per = pe.astype(f32).reshape(2, CMP_STRIDE, 1, HEAD_DIM)
    pe_rows = jnp.broadcast_to(per, (2, CMP_STRIDE, KV_GROUPS, HEAD_DIM)).reshape(2, 1, CMP_STRIDE * LANES)
    return pe_rows[0], pe_rows[1], big[0], big[1], w2bd


def compress_prompt(x, params):
    b, t, _ = x.shape
    n_ch = t // CMP_STRIDE
    ch = x.reshape(b, n_ch, CMP_STRIDE * LANES)
    pet, peb, w1t, w1b, w2bd = params
    full = lambda shape: pl.BlockSpec(shape, lambda i: (0,) * len(shape))
    return pl.pallas_call(
        _compress_kernel,
        grid=(b,),
        in_specs=[pl.BlockSpec((None, n_ch, CMP_STRIDE * LANES), lambda i: (i, 0, 0)),
                  full((1, CMP_STRIDE * LANES)), full((1, CMP_STRIDE * LANES)),
                  full((CMP_STRIDE * LANES, LANES)), full((CMP_STRIDE * LANES, LANES)), full((LANES, LANES))],
        out_specs=pl.BlockSpec((None, n_ch, LANES), lambda i: (i, 0, 0)),
        out_shape=jax.ShapeDtypeStruct((b, n_ch, LANES), jnp.bfloat16),
        compiler_params=_cparams("parallel"),
        name="compress_prompt",
    )(ch, pet, peb, w1t, w1b, w2bd)


def _cmp_sample_kernel(pt_ref, *refs, n_pages):
    f32, bf16 = jnp.float32, jnp.bfloat16
    pp = PAGES_PER_STEP
    pages = refs[0:pp]
    (perm_ref, newk_ref, newv_ref, wk_ref, wv_ref, ck_ref, cv_ref, w2k_ref, w2v_ref,
     kc_ref, vc_ref, slab_k, slab_v) = refs[pp:]
    s = pl.program_id(1)
    cpp = PAGE_SIZE // CMP_STRIDE
    base = pl.multiple_of(s * (pp * cpp), pp * cpp)
    for half, slab in enumerate((slab_k, slab_v)):
        for i in range(pp):
            page = pages[i][half * LANES:(half + 1) * LANES, :].astype(bf16)
            rows = _dot_nt(perm_ref[...], page)
            for j in range(CMP_STRIDE):
                slab[j, pl.ds(base + i * cpp, cpp), :] = rows[j * cpp:(j + 1) * cpp, :]

    @pl.when(s == pl.num_programs(1) - 1)
    def _():
        n_ch = n_pages * (PAGE_SIZE // CMP_STRIDE)
        row = lax.broadcasted_iota(jnp.int32, (n_ch, LANES), 0)
        for slab, new_ref, w_ref, c_ref, w2_ref, o_ref in ((slab_k, newk_ref, wk_ref, ck_ref, w2k_ref, kc_ref),
                                                           (slab_v, newv_ref, wv_ref, cv_ref, w2v_ref, vc_ref)):
            ch = jnp.concatenate([slab[j] for j in range(CMP_STRIDE)], axis=1).astype(bf16)
            ab = jnp.dot(ch, w_ref[...], preferred_element_type=f32)
            b_new = jnp.dot(new_ref[...].astype(bf16), w_ref[...], preferred_element_type=f32)[0:1, LANES:]
            nxt = pltpu.roll(ab[:, LANES:], n_ch - 1, axis=0)
            nxt = jnp.where(row == n_ch - 1, b_new, nxt)
            pre = ab[:, :LANES] + nxt + c_ref[...]
            o_ref[...] = jnp.dot(jax.nn.gelu(pre).astype(bf16), w2_ref[...],
                                 preferred_element_type=f32).astype(o_ref.dtype)


def compress_sample_params(w1, w2, pe):
    pet, peb, w1t, w1b, w2bd = compress_params(w1, w2, pe)
    hp = lax.Precision.HIGHEST
    const = (jnp.dot(pet, w1t.astype(jnp.float32), precision=hp)
             + jnp.dot(peb, w1b.astype(jnp.float32), precision=hp))
    return jnp.concatenate([w1t, w1b], axis=1), const, w2bd


def _page_spec(i, pair):
    return pl.BlockSpec((None, 2 * LANES, PAGE_SIZE),
                        lambda b, s, pt: (pt[b, PAGES_PER_STEP * s + i], pair, 0))


def _per_seq(shape):
    return pl.BlockSpec((None,) + shape, lambda b, s, pt: (b, 0, 0))


def compress_sample(pool_t, page_table, new_k, new_v, pk, pv):
    bsz, n_pages = page_table.shape
    pp = PAGES_PER_STEP
    n_ch = n_pages * (PAGE_SIZE // CMP_STRIDE)
    t_new = new_k.shape[1]

    def chunk_rows(x):
        x = jnp.pad(x, ((0, 0), (0, CMP_STRIDE - t_new), (0, 0))).reshape(bsz, 1, CMP_STRIDE * LANES)
        return jnp.pad(x, ((0, 0), (0, 7), (0, 0)))

    full = lambda shape: pl.BlockSpec(shape, lambda b, s, pt: (0,) * len(shape))
    r = jnp.arange(PAGE_SIZE)
    cpp = PAGE_SIZE // CMP_STRIDE
    perm = (r[None, :] == (r[:, None] % cpp) * CMP_STRIDE + r[:, None] // cpp).astype(jnp.bfloat16)
    wk, ck, w2k = pk
    wv, cv, w2v = pv
    grid_spec = pltpu.PrefetchScalarGridSpec(
        num_scalar_prefetch=1,
        grid=(bsz, n_pages // pp),
        in_specs=[_page_spec(i, 0) for i in range(pp)]
                 + [full((PAGE_SIZE, PAGE_SIZE)), _per_seq((8, CMP_STRIDE * LANES)),
                    _per_seq((8, CMP_STRIDE * LANES)),
                    full((CMP_STRIDE * LANES, 2 * LANES)), full((CMP_STRIDE * LANES, 2 * LANES)),
                    full((1, LANES)), full((1, LANES)), full((LANES, LANES)), full((LANES, LANES))],
        out_specs=[_per_seq((n_ch, LANES)), _per_seq((n_ch, LANES))],
        scratch_shapes=[pltpu.VMEM((CMP_STRIDE, n_ch, LANES), jnp.float32)] * 2,
    )
    return pl.pallas_call(
        functools.partial(_cmp_sample_kernel, n_pages=n_pages),
        grid_spec=grid_spec,
        out_shape=[jax.ShapeDtypeStruct((bsz, n_ch, LANES), jnp.bfloat16)] * 2,
        compiler_params=_cparams("parallel", "arbitrary"),
        name="compress_sample",
    )(page_table, *([pool_t] * pp), perm, chunk_rows(new_k), chunk_rows(new_v), wk, wv, ck, cv, w2k, w2v)


def _nsa_sample_kernel(pt_ref, *refs, n_pages, t_new, w_buf):
    f32, bf16 = jnp.float32, jnp.bfloat16
    pp = PAGES_PER_STEP
    q_ref, gate_ref, kc_ref, vc_ref = refs[0:4]
    pages = refs[4:4 + pp]
    (ksn_ref, vsn_ref, win_ref, kwn_ref, vwn_ref, o_ref,
     sel_scr, exp_scr, oc_scr, m_scr, l_scr, acc_scr) = refs[4 + pp:]
    r4, g2 = HEADS_PER_GROUP, KV_GROUPS
    n_rows = g2 * r4 * t_new
    past_len = n_pages * PAGE_SIZE
    n_cmp = kc_ref.shape[0]
    n_bpad = sel_scr.shape[1]
    tile = pp * PAGE_SIZE
    s = pl.program_id(1)
    qall = q_ref[...]
    qpos = past_len + lax.broadcasted_iota(jnp.int32, (n_rows, 1), 0) % t_new

    def grouped(x):
        return x.reshape(g2, 1, t_new, x.shape[-1])

    @pl.when(s == 0)
    def _():
        s_c = _dot_nt(qall, kc_ref[...])
        n_idx = lax.broadcasted_iota(jnp.int32, (n_rows, n_cmp), 1)
        p_c = _softmax_rows(s_c, (n_idx * CMP_STRIDE + (CMP_LEN - 1)) <= qpos)
        oc_scr[...] = jnp.dot(p_c.astype(bf16), vc_ref[...], preferred_element_type=f32)
        psum = jnp.sum(p_c.reshape(g2, r4, t_new, n_cmp), axis=1).reshape(g2 * t_new, n_cmp)
        psum = jnp.concatenate([psum, jnp.zeros((LANES - g2 * t_new, n_cmp), f32)], axis=0)
        p_hi = psum.astype(bf16)
        rem = psum - p_hi.astype(f32)
        p_mid = rem.astype(bf16)
        p_lo = (rem - p_mid.astype(f32)).astype(bf16)
        ratio = SEL_BLOCK // CMP_STRIDE
        gsum = (lax.broadcasted_iota(jnp.int32, (n_bpad, n_cmp), 1) // ratio
                == lax.broadcasted_iota(jnp.int32, (n_bpad, n_cmp), 0)).astype(bf16)
        imp_t = _dot_nt(gsum, p_hi) + _dot_nt(gsum, p_mid) + _dot_nt(gsum, p_lo)
        blk = lax.broadcasted_iota(jnp.int32, (n_bpad, LANES), 0)
        jq = (past_len + lax.broadcasted_iota(jnp.int32, (n_bpad, LANES), 1) % t_new) // SEL_BLOCK
        forced = (blk == 0) | (blk == jq) | (blk == jq - 1)
        score = jnp.where(blk <= jq, imp_t + jnp.where(forced, FORCE, 0.0), NEG)
        blk_f = blk.astype(f32)
        sel_t = jnp.zeros((n_bpad, LANES), f32)
        for _ in range(N_SEL):
            m = jnp.max(score, axis=0, keepdims=True)
            idx = jnp.min(jnp.where(score == m, blk_f, float(n_bpad)), axis=0, keepdims=True)
            hit = blk_f == idx
            sel_t = jnp.where(hit & (m > 0.5 * NEG), 1.0, sel_t)
            score = jnp.where(hit, REMOVED, score)
        sel = jnp.concatenate([sel_t[k * LANES:(k + 1) * LANES].T for k in range(n_bpad // LANES)], axis=1)
        sel_scr[...] = sel[0:g2 * t_new]
        exp_scr[...] = (lax.broadcasted_iota(jnp.int32, (LANES, tile), 0)
                        == lax.broadcasted_iota(jnp.int32, (LANES, tile), 1) // SEL_BLOCK).astype(bf16)
        m_scr[...] = jnp.full(m_scr.shape, NEG, f32)
        l_scr[...] = jnp.zeros(l_scr.shape, f32)
        acc_scr[...] = jnp.zeros(acc_scr.shape, f32)

    def online_update(s_t, mk, v, v_feature_major):
        n = s_t.shape[-1]
        s4 = jnp.where(mk, s_t.reshape(g2, r4, t_new, n), NEG)
        m_run = m_scr[...].reshape(g2, r4, t_new, 1)
        m_new = jnp.maximum(m_run, jnp.max(s4, axis=-1, keepdims=True))
        alpha = jnp.exp2(m_run - m_new)
        p = jnp.exp2(s4 - m_new)
        l_new = alpha * l_scr[...].reshape(g2, r4, t_new, 1) + jnp.sum(p, axis=-1, keepdims=True)
        pb = p.reshape(n_rows, n).astype(bf16)
        pv = _dot_nt(pb, v) if v_feature_major else jnp.dot(pb, v, preferred_element_type=f32)
        m_scr[...] = m_new.reshape(n_rows, 1)
        l_scr[...] = l_new.reshape(n_rows, 1)
        acc_scr[...] = alpha.reshape(n_rows, 1) * acc_scr[...] + pv

    kt = jnp.concatenate([r[0:LANES, :] for r in pages], axis=1).astype(bf16)
    vt = jnp.concatenate([r[LANES:2 * LANES, :] for r in pages], axis=1).astype(bf16)
    shifted = pltpu.roll(sel_scr[...], (n_bpad - s * (tile // SEL_BLOCK)) % n_bpad, axis=1)
    picked = jnp.dot(shifted[:, 0:LANES].astype(bf16), exp_scr[...], preferred_element_type=f32)
    online_update(jnp.dot(qall, kt, preferred_element_type=f32), grouped(picked) > 0.5, vt, True)

    @pl.when(s == pl.num_programs(1) - 1)
    def _():
        new_blk = past_len // SEL_BLOCK
        kidx = lax.broadcasted_iota(jnp.int32, (n_rows, NEW_PAD), 1)
        causal = ((past_len + kidx) <= qpos) & (kidx < t_new)
        picked_new = sel_scr[:, new_blk:new_blk + 1]
        mk = (grouped(picked_new) > 0.5) & causal.reshape(g2, r4, t_new, NEW_PAD)
        online_update(_dot_nt(qall, ksn_ref[...]), mk, vsn_ref[...], False)
        o_s = acc_scr[...] * jnp.where(m_scr[...] > 0.5 * NEG, 1.0 / l_scr[...], 0.0)

        n_win = w_buf + NEW_PAD
        kw_t = win_ref[0:LANES, :].astype(bf16)
        vw_t = win_ref[LANES:2 * LANES, :].astype(bf16)
        widx = lax.broadcasted_iota(jnp.int32, (n_rows, n_win), 1)
        wpos = past_len - w_buf + widx
        wmask = (wpos <= qpos) & (wpos > qpos - WINDOW) & (wpos >= 0) & (widx < w_buf + t_new)
        s_w = jnp.concatenate([jnp.dot(qall, kw_t, preferred_element_type=f32), _dot_nt(qall, kwn_ref[...])],
                              axis=1)
        p_w = _softmax_rows(s_w, wmask).astype(bf16)
        o_w = (_dot_nt(p_w[:, 0:w_buf], vw_t)
               + jnp.dot(p_w[:, w_buf:], vwn_ref[...], preferred_element_type=f32))
        gate = gate_ref[...]
        o_ref[...] = gate[:, 0:1] * oc_scr[...] + gate[:, 1:2] * o_s + gate[:, 2:3] * o_w


def nsa_sample(q, gates, kc, vc, pool_t, page_table, ks_new, vs_new, win, kw_new, vw_new):
    f32, bf16 = jnp.float32, jnp.bfloat16
    bsz, t_new = q.shape[0], q.shape[1]
    n_pages = page_table.shape[1]
    pp = PAGES_PER_STEP
    w_buf = win.shape[2]
    r4, g2 = HEADS_PER_GROUP, KV_GROUPS
    n_rows = g2 * r4 * t_new
    past_len = n_pages * PAGE_SIZE
    assert past_len % SEL_BLOCK == 0 and t_new <= SEL_BLOCK and past_len >= w_buf and n_pages % pp == 0
    n_sel = past_len // SEL_BLOCK + 1
    n_bpad = -(-n_sel // LANES) * LANES
    eye = jnp.eye(g2, dtype=f32)
    qg = q.reshape(bsz, t_new, g2, r4, HEAD_DIM).transpose(0, 2, 3, 1, 4) * QK_SCALE
    qall = jnp.einsum('bgrqd,gk->bgrqkd', qg, eye).reshape(bsz, n_rows, LANES).astype(bf16)
    gall = gates.reshape(bsz, t_new, g2, r4, 3).transpose(0, 2, 3, 1, 4).reshape(bsz, n_rows, 3)
    pad_rows = lambda x: jnp.pad(x, ((0, 0), (0, NEW_PAD - t_new), (0, 0))).astype(bf16)
    n_cmp = kc.shape[1]
    grid_spec = pltpu.PrefetchScalarGridSpec(
        num_scalar_prefetch=1,
        grid=(bsz, n_pages // pp),
        in_specs=[_per_seq((n_rows, LANES)), _per_seq((n_rows, 3)), _per_seq((n_cmp, LANES)),
                  _per_seq((n_cmp, LANES))]
                 + [_page_spec(i, 1) for i in range(pp)]
                 + [_per_seq((NEW_PAD, LANES)), _per_seq((NEW_PAD, LANES)), _per_seq((2 * LANES, w_buf)),
                    _per_seq((NEW_PAD, LANES)), _per_seq((NEW_PAD, LANES))],
        out_specs=_per_seq((n_rows, LANES)),
        scratch_shapes=[pltpu.VMEM((g2 * t_new, n_bpad), f32), pltpu.VMEM((LANES, pp * PAGE_SIZE), bf16),
                        pltpu.VMEM((n_rows, LANES), f32),
                        pltpu.VMEM((n_rows, 1), f32), pltpu.VMEM((n_rows, 1), f32),
                        pltpu.VMEM((n_rows, LANES), f32)],
    )
    o = pl.pallas_call(
        functools.partial(_nsa_sample_kernel, n_pages=n_pages, t_new=t_new, w_buf=w_buf),
        grid_spec=grid_spec,
        out_shape=jax.ShapeDtypeStruct((bsz, n_rows, LANES), f32),
        compiler_params=_cparams("parallel", "arbitrary"),
        name="nsa_sample",
    )(page_table, qall, gall, kc, vc, *([pool_t] * pp), pad_rows(ks_new), pad_rows(vs_new), win,
      pad_rows(kw_new), pad_rows(vw_new))
    o = jnp.einsum('bgrqkd,gk->bqgrd', o.reshape(bsz, g2, r4, t_new, g2, HEAD_DIM), eye)
    return o.reshape(bsz, t_new, NSA_Q)


def _ssd_kernel(x_ref, b_ref, c_ref, dt_ref, a_ref, za_ref, zb_ref, dskip_ref, ng_ref, y_ref, hout_ref, h_scr, *,
                chunk):
    f32, bf16 = jnp.float32, jnp.bfloat16
    n_l = chunk
    hpg = SSD_HEADS // SSD_GROUPS
    gw = hpg * SSD_HEAD_DIM
    j = pl.program_id(1)

    @pl.when(j == 0)
    def _():
        h_scr[...] = jnp.zeros(h_scr.shape, f32)

    x = x_ref[...]
    dt = dt_ref[...]
    tri_b = (lax.broadcasted_iota(jnp.int32, (n_l, n_l), 0) >= lax.broadcasted_iota(jnp.int32, (n_l, n_l), 1))
    tri = tri_b.astype(bf16)
    cum = sum(jnp.dot(tri, part, preferred_element_type=f32) for part in _split3(dt * a_ref[...]))
    cum_t = cum.T
    dt_t = dt.T
    ecum = jnp.exp(cum)
    clast = cum[n_l - 1:n_l, :]
    wt = jnp.exp(clast - cum) * dt
    elast = jnp.exp(clast)
    lane = lax.broadcasted_iota(jnp.int32, (n_l, LANES), 1)
    low = lane < SSD_HEAD_DIM

    def pair(v, h0):
        return jnp.where(low[:v.shape[0]], v[:, h0:h0 + 1], v[:, h0 + 1:h0 + 2])

    tiles = []
    for g in range(SSD_GROUPS):
        bg = b_ref[:, g * SSD_STATE:(g + 1) * SSD_STATE]
        cgb = c_ref[:, g * SSD_STATE:(g + 1) * SSD_STATE].astype(bf16)
        bgt = bg.T.astype(bf16)
        cb = jnp.dot(cgb, bgt, preferred_element_type=f32)
        hg = h_scr[g]
        y_inter = jnp.dot(cgb, hg.astype(bf16), preferred_element_type=f32)
        xw, dec = [], []
        for pr in range(hpg // 2):
            h0 = hpg * g + 2 * pr
            xt = x[:, (h0 // 2) * LANES:(h0 // 2 + 1) * LANES]
            acc = None
            for k in range(2):
                h = h0 + k
                seg = cum[:, h:h + 1] - cum_t[h:h + 1, :]
                w = cb * jnp.exp(jnp.where(tri_b, seg, NEG)) * dt_t[h:h + 1, :]
                xm = jnp.where(low if k == 0 else jnp.logical_not(low), xt, 0.0).astype(bf16)
                part = jnp.dot(w.astype(bf16), xm, preferred_element_type=f32)
                acc = part if acc is None else acc + part
            tiles.append(acc + y_inter[:, pr * LANES:(pr + 1) * LANES] * pair(ecum, h0))
            xw.append((xt * pair(wt, h0)).astype(bf16))
            dec.append(pair(elast, h0))
        h_scr[g] = (hg * jnp.concatenate(dec, axis=1)
                    + jnp.dot(bgt, jnp.concatenate(xw, axis=1), preferred_element_type=f32))
    y = jnp.concatenate(tiles, axis=1) + dskip_ref[...] * x
    zg = jnp.concatenate([za_ref[...], zb_ref[...]], axis=1)
    v = y * (zg * jax.nn.sigmoid(zg))
    outs = []
    for g in range(SSD_GROUPS):
        vg = v[:, g * gw:(g + 1) * gw]
        outs.append(vg * lax.rsqrt(jnp.mean(vg * vg, axis=-1, keepdims=True) + RMS_EPS))
    y_ref[...] = jnp.concatenate(outs, axis=1) * ng_ref[...]

    @pl.when(j == pl.num_programs(1) - 1)
    def _():
        hout_ref[...] = h_scr[...]


def ssd_prompt(xbc, dt, a, z, zg_offset, d_skip, norm_g):
    f32 = jnp.float32
    bsz, t, _ = xbc.shape
    hpg = SSD_HEADS // SSD_GROUPS
    gn = SSD_GROUPS * SSD_STATE
    half = SSD_INNER // 2
    assert zg_offset % half == 0
    dt_p = jnp.pad(dt, ((0, 0), (0, 0), (0, LANES - SSD_HEADS)))
    a_p = jnp.pad(a.astype(f32), (0, LANES - SSD_HEADS)).reshape(1, LANES)
    dsk = jnp.repeat(d_skip.astype(f32), SSD_HEAD_DIM).reshape(1, SSD_INNER)
    blk = lambda w, c: pl.BlockSpec((None, SSD_CHUNK, w), lambda b, j: (b, j, c))
    full = lambda shape: pl.BlockSpec(shape, lambda b, j: (0,) * len(shape))
    state_spec = pl.BlockSpec((None, SSD_GROUPS, SSD_STATE, hpg * SSD_HEAD_DIM), lambda b, j: (b, 0, 0, 0))
    y, h = pl.pallas_call(
        functools.partial(_ssd_kernel, chunk=SSD_CHUNK),
        grid=(bsz, t // SSD_CHUNK),
        in_specs=[blk(SSD_INNER, 0), blk(gn, SSD_INNER // gn), blk(gn, SSD_INNER // gn + 1), blk(LANES, 0),
                  full((1, LANES)), blk(half, zg_offset // half), blk(half, zg_offset // half + 1),
                  full((1, SSD_INNER)), full((1, SSD_INNER))],
        out_specs=[blk(SSD_INNER, 0), state_spec],
        out_shape=[jax.ShapeDtypeStruct((bsz, t, SSD_INNER), f32),
                   jax.ShapeDtypeStruct((bsz, SSD_GROUPS, SSD_STATE, hpg * SSD_HEAD_DIM), f32)],
        scratch_shapes=[pltpu.VMEM((SSD_GROUPS, SSD_STATE, hpg * SSD_HEAD_DIM), f32)],
        compiler_params=_cparams("parallel", "arbitrary"),
        name="ssd_prompt",
    )(xbc, xbc, xbc, dt_p, a_p, z, z, dsk, norm_g.astype(f32).reshape(1, SSD_INNER))
    h = h.reshape(bsz, SSD_GROUPS, SSD_STATE, hpg, SSD_HEAD_DIM).transpose(0, 1, 3, 4, 2)
    return y, h.reshape(bsz, SSD_HEADS, SSD_HEAD_DIM, SSD_STATE)


def _causal_conv_tile(x, tail, w_ref, b_ref, width):
    row = lax.broadcasted_iota(jnp.int32, (TAIL, x.shape[1]), 0)
    acc = b_ref[...] + w_ref[width - 1:width, :] * x
    for k in range(1, width):
        xs = pltpu.roll(x, k, axis=0)
        head = jnp.where(row < k, pltpu.roll(tail, k, axis=0), xs[0:TAIL])
        xs = jnp.concatenate([head, xs[TAIL:]], axis=0)
        acc = acc + w_ref[width - 1 - k:width - k, :] * xs
    return acc


def _odd_prep_kernel(z_ref, scw_ref, scb_ref, cvw_ref, cvb_ref, dtb_ref,
                     ysc_ref, xbc_ref, dt_ref, tsc_ref, tx_ref, tail_sc, tail_x):
    j = pl.program_id(1)

    @pl.when(j == 0)
    def _():
        tail_sc[...] = jnp.zeros(tail_sc.shape, jnp.float32)
        tail_x[...] = jnp.zeros(tail_x.shape, jnp.float32)

    o_zg = 3 * SC_DIM
    o_x = o_zg + SSD_INNER
    o_dt = o_x + SSD_CONV_DIM
    n = z_ref.shape[0]
    prod = z_ref[:, 2 * SC_DIM:3 * SC_DIM] * z_ref[:, 0:SC_DIM]
    ysc_ref[...] = z_ref[:, SC_DIM:2 * SC_DIM] * _causal_conv_tile(prod, tail_sc[...], scw_ref, scb_ref, SC_WIDTH)
    xbc = z_ref[:, o_x:o_dt]
    c = _causal_conv_tile(xbc, tail_x[...], cvw_ref, cvb_ref, SSD_CONV)
    xbc_ref[...] = c * jax.nn.sigmoid(c)
    dt_ref[...] = jax.nn.softplus(z_ref[:, o_dt:o_dt + SSD_HEADS] + dtb_ref[...])
    tail_sc[...] = prod[n - TAIL:n]
    tail_x[...] = xbc[n - TAIL:n]
    tsc_ref[...] = prod[n - TAIL:n]
    tx_ref[...] = xbc[n - TAIL:n]


def odd_prep(z, sc_w, sc_b, cv_w, cv_b, dt_bias):
    f32 = jnp.float32
    bsz, t, width = z.shape
    blk = lambda w: pl.BlockSpec((None, PREP_ROWS, w), lambda b, j: (b, j, 0))
    full = lambda shape: pl.BlockSpec(shape, lambda b, j: (0,) * len(shape))
    last = lambda w: pl.BlockSpec((None, TAIL, w), lambda b, j: (b, 0, 0))
    ysc, xbc, dt, tsc, tx = pl.pallas_call(
        _odd_prep_kernel,
        grid=(bsz, t // PREP_ROWS),
        in_specs=[blk(width), full((SC_WIDTH, SC_DIM)), full((1, SC_DIM)), full((SSD_CONV, SSD_CONV_DIM)),
                  full((1, SSD_CONV_DIM)), full((1, SSD_HEADS))],
        out_specs=[blk(SC_DIM), blk(SSD_CONV_DIM), blk(SSD_HEADS), last(SC_DIM), last(SSD_CONV_DIM)],
        out_shape=[jax.ShapeDtypeStruct((bsz, t, SC_DIM), f32), jax.ShapeDtypeStruct((bsz, t, SSD_CONV_DIM), f32),
                   jax.ShapeDtypeStruct((bsz, t, SSD_HEADS), f32), jax.ShapeDtypeStruct((bsz, TAIL, SC_DIM), f32),
                   jax.ShapeDtypeStruct((bsz, TAIL, SSD_CONV_DIM), f32)],
        scratch_shapes=[pltpu.VMEM((TAIL, SC_DIM), f32), pltpu.VMEM((TAIL, SSD_CONV_DIM), f32)],
        compiler_params=_cparams("parallel", "arbitrary"),
        name="odd_prep",
    )(z, sc_w.astype(f32), sc_b.astype(f32).reshape(1, SC_DIM), cv_w.astype(f32),
      cv_b.astype(f32).reshape(1, SSD_CONV_DIM), dt_bias.astype(f32).reshape(1, SSD_HEADS))
    return ysc, xbc, dt, tsc[:, TAIL - (SC_WIDTH - 1):], tx[:, TAIL - (SSD_CONV - 1):]


def layer_norm(x, g, b):
    mu = jnp.mean(x, -1, keepdims=True)
    xc = x - mu
    var = jnp.mean(xc * xc, -1, keepdims=True)
    return xc * lax.rsqrt(var + LN_EPS) * g + b


def rope(x, pos):
    half = ROT_DIM // 2
    inv = ROPE_THETA ** (-jnp.arange(half, dtype=jnp.float32) * 2.0 / ROT_DIM)
    ang = pos.astype(jnp.float32)[:, None] * inv[None, :]
    cos = jnp.cos(ang)[:, None, :]
    sin = jnp.sin(ang)[:, None, :]
    x1 = x[..., :half]
    x2 = x[..., half:ROT_DIM]
    return jnp.concatenate([x1 * cos - x2 * sin, x2 * cos + x1 * sin, x[..., ROT_DIM:]], axis=-1)


def last_rows(x, n):
    t = x.shape[1]
    if t < n:
        x = jnp.pad(x, [(0, 0), (n - t, 0)] + [(0, 0)] * (x.ndim - 2))
    return x[:, x.shape[1] - n:]


def causal_conv(x, buf, w, b):
    t = x.shape[1]
    width = w.shape[0]
    xp = jnp.concatenate([buf, x], axis=1)
    y = b + sum(xp[:, j:j + t] * w[j] for j in range(width))
    return y, xp[:, xp.shape[1] - (width - 1):]


def even_split(z, pos):
    bt, t, _ = z.shape
    o = [0]
    for w in (S5_DIM, NSA_Q, NSA_KV, NSA_KV, NSA_KV, 3 * N_HEADS):
        o.append(o[-1] + w)
    u = z[..., o[0]:o[1]]
    q = rope(z[..., o[1]:o[2]].reshape(bt, t, N_HEADS, HEAD_DIM), pos)

    def kv(a, b):
        r = z[..., a:b].reshape(bt, t, 2, KV_GROUPS, HEAD_DIM)
        return jnp.stack([rope(r[:, :, 0], pos), r[:, :, 1]], axis=2)

    kvc = kv(o[2], o[3])
    kvs = kv(o[3], o[4])
    kvw = kv(o[4], o[5])
    gates = jax.nn.sigmoid(z[..., o[5]:o[6]]).reshape(bt, t, N_HEADS, 3)
    return u, q, kvc, kvs, kvw, gates


def even_prompt_mix(z, s5p, cmpp, w_buf):
    bt, t, _ = z.shape
    pos = jnp.arange(t)
    u, q, kvc, kvs, kvw, gates = even_split(z, pos)
    y_s5, s5_state = s5_scan(u, jnp.zeros((bt, S5_GROUPS, S5_STATE, 2), jnp.float32), s5p, S5_CHUNK)
    bf16 = jnp.bfloat16
    lanes = lambda a: a.reshape(bt, t, KV_GROUPS * HEAD_DIM)
    kc = compress_prompt(lanes(kvc[:, :, 0]), compress_params(cmpp[0], cmpp[1], cmpp[2]))
    vc = compress_prompt(lanes(kvc[:, :, 1]), compress_params(cmpp[3], cmpp[4], cmpp[5]))
    front = lambda a: jnp.pad(lanes(a).astype(bf16), ((0, 0), (WINDOW, 0), (0, 0)))
    y_nsa = nsa_prompt(q.reshape(bt, t, NSA_Q), gates.reshape(bt, t, 3 * N_HEADS), kc, vc,
                       lanes(kvs[:, :, 0]).astype(bf16), lanes(kvs[:, :, 1]).astype(bf16),
                       front(kvw[:, :, 0]), front(kvw[:, :, 1]))
    new_rows = jnp.concatenate([kvc, kvs], axis=2)
    return (y_s5, y_nsa), s5_state, new_rows, last_rows(kvw, w_buf)


def even_sample_mix(z, s5_h0, pool, page_table, win_buf, s5p, cmpp):
    bt, t, _ = z.shape
    f32 = jnp.float32
    pos = page_table.shape[1] * PAGE_SIZE + jnp.arange(t)
    u, q, kvc, kvs, kvw, gates = even_split(z, pos)
    y_s5, s5_state = s5_scan(u, s5_h0.astype(f32), s5p, t)
    lanes = lambda a: a.reshape(bt, t, KV_GROUPS * HEAD_DIM)
    feat = KV_GROUPS * HEAD_DIM
    pool_t = pool.astype(f32).transpose(0, 2, 3, 4, 1).reshape(pool.shape[0], 4 * feat, PAGE_SIZE)
    kc, vc = compress_sample(pool_t, page_table, lanes(kvc[:, :, 0]), lanes(kvc[:, :, 1]),
                             compress_sample_params(cmpp[0], cmpp[1], cmpp[2]),
                             compress_sample_params(cmpp[3], cmpp[4], cmpp[5]))
    w_buf = win_buf.shape[1]
    win_f = win_buf.astype(f32)
    y_nsa = nsa_sample(q, gates, kc, vc, pool_t, page_table, lanes(kvs[:, :, 0]), lanes(kvs[:, :, 1]),
                       win_f.transpose(0, 2, 3, 4, 1).reshape(bt, 2 * feat, w_buf), lanes(kvw[:, :, 0]),
                       lanes(kvw[:, :, 1]))
    new_rows = jnp.concatenate([kvc, kvs], axis=2)
    win = jnp.concatenate([win_f, kvw], axis=1)
    return (y_s5, y_nsa), s5_state, new_rows, win[:, t:]


def ssd_scan(x, dt, a, bm, cm, h0, chunk):
    bt, t, nh, p = x.shape
    nch = t // chunk
    r = nh // SSD_GROUPS
    tri = jnp.arange(chunk)[:, None] >= jnp.arange(chunk)[None, :]

    def to_chunks(v):
        return jnp.moveaxis(v.reshape((bt, nch, chunk) + v.shape[2:]), 1, 0)

    def step(h, inp):
        xc, dtc, bc, cc = inp
        cum = jnp.cumsum(dtc * a, axis=1)
        seg = cum[:, :, None, :] - cum[:, None, :, :]
        decay = jnp.exp(jnp.where(tri[None, :, :, None], seg, NEG)).reshape(bt, chunk, chunk, SSD_GROUPS, r)
        cb = jnp.einsum('btgn,bsgn->btsg', cc, bc)
        xg = xc.reshape(bt, chunk, SSD_GROUPS, r, p)
        dg = dtc.reshape(bt, chunk, SSD_GROUPS, r)
        w = cb[..., None] * decay * dg[:, None]
        y_intra = jnp.einsum('btsgr,bsgrp->btgrp', w, xg)
        hg = h.reshape(bt, SSD_GROUPS, r, p, SSD_STATE)
        y_inter = jnp.einsum('btgn,bgrpn->btgrp', cc, hg) * jnp.exp(cum).reshape(bt, chunk, SSD_GROUPS, r)[..., None]
        wt = (jnp.exp(cum[:, -1:, :] - cum) * dtc).reshape(bt, chunk, SSD_GROUPS, r)
        h_new = (hg * jnp.exp(cum[:, -1]).reshape(bt, SSD_GROUPS, r)[..., None, None]
                 + jnp.einsum('bsgr,bsgrp,bsgn->bgrpn', wt, xg, bc))
        return h_new.reshape(bt, nh, p, SSD_STATE), (y_intra + y_inter).reshape(bt, chunk, nh, p)

    h_fin, ys = lax.scan(step, h0, (to_chunks(x), to_chunks(dt), to_chunks(bm), to_chunks(cm)))
    return jnp.moveaxis(ys, 0, 1).reshape(bt, t, nh, p), h_fin


def gated_rmsnorm(y, z, g):
    v = y * jax.nn.silu(z)
    bt, t, _ = v.shape
    vg = v.reshape(bt, t, SSD_GROUPS, SSD_INNER // SSD_GROUPS)
    vg = vg * lax.rsqrt(jnp.mean(vg * vg, -1, keepdims=True) + RMS_EPS)
    return vg.reshape(bt, t, SSD_INNER) * g


def odd_mix(z, sc_buf, conv_buf, h0, chunk, sc_w, sc_b, cv_w, cv_b, dt_bias, a_log, d_skip, norm_g):
    f32 = jnp.float32
    bt, t, _ = z.shape
    a = -jnp.exp(a_log.astype(f32))
    if h0 is None:
        y_sc, xbc_c, dt, new_sc, new_conv = odd_prep(z, sc_w, sc_b, cv_w, cv_b, dt_bias)
        y, h_new = ssd_prompt(xbc_c, dt, a, z, 3 * SC_DIM, d_skip, norm_g)
        return (y_sc, y), new_sc, new_conv, h_new
    o1 = SC_DIM
    o2 = 2 * SC_DIM
    o3 = 3 * SC_DIM
    o4 = o3 + SSD_INNER
    o5 = o4 + SSD_CONV_DIM
    sc_h = z[..., :o1]
    sc_bg = z[..., o1:o2]
    sc_cg = z[..., o2:o3]
    zg = z[..., o3:o4]
    xbc = z[..., o4:o5]
    dt_raw = z[..., o5:]
    conv_sc, new_sc = causal_conv(sc_cg * sc_h, sc_buf.astype(f32), sc_w, sc_b)
    y_sc = sc_bg * conv_sc
    xbc_c, new_conv = causal_conv(xbc, conv_buf.astype(f32), cv_w, cv_b)
    xbc_c = jax.nn.silu(xbc_c)
    gn = SSD_GROUPS * SSD_STATE
    xs = xbc_c[..., :SSD_INNER].reshape(bt, t, SSD_HEADS, SSD_HEAD_DIM)
    bm = xbc_c[..., SSD_INNER:SSD_INNER + gn].reshape(bt, t, SSD_GROUPS, SSD_STATE)
    cm = xbc_c[..., SSD_INNER + gn:].reshape(bt, t, SSD_GROUPS, SSD_STATE)
    dt = jax.nn.softplus((dt_raw + dt_bias).astype(f32))
    y, h_new = ssd_scan(xs, dt, a, bm, cm, h0.astype(f32), chunk)
    y = (y + d_skip[:, None] * xs).reshape(bt, t, SSD_INNER)
    y = gated_rmsnorm(y, zg, norm_g)
    return (y_sc, y), new_sc, new_conv, h_new


def moe_ffn(x, w_r, b_r, w_gu_bf16, w_down_bf16):
    n, d = x.shape
    logits = jnp.dot(x, w_r, precision=lax.Precision.HIGHEST) + b_r
    top_v, top_i = lax.top_k(logits, TOP_K)
    gate = jax.nn.softmax(top_v, axis=-1)
    flat_e = top_i.reshape(-1)
    blk = 128
    assert (TOP_K * n) % blk == 0
    onehot = jax.nn.one_hot(flat_e, N_EXPERTS, dtype=jnp.float32).reshape(-1, blk, N_EXPERTS)
    tri = (jnp.arange(blk)[:, None] >= jnp.arange(blk)[None, :]).astype(jnp.float32)
    local = jnp.einsum('ij,bjk->bik', tri, onehot)
    block_total = local[:, -1, :]
    block_off = jnp.cumsum(block_total, axis=0) - block_total
    incl = (local + block_off[:, None, :]).reshape(-1, N_EXPERTS)
    rank = jnp.take_along_axis(incl, flat_e[:, None], axis=1)[:, 0].astype(jnp.int32) - 1
    counts = jnp.sum(block_total, axis=0).astype(jnp.int32)
    padded = ((counts + ROW_TILE - 1) // ROW_TILE) * ROW_TILE
    pad_start = jnp.cumsum(padded) - padded
    dest = (pad_start[flat_e] + rank).astype(jnp.int32)
    n_tiles = (TOP_K * n) // ROW_TILE + N_EXPERTS
    rows = n_tiles * ROW_TILE
    row_token = jnp.zeros((rows,), jnp.int32).at[dest].set(jnp.arange(TOP_K * n, dtype=jnp.int32) // TOP_K,
                                                           unique_indices=True, mode='promise_in_bounds')
    tile_end = jnp.cumsum(padded) // ROW_TILE
    tile_expert = jnp.minimum(jnp.searchsorted(tile_end, jnp.arange(n_tiles), side='right'),
                              N_EXPERTS - 1).astype(jnp.int32)
    n_used = tile_end[-1:].astype(jnp.int32)
    xs = x.at[row_token].get(mode='promise_in_bounds')
    ys = grouped_ffn(xs, w_gu_bf16, w_down_bf16, tile_expert, n_used)
    dest = dest.reshape(n, TOP_K)
    y0 = ys.at[dest[:, 0]].get(mode='promise_in_bounds')
    y1 = ys.at[dest[:, 1]].get(mode='promise_in_bounds')
    return gate[:, 0:1] * y0 + gate[:, 1:2] * y1


def kernel(x_prompt, x_sample, state_s5, cache_nsa_kv, state_win_kv, state_sc_conv, state_ssd_conv, state_ssd,
           page_table, ln_g, ln_b, w_in_even, s5_lam_re, s5_lam_im, s5_log_dt, s5_b, s5_c, s5_d, s5_w_glu,
           nsa_wk1, nsa_wk2, nsa_pe_k, nsa_wv1, nsa_wv2, nsa_pe_v, w_out_even, ffn_w_gu, ffn_w_down,
           w_in_odd, sc_conv_w, sc_conv_b, ssd_conv_w, ssd_conv_b, ssd_dt_bias, ssd_a_log, ssd_d, ssd_norm_g,
           w_out_odd, moe_router, moe_router_b, moe_w_gu, moe_w_down):
    f32 = jnp.float32
    bf16 = jnp.bfloat16
    bp, tp, d = x_prompt.shape
    bs, ts, _ = x_sample.shape
    n_p = bp * tp
    n_s = bs * ts
    w_buf = state_win_kv.shape[2]
    streams = [x_prompt.astype(f32).reshape(n_p, d), x_sample.astype(f32).reshape(n_s, d)]
    shapes = [(bp, tp), (bs, ts)]

    def flat(parts, n_rows):
        return [p.reshape(n_rows, p.shape[-1]) for p in parts]

    def out_proj(parts, w_out, width, h, g, b):
        w = w_out.astype(bf16)
        return matmul(flat(parts, h.shape[0]), [w[:width], w[width:]], ln=(h, g, b))

    def single_expert(n_rows):
        n_tiles = pl.cdiv(n_rows, min(ROW_TILE, n_rows))
        return jnp.zeros((n_tiles,), jnp.int32), jnp.full((1,), n_tiles, jnp.int32)

    s5p = s5_params(s5_lam_re[0], s5_lam_im[0], s5_log_dt[0], s5_b[0], s5_c[0], s5_d[0], s5_w_glu[0])
    cmpp = (nsa_wk1[0], nsa_wk2[0], nsa_pe_k[0], nsa_wv1[0], nsa_wv2[0], nsa_pe_v[0])
    w_in = w_in_even[0].astype(bf16)
    zp, zs = [matmul([h], [w_in]).reshape(sh + (-1,)) for h, sh in zip(streams, shapes)]
    mix_p, s5_p, kv_p, win_p = even_prompt_mix(zp, s5p, cmpp, w_buf)
    mix_s, s5_s, kv_s, win_s = even_sample_mix(zs, state_s5[0], cache_nsa_kv[0], page_table, state_win_kv[0],
                                               s5p, cmpp)
    streams = [out_proj(mix, w_out_even[0], S5_DIM, h, ln_g[0, 0], ln_b[0, 0])
               for mix, h in zip((mix_p, mix_s), streams)]
    w_gu, w_down = to_bf16(ffn_w_gu), to_bf16(ffn_w_down)
    streams = [grouped_ffn(h, w_gu, w_down, *single_expert(h.shape[0]), ln=(ln_g[0, 1], ln_b[0, 1]))
               for h in streams]

    oddp = (sc_conv_w[0], sc_conv_b[0], ssd_conv_w[0], ssd_conv_b[0], ssd_dt_bias[0],
            ssd_a_log[0], ssd_d[0], ssd_norm_g[0])
    w_in = w_in_odd[0].astype(bf16)
    zp, zs = [matmul([h], [w_in]).reshape(sh + (-1,)) for h, sh in zip(streams, shapes)]
    mix_p, scc_p, sdc_p, ssd_p = odd_mix(zp, None, None, None, SSD_CHUNK, *oddp)
    mix_s, scc_s, sdc_s, ssd_s = odd_mix(zs, state_sc_conv[0], state_ssd_conv[0], state_ssd[0], ts, *oddp)
    streams = [out_proj(mix, w_out_odd[0], SC_DIM, h, ln_g[1, 0], ln_b[1, 0])
               for mix, h in zip((mix_p, mix_s), streams)]
    h = jnp.concatenate(streams, axis=0)
    f = moe_ffn(h, moe_router[0], moe_router_b[0], to_bf16(moe_w_gu[0]), to_bf16(moe_w_down[0]))
    h = layer_norm(ALPHA * h + f, ln_g[1, 1], ln_b[1, 1])

    hp = h[:n_p].reshape(bp, tp, d)
    hs = h[n_p:].reshape(bs, ts, d)
    st = lambda a, ref: a[None].astype(ref.dtype)
    return (hp.astype(x_prompt.dtype), hs.astype(x_sample.dtype),
            st(s5_p, state_s5), st(s5_s, state_s5),
            st(kv_p, cache_nsa_kv), st(kv_s, cache_nsa_kv),
            st(win_p, state_win_kv), st(win_s, state_win_kv),
            st(scc_p, state_sc_conv), st(scc_s, state_sc_conv),
            st(sdc_p, state_ssd_conv), st(sdc_s, state_ssd_conv),
            st(ssd_p, state_ssd), st(ssd_s, state_ssd))
```

```python
import functools
import math

import jax
import jax.numpy as jnp
from jax import lax
from jax.experimental import pallas as pl
from jax.experimental.pallas import tpu as pltpu

D_MODEL = 1024
SEQ = 8192
DEPTH = 2
DEC_SEQ = 8
PAST_LEN = 16384
ALPHA = (2.0 * DEPTH) ** 0.25
LN_EPS = 1e-5
RMS_EPS = 1e-5
NEG = -1e30

S5_DIM = D_MODEL // 2
S5_GROUP = 16
S5_GROUPS = S5_DIM // S5_GROUP
S5_STATE = 64

HEAD_DIM = 64
N_HEADS = (D_MODEL // 2) // HEAD_DIM
KV_GROUPS = 2
HEADS_PER_GROUP = N_HEADS // KV_GROUPS
CMP_STRIDE = 16
CMP_LEN = 2 * CMP_STRIDE
SEL_BLOCK = 64
N_SEL = 16
WINDOW = 512
Q_BLOCK = 128
ROPE_THETA = 500000.0
ROT_DIM = HEAD_DIM // 4
FORCE = 1e4
NSA_Q = N_HEADS * HEAD_DIM
NSA_KV = 2 * KV_GROUPS * HEAD_DIM

SC_DIM = D_MODEL // 2
SC_WIDTH = 3
SSD_HEAD_DIM = 64
SSD_HEADS = 16
SSD_INNER = SSD_HEADS * SSD_HEAD_DIM
SSD_GROUPS = 4
SSD_STATE = 128
SSD_CONV = 4
SSD_CONV_DIM = SSD_INNER + 2 * SSD_GROUPS * SSD_STATE
SSD_CHUNK = 128

D_FF = 2816
N_EXPERTS = 8
TOP_K = 2

VMEM_LIMIT_BYTES = 56 * 1024 * 1024
LANES = 128
S5_N = S5_GROUPS * S5_STATE
S5_LT = S5_N // LANES
S5_CHUNK = 256
SEL_TILE = 1024
QK_SCALE = HEAD_DIM ** -0.5 * math.log2(math.e)
REMOVED = -3e38
PAGE_SIZE = 128
PAGES_PER_STEP = 32
NEW_PAD = 128
CAST_ROWS = 256
CAST_SPLIT = 4
TAIL = 8
ROW_TILE = 512
FF_TILE = D_FF // 2


def _cparams(*sem):
    return pltpu.CompilerParams(dimension_semantics=sem, vmem_limit_bytes=VMEM_LIMIT_BYTES)


def _deepnorm(resid, update, g, b):
    y = ALPHA * resid + update
    mu = jnp.mean(y, axis=-1, keepdims=True)
    yc = y - mu
    var = jnp.mean(yc * yc, axis=-1, keepdims=True)
    return yc * lax.rsqrt(var + LN_EPS) * g + b


def _mm_kernel(*refs, n_in, fuse_ln):
    xs, ws = refs[0:n_in], refs[n_in:2 * n_in]
    o_ref = refs[-1]
    acc = None
    for x_ref, w_ref in zip(xs, ws):
        part = jnp.dot(x_ref[...].astype(jnp.bfloat16), w_ref[...], preferred_element_type=jnp.float32)
        acc = part if acc is None else acc + part
    if fuse_ln:
        r_ref, g_ref, b_ref = refs[2 * n_in:2 * n_in + 3]
        acc = _deepnorm(r_ref[...], acc, g_ref[...], b_ref[...])
    o_ref[...] = acc


def matmul(xs, ws_bf16, ln=None):
    m = xs[0].shape[0]
    n = ws_bf16[0].shape[1]
    tile = min(ROW_TILE, m)
    row = lambda width: pl.BlockSpec((tile, width), lambda i: (i, 0))
    fixed = lambda shape: pl.BlockSpec(shape, lambda i: (0, 0), pipeline_mode=pl.Buffered(1))
    in_specs = [row(x.shape[1]) for x in xs] + [fixed(w.shape) for w in ws_bf16]
    args = list(xs) + list(ws_bf16)
    if ln is not None:
        resid, g, b = ln
        in_specs += [row(n), fixed((1, n)), fixed((1, n))]
        args += [resid, g.reshape(1, n), b.reshape(1, n)]
    return pl.pallas_call(
        functools.partial(_mm_kernel, n_in=len(xs), fuse_ln=ln is not None),
        grid=(pl.cdiv(m, tile),),
        in_specs=in_specs,
        out_specs=row(n),
        out_shape=jax.ShapeDtypeStruct((m, n), jnp.float32),
        compiler_params=_cparams("parallel"),
        name="matmul",
    )(*args)


def _cast_kernel(*refs):
    o_ref = refs[-1]
    o_ref[...] = jnp.concatenate([r[...].astype(o_ref.dtype) for r in refs[:-1]], axis=1)


def to_bf16(w):
    shape = w.shape
    w2 = w.reshape(-1, shape[-1])
    rows, cols = w2.shape
    split = CAST_SPLIT if cols % (CAST_SPLIT * LANES) == 0 else 1
    out = pl.pallas_call(
        _cast_kernel,
        grid=(pl.cdiv(rows, CAST_ROWS),),
        in_specs=[pl.BlockSpec((CAST_ROWS, cols // split), lambda i, c=c: (i, c)) for c in range(split)],
        out_specs=pl.BlockSpec((CAST_ROWS, cols), lambda i: (i, 0)),
        out_shape=jax.ShapeDtypeStruct((rows, cols), jnp.bfloat16),
        compiler_params=_cparams("parallel"),
        name="to_bf16",
    )(*([w2] * split))
    return out.reshape(shape)


def _ffn_kernel(te_ref, nt_ref, x_ref, wg_ref, wu_ref, wd_ref, *rest, fuse_ln):
    o_ref = rest[-1]
    t = pl.program_id(0)
    j = pl.program_id(1)

    @pl.when(t < nt_ref[0])
    def _():
        x = x_ref[...].astype(jnp.bfloat16)
        g = jnp.dot(x, wg_ref[...], preferred_element_type=jnp.float32)
        u = jnp.dot(x, wu_ref[...], preferred_element_type=jnp.float32)
        h = (g * jax.nn.sigmoid(g) * u).astype(jnp.bfloat16)
        part = jnp.dot(h, wd_ref[...], preferred_element_type=jnp.float32)

        @pl.when(j == 0)
        def _():
            o_ref[...] = part

        @pl.when(j > 0)
        def _():
            if fuse_ln:
                o_ref[...] = _deepnorm(x_ref[...], o_ref[...] + part, rest[0][...], rest[1][...])
            else:
                o_ref[...] += part

    @pl.when(jnp.logical_and(t >= nt_ref[0], j == 0))
    def _():
        o_ref[...] = jnp.zeros_like(o_ref)


def grouped_ffn(x, w_gu_bf16, w_down_bf16, tile_expert, n_tiles_used, ln=None):
    r, d = x.shape
    nf = D_FF // FF_TILE
    assert nf == 2
    tile = min(ROW_TILE, r)
    n_tiles = pl.cdiv(r, tile)
    in_specs = [
        pl.BlockSpec((tile, d), lambda t, j, te, nt: (t, 0)),
        pl.BlockSpec((None, d, FF_TILE), lambda t, j, te, nt: (te[t], 0, j)),
        pl.BlockSpec((None, d, FF_TILE), lambda t, j, te, nt: (te[t], 0, nf + j)),
        pl.BlockSpec((None, FF_TILE, d), lambda t, j, te, nt: (te[t], j, 0)),
    ]
    args = [tile_expert, n_tiles_used, x, w_gu_bf16, w_gu_bf16, w_down_bf16]
    if ln is not None:
        in_specs += [pl.BlockSpec((1, d), lambda t, j, te, nt: (0, 0))] * 2
        args += [ln[0].reshape(1, d), ln[1].reshape(1, d)]
    grid_spec = pltpu.PrefetchScalarGridSpec(
        num_scalar_prefetch=2,
        grid=(n_tiles, nf),
        in_specs=in_specs,
        out_specs=pl.BlockSpec((tile, d), lambda t, j, te, nt: (t, 0)),
    )
    return pl.pallas_call(
        functools.partial(_ffn_kernel, fuse_ln=ln is not None),
        grid_spec=grid_spec,
        out_shape=jax.ShapeDtypeStruct((r, d), jnp.float32),
        compiler_params=_cparams("parallel", "arbitrary"),
        name="grouped_ffn",
    )(*args)


def _s5_kernel(u_ref, h0r_ref, h0i_ref, ar_ref, ai_ref, bbr_ref, bbi_ref, cr_ref, ci_ref, d_ref, wglu_ref,
               y_ref, hro_ref, hio_ref, bur, bui, sr, si, hr, hi, *, chains, chunk):
    j = pl.program_id(0)

    @pl.when(j == 0)
    def _():
        hr[...] = h0r_ref[...]
        hi[...] = h0i_ref[...]

    u = u_ref[...].reshape(chains * chunk, S5_DIM)
    ub = u.astype(jnp.bfloat16)
    hd, hn = S5_DIM // 2, S5_N // 2

    def b_proj(w_ref):
        return jnp.concatenate([jnp.dot(ub[:, h * hd:(h + 1) * hd], w_ref[h * hd:(h + 1) * hd, h * hn:(h + 1) * hn],
                                        preferred_element_type=jnp.float32) for h in range(2)], axis=1)

    bu_r = b_proj(bbr_ref)
    bu_i = b_proj(bbi_ref)
    for k in range(S5_LT):
        bur[k] = bu_r[:, k * LANES:(k + 1) * LANES]
        bui[k] = bu_i[:, k * LANES:(k + 1) * LANES]
    ar = [jnp.broadcast_to(ar_ref[:, k * LANES:(k + 1) * LANES], (chains, LANES)) for k in range(S5_LT)]
    ai = [jnp.broadcast_to(ai_ref[:, k * LANES:(k + 1) * LANES], (chains, LANES)) for k in range(S5_LT)]

    def body(t, carry):
        rows = pl.ds(t, chains, stride=chunk)
        out = []
        for k in range(S5_LT):
            xr, xi = carry[2 * k], carry[2 * k + 1]
            nr = ar[k] * xr - ai[k] * xi + bur[k, rows, :]
            ni = ar[k] * xi + ai[k] * xr + bui[k, rows, :]
            sr[k, rows, :] = nr
            si[k, rows, :] = ni
            out += [nr, ni]
        return tuple(out)

    init = []
    for k in range(S5_LT):
        init += [hr[:, k * LANES:(k + 1) * LANES], hi[:, k * LANES:(k + 1) * LANES]]
    fin = lax.fori_loop(0, chunk, body, tuple(init), unroll=2)
    xr = jnp.concatenate(fin[0::2], axis=1)
    xi = jnp.concatenate(fin[1::2], axis=1)
    hr[...] = xr
    hi[...] = xi
    hro_ref[...] = xr
    hio_ref[...] = xi
    s_r = jnp.concatenate([sr[k] for k in range(S5_LT)], axis=1).astype(jnp.bfloat16)
    s_i = jnp.concatenate([si[k] for k in range(S5_LT)], axis=1).astype(jnp.bfloat16)
    y = jnp.concatenate(
        [jnp.dot(s_r[:, h * hn:(h + 1) * hn], cr_ref[h * hn:(h + 1) * hn, h * hd:(h + 1) * hd],
                 preferred_element_type=jnp.float32)
         - jnp.dot(s_i[:, h * hn:(h + 1) * hn], ci_ref[h * hn:(h + 1) * hn, h * hd:(h + 1) * hd],
                   preferred_element_type=jnp.float32) for h in range(2)], axis=1) + d_ref[...] * u
    z = jax.nn.gelu(y)
    gate = jax.nn.sigmoid(jnp.dot(z.astype(jnp.bfloat16), wglu_ref[...], preferred_element_type=jnp.float32))
    y_ref[...] = (z * gate).reshape(chains, chunk, S5_DIM)


def s5_params(lam_re, lam_im, log_dt, b, c, d, w_glu):
    f32 = jnp.float32
    dt = jnp.exp(log_dt.astype(f32))[:, None]
    mag = jnp.exp(lam_re * dt)
    ang = lam_im * dt
    ab_re = mag * jnp.cos(ang)
    ab_im = mag * jnp.sin(ang)
    den = lam_re * lam_re + lam_im * lam_im
    nr = ab_re - 1.0
    coef_re = (nr * lam_re + ab_im * lam_im) / den
    coef_im = (ab_im * lam_re - nr * lam_im) / den
    b_re = b[..., 0].astype(f32)
    b_im = b[..., 1].astype(f32)
    bb_re = coef_re[..., None] * b_re - coef_im[..., None] * b_im
    bb_im = coef_re[..., None] * b_im + coef_im[..., None] * b_re
    eye = jnp.eye(S5_GROUPS, dtype=f32)
    bbr = jnp.einsum('gnk,gh->gkhn', bb_re, eye).reshape(S5_DIM, S5_N).astype(jnp.bfloat16)
    bbi = jnp.einsum('gnk,gh->gkhn', bb_im, eye).reshape(S5_DIM, S5_N).astype(jnp.bfloat16)
    cr = jnp.einsum('gkn,gh->gnhk', c[..., 0].astype(f32), eye).reshape(S5_N, S5_DIM).astype(jnp.bfloat16)
    ci = jnp.einsum('gkn,gh->gnhk', c[..., 1].astype(f32), eye).reshape(S5_N, S5_DIM).astype(jnp.bfloat16)
    return (ab_re.reshape(1, S5_N), ab_im.reshape(1, S5_N), bbr, bbi, cr, ci,
            d.astype(f32).reshape(1, S5_DIM), w_glu.astype(jnp.bfloat16))


def s5_scan(u, h0, params, chunk):
    chains, t, _ = u.shape
    ar, ai, bbr, bbi, cr, ci, d, wglu = params
    h0r = h0[..., 0].reshape(chains, S5_N)
    h0i = h0[..., 1].reshape(chains, S5_N)
    full = lambda shape: pl.BlockSpec(shape, lambda j: (0,) * len(shape))
    rows = chains * chunk
    y, hr, hi = pl.pallas_call(
        functools.partial(_s5_kernel, chains=chains, chunk=chunk),
        grid=(t // chunk,),
        in_specs=[pl.BlockSpec((chains, chunk, S5_DIM), lambda j: (0, j, 0)),
                  full((chains, S5_N)), full((chains, S5_N)), full((1, S5_N)), full((1, S5_N)),
                  full((S5_DIM, S5_N)), full((S5_DIM, S5_N)), full((S5_N, S5_DIM)), full((S5_N, S5_DIM)),
                  full((1, S5_DIM)), full((S5_DIM, S5_DIM))],
        out_specs=[pl.BlockSpec((chains, chunk, S5_DIM), lambda j: (0, j, 0)),
                   full((chains, S5_N)), full((chains, S5_N))],
        out_shape=[jax.ShapeDtypeStruct((chains, t, S5_DIM), jnp.float32),
                   jax.ShapeDtypeStruct((chains, S5_N), jnp.float32),
                   jax.ShapeDtypeStruct((chains, S5_N), jnp.float32)],
        scratch_shapes=[pltpu.VMEM((S5_LT, rows, LANES), jnp.float32)] * 4
                       + [pltpu.VMEM((chains, S5_N), jnp.float32)] * 2,
        compiler_params=_cparams("arbitrary"),
        name="s5_scan",
    )(u, h0r, h0i, ar, ai, bbr, bbi, cr, ci, d, wglu)
    new_state = jnp.stack([hr.reshape(chains, S5_GROUPS, S5_STATE), hi.reshape(chains, S5_GROUPS, S5_STATE)],
                          axis=-1)
    return y, new_state


def _dot_nt(a, b):
    return lax.dot_general(a, b, (((1,), (1,)), ((), ())), preferred_element_type=jnp.float32)


def _split3(x):
    hi = x.astype(jnp.bfloat16)
    rem = x - hi.astype(jnp.float32)
    mid = rem.astype(jnp.bfloat16)
    lo = (rem - mid.astype(jnp.float32)).astype(jnp.bfloat16)
    return hi, mid, lo


def _softmax_rows(s, mask):
    s = jnp.where(mask, s, NEG)
    m = jnp.max(s, axis=-1, keepdims=True)
    p = jnp.exp2(s - m)
    inv = jnp.where(m > 0.5 * NEG, 1.0 / jnp.sum(p, axis=-1, keepdims=True), 0.0)
    return p * inv


def _nsa_prompt_kernel(q_ref, gate_ref, kc_ref, vc_ref, ks_ref, vs_ref, kw_ref, vw_ref, o_ref, *, n_cmp, n_blk):
    f32, bf16 = jnp.float32, jnp.bfloat16
    r4 = HEADS_PER_GROUP
    n_cpad = kc_ref.shape[0]
    start = pl.program_id(1) * Q_BLOCK
    q = q_ref[...] * QK_SCALE
    gate = gate_ref[...]
    lane = lax.broadcasted_iota(jnp.int32, (Q_BLOCK, LANES), 1)
    qpos = start + lax.broadcasted_iota(jnp.int32, (Q_BLOCK, 1), 0)
    n_idx = lax.broadcasted_iota(jnp.int32, (Q_BLOCK, n_cpad), 1)
    cmask = (((n_idx * CMP_STRIDE + (CMP_LEN - 1)) <= qpos) & (n_idx < n_cmp))[None]
    ratio = SEL_BLOCK // CMP_STRIDE
    gsum = (lax.broadcasted_iota(jnp.int32, (n_blk, n_cpad), 1) // ratio
            == lax.broadcasted_iota(jnp.int32, (n_blk, n_cpad), 0)).astype(bf16)
    blk = lax.broadcasted_iota(jnp.int32, (n_blk, Q_BLOCK), 0)
    blk_f = blk.astype(f32)
    jq = (start + lax.broadcasted_iota(jnp.int32, (n_blk, Q_BLOCK), 1)) // SEL_BLOCK
    force = jnp.where((blk == 0) | (blk == jq) | (blk == jq - 1), FORCE, 0.0)
    qgs, o_cs, sels = [], [], []
    for g in range(KV_GROUPS):
        keep = (lane < HEAD_DIM) if g == 0 else (lane >= HEAD_DIM)
        parts = []
        for r in range(r4):
            h = r4 * g + r
            tile = q[:, (h // 2) * LANES:(h // 2 + 1) * LANES]
            if h % 2 != g:
                tile = pltpu.roll(tile, HEAD_DIM, axis=1)
            parts.append(jnp.where(keep, tile, 0.0))
        qg = jnp.concatenate(parts, axis=0).astype(bf16)
        qgs.append(qg)

        p_c = _softmax_rows(_dot_nt(qg, kc_ref[...]).reshape(r4, Q_BLOCK, n_cpad), cmask)
        o_cs.append(jnp.dot(p_c.reshape(r4 * Q_BLOCK, n_cpad).astype(bf16), vc_ref[...],
                            preferred_element_type=f32).reshape(r4, Q_BLOCK, LANES))
        psum = p_c[0] + p_c[1] + p_c[2] + p_c[3]
        imp_t = sum(_dot_nt(gsum, part) for part in _split3(psum))

        score = jnp.where(blk <= jq, imp_t + force, NEG)
        sel_t = jnp.zeros((n_blk, Q_BLOCK), f32)
        for _ in range(min(N_SEL, n_blk)):
            m = jnp.max(score, axis=0, keepdims=True)
            idx = jnp.min(jnp.where(score == m, blk_f, float(n_blk)), axis=0, keepdims=True)
            hit = blk_f == idx
            sel_t = jnp.where(hit & (m > 0.5 * NEG), 1.0, sel_t)
            score = jnp.where(hit, REMOVED, score)
        sels.append(sel_t.T)

    n_full = start // SEL_TILE
    expand0 = (lax.broadcasted_iota(jnp.int32, (n_blk, SEL_TILE), 0)
               == lax.broadcasted_iota(jnp.int32, (n_blk, SEL_TILE), 1) // SEL_BLOCK).astype(bf16)

    def tile_update(i, carry, causal):
        off = pl.multiple_of(i * SEL_TILE, SEL_TILE)
        k = ks_ref[pl.ds(off, SEL_TILE), :]
        v = vs_ref[pl.ds(off, SEL_TILE), :]
        out = []
        for g in range(KV_GROUPS):
            m_run, l_run, acc = carry[g]
            s_t = _dot_nt(qgs[g], k).reshape(r4, Q_BLOCK, SEL_TILE)
            shifted = pltpu.roll(sels[g], (n_blk - i * (SEL_TILE // SEL_BLOCK)) % n_blk, axis=1).astype(bf16)
            mk = jnp.dot(shifted, expand0, preferred_element_type=f32) > 0.5
            if causal:
                kpos = i * SEL_TILE + lax.broadcasted_iota(jnp.int32, (Q_BLOCK, SEL_TILE), 1)
                mk = mk & (kpos <= qpos)
            s_t = jnp.where(mk[None], s_t, NEG)
            m_new = jnp.maximum(m_run, jnp.max(s_t, axis=-1, keepdims=True))
            alpha = jnp.exp2(m_run - m_new)
            p = jnp.exp2(s_t - m_new)
            l_new = alpha * l_run + jnp.sum(p, axis=-1, keepdims=True)
            pv = jnp.dot(p.reshape(r4 * Q_BLOCK, SEL_TILE).astype(bf16), v, preferred_element_type=f32)
            out.append((m_new, l_new, alpha * acc + pv.reshape(r4, Q_BLOCK, LANES)))
        return tuple(out)

    init = (jnp.full((r4, Q_BLOCK, 1), NEG, f32), jnp.zeros((r4, Q_BLOCK, 1), f32),
            jnp.zeros((r4, Q_BLOCK, LANES), f32))
    carry = lax.fori_loop(0, n_full, lambda i, c: tile_update(i, c, False), (init, init))
    fin = tile_update(n_full, carry, True)

    n_win = WINDOW + Q_BLOCK
    woff = pl.multiple_of(start, Q_BLOCK)
    kwin = kw_ref[pl.ds(woff, n_win), :]
    vwin = vw_ref[pl.ds(woff, n_win), :]
    wpos = start - WINDOW + lax.broadcasted_iota(jnp.int32, (Q_BLOCK, n_win), 1)
    wmask = ((wpos <= qpos) & (wpos > qpos - WINDOW) & (wpos >= 0))[None]
    heads = [None] * N_HEADS
    for g in range(KV_GROUPS):
        m_fin, l_fin, acc = fin[g]
        o_s = acc * jnp.where(m_fin > 0.5 * NEG, 1.0 / l_fin, 0.0)
        p_w = _softmax_rows(_dot_nt(qgs[g], kwin).reshape(r4, Q_BLOCK, n_win), wmask)
        o_w = jnp.dot(p_w.reshape(r4 * Q_BLOCK, n_win).astype(bf16), vwin,
                      preferred_element_type=f32).reshape(r4, Q_BLOCK, LANES)
        for r in range(r4):
            h = r4 * g + r
            heads[h] = (gate[:, 3 * h:3 * h + 1] * o_cs[g][r] + gate[:, 3 * h + 1:3 * h + 2] * o_s[r]
                        + gate[:, 3 * h + 2:3 * h + 3] * o_w[r])

    tiles = []
    for j in range(N_HEADS // 2):
        even, odd = heads[2 * j], heads[2 * j + 1]
        if j // 2 == 0:
            tiles.append(jnp.where(lane < HEAD_DIM, even, pltpu.roll(odd, HEAD_DIM, axis=1)))
        else:
            tiles.append(jnp.where(lane < HEAD_DIM, pltpu.roll(even, HEAD_DIM, axis=1), odd))
    o_ref[...] = jnp.concatenate(tiles, axis=1)


def nsa_prompt(q, gates, kc, vc, kvs, kvw_pad):
    b, t, _ = q.shape
    n_cpad = kc.shape[1]
    kern = functools.partial(_nsa_prompt_kernel, n_cmp=t // CMP_STRIDE - 1, n_blk=t // SEL_BLOCK)
    whole = lambda rows, c=0: pl.BlockSpec((None, rows, LANES), lambda i, j: (i, 0, c))
    return pl.pallas_call(
        kern,
        grid=(b, t // Q_BLOCK),
        in_specs=[pl.BlockSpec((None, Q_BLOCK, NSA_Q), lambda i, j: (i, j, 0)),
                  pl.BlockSpec((None, Q_BLOCK, 3 * N_HEADS), lambda i, j: (i, j, 0)),
                  whole(n_cpad), whole(n_cpad), whole(t, 0), whole(t, 1), whole(t + WINDOW, 0),
                  whole(t + WINDOW, 1)],
        out_specs=pl.BlockSpec((None, Q_BLOCK, NSA_Q), lambda i, j: (i, j, 0)),
        out_shape=jax.ShapeDtypeStruct((b, t, NSA_Q), jnp.float32),
        compiler_params=_cparams("parallel", "arbitrary"),
        name="nsa_prompt",
    )(q, gates, kc, vc, kvs, kvs, kvw_pad, kvw_pad)


def _compress_kernel(ch_ref, pet_ref, peb_ref, w1t_ref, w1b_ref, w2_ref, o_ref):
    bf16 = jnp.bfloat16
    ch = ch_ref[...]
    n_ch = ch.shape[0]
    a = jnp.dot((ch + pet_ref[...]).astype(bf16), w1t_ref[...], preferred_element_type=jnp.float32)
    b = jnp.dot((ch + peb_ref[...]).astype(bf16), w1b_ref[...], preferred_element_type=jnp.float32)
    pre = a + pltpu.roll(b, n_ch - 1, axis=0)
    o_ref[...] = jnp.dot(jax.nn.gelu(pre).astype(bf16), w2_ref[...],
                         preferred_element_type=jnp.float32).astype(o_ref.dtype)


def compress_params(w1, w2, pe):
    f32 = jnp.float32
    eye = jnp.eye(KV_GROUPS, dtype=f32)
    w1r = w1.astype(f32).reshape(2, CMP_STRIDE, HEAD_DIM, HEAD_DIM)
    big = jnp.einsum('hjde,gk->hjgdke', w1r, eye).reshape(2, CMP_STRIDE * LANES, LANES).astype(jnp.bfloat16)
    w2bd = jnp.einsum('de,gk->gdke', w2.astype(f32), eye).reshape(LANES, LANES).astype(jnp.bfloat16)
    per = pe.astype(f32).reshape(2, CMP_STRIDE, 1, HEAD_DIM)
    pe_rows = jnp.broadcast_to(per, (2, CMP_STRIDE, KV_GROUPS, HEAD_DIM)).reshape(2, 1, CMP_STRIDE * LANES)
    return pe_rows[0], pe_rows[1], big[0], big[1], w2bd


def compress_prompt(x, params):
    b, t, _ = x.shape
    n_ch = t // CMP_STRIDE
    ch = x.reshape(b, n_ch, CMP_STRIDE * LANES)
    pet, peb, w1t, w1b, w2bd = params
    full = lambda shape: pl.BlockSpec(shape, lambda i: (0,) * len(shape))
    return pl.pallas_call(
        _compress_kernel,
        grid=(b,),
        in_specs=[pl.BlockSpec((None, n_ch, CMP_STRIDE * LANES), lambda i: (i, 0, 0)),
                  full((1, CMP_STRIDE * LANES)), full((1, CMP_STRIDE * LANES)),
                  full((CMP_STRIDE * LANES, LANES)), full((CMP_STRIDE * LANES, LANES)), full((LANES, LANES))],
        out_specs=pl.BlockSpec((None, n_ch, LANES), lambda i: (i, 0, 0)),
        out_shape=jax.ShapeDtypeStruct((b, n_ch, LANES), jnp.bfloat16),
        compiler_params=_cparams("parallel"),
        name="compress_prompt",
    )(ch, pet, peb, w1t, w1b, w2bd)


def _cmp_sample_kernel(pt_ref, *refs, n_pages):
    f32, bf16 = jnp.float32, jnp.bfloat16
    pp = PAGES_PER_STEP
    pages = refs[0:pp]
    (perm_ref, newk_ref, newv_ref, wk_ref, wv_ref, ck_ref, cv_ref, w2k_ref, w2v_ref,
     kc_ref, vc_ref, slab_k, slab_v) = refs[pp:]
    s = pl.program_id(1)
    cpp = PAGE_SIZE // CMP_STRIDE
    base = pl.multiple_of(s * (pp * cpp), pp * cpp)
    for half, slab in enumerate((slab_k, slab_v)):
        for i in range(pp):
            page = pages[i][half * LANES:(half + 1) * LANES, :].astype(bf16)
            rows = _dot_nt(perm_ref[...], page)
            for j in range(CMP_STRIDE):
                slab[j, pl.ds(base + i * cpp, cpp), :] = rows[j * cpp:(j + 1) * cpp, :]

    @pl.when(s == pl.num_programs(1) - 1)
    def _():
        n_ch = n_pages * (PAGE_SIZE // CMP_STRIDE)
        row = lax.broadcasted_iota(jnp.int32, (n_ch, LANES), 0)
        for slab, new_ref, w_ref, c_ref, w2_ref, o_ref in ((slab_k, newk_ref, wk_ref, ck_ref, w2k_ref, kc_ref),
                                                           (slab_v, newv_ref, wv_ref, cv_ref, w2v_ref, vc_ref)):
            ch = jnp.concatenate([slab[j] for j in range(CMP_STRIDE)], axis=1).astype(bf16)
            ab = jnp.dot(ch, w_ref[...], preferred_element_type=f32)
            b_new = jnp.dot(new_ref[...].astype(bf16), w_ref[...], preferred_element_type=f32)[0:1, LANES:]
            nxt = pltpu.roll(ab[:, LANES:], n_ch - 1, axis=0)
            nxt = jnp.where(row == n_ch - 1, b_new, nxt)
            pre = ab[:, :LANES] + nxt + c_ref[...]
            o_ref[...] = jnp.dot(jax.nn.gelu(pre).astype(bf16), w2_ref[...],
                                 preferred_element_type=f32).astype(o_ref.dtype)


def compress_sample_params(w1, w2, pe):
    pet, peb, w1t, w1b, w2bd = compress_params(w1, w2, pe)
    hp = lax.Precision.HIGHEST
    const = (jnp.dot(pet, w1t.astype(jnp.float32), precision=hp)
             + jnp.dot(peb, w1b.astype(jnp.float32), precision=hp))
    return jnp.concatenate([w1t, w1b], axis=1), const, w2bd


def _page_spec(i, pair):
    return pl.BlockSpec((None, 2 * LANES, PAGE_SIZE),
                        lambda b, s, pt: (pt[b, PAGES_PER_STEP * s + i], pair, 0))


def _per_seq(shape):
    return pl.BlockSpec((None,) + shape, lambda b, s, pt: (b, 0, 0))


def compress_sample(pool_t, page_table, new_k, new_v, pk, pv):
    bsz, n_pages = page_table.shape
    pp = PAGES_PER_STEP
    n_ch = n_pages * (PAGE_SIZE // CMP_STRIDE)
    t_new = new_k.shape[1]

    def chunk_rows(x):
        x = jnp.pad(x, ((0, 0), (0, CMP_STRIDE - t_new), (0, 0))).reshape(bsz, 1, CMP_STRIDE * LANES)
        return jnp.pad(x, ((0, 0), (0, 7), (0, 0)))

    full = lambda shape: pl.BlockSpec(shape, lambda b, s, pt: (0,) * len(shape))
    r = jnp.arange(PAGE_SIZE)
    cpp = PAGE_SIZE // CMP_STRIDE
    perm = (r[None, :] == (r[:, None] % cpp) * CMP_STRIDE + r[:, None] // cpp).astype(jnp.bfloat16)
    wk, ck, w2k = pk
    wv, cv, w2v = pv
    grid_spec = pltpu.PrefetchScalarGridSpec(
        num_scalar_prefetch=1,
        grid=(bsz, n_pages // pp),
        in_specs=[_page_spec(i, 0) for i in range(pp)]
                 + [full((PAGE_SIZE, PAGE_SIZE)), _per_seq((8, CMP_STRIDE * LANES)),
                    _per_seq((8, CMP_STRIDE * LANES)),
                    full((CMP_STRIDE * LANES, 2 * LANES)), full((CMP_STRIDE * LANES, 2 * LANES)),
                    full((1, LANES)), full((1, LANES)), full((LANES, LANES)), full((LANES, LANES))],
        out_specs=[_per_seq((n_ch, LANES)), _per_seq((n_ch, LANES))],
        scratch_shapes=[pltpu.VMEM((CMP_STRIDE, n_ch, LANES), jnp.float32)] * 2,
    )
    return pl.pallas_call(
        functools.partial(_cmp_sample_kernel, n_pages=n_pages),
        grid_spec=grid_spec,
        out_shape=[jax.ShapeDtypeStruct((bsz, n_ch, LANES), jnp.bfloat16)] * 2,
        compiler_params=_cparams("parallel", "arbitrary"),
        name="compress_sample",
    )(page_table, *([pool_t] * pp), perm, chunk_rows(new_k), chunk_rows(new_v), wk, wv, ck, cv, w2k, w2v)


def _nsa_sample_kernel(pt_ref, *refs, n_pages, t_new, w_buf):
    f32, bf16 = jnp.float32, jnp.bfloat16
    pp = PAGES_PER_STEP
    q_ref, gate_ref, kc_ref, vc_ref = refs[0:4]
    pages = refs[4:4 + pp]
    (ksn_ref, vsn_ref, win_ref, kwn_ref, vwn_ref, o_ref,
     sel_scr, exp_scr, oc_scr, m_scr, l_scr, acc_scr) = refs[4 + pp:]
    r4, g2 = HEADS_PER_GROUP, KV_GROUPS
    n_rows = g2 * r4 * t_new
    past_len = n_pages * PAGE_SIZE
    n_cmp = kc_ref.shape[0]
    n_bpad = sel_scr.shape[1]
    tile = pp * PAGE_SIZE
    s = pl.program_id(1)
    qall = q_ref[...]
    qpos = past_len + lax.broadcasted_iota(jnp.int32, (n_rows, 1), 0) % t_new

    def grouped(x):
        return x.reshape(g2, 1, t_new, x.shape[-1])

    @pl.when(s == 0)
    def _():
        s_c = _dot_nt(qall, kc_ref[...])
        n_idx = lax.broadcasted_iota(jnp.int32, (n_rows, n_cmp), 1)
        p_c = _softmax_rows(s_c, (n_idx * CMP_STRIDE + (CMP_LEN - 1)) <= qpos)
        oc_scr[...] = jnp.dot(p_c.astype(bf16), vc_ref[...], preferred_element_type=f32)
        psum = jnp.sum(p_c.reshape(g2, r4, t_new, n_cmp), axis=1).reshape(g2 * t_new, n_cmp)
        psum = jnp.concatenate([psum, jnp.zeros((LANES - g2 * t_new, n_cmp), f32)], axis=0)
        p_hi = psum.astype(bf16)
        rem = psum - p_hi.astype(f32)
        p_mid = rem.astype(bf16)
        p_lo = (rem - p_mid.astype(f32)).astype(bf16)
        ratio = SEL_BLOCK // CMP_STRIDE
        gsum = (lax.broadcasted_iota(jnp.int32, (n_bpad, n_cmp), 1) // ratio
                == lax.broadcasted_iota(jnp.int32, (n_bpad, n_cmp), 0)).astype(bf16)
        imp_t = _dot_nt(gsum, p_hi) + _dot_nt(gsum, p_mid) + _dot_nt(gsum, p_lo)
        blk = lax.broadcasted_iota(jnp.int32, (n_bpad, LANES), 0)
        jq = (past_len + lax.broadcasted_iota(jnp.int32, (n_bpad, LANES), 1) % t_new) // SEL_BLOCK
        forced = (blk == 0) | (blk == jq) | (blk == jq - 1)
        score = jnp.where(blk <= jq, imp_t + jnp.where(forced, FORCE, 0.0), NEG)
        blk_f = blk.astype(f32)
        sel_t = jnp.zeros((n_bpad, LANES), f32)
        for _ in range(N_SEL):
            m = jnp.max(score, axis=0, keepdims=True)
            idx = jnp.min(jnp.where(score == m, blk_f, float(n_bpad)), axis=0, keepdims=True)
            hit = blk_f == idx
            sel_t = jnp.where(hit & (m > 0.5 * NEG), 1.0, sel_t)
            score = jnp.where(hit, REMOVED, score)
        sel = jnp.concatenate([sel_t[k * LANES:(k + 1) * LANES].T for k in range(n_bpad // LANES)], axis=1)
        sel_scr[...] = sel[0:g2 * t_new]
        exp_scr[...] = (lax.broadcasted_iota(jnp.int32, (LANES, tile), 0)
                        == lax.broadcasted_iota(jnp.int32, (LANES, tile), 1) // SEL_BLOCK).astype(bf16)
        m_scr[...] = jnp.full(m_scr.shape, NEG, f32)
        l_scr[...] = jnp.zeros(l_scr.shape, f32)
        acc_scr[...] = jnp.zeros(acc_scr.shape, f32)

    def online_update(s_t, mk, v, v_feature_major):
        n = s_t.shape[-1]
        s4 = jnp.where(mk, s_t.reshape(g2, r4, t_new, n), NEG)
        m_run = m_scr[...].reshape(g2, r4, t_new, 1)
        m_new = jnp.maximum(m_run, jnp.max(s4, axis=-1, keepdims=True))
        alpha = jnp.exp2(m_run - m_new)
        p = jnp.exp2(s4 - m_new)
        l_new = alpha * l_scr[...].reshape(g2, r4, t_new, 1) + jnp.sum(p, axis=-1, keepdims=True)
        pb = p.reshape(n_rows, n).astype(bf16)
        pv = _dot_nt(pb, v) if v_feature_major else jnp.dot(pb, v, preferred_element_type=f32)
        m_scr[...] = m_new.reshape(n_rows, 1)
        l_scr[...] = l_new.reshape(n_rows, 1)
        acc_scr[...] = alpha.reshape(n_rows, 1) * acc_scr[...] + pv

    kt = jnp.concatenate([r[0:LANES, :] for r in pages], axis=1).astype(bf16)
    vt = jnp.concatenate([r[LANES:2 * LANES, :] for r in pages], axis=1).astype(bf16)
    shifted = pltpu.roll(sel_scr[...], (n_bpad - s * (tile // SEL_BLOCK)) % n_bpad, axis=1)
    picked = jnp.dot(shifted[:, 0:LANES].astype(bf16), exp_scr[...], preferred_element_type=f32)
    online_update(jnp.dot(qall, kt, preferred_element_type=f32), grouped(picked) > 0.5, vt, True)

    @pl.when(s == pl.num_programs(1) - 1)
    def _():
        new_blk = past_len // SEL_BLOCK
        kidx = lax.broadcasted_iota(jnp.int32, (n_rows, NEW_PAD), 1)
        causal = ((past_len + kidx) <= qpos) & (kidx < t_new)
        picked_new = sel_scr[:, new_blk:new_blk + 1]
        mk = (grouped(picked_new) > 0.5) & causal.reshape(g2, r4, t_new, NEW_PAD)
        online_update(_dot_nt(qall, ksn_ref[...]), mk, vsn_ref[...], False)
        o_s = acc_scr[...] * jnp.where(m_scr[...] > 0.5 * NEG, 1.0 / l_scr[...], 0.0)

        n_win = w_buf + NEW_PAD
        kw_t = win_ref[0:LANES, :].astype(bf16)
        vw_t = win_ref[LANES:2 * LANES, :].astype(bf16)
        widx = lax.broadcasted_iota(jnp.int32, (n_rows, n_win), 1)
        wpos = past_len - w_buf + widx
        wmask = (wpos <= qpos) & (wpos > qpos - WINDOW) & (wpos >= 0) & (widx < w_buf + t_new)
        s_w = jnp.concatenate([jnp.dot(qall, kw_t, preferred_element_type=f32), _dot_nt(qall, kwn_ref[...])],
                              axis=1)
        p_w = _softmax_rows(s_w, wmask).astype(bf16)
        o_w = (_dot_nt(p_w[:, 0:w_buf], vw_t)
               + jnp.dot(p_w[:, w_buf:], vwn_ref[...], preferred_element_type=f32))
        gate = gate_ref[...]
        o_ref[...] = gate[:, 0:1] * oc_scr[...] + gate[:, 1:2] * o_s + gate[:, 2:3] * o_w


def nsa_sample(q, gates, kc, vc, pool_t, page_table, ks_new, vs_new, win, kw_new, vw_new):
    f32, bf16 = jnp.float32, jnp.bfloat16
    bsz, t_new = q.shape[0], q.shape[1]
    n_pages = page_table.shape[1]
    pp = PAGES_PER_STEP
    w_buf = win.shape[2]
    r4, g2 = HEADS_PER_GROUP, KV_GROUPS
    n_rows = g2 * r4 * t_new
    past_len = n_pages * PAGE_SIZE
    assert past_len % SEL_BLOCK == 0 and t_new <= SEL_BLOCK and past_len >= w_buf and n_pages % pp == 0
    n_sel = past_len // SEL_BLOCK + 1
    n_bpad = -(-n_sel // LANES) * LANES
    eye = jnp.eye(g2, dtype=f32)
    qg = q.reshape(bsz, t_new, g2, r4, HEAD_DIM).transpose(0, 2, 3, 1, 4) * QK_SCALE
    qall = jnp.einsum('bgrqd,gk->bgrqkd', qg, eye).reshape(bsz, n_rows, LANES).astype(bf16)
    gall = gates.reshape(bsz, t_new, g2, r4, 3).transpose(0, 2, 3, 1, 4).reshape(bsz, n_rows, 3)
    pad_rows = lambda x: jnp.pad(x, ((0, 0), (0, NEW_PAD - t_new), (0, 0))).astype(bf16)
    n_cmp = kc.shape[1]
    grid_spec = pltpu.PrefetchScalarGridSpec(
        num_scalar_prefetch=1,
        grid=(bsz, n_pages // pp),
        in_specs=[_per_seq((n_rows, LANES)), _per_seq((n_rows, 3)), _per_seq((n_cmp, LANES)),
                  _per_seq((n_cmp, LANES))]
                 + [_page_spec(i, 1) for i in range(pp)]
                 + [_per_seq((NEW_PAD, LANES)), _per_seq((NEW_PAD, LANES)), _per_seq((2 * LANES, w_buf)),
                    _per_seq((NEW_PAD, LANES)), _per_seq((NEW_PAD, LANES))],
        out_specs=_per_seq((n_rows, LANES)),
        scratch_shapes=[pltpu.VMEM((g2 * t_new, n_bpad), f32), pltpu.VMEM((LANES, pp * PAGE_SIZE), bf16),
                        pltpu.VMEM((n_rows, LANES), f32),
                        pltpu.VMEM((n_rows, 1), f32), pltpu.VMEM((n_rows, 1), f32),
                        pltpu.VMEM((n_rows, LANES), f32)],
    )
    o = pl.pallas_call(
        functools.partial(_nsa_sample_kernel, n_pages=n_pages, t_new=t_new, w_buf=w_buf),
        grid_spec=grid_spec,
        out_shape=jax.ShapeDtypeStruct((bsz, n_rows, LANES), f32),
        compiler_params=_cparams("parallel", "arbitrary"),
        name="nsa_sample",
    )(page_table, qall, gall, kc, vc, *([pool_t] * pp), pad_rows(ks_new), pad_rows(vs_new), win,
      pad_rows(kw_new), pad_rows(vw_new))
    o = jnp.einsum('bgrqkd,gk->bqgrd', o.reshape(bsz, g2, r4, t_new, g2, HEAD_DIM), eye)
    return o.reshape(bsz, t_new, NSA_Q)


def _ssd_kernel(x_ref, b_ref, c_ref, dt_ref, a_ref, za_ref, zb_ref, dskip_ref, ng_ref, y_ref, hout_ref, h_scr, *,
                chunk):
    f32, bf16 = jnp.float32, jnp.bfloat16
    n_l = chunk
    hpg = SSD_HEADS // SSD_GROUPS
    gw = hpg * SSD_HEAD_DIM
    j = pl.program_id(1)

    @pl.when(j == 0)
    def _():
        h_scr[...] = jnp.zeros(h_scr.shape, f32)

    x = x_ref[...]
    dt = dt_ref[...]
    tri_b = (lax.broadcasted_iota(jnp.int32, (n_l, n_l), 0) >= lax.broadcasted_iota(jnp.int32, (n_l, n_l), 1))
    tri = tri_b.astype(bf16)
    cum = sum(jnp.dot(tri, part, preferred_element_type=f32) for part in _split3(dt * a_ref[...]))
    cum_t = cum.T
    dt_t = dt.T
    ecum = jnp.exp(cum)
    clast = cum[n_l - 1:n_l, :]
    wt = jnp.exp(clast - cum) * dt
    elast = jnp.exp(clast)
    lane = lax.broadcasted_iota(jnp.int32, (n_l, LANES), 1)
    low = lane < SSD_HEAD_DIM

    def pair(v, h0):
        return jnp.where(low[:v.shape[0]], v[:, h0:h0 + 1], v[:, h0 + 1:h0 + 2])

    tiles = []
    for g in range(SSD_GROUPS):
        bg = b_ref[:, g * SSD_STATE:(g + 1) * SSD_STATE]
        cgb = c_ref[:, g * SSD_STATE:(g + 1) * SSD_STATE].astype(bf16)
        bgt = bg.T.astype(bf16)
        cb = jnp.dot(cgb, bgt, preferred_element_type=f32)
        hg = h_scr[g]
        y_inter = jnp.dot(cgb, hg.astype(bf16), preferred_element_type=f32)
        xw, dec = [], []
        for pr in range(hpg // 2):
            h0 = hpg * g + 2 * pr
            xt = x[:, (h0 // 2) * LANES:(h0 // 2 + 1) * LANES]
            acc = None
            for k in range(2):
                h = h0 + k
                seg = cum[:, h:h + 1] - cum_t[h:h + 1, :]
                w = cb * jnp.exp(jnp.where(tri_b, seg, NEG)) * dt_t[h:h + 1, :]
                xm = jnp.where(low if k == 0 else jnp.logical_not(low), xt, 0.0).astype(bf16)
                part = jnp.dot(w.astype(bf16), xm, preferred_element_type=f32)
                acc = part if acc is None else acc + part
            tiles.append(acc + y_inter[:, pr * LANES:(pr + 1) * LANES] * pair(ecum, h0))
            xw.append((xt * pair(wt, h0)).astype(bf16))
            dec.append(pair(elast, h0))
        h_scr[g] = (hg * jnp.concatenate(dec, axis=1)
                    + jnp.dot(bgt, jnp.concatenate(xw, axis=1), preferred_element_type=f32))
    y = jnp.concatenate(tiles, axis=1) + dskip_ref[...] * x
    zg = jnp.concatenate([za_ref[...], zb_ref[...]], axis=1)
    v = y * (zg * jax.nn.sigmoid(zg))
    outs = []
    for g in range(SSD_GROUPS):
        vg = v[:, g * gw:(g + 1) * gw]
        outs.append(vg * lax.rsqrt(jnp.mean(vg * vg, axis=-1, keepdims=True) + RMS_EPS))
    y_ref[...] = jnp.concatenate(outs, axis=1) * ng_ref[...]

    @pl.when(j == pl.num_programs(1) - 1)
    def _():
        hout_ref[...] = h_scr[...]


def ssd_prompt(xbc, dt, a, z, zg_offset, d_skip, norm_g):
    f32 = jnp.float32
    bsz, t, _ = xbc.shape
    hpg = SSD_HEADS // SSD_GROUPS
    gn = SSD_GROUPS * SSD_STATE
    half = SSD_INNER // 2
    assert zg_offset % half == 0
    dt_p = jnp.pad(dt, ((0, 0), (0, 0), (0, LANES - SSD_HEADS)))
    a_p = jnp.pad(a.astype(f32), (0, LANES - SSD_HEADS)).reshape(1, LANES)
    dsk = jnp.repeat(d_skip.astype(f32), SSD_HEAD_DIM).reshape(1, SSD_INNER)
    blk = lambda w, c: pl.BlockSpec((None, SSD_CHUNK, w), lambda b, j: (b, j, c))
    full = lambda shape: pl.BlockSpec(shape, lambda b, j: (0,) * len(shape))
    state_spec = pl.BlockSpec((None, SSD_GROUPS, SSD_STATE, hpg * SSD_HEAD_DIM), lambda b, j: (b, 0, 0, 0))
    y, h = pl.pallas_call(
        functools.partial(_ssd_kernel, chunk=SSD_CHUNK),
        grid=(bsz, t // SSD_CHUNK),
        in_specs=[blk(SSD_INNER, 0), blk(gn, SSD_INNER // gn), blk(gn, SSD_INNER // gn + 1), blk(LANES, 0),
                  full((1, LANES)), blk(half, zg_offset // half), blk(half, zg_offset // half + 1),
                  full((1, SSD_INNER)), full((1, SSD_INNER))],
        out_specs=[blk(SSD_INNER, 0), state_spec],
        out_shape=[jax.ShapeDtypeStruct((bsz, t, SSD_INNER), f32),
                   jax.ShapeDtypeStruct((bsz, SSD_GROUPS, SSD_STATE, hpg * SSD_HEAD_DIM), f32)],
        scratch_shapes=[pltpu.VMEM((SSD_GROUPS, SSD_STATE, hpg * SSD_HEAD_DIM), f32)],
        compiler_params=_cparams("parallel", "arbitrary"),
        name="ssd_prompt",
    )(xbc, xbc, xbc, dt_p, a_p, z, z, dsk, norm_g.astype(f32).reshape(1, SSD_INNER))
    h = h.reshape(bsz, SSD_GROUPS, SSD_STATE, hpg, SSD_HEAD_DIM).transpose(0, 1, 3, 4, 2)
    return y, h.reshape(bsz, SSD_HEADS, SSD_HEAD_DIM, SSD_STATE)


def _inproj_even_kernel(x_ref, w_ref, c_ref, sa_ref, sb_ref,
                        u_ref, q_ref, rows_ref, kvw_ref, kvsb_ref, kvwb_ref, g_ref):
    bf16 = jnp.bfloat16
    z = jnp.dot(x_ref[...].astype(bf16), w_ref[...], preferred_element_type=jnp.float32)
    cos, s_up, s_down = c_ref[...], sa_ref[...], sb_ref[...]

    def rot(t):
        return (t * cos + pltpu.roll(t, LANES - ROT_DIM // 2, axis=1) * s_up
                + pltpu.roll(t, ROT_DIM // 2, axis=1) * s_down)

    tile = lambda k: z[:, k * LANES:(k + 1) * LANES]
    q0 = S5_DIM // LANES
    kv0 = q0 + NSA_Q // LANES
    u_ref[...] = z[:, 0:S5_DIM]
    q_ref[...] = jnp.concatenate([rot(tile(q0 + k)) for k in range(NSA_Q // LANES)], axis=1)
    kc, vc, ks, vs, kw, vw = (rot(tile(kv0)), tile(kv0 + 1), rot(tile(kv0 + 2)), tile(kv0 + 3),
                              rot(tile(kv0 + 4)), tile(kv0 + 5))
    rows_ref[...] = jnp.concatenate([kc, vc, ks, vs], axis=1)
    kvw = jnp.concatenate([kw, vw], axis=1)
    kvw_ref[...] = kvw
    kvwb_ref[...] = kvw.astype(bf16)
    kvsb_ref[...] = jnp.concatenate([ks, vs], axis=1).astype(bf16)
    g0 = (kv0 + 6) * LANES
    g_ref[...] = jax.nn.sigmoid(z[:, g0:g0 + 3 * N_HEADS])


def rope_tables(pos):
    half = ROT_DIM // 2
    inv = ROPE_THETA ** (-jnp.arange(half, dtype=jnp.float32) * 2.0 / ROT_DIM)
    ang = pos.astype(jnp.float32)[:, None] * inv[None, :]
    d = jnp.arange(LANES) % HEAD_DIM
    cos = jnp.take(jnp.cos(ang), d % half, axis=1)
    sin = jnp.take(jnp.sin(ang), d % half, axis=1)
    return (jnp.where(d < ROT_DIM, cos, 1.0), jnp.where(d < half, -sin, 0.0),
            jnp.where((d >= half) & (d < ROT_DIM), sin, 0.0))


def inproj_even(h, w_bf16, pos):
    f32, bf16 = jnp.float32, jnp.bfloat16
    n, d = h.shape
    tile = min(ROW_TILE, n)
    row = lambda w: pl.BlockSpec((tile, w), lambda i: (i, 0))
    fixed = lambda shape: pl.BlockSpec(shape, lambda i: (0, 0), pipeline_mode=pl.Buffered(1))
    widths = [(S5_DIM, f32), (NSA_Q, f32), (2 * NSA_KV, f32), (NSA_KV, f32), (NSA_KV, bf16), (NSA_KV, bf16),
              (3 * N_HEADS, f32)]
    return pl.pallas_call(
        _inproj_even_kernel,
        grid=(pl.cdiv(n, tile),),
        in_specs=[row(d), fixed(w_bf16.shape), row(LANES), row(LANES), row(LANES)],
        out_specs=[row(w) for w, _ in widths],
        out_shape=[jax.ShapeDtypeStruct((n, w), dt) for w, dt in widths],
        compiler_params=_cparams("parallel"),
        name="inproj_even",
    )(h, w_bf16, *rope_tables(pos))


def _causal_conv_tile(x, tail, w_ref, b_ref, width):
    row = lax.broadcasted_iota(jnp.int32, (TAIL, x.shape[1]), 0)
    acc = b_ref[...] + w_ref[width - 1:width, :] * x
    for k in range(1, width):
        xs = pltpu.roll(x, k, axis=0)
        head = jnp.where(row < k, pltpu.roll(tail, k, axis=0), xs[0:TAIL])
        xs = jnp.concatenate([head, xs[TAIL:]], axis=0)
        acc = acc + w_ref[width - 1 - k:width - k, :] * xs
    return acc


def _inproj_odd_kernel(x_ref, w_ref, scw_ref, scb_ref, cvw_ref, cvb_ref, dtb_ref,
                       ysc_ref, xbc_ref, dt_ref, zg_ref, tsc_ref, tx_ref, tail_sc, tail_x, *, tiles_per_seq):
    @pl.when(pl.program_id(0) % tiles_per_seq == 0)
    def _():
        tail_sc[...] = jnp.zeros(tail_sc.shape, jnp.float32)
        tail_x[...] = jnp.zeros(tail_x.shape, jnp.float32)

    z = jnp.dot(x_ref[...].astype(jnp.bfloat16), w_ref[...], preferred_element_type=jnp.float32)
    o_zg = 3 * SC_DIM
    o_x = o_zg + SSD_INNER
    o_dt = o_x + SSD_CONV_DIM
    n = z.shape[0]
    prod = z[:, 2 * SC_DIM:3 * SC_DIM] * z[:, 0:SC_DIM]
    ysc_ref[...] = z[:, SC_DIM:2 * SC_DIM] * _causal_conv_tile(prod, tail_sc[...], scw_ref, scb_ref, SC_WIDTH)
    xbc = z[:, o_x:o_dt]
    c = _causal_conv_tile(xbc, tail_x[...], cvw_ref, cvb_ref, SSD_CONV)
    xbc_ref[...] = c * jax.nn.sigmoid(c)
    dt_ref[...] = jax.nn.softplus(z[:, o_dt:o_dt + SSD_HEADS] + dtb_ref[...])
    zg_ref[...] = z[:, o_zg:o_x]
    tail_sc[...] = prod[n - TAIL:n]
    tail_x[...] = xbc[n - TAIL:n]
    tsc_ref[...] = prod[n - TAIL:n]
    tx_ref[...] = xbc[n - TAIL:n]


def inproj_odd_prompt(h, w_bf16, bsz, sc_w, sc_b, cv_w, cv_b, dt_bias):
    f32 = jnp.float32
    n_rows, d = h.shape
    t = n_rows // bsz
    assert t % ROW_TILE == 0
    tps = t // ROW_TILE
    row = lambda w: pl.BlockSpec((ROW_TILE, w), lambda i: (i, 0))
    fixed = lambda shape: pl.BlockSpec(shape, lambda i: (0,) * len(shape), pipeline_mode=pl.Buffered(1))
    last = lambda w: pl.BlockSpec((None, TAIL, w), lambda i: (i // tps, 0, 0))
    ysc, xbc, dt, zg, tsc, tx = pl.pallas_call(
        functools.partial(_inproj_odd_kernel, tiles_per_seq=tps),
        grid=(n_rows // ROW_TILE,),
        in_specs=[row(d), fixed(w_bf16.shape), fixed((SC_WIDTH, SC_DIM)), fixed((1, SC_DIM)),
                  fixed((SSD_CONV, SSD_CONV_DIM)), fixed((1, SSD_CONV_DIM)), fixed((1, SSD_HEADS))],
        out_specs=[row(SC_DIM), row(SSD_CONV_DIM), row(SSD_HEADS), row(SSD_INNER), last(SC_DIM), last(SSD_CONV_DIM)],
        out_shape=[jax.ShapeDtypeStruct((n_rows, SC_DIM), f32), jax.ShapeDtypeStruct((n_rows, SSD_CONV_DIM), f32),
                   jax.ShapeDtypeStruct((n_rows, SSD_HEADS), f32), jax.ShapeDtypeStruct((n_rows, SSD_INNER), f32),
                   jax.ShapeDtypeStruct((bsz, TAIL, SC_DIM), f32), jax.ShapeDtypeStruct((bsz, TAIL, SSD_CONV_DIM), f32)],
        scratch_shapes=[pltpu.VMEM((TAIL, SC_DIM), f32), pltpu.VMEM((TAIL, SSD_CONV_DIM), f32)],
        compiler_params=_cparams("arbitrary"),
        name="inproj_odd",
    )(h, w_bf16, sc_w.astype(f32), sc_b.astype(f32).reshape(1, SC_DIM), cv_w.astype(f32),
      cv_b.astype(f32).reshape(1, SSD_CONV_DIM), dt_bias.astype(f32).reshape(1, SSD_HEADS))
    seq = lambda a: a.reshape(bsz, t, a.shape[-1])
    return (seq(ysc), seq(xbc), seq(dt), seq(zg),
            tsc[:, TAIL - (SC_WIDTH - 1):], tx[:, TAIL - (SSD_CONV - 1):])


def layer_norm(x, g, b):
    mu = jnp.mean(x, -1, keepdims=True)
    xc = x - mu
    var = jnp.mean(xc * xc, -1, keepdims=True)
    return xc * lax.rsqrt(var + LN_EPS) * g + b


def last_rows(x, n):
    t = x.shape[1]
    if t < n:
        x = jnp.pad(x, [(0, 0), (n - t, 0)] + [(0, 0)] * (x.ndim - 2))
    return x[:, x.shape[1] - n:]


def causal_conv(x, buf, w, b):
    t = x.shape[1]
    width = w.shape[0]
    xp = jnp.concatenate([buf, x], axis=1)
    y = b + sum(xp[:, j:j + t] * w[j] for j in range(width))
    return y, xp[:, xp.shape[1] - (width - 1):]


def even_prompt_mix(h, w_in_bf16, bt, s5p, cmpp, w_buf):
    t = h.shape[0] // bt
    u, q, rows, kvw, kvs_b, kvw_b, gates = inproj_even(h, w_in_bf16, jnp.arange(h.shape[0]) % t)
    seq = lambda a: a.reshape(bt, t, a.shape[-1])
    feat = KV_GROUPS * HEAD_DIM
    y_s5, s5_state = s5_scan(seq(u), jnp.zeros((bt, S5_GROUPS, S5_STATE, 2), jnp.float32), s5p, S5_CHUNK)
    rows = seq(rows)
    kc = compress_prompt(rows[..., 0:feat], compress_params(cmpp[0], cmpp[1], cmpp[2]))
    vc = compress_prompt(rows[..., feat:2 * feat], compress_params(cmpp[3], cmpp[4], cmpp[5]))
    y_nsa = nsa_prompt(seq(q), seq(gates), kc, vc, seq(kvs_b), jnp.pad(seq(kvw_b), ((0, 0), (WINDOW, 0), (0, 0))))
    new_rows = rows.reshape(bt, t, 4, KV_GROUPS, HEAD_DIM)
    return (y_s5, y_nsa), s5_state, new_rows, last_rows(seq(kvw).reshape(bt, t, 2, KV_GROUPS, HEAD_DIM), w_buf)


def even_sample_mix(h, w_in_bf16, bt, s5_h0, pool, page_table, win_buf, s5p, cmpp):
    f32 = jnp.float32
    t = h.shape[0] // bt
    pos = page_table.shape[1] * PAGE_SIZE + jnp.arange(h.shape[0]) % t
    u, q, rows, kvw, _, _, gates = inproj_even(h, w_in_bf16, pos)
    seq = lambda a: a.reshape(bt, t, a.shape[-1])
    feat = KV_GROUPS * HEAD_DIM
    y_s5, s5_state = s5_scan(seq(u), s5_h0.astype(f32), s5p, t)
    rows, kvw = seq(rows), seq(kvw)
    pool_t = pool.astype(f32).transpose(0, 2, 3, 4, 1).reshape(pool.shape[0], 4 * feat, PAGE_SIZE)
    kc, vc = compress_sample(pool_t, page_table, rows[..., 0:feat], rows[..., feat:2 * feat],
                             compress_sample_params(cmpp[0], cmpp[1], cmpp[2]),
                             compress_sample_params(cmpp[3], cmpp[4], cmpp[5]))
    w_buf = win_buf.shape[1]
    win_f = win_buf.astype(f32)
    y_nsa = nsa_sample(q.reshape(bt, t, N_HEADS, HEAD_DIM), gates.reshape(bt, t, N_HEADS, 3), kc, vc, pool_t,
                       page_table, rows[..., 2 * feat:3 * feat], rows[..., 3 * feat:4 * feat],
                       win_f.transpose(0, 2, 3, 4, 1).reshape(bt, 2 * feat, w_buf), kvw[..., 0:feat],
                       kvw[..., feat:2 * feat])
    new_rows = rows.reshape(bt, t, 4, KV_GROUPS, HEAD_DIM)
    win = jnp.concatenate([win_f, kvw.reshape(bt, t, 2, KV_GROUPS, HEAD_DIM)], axis=1)
    return (y_s5, y_nsa), s5_state, new_rows, win[:, t:]


def ssd_scan(x, dt, a, bm, cm, h0, chunk):
    bt, t, nh, p = x.shape
    nch = t // chunk
    r = nh // SSD_GROUPS
    tri = jnp.arange(chunk)[:, None] >= jnp.arange(chunk)[None, :]

    def to_chunks(v):
        return jnp.moveaxis(v.reshape((bt, nch, chunk) + v.shape[2:]), 1, 0)

    def step(h, inp):
        xc, dtc, bc, cc = inp
        cum = jnp.cumsum(dtc * a, axis=1)
        seg = cum[:, :, None, :] - cum[:, None, :, :]
        decay = jnp.exp(jnp.where(tri[None, :, :, None], seg, NEG)).reshape(bt, chunk, chunk, SSD_GROUPS, r)
        cb = jnp.einsum('btgn,bsgn->btsg', cc, bc)
        xg = xc.reshape(bt, chunk, SSD_GROUPS, r, p)
        dg = dtc.reshape(bt, chunk, SSD_GROUPS, r)
        w = cb[..., None] * decay * dg[:, None]
        y_intra = jnp.einsum('btsgr,bsgrp->btgrp', w, xg)
        hg = h.reshape(bt, SSD_GROUPS, r, p, SSD_STATE)
        y_inter = jnp.einsum('btgn,bgrpn->btgrp', cc, hg) * jnp.exp(cum).reshape(bt, chunk, SSD_GROUPS, r)[..., None]
        wt = (jnp.exp(cum[:, -1:, :] - cum) * dtc).reshape(bt, chunk, SSD_GROUPS, r)
        h_new = (hg * jnp.exp(cum[:, -1]).reshape(bt, SSD_GROUPS, r)[..., None, None]
                 + jnp.einsum('bsgr,bsgrp,bsgn->bgrpn', wt, xg, bc))
        return h_new.reshape(bt, nh, p, SSD_STATE), (y_intra + y_inter).reshape(bt, chunk, nh, p)

    h_fin, ys = lax.scan(step, h0, (to_chunks(x), to_chunks(dt), to_chunks(bm), to_chunks(cm)))
    return jnp.moveaxis(ys, 0, 1).reshape(bt, t, nh, p), h_fin


def gated_rmsnorm(y, z, g):
    v = y * jax.nn.silu(z)
    bt, t, _ = v.shape
    vg = v.reshape(bt, t, SSD_GROUPS, SSD_INNER // SSD_GROUPS)
    vg = vg * lax.rsqrt(jnp.mean(vg * vg, -1, keepdims=True) + RMS_EPS)
    return vg.reshape(bt, t, SSD_INNER) * g


def odd_prompt_mix(h, w_in_bf16, bsz, sc_w, sc_b, cv_w, cv_b, dt_bias, a_log, d_skip, norm_g):
    a = -jnp.exp(a_log.astype(jnp.float32))
    y_sc, xbc_c, dt, zg, new_sc, new_conv = inproj_odd_prompt(h, w_in_bf16, bsz, sc_w, sc_b, cv_w, cv_b, dt_bias)
    y, h_new = ssd_prompt(xbc_c, dt, a, zg, 0, d_skip, norm_g)
    return (y_sc, y), new_sc, new_conv, h_new


def odd_mix(z, sc_buf, conv_buf, h0, chunk, sc_w, sc_b, cv_w, cv_b, dt_bias, a_log, d_skip, norm_g):
    f32 = jnp.float32
    bt, t, _ = z.shape
    a = -jnp.exp(a_log.astype(f32))
    o1 = SC_DIM
    o2 = 2 * SC_DIM
    o3 = 3 * SC_DIM
    o4 = o3 + SSD_INNER
    o5 = o4 + SSD_CONV_DIM
    sc_h = z[..., :o1]
    sc_bg = z[..., o1:o2]
    sc_cg = z[..., o2:o3]
    zg = z[..., o3:o4]
    xbc = z[..., o4:o5]
    dt_raw = z[..., o5:]
    conv_sc, new_sc = causal_conv(sc_cg * sc_h, sc_buf.astype(f32), sc_w, sc_b)
    y_sc = sc_bg * conv_sc
    xbc_c, new_conv = causal_conv(xbc, conv_buf.astype(f32), cv_w, cv_b)
    xbc_c = jax.nn.silu(xbc_c)
    gn = SSD_GROUPS * SSD_STATE
    xs = xbc_c[..., :SSD_INNER].reshape(bt, t, SSD_HEADS, SSD_HEAD_DIM)
    bm = xbc_c[..., SSD_INNER:SSD_INNER + gn].reshape(bt, t, SSD_GROUPS, SSD_STATE)
    cm = xbc_c[..., SSD_INNER + gn:].reshape(bt, t, SSD_GROUPS, SSD_STATE)
    dt = jax.nn.softplus((dt_raw + dt_bias).astype(f32))
    y, h_new = ssd_scan(xs, dt, a, bm, cm, h0.astype(f32), chunk)
    y = (y + d_skip[:, None] * xs).reshape(bt, t, SSD_INNER)
    y = gated_rmsnorm(y, zg, norm_g)
    return (y_sc, y), new_sc, new_conv, h_new


def moe_ffn(x, w_r, b_r, w_gu_bf16, w_down_bf16):
    n, d = x.shape
    logits = jnp.dot(x, w_r, precision=lax.Precision.HIGHEST) + b_r
    top_v, top_i = lax.top_k(logits, TOP_K)
    gate = jax.nn.softmax(top_v, axis=-1)
    flat_e = top_i.reshape(-1)
    blk = 128
    assert (TOP_K * n) % blk == 0
    onehot = jax.nn.one_hot(flat_e, N_EXPERTS, dtype=jnp.float32).reshape(-1, blk, N_EXPERTS)
    tri = (jnp.arange(blk)[:, None] >= jnp.arange(blk)[None, :]).astype(jnp.float32)
    local = jnp.einsum('ij,bjk->bik', tri, onehot)
    block_total = local[:, -1, :]
    block_off = jnp.cumsum(block_total, axis=0) - block_total
    incl = (local + block_off[:, None, :]).reshape(-1, N_EXPERTS)
    rank = jnp.take_along_axis(incl, flat_e[:, None], axis=1)[:, 0].astype(jnp.int32) - 1
    counts = jnp.sum(block_total, axis=0).astype(jnp.int32)
    padded = ((counts + ROW_TILE - 1) // ROW_TILE) * ROW_TILE
    pad_start = jnp.cumsum(padded) - padded
    dest = (pad_start[flat_e] + rank).astype(jnp.int32)
    n_tiles = (TOP_K * n) // ROW_TILE + N_EXPERTS
    rows = n_tiles * ROW_TILE
    row_token = jnp.zeros((rows,), jnp.int32).at[dest].set(jnp.arange(TOP_K * n, dtype=jnp.int32) // TOP_K,
                                                           unique_indices=True, mode='promise_in_bounds')
    tile_end = jnp.cumsum(padded) // ROW_TILE
    tile_expert = jnp.minimum(jnp.searchsorted(tile_end, jnp.arange(n_tiles), side='right'),
                              N_EXPERTS - 1).astype(jnp.int32)
    n_used = tile_end[-1:].astype(jnp.int32)
    xs = x.at[row_token].get(mode='promise_in_bounds')
    ys = grouped_ffn(xs, w_gu_bf16, w_down_bf16, tile_expert, n_used)
    dest = dest.reshape(n, TOP_K)
    y0 = ys.at[dest[:, 0]].get(mode='promise_in_bounds')
    y1 = ys.at[dest[:, 1]].get(mode='promise_in_bounds')
    return gate[:, 0:1] * y0 + gate[:, 1:2] * y1


def kernel(x_prompt, x_sample, state_s5, cache_nsa_kv, state_win_kv, state_sc_conv, state_ssd_conv, state_ssd,
           page_table, ln_g, ln_b, w_in_even, s5_lam_re, s5_lam_im, s5_log_dt, s5_b, s5_c, s5_d, s5_w_glu,
           nsa_wk1, nsa_wk2, nsa_pe_k, nsa_wv1, nsa_wv2, nsa_pe_v, w_out_even, ffn_w_gu, ffn_w_down,
           w_in_odd, sc_conv_w, sc_conv_b, ssd_conv_w, ssd_conv_b, ssd_dt_bias, ssd_a_log, ssd_d, ssd_norm_g,
           w_out_odd, moe_router, moe_router_b, moe_w_gu, moe_w_down):
    f32 = jnp.float32
    bf16 = jnp.bfloat16
    bp, tp, d = x_prompt.shape
    bs, ts, _ = x_sample.shape
    n_p = bp * tp
    n_s = bs * ts
    w_buf = state_win_kv.shape[2]
    streams = [x_prompt.astype(f32).reshape(n_p, d), x_sample.astype(f32).reshape(n_s, d)]
    shapes = [(bp, tp), (bs, ts)]

    def flat(parts, n_rows):
        return [p.reshape(n_rows, p.shape[-1]) for p in parts]

    def out_proj(parts, w_out, width, h, g, b):
        w = w_out.astype(bf16)
        return matmul(flat(parts, h.shape[0]), [w[:width], w[width:]], ln=(h, g, b))

    def single_expert(n_rows):
        n_tiles = pl.cdiv(n_rows, min(ROW_TILE, n_rows))
        return jnp.zeros((n_tiles,), jnp.int32), jnp.full((1,), n_tiles, jnp.int32)

    s5p = s5_params(s5_lam_re[0], s5_lam_im[0], s5_log_dt[0], s5_b[0], s5_c[0], s5_d[0], s5_w_glu[0])
    cmpp = (nsa_wk1[0], nsa_wk2[0], nsa_pe_k[0], nsa_wv1[0], nsa_wv2[0], nsa_pe_v[0])
    w_in = w_in_even[0].astype(bf16)
    mix_p, s5_p, kv_p, win_p = even_prompt_mix(streams[0], w_in, bp, s5p, cmpp, w_buf)
    mix_s, s5_s, kv_s, win_s = even_sample_mix(streams[1], w_in, bs, state_s5[0], cache_nsa_kv[0], page_table,
                                               state_win_kv[0], s5p, cmpp)
    streams = [out_proj(mix, w_out_even[0], S5_DIM, h, ln_g[0, 0], ln_b[0, 0])
               for mix, h in zip((mix_p, mix_s), streams)]
    w_gu, w_down = to_bf16(ffn_w_gu), to_bf16(ffn_w_down)
    streams = [grouped_ffn(h, w_gu, w_down, *single_expert(h.shape[0]), ln=(ln_g[0, 1], ln_b[0, 1]))
               for h in streams]

    oddp = (sc_conv_w[0], sc_conv_b[0], ssd_conv_w[0], ssd_conv_b[0], ssd_dt_bias[0],
            ssd_a_log[0], ssd_d[0], ssd_norm_g[0])
    w_in = w_in_odd[0].astype(bf16)
    mix_p, scc_p, sdc_p, ssd_p = odd_prompt_mix(streams[0], w_in, bp, *oddp)
    zs = matmul([streams[1]], [w_in]).reshape(bs, ts, -1)
    mix_s, scc_s, sdc_s, ssd_s = odd_mix(zs, state_sc_conv[0], state_ssd_conv[0], state_ssd[0], ts, *oddp)
    streams = [out_proj(mix, w_out_odd[0], SC_DIM, h, ln_g[1, 0], ln_b[1, 0])
               for mix, h in zip((mix_p, mix_s), streams)]
    h = jnp.concatenate(streams, axis=0)
    f = moe_ffn(h, moe_router[0], moe_router_b[0], to_bf16(moe_w_gu[0]), to_bf16(moe_w_down[0]))
    h = layer_norm(ALPHA * h + f, ln_g[1, 1], ln_b[1, 1])

    hp = h[:n_p].reshape(bp, tp, d)
    hs = h[n_p:].reshape(bs, ts, d)
    st = lambda a, ref: a[None].astype(ref.dtype)
    return (hp.astype(x_prompt.dtype), hs.astype(x_sample.dtype),
            st(s5_p, state_s5), st(s5_s, state_s5),
            st(kv_p, cache_nsa_kv), st(kv_s, cache_nsa_kv),
            st(win_p, state_win_kv), st(win_s, state_win_kv),
            st(scc_p, state_sc_conv), st(scc_s, state_sc_conv),
            st(sdc_p, state_ssd_conv), st(sdc_s, state_ssd_conv),
            st(ssd_p, state_ssd), st(ssd_s, state_ssd))
```

```python
import functools
import math

import jax
import jax.numpy as jnp
from jax import lax
from jax.experimental import pallas as pl
from jax.experimental.pallas import tpu as pltpu

D_MODEL = 1024
DEPTH = 2
ALPHA = (2.0 * DEPTH) ** 0.25
LN_EPS = 1e-5
RMS_EPS = 1e-5
NEG = -1e30

S5_DIM = D_MODEL // 2
S5_GROUP = 16
S5_GROUPS = S5_DIM // S5_GROUP
S5_STATE = 64

HEAD_DIM = 64
N_HEADS = (D_MODEL // 2) // HEAD_DIM
KV_GROUPS = 2
HEADS_PER_GROUP = N_HEADS // KV_GROUPS
CMP_STRIDE = 16
CMP_LEN = 2 * CMP_STRIDE
SEL_BLOCK = 64
N_SEL = 16
WINDOW = 512
Q_BLOCK = 128
ROPE_THETA = 500000.0
ROT_DIM = HEAD_DIM // 4
FORCE = 1e4
NSA_Q = N_HEADS * HEAD_DIM
NSA_KV = 2 * KV_GROUPS * HEAD_DIM

SC_DIM = D_MODEL // 2
SC_WIDTH = 3
SSD_HEAD_DIM = 64
SSD_HEADS = 16
SSD_INNER = SSD_HEADS * SSD_HEAD_DIM
SSD_GROUPS = 4
SSD_STATE = 128
SSD_CONV = 4
SSD_CONV_DIM = SSD_INNER + 2 * SSD_GROUPS * SSD_STATE
SSD_CHUNK = 128

D_FF = 2816
N_EXPERTS = 8
TOP_K = 2

VMEM_LIMIT_BYTES = 56 * 1024 * 1024
LANES = 128
S5_N = S5_GROUPS * S5_STATE
S5_LT = S5_N // LANES
S5_CHUNK = 256
SEL_TILE = 1024
QK_SCALE = HEAD_DIM ** -0.5 * math.log2(math.e)
REMOVED = -3e38
PAGE_SIZE = 128
PAGES_PER_STEP = 32
NEW_PAD = 128
CAST_ROWS = 256
CAST_SPLIT = 4
TAIL = 8
ROW_TILE = 512
FF_TILE = D_FF // 2


def _cparams(*sem):
    return pltpu.CompilerParams(dimension_semantics=sem, vmem_limit_bytes=VMEM_LIMIT_BYTES)


def _deepnorm(resid, update, g, b):
    y = ALPHA * resid + update
    mu = jnp.mean(y, axis=-1, keepdims=True)
    yc = y - mu
    var = jnp.mean(yc * yc, axis=-1, keepdims=True)
    return yc * lax.rsqrt(var + LN_EPS) * g + b


def _mm_kernel(*refs, n_in, fuse_ln):
    xs, ws = refs[0:n_in], refs[n_in:2 * n_in]
    o_ref = refs[-1]
    acc = None
    for x_ref, w_ref in zip(xs, ws):
        part = jnp.dot(x_ref[...].astype(jnp.bfloat16), w_ref[...], preferred_element_type=jnp.float32)
        acc = part if acc is None else acc + part
    if fuse_ln:
        r_ref, g_ref, b_ref = refs[2 * n_in:2 * n_in + 3]
        acc = _deepnorm(r_ref[...], acc, g_ref[...], b_ref[...])
    o_ref[...] = acc


def matmul(xs, ws_bf16, ln=None):
    m = xs[0].shape[0]
    n = ws_bf16[0].shape[1]
    tile = min(ROW_TILE, m)
    row = lambda width: pl.BlockSpec((tile, width), lambda i: (i, 0))
    fixed = lambda shape: pl.BlockSpec(shape, lambda i: (0, 0), pipeline_mode=pl.Buffered(1))
    in_specs = [row(x.shape[1]) for x in xs] + [fixed(w.shape) for w in ws_bf16]
    args = list(xs) + list(ws_bf16)
    if ln is not None:
        resid, g, b = ln
        in_specs += [row(n), fixed((1, n)), fixed((1, n))]
        args += [resid, g.reshape(1, n), b.reshape(1, n)]
    return pl.pallas_call(
        functools.partial(_mm_kernel, n_in=len(xs), fuse_ln=ln is not None),
        grid=(pl.cdiv(m, tile),),
        in_specs=in_specs,
        out_specs=row(n),
        out_shape=jax.ShapeDtypeStruct((m, n), jnp.float32),
        compiler_params=_cparams("parallel"),
        name="matmul",
    )(*args)


def _cast_kernel(*refs):
    o_ref = refs[-1]
    o_ref[...] = jnp.concatenate([r[...].astype(o_ref.dtype) for r in refs[:-1]], axis=1)


def to_bf16(w):
    shape = w.shape
    w2 = w.reshape(-1, shape[-1])
    rows, cols = w2.shape
    split = CAST_SPLIT if cols % (CAST_SPLIT * LANES) == 0 else 1
    out = pl.pallas_call(
        _cast_kernel,
        grid=(pl.cdiv(rows, CAST_ROWS),),
        in_specs=[pl.BlockSpec((CAST_ROWS, cols // split), lambda i, c=c: (i, c)) for c in range(split)],
        out_specs=pl.BlockSpec((CAST_ROWS, cols), lambda i: (i, 0)),
        out_shape=jax.ShapeDtypeStruct((rows, cols), jnp.bfloat16),
        compiler_params=_cparams("parallel"),
        name="to_bf16",
    )(*([w2] * split))
    return out.reshape(shape)


def _ffn_kernel(te_ref, nt_ref, x_ref, wg_ref, wu_ref, wd_ref, *rest, fuse_ln):
    o_ref = rest[-1]
    t = pl.program_id(0)
    j = pl.program_id(1)

    @pl.when(t < nt_ref[0])
    def _():
        x = x_ref[...].astype(jnp.bfloat16)
        g = jnp.dot(x, wg_ref[...], preferred_element_type=jnp.float32)
        u = jnp.dot(x, wu_ref[...], preferred_element_type=jnp.float32)
        h = (g * jax.nn.sigmoid(g) * u).astype(jnp.bfloat16)
        part = jnp.dot(h, wd_ref[...], preferred_element_type=jnp.float32)

        @pl.when(j == 0)
        def _():
            o_ref[...] = part

        @pl.when(j > 0)
        def _():
            if fuse_ln:
                o_ref[...] = _deepnorm(x_ref[...], o_ref[...] + part, rest[0][...], rest[1][...])
            else:
                o_ref[...] += part

    @pl.when(jnp.logical_and(t >= nt_ref[0], j == 0))
    def _():
        o_ref[...] = jnp.zeros_like(o_ref)


def grouped_ffn(x, w_gu_bf16, w_down_bf16, tile_expert, n_tiles_used, ln=None):
    r, d = x.shape
    nf = D_FF // FF_TILE
    assert nf == 2
    tile = min(ROW_TILE, r)
    n_tiles = pl.cdiv(r, tile)
    in_specs = [
        pl.BlockSpec((tile, d), lambda t, j, te, nt: (t, 0)),
        pl.BlockSpec((None, d, FF_TILE), lambda t, j, te, nt: (te[t], 0, j)),
        pl.BlockSpec((None, d, FF_TILE), lambda t, j, te, nt: (te[t], 0, nf + j)),
        pl.BlockSpec((None, FF_TILE, d), lambda t, j, te, nt: (te[t], j, 0)),
    ]
    args = [tile_expert, n_tiles_used, x, w_gu_bf16, w_gu_bf16, w_down_bf16]
    if ln is not None:
        in_specs += [pl.BlockSpec((1, d), lambda t, j, te, nt: (0, 0))] * 2
        args += [ln[0].reshape(1, d), ln[1].reshape(1, d)]
    grid_spec = pltpu.PrefetchScalarGridSpec(
        num_scalar_prefetch=2,
        grid=(n_tiles, nf),
        in_specs=in_specs,
        out_specs=pl.BlockSpec((tile, d), lambda t, j, te, nt: (t, 0)),
    )
    return pl.pallas_call(
        functools.partial(_ffn_kernel, fuse_ln=ln is not None),
        grid_spec=grid_spec,
        out_shape=jax.ShapeDtypeStruct((r, d), jnp.float32),
        compiler_params=_cparams("parallel", "arbitrary"),
        name="grouped_ffn",
    )(*args)


def _s5_kernel(u_ref, perm_ref, h0r_ref, h0i_ref, ar_ref, ai_ref, bbr_ref, bbi_ref, cr_ref, ci_ref, d_ref, wglu_ref,
               y_ref, hro_ref, hio_ref, bur, bui, sr, si, hr, hi, *, chains, chunk):
    j = pl.program_id(0)

    @pl.when(j == 0)
    def _():
        hr[...] = h0r_ref[...]
        hi[...] = h0i_ref[...]

    rows_n = chains * chunk
    u = u_ref[...].reshape(rows_n, S5_DIM)
    to_tc = perm_ref[...]
    ub = jnp.dot(to_tc, u.astype(jnp.bfloat16), preferred_element_type=jnp.float32).astype(jnp.bfloat16)
    hd, hn = S5_DIM // 2, S5_N // 2

    def b_proj(w_ref):
        return jnp.concatenate([jnp.dot(ub[:, h * hd:(h + 1) * hd], w_ref[h * hd:(h + 1) * hd, h * hn:(h + 1) * hn],
                                        preferred_element_type=jnp.float32) for h in range(2)], axis=1)

    bu_r = b_proj(bbr_ref)
    bu_i = b_proj(bbi_ref)
    for k in range(S5_LT):
        bur[k] = bu_r[:, k * LANES:(k + 1) * LANES]
        bui[k] = bu_i[:, k * LANES:(k + 1) * LANES]
    ar = [jnp.broadcast_to(ar_ref[:, k * LANES:(k + 1) * LANES], (chains, LANES)) for k in range(S5_LT)]
    ai = [jnp.broadcast_to(ai_ref[:, k * LANES:(k + 1) * LANES], (chains, LANES)) for k in range(S5_LT)]

    def body(t, carry):
        rows = pl.ds(t * chains, chains)
        out = []
        for k in range(S5_LT):
            xr, xi = carry[2 * k], carry[2 * k + 1]
            nr = ar[k] * xr - ai[k] * xi + bur[k, rows, :]
            ni = ar[k] * xi + ai[k] * xr + bui[k, rows, :]
            sr[k, rows, :] = nr
            si[k, rows, :] = ni
            out += [nr, ni]
        return tuple(out)

    init = []
    for k in range(S5_LT):
        init += [hr[:, k * LANES:(k + 1) * LANES], hi[:, k * LANES:(k + 1) * LANES]]
    fin = lax.fori_loop(0, chunk, body, tuple(init), unroll=2)
    xr = jnp.concatenate(fin[0::2], axis=1)
    xi = jnp.concatenate(fin[1::2], axis=1)
    hr[...] = xr
    hi[...] = xi
    hro_ref[...] = xr
    hio_ref[...] = xi
    s_r = jnp.concatenate([sr[k] for k in range(S5_LT)], axis=1).astype(jnp.bfloat16)
    s_i = jnp.concatenate([si[k] for k in range(S5_LT)], axis=1).astype(jnp.bfloat16)
    y = jnp.concatenate(
        [jnp.dot(s_r[:, h * hn:(h + 1) * hn], cr_ref[h * hn:(h + 1) * hn, h * hd:(h + 1) * hd],
                 preferred_element_type=jnp.float32)
         - jnp.dot(s_i[:, h * hn:(h + 1) * hn], ci_ref[h * hn:(h + 1) * hn, h * hd:(h + 1) * hd],
                   preferred_element_type=jnp.float32) for h in range(2)], axis=1)
    y = sum(lax.dot_general(to_tc, part, (((0,), (0,)), ((), ())), preferred_element_type=jnp.float32)
            for part in _split3(y)) + d_ref[...] * u
    z = jax.nn.gelu(y)
    gate = jax.nn.sigmoid(jnp.dot(z.astype(jnp.bfloat16), wglu_ref[...], preferred_element_type=jnp.float32))
    y_ref[...] = (z * gate).reshape(chains, chunk, S5_DIM)


def s5_params(lam_re, lam_im, log_dt, b, c, d, w_glu):
    f32 = jnp.float32
    dt = jnp.exp(log_dt.astype(f32))[:, None]
    mag = jnp.exp(lam_re * dt)
    ang = lam_im * dt
    ab_re = mag * jnp.cos(ang)
    ab_im = mag * jnp.sin(ang)
    den = lam_re * lam_re + lam_im * lam_im
    nr = ab_re - 1.0
    coef_re = (nr * lam_re + ab_im * lam_im) / den
    coef_im = (ab_im * lam_re - nr * lam_im) / den
    b_re = b[..., 0].astype(f32)
    b_im = b[..., 1].astype(f32)
    bb_re = coef_re[..., None] * b_re - coef_im[..., None] * b_im
    bb_im = coef_re[..., None] * b_im + coef_im[..., None] * b_re
    eye = jnp.eye(S5_GROUPS, dtype=f32)
    bbr = jnp.einsum('gnk,gh->gkhn', bb_re, eye).reshape(S5_DIM, S5_N).astype(jnp.bfloat16)
    bbi = jnp.einsum('gnk,gh->gkhn', bb_im, eye).reshape(S5_DIM, S5_N).astype(jnp.bfloat16)
    cr = jnp.einsum('gkn,gh->gnhk', c[..., 0].astype(f32), eye).reshape(S5_N, S5_DIM).astype(jnp.bfloat16)
    ci = jnp.einsum('gkn,gh->gnhk', c[..., 1].astype(f32), eye).reshape(S5_N, S5_DIM).astype(jnp.bfloat16)
    return (ab_re.reshape(1, S5_N), ab_im.reshape(1, S5_N), bbr, bbi, cr, ci,
            d.astype(f32).reshape(1, S5_DIM), w_glu.astype(jnp.bfloat16))


def s5_scan(u, h0, params, chunk):
    chains, t, _ = u.shape
    ar, ai, bbr, bbi, cr, ci, d, wglu = params
    h0r = h0[..., 0].reshape(chains, S5_N)
    h0i = h0[..., 1].reshape(chains, S5_N)
    full = lambda shape: pl.BlockSpec(shape, lambda j: (0,) * len(shape))
    rows = chains * chunk
    r = jnp.arange(rows)
    to_tc = (r[None, :] == (r[:, None] % chains) * chunk + r[:, None] // chains).astype(jnp.bfloat16)
    y, hr, hi = pl.pallas_call(
        functools.partial(_s5_kernel, chains=chains, chunk=chunk),
        grid=(t // chunk,),
        in_specs=[pl.BlockSpec((chains, chunk, S5_DIM), lambda j: (0, j, 0)), full((rows, rows)),
                  full((chains, S5_N)), full((chains, S5_N)), full((1, S5_N)), full((1, S5_N)),
                  full((S5_DIM, S5_N)), full((S5_DIM, S5_N)), full((S5_N, S5_DIM)), full((S5_N, S5_DIM)),
                  full((1, S5_DIM)), full((S5_DIM, S5_DIM))],
        out_specs=[pl.BlockSpec((chains, chunk, S5_DIM), lambda j: (0, j, 0)),
                   full((chains, S5_N)), full((chains, S5_N))],
        out_shape=[jax.ShapeDtypeStruct((chains, t, S5_DIM), jnp.float32),
                   jax.ShapeDtypeStruct((chains, S5_N), jnp.float32),
                   jax.ShapeDtypeStruct((chains, S5_N), jnp.float32)],
        scratch_shapes=[pltpu.VMEM((S5_LT, rows, LANES), jnp.float32)] * 4
                       + [pltpu.VMEM((chains, S5_N), jnp.float32)] * 2,
        compiler_params=_cparams("arbitrary"),
        name="s5_scan",
    )(u, to_tc, h0r, h0i, ar, ai, bbr, bbi, cr, ci, d, wglu)
    new_state = jnp.stack([hr.reshape(chains, S5_GROUPS, S5_STATE), hi.reshape(chains, S5_GROUPS, S5_STATE)],
                          axis=-1)
    return y, new_state


def _dot_nt(a, b):
    return lax.dot_general(a, b, (((1,), (1,)), ((), ())), preferred_element_type=jnp.float32)


def _split3(x):
    hi = x.astype(jnp.bfloat16)
    rem = x - hi.astype(jnp.float32)
    mid = rem.astype(jnp.bfloat16)
    lo = (rem - mid.astype(jnp.float32)).astype(jnp.bfloat16)
    return hi, mid, lo


def _softmax_rows(s, mask):
    s = jnp.where(mask, s, NEG)
    m = jnp.max(s, axis=-1, keepdims=True)
    p = jnp.exp2(s - m)
    inv = jnp.where(m > 0.5 * NEG, 1.0 / jnp.sum(p, axis=-1, keepdims=True), 0.0)
    return p * inv


def _nsa_prompt_kernel(q_ref, gate_ref, kc_ref, vc_ref, ks_ref, vs_ref, kw_ref, vw_ref, o_ref, *, n_cmp, n_blk):
    f32, bf16 = jnp.float32, jnp.bfloat16
    r4 = HEADS_PER_GROUP
    n_cpad = kc_ref.shape[0]
    start = pl.program_id(1) * Q_BLOCK
    q = q_ref[...] * QK_SCALE
    gate = gate_ref[...]
    lane = lax.broadcasted_iota(jnp.int32, (Q_BLOCK, LANES), 1)
    qpos = start + lax.broadcasted_iota(jnp.int32, (Q_BLOCK, 1), 0)
    n_idx = lax.broadcasted_iota(jnp.int32, (Q_BLOCK, n_cpad), 1)
    cmask = (((n_idx * CMP_STRIDE + (CMP_LEN - 1)) <= qpos) & (n_idx < n_cmp))[None]
    ratio = SEL_BLOCK // CMP_STRIDE
    gsum = (lax.broadcasted_iota(jnp.int32, (n_blk, n_cpad), 1) // ratio
            == lax.broadcasted_iota(jnp.int32, (n_blk, n_cpad), 0)).astype(bf16)
    blk = lax.broadcasted_iota(jnp.int32, (n_blk, Q_BLOCK), 0)
    blk_f = blk.astype(f32)
    jq = (start + lax.broadcasted_iota(jnp.int32, (n_blk, Q_BLOCK), 1)) // SEL_BLOCK
    force = jnp.where((blk == 0) | (blk == jq) | (blk == jq - 1), FORCE, 0.0)
    qgs, o_cs, sels = [], [], []
    for g in range(KV_GROUPS):
        keep = (lane < HEAD_DIM) if g == 0 else (lane >= HEAD_DIM)
        parts = []
        for r in range(r4):
            h = r4 * g + r
            tile = q[:, (h // 2) * LANES:(h // 2 + 1) * LANES]
            if h % 2 != g:
                tile = pltpu.roll(tile, HEAD_DIM, axis=1)
            parts.append(jnp.where(keep, tile, 0.0))
        qg = jnp.concatenate(parts, axis=0).astype(bf16)
        qgs.append(qg)

        p_c = _softmax_rows(_dot_nt(qg, kc_ref[...]).reshape(r4, Q_BLOCK, n_cpad), cmask)
        o_cs.append(jnp.dot(p_c.reshape(r4 * Q_BLOCK, n_cpad).astype(bf16), vc_ref[...],
                            preferred_element_type=f32).reshape(r4, Q_BLOCK, LANES))
        psum = p_c[0] + p_c[1] + p_c[2] + p_c[3]
        imp_t = sum(_dot_nt(gsum, part) for part in _split3(psum))

        score = jnp.where(blk <= jq, imp_t + force, NEG)
        sel_t = jnp.zeros((n_blk, Q_BLOCK), f32)
        for _ in range(min(N_SEL, n_blk)):
            m = jnp.max(score, axis=0, keepdims=True)
            idx = jnp.min(jnp.where(score == m, blk_f, float(n_blk)), axis=0, keepdims=True)
            hit = blk_f == idx
            sel_t = jnp.where(hit & (m > 0.5 * NEG), 1.0, sel_t)
            score = jnp.where(hit, REMOVED, score)
        sels.append(sel_t.T)

    n_full = start // SEL_TILE
    expand0 = (lax.broadcasted_iota(jnp.int32, (n_blk, SEL_TILE), 0)
               == lax.broadcasted_iota(jnp.int32, (n_blk, SEL_TILE), 1) // SEL_BLOCK).astype(bf16)

    def tile_update(i, carry, causal):
        off = pl.multiple_of(i * SEL_TILE, SEL_TILE)
        k = ks_ref[pl.ds(off, SEL_TILE), :]
        v = vs_ref[pl.ds(off, SEL_TILE), :]
        out = []
        for g in range(KV_GROUPS):
            m_run, l_run, acc = carry[g]
            s_t = _dot_nt(qgs[g], k).reshape(r4, Q_BLOCK, SEL_TILE)
            shifted = pltpu.roll(sels[g], (n_blk - i * (SEL_TILE // SEL_BLOCK)) % n_blk, axis=1).astype(bf16)
            mk = jnp.dot(shifted, expand0, preferred_element_type=f32) > 0.5
            if causal:
                kpos = i * SEL_TILE + lax.broadcasted_iota(jnp.int32, (Q_BLOCK, SEL_TILE), 1)
                mk = mk & (kpos <= qpos)
            s_t = jnp.where(mk[None], s_t, NEG)
            m_new = jnp.maximum(m_run, jnp.max(s_t, axis=-1, keepdims=True))
            alpha = jnp.exp2(m_run - m_new)
            p = jnp.exp2(s_t - m_new)
            l_new = alpha * l_run + jnp.sum(p, axis=-1, keepdims=True)
            pv = jnp.dot(p.reshape(r4 * Q_BLOCK, SEL_TILE).astype(bf16), v, preferred_element_type=f32)
            out.append((m_new, l_new, alpha * acc + pv.reshape(r4, Q_BLOCK, LANES)))
        return tuple(out)

    init = (jnp.full((r4, Q_BLOCK, 1), NEG, f32), jnp.zeros((r4, Q_BLOCK, 1), f32),
            jnp.zeros((r4, Q_BLOCK, LANES), f32))
    carry = lax.fori_loop(0, n_full, lambda i, c: tile_update(i, c, False), (init, init))
    fin = tile_update(n_full, carry, True)

    n_win = WINDOW + Q_BLOCK
    woff = pl.multiple_of(start, Q_BLOCK)
    kwin = kw_ref[pl.ds(woff, n_win), :]
    vwin = vw_ref[pl.ds(woff, n_win), :]
    wpos = start - WINDOW + lax.broadcasted_iota(jnp.int32, (Q_BLOCK, n_win), 1)
    wmask = ((wpos <= qpos) & (wpos > qpos - WINDOW) & (wpos >= 0))[None]
    heads = [None] * N_HEADS
    for g in range(KV_GROUPS):
        m_fin, l_fin, acc = fin[g]
        o_s = acc * jnp.where(m_fin > 0.5 * NEG, 1.0 / l_fin, 0.0)
        p_w = _softmax_rows(_dot_nt(qgs[g], kwin).reshape(r4, Q_BLOCK, n_win), wmask)
        o_w = jnp.dot(p_w.reshape(r4 * Q_BLOCK, n_win).astype(bf16), vwin,
                      preferred_element_type=f32).reshape(r4, Q_BLOCK, LANES)
        for r in range(r4):
            h = r4 * g + r
            heads[h] = (gate[:, 3 * h:3 * h + 1] * o_cs[g][r] + gate[:, 3 * h + 1:3 * h + 2] * o_s[r]
                        + gate[:, 3 * h + 2:3 * h + 3] * o_w[r])

    tiles = []
    for j in range(N_HEADS // 2):
        even, odd = heads[2 * j], heads[2 * j + 1]
        if j // 2 == 0:
            tiles.append(jnp.where(lane < HEAD_DIM, even, pltpu.roll(odd, HEAD_DIM, axis=1)))
        else:
            tiles.append(jnp.where(lane < HEAD_DIM, pltpu.roll(even, HEAD_DIM, axis=1), odd))
    o_ref[...] = jnp.concatenate(tiles, axis=1)


def nsa_prompt(q, gates, kc, vc, kvs, kvw_pad):
    b, t, _ = q.shape
    n_cpad = kc.shape[1]
    kern = functools.partial(_nsa_prompt_kernel, n_cmp=t // CMP_STRIDE - 1, n_blk=t // SEL_BLOCK)
    whole = lambda rows, c=0: pl.BlockSpec((None, rows, LANES), lambda i, j: (i, 0, c))
    return pl.pallas_call(
        kern,
        grid=(b, t // Q_BLOCK),
        in_specs=[pl.BlockSpec((None, Q_BLOCK, NSA_Q), lambda i, j: (i, j, 0)),
                  pl.BlockSpec((None, Q_BLOCK, 3 * N_HEADS), lambda i, j: (i, j, 0)),
                  whole(n_cpad), whole(n_cpad), whole(t, 0), whole(t, 1), whole(t + WINDOW, 0),
                  whole(t + WINDOW, 1)],
        out_specs=pl.BlockSpec((None, Q_BLOCK, NSA_Q), lambda i, j: (i, j, 0)),
        out_shape=jax.ShapeDtypeStruct((b, t, NSA_Q), jnp.float32),
        compiler_params=_cparams("parallel", "arbitrary"),
        name="nsa_prompt",
    )(q, gates, kc, vc, kvs, kvs, kvw_pad, kvw_pad)


def _compress_kernel(ch_ref, pet_ref, peb_ref, w1t_ref, w1b_ref, w2_ref, o_ref):
    bf16 = jnp.bfloat16
    ch = ch_ref[...]
    n_ch = ch.shape[0]
    a = jnp.dot((ch + pet_ref[...]).astype(bf16), w1t_ref[...], preferred_element_type=jnp.float32)
    b = jnp.dot((ch + peb_ref[...]).astype(bf16), w1b_ref[...], preferred_element_type=jnp.float32)
    pre = a + pltpu.roll(b, n_ch - 1, axis=0)
    o_ref[...] = jnp.dot(jax.nn.gelu(pre).astype(bf16), w2_ref[...],
                         preferred_element_type=jnp.float32).astype(o_ref.dtype)


def compress_params(w1, w2, pe):
    f32 = jnp.float32
    eye = jnp.eye(KV_GROUPS, dtype=f32)
    w1r = w1.astype(f32).reshape(2, CMP_STRIDE, HEAD_DIM, HEAD_DIM)
    big = jnp.einsum('hjde,gk->hjgdke', w1r, eye).reshape(2, CMP_STRIDE * LANES, LANES).astype(jnp.bfloat16)
    w2bd = jnp.einsum('de,gk->gdke', w2.astype(f32), eye).reshape(LANES, LANES).astype(jnp.bfloat16)
    per = pe.astype(f32).reshape(2, CMP_STRIDE, 1, HEAD_DIM)
    pe_rows = jnp.broadcast_to(per, (2, CMP_STRIDE, KV_GROUPS, HEAD_DIM)).reshape(2, 1, CMP_STRIDE * LANES)
    return pe_rows[0], pe_rows[1], big[0], big[1], w2bd


def compress_prompt(x, params):
    b, t, _ = x.shape
    n_ch = t // CMP_STRIDE
    ch = x.reshape(b, n_ch, CMP_STRIDE * LANES)
    pet, peb, w1t, w1b, w2bd = params
    full = lambda shape: pl.BlockSpec(shape, lambda i: (0,) * len(shape))
    return pl.pallas_call(
        _compress_kernel,
        grid=(b,),
        in_specs=[pl.BlockSpec((None, n_ch, CMP_STRIDE * LANES), lambda i: (i, 0, 0)),
                  full((1, CMP_STRIDE * LANES)), full((1, CMP_STRIDE * LANES)),
                  full((CMP_STRIDE * LANES, LANES)), full((CMP_STRIDE * LANES, LANES)), full((LANES, LANES))],
        out_specs=pl.BlockSpec((None, n_ch, LANES), lambda i: (i, 0, 0)),
        out_shape=jax.ShapeDtypeStruct((b, n_ch, LANES), jnp.bfloat16),
        compiler_params=_cparams("parallel"),
        name="compress_prompt",
    )(ch, pet, peb, w1t, w1b, w2bd)


def _cmp_sample_kernel(pt_ref, *refs, n_pages):
    f32, bf16 = jnp.float32, jnp.bfloat16
    pp = PAGES_PER_STEP
    pages = refs[0:pp]
    (perm_ref, newk_ref, newv_ref, wk_ref, wv_ref, ck_ref, cv_ref, w2k_ref, w2v_ref,
     kc_ref, vc_ref, slab_k, slab_v) = refs[pp:]
    s = pl.program_id(1)
    cpp = PAGE_SIZE // CMP_STRIDE
    base = pl.multiple_of(s * (pp * cpp), pp * cpp)
    for half, slab in enumerate((slab_k, slab_v)):
        for i in range(pp):
            page = pages[i][half * LANES:(half + 1) * LANES, :].astype(bf16)
            rows = _dot_nt(perm_ref[...], page)
            for j in range(CMP_STRIDE):
                slab[j, pl.ds(base + i * cpp, cpp), :] = rows[j * cpp:(j + 1) * cpp, :]

    @pl.when(s == pl.num_programs(1) - 1)
    def _():
        n_ch = n_pages * (PAGE_SIZE // CMP_STRIDE)
        row = lax.broadcasted_iota(jnp.int32, (n_ch, LANES), 0)
        for slab, new_ref, w_ref, c_ref, w2_ref, o_ref in ((slab_k, newk_ref, wk_ref, ck_ref, w2k_ref, kc_ref),
                                                           (slab_v, newv_ref, wv_ref, cv_ref, w2v_ref, vc_ref)):
            ch = jnp.concatenate([slab[j] for j in range(CMP_STRIDE)], axis=1).astype(bf16)
            ab = jnp.dot(ch, w_ref[...], preferred_element_type=f32)
            b_new = jnp.dot(new_ref[...].astype(bf16), w_ref[...], preferred_element_type=f32)[0:1, LANES:]
            nxt = pltpu.roll(ab[:, LANES:], n_ch - 1, axis=0)
            nxt = jnp.where(row == n_ch - 1, b_new, nxt)
            pre = ab[:, :LANES] + nxt + c_ref[...]
            o_ref[...] = jnp.dot(jax.nn.gelu(pre).astype(bf16), w2_ref[...],
                                 preferred_element_type=f32).astype(o_ref.dtype)


def compress_sample_params(w1, w2, pe):
    pet, peb, w1t, w1b, w2bd = compress_params(w1, w2, pe)
    hp = lax.Precision.HIGHEST
    const = (jnp.dot(pet, w1t.astype(jnp.float32), precision=hp)
             + jnp.dot(peb, w1b.astype(jnp.float32), precision=hp))
    return jnp.concatenate([w1t, w1b], axis=1), const, w2bd


def _page_spec(i, pair):
    return pl.BlockSpec((None, 2 * LANES, PAGE_SIZE),
                        lambda b, s, pt: (pt[b, PAGES_PER_STEP * s + i], pair, 0))


def _per_seq(shape):
    return pl.BlockSpec((None,) + shape, lambda b, s, pt: (b, 0, 0))


def compress_sample(pool_t, page_table, new_k, new_v, pk, pv):
    bsz, n_pages = page_table.shape
    pp = PAGES_PER_STEP
    n_ch = n_pages * (PAGE_SIZE // CMP_STRIDE)
    t_new = new_k.shape[1]

    def chunk_rows(x):
        x = jnp.pad(x, ((0, 0), (0, CMP_STRIDE - t_new), (0, 0))).reshape(bsz, 1, CMP_STRIDE * LANES)
        return jnp.pad(x, ((0, 0), (0, 7), (0, 0)))

    full = lambda shape: pl.BlockSpec(shape, lambda b, s, pt: (0,) * len(shape))
    r = jnp.arange(PAGE_SIZE)
    cpp = PAGE_SIZE // CMP_STRIDE
    perm = (r[None, :] == (r[:, None] % cpp) * CMP_STRIDE + r[:, None] // cpp).astype(jnp.bfloat16)
    wk, ck, w2k = pk
    wv, cv, w2v = pv
    grid_spec = pltpu.PrefetchScalarGridSpec(
        num_scalar_prefetch=1,
        grid=(bsz, n_pages // pp),
        in_specs=[_page_spec(i, 0) for i in range(pp)]
                 + [full((PAGE_SIZE, PAGE_SIZE)), _per_seq((8, CMP_STRIDE * LANES)),
                    _per_seq((8, CMP_STRIDE * LANES)),
                    full((CMP_STRIDE * LANES, 2 * LANES)), full((CMP_STRIDE * LANES, 2 * LANES)),
                    full((1, LANES)), full((1, LANES)), full((LANES, LANES)), full((LANES, LANES))],
        out_specs=[_per_seq((n_ch, LANES)), _per_seq((n_ch, LANES))],
        scratch_shapes=[pltpu.VMEM((CMP_STRIDE, n_ch, LANES), jnp.float32)] * 2,
    )
    return pl.pallas_call(
        functools.partial(_cmp_sample_kernel, n_pages=n_pages),
        grid_spec=grid_spec,
        out_shape=[jax.ShapeDtypeStruct((bsz, n_ch, LANES), jnp.bfloat16)] * 2,
        compiler_params=_cparams("parallel", "arbitrary"),
        name="compress_sample",
    )(page_table, *([pool_t] * pp), perm, chunk_rows(new_k), chunk_rows(new_v), wk, wv, ck, cv, w2k, w2v)


def _nsa_sample_kernel(pt_ref, *refs, n_pages, t_new, w_buf):
    f32, bf16 = jnp.float32, jnp.bfloat16
    pp = PAGES_PER_STEP
    q_ref, gate_ref, kc_ref, vc_ref = refs[0:4]
    pages = refs[4:4 + pp]
    (ksn_ref, vsn_ref, win_ref, kwn_ref, vwn_ref, o_ref,
     sel_scr, exp_scr, oc_scr, m_scr, l_scr, acc_scr) = refs[4 + pp:]
    r4, g2 = HEADS_PER_GROUP, KV_GROUPS
    n_rows = g2 * r4 * t_new
    past_len = n_pages * PAGE_SIZE
    n_cmp = kc_ref.shape[0]
    n_bpad = sel_scr.shape[1]
    tile = pp * PAGE_SIZE
    s = pl.program_id(1)
    qall = q_ref[...]
    qpos = past_len + lax.broadcasted_iota(jnp.int32, (n_rows, 1), 0) % t_new

    def grouped(x):
        return x.reshape(g2, 1, t_new, x.shape[-1])

    @pl.when(s == 0)
    def _():
        s_c = _dot_nt(qall, kc_ref[...])
        n_idx = lax.broadcasted_iota(jnp.int32, (n_rows, n_cmp), 1)
        p_c = _softmax_rows(s_c, (n_idx * CMP_STRIDE + (CMP_LEN - 1)) <= qpos)
        oc_scr[...] = jnp.dot(p_c.astype(bf16), vc_ref[...], preferred_element_type=f32)
        psum = jnp.sum(p_c.reshape(g2, r4, t_new, n_cmp), axis=1).reshape(g2 * t_new, n_cmp)
        psum = jnp.concatenate([psum, jnp.zeros((LANES - g2 * t_new, n_cmp), f32)], axis=0)
        p_hi = psum.astype(bf16)
        rem = psum - p_hi.astype(f32)
        p_mid = rem.astype(bf16)
        p_lo = (rem - p_mid.astype(f32)).astype(bf16)
        ratio = SEL_BLOCK // CMP_STRIDE
        gsum = (lax.broadcasted_iota(jnp.int32, (n_bpad, n_cmp), 1) // ratio
                == lax.broadcasted_iota(jnp.int32, (n_bpad, n_cmp), 0)).astype(bf16)
        imp_t = _dot_nt(gsum, p_hi) + _dot_nt(gsum, p_mid) + _dot_nt(gsum, p_lo)
        blk = lax.broadcasted_iota(jnp.int32, (n_bpad, LANES), 0)
        jq = (past_len + lax.broadcasted_iota(jnp.int32, (n_bpad, LANES), 1) % t_new) // SEL_BLOCK
        forced = (blk == 0) | (blk == jq) | (blk == jq - 1)
        score = jnp.where(blk <= jq, imp_t + jnp.where(forced, FORCE, 0.0), NEG)
        blk_f = blk.astype(f32)
        sel_t = jnp.zeros((n_bpad, LANES), f32)
        for _ in range(N_SEL):
            m = jnp.max(score, axis=0, keepdims=True)
            idx = jnp.min(jnp.where(score == m, blk_f, float(n_bpad)), axis=0, keepdims=True)
            hit = blk_f == idx
            sel_t = jnp.where(hit & (m > 0.5 * NEG), 1.0, sel_t)
            score = jnp.where(hit, REMOVED, score)
        sel = jnp.concatenate([sel_t[k * LANES:(k + 1) * LANES].T for k in range(n_bpad // LANES)], axis=1)
        sel_scr[...] = sel[0:g2 * t_new]
        exp_scr[...] = (lax.broadcasted_iota(jnp.int32, (LANES, tile), 0)
                        == lax.broadcasted_iota(jnp.int32, (LANES, tile), 1) // SEL_BLOCK).astype(bf16)
        m_scr[...] = jnp.full(m_scr.shape, NEG, f32)
        l_scr[...] = jnp.zeros(l_scr.shape, f32)
        acc_scr[...] = jnp.zeros(acc_scr.shape, f32)

    def online_update(s_t, mk, v, v_feature_major):
        n = s_t.shape[-1]
        s4 = jnp.where(mk, s_t.reshape(g2, r4, t_new, n), NEG)
        m_run = m_scr[...].reshape(g2, r4, t_new, 1)
        m_new = jnp.maximum(m_run, jnp.max(s4, axis=-1, keepdims=True))
        alpha = jnp.exp2(m_run - m_new)
        p = jnp.exp2(s4 - m_new)
        l_new = alpha * l_scr[...].reshape(g2, r4, t_new, 1) + jnp.sum(p, axis=-1, keepdims=True)
        pb = p.reshape(n_rows, n).astype(bf16)
        pv = _dot_nt(pb, v) if v_feature_major else jnp.dot(pb, v, preferred_element_type=f32)
        m_scr[...] = m_new.reshape(n_rows, 1)
        l_scr[...] = l_new.reshape(n_rows, 1)
        acc_scr[...] = alpha.reshape(n_rows, 1) * acc_scr[...] + pv

    kt = jnp.concatenate([r[0:LANES, :] for r in pages], axis=1).astype(bf16)
    vt = jnp.concatenate([r[LANES:2 * LANES, :] for r in pages], axis=1).astype(bf16)
    shifted = pltpu.roll(sel_scr[...], (n_bpad - s * (tile // SEL_BLOCK)) % n_bpad, axis=1)
    picked = jnp.dot(shifted[:, 0:LANES].astype(bf16), exp_scr[...], preferred_element_type=f32)
    online_update(jnp.dot(qall, kt, preferred_element_type=f32), grouped(picked) > 0.5, vt, True)

    @pl.when(s == pl.num_programs(1) - 1)
    def _():
        new_blk = past_len // SEL_BLOCK
        kidx = lax.broadcasted_iota(jnp.int32, (n_rows, NEW_PAD), 1)
        causal = ((past_len + kidx) <= qpos) & (kidx < t_new)
        picked_new = sel_scr[:, new_blk:new_blk + 1]
        mk = (grouped(picked_new) > 0.5) & causal.reshape(g2, r4, t_new, NEW_PAD)
        online_update(_dot_nt(qall, ksn_ref[...]), mk, vsn_ref[...], False)
        o_s = acc_scr[...] * jnp.where(m_scr[...] > 0.5 * NEG, 1.0 / l_scr[...], 0.0)

        n_win = w_buf + NEW_PAD
        kw_t = win_ref[0:LANES, :].astype(bf16)
        vw_t = win_ref[LANES:2 * LANES, :].astype(bf16)
        widx = lax.broadcasted_iota(jnp.int32, (n_rows, n_win), 1)
        wpos = past_len - w_buf + widx
        wmask = (wpos <= qpos) & (wpos > qpos - WINDOW) & (wpos >= 0) & (widx < w_buf + t_new)
        s_w = jnp.concatenate([jnp.dot(qall, kw_t, preferred_element_type=f32), _dot_nt(qall, kwn_ref[...])],
                              axis=1)
        p_w = _softmax_rows(s_w, wmask).astype(bf16)
        o_w = (_dot_nt(p_w[:, 0:w_buf], vw_t)
               + jnp.dot(p_w[:, w_buf:], vwn_ref[...], preferred_element_type=f32))
        gate = gate_ref[...]
        o_ref[...] = gate[:, 0:1] * oc_scr[...] + gate[:, 1:2] * o_s + gate[:, 2:3] * o_w


def nsa_sample(q, gates, kc, vc, pool_t, page_table, ks_new, vs_new, win, kw_new, vw_new):
    f32, bf16 = jnp.float32, jnp.bfloat16
    bsz, t_new = q.shape[0], q.shape[1]
    n_pages = page_table.shape[1]
    pp = PAGES_PER_STEP
    w_buf = win.shape[2]
    r4, g2 = HEADS_PER_GROUP, KV_GROUPS
    n_rows = g2 * r4 * t_new
    past_len = n_pages * PAGE_SIZE
    assert past_len % SEL_BLOCK == 0 and t_new <= SEL_BLOCK and past_len >= w_buf and n_pages % pp == 0
    n_sel = past_len // SEL_BLOCK + 1
    n_bpad = -(-n_sel // LANES) * LANES
    eye = jnp.eye(g2, dtype=f32)
    qg = q.reshape(bsz, t_new, g2, r4, HEAD_DIM).transpose(0, 2, 3, 1, 4) * QK_SCALE
    qall = jnp.einsum('bgrqd,gk->bgrqkd', qg, eye).reshape(bsz, n_rows, LANES).astype(bf16)
    gall = gates.reshape(bsz, t_new, g2, r4, 3).transpose(0, 2, 3, 1, 4).reshape(bsz, n_rows, 3)
    pad_rows = lambda x: jnp.pad(x, ((0, 0), (0, NEW_PAD - t_new), (0, 0))).astype(bf16)
    n_cmp = kc.shape[1]
    grid_spec = pltpu.PrefetchScalarGridSpec(
        num_scalar_prefetch=1,
        grid=(bsz, n_pages // pp),
        in_specs=[_per_seq((n_rows, LANES)), _per_seq((n_rows, 3)), _per_seq((n_cmp, LANES)),
                  _per_seq((n_cmp, LANES))]
                 + [_page_spec(i, 1) for i in range(pp)]
                 + [_per_seq((NEW_PAD, LANES)), _per_seq((NEW_PAD, LANES)), _per_seq((2 * LANES, w_buf)),
                    _per_seq((NEW_PAD, LANES)), _per_seq((NEW_PAD, LANES))],
        out_specs=_per_seq((n_rows, LANES)),
        scratch_shapes=[pltpu.VMEM((g2 * t_new, n_bpad), f32), pltpu.VMEM((LANES, pp * PAGE_SIZE), bf16),
                        pltpu.VMEM((n_rows, LANES), f32),
                        pltpu.VMEM((n_rows, 1), f32), pltpu.VMEM((n_rows, 1), f32),
                        pltpu.VMEM((n_rows, LANES), f32)],
    )
    o = pl.pallas_call(
        functools.partial(_nsa_sample_kernel, n_pages=n_pages, t_new=t_new, w_buf=w_buf),
        grid_spec=grid_spec,
        out_shape=jax.ShapeDtypeStruct((bsz, n_rows, LANES), f32),
        compiler_params=_cparams("parallel", "arbitrary"),
        name="nsa_sample",
    )(page_table, qall, gall, kc, vc, *([pool_t] * pp), pad_rows(ks_new), pad_rows(vs_new), win,
      pad_rows(kw_new), pad_rows(vw_new))
    o = jnp.einsum('bgrqkd,gk->bqgrd', o.reshape(bsz, g2, r4, t_new, g2, HEAD_DIM), eye)
    return o.reshape(bsz, t_new, NSA_Q)


def _ssd_kernel(x_ref, b_ref, c_ref, dt_ref, a_ref, za_ref, zb_ref, dskip_ref, ng_ref, y_ref, hout_ref, h_scr, *,
                chunk):
    f32, bf16 = jnp.float32, jnp.bfloat16
    n_l = chunk
    hpg = SSD_HEADS // SSD_GROUPS
    gw = hpg * SSD_HEAD_DIM
    j = pl.program_id(1)

    @pl.when(j == 0)
    def _():
        h_scr[...] = jnp.zeros(h_scr.shape, f32)

    x = x_ref[...]
    dt = dt_ref[...]
    tri_b = (lax.broadcasted_iota(jnp.int32, (n_l, n_l), 0) >= lax.broadcasted_iota(jnp.int32, (n_l, n_l), 1))
    tri = tri_b.astype(bf16)
    cum = sum(jnp.dot(tri, part, preferred_element_type=f32) for part in _split3(dt * a_ref[...]))
    cum_t = cum.T
    dt_t = dt.T
    ecum = jnp.exp(cum)
    clast = cum[n_l - 1:n_l, :]
    wt = jnp.exp(clast - cum) * dt
    elast = jnp.exp(clast)
    lane = lax.broadcasted_iota(jnp.int32, (n_l, LANES), 1)
    low = lane < SSD_HEAD_DIM

    def pair(v, h0):
        return jnp.where(low[:v.shape[0]], v[:, h0:h0 + 1], v[:, h0 + 1:h0 + 2])

    tiles = []
    for g in range(SSD_GROUPS):
        bg = b_ref[:, g * SSD_STATE:(g + 1) * SSD_STATE]
        cgb = c_ref[:, g * SSD_STATE:(g + 1) * SSD_STATE].astype(bf16)
        bgt = bg.T.astype(bf16)
        cb = jnp.dot(cgb, bgt, preferred_element_type=f32)
        hg = h_scr[g]
        y_inter = jnp.dot(cgb, hg.astype(bf16), preferred_element_type=f32)
        xw, dec = [], []
        for pr in range(hpg // 2):
            h0 = hpg * g + 2 * pr
            xt = x[:, (h0 // 2) * LANES:(h0 // 2 + 1) * LANES]
            acc = None
            for k in range(2):
                h = h0 + k
                seg = cum[:, h:h + 1] - cum_t[h:h + 1, :]
                w = cb * jnp.exp(jnp.where(tri_b, seg, NEG)) * dt_t[h:h + 1, :]
                xm = jnp.where(low if k == 0 else jnp.logical_not(low), xt, 0.0).astype(bf16)
                part = jnp.dot(w.astype(bf16), xm, preferred_element_type=f32)
                acc = part if acc is None else acc + part
            tiles.append(acc + y_inter[:, pr * LANES:(pr + 1) * LANES] * pair(ecum, h0))
            xw.append((xt * pair(wt, h0)).astype(bf16))
            dec.append(pair(elast, h0))
        h_scr[g] = (hg * jnp.concatenate(dec, axis=1)
                    + jnp.dot(bgt, jnp.concatenate(xw, axis=1), preferred_element_type=f32))
    y = jnp.concatenate(tiles, axis=1) + dskip_ref[...] * x
    zg = jnp.concatenate([za_ref[...], zb_ref[...]], axis=1)
    v = y * (zg * jax.nn.sigmoid(zg))
    outs = []
    for g in range(SSD_GROUPS):
        vg = v[:, g * gw:(g + 1) * gw]
        outs.append(vg * lax.rsqrt(jnp.mean(vg * vg, axis=-1, keepdims=True) + RMS_EPS))
    y_ref[...] = jnp.concatenate(outs, axis=1) * ng_ref[...]

    @pl.when(j == pl.num_programs(1) - 1)
    def _():
        hout_ref[...] = h_scr[...]


def ssd_prompt(xbc, dt, a, z, zg_offset, d_skip, norm_g):
    f32 = jnp.float32
    bsz, t, _ = xbc.shape
    hpg = SSD_HEADS // SSD_GROUPS
    gn = SSD_GROUPS * SSD_STATE
    half = SSD_INNER // 2
    assert zg_offset % half == 0
    dt_p = jnp.pad(dt, ((0, 0), (0, 0), (0, LANES - SSD_HEADS)))
    a_p = jnp.pad(a.astype(f32), (0, LANES - SSD_HEADS)).reshape(1, LANES)
    dsk = jnp.repeat(d_skip.astype(f32), SSD_HEAD_DIM).reshape(1, SSD_INNER)
    blk = lambda w, c: pl.BlockSpec((None, SSD_CHUNK, w), lambda b, j: (b, j, c))
    full = lambda shape: pl.BlockSpec(shape, lambda b, j: (0,) * len(shape))
    state_spec = pl.BlockSpec((None, SSD_GROUPS, SSD_STATE, hpg * SSD_HEAD_DIM), lambda b, j: (b, 0, 0, 0))
    y, h = pl.pallas_call(
        functools.partial(_ssd_kernel, chunk=SSD_CHUNK),
        grid=(bsz, t // SSD_CHUNK),
        in_specs=[blk(SSD_INNER, 0), blk(gn, SSD_INNER // gn), blk(gn, SSD_INNER // gn + 1), blk(LANES, 0),
                  full((1, LANES)), blk(half, zg_offset // half), blk(half, zg_offset // half + 1),
                  full((1, SSD_INNER)), full((1, SSD_INNER))],
        out_specs=[blk(SSD_INNER, 0), state_spec],
        out_shape=[jax.ShapeDtypeStruct((bsz, t, SSD_INNER), f32),
                   jax.ShapeDtypeStruct((bsz, SSD_GROUPS, SSD_STATE, hpg * SSD_HEAD_DIM), f32)],
        scratch_shapes=[pltpu.VMEM((SSD_GROUPS, SSD_STATE, hpg * SSD_HEAD_DIM), f32)],
        compiler_params=_cparams("parallel", "arbitrary"),
        name="ssd_prompt",
    )(xbc, xbc, xbc, dt_p, a_p, z, z, dsk, norm_g.astype(f32).reshape(1, SSD_INNER))
    h = h.reshape(bsz, SSD_GROUPS, SSD_STATE, hpg, SSD_HEAD_DIM).transpose(0, 1, 3, 4, 2)
    return y, h.reshape(bsz, SSD_HEADS, SSD_HEAD_DIM, SSD_STATE)


def _inproj_even_kernel(x_ref, w_ref, c_ref, sa_ref, sb_ref,
                        u_ref, q_ref, rows_ref, kvw_ref, kvsb_ref, kvwb_ref, g_ref):
    bf16 = jnp.bfloat16
    z = jnp.dot(x_ref[...].astype(bf16), w_ref[...], preferred_element_type=jnp.float32)
    cos, s_up, s_down = c_ref[...], sa_ref[...], sb_ref[...]

    def rot(t):
        return (t * cos + pltpu.roll(t, LANES - ROT_DIM // 2, axis=1) * s_up
                + pltpu.roll(t, ROT_DIM // 2, axis=1) * s_down)

    tile = lambda k: z[:, k * LANES:(k + 1) * LANES]
    q0 = S5_DIM // LANES
    kv0 = q0 + NSA_Q // LANES
    u_ref[...] = z[:, 0:S5_DIM]
    q_ref[...] = jnp.concatenate([rot(tile(q0 + k)) for k in range(NSA_Q // LANES)], axis=1)
    kc, vc, ks, vs, kw, vw = (rot(tile(kv0)), tile(kv0 + 1), rot(tile(kv0 + 2)), tile(kv0 + 3),
                              rot(tile(kv0 + 4)), tile(kv0 + 5))
    rows_ref[...] = jnp.concatenate([kc, vc, ks, vs], axis=1)
    kvw = jnp.concatenate([kw, vw], axis=1)
    kvw_ref[...] = kvw
    kvwb_ref[...] = kvw.astype(bf16)
    kvsb_ref[...] = jnp.concatenate([ks, vs], axis=1).astype(bf16)
    g0 = (kv0 + 6) * LANES
    g_ref[...] = jax.nn.sigmoid(z[:, g0:g0 + 3 * N_HEADS])


def rope_tables(pos):
    half = ROT_DIM // 2
    inv = ROPE_THETA ** (-jnp.arange(half, dtype=jnp.float32) * 2.0 / ROT_DIM)
    ang = pos.astype(jnp.float32)[:, None] * inv[None, :]
    d = jnp.arange(LANES) % HEAD_DIM
    cos = jnp.take(jnp.cos(ang), d % half, axis=1)
    sin = jnp.take(jnp.sin(ang), d % half, axis=1)
    return (jnp.where(d < ROT_DIM, cos, 1.0), jnp.where(d < half, -sin, 0.0),
            jnp.where((d >= half) & (d < ROT_DIM), sin, 0.0))


def inproj_even(h, w_bf16, pos):
    f32, bf16 = jnp.float32, jnp.bfloat16
    n, d = h.shape
    tile = min(ROW_TILE, n)
    row = lambda w: pl.BlockSpec((tile, w), lambda i: (i, 0))
    fixed = lambda shape: pl.BlockSpec(shape, lambda i: (0, 0), pipeline_mode=pl.Buffered(1))
    widths = [(S5_DIM, f32), (NSA_Q, f32), (2 * NSA_KV, f32), (NSA_KV, f32), (NSA_KV, bf16), (NSA_KV, bf16),
              (3 * N_HEADS, f32)]
    return pl.pallas_call(
        _inproj_even_kernel,
        grid=(pl.cdiv(n, tile),),
        in_specs=[row(d), fixed(w_bf16.shape), row(LANES), row(LANES), row(LANES)],
        out_specs=[row(w) for w, _ in widths],
        out_shape=[jax.ShapeDtypeStruct((n, w), dt) for w, dt in widths],
        compiler_params=_cparams("parallel"),
        name="inproj_even",
    )(h, w_bf16, *rope_tables(pos))


def _causal_conv_tile(x, tail, w_ref, b_ref, width):
    row = lax.broadcasted_iota(jnp.int32, (TAIL, x.shape[1]), 0)
    acc = b_ref[...] + w_ref[width - 1:width, :] * x
    for k in range(1, width):
        xs = pltpu.roll(x, k, axis=0)
        head = jnp.where(row < k, pltpu.roll(tail, k, axis=0), xs[0:TAIL])
        xs = jnp.concatenate([head, xs[TAIL:]], axis=0)
        acc = acc + w_ref[width - 1 - k:width - k, :] * xs
    return acc


def _inproj_odd_kernel(x_ref, w_ref, scw_ref, scb_ref, cvw_ref, cvb_ref, dtb_ref,
                       ysc_ref, xbc_ref, dt_ref, zg_ref, tsc_ref, tx_ref, tail_sc, tail_x, *, tiles_per_seq):
    @pl.when(pl.program_id(0) % tiles_per_seq == 0)
    def _():
        tail_sc[...] = jnp.zeros(tail_sc.shape, jnp.float32)
        tail_x[...] = jnp.zeros(tail_x.shape, jnp.float32)

    z = jnp.dot(x_ref[...].astype(jnp.bfloat16), w_ref[...], preferred_element_type=jnp.float32)
    o_zg = 3 * SC_DIM
    o_x = o_zg + SSD_INNER
    o_dt = o_x + SSD_CONV_DIM
    n = z.shape[0]
    prod = z[:, 2 * SC_DIM:3 * SC_DIM] * z[:, 0:SC_DIM]
    ysc_ref[...] = z[:, SC_DIM:2 * SC_DIM] * _causal_conv_tile(prod, tail_sc[...], scw_ref, scb_ref, SC_WIDTH)
    xbc = z[:, o_x:o_dt]
    c = _causal_conv_tile(xbc, tail_x[...], cvw_ref, cvb_ref, SSD_CONV)
    xbc_ref[...] = c * jax.nn.sigmoid(c)
    dt_ref[...] = jax.nn.softplus(z[:, o_dt:o_dt + SSD_HEADS] + dtb_ref[...])
    zg_ref[...] = z[:, o_zg:o_x]
    tail_sc[...] = prod[n - TAIL:n]
    tail_x[...] = xbc[n - TAIL:n]
    tsc_ref[...] = prod[n - TAIL:n]
    tx_ref[...] = xbc[n - TAIL:n]


def inproj_odd_prompt(h, w_bf16, bsz, sc_w, sc_b, cv_w, cv_b, dt_bias):
    f32 = jnp.float32
    n_rows, d = h.shape
    t = n_rows // bsz
    assert t % ROW_TILE == 0
    tps = t // ROW_TILE
    row = lambda w: pl.BlockSpec((ROW_TILE, w), lambda i: (i, 0))
    fixed = lambda shape: pl.BlockSpec(shape, lambda i: (0,) * len(shape), pipeline_mode=pl.Buffered(1))
    last = lambda w: pl.BlockSpec((None, TAIL, w), lambda i: (i // tps, 0, 0))
    ysc, xbc, dt, zg, tsc, tx = pl.pallas_call(
        functools.partial(_inproj_odd_kernel, tiles_per_seq=tps),
        grid=(n_rows // ROW_TILE,),
        in_specs=[row(d), fixed(w_bf16.shape), fixed((SC_WIDTH, SC_DIM)), fixed((1, SC_DIM)),
                  fixed((SSD_CONV, SSD_CONV_DIM)), fixed((1, SSD_CONV_DIM)), fixed((1, SSD_HEADS))],
        out_specs=[row(SC_DIM), row(SSD_CONV_DIM), row(SSD_HEADS), row(SSD_INNER), last(SC_DIM), last(SSD_CONV_DIM)],
        out_shape=[jax.ShapeDtypeStruct((n_rows, SC_DIM), f32), jax.ShapeDtypeStruct((n_rows, SSD_CONV_DIM), f32),
                   jax.ShapeDtypeStruct((n_rows, SSD_HEADS), f32), jax.ShapeDtypeStruct((n_rows, SSD_INNER), f32),
                   jax.ShapeDtypeStruct((bsz, TAIL, SC_DIM), f32), jax.ShapeDtypeStruct((bsz, TAIL, SSD_CONV_DIM), f32)],
        scratch_shapes=[pltpu.VMEM((TAIL, SC_DIM), f32), pltpu.VMEM((TAIL, SSD_CONV_DIM), f32)],
        compiler_params=_cparams("arbitrary"),
        name="inproj_odd",
    )(h, w_bf16, sc_w.astype(f32), sc_b.astype(f32).reshape(1, SC_DIM), cv_w.astype(f32),
      cv_b.astype(f32).reshape(1, SSD_CONV_DIM), dt_bias.astype(f32).reshape(1, SSD_HEADS))
    seq = lambda a: a.reshape(bsz, t, a.shape[-1])
    return (seq(ysc), seq(xbc), seq(dt), seq(zg),
            tsc[:, TAIL - (SC_WIDTH - 1):], tx[:, TAIL - (SSD_CONV - 1):])


def layer_norm(x, g, b):
    mu = jnp.mean(x, -1, keepdims=True)
    xc = x - mu
    var = jnp.mean(xc * xc, -1, keepdims=True)
    return xc * lax.rsqrt(var + LN_EPS) * g + b


def last_rows(x, n):
    t = x.shape[1]
    if t < n:
        x = jnp.pad(x, [(0, 0), (n - t, 0)] + [(0, 0)] * (x.ndim - 2))
    return x[:, x.shape[1] - n:]


def causal_conv(x, buf, w, b):
    t = x.shape[1]
    width = w.shape[0]
    xp = jnp.concatenate([buf, x], axis=1)
    y = b + sum(xp[:, j:j + t] * w[j] for j in range(width))
    return y, xp[:, xp.shape[1] - (width - 1):]


def even_prompt_mix(h, w_in_bf16, bt, s5p, cmpp, w_buf):
    t = h.shape[0] // bt
    u, q, rows, kvw, kvs_b, kvw_b, gates = inproj_even(h, w_in_bf16, jnp.arange(h.shape[0]) % t)
    seq = lambda a: a.reshape(bt, t, a.shape[-1])
    feat = KV_GROUPS * HEAD_DIM
    y_s5, s5_state = s5_scan(seq(u), jnp.zeros((bt, S5_GROUPS, S5_STATE, 2), jnp.float32), s5p, S5_CHUNK)
    rows = seq(rows)
    kc = compress_prompt(rows[..., 0:feat], compress_params(cmpp[0], cmpp[1], cmpp[2]))
    vc = compress_prompt(rows[..., feat:2 * feat], compress_params(cmpp[3], cmpp[4], cmpp[5]))
    y_nsa = nsa_prompt(seq(q), seq(gates), kc, vc, seq(kvs_b), jnp.pad(seq(kvw_b), ((0, 0), (WINDOW, 0), (0, 0))))
    new_rows = rows.reshape(bt, t, 4, KV_GROUPS, HEAD_DIM)
    return (y_s5, y_nsa), s5_state, new_rows, last_rows(seq(kvw).reshape(bt, t, 2, KV_GROUPS, HEAD_DIM), w_buf)


def even_sample_mix(h, w_in_bf16, bt, s5_h0, pool, page_table, win_buf, s5p, cmpp):
    f32 = jnp.float32
    t = h.shape[0] // bt
    pos = page_table.shape[1] * PAGE_SIZE + jnp.arange(h.shape[0]) % t
    u, q, rows, kvw, _, _, gates = inproj_even(h, w_in_bf16, pos)
    seq = lambda a: a.reshape(bt, t, a.shape[-1])
    feat = KV_GROUPS * HEAD_DIM
    y_s5, s5_state = s5_scan(seq(u), s5_h0.astype(f32), s5p, t)
    rows, kvw = seq(rows), seq(kvw)
    pool_t = pool.astype(f32).transpose(0, 2, 3, 4, 1).reshape(pool.shape[0], 4 * feat, PAGE_SIZE)
    kc, vc = compress_sample(pool_t, page_table, rows[..., 0:feat], rows[..., feat:2 * feat],
                             compress_sample_params(cmpp[0], cmpp[1], cmpp[2]),
                             compress_sample_params(cmpp[3], cmpp[4], cmpp[5]))
    w_buf = win_buf.shape[1]
    win_f = win_buf.astype(f32)
    y_nsa = nsa_sample(q.reshape(bt, t, N_HEADS, HEAD_DIM), gates.reshape(bt, t, N_HEADS, 3), kc, vc, pool_t,
                       page_table, rows[..., 2 * feat:3 * feat], rows[..., 3 * feat:4 * feat],
                       win_f.transpose(0, 2, 3, 4, 1).reshape(bt, 2 * feat, w_buf), kvw[..., 0:feat],
                       kvw[..., feat:2 * feat])
    new_rows = rows.reshape(bt, t, 4, KV_GROUPS, HEAD_DIM)
    win = jnp.concatenate([win_f, kvw.reshape(bt, t, 2, KV_GROUPS, HEAD_DIM)], axis=1)
    return (y_s5, y_nsa), s5_state, new_rows, win[:, t:]


def ssd_scan(x, dt, a, bm, cm, h0, chunk):
    bt, t, nh, p = x.shape
    nch = t // chunk
    r = nh // SSD_GROUPS
    tri = jnp.arange(chunk)[:, None] >= jnp.arange(chunk)[None, :]

    def to_chunks(v):
        return jnp.moveaxis(v.reshape((bt, nch, chunk) + v.shape[2:]), 1, 0)

    def step(h, inp):
        xc, dtc, bc, cc = inp
        cum = jnp.cumsum(dtc * a, axis=1)
        seg = cum[:, :, None, :] - cum[:, None, :, :]
        decay = jnp.exp(jnp.where(tri[None, :, :, None], seg, NEG)).reshape(bt, chunk, chunk, SSD_GROUPS, r)
        cb = jnp.einsum('btgn,bsgn->btsg', cc, bc)
        xg = xc.reshape(bt, chunk, SSD_GROUPS, r, p)
        dg = dtc.reshape(bt, chunk, SSD_GROUPS, r)
        w = cb[..., None] * decay * dg[:, None]
        y_intra = jnp.einsum('btsgr,bsgrp->btgrp', w, xg)
        hg = h.reshape(bt, SSD_GROUPS, r, p, SSD_STATE)
        y_inter = jnp.einsum('btgn,bgrpn->btgrp', cc, hg) * jnp.exp(cum).reshape(bt, chunk, SSD_GROUPS, r)[..., None]
        wt = (jnp.exp(cum[:, -1:, :] - cum) * dtc).reshape(bt, chunk, SSD_GROUPS, r)
        h_new = (hg * jnp.exp(cum[:, -1]).reshape(bt, SSD_GROUPS, r)[..., None, None]
                 + jnp.einsum('bsgr,bsgrp,bsgn->bgrpn', wt, xg, bc))
        return h_new.reshape(bt, nh, p, SSD_STATE), (y_intra + y_inter).reshape(bt, chunk, nh, p)

    h_fin, ys = lax.scan(step, h0, (to_chunks(x), to_chunks(dt), to_chunks(bm), to_chunks(cm)))
    return jnp.moveaxis(ys, 0, 1).reshape(bt, t, nh, p), h_fin


def gated_rmsnorm(y, z, g):
    v = y * jax.nn.silu(z)
    bt, t, _ = v.shape
    vg = v.reshape(bt, t, SSD_GROUPS, SSD_INNER // SSD_GROUPS)
    vg = vg * lax.rsqrt(jnp.mean(vg * vg, -1, keepdims=True) + RMS_EPS)
    return vg.reshape(bt, t, SSD_INNER) * g


def odd_prompt_mix(h, w_in_bf16, bsz, sc_w, sc_b, cv_w, cv_b, dt_bias, a_log, d_skip, norm_g):
    a = -jnp.exp(a_log.astype(jnp.float32))
    y_sc, xbc_c, dt, zg, new_sc, new_conv = inproj_odd_prompt(h, w_in_bf16, bsz, sc_w, sc_b, cv_w, cv_b, dt_bias)
    y, h_new = ssd_prompt(xbc_c, dt, a, zg, 0, d_skip, norm_g)
    return (y_sc, y), new_sc, new_conv, h_new


def odd_mix(z, sc_buf, conv_buf, h0, chunk, sc_w, sc_b, cv_w, cv_b, dt_bias, a_log, d_skip, norm_g):
    f32 = jnp.float32
    bt, t, _ = z.shape
    a = -jnp.exp(a_log.astype(f32))
    o1 = SC_DIM
    o2 = 2 * SC_DIM
    o3 = 3 * SC_DIM
    o4 = o3 + SSD_INNER
    o5 = o4 + SSD_CONV_DIM
    sc_h = z[..., :o1]
    sc_bg = z[..., o1:o2]
    sc_cg = z[..., o2:o3]
    zg = z[..., o3:o4]
    xbc = z[..., o4:o5]
    dt_raw = z[..., o5:]
    conv_sc, new_sc = causal_conv(sc_cg * sc_h, sc_buf.astype(f32), sc_w, sc_b)
    y_sc = sc_bg * conv_sc
    xbc_c, new_conv = causal_conv(xbc, conv_buf.astype(f32), cv_w, cv_b)
    xbc_c = jax.nn.silu(xbc_c)
    gn = SSD_GROUPS * SSD_STATE
    xs = xbc_c[..., :SSD_INNER].reshape(bt, t, SSD_HEADS, SSD_HEAD_DIM)
    bm = xbc_c[..., SSD_INNER:SSD_INNER + gn].reshape(bt, t, SSD_GROUPS, SSD_STATE)
    cm = xbc_c[..., SSD_INNER + gn:].reshape(bt, t, SSD_GROUPS, SSD_STATE)
    dt = jax.nn.softplus((dt_raw + dt_bias).astype(f32))
    y, h_new = ssd_scan(xs, dt, a, bm, cm, h0.astype(f32), chunk)
    y = (y + d_skip[:, None] * xs).reshape(bt, t, SSD_INNER)
    y = gated_rmsnorm(y, zg, norm_g)
    return (y_sc, y), new_sc, new_conv, h_new


def moe_ffn(x, w_r, b_r, w_gu_bf16, w_down_bf16):
    n, d = x.shape
    logits = jnp.dot(x, w_r, precision=lax.Precision.HIGHEST) + b_r
    top_v, top_i = lax.top_k(logits, TOP_K)
    gate = jax.nn.softmax(top_v, axis=-1)
    flat_e = top_i.reshape(-1)
    blk = 128
    assert (TOP_K * n) % blk == 0
    onehot = jax.nn.one_hot(flat_e, N_EXPERTS, dtype=jnp.float32).reshape(-1, blk, N_EXPERTS)
    tri = (jnp.arange(blk)[:, None] >= jnp.arange(blk)[None, :]).astype(jnp.float32)
    local = jnp.einsum('ij,bjk->bik', tri, onehot)
    block_total = local[:, -1, :]
    block_off = jnp.cumsum(block_total, axis=0) - block_total
    incl = (local + block_off[:, None, :]).reshape(-1, N_EXPERTS)
    rank = jnp.take_along_axis(incl, flat_e[:, None], axis=1)[:, 0].astype(jnp.int32) - 1
    counts = jnp.sum(block_total, axis=0).astype(jnp.int32)
    padded = ((counts + ROW_TILE - 1) // ROW_TILE) * ROW_TILE
    pad_start = jnp.cumsum(padded) - padded
    dest = (pad_start[flat_e] + rank).astype(jnp.int32)
    n_tiles = (TOP_K * n) // ROW_TILE + N_EXPERTS
    rows = n_tiles * ROW_TILE
    row_token = jnp.zeros((rows,), jnp.int32).at[dest].set(jnp.arange(TOP_K * n, dtype=jnp.int32) // TOP_K,
                                                           unique_indices=True, mode='promise_in_bounds')
    tile_end = jnp.cumsum(padded) // ROW_TILE
    tile_expert = jnp.minimum(jnp.searchsorted(tile_end, jnp.arange(n_tiles), side='right'),
                              N_EXPERTS - 1).astype(jnp.int32)
    n_used = tile_end[-1:].astype(jnp.int32)
    xs = x.at[row_token].get(mode='promise_in_bounds')
    ys = grouped_ffn(xs, w_gu_bf16, w_down_bf16, tile_expert, n_used)
    dest = dest.reshape(n, TOP_K)
    y0 = ys.at[dest[:, 0]].get(mode='promise_in_bounds')
    y1 = ys.at[dest[:, 1]].get(mode='promise_in_bounds')
    return gate[:, 0:1] * y0 + gate[:, 1:2] * y1


def kernel(x_prompt, x_sample, state_s5, cache_nsa_kv, state_win_kv, state_sc_conv, state_ssd_conv, state_ssd,
           page_table, ln_g, ln_b, w_in_even, s5_lam_re, s5_lam_im, s5_log_dt, s5_b, s5_c, s5_d, s5_w_glu,
           nsa_wk1, nsa_wk2, nsa_pe_k, nsa_wv1, nsa_wv2, nsa_pe_v, w_out_even, ffn_w_gu, ffn_w_down,
           w_in_odd, sc_conv_w, sc_conv_b, ssd_conv_w, ssd_conv_b, ssd_dt_bias, ssd_a_log, ssd_d, ssd_norm_g,
           w_out_odd, moe_router, moe_router_b, moe_w_gu, moe_w_down):
    f32 = jnp.float32
    bf16 = jnp.bfloat16
    bp, tp, d = x_prompt.shape
    bs, ts, _ = x_sample.shape
    n_p = bp * tp
    n_s = bs * ts
    w_buf = state_win_kv.shape[2]
    streams = [x_prompt.astype(f32).reshape(n_p, d), x_sample.astype(f32).reshape(n_s, d)]

    def flat(parts, n_rows):
        return [p.reshape(n_rows, p.shape[-1]) for p in parts]

    def out_proj(parts, w_out, width, h, g, b):
        w = w_out.astype(bf16)
        return matmul(flat(parts, h.shape[0]), [w[:width], w[width:]], ln=(h, g, b))

    def single_expert(n_rows):
        n_tiles = pl.cdiv(n_rows, min(ROW_TILE, n_rows))
        return jnp.zeros((n_tiles,), jnp.int32), jnp.full((1,), n_tiles, jnp.int32)

    s5p = s5_params(s5_lam_re[0], s5_lam_im[0], s5_log_dt[0], s5_b[0], s5_c[0], s5_d[0], s5_w_glu[0])
    cmpp = (nsa_wk1[0], nsa_wk2[0], nsa_pe_k[0], nsa_wv1[0], nsa_wv2[0], nsa_pe_v[0])
    w_in = w_in_even[0].astype(bf16)
    mix_p, s5_p, kv_p, win_p = even_prompt_mix(streams[0], w_in, bp, s5p, cmpp, w_buf)
    mix_s, s5_s, kv_s, win_s = even_sample_mix(streams[1], w_in, bs, state_s5[0], cache_nsa_kv[0], page_table,
                                               state_win_kv[0], s5p, cmpp)
    streams = [out_proj(mix, w_out_even[0], S5_DIM, h, ln_g[0, 0], ln_b[0, 0])
               for mix, h in zip((mix_p, mix_s), streams)]
    w_gu, w_down = to_bf16(ffn_w_gu), to_bf16(ffn_w_down)
    streams = [grouped_ffn(h, w_gu, w_down, *single_expert(h.shape[0]), ln=(ln_g[0, 1], ln_b[0, 1]))
               for h in streams]

    oddp = (sc_conv_w[0], sc_conv_b[0], ssd_conv_w[0], ssd_conv_b[0], ssd_dt_bias[0],
            ssd_a_log[0], ssd_d[0], ssd_norm_g[0])
    w_in = w_in_odd[0].astype(bf16)
    mix_p, scc_p, sdc_p, ssd_p = odd_prompt_mix(streams[0], w_in, bp, *oddp)
    zs = matmul([streams[1]], [w_in]).reshape(bs, ts, -1)
    mix_s, scc_s, sdc_s, ssd_s = odd_mix(zs, state_sc_conv[0], state_ssd_conv[0], state_ssd[0], ts, *oddp)
    streams = [out_proj(mix, w_out_odd[0], SC_DIM, h, ln_g[1, 0], ln_b[1, 0])
               for mix, h in zip((mix_p, mix_s), streams)]
    h = jnp.concatenate(streams, axis=0)
    f = moe_ffn(h, moe_router[0], moe_router_b[0], to_bf16(moe_w_gu[0]), to_bf16(moe_w_down[0]))
    h = layer_norm(ALPHA * h + f, ln_g[1, 1], ln_b[1, 1])

    hp = h[:n_p].reshape(bp, tp, d)
    hs = h[n_p:].reshape(bs, ts, d)
    st = lambda a, ref: a[None].astype(ref.dtype)
    return (hp.astype(x_prompt.dtype), hs.astype(x_sample.dtype),
            st(s5_p, state_s5), st(s5_s, state_s5),
            st(kv_p, cache_nsa_kv), st(kv_s, cache_nsa_kv),
            st(win_p, state_win_kv), st(win_s, state_win_kv),
            st(scc_p, state_sc_conv), st(scc_s, state_sc_conv),
            st(sdc_p, state_ssd_conv), st(sdc_s, state_ssd_conv),
            st(ssd_p, state_ssd), st(ssd_s, state_ssd))
```

```python
import functools
import math

import jax
import jax.numpy as jnp
from jax import lax
from jax.experimental import pallas as pl
from jax.experimental.pallas import tpu as pltpu

D_MODEL = 1024
DEPTH = 2
ALPHA = (2.0 * DEPTH) ** 0.25
LN_EPS = 1e-5
RMS_EPS = 1e-5
NEG = -1e30

S5_DIM = D_MODEL // 2
S5_GROUP = 16
S5_GROUPS = S5_DIM // S5_GROUP
S5_STATE = 64

HEAD_DIM = 64
N_HEADS = (D_MODEL // 2) // HEAD_DIM
KV_GROUPS = 2
HEADS_PER_GROUP = N_HEADS // KV_GROUPS
CMP_STRIDE = 16
CMP_LEN = 2 * CMP_STRIDE
SEL_BLOCK = 64
N_SEL = 16
WINDOW = 512
Q_BLOCK = 128
ROPE_THETA = 500000.0
ROT_DIM = HEAD_DIM // 4
FORCE = 1e4
NSA_Q = N_HEADS * HEAD_DIM
NSA_KV = 2 * KV_GROUPS * HEAD_DIM

SC_DIM = D_MODEL // 2
SC_WIDTH = 3
SSD_HEAD_DIM = 64
SSD_HEADS = 16
SSD_INNER = SSD_HEADS * SSD_HEAD_DIM
SSD_GROUPS = 4
SSD_STATE = 128
SSD_CONV = 4
SSD_CONV_DIM = SSD_INNER + 2 * SSD_GROUPS * SSD_STATE
SSD_CHUNK = 128

D_FF = 2816
N_EXPERTS = 8
TOP_K = 2

VMEM_LIMIT_BYTES = 56 * 1024 * 1024
LANES = 128
S5_N = S5_GROUPS * S5_STATE
S5_LT = S5_N // LANES
S5_CHUNK = 256
SEL_TILE = 1024
QK_SCALE = HEAD_DIM ** -0.5 * math.log2(math.e)
REMOVED = -3e38
PAGE_SIZE = 128
PAGES_PER_STEP = 32
NEW_PAD = 128
CAST_ROWS = 256
CAST_SPLIT = 4
TAIL = 8
ROW_TILE = 512
FF_TILE = D_FF // 2


def _cparams(*sem):
    return pltpu.CompilerParams(dimension_semantics=sem, vmem_limit_bytes=VMEM_LIMIT_BYTES)


def _deepnorm(resid, update, g, b):
    y = ALPHA * resid + update
    mu = jnp.mean(y, axis=-1, keepdims=True)
    yc = y - mu
    var = jnp.mean(yc * yc, axis=-1, keepdims=True)
    return yc * lax.rsqrt(var + LN_EPS) * g + b


def _mm_kernel(*refs, n_in, fuse_ln):
    xs, ws = refs[0:n_in], refs[n_in:2 * n_in]
    o_ref = refs[-1]
    acc = None
    for x_ref, w_ref in zip(xs, ws):
        part = jnp.dot(x_ref[...].astype(jnp.bfloat16), w_ref[...], preferred_element_type=jnp.float32)
        acc = part if acc is None else acc + part
    if fuse_ln:
        r_ref, g_ref, b_ref = refs[2 * n_in:2 * n_in + 3]
        acc = _deepnorm(r_ref[...], acc, g_ref[...], b_ref[...])
    o_ref[...] = acc


def matmul(xs, ws_bf16, ln=None):
    m = xs[0].shape[0]
    n = ws_bf16[0].shape[1]
    tile = min(ROW_TILE, m)
    row = lambda width: pl.BlockSpec((tile, width), lambda i: (i, 0))
    fixed = lambda shape: pl.BlockSpec(shape, lambda i: (0, 0), pipeline_mode=pl.Buffered(1))
    in_specs = [row(x.shape[1]) for x in xs] + [fixed(w.shape) for w in ws_bf16]
    args = list(xs) + list(ws_bf16)
    if ln is not None:
        resid, g, b = ln
        in_specs += [row(n), fixed((1, n)), fixed((1, n))]
        args += [resid, g.reshape(1, n), b.reshape(1, n)]
    return pl.pallas_call(
        functools.partial(_mm_kernel, n_in=len(xs), fuse_ln=ln is not None),
        grid=(pl.cdiv(m, tile),),
        in_specs=in_specs,
        out_specs=row(n),
        out_shape=jax.ShapeDtypeStruct((m, n), jnp.float32),
        compiler_params=_cparams("parallel"),
        name="matmul",
    )(*args)


def _cast_kernel(*refs):
    o_ref = refs[-1]
    o_ref[...] = jnp.concatenate([r[...].astype(o_ref.dtype) for r in refs[:-1]], axis=1)


def to_bf16(w):
    shape = w.shape
    w2 = w.reshape(-1, shape[-1])
    rows, cols = w2.shape
    split = CAST_SPLIT if cols % (CAST_SPLIT * LANES) == 0 else 1
    out = pl.pallas_call(
        _cast_kernel,
        grid=(pl.cdiv(rows, CAST_ROWS),),
        in_specs=[pl.BlockSpec((CAST_ROWS, cols // split), lambda i, c=c: (i, c)) for c in range(split)],
        out_specs=pl.BlockSpec((CAST_ROWS, cols), lambda i: (i, 0)),
        out_shape=jax.ShapeDtypeStruct((rows, cols), jnp.bfloat16),
        compiler_params=_cparams("parallel"),
        name="to_bf16",
    )(*([w2] * split))
    return out.reshape(shape)


def _ffn_kernel(te_ref, nt_ref, x_ref, wg_ref, wu_ref, wd_ref, *rest, fuse_ln):
    o_ref = rest[-1]
    t = pl.program_id(0)
    j = pl.program_id(1)

    @pl.when(t < nt_ref[0])
    def _():
        x = x_ref[...].astype(jnp.bfloat16)
        g = jnp.dot(x, wg_ref[...], preferred_element_type=jnp.float32)
        u = jnp.dot(x, wu_ref[...], preferred_element_type=jnp.float32)
        h = (g * jax.nn.sigmoid(g) * u).astype(jnp.bfloat16)
        part = jnp.dot(h, wd_ref[...], preferred_element_type=jnp.float32)

        @pl.when(j == 0)
        def _():
            o_ref[...] = part

        @pl.when(j > 0)
        def _():
            if fuse_ln:
                o_ref[...] = _deepnorm(x_ref[...], o_ref[...] + part, rest[0][...], rest[1][...])
            else:
                o_ref[...] += part

    @pl.when(jnp.logical_and(t >= nt_ref[0], j == 0))
    def _():
        o_ref[...] = jnp.zeros_like(o_ref)


def grouped_ffn(x, w_gu_bf16, w_down_bf16, tile_expert, n_tiles_used, ln=None):
    r, d = x.shape
    nf = D_FF // FF_TILE
    assert nf == 2
    tile = min(ROW_TILE, r)
    n_tiles = pl.cdiv(r, tile)
    in_specs = [
        pl.BlockSpec((tile, d), lambda t, j, te, nt: (t, 0)),
        pl.BlockSpec((None, d, FF_TILE), lambda t, j, te, nt: (te[t], 0, j)),
        pl.BlockSpec((None, d, FF_TILE), lambda t, j, te, nt: (te[t], 0, nf + j)),
        pl.BlockSpec((None, FF_TILE, d), lambda t, j, te, nt: (te[t], j, 0)),
    ]
    args = [tile_expert, n_tiles_used, x, w_gu_bf16, w_gu_bf16, w_down_bf16]
    if ln is not None:
        in_specs += [pl.BlockSpec((1, d), lambda t, j, te, nt: (0, 0))] * 2
        args += [ln[0].reshape(1, d), ln[1].reshape(1, d)]
    grid_spec = pltpu.PrefetchScalarGridSpec(
        num_scalar_prefetch=2,
        grid=(n_tiles, nf),
        in_specs=in_specs,
        out_specs=pl.BlockSpec((tile, d), lambda t, j, te, nt: (t, 0)),
    )
    return pl.pallas_call(
        functools.partial(_ffn_kernel, fuse_ln=ln is not None),
        grid_spec=grid_spec,
        out_shape=jax.ShapeDtypeStruct((r, d), jnp.float32),
        compiler_params=_cparams("parallel", "arbitrary"),
        name="grouped_ffn",
    )(*args)


def _s5_kernel(u_ref, perm_ref, h0r_ref, h0i_ref, ar_ref, ai_ref, bbr_ref, bbi_ref, cr_ref, ci_ref, d_ref, wglu_ref,
               y_ref, hro_ref, hio_ref, bur, bui, sr, si, hr, hi, *, chains, chunk):
    j = pl.program_id(0)

    @pl.when(j == 0)
    def _():
        hr[...] = h0r_ref[...]
        hi[...] = h0i_ref[...]

    rows_n = chains * chunk
    u = u_ref[...].reshape(rows_n, S5_DIM)
    to_tc = perm_ref[...]
    ub = jnp.dot(to_tc, u.astype(jnp.bfloat16), preferred_element_type=jnp.float32).astype(jnp.bfloat16)
    hd, hn = S5_DIM // 2, S5_N // 2

    def b_proj(w_ref):
        return jnp.concatenate([jnp.dot(ub[:, h * hd:(h + 1) * hd], w_ref[h * hd:(h + 1) * hd, h * hn:(h + 1) * hn],
                                        preferred_element_type=jnp.float32) for h in range(2)], axis=1)

    bu_r = b_proj(bbr_ref)
    bu_i = b_proj(bbi_ref)
    for k in range(S5_LT):
        bur[k] = bu_r[:, k * LANES:(k + 1) * LANES]
        bui[k] = bu_i[:, k * LANES:(k + 1) * LANES]
    ar = [jnp.broadcast_to(ar_ref[:, k * LANES:(k + 1) * LANES], (chains, LANES)) for k in range(S5_LT)]
    ai = [jnp.broadcast_to(ai_ref[:, k * LANES:(k + 1) * LANES], (chains, LANES)) for k in range(S5_LT)]

    def body(t, carry):
        rows = pl.ds(t * chains, chains)
        out = []
        for k in range(S5_LT):
            xr, xi = carry[2 * k], carry[2 * k + 1]
            nr = ar[k] * xr - ai[k] * xi + bur[k, rows, :]
            ni = ar[k] * xi + ai[k] * xr + bui[k, rows, :]
            sr[k, rows, :] = nr
            si[k, rows, :] = ni
            out += [nr, ni]
        return tuple(out)

    init = []
    for k in range(S5_LT):
        init += [hr[:, k * LANES:(k + 1) * LANES], hi[:, k * LANES:(k + 1) * LANES]]
    fin = lax.fori_loop(0, chunk, body, tuple(init), unroll=2)
    xr = jnp.concatenate(fin[0::2], axis=1)
    xi = jnp.concatenate(fin[1::2], axis=1)
    hr[...] = xr
    hi[...] = xi
    hro_ref[...] = xr
    hio_ref[...] = xi
    s_r = jnp.concatenate([sr[k] for k in range(S5_LT)], axis=1).astype(jnp.bfloat16)
    s_i = jnp.concatenate([si[k] for k in range(S5_LT)], axis=1).astype(jnp.bfloat16)
    y = jnp.concatenate(
        [jnp.dot(s_r[:, h * hn:(h + 1) * hn], cr_ref[h * hn:(h + 1) * hn, h * hd:(h + 1) * hd],
                 preferred_element_type=jnp.float32)
         - jnp.dot(s_i[:, h * hn:(h + 1) * hn], ci_ref[h * hn:(h + 1) * hn, h * hd:(h + 1) * hd],
                   preferred_element_type=jnp.float32) for h in range(2)], axis=1)
    y = sum(lax.dot_general(to_tc, part, (((0,), (0,)), ((), ())), preferred_element_type=jnp.float32)
            for part in _split3(y)) + d_ref[...] * u
    z = jax.nn.gelu(y)
    gate = jax.nn.sigmoid(jnp.dot(z.astype(jnp.bfloat16), wglu_ref[...], preferred_element_type=jnp.float32))
    y_ref[...] = (z * gate).reshape(chains, chunk, S5_DIM)


def s5_params(lam_re, lam_im, log_dt, b, c, d, w_glu):
    f32 = jnp.float32
    dt = jnp.exp(log_dt.astype(f32))[:, None]
    mag = jnp.exp(lam_re * dt)
    ang = lam_im * dt
    ab_re = mag * jnp.cos(ang)
    ab_im = mag * jnp.sin(ang)
    den = lam_re * lam_re + lam_im * lam_im
    nr = ab_re - 1.0
    coef_re = (nr * lam_re + ab_im * lam_im) / den
    coef_im = (ab_im * lam_re - nr * lam_im) / den
    b_re = b[..., 0].astype(f32)
    b_im = b[..., 1].astype(f32)
    bb_re = coef_re[..., None] * b_re - coef_im[..., None] * b_im
    bb_im = coef_re[..., None] * b_im + coef_im[..., None] * b_re
    eye = jnp.eye(S5_GROUPS, dtype=f32)
    bbr = jnp.einsum('gnk,gh->gkhn', bb_re, eye).reshape(S5_DIM, S5_N).astype(jnp.bfloat16)
    bbi = jnp.einsum('gnk,gh->gkhn', bb_im, eye).reshape(S5_DIM, S5_N).astype(jnp.bfloat16)
    cr = jnp.einsum('gkn,gh->gnhk', c[..., 0].astype(f32), eye).reshape(S5_N, S5_DIM).astype(jnp.bfloat16)
    ci = jnp.einsum('gkn,gh->gnhk', c[..., 1].astype(f32), eye).reshape(S5_N, S5_DIM).astype(jnp.bfloat16)
    return (ab_re.reshape(1, S5_N), ab_im.reshape(1, S5_N), bbr, bbi, cr, ci,
            d.astype(f32).reshape(1, S5_DIM), w_glu.astype(jnp.bfloat16))


def s5_scan(u, h0, params, chunk):
    chains, t, _ = u.shape
    ar, ai, bbr, bbi, cr, ci, d, wglu = params
    h0r = h0[..., 0].reshape(chains, S5_N)
    h0i = h0[..., 1].reshape(chains, S5_N)
    full = lambda shape: pl.BlockSpec(shape, lambda j: (0,) * len(shape))
    rows = chains * chunk
    r = jnp.arange(rows)
    to_tc = (r[None, :] == (r[:, None] % chains) * chunk + r[:, None] // chains).astype(jnp.bfloat16)
    y, hr, hi = pl.pallas_call(
        functools.partial(_s5_kernel, chains=chains, chunk=chunk),
        grid=(t // chunk,),
        in_specs=[pl.BlockSpec((chains, chunk, S5_DIM), lambda j: (0, j, 0)), full((rows, rows)),
                  full((chains, S5_N)), full((chains, S5_N)), full((1, S5_N)), full((1, S5_N)),
                  full((S5_DIM, S5_N)), full((S5_DIM, S5_N)), full((S5_N, S5_DIM)), full((S5_N, S5_DIM)),
                  full((1, S5_DIM)), full((S5_DIM, S5_DIM))],
        out_specs=[pl.BlockSpec((chains, chunk, S5_DIM), lambda j: (0, j, 0)),
                   full((chains, S5_N)), full((chains, S5_N))],
        out_shape=[jax.ShapeDtypeStruct((chains, t, S5_DIM), jnp.float32),
                   jax.ShapeDtypeStruct((chains, S5_N), jnp.float32),
                   jax.ShapeDtypeStruct((chains, S5_N), jnp.float32)],
        scratch_shapes=[pltpu.VMEM((S5_LT, rows, LANES), jnp.float32)] * 4
                       + [pltpu.VMEM((chains, S5_N), jnp.float32)] * 2,
        compiler_params=_cparams("arbitrary"),
        name="s5_scan",
    )(u, to_tc, h0r, h0i, ar, ai, bbr, bbi, cr, ci, d, wglu)
    new_state = jnp.stack([hr.reshape(chains, S5_GROUPS, S5_STATE), hi.reshape(chains, S5_GROUPS, S5_STATE)],
                          axis=-1)
    return y, new_state


def _dot_nt(a, b):
    return lax.dot_general(a, b, (((1,), (1,)), ((), ())), preferred_element_type=jnp.float32)


def _split3(x):
    hi = x.astype(jnp.bfloat16)
    rem = x - hi.astype(jnp.float32)
    mid = rem.astype(jnp.bfloat16)
    lo = (rem - mid.astype(jnp.float32)).astype(jnp.bfloat16)
    return hi, mid, lo


def _softmax_rows(s, mask):
    s = jnp.where(mask, s, NEG)
    m = jnp.max(s, axis=-1, keepdims=True)
    p = jnp.exp2(s - m)
    inv = jnp.where(m > 0.5 * NEG, 1.0 / jnp.sum(p, axis=-1, keepdims=True), 0.0)
    return p * inv


def _nsa_prompt_kernel(q_ref, gate_ref, kc_ref, vc_ref, ks_ref, vs_ref, kw_ref, vw_ref, o_ref, *, n_cmp, n_blk):
    f32, bf16 = jnp.float32, jnp.bfloat16
    r4 = HEADS_PER_GROUP
    n_cpad = kc_ref.shape[0]
    start = pl.program_id(1) * Q_BLOCK
    q = q_ref[...] * QK_SCALE
    gate = gate_ref[...]
    lane = lax.broadcasted_iota(jnp.int32, (Q_BLOCK, LANES), 1)
    qpos = start + lax.broadcasted_iota(jnp.int32, (Q_BLOCK, 1), 0)
    n_idx = lax.broadcasted_iota(jnp.int32, (Q_BLOCK, n_cpad), 1)
    cmask = (((n_idx * CMP_STRIDE + (CMP_LEN - 1)) <= qpos) & (n_idx < n_cmp))[None]
    ratio = SEL_BLOCK // CMP_STRIDE
    gsum = (lax.broadcasted_iota(jnp.int32, (n_blk, n_cpad), 1) // ratio
            == lax.broadcasted_iota(jnp.int32, (n_blk, n_cpad), 0)).astype(bf16)
    blk = lax.broadcasted_iota(jnp.int32, (n_blk, Q_BLOCK), 0)
    blk_f = blk.astype(f32)
    jq = (start + lax.broadcasted_iota(jnp.int32, (n_blk, Q_BLOCK), 1)) // SEL_BLOCK
    force = jnp.where((blk == 0) | (blk == jq) | (blk == jq - 1), FORCE, 0.0)
    qgs, o_cs, sels = [], [], []
    for g in range(KV_GROUPS):
        keep = (lane < HEAD_DIM) if g == 0 else (lane >= HEAD_DIM)
        parts = []
        for r in range(r4):
            h = r4 * g + r
            tile = q[:, (h // 2) * LANES:(h // 2 + 1) * LANES]
            if h % 2 != g:
                tile = pltpu.roll(tile, HEAD_DIM, axis=1)
            parts.append(jnp.where(keep, tile, 0.0))
        qg = jnp.concatenate(parts, axis=0).astype(bf16)
        qgs.append(qg)

        p_c = _softmax_rows(_dot_nt(qg, kc_ref[...]).reshape(r4, Q_BLOCK, n_cpad), cmask)
        o_cs.append(jnp.dot(p_c.reshape(r4 * Q_BLOCK, n_cpad).astype(bf16), vc_ref[...],
                            preferred_element_type=f32).reshape(r4, Q_BLOCK, LANES))
        psum = p_c[0] + p_c[1] + p_c[2] + p_c[3]
        imp_t = sum(_dot_nt(gsum, part) for part in _split3(psum))

        score = jnp.where(blk <= jq, imp_t + force, NEG)
        sel_t = jnp.zeros((n_blk, Q_BLOCK), f32)
        for _ in range(min(N_SEL, n_blk)):
            m = jnp.max(score, axis=0, keepdims=True)
            idx = jnp.min(jnp.where(score == m, blk_f, float(n_blk)), axis=0, keepdims=True)
            hit = blk_f == idx
            sel_t = jnp.where(hit & (m > 0.5 * NEG), 1.0, sel_t)
            score = jnp.where(hit, REMOVED, score)
        sels.append(sel_t.T)

    n_full = start // SEL_TILE
    expand0 = (lax.broadcasted_iota(jnp.int32, (n_blk, SEL_TILE), 0)
               == lax.broadcasted_iota(jnp.int32, (n_blk, SEL_TILE), 1) // SEL_BLOCK).astype(bf16)

    def tile_update(i, carry, causal):
        off = pl.multiple_of(i * SEL_TILE, SEL_TILE)
        k = ks_ref[pl.ds(off, SEL_TILE), :]
        v = vs_ref[pl.ds(off, SEL_TILE), :]
        vlane = lax.broadcasted_iota(jnp.int32, (SEL_TILE, LANES), 1)
        v_ones = [jnp.where((vlane < HEAD_DIM) == (g == 0), v, jnp.ones_like(v)) for g in range(KV_GROUPS)]
        out = []
        for g in range(KV_GROUPS):
            m_run, l_run, acc = carry[g]
            s_t = _dot_nt(qgs[g], k).reshape(r4, Q_BLOCK, SEL_TILE)
            shifted = pltpu.roll(sels[g], (n_blk - i * (SEL_TILE // SEL_BLOCK)) % n_blk, axis=1).astype(bf16)
            mk = jnp.dot(shifted, expand0, preferred_element_type=f32) > 0.5
            if causal:
                kpos = i * SEL_TILE + lax.broadcasted_iota(jnp.int32, (Q_BLOCK, SEL_TILE), 1)
                mk = mk & (kpos <= qpos)
            s_t = jnp.where(mk[None], s_t, NEG)
            m_new = jnp.maximum(m_run, jnp.max(s_t, axis=-1, keepdims=True))
            alpha = jnp.exp2(m_run - m_new)
            p = jnp.exp2((s_t - m_new).astype(bf16))
            pv = jnp.dot(p.reshape(r4 * Q_BLOCK, SEL_TILE), v_ones[g], preferred_element_type=f32)
            pv = pv.reshape(r4, Q_BLOCK, LANES)
            l_new = alpha * l_run + pv[:, :, (1 - g) * HEAD_DIM:(1 - g) * HEAD_DIM + 1]
            out.append((m_new, l_new, alpha * acc + pv))
        return tuple(out)

    init = (jnp.full((r4, Q_BLOCK, 1), NEG, f32), jnp.zeros((r4, Q_BLOCK, 1), f32),
            jnp.zeros((r4, Q_BLOCK, LANES), f32))
    carry = lax.fori_loop(0, n_full, lambda i, c: tile_update(i, c, False), (init, init))
    fin = tile_update(n_full, carry, True)

    n_win = WINDOW + Q_BLOCK
    woff = pl.multiple_of(start, Q_BLOCK)
    kwin = kw_ref[pl.ds(woff, n_win), :]
    vwin = vw_ref[pl.ds(woff, n_win), :]
    wpos = start - WINDOW + lax.broadcasted_iota(jnp.int32, (Q_BLOCK, n_win), 1)
    wmask = ((wpos <= qpos) & (wpos > qpos - WINDOW) & (wpos >= 0))[None]
    heads = [None] * N_HEADS
    for g in range(KV_GROUPS):
        m_fin, l_fin, acc = fin[g]
        o_s = acc * jnp.where(m_fin > 0.5 * NEG, 1.0 / l_fin, 0.0)
        p_w = _softmax_rows(_dot_nt(qgs[g], kwin).reshape(r4, Q_BLOCK, n_win), wmask)
        o_w = jnp.dot(p_w.reshape(r4 * Q_BLOCK, n_win).astype(bf16), vwin,
                      preferred_element_type=f32).reshape(r4, Q_BLOCK, LANES)
        for r in range(r4):
            h = r4 * g + r
            heads[h] = (gate[:, 3 * h:3 * h + 1] * o_cs[g][r] + gate[:, 3 * h + 1:3 * h + 2] * o_s[r]
                        + gate[:, 3 * h + 2:3 * h + 3] * o_w[r])

    tiles = []
    for j in range(N_HEADS // 2):
        even, odd = heads[2 * j], heads[2 * j + 1]
        if j // 2 == 0:
            tiles.append(jnp.where(lane < HEAD_DIM, even, pltpu.roll(odd, HEAD_DIM, axis=1)))
        else:
            tiles.append(jnp.where(lane < HEAD_DIM, pltpu.roll(even, HEAD_DIM, axis=1), odd))
    o_ref[...] = jnp.concatenate(tiles, axis=1)


def nsa_prompt(q, gates, kc, vc, kvs, kvw_pad):
    b, t, _ = q.shape
    n_cpad = kc.shape[1]
    kern = functools.partial(_nsa_prompt_kernel, n_cmp=t // CMP_STRIDE - 1, n_blk=t // SEL_BLOCK)
    whole = lambda rows, c=0: pl.BlockSpec((None, rows, LANES), lambda i, j: (i, 0, c))
    return pl.pallas_call(
        kern,
        grid=(b, t // Q_BLOCK),
        in_specs=[pl.BlockSpec((None, Q_BLOCK, NSA_Q), lambda i, j: (i, j, 0)),
                  pl.BlockSpec((None, Q_BLOCK, 3 * N_HEADS), lambda i, j: (i, j, 0)),
                  whole(n_cpad), whole(n_cpad), whole(t, 0), whole(t, 1), whole(t + WINDOW, 0),
                  whole(t + WINDOW, 1)],
        out_specs=pl.BlockSpec((None, Q_BLOCK, NSA_Q), lambda i, j: (i, j, 0)),
        out_shape=jax.ShapeDtypeStruct((b, t, NSA_Q), jnp.float32),
        compiler_params=_cparams("parallel", "arbitrary"),
        name="nsa_prompt",
    )(q, gates, kc, vc, kvs, kvs, kvw_pad, kvw_pad)


def _compress_kernel(ch_ref, pet_ref, peb_ref, w1t_ref, w1b_ref, w2_ref, o_ref):
    bf16 = jnp.bfloat16
    ch = ch_ref[...]
    n_ch = ch.shape[0]
    a = jnp.dot((ch + pet_ref[...]).astype(bf16), w1t_ref[...], preferred_element_type=jnp.float32)
    b = jnp.dot((ch + peb_ref[...]).astype(bf16), w1b_ref[...], preferred_element_type=jnp.float32)
    pre = a + pltpu.roll(b, n_ch - 1, axis=0)
    o_ref[...] = jnp.dot(jax.nn.gelu(pre).astype(bf16), w2_ref[...],
                         preferred_element_type=jnp.float32).astype(o_ref.dtype)


def compress_params(w1, w2, pe):
    f32 = jnp.float32
    eye = jnp.eye(KV_GROUPS, dtype=f32)
    w1r = w1.astype(f32).reshape(2, CMP_STRIDE, HEAD_DIM, HEAD_DIM)
    big = jnp.einsum('hjde,gk->hjgdke', w1r, eye).reshape(2, CMP_STRIDE * LANES, LANES).astype(jnp.bfloat16)
    w2bd = jnp.einsum('de,gk->gdke', w2.astype(f32), eye).reshape(LANES, LANES).astype(jnp.bfloat16)
    per = pe.astype(f32).reshape(2, CMP_STRIDE, 1, HEAD_DIM)
    pe_rows = jnp.broadcast_to(per, (2, CMP_STRIDE, KV_GROUPS, HEAD_DIM)).reshape(2, 1, CMP_STRIDE * LANES)
    return pe_rows[0], pe_rows[1], big[0], big[1], w2bd


def compress_prompt(x, params):
    b, t, _ = x.shape
    n_ch = t // CMP_STRIDE
    ch = x.reshape(b, n_ch, CMP_STRIDE * LANES)
    pet, peb, w1t, w1b, w2bd = params
    full = lambda shape: pl.BlockSpec(shape, lambda i: (0,) * len(shape))
    return pl.pallas_call(
        _compress_kernel,
        grid=(b,),
        in_specs=[pl.BlockSpec((None, n_ch, CMP_STRIDE * LANES), lambda i: (i, 0, 0)),
                  full((1, CMP_STRIDE * LANES)), full((1, CMP_STRIDE * LANES)),
                  full((CMP_STRIDE * LANES, LANES)), full((CMP_STRIDE * LANES, LANES)), full((LANES, LANES))],
        out_specs=pl.BlockSpec((None, n_ch, LANES), lambda i: (i, 0, 0)),
        out_shape=jax.ShapeDtypeStruct((b, n_ch, LANES), jnp.bfloat16),
        compiler_params=_cparams("parallel"),
        name="compress_prompt",
    )(ch, pet, peb, w1t, w1b, w2bd)


def _cmp_sample_kernel(pt_ref, *refs, n_pages):
    f32, bf16 = jnp.float32, jnp.bfloat16
    pp = PAGES_PER_STEP
    pages = refs[0:pp]
    (perm_ref, newk_ref, newv_ref, wk_ref, wv_ref, ck_ref, cv_ref, w2k_ref, w2v_ref,
     kc_ref, vc_ref, slab_k, slab_v) = refs[pp:]
    s = pl.program_id(1)
    cpp = PAGE_SIZE // CMP_STRIDE
    base = pl.multiple_of(s * (pp * cpp), pp * cpp)
    for half, slab in enumerate((slab_k, slab_v)):
        for i in range(pp):
            page = pages[i][half * LANES:(half + 1) * LANES, :].astype(bf16)
            rows = _dot_nt(perm_ref[...], page)
            for j in range(CMP_STRIDE):
                slab[j, pl.ds(base + i * cpp, cpp), :] = rows[j * cpp:(j + 1) * cpp, :]

    @pl.when(s == pl.num_programs(1) - 1)
    def _():
        n_ch = n_pages * (PAGE_SIZE // CMP_STRIDE)
        row = lax.broadcasted_iota(jnp.int32, (n_ch, LANES), 0)
        for slab, new_ref, w_ref, c_ref, w2_ref, o_ref in ((slab_k, newk_ref, wk_ref, ck_ref, w2k_ref, kc_ref),
                                                           (slab_v, newv_ref, wv_ref, cv_ref, w2v_ref, vc_ref)):
            ch = jnp.concatenate([slab[j] for j in range(CMP_STRIDE)], axis=1).astype(bf16)
            ab = jnp.dot(ch, w_ref[...], preferred_element_type=f32)
            b_new = jnp.dot(new_ref[...].astype(bf16), w_ref[...], preferred_element_type=f32)[0:1, LANES:]
            nxt = pltpu.roll(ab[:, LANES:], n_ch - 1, axis=0)
            nxt = jnp.where(row == n_ch - 1, b_new, nxt)
            pre = ab[:, :LANES] + nxt + c_ref[...]
            o_ref[...] = jnp.dot(jax.nn.gelu(pre).astype(bf16), w2_ref[...],
                                 preferred_element_type=f32).astype(o_ref.dtype)


def compress_sample_params(w1, w2, pe):
    pet, peb, w1t, w1b, w2bd = compress_params(w1, w2, pe)
    hp = lax.Precision.HIGHEST
    const = (jnp.dot(pet, w1t.astype(jnp.float32), precision=hp)
             + jnp.dot(peb, w1b.astype(jnp.float32), precision=hp))
    return jnp.concatenate([w1t, w1b], axis=1), const, w2bd


def _page_spec(i, pair):
    return pl.BlockSpec((None, 2 * LANES, PAGE_SIZE),
                        lambda b, s, pt: (pt[b, PAGES_PER_STEP * s + i], pair, 0))


def _per_seq(shape):
    return pl.BlockSpec((None,) + shape, lambda b, s, pt: (b, 0, 0))


def compress_sample(pool_t, page_table, new_k, new_v, pk, pv):
    bsz, n_pages = page_table.shape
    pp = PAGES_PER_STEP
    n_ch = n_pages * (PAGE_SIZE // CMP_STRIDE)
    t_new = new_k.shape[1]

    def chunk_rows(x):
        x = jnp.pad(x, ((0, 0), (0, CMP_STRIDE - t_new), (0, 0))).reshape(bsz, 1, CMP_STRIDE * LANES)
        return jnp.pad(x, ((0, 0), (0, 7), (0, 0)))

    full = lambda shape: pl.BlockSpec(shape, lambda b, s, pt: (0,) * len(shape))
    r = jnp.arange(PAGE_SIZE)
    cpp = PAGE_SIZE // CMP_STRIDE
    perm = (r[None, :] == (r[:, None] % cpp) * CMP_STRIDE + r[:, None] // cpp).astype(jnp.bfloat16)
    wk, ck, w2k = pk
    wv, cv, w2v = pv
    grid_spec = pltpu.PrefetchScalarGridSpec(
        num_scalar_prefetch=1,
        grid=(bsz, n_pages // pp),
        in_specs=[_page_spec(i, 0) for i in range(pp)]
                 + [full((PAGE_SIZE, PAGE_SIZE)), _per_seq((8, CMP_STRIDE * LANES)),
                    _per_seq((8, CMP_STRIDE * LANES)),
                    full((CMP_STRIDE * LANES, 2 * LANES)), full((CMP_STRIDE * LANES, 2 * LANES)),
                    full((1, LANES)), full((1, LANES)), full((LANES, LANES)), full((LANES, LANES))],
        out_specs=[_per_seq((n_ch, LANES)), _per_seq((n_ch, LANES))],
        scratch_shapes=[pltpu.VMEM((CMP_STRIDE, n_ch, LANES), jnp.float32)] * 2,
    )
    return pl.pallas_call(
        functools.partial(_cmp_sample_kernel, n_pages=n_pages),
        grid_spec=grid_spec,
        out_shape=[jax.ShapeDtypeStruct((bsz, n_ch, LANES), jnp.bfloat16)] * 2,
        compiler_params=_cparams("parallel", "arbitrary"),
        name="compress_sample",
    )(page_table, *([pool_t] * pp), perm, chunk_rows(new_k), chunk_rows(new_v), wk, wv, ck, cv, w2k, w2v)


def _nsa_sample_kernel(pt_ref, *refs, n_pages, t_new, w_buf):
    f32, bf16 = jnp.float32, jnp.bfloat16
    pp = PAGES_PER_STEP
    q_ref, gate_ref, kc_ref, vc_ref = refs[0:4]
    pages = refs[4:4 + pp]
    (ksn_ref, vsn_ref, win_ref, kwn_ref, vwn_ref, o_ref,
     sel_scr, exp_scr, oc_scr, m_scr, l_scr, acc_scr) = refs[4 + pp:]
    r4, g2 = HEADS_PER_GROUP, KV_GROUPS
    n_rows = g2 * r4 * t_new
    past_len = n_pages * PAGE_SIZE
    n_cmp = kc_ref.shape[0]
    n_bpad = sel_scr.shape[1]
    tile = pp * PAGE_SIZE
    s = pl.program_id(1)
    qall = q_ref[...]
    qpos = past_len + lax.broadcasted_iota(jnp.int32, (n_rows, 1), 0) % t_new

    def grouped(x):
        return x.reshape(g2, 1, t_new, x.shape[-1])

    @pl.when(s == 0)
    def _():
        s_c = _dot_nt(qall, kc_ref[...])
        n_idx = lax.broadcasted_iota(jnp.int32, (n_rows, n_cmp), 1)
        p_c = _softmax_rows(s_c, (n_idx * CMP_STRIDE + (CMP_LEN - 1)) <= qpos)
        oc_scr[...] = jnp.dot(p_c.astype(bf16), vc_ref[...], preferred_element_type=f32)
        psum = jnp.sum(p_c.reshape(g2, r4, t_new, n_cmp), axis=1).reshape(g2 * t_new, n_cmp)
        psum = jnp.concatenate([psum, jnp.zeros((LANES - g2 * t_new, n_cmp), f32)], axis=0)
        p_hi = psum.astype(bf16)
        rem = psum - p_hi.astype(f32)
        p_mid = rem.astype(bf16)
        p_lo = (rem - p_mid.astype(f32)).astype(bf16)
        ratio = SEL_BLOCK // CMP_STRIDE
        gsum = (lax.broadcasted_iota(jnp.int32, (n_bpad, n_cmp), 1) // ratio
                == lax.broadcasted_iota(jnp.int32, (n_bpad, n_cmp), 0)).astype(bf16)
        imp_t = _dot_nt(gsum, p_hi) + _dot_nt(gsum, p_mid) + _dot_nt(gsum, p_lo)
        blk = lax.broadcasted_iota(jnp.int32, (n_bpad, LANES), 0)
        jq = (past_len + lax.broadcasted_iota(jnp.int32, (n_bpad, LANES), 1) % t_new) // SEL_BLOCK
        forced = (blk == 0) | (blk == jq) | (blk == jq - 1)
        score = jnp.where(blk <= jq, imp_t + jnp.where(forced, FORCE, 0.0), NEG)
        blk_f = blk.astype(f32)
        sel_t = jnp.zeros((n_bpad, LANES), f32)
        for _ in range(N_SEL):
            m = jnp.max(score, axis=0, keepdims=True)
            idx = jnp.min(jnp.where(score == m, blk_f, float(n_bpad)), axis=0, keepdims=True)
            hit = blk_f == idx
            sel_t = jnp.where(hit & (m > 0.5 * NEG), 1.0, sel_t)
            score = jnp.where(hit, REMOVED, score)
        sel = jnp.concatenate([sel_t[k * LANES:(k + 1) * LANES].T for k in range(n_bpad // LANES)], axis=1)
        sel_scr[...] = sel[0:g2 * t_new]
        exp_scr[...] = (lax.broadcasted_iota(jnp.int32, (LANES, tile), 0)
                        == lax.broadcasted_iota(jnp.int32, (LANES, tile), 1) // SEL_BLOCK).astype(bf16)
        m_scr[...] = jnp.full(m_scr.shape, NEG, f32)
        l_scr[...] = jnp.zeros(l_scr.shape, f32)
        acc_scr[...] = jnp.zeros(acc_scr.shape, f32)

    def online_update(s_t, mk, v, v_feature_major):
        n = s_t.shape[-1]
        s4 = jnp.where(mk, s_t.reshape(g2, r4, t_new, n), NEG)
        m_run = m_scr[...].reshape(g2, r4, t_new, 1)
        m_new = jnp.maximum(m_run, jnp.max(s4, axis=-1, keepdims=True))
        alpha = jnp.exp2(m_run - m_new)
        p = jnp.exp2(s4 - m_new)
        l_new = alpha * l_scr[...].reshape(g2, r4, t_new, 1) + jnp.sum(p, axis=-1, keepdims=True)
        pb = p.reshape(n_rows, n).astype(bf16)
        pv = _dot_nt(pb, v) if v_feature_major else jnp.dot(pb, v, preferred_element_type=f32)
        m_scr[...] = m_new.reshape(n_rows, 1)
        l_scr[...] = l_new.reshape(n_rows, 1)
        acc_scr[...] = alpha.reshape(n_rows, 1) * acc_scr[...] + pv

    kt = jnp.concatenate([r[0:LANES, :] for r in pages], axis=1).astype(bf16)
    vt = jnp.concatenate([r[LANES:2 * LANES, :] for r in pages], axis=1).astype(bf16)
    shifted = pltpu.roll(sel_scr[...], (n_bpad - s * (tile // SEL_BLOCK)) % n_bpad, axis=1)
    picked = jnp.dot(shifted[:, 0:LANES].astype(bf16), exp_scr[...], preferred_element_type=f32)
    online_update(jnp.dot(qall, kt, preferred_element_type=f32), grouped(picked) > 0.5, vt, True)

    @pl.when(s == pl.num_programs(1) - 1)
    def _():
        new_blk = past_len // SEL_BLOCK
        kidx = lax.broadcasted_iota(jnp.int32, (n_rows, NEW_PAD), 1)
        causal = ((past_len + kidx) <= qpos) & (kidx < t_new)
        picked_new = sel_scr[:, new_blk:new_blk + 1]
        mk = (grouped(picked_new) > 0.5) & causal.reshape(g2, r4, t_new, NEW_PAD)
        online_update(_dot_nt(qall, ksn_ref[...]), mk, vsn_ref[...], False)
        o_s = acc_scr[...] * jnp.where(m_scr[...] > 0.5 * NEG, 1.0 / l_scr[...], 0.0)

        n_win = w_buf + NEW_PAD
        kw_t = win_ref[0:LANES, :].astype(bf16)
        vw_t = win_ref[LANES:2 * LANES, :].astype(bf16)
        widx = lax.broadcasted_iota(jnp.int32, (n_rows, n_win), 1)
        wpos = past_len - w_buf + widx
        wmask = (wpos <= qpos) & (wpos > qpos - WINDOW) & (wpos >= 0) & (widx < w_buf + t_new)
        s_w = jnp.concatenate([jnp.dot(qall, kw_t, preferred_element_type=f32), _dot_nt(qall, kwn_ref[...])],
                              axis=1)
        p_w = _softmax_rows(s_w, wmask).astype(bf16)
        o_w = (_dot_nt(p_w[:, 0:w_buf], vw_t)
               + jnp.dot(p_w[:, w_buf:], vwn_ref[...], preferred_element_type=f32))
        gate = gate_ref[...]
        o_ref[...] = gate[:, 0:1] * oc_scr[...] + gate[:, 1:2] * o_s + gate[:, 2:3] * o_w


def nsa_sample(q, gates, kc, vc, pool_t, page_table, ks_new, vs_new, win, kw_new, vw_new):
    f32, bf16 = jnp.float32, jnp.bfloat16
    bsz, t_new = q.shape[0], q.shape[1]
    n_pages = page_table.shape[1]
    pp = PAGES_PER_STEP
    w_buf = win.shape[2]
    r4, g2 = HEADS_PER_GROUP, KV_GROUPS
    n_rows = g2 * r4 * t_new
    past_len = n_pages * PAGE_SIZE
    assert past_len % SEL_BLOCK == 0 and t_new <= SEL_BLOCK and past_len >= w_buf and n_pages % pp == 0
    n_sel = past_len // SEL_BLOCK + 1
    n_bpad = -(-n_sel // LANES) * LANES
    eye = jnp.eye(g2, dtype=f32)
    qg = q.reshape(bsz, t_new, g2, r4, HEAD_DIM).transpose(0, 2, 3, 1, 4) * QK_SCALE
    qall = jnp.einsum('bgrqd,gk->bgrqkd', qg, eye).reshape(bsz, n_rows, LANES).astype(bf16)
    gall = gates.reshape(bsz, t_new, g2, r4, 3).transpose(0, 2, 3, 1, 4).reshape(bsz, n_rows, 3)
    pad_rows = lambda x: jnp.pad(x, ((0, 0), (0, NEW_PAD - t_new), (0, 0))).astype(bf16)
    n_cmp = kc.shape[1]
    grid_spec = pltpu.PrefetchScalarGridSpec(
        num_scalar_prefetch=1,
        grid=(bsz, n_pages // pp),
        in_specs=[_per_seq((n_rows, LANES)), _per_seq((n_rows, 3)), _per_seq((n_cmp, LANES)),
                  _per_seq((n_cmp, LANES))]
                 + [_page_spec(i, 1) for i in range(pp)]
                 + [_per_seq((NEW_PAD, LANES)), _per_seq((NEW_PAD, LANES)), _per_seq((2 * LANES, w_buf)),
                    _per_seq((NEW_PAD, LANES)), _per_seq((NEW_PAD, LANES))],
        out_specs=_per_seq((n_rows, LANES)),
        scratch_shapes=[pltpu.VMEM((g2 * t_new, n_bpad), f32), pltpu.VMEM((LANES, pp * PAGE_SIZE), bf16),
                        pltpu.VMEM((n_rows, LANES), f32),
                        pltpu.VMEM((n_rows, 1), f32), pltpu.VMEM((n_rows, 1), f32),
                        pltpu.VMEM((n_rows, LANES), f32)],
    )
    o = pl.pallas_call(
        functools.partial(_nsa_sample_kernel, n_pages=n_pages, t_new=t_new, w_buf=w_buf),
        grid_spec=grid_spec,
        out_shape=jax.ShapeDtypeStruct((bsz, n_rows, LANES), f32),
        compiler_params=_cparams("parallel", "arbitrary"),
        name="nsa_sample",
    )(page_table, qall, gall, kc, vc, *([pool_t] * pp), pad_rows(ks_new), pad_rows(vs_new), win,
      pad_rows(kw_new), pad_rows(vw_new))
    o = jnp.einsum('bgrqkd,gk->bqgrd', o.reshape(bsz, g2, r4, t_new, g2, HEAD_DIM), eye)
    return o.reshape(bsz, t_new, NSA_Q)


def _ssd_kernel(x_ref, b_ref, c_ref, dt_ref, a_ref, za_ref, zb_ref, dskip_ref, ng_ref, y_ref, hout_ref, h_scr, *,
                chunk):
    f32, bf16 = jnp.float32, jnp.bfloat16
    n_l = chunk
    hpg = SSD_HEADS // SSD_GROUPS
    gw = hpg * SSD_HEAD_DIM
    j = pl.program_id(1)

    @pl.when(j == 0)
    def _():
        h_scr[...] = jnp.zeros(h_scr.shape, f32)

    x = x_ref[...]
    dt = dt_ref[...]
    tri_b = (lax.broadcasted_iota(jnp.int32, (n_l, n_l), 0) >= lax.broadcasted_iota(jnp.int32, (n_l, n_l), 1))
    tri = tri_b.astype(bf16)
    cum = sum(jnp.dot(tri, part, preferred_element_type=f32) for part in _split3(dt * a_ref[...]))
    cum_t = cum.T
    dt_t = dt.T
    ecum = jnp.exp(cum)
    clast = cum[n_l - 1:n_l, :]
    wt = jnp.exp(clast - cum) * dt
    elast = jnp.exp(clast)
    lane = lax.broadcasted_iota(jnp.int32, (n_l, LANES), 1)
    low = lane < SSD_HEAD_DIM

    def pair(v, h0):
        return jnp.where(low[:v.shape[0]], v[:, h0:h0 + 1], v[:, h0 + 1:h0 + 2])

    tiles = []
    for g in range(SSD_GROUPS):
        bg = b_ref[:, g * SSD_STATE:(g + 1) * SSD_STATE]
        cgb = c_ref[:, g * SSD_STATE:(g + 1) * SSD_STATE].astype(bf16)
        bgt = bg.T.astype(bf16)
        cb = jnp.dot(cgb, bgt, preferred_element_type=f32)
        hg = h_scr[g]
        y_inter = jnp.dot(cgb, hg.astype(bf16), preferred_element_type=f32)
        xw, dec = [], []
        for pr in range(hpg // 2):
            h0 = hpg * g + 2 * pr
            xt = x[:, (h0 // 2) * LANES:(h0 // 2 + 1) * LANES]
            acc = None
            for k in range(2):
                h = h0 + k
                seg = cum[:, h:h + 1] - cum_t[h:h + 1, :]
                w = cb * jnp.exp(jnp.where(tri_b, seg, NEG)) * dt_t[h:h + 1, :]
                xm = jnp.where(low if k == 0 else jnp.logical_not(low), xt, 0.0).astype(bf16)
                part = jnp.dot(w.astype(bf16), xm, preferred_element_type=f32)
                acc = part if acc is None else acc + part
            tiles.append(acc + y_inter[:, pr * LANES:(pr + 1) * LANES] * pair(ecum, h0))
            xw.append((xt * pair(wt, h0)).astype(bf16))
            dec.append(pair(elast, h0))
        h_scr[g] = (hg * jnp.concatenate(dec, axis=1)
                    + jnp.dot(bgt, jnp.concatenate(xw, axis=1), preferred_element_type=f32))
    y = jnp.concatenate(tiles, axis=1) + dskip_ref[...] * x
    zg = jnp.concatenate([za_ref[...], zb_ref[...]], axis=1)
    v = y * (zg * jax.nn.sigmoid(zg))
    outs = []
    for g in range(SSD_GROUPS):
        vg = v[:, g * gw:(g + 1) * gw]
        outs.append(vg * lax.rsqrt(jnp.mean(vg * vg, axis=-1, keepdims=True) + RMS_EPS))
    y_ref[...] = jnp.concatenate(outs, axis=1) * ng_ref[...]

    @pl.when(j == pl.num_programs(1) - 1)
    def _():
        hout_ref[...] = h_scr[...]


def ssd_prompt(xbc, dt, a, z, zg_offset, d_skip, norm_g):
    f32 = jnp.float32
    bsz, t, _ = xbc.shape
    hpg = SSD_HEADS // SSD_GROUPS
    gn = SSD_GROUPS * SSD_STATE
    half = SSD_INNER // 2
    assert zg_offset % half == 0
    dt_p = jnp.pad(dt, ((0, 0), (0, 0), (0, LANES - SSD_HEADS)))
    a_p = jnp.pad(a.astype(f32), (0, LANES - SSD_HEADS)).reshape(1, LANES)
    dsk = jnp.repeat(d_skip.astype(f32), SSD_HEAD_DIM).reshape(1, SSD_INNER)
    blk = lambda w, c: pl.BlockSpec((None, SSD_CHUNK, w), lambda b, j: (b, j, c))
    full = lambda shape: pl.BlockSpec(shape, lambda b, j: (0,) * len(shape))
    state_spec = pl.BlockSpec((None, SSD_GROUPS, SSD_STATE, hpg * SSD_HEAD_DIM), lambda b, j: (b, 0, 0, 0))
    y, h = pl.pallas_call(
        functools.partial(_ssd_kernel, chunk=SSD_CHUNK),
        grid=(bsz, t // SSD_CHUNK),
        in_specs=[blk(SSD_INNER, 0), blk(gn, SSD_INNER // gn), blk(gn, SSD_INNER // gn + 1), blk(LANES, 0),
                  full((1, LANES)), blk(half, zg_offset // half), blk(half, zg_offset // half + 1),
                  full((1, SSD_INNER)), full((1, SSD_INNER))],
        out_specs=[blk(SSD_INNER, 0), state_spec],
        out_shape=[jax.ShapeDtypeStruct((bsz, t, SSD_INNER), f32),
                   jax.ShapeDtypeStruct((bsz, SSD_GROUPS, SSD_STATE, hpg * SSD_HEAD_DIM), f32)],
        scratch_shapes=[pltpu.VMEM((SSD_GROUPS, SSD_STATE, hpg * SSD_HEAD_DIM), f32)],
        compiler_params=_cparams("parallel", "arbitrary"),
        name="ssd_prompt",
    )(xbc, xbc, xbc, dt_p, a_p, z, z, dsk, norm_g.astype(f32).reshape(1, SSD_INNER))
    h = h.reshape(bsz, SSD_GROUPS, SSD_STATE, hpg, SSD_HEAD_DIM).transpose(0, 1, 3, 4, 2)
    return y, h.reshape(bsz, SSD_HEADS, SSD_HEAD_DIM, SSD_STATE)


def _inproj_even_kernel(x_ref, w_ref, c_ref, sa_ref, sb_ref,
                        u_ref, q_ref, rows_ref, kvw_ref, kvsb_ref, kvwb_ref, g_ref):
    bf16 = jnp.bfloat16
    z = jnp.dot(x_ref[...].astype(bf16), w_ref[...], preferred_element_type=jnp.float32)
    cos, s_up, s_down = c_ref[...], sa_ref[...], sb_ref[...]

    def rot(t):
        return (t * cos + pltpu.roll(t, LANES - ROT_DIM // 2, axis=1) * s_up
                + pltpu.roll(t, ROT_DIM // 2, axis=1) * s_down)

    tile = lambda k: z[:, k * LANES:(k + 1) * LANES]
    q0 = S5_DIM // LANES
    kv0 = q0 + NSA_Q // LANES
    u_ref[...] = z[:, 0:S5_DIM]
    q_ref[...] = jnp.concatenate([rot(tile(q0 + k)) for k in range(NSA_Q // LANES)], axis=1)
    kc, vc, ks, vs, kw, vw = (rot(tile(kv0)), tile(kv0 + 1), rot(tile(kv0 + 2)), tile(kv0 + 3),
                              rot(tile(kv0 + 4)), tile(kv0 + 5))
    rows_ref[...] = jnp.concatenate([kc, vc, ks, vs], axis=1)
    kvw = jnp.concatenate([kw, vw], axis=1)
    kvw_ref[...] = kvw
    kvwb_ref[...] = kvw.astype(bf16)
    kvsb_ref[...] = jnp.concatenate([ks, vs], axis=1).astype(bf16)
    g0 = (kv0 + 6) * LANES
    g_ref[...] = jax.nn.sigmoid(z[:, g0:g0 + 3 * N_HEADS])


def rope_tables(pos):
    half = ROT_DIM // 2
    inv = ROPE_THETA ** (-jnp.arange(half, dtype=jnp.float32) * 2.0 / ROT_DIM)
    ang = pos.astype(jnp.float32)[:, None] * inv[None, :]
    d = jnp.arange(LANES) % HEAD_DIM
    cos = jnp.take(jnp.cos(ang), d % half, axis=1)
    sin = jnp.take(jnp.sin(ang), d % half, axis=1)
    return (jnp.where(d < ROT_DIM, cos, 1.0), jnp.where(d < half, -sin, 0.0),
            jnp.where((d >= half) & (d < ROT_DIM), sin, 0.0))


def inproj_even(h, w_bf16, pos):
    f32, bf16 = jnp.float32, jnp.bfloat16
    n, d = h.shape
    tile = min(ROW_TILE, n)
    row = lambda w: pl.BlockSpec((tile, w), lambda i: (i, 0))
    fixed = lambda shape: pl.BlockSpec(shape, lambda i: (0, 0), pipeline_mode=pl.Buffered(1))
    widths = [(S5_DIM, f32), (NSA_Q, f32), (2 * NSA_KV, f32), (NSA_KV, f32), (NSA_KV, bf16), (NSA_KV, bf16),
              (3 * N_HEADS, f32)]
    return pl.pallas_call(
        _inproj_even_kernel,
        grid=(pl.cdiv(n, tile),),
        in_specs=[row(d), fixed(w_bf16.shape), row(LANES), row(LANES), row(LANES)],
        out_specs=[row(w) for w, _ in widths],
        out_shape=[jax.ShapeDtypeStruct((n, w), dt) for w, dt in widths],
        compiler_params=_cparams("parallel"),
        name="inproj_even",
    )(h, w_bf16, *rope_tables(pos))


def _causal_conv_tile(x, tail, w_ref, b_ref, width):
    row = lax.broadcasted_iota(jnp.int32, (TAIL, x.shape[1]), 0)
    acc = b_ref[...] + w_ref[width - 1:width, :] * x
    for k in range(1, width):
        xs = pltpu.roll(x, k, axis=0)
        head = jnp.where(row < k, pltpu.roll(tail, k, axis=0), xs[0:TAIL])
        xs = jnp.concatenate([head, xs[TAIL:]], axis=0)
        acc = acc + w_ref[width - 1 - k:width - k, :] * xs
    return acc


def _inproj_odd_kernel(x_ref, w_ref, scw_ref, scb_ref, cvw_ref, cvb_ref, dtb_ref,
                       ysc_ref, xbc_ref, dt_ref, zg_ref, tsc_ref, tx_ref, tail_sc, tail_x, *, tiles_per_seq):
    @pl.when(pl.program_id(0) % tiles_per_seq == 0)
    def _():
        tail_sc[...] = jnp.zeros(tail_sc.shape, jnp.float32)
        tail_x[...] = jnp.zeros(tail_x.shape, jnp.float32)

    z = jnp.dot(x_ref[...].astype(jnp.bfloat16), w_ref[...], preferred_element_type=jnp.float32)
    o_zg = 3 * SC_DIM
    o_x = o_zg + SSD_INNER
    o_dt = o_x + SSD_CONV_DIM
    n = z.shape[0]
    prod = z[:, 2 * SC_DIM:3 * SC_DIM] * z[:, 0:SC_DIM]
    ysc_ref[...] = z[:, SC_DIM:2 * SC_DIM] * _causal_conv_tile(prod, tail_sc[...], scw_ref, scb_ref, SC_WIDTH)
    xbc = z[:, o_x:o_dt]
    c = _causal_conv_tile(xbc, tail_x[...], cvw_ref, cvb_ref, SSD_CONV)
    xbc_ref[...] = c * jax.nn.sigmoid(c)
    dt_ref[...] = jax.nn.softplus(z[:, o_dt:o_dt + SSD_HEADS] + dtb_ref[...])
    zg_ref[...] = z[:, o_zg:o_x]
    tail_sc[...] = prod[n - TAIL:n]
    tail_x[...] = xbc[n - TAIL:n]
    tsc_ref[...] = prod[n - TAIL:n]
    tx_ref[...] = xbc[n - TAIL:n]


def inproj_odd_prompt(h, w_bf16, bsz, sc_w, sc_b, cv_w, cv_b, dt_bias):
    f32 = jnp.float32
    n_rows, d = h.shape
    t = n_rows // bsz
    assert t % ROW_TILE == 0
    tps = t // ROW_TILE
    row = lambda w: pl.BlockSpec((ROW_TILE, w), lambda i: (i, 0))
    fixed = lambda shape: pl.BlockSpec(shape, lambda i: (0,) * len(shape), pipeline_mode=pl.Buffered(1))
    last = lambda w: pl.BlockSpec((None, TAIL, w), lambda i: (i // tps, 0, 0))
    ysc, xbc, dt, zg, tsc, tx = pl.pallas_call(
        functools.partial(_inproj_odd_kernel, tiles_per_seq=tps),
        grid=(n_rows // ROW_TILE,),
        in_specs=[row(d), fixed(w_bf16.shape), fixed((SC_WIDTH, SC_DIM)), fixed((1, SC_DIM)),
                  fixed((SSD_CONV, SSD_CONV_DIM)), fixed((1, SSD_CONV_DIM)), fixed((1, SSD_HEADS))],
        out_specs=[row(SC_DIM), row(SSD_CONV_DIM), row(SSD_HEADS), row(SSD_INNER), last(SC_DIM), last(SSD_CONV_DIM)],
        out_shape=[jax.ShapeDtypeStruct((n_rows, SC_DIM), f32), jax.ShapeDtypeStruct((n_rows, SSD_CONV_DIM), f32),
                   jax.ShapeDtypeStruct((n_rows, SSD_HEADS), f32), jax.ShapeDtypeStruct((n_rows, SSD_INNER), f32),
                   jax.ShapeDtypeStruct((bsz, TAIL, SC_DIM), f32), jax.ShapeDtypeStruct((bsz, TAIL, SSD_CONV_DIM), f32)],
        scratch_shapes=[pltpu.VMEM((TAIL, SC_DIM), f32), pltpu.VMEM((TAIL, SSD_CONV_DIM), f32)],
        compiler_params=_cparams("arbitrary"),
        name="inproj_odd",
    )(h, w_bf16, sc_w.astype(f32), sc_b.astype(f32).reshape(1, SC_DIM), cv_w.astype(f32),
      cv_b.astype(f32).reshape(1, SSD_CONV_DIM), dt_bias.astype(f32).reshape(1, SSD_HEADS))
    seq = lambda a: a.reshape(bsz, t, a.shape[-1])
    return (seq(ysc), seq(xbc), seq(dt), seq(zg),
            tsc[:, TAIL - (SC_WIDTH - 1):], tx[:, TAIL - (SSD_CONV - 1):])


def layer_norm(x, g, b):
    mu = jnp.mean(x, -1, keepdims=True)
    xc = x - mu
    var = jnp.mean(xc * xc, -1, keepdims=True)
    return xc * lax.rsqrt(var + LN_EPS) * g + b


def last_rows(x, n):
    t = x.shape[1]
    if t < n:
        x = jnp.pad(x, [(0, 0), (n - t, 0)] + [(0, 0)] * (x.ndim - 2))
    return x[:, x.shape[1] - n:]


def causal_conv(x, buf, w, b):
    t = x.shape[1]
    width = w.shape[0]
    xp = jnp.concatenate([buf, x], axis=1)
    y = b + sum(xp[:, j:j + t] * w[j] for j in range(width))
    return y, xp[:, xp.shape[1] - (width - 1):]


def even_prompt_mix(h, w_in_bf16, bt, s5p, cmpp, w_buf):
    t = h.shape[0] // bt
    u, q, rows, kvw, kvs_b, kvw_b, gates = inproj_even(h, w_in_bf16, jnp.arange(h.shape[0]) % t)
    seq = lambda a: a.reshape(bt, t, a.shape[-1])
    feat = KV_GROUPS * HEAD_DIM
    y_s5, s5_state = s5_scan(seq(u), jnp.zeros((bt, S5_GROUPS, S5_STATE, 2), jnp.float32), s5p, S5_CHUNK)
    rows = seq(rows)
    kc = compress_prompt(rows[..., 0:feat], compress_params(cmpp[0], cmpp[1], cmpp[2]))
    vc = compress_prompt(rows[..., feat:2 * feat], compress_params(cmpp[3], cmpp[4], cmpp[5]))
    y_nsa = nsa_prompt(seq(q), seq(gates), kc, vc, seq(kvs_b), jnp.pad(seq(kvw_b), ((0, 0), (WINDOW, 0), (0, 0))))
    new_rows = rows.reshape(bt, t, 4, KV_GROUPS, HEAD_DIM)
    return (y_s5, y_nsa), s5_state, new_rows, last_rows(seq(kvw).reshape(bt, t, 2, KV_GROUPS, HEAD_DIM), w_buf)


def even_sample_mix(h, w_in_bf16, bt, s5_h0, pool, page_table, win_buf, s5p, cmpp):
    f32 = jnp.float32
    t = h.shape[0] // bt
    pos = page_table.shape[1] * PAGE_SIZE + jnp.arange(h.shape[0]) % t
    u, q, rows, kvw, _, _, gates = inproj_even(h, w_in_bf16, pos)
    seq = lambda a: a.reshape(bt, t, a.shape[-1])
    feat = KV_GROUPS * HEAD_DIM
    y_s5, s5_state = s5_scan(seq(u), s5_h0.astype(f32), s5p, t)
    rows, kvw = seq(rows), seq(kvw)
    pool_t = pool.astype(f32).transpose(0, 2, 3, 4, 1).reshape(pool.shape[0], 4 * feat, PAGE_SIZE)
    kc, vc = compress_sample(pool_t, page_table, rows[..., 0:feat], rows[..., feat:2 * feat],
                             compress_sample_params(cmpp[0], cmpp[1], cmpp[2]),
                             compress_sample_params(cmpp[3], cmpp[4], cmpp[5]))
    w_buf = win_buf.shape[1]
    win_f = win_buf.astype(f32)
    y_nsa = nsa_sample(q.reshape(bt, t, N_HEADS, HEAD_DIM), gates.reshape(bt, t, N_HEADS, 3), kc, vc, pool_t,
                       page_table, rows[..., 2 * feat:3 * feat], rows[..., 3 * feat:4 * feat],
                       win_f.transpose(0, 2, 3, 4, 1).reshape(bt, 2 * feat, w_buf), kvw[..., 0:feat],
                       kvw[..., feat:2 * feat])
    new_rows = rows.reshape(bt, t, 4, KV_GROUPS, HEAD_DIM)
    win = jnp.concatenate([win_f, kvw.reshape(bt, t, 2, KV_GROUPS, HEAD_DIM)], axis=1)
    return (y_s5, y_nsa), s5_state, new_rows, win[:, t:]


def ssd_scan(x, dt, a, bm, cm, h0, chunk):
    bt, t, nh, p = x.shape
    nch = t // chunk
    r = nh // SSD_GROUPS
    tri = jnp.arange(chunk)[:, None] >= jnp.arange(chunk)[None, :]

    def to_chunks(v):
        return jnp.moveaxis(v.reshape((bt, nch, chunk) + v.shape[2:]), 1, 0)

    def step(h, inp):
        xc, dtc, bc, cc = inp
        cum = jnp.cumsum(dtc * a, axis=1)
        seg = cum[:, :, None, :] - cum[:, None, :, :]
        decay = jnp.exp(jnp.where(tri[None, :, :, None], seg, NEG)).reshape(bt, chunk, chunk, SSD_GROUPS, r)
        cb = jnp.einsum('btgn,bsgn->btsg', cc, bc)
        xg = xc.reshape(bt, chunk, SSD_GROUPS, r, p)
        dg = dtc.reshape(bt, chunk, SSD_GROUPS, r)
        w = cb[..., None] * decay * dg[:, None]
        y_intra = jnp.einsum('btsgr,bsgrp->btgrp', w, xg)
        hg = h.reshape(bt, SSD_GROUPS, r, p, SSD_STATE)
        y_inter = jnp.einsum('btgn,bgrpn->btgrp', cc, hg) * jnp.exp(cum).reshape(bt, chunk, SSD_GROUPS, r)[..., None]
        wt = (jnp.exp(cum[:, -1:, :] - cum) * dtc).reshape(bt, chunk, SSD_GROUPS, r)
        h_new = (hg * jnp.exp(cum[:, -1]).reshape(bt, SSD_GROUPS, r)[..., None, None]
                 + jnp.einsum('bsgr,bsgrp,bsgn->bgrpn', wt, xg, bc))
        return h_new.reshape(bt, nh, p, SSD_STATE), (y_intra + y_inter).reshape(bt, chunk, nh, p)

    h_fin, ys = lax.scan(step, h0, (to_chunks(x), to_chunks(dt), to_chunks(bm), to_chunks(cm)))
    return jnp.moveaxis(ys, 0, 1).reshape(bt, t, nh, p), h_fin


def gated_rmsnorm(y, z, g):
    v = y * jax.nn.silu(z)
    bt, t, _ = v.shape
    vg = v.reshape(bt, t, SSD_GROUPS, SSD_INNER // SSD_GROUPS)
    vg = vg * lax.rsqrt(jnp.mean(vg * vg, -1, keepdims=True) + RMS_EPS)
    return vg.reshape(bt, t, SSD_INNER) * g


def odd_prompt_mix(h, w_in_bf16, bsz, sc_w, sc_b, cv_w, cv_b, dt_bias, a_log, d_skip, norm_g):
    a = -jnp.exp(a_log.astype(jnp.float32))
    y_sc, xbc_c, dt, zg, new_sc, new_conv = inproj_odd_prompt(h, w_in_bf16, bsz, sc_w, sc_b, cv_w, cv_b, dt_bias)
    y, h_new = ssd_prompt(xbc_c, dt, a, zg, 0, d_skip, norm_g)
    return (y_sc, y), new_sc, new_conv, h_new


def odd_mix(z, sc_buf, conv_buf, h0, chunk, sc_w, sc_b, cv_w, cv_b, dt_bias, a_log, d_skip, norm_g):
    f32 = jnp.float32
    bt, t, _ = z.shape
    a = -jnp.exp(a_log.astype(f32))
    o1 = SC_DIM
    o2 = 2 * SC_DIM
    o3 = 3 * SC_DIM
    o4 = o3 + SSD_INNER
    o5 = o4 + SSD_CONV_DIM
    sc_h = z[..., :o1]
    sc_bg = z[..., o1:o2]
    sc_cg = z[..., o2:o3]
    zg = z[..., o3:o4]
    xbc = z[..., o4:o5]
    dt_raw = z[..., o5:]
    conv_sc, new_sc = causal_conv(sc_cg * sc_h, sc_buf.astype(f32), sc_w, sc_b)
    y_sc = sc_bg * conv_sc
    xbc_c, new_conv = causal_conv(xbc, conv_buf.astype(f32), cv_w, cv_b)
    xbc_c = jax.nn.silu(xbc_c)
    gn = SSD_GROUPS * SSD_STATE
    xs = xbc_c[..., :SSD_INNER].reshape(bt, t, SSD_HEADS, SSD_HEAD_DIM)
    bm = xbc_c[..., SSD_INNER:SSD_INNER + gn].reshape(bt, t, SSD_GROUPS, SSD_STATE)
    cm = xbc_c[..., SSD_INNER + gn:].reshape(bt, t, SSD_GROUPS, SSD_STATE)
    dt = jax.nn.softplus((dt_raw + dt_bias).astype(f32))
    y, h_new = ssd_scan(xs, dt, a, bm, cm, h0.astype(f32), chunk)
    y = (y + d_skip[:, None] * xs).reshape(bt, t, SSD_INNER)
    y = gated_rmsnorm(y, zg, norm_g)
    return (y_sc, y), new_sc, new_conv, h_new


def moe_ffn(x, w_r, b_r, w_gu_bf16, w_down_bf16):
    n, d = x.shape
    logits = jnp.dot(x, w_r, precision=lax.Precision.HIGHEST) + b_r
    top_v, top_i = lax.top_k(logits, TOP_K)
    gate = jax.nn.softmax(top_v, axis=-1)
    flat_e = top_i.reshape(-1)
    blk = 128
    assert (TOP_K * n) % blk == 0
    onehot = jax.nn.one_hot(flat_e, N_EXPERTS, dtype=jnp.float32).reshape(-1, blk, N_EXPERTS)
    tri = (jnp.arange(blk)[:, None] >= jnp.arange(blk)[None, :]).astype(jnp.float32)
    local = jnp.einsum('ij,bjk->bik', tri, onehot)
    block_total = local[:, -1, :]
    block_off = jnp.cumsum(block_total, axis=0) - block_total
    incl = (local + block_off[:, None, :]).reshape(-1, N_EXPERTS)
    rank = jnp.take_along_axis(incl, flat_e[:, None], axis=1)[:, 0].astype(jnp.int32) - 1
    counts = jnp.sum(block_total, axis=0).astype(jnp.int32)
    padded = ((counts + ROW_TILE - 1) // ROW_TILE) * ROW_TILE
    pad_start = jnp.cumsum(padded) - padded
    dest = (pad_start[flat_e] + rank).astype(jnp.int32)
    n_tiles = (TOP_K * n) // ROW_TILE + N_EXPERTS
    rows = n_tiles * ROW_TILE
    row_token = jnp.zeros((rows,), jnp.int32).at[dest].set(jnp.arange(TOP_K * n, dtype=jnp.int32) // TOP_K,
                                                           unique_indices=True, mode='promise_in_bounds')
    tile_end = jnp.cumsum(padded) // ROW_TILE
    tile_expert = jnp.minimum(jnp.searchsorted(tile_end, jnp.arange(n_tiles), side='right'),
                              N_EXPERTS - 1).astype(jnp.int32)
    n_used = tile_end[-1:].astype(jnp.int32)
    xs = x.at[row_token].get(mode='promise_in_bounds')
    ys = grouped_ffn(xs, w_gu_bf16, w_down_bf16, tile_expert, n_used)
    dest = dest.reshape(n, TOP_K)
    y0 = ys.at[dest[:, 0]].get(mode='promise_in_bounds')
    y1 = ys.at[dest[:, 1]].get(mode='promise_in_bounds')
    return gate[:, 0:1] * y0 + gate[:, 1:2] * y1


def kernel(x_prompt, x_sample, state_s5, cache_nsa_kv, state_win_kv, state_sc_conv, state_ssd_conv, state_ssd,
           page_table, ln_g, ln_b, w_in_even, s5_lam_re, s5_lam_im, s5_log_dt, s5_b, s5_c, s5_d, s5_w_glu,
           nsa_wk1, nsa_wk2, nsa_pe_k, nsa_wv1, nsa_wv2, nsa_pe_v, w_out_even, ffn_w_gu, ffn_w_down,
           w_in_odd, sc_conv_w, sc_conv_b, ssd_conv_w, ssd_conv_b, ssd_dt_bias, ssd_a_log, ssd_d, ssd_norm_g,
           w_out_odd, moe_router, moe_router_b, moe_w_gu, moe_w_down):
    f32 = jnp.float32
    bf16 = jnp.bfloat16
    bp, tp, d = x_prompt.shape
    bs, ts, _ = x_sample.shape
    n_p = bp * tp
    n_s = bs * ts
    w_buf = state_win_kv.shape[2]
    streams = [x_prompt.astype(f32).reshape(n_p, d), x_sample.astype(f32).reshape(n_s, d)]

    def flat(parts, n_rows):
        return [p.reshape(n_rows, p.shape[-1]) for p in parts]

    def out_proj(parts, w_out, width, h, g, b):
        w = w_out.astype(bf16)
        return matmul(flat(parts, h.shape[0]), [w[:width], w[width:]], ln=(h, g, b))

    def single_expert(n_rows):
        n_tiles = pl.cdiv(n_rows, min(ROW_TILE, n_rows))
        return jnp.zeros((n_tiles,), jnp.int32), jnp.full((1,), n_tiles, jnp.int32)

    s5p = s5_params(s5_lam_re[0], s5_lam_im[0], s5_log_dt[0], s5_b[0], s5_c[0], s5_d[0], s5_w_glu[0])
    cmpp = (nsa_wk1[0], nsa_wk2[0], nsa_pe_k[0], nsa_wv1[0], nsa_wv2[0], nsa_pe_v[0])
    w_in = w_in_even[0].astype(bf16)
    mix_p, s5_p, kv_p, win_p = even_prompt_mix(streams[0], w_in, bp, s5p, cmpp, w_buf)
    mix_s, s5_s, kv_s, win_s = even_sample_mix(streams[1], w_in, bs, state_s5[0], cache_nsa_kv[0], page_table,
                                               state_win_kv[0], s5p, cmpp)
    streams = [out_proj(mix, w_out_even[0], S5_DIM, h, ln_g[0, 0], ln_b[0, 0])
               for mix, h in zip((mix_p, mix_s), streams)]
    w_gu, w_down = to_bf16(ffn_w_gu), to_bf16(ffn_w_down)
    streams = [grouped_ffn(h, w_gu, w_down, *single_expert(h.shape[0]), ln=(ln_g[0, 1], ln_b[0, 1]))
               for h in streams]

    oddp = (sc_conv_w[0], sc_conv_b[0], ssd_conv_w[0], ssd_conv_b[0], ssd_dt_bias[0],
            ssd_a_log[0], ssd_d[0], ssd_norm_g[0])
    w_in = w_in_odd[0].astype(bf16)
    mix_p, scc_p, sdc_p, ssd_p = odd_prompt_mix(streams[0], w_in, bp, *oddp)
    zs = matmul([streams[1]], [w_in]).reshape(bs, ts, -1)
    mix_s, scc_s, sdc_s, ssd_s = odd_mix(zs, state_sc_conv[0], state_ssd_conv[0], state_ssd[0], ts, *oddp)
    streams = [out_proj(mix, w_out_odd[0], SC_DIM, h, ln_g[1, 0], ln_b[1, 0])
               for mix, h in zip((mix_p, mix_s), streams)]
    h = jnp.concatenate(streams, axis=0)
    f = moe_ffn(h, moe_router[0], moe_router_b[0], to_bf16(moe_w_gu[0]), to_bf16(moe_w_down[0]))
    h = layer_norm(ALPHA * h + f, ln_g[1, 1], ln_b[1, 1])

    hp = h[:n_p].reshape(bp, tp, d)
    hs = h[n_p:].reshape(bs, ts, d)
    st = lambda a, ref: a[None].astype(ref.dtype)
    return (hp.astype(x_prompt.dtype), hs.astype(x_sample.dtype),
            st(s5_p, state_s5), st(s5_s, state_s5),
            st(kv_p, cache_nsa_kv), st(kv_s, cache_nsa_kv),
            st(win_p, state_win_kv), st(win_s, state_win_kv),
            st(scc_p, state_sc_conv), st(scc_s, state_sc_conv),
            st(sdc_p, state_ssd_conv), st(sdc_s, state_ssd_conv),
            st(ssd_p, state_ssd), st(ssd_s, state_ssd))
```

```python
import functools
import math

import jax
import jax.numpy as jnp
from jax import lax
from jax.experimental import pallas as pl
from jax.experimental.pallas import tpu as pltpu

D_MODEL = 1024
DEPTH = 2
ALPHA = (2.0 * DEPTH) ** 0.25
LN_EPS = 1e-5
RMS_EPS = 1e-5
NEG = -1e30

S5_DIM = D_MODEL // 2
S5_GROUP = 16
S5_GROUPS = S5_DIM // S5_GROUP
S5_STATE = 64

HEAD_DIM = 64
N_HEADS = (D_MODEL // 2) // HEAD_DIM
KV_GROUPS = 2
HEADS_PER_GROUP = N_HEADS // KV_GROUPS
CMP_STRIDE = 16
CMP_LEN = 2 * CMP_STRIDE
SEL_BLOCK = 64
N_SEL = 16
WINDOW = 512
Q_BLOCK = 128
ROPE_THETA = 500000.0
ROT_DIM = HEAD_DIM // 4
FORCE = 1e4
NSA_Q = N_HEADS * HEAD_DIM
NSA_KV = 2 * KV_GROUPS * HEAD_DIM

SC_DIM = D_MODEL // 2
SC_WIDTH = 3
SSD_HEAD_DIM = 64
SSD_HEADS = 16
SSD_INNER = SSD_HEADS * SSD_HEAD_DIM
SSD_GROUPS = 4
SSD_STATE = 128
SSD_CONV = 4
SSD_CONV_DIM = SSD_INNER + 2 * SSD_GROUPS * SSD_STATE
SSD_CHUNK = 128

D_FF = 2816
N_EXPERTS = 8
TOP_K = 2

VMEM_LIMIT_BYTES = 56 * 1024 * 1024
LANES = 128
S5_N = S5_GROUPS * S5_STATE
S5_LT = S5_N // LANES
S5_CHUNK = 256
SEL_TILE = 1024
QK_SCALE = HEAD_DIM ** -0.5 * math.log2(math.e)
REMOVED = -3e38
PAGE_SIZE = 128
PAGES_PER_STEP = 64
NEW_PAD = 128
CAST_ROWS = 512
CAST_SPLIT = 4
TAIL = 8
ROW_TILE = 512
FF_TILE = D_FF // 2


def _cparams(*sem):
    return pltpu.CompilerParams(dimension_semantics=sem, vmem_limit_bytes=VMEM_LIMIT_BYTES)


def _deepnorm(resid, update, g, b):
    y = ALPHA * resid + update
    mu = jnp.mean(y, axis=-1, keepdims=True)
    yc = y - mu
    var = jnp.mean(yc * yc, axis=-1, keepdims=True)
    return yc * lax.rsqrt(var + LN_EPS) * g + b


def _mm_kernel(*refs, n_in, fuse_ln):
    xs, ws = refs[0:n_in], refs[n_in:2 * n_in]
    o_ref = refs[-1]
    acc = None
    for x_ref, w_ref in zip(xs, ws):
        part = jnp.dot(x_ref[...].astype(jnp.bfloat16), w_ref[...], preferred_element_type=jnp.float32)
        acc = part if acc is None else acc + part
    if fuse_ln:
        r_ref, g_ref, b_ref = refs[2 * n_in:2 * n_in + 3]
        acc = _deepnorm(r_ref[...], acc, g_ref[...], b_ref[...])
    o_ref[...] = acc


def matmul(xs, ws_bf16, ln=None):
    m = xs[0].shape[0]
    n = ws_bf16[0].shape[1]
    tile = min(ROW_TILE, m)
    row = lambda width: pl.BlockSpec((tile, width), lambda i: (i, 0))
    fixed = lambda shape: pl.BlockSpec(shape, lambda i: (0, 0), pipeline_mode=pl.Buffered(1))
    in_specs = [row(x.shape[1]) for x in xs] + [fixed(w.shape) for w in ws_bf16]
    args = list(xs) + list(ws_bf16)
    if ln is not None:
        resid, g, b = ln
        in_specs += [row(n), fixed((1, n)), fixed((1, n))]
        args += [resid, g.reshape(1, n), b.reshape(1, n)]
    return pl.pallas_call(
        functools.partial(_mm_kernel, n_in=len(xs), fuse_ln=ln is not None),
        grid=(pl.cdiv(m, tile),),
        in_specs=in_specs,
        out_specs=row(n),
        out_shape=jax.ShapeDtypeStruct((m, n), jnp.float32),
        compiler_params=_cparams("parallel"),
        name="matmul",
    )(*args)


def _cast_kernel(*refs):
    o_ref = refs[-1]
    o_ref[...] = jnp.concatenate([r[...].astype(o_ref.dtype) for r in refs[:-1]], axis=1)


def to_bf16(w):
    shape = w.shape
    w2 = w.reshape(-1, shape[-1])
    rows, cols = w2.shape
    split = CAST_SPLIT if cols % (CAST_SPLIT * LANES) == 0 else 1
    out = pl.pallas_call(
        _cast_kernel,
        grid=(pl.cdiv(rows, CAST_ROWS),),
        in_specs=[pl.BlockSpec((CAST_ROWS, cols // split), lambda i, c=c: (i, c)) for c in range(split)],
        out_specs=pl.BlockSpec((CAST_ROWS, cols), lambda i: (i, 0)),
        out_shape=jax.ShapeDtypeStruct((rows, cols), jnp.bfloat16),
        compiler_params=_cparams("parallel"),
        name="to_bf16",
    )(*([w2] * split))
    return out.reshape(shape)


def _ffn_kernel(te_ref, nt_ref, x_ref, wg_ref, wu_ref, wd_ref, *rest, fuse_ln):
    o_ref = rest[-1]
    t = pl.program_id(0)
    j = pl.program_id(1)

    @pl.when(t < nt_ref[0])
    def _():
        x = x_ref[...].astype(jnp.bfloat16)
        g = jnp.dot(x, wg_ref[...], preferred_element_type=jnp.float32)
        u = jnp.dot(x, wu_ref[...], preferred_element_type=jnp.float32)
        h = (g * jax.nn.sigmoid(g) * u).astype(jnp.bfloat16)
        part = jnp.dot(h, wd_ref[...], preferred_element_type=jnp.float32)

        @pl.when(j == 0)
        def _():
            o_ref[...] = part

        @pl.when(j > 0)
        def _():
            if fuse_ln:
                o_ref[...] = _deepnorm(x_ref[...], o_ref[...] + part, rest[0][...], rest[1][...])
            else:
                o_ref[...] += part

    @pl.when(jnp.logical_and(t >= nt_ref[0], j == 0))
    def _():
        o_ref[...] = jnp.zeros_like(o_ref)


def grouped_ffn(x, w_gu_bf16, w_down_bf16, tile_expert, n_tiles_used, ln=None):
    r, d = x.shape
    nf = D_FF // FF_TILE
    assert nf == 2
    tile = min(ROW_TILE, r)
    n_tiles = pl.cdiv(r, tile)
    in_specs = [
        pl.BlockSpec((tile, d), lambda t, j, te, nt: (t, 0)),
        pl.BlockSpec((None, d, FF_TILE), lambda t, j, te, nt: (te[t], 0, j)),
        pl.BlockSpec((None, d, FF_TILE), lambda t, j, te, nt: (te[t], 0, nf + j)),
        pl.BlockSpec((None, FF_TILE, d), lambda t, j, te, nt: (te[t], j, 0)),
    ]
    args = [tile_expert, n_tiles_used, x, w_gu_bf16, w_gu_bf16, w_down_bf16]
    if ln is not None:
        in_specs += [pl.BlockSpec((1, d), lambda t, j, te, nt: (0, 0))] * 2
        args += [ln[0].reshape(1, d), ln[1].reshape(1, d)]
    grid_spec = pltpu.PrefetchScalarGridSpec(
        num_scalar_prefetch=2,
        grid=(n_tiles, nf),
        in_specs=in_specs,
        out_specs=pl.BlockSpec((tile, d), lambda t, j, te, nt: (t, 0)),
    )
    return pl.pallas_call(
        functools.partial(_ffn_kernel, fuse_ln=ln is not None),
        grid_spec=grid_spec,
        out_shape=jax.ShapeDtypeStruct((r, d), jnp.float32),
        compiler_params=_cparams("parallel", "arbitrary"),
        name="grouped_ffn",
    )(*args)


def _s5_kernel(u_ref, perm_ref, h0r_ref, h0i_ref, ar_ref, ai_ref, bbr_ref, bbi_ref, cr_ref, ci_ref, d_ref, wglu_ref,
               y_ref, hro_ref, hio_ref, bur, bui, sr, si, hr, hi, *, chains, chunk):
    j = pl.program_id(0)

    @pl.when(j == 0)
    def _():
        hr[...] = h0r_ref[...]
        hi[...] = h0i_ref[...]

    rows_n = chains * chunk
    u = u_ref[...].reshape(rows_n, S5_DIM)
    to_tc = perm_ref[...]
    ub = jnp.dot(to_tc, u.astype(jnp.bfloat16), preferred_element_type=jnp.float32).astype(jnp.bfloat16)
    hd, hn = S5_DIM // 2, S5_N // 2

    def b_proj(w_ref):
        return jnp.concatenate([jnp.dot(ub[:, h * hd:(h + 1) * hd], w_ref[h * hd:(h + 1) * hd, h * hn:(h + 1) * hn],
                                        preferred_element_type=jnp.float32) for h in range(2)], axis=1)

    bu_r = b_proj(bbr_ref)
    bu_i = b_proj(bbi_ref)
    for k in range(S5_LT):
        bur[k] = bu_r[:, k * LANES:(k + 1) * LANES]
        bui[k] = bu_i[:, k * LANES:(k + 1) * LANES]
    ar = [jnp.broadcast_to(ar_ref[:, k * LANES:(k + 1) * LANES], (chains, LANES)) for k in range(S5_LT)]
    ai = [jnp.broadcast_to(ai_ref[:, k * LANES:(k + 1) * LANES], (chains, LANES)) for k in range(S5_LT)]

    def body(t, carry):
        rows = pl.ds(t * chains, chains)
        out = []
        for k in range(S5_LT):
            xr, xi = carry[2 * k], carry[2 * k + 1]
            nr = ar[k] * xr - ai[k] * xi + bur[k, rows, :]
            ni = ar[k] * xi + ai[k] * xr + bui[k, rows, :]
            sr[k, rows, :] = nr
            si[k, rows, :] = ni
            out += [nr, ni]
        return tuple(out)

    init = []
    for k in range(S5_LT):
        init += [hr[:, k * LANES:(k + 1) * LANES], hi[:, k * LANES:(k + 1) * LANES]]
    fin = lax.fori_loop(0, chunk, body, tuple(init), unroll=2)
    xr = jnp.concatenate(fin[0::2], axis=1)
    xi = jnp.concatenate(fin[1::2], axis=1)
    hr[...] = xr
    hi[...] = xi
    hro_ref[...] = xr
    hio_ref[...] = xi
    s_r = jnp.concatenate([sr[k] for k in range(S5_LT)], axis=1).astype(jnp.bfloat16)
    s_i = jnp.concatenate([si[k] for k in range(S5_LT)], axis=1).astype(jnp.bfloat16)
    y = jnp.concatenate(
        [jnp.dot(s_r[:, h * hn:(h + 1) * hn], cr_ref[h * hn:(h + 1) * hn, h * hd:(h + 1) * hd],
                 preferred_element_type=jnp.float32)
         - jnp.dot(s_i[:, h * hn:(h + 1) * hn], ci_ref[h * hn:(h + 1) * hn, h * hd:(h + 1) * hd],
                   preferred_element_type=jnp.float32) for h in range(2)], axis=1)
    y = sum(lax.dot_general(to_tc, part, (((0,), (0,)), ((), ())), preferred_element_type=jnp.float32)
            for part in _split3(y)) + d_ref[...] * u
    z = jax.nn.gelu(y)
    gate = jax.nn.sigmoid(jnp.dot(z.astype(jnp.bfloat16), wglu_ref[...], preferred_element_type=jnp.float32))
    y_ref[...] = (z * gate).reshape(chains, chunk, S5_DIM)


def s5_params(lam_re, lam_im, log_dt, b, c, d, w_glu):
    f32 = jnp.float32
    dt = jnp.exp(log_dt.astype(f32))[:, None]
    mag = jnp.exp(lam_re * dt)
    ang = lam_im * dt
    ab_re = mag * jnp.cos(ang)
    ab_im = mag * jnp.sin(ang)
    den = lam_re * lam_re + lam_im * lam_im
    nr = ab_re - 1.0
    coef_re = (nr * lam_re + ab_im * lam_im) / den
    coef_im = (ab_im * lam_re - nr * lam_im) / den
    b_re = b[..., 0].astype(f32)
    b_im = b[..., 1].astype(f32)
    bb_re = coef_re[..., None] * b_re - coef_im[..., None] * b_im
    bb_im = coef_re[..., None] * b_im + coef_im[..., None] * b_re
    eye = jnp.eye(S5_GROUPS, dtype=f32)
    bbr = jnp.einsum('gnk,gh->gkhn', bb_re, eye).reshape(S5_DIM, S5_N).astype(jnp.bfloat16)
    bbi = jnp.einsum('gnk,gh->gkhn', bb_im, eye).reshape(S5_DIM, S5_N).astype(jnp.bfloat16)
    cr = jnp.einsum('gkn,gh->gnhk', c[..., 0].astype(f32), eye).reshape(S5_N, S5_DIM).astype(jnp.bfloat16)
    ci = jnp.einsum('gkn,gh->gnhk', c[..., 1].astype(f32), eye).reshape(S5_N, S5_DIM).astype(jnp.bfloat16)
    return (ab_re.reshape(1, S5_N), ab_im.reshape(1, S5_N), bbr, bbi, cr, ci,
            d.astype(f32).reshape(1, S5_DIM), w_glu.astype(jnp.bfloat16))


def s5_scan(u, h0, params, chunk):
    chains, t, _ = u.shape
    ar, ai, bbr, bbi, cr, ci, d, wglu = params
    h0r = h0[..., 0].reshape(chains, S5_N)
    h0i = h0[..., 1].reshape(chains, S5_N)
    full = lambda shape: pl.BlockSpec(shape, lambda j: (0,) * len(shape))
    rows = chains * chunk
    r = jnp.arange(rows)
    to_tc = (r[None, :] == (r[:, None] % chains) * chunk + r[:, None] // chains).astype(jnp.bfloat16)
    y, hr, hi = pl.pallas_call(
        functools.partial(_s5_kernel, chains=chains, chunk=chunk),
        grid=(t // chunk,),
        in_specs=[pl.BlockSpec((chains, chunk, S5_DIM), lambda j: (0, j, 0)), full((rows, rows)),
                  full((chains, S5_N)), full((chains, S5_N)), full((1, S5_N)), full((1, S5_N)),
                  full((S5_DIM, S5_N)), full((S5_DIM, S5_N)), full((S5_N, S5_DIM)), full((S5_N, S5_DIM)),
                  full((1, S5_DIM)), full((S5_DIM, S5_DIM))],
        out_specs=[pl.BlockSpec((chains, chunk, S5_DIM), lambda j: (0, j, 0)),
                   full((chains, S5_N)), full((chains, S5_N))],
        out_shape=[jax.ShapeDtypeStruct((chains, t, S5_DIM), jnp.float32),
                   jax.ShapeDtypeStruct((chains, S5_N), jnp.float32),
                   jax.ShapeDtypeStruct((chains, S5_N), jnp.float32)],
        scratch_shapes=[pltpu.VMEM((S5_LT, rows, LANES), jnp.float32)] * 4
                       + [pltpu.VMEM((chains, S5_N), jnp.float32)] * 2,
        compiler_params=_cparams("arbitrary"),
        name="s5_scan",
    )(u, to_tc, h0r, h0i, ar, ai, bbr, bbi, cr, ci, d, wglu)
    new_state = jnp.stack([hr.reshape(chains, S5_GROUPS, S5_STATE), hi.reshape(chains, S5_GROUPS, S5_STATE)],
                          axis=-1)
    return y, new_state


def _dot_nt(a, b):
    return lax.dot_general(a, b, (((1,), (1,)), ((), ())), preferred_element_type=jnp.float32)


def _split3(x):
    hi = x.astype(jnp.bfloat16)
    rem = x - hi.astype(jnp.float32)
    mid = rem.astype(jnp.bfloat16)
    lo = (rem - mid.astype(jnp.float32)).astype(jnp.bfloat16)
    return hi, mid, lo


def _softmax_rows(s, mask):
    s = jnp.where(mask, s, NEG)
    m = jnp.max(s, axis=-1, keepdims=True)
    p = jnp.exp2(s - m)
    inv = jnp.where(m > 0.5 * NEG, 1.0 / jnp.sum(p, axis=-1, keepdims=True), 0.0)
    return p * inv


def _nsa_prompt_kernel(q_ref, gate_ref, kc_ref, vc_ref, ks_ref, vs_ref, kw_ref, vw_ref, o_ref, *, n_cmp, n_blk):
    f32, bf16 = jnp.float32, jnp.bfloat16
    r4 = HEADS_PER_GROUP
    n_cpad = kc_ref.shape[0]
    start = pl.program_id(1) * Q_BLOCK
    q = q_ref[...] * QK_SCALE
    gate = gate_ref[...]
    lane = lax.broadcasted_iota(jnp.int32, (Q_BLOCK, LANES), 1)
    qpos = start + lax.broadcasted_iota(jnp.int32, (Q_BLOCK, 1), 0)
    n_idx = lax.broadcasted_iota(jnp.int32, (Q_BLOCK, n_cpad), 1)
    cmask = (((n_idx * CMP_STRIDE + (CMP_LEN - 1)) <= qpos) & (n_idx < n_cmp))[None]
    ratio = SEL_BLOCK // CMP_STRIDE
    gsum = (lax.broadcasted_iota(jnp.int32, (n_blk, n_cpad), 1) // ratio
            == lax.broadcasted_iota(jnp.int32, (n_blk, n_cpad), 0)).astype(bf16)
    blk = lax.broadcasted_iota(jnp.int32, (n_blk, Q_BLOCK), 0)
    blk_f = blk.astype(f32)
    jq = (start + lax.broadcasted_iota(jnp.int32, (n_blk, Q_BLOCK), 1)) // SEL_BLOCK
    force = jnp.where((blk == 0) | (blk == jq) | (blk == jq - 1), FORCE, 0.0)
    qgs, o_cs, sels = [], [], []
    for g in range(KV_GROUPS):
        keep = (lane < HEAD_DIM) if g == 0 else (lane >= HEAD_DIM)
        parts = []
        for r in range(r4):
            h = r4 * g + r
            tile = q[:, (h // 2) * LANES:(h // 2 + 1) * LANES]
            if h % 2 != g:
                tile = pltpu.roll(tile, HEAD_DIM, axis=1)
            parts.append(jnp.where(keep, tile, 0.0))
        qg = jnp.concatenate(parts, axis=0).astype(bf16)
        qgs.append(qg)

        p_c = _softmax_rows(_dot_nt(qg, kc_ref[...]).reshape(r4, Q_BLOCK, n_cpad), cmask)
        o_cs.append(jnp.dot(p_c.reshape(r4 * Q_BLOCK, n_cpad).astype(bf16), vc_ref[...],
                            preferred_element_type=f32).reshape(r4, Q_BLOCK, LANES))
        psum = p_c[0] + p_c[1] + p_c[2] + p_c[3]
        imp_t = sum(_dot_nt(gsum, part) for part in _split3(psum))

        score = jnp.where(blk <= jq, imp_t + force, NEG)
        sel_t = jnp.zeros((n_blk, Q_BLOCK), f32)
        for _ in range(min(N_SEL, n_blk)):
            m = jnp.max(score, axis=0, keepdims=True)
            idx = jnp.min(jnp.where(score == m, blk_f, float(n_blk)), axis=0, keepdims=True)
            hit = blk_f == idx
            sel_t = jnp.where(hit & (m > 0.5 * NEG), 1.0, sel_t)
            score = jnp.where(hit, REMOVED, score)
        sels.append(sel_t.T)

    n_full = start // SEL_TILE
    expand0 = (lax.broadcasted_iota(jnp.int32, (n_blk, SEL_TILE), 0)
               == lax.broadcasted_iota(jnp.int32, (n_blk, SEL_TILE), 1) // SEL_BLOCK).astype(bf16)

    def tile_update(i, carry, causal):
        off = pl.multiple_of(i * SEL_TILE, SEL_TILE)
        k = ks_ref[pl.ds(off, SEL_TILE), :]
        v = vs_ref[pl.ds(off, SEL_TILE), :]
        vlane = lax.broadcasted_iota(jnp.int32, (SEL_TILE, LANES), 1)
        v_ones = [jnp.where((vlane < HEAD_DIM) == (g == 0), v, jnp.ones_like(v)) for g in range(KV_GROUPS)]
        out = []
        for g in range(KV_GROUPS):
            m_run, l_run, acc = carry[g]
            s_t = _dot_nt(qgs[g], k).reshape(r4, Q_BLOCK, SEL_TILE)
            shifted = pltpu.roll(sels[g], (n_blk - i * (SEL_TILE // SEL_BLOCK)) % n_blk, axis=1).astype(bf16)
            mk = jnp.dot(shifted, expand0, preferred_element_type=f32) > 0.5
            if causal:
                kpos = i * SEL_TILE + lax.broadcasted_iota(jnp.int32, (Q_BLOCK, SEL_TILE), 1)
                mk = mk & (kpos <= qpos)
            s_t = jnp.where(mk[None], s_t, NEG)
            m_new = jnp.maximum(m_run, jnp.max(s_t, axis=-1, keepdims=True))
            alpha = jnp.exp2(m_run - m_new)
            p = jnp.exp2((s_t - m_new).astype(bf16))
            pv = jnp.dot(p.reshape(r4 * Q_BLOCK, SEL_TILE), v_ones[g], preferred_element_type=f32)
            pv = pv.reshape(r4, Q_BLOCK, LANES)
            l_new = alpha * l_run + pv[:, :, (1 - g) * HEAD_DIM:(1 - g) * HEAD_DIM + 1]
            out.append((m_new, l_new, alpha * acc + pv))
        return tuple(out)

    init = (jnp.full((r4, Q_BLOCK, 1), NEG, f32), jnp.zeros((r4, Q_BLOCK, 1), f32),
            jnp.zeros((r4, Q_BLOCK, LANES), f32))
    carry = lax.fori_loop(0, n_full, lambda i, c: tile_update(i, c, False), (init, init))
    fin = tile_update(n_full, carry, True)

    n_win = WINDOW + Q_BLOCK
    woff = pl.multiple_of(start, Q_BLOCK)
    kwin = kw_ref[pl.ds(woff, n_win), :]
    vwin = vw_ref[pl.ds(woff, n_win), :]
    wpos = start - WINDOW + lax.broadcasted_iota(jnp.int32, (Q_BLOCK, n_win), 1)
    wmask = ((wpos <= qpos) & (wpos > qpos - WINDOW) & (wpos >= 0))[None]
    heads = [None] * N_HEADS
    for g in range(KV_GROUPS):
        m_fin, l_fin, acc = fin[g]
        o_s = acc * jnp.where(m_fin > 0.5 * NEG, 1.0 / l_fin, 0.0)
        p_w = _softmax_rows(_dot_nt(qgs[g], kwin).reshape(r4, Q_BLOCK, n_win), wmask)
        o_w = jnp.dot(p_w.reshape(r4 * Q_BLOCK, n_win).astype(bf16), vwin,
                      preferred_element_type=f32).reshape(r4, Q_BLOCK, LANES)
        for r in range(r4):
            h = r4 * g + r
            heads[h] = (gate[:, 3 * h:3 * h + 1] * o_cs[g][r] + gate[:, 3 * h + 1:3 * h + 2] * o_s[r]
                        + gate[:, 3 * h + 2:3 * h + 3] * o_w[r])

    tiles = []
    for j in range(N_HEADS // 2):
        even, odd = heads[2 * j], heads[2 * j + 1]
        if j // 2 == 0:
            tiles.append(jnp.where(lane < HEAD_DIM, even, pltpu.roll(odd, HEAD_DIM, axis=1)))
        else:
            tiles.append(jnp.where(lane < HEAD_DIM, pltpu.roll(even, HEAD_DIM, axis=1), odd))
    o_ref[...] = jnp.concatenate(tiles, axis=1)


def nsa_prompt(q, gates, kc, vc, kvs, kvw_pad):
    b, t, _ = q.shape
    n_cpad = kc.shape[1]
    kern = functools.partial(_nsa_prompt_kernel, n_cmp=t // CMP_STRIDE - 1, n_blk=t // SEL_BLOCK)
    whole = lambda rows, c=0: pl.BlockSpec((None, rows, LANES), lambda i, j: (i, 0, c))
    return pl.pallas_call(
        kern,
        grid=(b, t // Q_BLOCK),
        in_specs=[pl.BlockSpec((None, Q_BLOCK, NSA_Q), lambda i, j: (i, j, 0)),
                  pl.BlockSpec((None, Q_BLOCK, 3 * N_HEADS), lambda i, j: (i, j, 0)),
                  whole(n_cpad), whole(n_cpad), whole(t, 0), whole(t, 1), whole(t + WINDOW, 0),
                  whole(t + WINDOW, 1)],
        out_specs=pl.BlockSpec((None, Q_BLOCK, NSA_Q), lambda i, j: (i, j, 0)),
        out_shape=jax.ShapeDtypeStruct((b, t, NSA_Q), jnp.float32),
        compiler_params=_cparams("parallel", "arbitrary"),
        name="nsa_prompt",
    )(q, gates, kc, vc, kvs, kvs, kvw_pad, kvw_pad)


def _compress_kernel(ch_ref, pet_ref, peb_ref, w1t_ref, w1b_ref, w2_ref, o_ref):
    bf16 = jnp.bfloat16
    ch = ch_ref[...]
    n_ch = ch.shape[0]
    a = jnp.dot((ch + pet_ref[...]).astype(bf16), w1t_ref[...], preferred_element_type=jnp.float32)
    b = jnp.dot((ch + peb_ref[...]).astype(bf16), w1b_ref[...], preferred_element_type=jnp.float32)
    pre = a + pltpu.roll(b, n_ch - 1, axis=0)
    o_ref[...] = jnp.dot(jax.nn.gelu(pre).astype(bf16), w2_ref[...],
                         preferred_element_type=jnp.float32).astype(o_ref.dtype)


def compress_params(w1, w2, pe):
    f32 = jnp.float32
    eye = jnp.eye(KV_GROUPS, dtype=f32)
    w1r = w1.astype(f32).reshape(2, CMP_STRIDE, HEAD_DIM, HEAD_DIM)
    big = jnp.einsum('hjde,gk->hjgdke', w1r, eye).reshape(2, CMP_STRIDE * LANES, LANES).astype(jnp.bfloat16)
    w2bd = jnp.einsum('de,gk->gdke', w2.astype(f32), eye).reshape(LANES, LANES).astype(jnp.bfloat16)
    per = pe.astype(f32).reshape(2, CMP_STRIDE, 1, HEAD_DIM)
    pe_rows = jnp.broadcast_to(per, (2, CMP_STRIDE, KV_GROUPS, HEAD_DIM)).reshape(2, 1, CMP_STRIDE * LANES)
    return pe_rows[0], pe_rows[1], big[0], big[1], w2bd


def compress_prompt(x, params):
    b, t, _ = x.shape
    n_ch = t // CMP_STRIDE
    ch = x.reshape(b, n_ch, CMP_STRIDE * LANES)
    pet, peb, w1t, w1b, w2bd = params
    full = lambda shape: pl.BlockSpec(shape, lambda i: (0,) * len(shape))
    return pl.pallas_call(
        _compress_kernel,
        grid=(b,),
        in_specs=[pl.BlockSpec((None, n_ch, CMP_STRIDE * LANES), lambda i: (i, 0, 0)),
                  full((1, CMP_STRIDE * LANES)), full((1, CMP_STRIDE * LANES)),
                  full((CMP_STRIDE * LANES, LANES)), full((CMP_STRIDE * LANES, LANES)), full((LANES, LANES))],
        out_specs=pl.BlockSpec((None, n_ch, LANES), lambda i: (i, 0, 0)),
        out_shape=jax.ShapeDtypeStruct((b, n_ch, LANES), jnp.bfloat16),
        compiler_params=_cparams("parallel"),
        name="compress_prompt",
    )(ch, pet, peb, w1t, w1b, w2bd)


def _cmp_sample_kernel(pt_ref, *refs, n_pages):
    f32, bf16 = jnp.float32, jnp.bfloat16
    pp = PAGES_PER_STEP
    pages = refs[0:pp]
    (perm_ref, newk_ref, newv_ref, wk_ref, wv_ref, ck_ref, cv_ref, w2k_ref, w2v_ref,
     kc_ref, vc_ref, slab_k, slab_v) = refs[pp:]
    s = pl.program_id(1)
    cpp = PAGE_SIZE // CMP_STRIDE
    base = pl.multiple_of(s * (pp * cpp), pp * cpp)
    for half, slab in enumerate((slab_k, slab_v)):
        for i in range(pp):
            page = pages[i][half * LANES:(half + 1) * LANES, :].astype(bf16)
            rows = _dot_nt(perm_ref[...], page)
            for j in range(CMP_STRIDE):
                slab[j, pl.ds(base + i * cpp, cpp), :] = rows[j * cpp:(j + 1) * cpp, :]

    @pl.when(s == pl.num_programs(1) - 1)
    def _():
        n_ch = n_pages * (PAGE_SIZE // CMP_STRIDE)
        row = lax.broadcasted_iota(jnp.int32, (n_ch, LANES), 0)
        for slab, new_ref, w_ref, c_ref, w2_ref, o_ref in ((slab_k, newk_ref, wk_ref, ck_ref, w2k_ref, kc_ref),
                                                           (slab_v, newv_ref, wv_ref, cv_ref, w2v_ref, vc_ref)):
            ch = jnp.concatenate([slab[j] for j in range(CMP_STRIDE)], axis=1).astype(bf16)
            ab = jnp.dot(ch, w_ref[...], preferred_element_type=f32)
            b_new = jnp.dot(new_ref[...].astype(bf16), w_ref[...], preferred_element_type=f32)[0:1, LANES:]
            nxt = pltpu.roll(ab[:, LANES:], n_ch - 1, axis=0)
            nxt = jnp.where(row == n_ch - 1, b_new, nxt)
            pre = ab[:, :LANES] + nxt + c_ref[...]
            o_ref[...] = jnp.dot(jax.nn.gelu(pre).astype(bf16), w2_ref[...],
                                 preferred_element_type=f32).astype(o_ref.dtype)


def compress_sample_params(w1, w2, pe):
    pet, peb, w1t, w1b, w2bd = compress_params(w1, w2, pe)
    hp = lax.Precision.HIGHEST
    const = (jnp.dot(pet, w1t.astype(jnp.float32), precision=hp)
             + jnp.dot(peb, w1b.astype(jnp.float32), precision=hp))
    return jnp.concatenate([w1t, w1b], axis=1), const, w2bd


def _page_spec(i, pair):
    return pl.BlockSpec((None, 2 * LANES, PAGE_SIZE),
                        lambda b, s, pt: (pt[b, PAGES_PER_STEP * s + i], pair, 0))


def _per_seq(shape):
    return pl.BlockSpec((None,) + shape, lambda b, s, pt: (b, 0, 0))


def compress_sample(pool_t, page_table, new_k, new_v, pk, pv):
    bsz, n_pages = page_table.shape
    pp = PAGES_PER_STEP
    n_ch = n_pages * (PAGE_SIZE // CMP_STRIDE)
    t_new = new_k.shape[1]

    def chunk_rows(x):
        x = jnp.pad(x, ((0, 0), (0, CMP_STRIDE - t_new), (0, 0))).reshape(bsz, 1, CMP_STRIDE * LANES)
        return jnp.pad(x, ((0, 0), (0, 7), (0, 0)))

    full = lambda shape: pl.BlockSpec(shape, lambda b, s, pt: (0,) * len(shape))
    r = jnp.arange(PAGE_SIZE)
    cpp = PAGE_SIZE // CMP_STRIDE
    perm = (r[None, :] == (r[:, None] % cpp) * CMP_STRIDE + r[:, None] // cpp).astype(jnp.bfloat16)
    wk, ck, w2k = pk
    wv, cv, w2v = pv
    grid_spec = pltpu.PrefetchScalarGridSpec(
        num_scalar_prefetch=1,
        grid=(bsz, n_pages // pp),
        in_specs=[_page_spec(i, 0) for i in range(pp)]
                 + [full((PAGE_SIZE, PAGE_SIZE)), _per_seq((8, CMP_STRIDE * LANES)),
                    _per_seq((8, CMP_STRIDE * LANES)),
                    full((CMP_STRIDE * LANES, 2 * LANES)), full((CMP_STRIDE * LANES, 2 * LANES)),
                    full((1, LANES)), full((1, LANES)), full((LANES, LANES)), full((LANES, LANES))],
        out_specs=[_per_seq((n_ch, LANES)), _per_seq((n_ch, LANES))],
        scratch_shapes=[pltpu.VMEM((CMP_STRIDE, n_ch, LANES), jnp.float32)] * 2,
    )
    return pl.pallas_call(
        functools.partial(_cmp_sample_kernel, n_pages=n_pages),
        grid_spec=grid_spec,
        out_shape=[jax.ShapeDtypeStruct((bsz, n_ch, LANES), jnp.bfloat16)] * 2,
        compiler_params=_cparams("parallel", "arbitrary"),
        name="compress_sample",
    )(page_table, *([pool_t] * pp), perm, chunk_rows(new_k), chunk_rows(new_v), wk, wv, ck, cv, w2k, w2v)


def _nsa_sample_kernel(pt_ref, *refs, n_pages, t_new, w_buf):
    f32, bf16 = jnp.float32, jnp.bfloat16
    pp = PAGES_PER_STEP
    q_ref, gate_ref, kc_ref, vc_ref = refs[0:4]
    pages = refs[4:4 + pp]
    (ksn_ref, vsn_ref, win_ref, kwn_ref, vwn_ref, o_ref,
     sel_scr, exp_scr, oc_scr, m_scr, l_scr, acc_scr) = refs[4 + pp:]
    r4, g2 = HEADS_PER_GROUP, KV_GROUPS
    n_rows = g2 * r4 * t_new
    past_len = n_pages * PAGE_SIZE
    n_cmp = kc_ref.shape[0]
    n_bpad = sel_scr.shape[1]
    tile = pp * PAGE_SIZE
    s = pl.program_id(1)
    qall = q_ref[...]
    qpos = past_len + lax.broadcasted_iota(jnp.int32, (n_rows, 1), 0) % t_new

    def grouped(x):
        return x.reshape(g2, 1, t_new, x.shape[-1])

    @pl.when(s == 0)
    def _():
        s_c = _dot_nt(qall, kc_ref[...])
        n_idx = lax.broadcasted_iota(jnp.int32, (n_rows, n_cmp), 1)
        p_c = _softmax_rows(s_c, (n_idx * CMP_STRIDE + (CMP_LEN - 1)) <= qpos)
        oc_scr[...] = jnp.dot(p_c.astype(bf16), vc_ref[...], preferred_element_type=f32)
        psum = jnp.sum(p_c.reshape(g2, r4, t_new, n_cmp), axis=1).reshape(g2 * t_new, n_cmp)
        psum = jnp.concatenate([psum, jnp.zeros((LANES - g2 * t_new, n_cmp), f32)], axis=0)
        p_hi = psum.astype(bf16)
        rem = psum - p_hi.astype(f32)
        p_mid = rem.astype(bf16)
        p_lo = (rem - p_mid.astype(f32)).astype(bf16)
        ratio = SEL_BLOCK // CMP_STRIDE
        gsum = (lax.broadcasted_iota(jnp.int32, (n_bpad, n_cmp), 1) // ratio
                == lax.broadcasted_iota(jnp.int32, (n_bpad, n_cmp), 0)).astype(bf16)
        imp_t = _dot_nt(gsum, p_hi) + _dot_nt(gsum, p_mid) + _dot_nt(gsum, p_lo)
        blk = lax.broadcasted_iota(jnp.int32, (n_bpad, LANES), 0)
        jq = (past_len + lax.broadcasted_iota(jnp.int32, (n_bpad, LANES), 1) % t_new) // SEL_BLOCK
        forced = (blk == 0) | (blk == jq) | (blk == jq - 1)
        score = jnp.where(blk <= jq, imp_t + jnp.where(forced, FORCE, 0.0), NEG)
        blk_f = blk.astype(f32)
        sel_t = jnp.zeros((n_bpad, LANES), f32)
        for _ in range(N_SEL):
            m = jnp.max(score, axis=0, keepdims=True)
            idx = jnp.min(jnp.where(score == m, blk_f, float(n_bpad)), axis=0, keepdims=True)
            hit = blk_f == idx
            sel_t = jnp.where(hit & (m > 0.5 * NEG), 1.0, sel_t)
            score = jnp.where(hit, REMOVED, score)
        sel = jnp.concatenate([sel_t[k * LANES:(k + 1) * LANES].T for k in range(n_bpad // LANES)], axis=1)
        sel_scr[...] = sel[0:g2 * t_new]
        exp_scr[...] = (lax.broadcasted_iota(jnp.int32, (LANES, tile), 0)
                        == lax.broadcasted_iota(jnp.int32, (LANES, tile), 1) // SEL_BLOCK).astype(bf16)
        m_scr[...] = jnp.full(m_scr.shape, NEG, f32)
        l_scr[...] = jnp.zeros(l_scr.shape, f32)
        acc_scr[...] = jnp.zeros(acc_scr.shape, f32)

    def online_update(s_t, mk, v, v_feature_major):
        n = s_t.shape[-1]
        s4 = jnp.where(mk, s_t.reshape(g2, r4, t_new, n), NEG)
        m_run = m_scr[...].reshape(g2, r4, t_new, 1)
        m_new = jnp.maximum(m_run, jnp.max(s4, axis=-1, keepdims=True))
        alpha = jnp.exp2(m_run - m_new)
        p = jnp.exp2(s4 - m_new)
        l_new = alpha * l_scr[...].reshape(g2, r4, t_new, 1) + jnp.sum(p, axis=-1, keepdims=True)
        pb = p.reshape(n_rows, n).astype(bf16)
        pv = _dot_nt(pb, v) if v_feature_major else jnp.dot(pb, v, preferred_element_type=f32)
        m_scr[...] = m_new.reshape(n_rows, 1)
        l_scr[...] = l_new.reshape(n_rows, 1)
        acc_scr[...] = alpha.reshape(n_rows, 1) * acc_scr[...] + pv

    kt = jnp.concatenate([r[0:LANES, :] for r in pages], axis=1).astype(bf16)
    vt = jnp.concatenate([r[LANES:2 * LANES, :] for r in pages], axis=1).astype(bf16)
    shifted = pltpu.roll(sel_scr[...], (n_bpad - s * (tile // SEL_BLOCK)) % n_bpad, axis=1)
    picked = jnp.dot(shifted[:, 0:LANES].astype(bf16), exp_scr[...], preferred_element_type=f32)
    online_update(jnp.dot(qall, kt, preferred_element_type=f32), grouped(picked) > 0.5, vt, True)

    @pl.when(s == pl.num_programs(1) - 1)
    def _():
        new_blk = past_len // SEL_BLOCK
        kidx = lax.broadcasted_iota(jnp.int32, (n_rows, NEW_PAD), 1)
        causal = ((past_len + kidx) <= qpos) & (kidx < t_new)
        picked_new = sel_scr[:, new_blk:new_blk + 1]
        mk = (grouped(picked_new) > 0.5) & causal.reshape(g2, r4, t_new, NEW_PAD)
        online_update(_dot_nt(qall, ksn_ref[...]), mk, vsn_ref[...], False)
        o_s = acc_scr[...] * jnp.where(m_scr[...] > 0.5 * NEG, 1.0 / l_scr[...], 0.0)

        n_win = w_buf + NEW_PAD
        kw_t = win_ref[0:LANES, :].astype(bf16)
        vw_t = win_ref[LANES:2 * LANES, :].astype(bf16)
        widx = lax.broadcasted_iota(jnp.int32, (n_rows, n_win), 1)
        wpos = past_len - w_buf + widx
        wmask = (wpos <= qpos) & (wpos > qpos - WINDOW) & (wpos >= 0) & (widx < w_buf + t_new)
        s_w = jnp.concatenate([jnp.dot(qall, kw_t, preferred_element_type=f32), _dot_nt(qall, kwn_ref[...])],
                              axis=1)
        p_w = _softmax_rows(s_w, wmask).astype(bf16)
        o_w = (_dot_nt(p_w[:, 0:w_buf], vw_t)
               + jnp.dot(p_w[:, w_buf:], vwn_ref[...], preferred_element_type=f32))
        gate = gate_ref[...]
        o_ref[...] = gate[:, 0:1] * oc_scr[...] + gate[:, 1:2] * o_s + gate[:, 2:3] * o_w


def nsa_sample(q, gates, kc, vc, pool_t, page_table, ks_new, vs_new, win, kw_new, vw_new):
    f32, bf16 = jnp.float32, jnp.bfloat16
    bsz, t_new = q.shape[0], q.shape[1]
    n_pages = page_table.shape[1]
    pp = PAGES_PER_STEP
    w_buf = win.shape[2]
    r4, g2 = HEADS_PER_GROUP, KV_GROUPS
    n_rows = g2 * r4 * t_new
    past_len = n_pages * PAGE_SIZE
    assert past_len % SEL_BLOCK == 0 and t_new <= SEL_BLOCK and past_len >= w_buf and n_pages % pp == 0
    n_sel = past_len // SEL_BLOCK + 1
    n_bpad = -(-n_sel // LANES) * LANES
    eye = jnp.eye(g2, dtype=f32)
    qg = q.reshape(bsz, t_new, g2, r4, HEAD_DIM).transpose(0, 2, 3, 1, 4) * QK_SCALE
    qall = jnp.einsum('bgrqd,gk->bgrqkd', qg, eye).reshape(bsz, n_rows, LANES).astype(bf16)
    gall = gates.reshape(bsz, t_new, g2, r4, 3).transpose(0, 2, 3, 1, 4).reshape(bsz, n_rows, 3)
    pad_rows = lambda x: jnp.pad(x, ((0, 0), (0, NEW_PAD - t_new), (0, 0))).astype(bf16)
    n_cmp = kc.shape[1]
    grid_spec = pltpu.PrefetchScalarGridSpec(
        num_scalar_prefetch=1,
        grid=(bsz, n_pages // pp),
        in_specs=[_per_seq((n_rows, LANES)), _per_seq((n_rows, 3)), _per_seq((n_cmp, LANES)),
                  _per_seq((n_cmp, LANES))]
                 + [_page_spec(i, 1) for i in range(pp)]
                 + [_per_seq((NEW_PAD, LANES)), _per_seq((NEW_PAD, LANES)), _per_seq((2 * LANES, w_buf)),
                    _per_seq((NEW_PAD, LANES)), _per_seq((NEW_PAD, LANES))],
        out_specs=_per_seq((n_rows, LANES)),
        scratch_shapes=[pltpu.VMEM((g2 * t_new, n_bpad), f32), pltpu.VMEM((LANES, pp * PAGE_SIZE), bf16),
                        pltpu.VMEM((n_rows, LANES), f32),
                        pltpu.VMEM((n_rows, 1), f32), pltpu.VMEM((n_rows, 1), f32),
                        pltpu.VMEM((n_rows, LANES), f32)],
    )
    o = pl.pallas_call(
        functools.partial(_nsa_sample_kernel, n_pages=n_pages, t_new=t_new, w_buf=w_buf),
        grid_spec=grid_spec,
        out_shape=jax.ShapeDtypeStruct((bsz, n_rows, LANES), f32),
        compiler_params=_cparams("parallel", "arbitrary"),
        name="nsa_sample",
    )(page_table, qall, gall, kc, vc, *([pool_t] * pp), pad_rows(ks_new), pad_rows(vs_new), win,
      pad_rows(kw_new), pad_rows(vw_new))
    o = jnp.einsum('bgrqkd,gk->bqgrd', o.reshape(bsz, g2, r4, t_new, g2, HEAD_DIM), eye)
    return o.reshape(bsz, t_new, NSA_Q)


def _ssd_kernel(x_ref, b_ref, c_ref, dt_ref, a_ref, za_ref, zb_ref, dskip_ref, ng_ref, y_ref, hout_ref, h_scr, *,
                chunk):
    f32, bf16 = jnp.float32, jnp.bfloat16
    n_l = chunk
    hpg = SSD_HEADS // SSD_GROUPS
    gw = hpg * SSD_HEAD_DIM
    j = pl.program_id(1)

    @pl.when(j == 0)
    def _():
        h_scr[...] = jnp.zeros(h_scr.shape, f32)

    x = x_ref[...]
    dt = dt_ref[...]
    tri_b = (lax.broadcasted_iota(jnp.int32, (n_l, n_l), 0) >= lax.broadcasted_iota(jnp.int32, (n_l, n_l), 1))
    tri = tri_b.astype(bf16)
    cum = sum(jnp.dot(tri, part, preferred_element_type=f32) for part in _split3(dt * a_ref[...]))
    cum_t = cum.T
    dt_t = dt.T
    ecum = jnp.exp(cum)
    clast = cum[n_l - 1:n_l, :]
    wt = jnp.exp(clast - cum) * dt
    elast = jnp.exp(clast)
    lane = lax.broadcasted_iota(jnp.int32, (n_l, LANES), 1)
    low = lane < SSD_HEAD_DIM

    def pair(v, h0):
        return jnp.where(low[:v.shape[0]], v[:, h0:h0 + 1], v[:, h0 + 1:h0 + 2])

    tiles = []
    for g in range(SSD_GROUPS):
        bg = b_ref[:, g * SSD_STATE:(g + 1) * SSD_STATE]
        cgb = c_ref[:, g * SSD_STATE:(g + 1) * SSD_STATE].astype(bf16)
        bgt = bg.T.astype(bf16)
        cb = jnp.dot(cgb, bgt, preferred_element_type=f32)
        hg = h_scr[g]
        y_inter = jnp.dot(cgb, hg.astype(bf16), preferred_element_type=f32)
        xw, dec = [], []
        for pr in range(hpg // 2):
            h0 = hpg * g + 2 * pr
            xt = x[:, (h0 // 2) * LANES:(h0 // 2 + 1) * LANES]
            acc = None
            for k in range(2):
                h = h0 + k
                seg = cum[:, h:h + 1] - cum_t[h:h + 1, :]
                w = cb * jnp.exp(jnp.where(tri_b, seg, NEG)) * dt_t[h:h + 1, :]
                xm = jnp.where(low if k == 0 else jnp.logical_not(low), xt, 0.0).astype(bf16)
                part = jnp.dot(w.astype(bf16), xm, preferred_element_type=f32)
                acc = part if acc is None else acc + part
            tiles.append(acc + y_inter[:, pr * LANES:(pr + 1) * LANES] * pair(ecum, h0))
            xw.append((xt * pair(wt, h0)).astype(bf16))
            dec.append(pair(elast, h0))
        h_scr[g] = (hg * jnp.concatenate(dec, axis=1)
                    + jnp.dot(bgt, jnp.concatenate(xw, axis=1), preferred_element_type=f32))
    y = jnp.concatenate(tiles, axis=1) + dskip_ref[...] * x
    zg = jnp.concatenate([za_ref[...], zb_ref[...]], axis=1)
    v = y * (zg * jax.nn.sigmoid(zg))
    outs = []
    for g in range(SSD_GROUPS):
        vg = v[:, g * gw:(g + 1) * gw]
        outs.append(vg * lax.rsqrt(jnp.mean(vg * vg, axis=-1, keepdims=True) + RMS_EPS))
    y_ref[...] = jnp.concatenate(outs, axis=1) * ng_ref[...]

    @pl.when(j == pl.num_programs(1) - 1)
    def _():
        hout_ref[...] = h_scr[...]


def ssd_prompt(xbc, dt, a, z, zg_offset, d_skip, norm_g):
    f32 = jnp.float32
    bsz, t, _ = xbc.shape
    hpg = SSD_HEADS // SSD_GROUPS
    gn = SSD_GROUPS * SSD_STATE
    half = SSD_INNER // 2
    assert zg_offset % half == 0
    dt_p = jnp.pad(dt, ((0, 0), (0, 0), (0, LANES - SSD_HEADS)))
    a_p = jnp.pad(a.astype(f32), (0, LANES - SSD_HEADS)).reshape(1, LANES)
    dsk = jnp.repeat(d_skip.astype(f32), SSD_HEAD_DIM).reshape(1, SSD_INNER)
    blk = lambda w, c: pl.BlockSpec((None, SSD_CHUNK, w), lambda b, j: (b, j, c))
    full = lambda shape: pl.BlockSpec(shape, lambda b, j: (0,) * len(shape))
    state_spec = pl.BlockSpec((None, SSD_GROUPS, SSD_STATE, hpg * SSD_HEAD_DIM), lambda b, j: (b, 0, 0, 0))
    y, h = pl.pallas_call(
        functools.partial(_ssd_kernel, chunk=SSD_CHUNK),
        grid=(bsz, t // SSD_CHUNK),
        in_specs=[blk(SSD_INNER, 0), blk(gn, SSD_INNER // gn), blk(gn, SSD_INNER // gn + 1), blk(LANES, 0),
                  full((1, LANES)), blk(half, zg_offset // half), blk(half, zg_offset // half + 1),
                  full((1, SSD_INNER)), full((1, SSD_INNER))],
        out_specs=[blk(SSD_INNER, 0), state_spec],
        out_shape=[jax.ShapeDtypeStruct((bsz, t, SSD_INNER), f32),
                   jax.ShapeDtypeStruct((bsz, SSD_GROUPS, SSD_STATE, hpg * SSD_HEAD_DIM), f32)],
        scratch_shapes=[pltpu.VMEM((SSD_GROUPS, SSD_STATE, hpg * SSD_HEAD_DIM), f32)],
        compiler_params=_cparams("parallel", "arbitrary"),
        name="ssd_prompt",
    )(xbc, xbc, xbc, dt_p, a_p, z, z, dsk, norm_g.astype(f32).reshape(1, SSD_INNER))
    h = h.reshape(bsz, SSD_GROUPS, SSD_STATE, hpg, SSD_HEAD_DIM).transpose(0, 1, 3, 4, 2)
    return y, h.reshape(bsz, SSD_HEADS, SSD_HEAD_DIM, SSD_STATE)


def _inproj_even_kernel(x_ref, w_ref, c_ref, sa_ref, sb_ref,
                        u_ref, q_ref, rows_ref, kvw_ref, kvsb_ref, kvwb_ref, g_ref):
    bf16 = jnp.bfloat16
    z = jnp.dot(x_ref[...].astype(bf16), w_ref[...], preferred_element_type=jnp.float32)
    cos, s_up, s_down = c_ref[...], sa_ref[...], sb_ref[...]

    def rot(t):
        return (t * cos + pltpu.roll(t, LANES - ROT_DIM // 2, axis=1) * s_up
                + pltpu.roll(t, ROT_DIM // 2, axis=1) * s_down)

    tile = lambda k: z[:, k * LANES:(k + 1) * LANES]
    q0 = S5_DIM // LANES
    kv0 = q0 + NSA_Q // LANES
    u_ref[...] = z[:, 0:S5_DIM]
    q_ref[...] = jnp.concatenate([rot(tile(q0 + k)) for k in range(NSA_Q // LANES)], axis=1)
    kc, vc, ks, vs, kw, vw = (rot(tile(kv0)), tile(kv0 + 1), rot(tile(kv0 + 2)), tile(kv0 + 3),
                              rot(tile(kv0 + 4)), tile(kv0 + 5))
    rows_ref[...] = jnp.concatenate([kc, vc, ks, vs], axis=1)
    kvw = jnp.concatenate([kw, vw], axis=1)
    kvw_ref[...] = kvw
    kvwb_ref[...] = kvw.astype(bf16)
    kvsb_ref[...] = jnp.concatenate([ks, vs], axis=1).astype(bf16)
    g0 = (kv0 + 6) * LANES
    g_ref[...] = jax.nn.sigmoid(z[:, g0:g0 + 3 * N_HEADS])


def rope_tables(pos):
    half = ROT_DIM // 2
    inv = ROPE_THETA ** (-jnp.arange(half, dtype=jnp.float32) * 2.0 / ROT_DIM)
    ang = pos.astype(jnp.float32)[:, None] * inv[None, :]
    d = jnp.arange(LANES) % HEAD_DIM
    cos = jnp.take(jnp.cos(ang), d % half, axis=1)
    sin = jnp.take(jnp.sin(ang), d % half, axis=1)
    return (jnp.where(d < ROT_DIM, cos, 1.0), jnp.where(d < half, -sin, 0.0),
            jnp.where((d >= half) & (d < ROT_DIM), sin, 0.0))


def inproj_even(h, w_bf16, pos):
    f32, bf16 = jnp.float32, jnp.bfloat16
    n, d = h.shape
    tile = min(ROW_TILE, n)
    row = lambda w: pl.BlockSpec((tile, w), lambda i: (i, 0))
    fixed = lambda shape: pl.BlockSpec(shape, lambda i: (0, 0), pipeline_mode=pl.Buffered(1))
    widths = [(S5_DIM, f32), (NSA_Q, f32), (2 * NSA_KV, f32), (NSA_KV, f32), (NSA_KV, bf16), (NSA_KV, bf16),
              (3 * N_HEADS, f32)]
    return pl.pallas_call(
        _inproj_even_kernel,
        grid=(pl.cdiv(n, tile),),
        in_specs=[row(d), fixed(w_bf16.shape), row(LANES), row(LANES), row(LANES)],
        out_specs=[row(w) for w, _ in widths],
        out_shape=[jax.ShapeDtypeStruct((n, w), dt) for w, dt in widths],
        compiler_params=_cparams("parallel"),
        name="inproj_even",
    )(h, w_bf16, *rope_tables(pos))


def _causal_conv_tile(x, tail, w_ref, b_ref, width):
    row = lax.broadcasted_iota(jnp.int32, (TAIL, x.shape[1]), 0)
    acc = b_ref[...] + w_ref[width - 1:width, :] * x
    for k in range(1, width):
        xs = pltpu.roll(x, k, axis=0)
        head = jnp.where(row < k, pltpu.roll(tail, k, axis=0), xs[0:TAIL])
        xs = jnp.concatenate([head, xs[TAIL:]], axis=0)
        acc = acc + w_ref[width - 1 - k:width - k, :] * xs
    return acc


def _inproj_odd_kernel(x_ref, w_ref, scw_ref, scb_ref, cvw_ref, cvb_ref, dtb_ref,
                       ysc_ref, xbc_ref, dt_ref, zg_ref, tsc_ref, tx_ref, tail_sc, tail_x, *, tiles_per_seq):
    @pl.when(pl.program_id(0) % tiles_per_seq == 0)
    def _():
        tail_sc[...] = jnp.zeros(tail_sc.shape, jnp.float32)
        tail_x[...] = jnp.zeros(tail_x.shape, jnp.float32)

    z = jnp.dot(x_ref[...].astype(jnp.bfloat16), w_ref[...], preferred_element_type=jnp.float32)
    o_zg = 3 * SC_DIM
    o_x = o_zg + SSD_INNER
    o_dt = o_x + SSD_CONV_DIM
    n = z.shape[0]
    prod = z[:, 2 * SC_DIM:3 * SC_DIM] * z[:, 0:SC_DIM]
    ysc_ref[...] = z[:, SC_DIM:2 * SC_DIM] * _causal_conv_tile(prod, tail_sc[...], scw_ref, scb_ref, SC_WIDTH)
    xbc = z[:, o_x:o_dt]
    c = _causal_conv_tile(xbc, tail_x[...], cvw_ref, cvb_ref, SSD_CONV)
    xbc_ref[...] = c * jax.nn.sigmoid(c)
    dt_ref[...] = jax.nn.softplus(z[:, o_dt:o_dt + SSD_HEADS] + dtb_ref[...])
    zg_ref[...] = z[:, o_zg:o_x]
    tail_sc[...] = prod[n - TAIL:n]
    tail_x[...] = xbc[n - TAIL:n]
    tsc_ref[...] = prod[n - TAIL:n]
    tx_ref[...] = xbc[n - TAIL:n]


def inproj_odd_prompt(h, w_bf16, bsz, sc_w, sc_b, cv_w, cv_b, dt_bias):
    f32 = jnp.float32
    n_rows, d = h.shape
    t = n_rows // bsz
    assert t % ROW_TILE == 0
    tps = t // ROW_TILE
    row = lambda w: pl.BlockSpec((ROW_TILE, w), lambda i: (i, 0))
    fixed = lambda shape: pl.BlockSpec(shape, lambda i: (0,) * len(shape), pipeline_mode=pl.Buffered(1))
    last = lambda w: pl.BlockSpec((None, TAIL, w), lambda i: (i // tps, 0, 0))
    ysc, xbc, dt, zg, tsc, tx = pl.pallas_call(
        functools.partial(_inproj_odd_kernel, tiles_per_seq=tps),
        grid=(n_rows // ROW_TILE,),
        in_specs=[row(d), fixed(w_bf16.shape), fixed((SC_WIDTH, SC_DIM)), fixed((1, SC_DIM)),
                  fixed((SSD_CONV, SSD_CONV_DIM)), fixed((1, SSD_CONV_DIM)), fixed((1, SSD_HEADS))],
        out_specs=[row(SC_DIM), row(SSD_CONV_DIM), row(SSD_HEADS), row(SSD_INNER), last(SC_DIM), last(SSD_CONV_DIM)],
        out_shape=[jax.ShapeDtypeStruct((n_rows, SC_DIM), f32), jax.ShapeDtypeStruct((n_rows, SSD_CONV_DIM), f32),
                   jax.ShapeDtypeStruct((n_rows, SSD_HEADS), f32), jax.ShapeDtypeStruct((n_rows, SSD_INNER), f32),
                   jax.ShapeDtypeStruct((bsz, TAIL, SC_DIM), f32), jax.ShapeDtypeStruct((bsz, TAIL, SSD_CONV_DIM), f32)],
        scratch_shapes=[pltpu.VMEM((TAIL, SC_DIM), f32), pltpu.VMEM((TAIL, SSD_CONV_DIM), f32)],
        compiler_params=_cparams("arbitrary"),
        name="inproj_odd",
    )(h, w_bf16, sc_w.astype(f32), sc_b.astype(f32).reshape(1, SC_DIM), cv_w.astype(f32),
      cv_b.astype(f32).reshape(1, SSD_CONV_DIM), dt_bias.astype(f32).reshape(1, SSD_HEADS))
    seq = lambda a: a.reshape(bsz, t, a.shape[-1])
    return (seq(ysc), seq(xbc), seq(dt), seq(zg),
            tsc[:, TAIL - (SC_WIDTH - 1):], tx[:, TAIL - (SSD_CONV - 1):])


def layer_norm(x, g, b):
    mu = jnp.mean(x, -1, keepdims=True)
    xc = x - mu
    var = jnp.mean(xc * xc, -1, keepdims=True)
    return xc * lax.rsqrt(var + LN_EPS) * g + b


def last_rows(x, n):
    t = x.shape[1]
    if t < n:
        x = jnp.pad(x, [(0, 0), (n - t, 0)] + [(0, 0)] * (x.ndim - 2))
    return x[:, x.shape[1] - n:]


def causal_conv(x, buf, w, b):
    t = x.shape[1]
    width = w.shape[0]
    xp = jnp.concatenate([buf, x], axis=1)
    y = b + sum(xp[:, j:j + t] * w[j] for j in range(width))
    return y, xp[:, xp.shape[1] - (width - 1):]


def even_prompt_mix(h, w_in_bf16, bt, s5p, cmpp, w_buf):
    t = h.shape[0] // bt
    u, q, rows, kvw, kvs_b, kvw_b, gates = inproj_even(h, w_in_bf16, jnp.arange(h.shape[0]) % t)
    seq = lambda a: a.reshape(bt, t, a.shape[-1])
    feat = KV_GROUPS * HEAD_DIM
    y_s5, s5_state = s5_scan(seq(u), jnp.zeros((bt, S5_GROUPS, S5_STATE, 2), jnp.float32), s5p, S5_CHUNK)
    rows = seq(rows)
    kc = compress_prompt(rows[..., 0:feat], compress_params(cmpp[0], cmpp[1], cmpp[2]))
    vc = compress_prompt(rows[..., feat:2 * feat], compress_params(cmpp[3], cmpp[4], cmpp[5]))
    y_nsa = nsa_prompt(seq(q), seq(gates), kc, vc, seq(kvs_b), jnp.pad(seq(kvw_b), ((0, 0), (WINDOW, 0), (0, 0))))
    new_rows = rows.reshape(bt, t, 4, KV_GROUPS, HEAD_DIM)
    return (y_s5, y_nsa), s5_state, new_rows, last_rows(seq(kvw).reshape(bt, t, 2, KV_GROUPS, HEAD_DIM), w_buf)


def even_sample_mix(h, w_in_bf16, bt, s5_h0, pool, page_table, win_buf, s5p, cmpp):
    f32 = jnp.float32
    t = h.shape[0] // bt
    pos = page_table.shape[1] * PAGE_SIZE + jnp.arange(h.shape[0]) % t
    u, q, rows, kvw, _, _, gates = inproj_even(h, w_in_bf16, pos)
    seq = lambda a: a.reshape(bt, t, a.shape[-1])
    feat = KV_GROUPS * HEAD_DIM
    y_s5, s5_state = s5_scan(seq(u), s5_h0.astype(f32), s5p, t)
    rows, kvw = seq(rows), seq(kvw)
    pool_t = pool.astype(f32).transpose(0, 2, 3, 4, 1).reshape(pool.shape[0], 4 * feat, PAGE_SIZE)
    kc, vc = compress_sample(pool_t, page_table, rows[..., 0:feat], rows[..., feat:2 * feat],
                             compress_sample_params(cmpp[0], cmpp[1], cmpp[2]),
                             compress_sample_params(cmpp[3], cmpp[4], cmpp[5]))
    w_buf = win_buf.shape[1]
    win_f = win_buf.astype(f32)
    y_nsa = nsa_sample(q.reshape(bt, t, N_HEADS, HEAD_DIM), gates.reshape(bt, t, N_HEADS, 3), kc, vc, pool_t,
                       page_table, rows[..., 2 * feat:3 * feat], rows[..., 3 * feat:4 * feat],
                       win_f.transpose(0, 2, 3, 4, 1).reshape(bt, 2 * feat, w_buf), kvw[..., 0:feat],
                       kvw[..., feat:2 * feat])
    new_rows = rows.reshape(bt, t, 4, KV_GROUPS, HEAD_DIM)
    win = jnp.concatenate([win_f, kvw.reshape(bt, t, 2, KV_GROUPS, HEAD_DIM)], axis=1)
    return (y_s5, y_nsa), s5_state, new_rows, win[:, t:]


def ssd_scan(x, dt, a, bm, cm, h0, chunk):
    bt, t, nh, p = x.shape
    nch = t // chunk
    r = nh // SSD_GROUPS
    tri = jnp.arange(chunk)[:, None] >= jnp.arange(chunk)[None, :]

    def to_chunks(v):
        return jnp.moveaxis(v.reshape((bt, nch, chunk) + v.shape[2:]), 1, 0)

    def step(h, inp):
        xc, dtc, bc, cc = inp
        cum = jnp.cumsum(dtc * a, axis=1)
        seg = cum[:, :, None, :] - cum[:, None, :, :]
        decay = jnp.exp(jnp.where(tri[None, :, :, None], seg, NEG)).reshape(bt, chunk, chunk, SSD_GROUPS, r)
        cb = jnp.einsum('btgn,bsgn->btsg', cc, bc)
        xg = xc.reshape(bt, chunk, SSD_GROUPS, r, p)
        dg = dtc.reshape(bt, chunk, SSD_GROUPS, r)
        w = cb[..., None] * decay * dg[:, None]
        y_intra = jnp.einsum('btsgr,bsgrp->btgrp', w, xg)
        hg = h.reshape(bt, SSD_GROUPS, r, p, SSD_STATE)
        y_inter = jnp.einsum('btgn,bgrpn->btgrp', cc, hg) * jnp.exp(cum).reshape(bt, chunk, SSD_GROUPS, r)[..., None]
        wt = (jnp.exp(cum[:, -1:, :] - cum) * dtc).reshape(bt, chunk, SSD_GROUPS, r)
        h_new = (hg * jnp.exp(cum[:, -1]).reshape(bt, SSD_GROUPS, r)[..., None, None]
                 + jnp.einsum('bsgr,bsgrp,bsgn->bgrpn', wt, xg, bc))
        return h_new.reshape(bt, nh, p, SSD_STATE), (y_intra + y_inter).reshape(bt, chunk, nh, p)

    h_fin, ys = lax.scan(step, h0, (to_chunks(x), to_chunks(dt), to_chunks(bm), to_chunks(cm)))
    return jnp.moveaxis(ys, 0, 1).reshape(bt, t, nh, p), h_fin


def gated_rmsnorm(y, z, g):
    v = y * jax.nn.silu(z)
    bt, t, _ = v.shape
    vg = v.reshape(bt, t, SSD_GROUPS, SSD_INNER // SSD_GROUPS)
    vg = vg * lax.rsqrt(jnp.mean(vg * vg, -1, keepdims=True) + RMS_EPS)
    return vg.reshape(bt, t, SSD_INNER) * g


def odd_prompt_mix(h, w_in_bf16, bsz, sc_w, sc_b, cv_w, cv_b, dt_bias, a_log, d_skip, norm_g):
    a = -jnp.exp(a_log.astype(jnp.float32))
    y_sc, xbc_c, dt, zg, new_sc, new_conv = inproj_odd_prompt(h, w_in_bf16, bsz, sc_w, sc_b, cv_w, cv_b, dt_bias)
    y, h_new = ssd_prompt(xbc_c, dt, a, zg, 0, d_skip, norm_g)
    return (y_sc, y), new_sc, new_conv, h_new


def odd_mix(z, sc_buf, conv_buf, h0, chunk, sc_w, sc_b, cv_w, cv_b, dt_bias, a_log, d_skip, norm_g):
    f32 = jnp.float32
    bt, t, _ = z.shape
    a = -jnp.exp(a_log.astype(f32))
    o1 = SC_DIM
    o2 = 2 * SC_DIM
    o3 = 3 * SC_DIM
    o4 = o3 + SSD_INNER
    o5 = o4 + SSD_CONV_DIM
    sc_h = z[..., :o1]
    sc_bg = z[..., o1:o2]
    sc_cg = z[..., o2:o3]
    zg = z[..., o3:o4]
    xbc = z[..., o4:o5]
    dt_raw = z[..., o5:]
    conv_sc, new_sc = causal_conv(sc_cg * sc_h, sc_buf.astype(f32), sc_w, sc_b)
    y_sc = sc_bg * conv_sc
    xbc_c, new_conv = causal_conv(xbc, conv_buf.astype(f32), cv_w, cv_b)
    xbc_c = jax.nn.silu(xbc_c)
    gn = SSD_GROUPS * SSD_STATE
    xs = xbc_c[..., :SSD_INNER].reshape(bt, t, SSD_HEADS, SSD_HEAD_DIM)
    bm = xbc_c[..., SSD_INNER:SSD_INNER + gn].reshape(bt, t, SSD_GROUPS, SSD_STATE)
    cm = xbc_c[..., SSD_INNER + gn:].reshape(bt, t, SSD_GROUPS, SSD_STATE)
    dt = jax.nn.softplus((dt_raw + dt_bias).astype(f32))
    y, h_new = ssd_scan(xs, dt, a, bm, cm, h0.astype(f32), chunk)
    y = (y + d_skip[:, None] * xs).reshape(bt, t, SSD_INNER)
    y = gated_rmsnorm(y, zg, norm_g)
    return (y_sc, y), new_sc, new_conv, h_new


def moe_ffn(x, w_r, b_r, w_gu_bf16, w_down_bf16):
    n, d = x.shape
    logits = jnp.dot(x, w_r, precision=lax.Precision.HIGHEST) + b_r
    top_v, top_i = lax.top_k(logits, TOP_K)
    gate = jax.nn.softmax(top_v, axis=-1)
    flat_e = top_i.reshape(-1)
    blk = 128
    assert (TOP_K * n) % blk == 0
    onehot = jax.nn.one_hot(flat_e, N_EXPERTS, dtype=jnp.float32).reshape(-1, blk, N_EXPERTS)
    tri = (jnp.arange(blk)[:, None] >= jnp.arange(blk)[None, :]).astype(jnp.float32)
    local = jnp.einsum('ij,bjk->bik', tri, onehot)
    block_total = local[:, -1, :]
    block_off = jnp.cumsum(block_total, axis=0) - block_total
    incl = (local + block_off[:, None, :]).reshape(-1, N_EXPERTS)
    rank = jnp.take_along_axis(incl, flat_e[:, None], axis=1)[:, 0].astype(jnp.int32) - 1
    counts = jnp.sum(block_total, axis=0).astype(jnp.int32)
    padded = ((counts + ROW_TILE - 1) // ROW_TILE) * ROW_TILE
    pad_start = jnp.cumsum(padded) - padded
    dest = (pad_start[flat_e] + rank).astype(jnp.int32)
    n_tiles = (TOP_K * n) // ROW_TILE + N_EXPERTS
    rows = n_tiles * ROW_TILE
    row_token = jnp.zeros((rows,), jnp.int32).at[dest].set(jnp.arange(TOP_K * n, dtype=jnp.int32) // TOP_K,
                                                           unique_indices=True, mode='promise_in_bounds')
    tile_end = jnp.cumsum(padded) // ROW_TILE
    tile_expert = jnp.minimum(jnp.searchsorted(tile_end, jnp.arange(n_tiles), side='right'),
                              N_EXPERTS - 1).astype(jnp.int32)
    n_used = tile_end[-1:].astype(jnp.int32)
    xs = x.at[row_token].get(mode='promise_in_bounds')
    ys = grouped_ffn(xs, w_gu_bf16, w_down_bf16, tile_expert, n_used)
    dest = dest.reshape(n, TOP_K)
    y0 = ys.at[dest[:, 0]].get(mode='promise_in_bounds')
    y1 = ys.at[dest[:, 1]].get(mode='promise_in_bounds')
    return gate[:, 0:1] * y0 + gate[:, 1:2] * y1


def kernel(x_prompt, x_sample, state_s5, cache_nsa_kv, state_win_kv, state_sc_conv, state_ssd_conv, state_ssd,
           page_table, ln_g, ln_b, w_in_even, s5_lam_re, s5_lam_im, s5_log_dt, s5_b, s5_c, s5_d, s5_w_glu,
           nsa_wk1, nsa_wk2, nsa_pe_k, nsa_wv1, nsa_wv2, nsa_pe_v, w_out_even, ffn_w_gu, ffn_w_down,
           w_in_odd, sc_conv_w, sc_conv_b, ssd_conv_w, ssd_conv_b, ssd_dt_bias, ssd_a_log, ssd_d, ssd_norm_g,
           w_out_odd, moe_router, moe_router_b, moe_w_gu, moe_w_down):
    f32 = jnp.float32
    bf16 = jnp.bfloat16
    bp, tp, d = x_prompt.shape
    bs, ts, _ = x_sample.shape
    n_p = bp * tp
    n_s = bs * ts
    w_buf = state_win_kv.shape[2]
    streams = [x_prompt.astype(f32).reshape(n_p, d), x_sample.astype(f32).reshape(n_s, d)]

    def flat(parts, n_rows):
        return [p.reshape(n_rows, p.shape[-1]) for p in parts]

    def out_proj(parts, w_out, width, h, g, b):
        w = w_out.astype(bf16)
        return matmul(flat(parts, h.shape[0]), [w[:width], w[width:]], ln=(h, g, b))

    def single_expert(n_rows):
        n_tiles = pl.cdiv(n_rows, min(ROW_TILE, n_rows))
        return jnp.zeros((n_tiles,), jnp.int32), jnp.full((1,), n_tiles, jnp.int32)

    s5p = s5_params(s5_lam_re[0], s5_lam_im[0], s5_log_dt[0], s5_b[0], s5_c[0], s5_d[0], s5_w_glu[0])
    cmpp = (nsa_wk1[0], nsa_wk2[0], nsa_pe_k[0], nsa_wv1[0], nsa_wv2[0], nsa_pe_v[0])
    w_in = w_in_even[0].astype(bf16)
    mix_p, s5_p, kv_p, win_p = even_prompt_mix(streams[0], w_in, bp, s5p, cmpp, w_buf)
    mix_s, s5_s, kv_s, win_s = even_sample_mix(streams[1], w_in, bs, state_s5[0], cache_nsa_kv[0], page_table,
                                               state_win_kv[0], s5p, cmpp)
    streams = [out_proj(mix, w_out_even[0], S5_DIM, h, ln_g[0, 0], ln_b[0, 0])
               for mix, h in zip((mix_p, mix_s), streams)]
    w_gu, w_down = to_bf16(ffn_w_gu), to_bf16(ffn_w_down)
    streams = [grouped_ffn(h, w_gu, w_down, *single_expert(h.shape[0]), ln=(ln_g[0, 1], ln_b[0, 1]))
               for h in streams]

    oddp = (sc_conv_w[0], sc_conv_b[0], ssd_conv_w[0], ssd_conv_b[0], ssd_dt_bias[0],
            ssd_a_log[0], ssd_d[0], ssd_norm_g[0])
    w_in = w_in_odd[0].astype(bf16)
    mix_p, scc_p, sdc_p, ssd_p = odd_prompt_mix(streams[0], w_in, bp, *oddp)
    zs = matmul([streams[1]], [w_in]).reshape(bs, ts, -1)
    mix_s, scc_s, sdc_s, ssd_s = odd_mix(zs, state_sc_conv[0], state_ssd_conv[0], state_ssd[0], ts, *oddp)
    streams = [out_proj(mix, w_out_odd[0], SC_DIM, h, ln_g[1, 0], ln_b[1, 0])
               for mix, h in zip((mix_p, mix_s), streams)]
    h = jnp.concatenate(streams, axis=0)
    f = moe_ffn(h, moe_router[0], moe_router_b[0], to_bf16(moe_w_gu[0]), to_bf16(moe_w_down[0]))
    h = layer_norm(ALPHA * h + f, ln_g[1, 1], ln_b[1, 1])

    hp = h[:n_p].reshape(bp, tp, d)
    hs = h[n_p:].reshape(bs, ts, d)
    st = lambda a, ref: a[None].astype(ref.dtype)
    return (hp.astype(x_prompt.dtype), hs.astype(x_sample.dtype),
            st(s5_p, state_s5), st(s5_s, state_s5),
            st(kv_p, cache_nsa_kv), st(kv_s, cache_nsa_kv),
            st(win_p, state_win_kv), st(win_s, state_win_kv),
            st(scc_p, state_sc_conv), st(scc_s, state_sc_conv),
            st(sdc_p, state_ssd_conv), st(sdc_s, state_ssd_conv),
            st(ssd_p, state_ssd), st(ssd_s, state_ssd))
```

```python
import functools
import math

import jax
import jax.numpy as jnp
from jax import lax
from jax.experimental import pallas as pl
from jax.experimental.pallas import tpu as pltpu

D_MODEL = 1024
DEPTH = 2
ALPHA = (2.0 * DEPTH) ** 0.25
LN_EPS = 1e-5
RMS_EPS = 1e-5
NEG = -1e30

S5_DIM = D_MODEL // 2
S5_GROUP = 16
S5_GROUPS = S5_DIM // S5_GROUP
S5_STATE = 64

HEAD_DIM = 64
N_HEADS = (D_MODEL // 2) // HEAD_DIM
KV_GROUPS = 2
HEADS_PER_GROUP = N_HEADS // KV_GROUPS
CMP_STRIDE = 16
CMP_LEN = 2 * CMP_STRIDE
SEL_BLOCK = 64
N_SEL = 16
WINDOW = 512
Q_BLOCK = 128
ROPE_THETA = 500000.0
ROT_DIM = HEAD_DIM // 4
FORCE = 1e4
NSA_Q = N_HEADS * HEAD_DIM
NSA_KV = 2 * KV_GROUPS * HEAD_DIM

SC_DIM = D_MODEL // 2
SC_WIDTH = 3
SSD_HEAD_DIM = 64
SSD_HEADS = 16
SSD_INNER = SSD_HEADS * SSD_HEAD_DIM
SSD_GROUPS = 4
SSD_STATE = 128
SSD_CONV = 4
SSD_CONV_DIM = SSD_INNER + 2 * SSD_GROUPS * SSD_STATE
SSD_CHUNK = 128

D_FF = 2816
N_EXPERTS = 8
TOP_K = 2

VMEM_LIMIT_BYTES = 56 * 1024 * 1024
LANES = 128
S5_N = S5_GROUPS * S5_STATE
S5_LT = S5_N // LANES
S5_CHUNK = 256
SEL_TILE = 1024
QK_SCALE = HEAD_DIM ** -0.5 * math.log2(math.e)
REMOVED = -3e38
PAGE_SIZE = 128
PAGES_PER_STEP = 64
NEW_PAD = 128
CAST_ROWS = 512
CAST_SPLIT = 4
TAIL = 8
ROW_TILE = 512
FF_TILE = D_FF // 2


def _cparams(*sem):
    return pltpu.CompilerParams(dimension_semantics=sem, vmem_limit_bytes=VMEM_LIMIT_BYTES)


def _deepnorm(resid, update, g, b):
    y = ALPHA * resid + update
    mu = jnp.mean(y, axis=-1, keepdims=True)
    yc = y - mu
    var = jnp.mean(yc * yc, axis=-1, keepdims=True)
    return yc * lax.rsqrt(var + LN_EPS) * g + b


def _mm_kernel(*refs, n_in, fuse_ln, router):
    xs, ws = refs[0:n_in], refs[n_in:2 * n_in]
    o_ref = refs[-2] if router else refs[-1]
    acc = None
    for x_ref, w_ref in zip(xs, ws):
        part = jnp.dot(x_ref[...].astype(jnp.bfloat16), w_ref[...], preferred_element_type=jnp.float32)
        acc = part if acc is None else acc + part
    if fuse_ln:
        r_ref, g_ref, b_ref = refs[2 * n_in:2 * n_in + 3]
        acc = _deepnorm(r_ref[...], acc, g_ref[...], b_ref[...])
    o_ref[...] = acc
    if router:
        wr_ref, br_ref = refs[-4], refs[-3]
        xparts = _split3(acc)
        logits = br_ref[...]
        for i in range(3):
            for j in range(3 - i):
                logits = logits + jnp.dot(xparts[i], wr_ref[j], preferred_element_type=jnp.float32)
        refs[-1][...] = logits


def matmul(xs, ws_bf16, ln=None, router=None):
    m = xs[0].shape[0]
    n = ws_bf16[0].shape[1]
    tile = min(ROW_TILE, m)
    row = lambda width: pl.BlockSpec((tile, width), lambda i: (i, 0))
    fixed = lambda shape: pl.BlockSpec(shape, lambda i: (0, 0), pipeline_mode=pl.Buffered(1))
    in_specs = [row(x.shape[1]) for x in xs] + [fixed(w.shape) for w in ws_bf16]
    args = list(xs) + list(ws_bf16)
    if ln is not None:
        resid, g, b = ln
        in_specs += [row(n), fixed((1, n)), fixed((1, n))]
        args += [resid, g.reshape(1, n), b.reshape(1, n)]
    out_specs, out_shape = row(n), jax.ShapeDtypeStruct((m, n), jnp.float32)
    if router is not None:
        w_r, b_r = router
        pad = LANES - w_r.shape[1]
        w_parts = jnp.stack(_split3(jnp.pad(w_r.astype(jnp.float32), ((0, 0), (0, pad)))))
        in_specs += [pl.BlockSpec((3, n, LANES), lambda i: (0, 0, 0), pipeline_mode=pl.Buffered(1)),
                     fixed((1, LANES))]
        args += [w_parts, jnp.pad(b_r.astype(jnp.float32), (0, pad)).reshape(1, LANES)]
        out_specs, out_shape = [out_specs, row(LANES)], [out_shape, jax.ShapeDtypeStruct((m, LANES), jnp.float32)]
    return pl.pallas_call(
        functools.partial(_mm_kernel, n_in=len(xs), fuse_ln=ln is not None, router=router is not None),
        grid=(pl.cdiv(m, tile),),
        in_specs=in_specs,
        out_specs=out_specs,
        out_shape=out_shape,
        compiler_params=_cparams("parallel"),
        name="matmul",
    )(*args)


def _combine_kernel(h_ref, y0_ref, y1_ref, gate_ref, g_ref, b_ref, o_ref):
    gate = gate_ref[...]
    f = gate[:, 0:1] * y0_ref[...] + gate[:, 1:2] * y1_ref[...]
    o_ref[...] = _deepnorm(h_ref[...], f, g_ref[...], b_ref[...])


def moe_combine_ln(h, y0, y1, gate, g, b):
    m, n = h.shape
    row = lambda width: pl.BlockSpec((ROW_TILE, width), lambda i: (i, 0))
    fixed = pl.BlockSpec((1, n), lambda i: (0, 0))
    return pl.pallas_call(
        _combine_kernel,
        grid=(pl.cdiv(m, ROW_TILE),),
        in_specs=[row(n), row(n), row(n), row(TOP_K), fixed, fixed],
        out_specs=row(n),
        out_shape=jax.ShapeDtypeStruct((m, n), jnp.float32),
        compiler_params=_cparams("parallel"),
        name="moe_combine_ln",
    )(h, y0, y1, gate, g.reshape(1, n), b.reshape(1, n))


def _cast_kernel(*refs):
    o_ref = refs[-1]
    o_ref[...] = jnp.concatenate([r[...].astype(o_ref.dtype) for r in refs[:-1]], axis=1)


def to_bf16(w):
    shape = w.shape
    w2 = w.reshape(-1, shape[-1])
    rows, cols = w2.shape
    split = CAST_SPLIT if cols % (CAST_SPLIT * LANES) == 0 else 1
    out = pl.pallas_call(
        _cast_kernel,
        grid=(pl.cdiv(rows, CAST_ROWS),),
        in_specs=[pl.BlockSpec((CAST_ROWS, cols // split), lambda i, c=c: (i, c)) for c in range(split)],
        out_specs=pl.BlockSpec((CAST_ROWS, cols), lambda i: (i, 0)),
        out_shape=jax.ShapeDtypeStruct((rows, cols), jnp.bfloat16),
        compiler_params=_cparams("parallel"),
        name="to_bf16",
    )(*([w2] * split))
    return out.reshape(shape)


def _ffn_kernel(te_ref, nt_ref, x_ref, wg_ref, wu_ref, wd_ref, *rest, fuse_ln):
    o_ref = rest[-1]
    t = pl.program_id(0)
    j = pl.program_id(1)

    @pl.when(t < nt_ref[0])
    def _():
        x = x_ref[...].astype(jnp.bfloat16)
        g = jnp.dot(x, wg_ref[...], preferred_element_type=jnp.float32)
        u = jnp.dot(x, wu_ref[...], preferred_element_type=jnp.float32)
        h = (g * jax.nn.sigmoid(g) * u).astype(jnp.bfloat16)
        part = jnp.dot(h, wd_ref[...], preferred_element_type=jnp.float32)

        @pl.when(j == 0)
        def _():
            o_ref[...] = part

        @pl.when(j > 0)
        def _():
            if fuse_ln:
                o_ref[...] = _deepnorm(x_ref[...], o_ref[...] + part, rest[0][...], rest[1][...])
            else:
                o_ref[...] += part

    @pl.when(jnp.logical_and(t >= nt_ref[0], j == 0))
    def _():
        o_ref[...] = jnp.zeros_like(o_ref)


def grouped_ffn(x, w_gu_bf16, w_down_bf16, tile_expert, n_tiles_used, ln=None):
    r, d = x.shape
    nf = D_FF // FF_TILE
    assert nf == 2
    tile = min(ROW_TILE, r)
    n_tiles = pl.cdiv(r, tile)
    in_specs = [
        pl.BlockSpec((tile, d), lambda t, j, te, nt: (t, 0)),
        pl.BlockSpec((None, d, FF_TILE), lambda t, j, te, nt: (te[t], 0, j)),
        pl.BlockSpec((None, d, FF_TILE), lambda t, j, te, nt: (te[t], 0, nf + j)),
        pl.BlockSpec((None, FF_TILE, d), lambda t, j, te, nt: (te[t], j, 0)),
    ]
    args = [tile_expert, n_tiles_used, x, w_gu_bf16, w_gu_bf16, w_down_bf16]
    if ln is not None:
        in_specs += [pl.BlockSpec((1, d), lambda t, j, te, nt: (0, 0))] * 2
        args += [ln[0].reshape(1, d), ln[1].reshape(1, d)]
    grid_spec = pltpu.PrefetchScalarGridSpec(
        num_scalar_prefetch=2,
        grid=(n_tiles, nf),
        in_specs=in_specs,
        out_specs=pl.BlockSpec((tile, d), lambda t, j, te, nt: (t, 0)),
    )
    return pl.pallas_call(
        functools.partial(_ffn_kernel, fuse_ln=ln is not None),
        grid_spec=grid_spec,
        out_shape=jax.ShapeDtypeStruct((r, d), jnp.float32),
        compiler_params=_cparams("parallel", "arbitrary"),
        name="grouped_ffn",
    )(*args)


def _s5_kernel(u_ref, perm_ref, h0r_ref, h0i_ref, ar_ref, ai_ref, bbr_ref, bbi_ref, cr_ref, ci_ref, d_ref, wglu_ref,
               y_ref, hro_ref, hio_ref, bur, bui, sr, si, hr, hi, *, chains, chunk):
    j = pl.program_id(0)

    @pl.when(j == 0)
    def _():
        hr[...] = h0r_ref[...]
        hi[...] = h0i_ref[...]

    rows_n = chains * chunk
    u = u_ref[...].reshape(rows_n, S5_DIM)
    to_tc = perm_ref[...]
    ub = jnp.dot(to_tc, u.astype(jnp.bfloat16), preferred_element_type=jnp.float32).astype(jnp.bfloat16)
    hd, hn = S5_DIM // 2, S5_N // 2

    def b_proj(w_ref):
        return jnp.concatenate([jnp.dot(ub[:, h * hd:(h + 1) * hd], w_ref[h * hd:(h + 1) * hd, h * hn:(h + 1) * hn],
                                        preferred_element_type=jnp.float32) for h in range(2)], axis=1)

    bu_r = b_proj(bbr_ref)
    bu_i = b_proj(bbi_ref)
    for k in range(S5_LT):
        bur[k] = bu_r[:, k * LANES:(k + 1) * LANES]
        bui[k] = bu_i[:, k * LANES:(k + 1) * LANES]
    ar = [jnp.broadcast_to(ar_ref[:, k * LANES:(k + 1) * LANES], (chains, LANES)) for k in range(S5_LT)]
    ai = [jnp.broadcast_to(ai_ref[:, k * LANES:(k + 1) * LANES], (chains, LANES)) for k in range(S5_LT)]

    def body(t, carry):
        rows = pl.ds(t * chains, chains)
        out = []
        for k in range(S5_LT):
            xr, xi = carry[2 * k], carry[2 * k + 1]
            nr = ar[k] * xr - ai[k] * xi + bur[k, rows, :]
            ni = ar[k] * xi + ai[k] * xr + bui[k, rows, :]
            sr[k, rows, :] = nr
            si[k, rows, :] = ni
            out += [nr, ni]
        return tuple(out)

    init = []
    for k in range(S5_LT):
        init += [hr[:, k * LANES:(k + 1) * LANES], hi[:, k * LANES:(k + 1) * LANES]]
    fin = lax.fori_loop(0, chunk, body, tuple(init), unroll=2)
    xr = jnp.concatenate(fin[0::2], axis=1)
    xi = jnp.concatenate(fin[1::2], axis=1)
    hr[...] = xr
    hi[...] = xi
    hro_ref[...] = xr
    hio_ref[...] = xi
    s_r = jnp.concatenate([sr[k] for k in range(S5_LT)], axis=1).astype(jnp.bfloat16)
    s_i = jnp.concatenate([si[k] for k in range(S5_LT)], axis=1).astype(jnp.bfloat16)
    y = jnp.concatenate(
        [jnp.dot(s_r[:, h * hn:(h + 1) * hn], cr_ref[h * hn:(h + 1) * hn, h * hd:(h + 1) * hd],
                 preferred_element_type=jnp.float32)
         - jnp.dot(s_i[:, h * hn:(h + 1) * hn], ci_ref[h * hn:(h + 1) * hn, h * hd:(h + 1) * hd],
                   preferred_element_type=jnp.float32) for h in range(2)], axis=1)
    y = sum(lax.dot_general(to_tc, part, (((0,), (0,)), ((), ())), preferred_element_type=jnp.float32)
            for part in _split3(y)) + d_ref[...] * u
    z = jax.nn.gelu(y)
    gate = jax.nn.sigmoid(jnp.dot(z.astype(jnp.bfloat16), wglu_ref[...], preferred_element_type=jnp.float32))
    y_ref[...] = (z * gate).reshape(chains, chunk, S5_DIM)


def s5_params(lam_re, lam_im, log_dt, b, c, d, w_glu):
    f32 = jnp.float32
    dt = jnp.exp(log_dt.astype(f32))[:, None]
    mag = jnp.exp(lam_re * dt)
    ang = lam_im * dt
    ab_re = mag * jnp.cos(ang)
    ab_im = mag * jnp.sin(ang)
    den = lam_re * lam_re + lam_im * lam_im
    nr = ab_re - 1.0
    coef_re = (nr * lam_re + ab_im * lam_im) / den
    coef_im = (ab_im * lam_re - nr * lam_im) / den
    b_re = b[..., 0].astype(f32)
    b_im = b[..., 1].astype(f32)
    bb_re = coef_re[..., None] * b_re - coef_im[..., None] * b_im
    bb_im = coef_re[..., None] * b_im + coef_im[..., None] * b_re
    eye = jnp.eye(S5_GROUPS, dtype=f32)
    bbr = jnp.einsum('gnk,gh->gkhn', bb_re, eye).reshape(S5_DIM, S5_N).astype(jnp.bfloat16)
    bbi = jnp.einsum('gnk,gh->gkhn', bb_im, eye).reshape(S5_DIM, S5_N).astype(jnp.bfloat16)
    cr = jnp.einsum('gkn,gh->gnhk', c[..., 0].astype(f32), eye).reshape(S5_N, S5_DIM).astype(jnp.bfloat16)
    ci = jnp.einsum('gkn,gh->gnhk', c[..., 1].astype(f32), eye).reshape(S5_N, S5_DIM).astype(jnp.bfloat16)
    return (ab_re.reshape(1, S5_N), ab_im.reshape(1, S5_N), bbr, bbi, cr, ci,
            d.astype(f32).reshape(1, S5_DIM), w_glu.astype(jnp.bfloat16))


def s5_scan(u, h0, params, chunk):
    chains, t, _ = u.shape
    ar, ai, bbr, bbi, cr, ci, d, wglu = params
    h0r = h0[..., 0].reshape(chains, S5_N)
    h0i = h0[..., 1].reshape(chains, S5_N)
    full = lambda shape: pl.BlockSpec(shape, lambda j: (0,) * len(shape))
    rows = chains * chunk
    r = jnp.arange(rows)
    to_tc = (r[None, :] == (r[:, None] % chains) * chunk + r[:, None] // chains).astype(jnp.bfloat16)
    y, hr, hi = pl.pallas_call(
        functools.partial(_s5_kernel, chains=chains, chunk=chunk),
        grid=(t // chunk,),
        in_specs=[pl.BlockSpec((chains, chunk, S5_DIM), lambda j: (0, j, 0)), full((rows, rows)),
                  full((chains, S5_N)), full((chains, S5_N)), full((1, S5_N)), full((1, S5_N)),
                  full((S5_DIM, S5_N)), full((S5_DIM, S5_N)), full((S5_N, S5_DIM)), full((S5_N, S5_DIM)),
                  full((1, S5_DIM)), full((S5_DIM, S5_DIM))],
        out_specs=[pl.BlockSpec((chains, chunk, S5_DIM), lambda j: (0, j, 0)),
                   full((chains, S5_N)), full((chains, S5_N))],
        out_shape=[jax.ShapeDtypeStruct((chains, t, S5_DIM), jnp.float32),
                   jax.ShapeDtypeStruct((chains, S5_N), jnp.float32),
                   jax.ShapeDtypeStruct((chains, S5_N), jnp.float32)],
        scratch_shapes=[pltpu.VMEM((S5_LT, rows, LANES), jnp.float32)] * 4
                       + [pltpu.VMEM((chains, S5_N), jnp.float32)] * 2,
        compiler_params=_cparams("arbitrary"),
        name="s5_scan",
    )(u, to_tc, h0r, h0i, ar, ai, bbr, bbi, cr, ci, d, wglu)
    new_state = jnp.stack([hr.reshape(chains, S5_GROUPS, S5_STATE), hi.reshape(chains, S5_GROUPS, S5_STATE)],
                          axis=-1)
    return y, new_state


def _dot_nt(a, b):
    return lax.dot_general(a, b, (((1,), (1,)), ((), ())), preferred_element_type=jnp.float32)


def _split3(x):
    hi = x.astype(jnp.bfloat16)
    rem = x - hi.astype(jnp.float32)
    mid = rem.astype(jnp.bfloat16)
    lo = (rem - mid.astype(jnp.float32)).astype(jnp.bfloat16)
    return hi, mid, lo


def _softmax_rows(s, mask):
    s = jnp.where(mask, s, NEG)
    m = jnp.max(s, axis=-1, keepdims=True)
    p = jnp.exp2(s - m)
    inv = jnp.where(m > 0.5 * NEG, 1.0 / jnp.sum(p, axis=-1, keepdims=True), 0.0)
    return p * inv


def _nsa_prompt_kernel(q_ref, gate_ref, kc_ref, vc_ref, ks_ref, vs_ref, kw_ref, vw_ref, o_ref, *, n_cmp, n_blk):
    f32, bf16 = jnp.float32, jnp.bfloat16
    r4 = HEADS_PER_GROUP
    n_cpad = kc_ref.shape[0]
    start = pl.program_id(1) * Q_BLOCK
    q = q_ref[...] * QK_SCALE
    gate = gate_ref[...]
    lane = lax.broadcasted_iota(jnp.int32, (Q_BLOCK, LANES), 1)
    qpos = start + lax.broadcasted_iota(jnp.int32, (Q_BLOCK, 1), 0)
    n_idx = lax.broadcasted_iota(jnp.int32, (Q_BLOCK, n_cpad), 1)
    cmask = (((n_idx * CMP_STRIDE + (CMP_LEN - 1)) <= qpos) & (n_idx < n_cmp))[None]
    ratio = SEL_BLOCK // CMP_STRIDE
    gsum = (lax.broadcasted_iota(jnp.int32, (n_blk, n_cpad), 1) // ratio
            == lax.broadcasted_iota(jnp.int32, (n_blk, n_cpad), 0)).astype(bf16)
    blk = lax.broadcasted_iota(jnp.int32, (n_blk, Q_BLOCK), 0)
    blk_f = blk.astype(f32)
    jq = (start + lax.broadcasted_iota(jnp.int32, (n_blk, Q_BLOCK), 1)) // SEL_BLOCK
    force = jnp.where((blk == 0) | (blk == jq) | (blk == jq - 1), FORCE, 0.0)
    qgs, o_cs, sels = [], [], []
    for g in range(KV_GROUPS):
        keep = (lane < HEAD_DIM) if g == 0 else (lane >= HEAD_DIM)
        parts = []
        for r in range(r4):
            h = r4 * g + r
            tile = q[:, (h // 2) * LANES:(h // 2 + 1) * LANES]
            if h % 2 != g:
                tile = pltpu.roll(tile, HEAD_DIM, axis=1)
            parts.append(jnp.where(keep, tile, 0.0))
        qg = jnp.concatenate(parts, axis=0).astype(bf16)
        qgs.append(qg)

        p_c = _softmax_rows(_dot_nt(qg, kc_ref[...]).reshape(r4, Q_BLOCK, n_cpad), cmask)
        o_cs.append(jnp.dot(p_c.reshape(r4 * Q_BLOCK, n_cpad).astype(bf16), vc_ref[...],
                            preferred_element_type=f32).reshape(r4, Q_BLOCK, LANES))
        psum = p_c[0] + p_c[1] + p_c[2] + p_c[3]
        imp_t = sum(_dot_nt(gsum, part) for part in _split3(psum))

        score = jnp.where(blk <= jq, imp_t + force, NEG)
        sel_t = jnp.zeros((n_blk, Q_BLOCK), f32)
        for _ in range(min(N_SEL, n_blk)):
            m = jnp.max(score, axis=0, keepdims=True)
            idx = jnp.min(jnp.where(score == m, blk_f, float(n_blk)), axis=0, keepdims=True)
            hit = blk_f == idx
            sel_t = jnp.where(hit & (m > 0.5 * NEG), 1.0, sel_t)
            score = jnp.where(hit, REMOVED, score)
        sels.append(sel_t.T)

    n_full = start // SEL_TILE
    expand0 = (lax.broadcasted_iota(jnp.int32, (n_blk, SEL_TILE), 0)
               == lax.broadcasted_iota(jnp.int32, (n_blk, SEL_TILE), 1) // SEL_BLOCK).astype(bf16)

    def tile_update(i, carry, causal):
        off = pl.multiple_of(i * SEL_TILE, SEL_TILE)
        k = ks_ref[pl.ds(off, SEL_TILE), :]
        v = vs_ref[pl.ds(off, SEL_TILE), :]
        vlane = lax.broadcasted_iota(jnp.int32, (SEL_TILE, LANES), 1)
        v_ones = [jnp.where((vlane < HEAD_DIM) == (g == 0), v, jnp.ones_like(v)) for g in range(KV_GROUPS)]
        out = []
        for g in range(KV_GROUPS):
            m_run, l_run, acc = carry[g]
            s_t = _dot_nt(qgs[g], k).reshape(r4, Q_BLOCK, SEL_TILE)
            shifted = pltpu.roll(sels[g], (n_blk - i * (SEL_TILE // SEL_BLOCK)) % n_blk, axis=1).astype(bf16)
            mk = jnp.dot(shifted, expand0, preferred_element_type=f32) > 0.5
            if causal:
                kpos = i * SEL_TILE + lax.broadcasted_iota(jnp.int32, (Q_BLOCK, SEL_TILE), 1)
                mk = mk & (kpos <= qpos)
            s_t = jnp.where(mk[None], s_t, NEG)
            m_new = jnp.maximum(m_run, jnp.max(s_t, axis=-1, keepdims=True))
            alpha = jnp.exp2(m_run - m_new)
            p = jnp.exp2((s_t - m_new).astype(bf16))
            pv = jnp.dot(p.reshape(r4 * Q_BLOCK, SEL_TILE), v_ones[g], preferred_element_type=f32)
            pv = pv.reshape(r4, Q_BLOCK, LANES)
            l_new = alpha * l_run + pv[:, :, (1 - g) * HEAD_DIM:(1 - g) * HEAD_DIM + 1]
            out.append((m_new, l_new, alpha * acc + pv))
        return tuple(out)

    init = (jnp.full((r4, Q_BLOCK, 1), NEG, f32), jnp.zeros((r4, Q_BLOCK, 1), f32),
            jnp.zeros((r4, Q_BLOCK, LANES), f32))
    carry = lax.fori_loop(0, n_full, lambda i, c: tile_update(i, c, False), (init, init))
    fin = tile_update(n_full, carry, True)

    n_win = WINDOW + Q_BLOCK
    woff = pl.multiple_of(start, Q_BLOCK)
    kwin = kw_ref[pl.ds(woff, n_win), :]
    vwin = vw_ref[pl.ds(woff, n_win), :]
    wpos = start - WINDOW + lax.broadcasted_iota(jnp.int32, (Q_BLOCK, n_win), 1)
    wmask = ((wpos <= qpos) & (wpos > qpos - WINDOW) & (wpos >= 0))[None]
    heads = [None] * N_HEADS
    for g in range(KV_GROUPS):
        m_fin, l_fin, acc = fin[g]
        o_s = acc * jnp.where(m_fin > 0.5 * NEG, 1.0 / l_fin, 0.0)
        p_w = _softmax_rows(_dot_nt(qgs[g], kwin).reshape(r4, Q_BLOCK, n_win), wmask)
        o_w = jnp.dot(p_w.reshape(r4 * Q_BLOCK, n_win).astype(bf16), vwin,
                      preferred_element_type=f32).reshape(r4, Q_BLOCK, LANES)
        for r in range(r4):
            h = r4 * g + r
            heads[h] = (gate[:, 3 * h:3 * h + 1] * o_cs[g][r] + gate[:, 3 * h + 1:3 * h + 2] * o_s[r]
                        + gate[:, 3 * h + 2:3 * h + 3] * o_w[r])

    tiles = []
    for j in range(N_HEADS // 2):
        even, odd = heads[2 * j], heads[2 * j + 1]
        if j // 2 == 0:
            tiles.append(jnp.where(lane < HEAD_DIM, even, pltpu.roll(odd, HEAD_DIM, axis=1)))
        else:
            tiles.append(jnp.where(lane < HEAD_DIM, pltpu.roll(even, HEAD_DIM, axis=1), odd))
    o_ref[...] = jnp.concatenate(tiles, axis=1)


def nsa_prompt(q, gates, kc, vc, kvs, kvw_pad):
    b, t, _ = q.shape
    n_cpad = kc.shape[1]
    kern = functools.partial(_nsa_prompt_kernel, n_cmp=t // CMP_STRIDE - 1, n_blk=t // SEL_BLOCK)
    whole = lambda rows, c=0: pl.BlockSpec((None, rows, LANES), lambda i, j: (i, 0, c))
    return pl.pallas_call(
        kern,
        grid=(b, t // Q_BLOCK),
        in_specs=[pl.BlockSpec((None, Q_BLOCK, NSA_Q), lambda i, j: (i, j, 0)),
                  pl.BlockSpec((None, Q_BLOCK, 3 * N_HEADS), lambda i, j: (i, j, 0)),
                  whole(n_cpad), whole(n_cpad), whole(t, 0), whole(t, 1), whole(t + WINDOW, 0),
                  whole(t + WINDOW, 1)],
        out_specs=pl.BlockSpec((None, Q_BLOCK, NSA_Q), lambda i, j: (i, j, 0)),
        out_shape=jax.ShapeDtypeStruct((b, t, NSA_Q), jnp.float32),
        compiler_params=_cparams("parallel", "arbitrary"),
        name="nsa_prompt",
    )(q, gates, kc, vc, kvs, kvs, kvw_pad, kvw_pad)


def _compress_kernel(ch_ref, pet_ref, peb_ref, w1t_ref, w1b_ref, w2_ref, o_ref):
    bf16 = jnp.bfloat16
    ch = ch_ref[...]
    n_ch = ch.shape[0]
    a = jnp.dot((ch + pet_ref[...]).astype(bf16), w1t_ref[...], preferred_element_type=jnp.float32)
    b = jnp.dot((ch + peb_ref[...]).astype(bf16), w1b_ref[...], preferred_element_type=jnp.float32)
    pre = a + pltpu.roll(b, n_ch - 1, axis=0)
    o_ref[...] = jnp.dot(jax.nn.gelu(pre).astype(bf16), w2_ref[...],
                         preferred_element_type=jnp.float32).astype(o_ref.dtype)


def compress_params(w1, w2, pe):
    f32 = jnp.float32
    eye = jnp.eye(KV_GROUPS, dtype=f32)
    w1r = w1.astype(f32).reshape(2, CMP_STRIDE, HEAD_DIM, HEAD_DIM)
    big = jnp.einsum('hjde,gk->hjgdke', w1r, eye).reshape(2, CMP_STRIDE * LANES, LANES).astype(jnp.bfloat16)
    w2bd = jnp.einsum('de,gk->gdke', w2.astype(f32), eye).reshape(LANES, LANES).astype(jnp.bfloat16)
    per = pe.astype(f32).reshape(2, CMP_STRIDE, 1, HEAD_DIM)
    pe_rows = jnp.broadcast_to(per, (2, CMP_STRIDE, KV_GROUPS, HEAD_DIM)).reshape(2, 1, CMP_STRIDE * LANES)
    return pe_rows[0], pe_rows[1], big[0], big[1], w2bd


def compress_prompt(x, params):
    b, t, _ = x.shape
    n_ch = t // CMP_STRIDE
    ch = x.reshape(b, n_ch, CMP_STRIDE * LANES)
    pet, peb, w1t, w1b, w2bd = params
    full = lambda shape: pl.BlockSpec(shape, lambda i: (0,) * len(shape))
    return pl.pallas_call(
        _compress_kernel,
        grid=(b,),
        in_specs=[pl.BlockSpec((None, n_ch, CMP_STRIDE * LANES), lambda i: (i, 0, 0)),
                  full((1, CMP_STRIDE * LANES)), full((1, CMP_STRIDE * LANES)),
                  full((CMP_STRIDE * LANES, LANES)), full((CMP_STRIDE * LANES, LANES)), full((LANES, LANES))],
        out_specs=pl.BlockSpec((None, n_ch, LANES), lambda i: (i, 0, 0)),
        out_shape=jax.ShapeDtypeStruct((b, n_ch, LANES), jnp.bfloat16),
        compiler_params=_cparams("parallel"),
        name="compress_prompt",
    )(ch, pet, peb, w1t, w1b, w2bd)


def _cmp_sample_kernel(pt_ref, *refs, n_pages):
    f32, bf16 = jnp.float32, jnp.bfloat16
    pp = PAGES_PER_STEP
    pages = refs[0:pp]
    (perm_ref, newk_ref, newv_ref, wk_ref, wv_ref, ck_ref, cv_ref, w2k_ref, w2v_ref,
     kc_ref, vc_ref, slab_k, slab_v) = refs[pp:]
    s = pl.program_id(1)
    cpp = PAGE_SIZE // CMP_STRIDE
    base = pl.multiple_of(s * (pp * cpp), pp * cpp)
    for half, slab in enumerate((slab_k, slab_v)):
        for i in range(pp):
            page = pages[i][half * LANES:(half + 1) * LANES, :].astype(bf16)
            rows = _dot_nt(perm_ref[...], page)
            for j in range(CMP_STRIDE):
                slab[j, pl.ds(base + i * cpp, cpp), :] = rows[j * cpp:(j + 1) * cpp, :]

    @pl.when(s == pl.num_programs(1) - 1)
    def _():
        n_ch = n_pages * (PAGE_SIZE // CMP_STRIDE)
        row = lax.broadcasted_iota(jnp.int32, (n_ch, LANES), 0)
        for slab, new_ref, w_ref, c_ref, w2_ref, o_ref in ((slab_k, newk_ref, wk_ref, ck_ref, w2k_ref, kc_ref),
                                                           (slab_v, newv_ref, wv_ref, cv_ref, w2v_ref, vc_ref)):
            ch = jnp.concatenate([slab[j] for j in range(CMP_STRIDE)], axis=1).astype(bf16)
            ab = jnp.dot(ch, w_ref[...], preferred_element_type=f32)
            b_new = jnp.dot(new_ref[...].astype(bf16), w_ref[...], preferred_element_type=f32)[0:1, LANES:]
            nxt = pltpu.roll(ab[:, LANES:], n_ch - 1, axis=0)
            nxt = jnp.where(row == n_ch - 1, b_new, nxt)
            pre = ab[:, :LANES] + nxt + c_ref[...]
            o_ref[...] = jnp.dot(jax.nn.gelu(pre).astype(bf16), w2_ref[...],
                                 preferred_element_type=f32).astype(o_ref.dtype)


def compress_sample_params(w1, w2, pe):
    pet, peb, w1t, w1b, w2bd = compress_params(w1, w2, pe)
    hp = lax.Precision.HIGHEST
    const = (jnp.dot(pet, w1t.astype(jnp.float32), precision=hp)
             + jnp.dot(peb, w1b.astype(jnp.float32), precision=hp))
    return jnp.concatenate([w1t, w1b], axis=1), const, w2bd


def _page_spec(i, pair):
    return pl.BlockSpec((None, 2 * LANES, PAGE_SIZE),
                        lambda b, s, pt: (pt[b, PAGES_PER_STEP * s + i], pair, 0))


def _per_seq(shape):
    return pl.BlockSpec((None,) + shape, lambda b, s, pt: (b, 0, 0))


def compress_sample(pool_t, page_table, new_k, new_v, pk, pv):
    bsz, n_pages = page_table.shape
    pp = PAGES_PER_STEP
    n_ch = n_pages * (PAGE_SIZE // CMP_STRIDE)
    t_new = new_k.shape[1]

    def chunk_rows(x):
        x = jnp.pad(x, ((0, 0), (0, CMP_STRIDE - t_new), (0, 0))).reshape(bsz, 1, CMP_STRIDE * LANES)
        return jnp.pad(x, ((0, 0), (0, 7), (0, 0)))

    full = lambda shape: pl.BlockSpec(shape, lambda b, s, pt: (0,) * len(shape))
    r = jnp.arange(PAGE_SIZE)
    cpp = PAGE_SIZE // CMP_STRIDE
    perm = (r[None, :] == (r[:, None] % cpp) * CMP_STRIDE + r[:, None] // cpp).astype(jnp.bfloat16)
    wk, ck, w2k = pk
    wv, cv, w2v = pv
    grid_spec = pltpu.PrefetchScalarGridSpec(
        num_scalar_prefetch=1,
        grid=(bsz, n_pages // pp),
        in_specs=[_page_spec(i, 0) for i in range(pp)]
                 + [full((PAGE_SIZE, PAGE_SIZE)), _per_seq((8, CMP_STRIDE * LANES)),
                    _per_seq((8, CMP_STRIDE * LANES)),
                    full((CMP_STRIDE * LANES, 2 * LANES)), full((CMP_STRIDE * LANES, 2 * LANES)),
                    full((1, LANES)), full((1, LANES)), full((LANES, LANES)), full((LANES, LANES))],
        out_specs=[_per_seq((n_ch, LANES)), _per_seq((n_ch, LANES))],
        scratch_shapes=[pltpu.VMEM((CMP_STRIDE, n_ch, LANES), jnp.float32)] * 2,
    )
    return pl.pallas_call(
        functools.partial(_cmp_sample_kernel, n_pages=n_pages),
        grid_spec=grid_spec,
        out_shape=[jax.ShapeDtypeStruct((bsz, n_ch, LANES), jnp.bfloat16)] * 2,
        compiler_params=_cparams("parallel", "arbitrary"),
        name="compress_sample",
    )(page_table, *([pool_t] * pp), perm, chunk_rows(new_k), chunk_rows(new_v), wk, wv, ck, cv, w2k, w2v)


def _nsa_sample_kernel(pt_ref, *refs, n_pages, t_new, w_buf):
    f32, bf16 = jnp.float32, jnp.bfloat16
    pp = PAGES_PER_STEP
    q_ref, gate_ref, kc_ref, vc_ref = refs[0:4]
    pages = refs[4:4 + pp]
    (ksn_ref, vsn_ref, win_ref, kwn_ref, vwn_ref, o_ref,
     sel_scr, exp_scr, oc_scr, m_scr, l_scr, acc_scr) = refs[4 + pp:]
    r4, g2 = HEADS_PER_GROUP, KV_GROUPS
    n_rows = g2 * r4 * t_new
    past_len = n_pages * PAGE_SIZE
    n_cmp = kc_ref.shape[0]
    n_bpad = sel_scr.shape[1]
    tile = pp * PAGE_SIZE
    s = pl.program_id(1)
    qall = q_ref[...]
    qpos = past_len + lax.broadcasted_iota(jnp.int32, (n_rows, 1), 0) % t_new

    def grouped(x):
        return x.reshape(g2, 1, t_new, x.shape[-1])

    @pl.when(s == 0)
    def _():
        s_c = _dot_nt(qall, kc_ref[...])
        n_idx = lax.broadcasted_iota(jnp.int32, (n_rows, n_cmp), 1)
        p_c = _softmax_rows(s_c, (n_idx * CMP_STRIDE + (CMP_LEN - 1)) <= qpos)
        oc_scr[...] = jnp.dot(p_c.astype(bf16), vc_ref[...], preferred_element_type=f32)
        psum = jnp.sum(p_c.reshape(g2, r4, t_new, n_cmp), axis=1).reshape(g2 * t_new, n_cmp)
        psum = jnp.concatenate([psum, jnp.zeros((LANES - g2 * t_new, n_cmp), f32)], axis=0)
        p_hi = psum.astype(bf16)
        rem = psum - p_hi.astype(f32)
        p_mid = rem.astype(bf16)
        p_lo = (rem - p_mid.astype(f32)).astype(bf16)
        ratio = SEL_BLOCK // CMP_STRIDE
        gsum = (lax.broadcasted_iota(jnp.int32, (n_bpad, n_cmp), 1) // ratio
                == lax.broadcasted_iota(jnp.int32, (n_bpad, n_cmp), 0)).astype(bf16)
        imp_t = _dot_nt(gsum, p_hi) + _dot_nt(gsum, p_mid) + _dot_nt(gsum, p_lo)
        blk = lax.broadcasted_iota(jnp.int32, (n_bpad, LANES), 0)
        jq = (past_len + lax.broadcasted_iota(jnp.int32, (n_bpad, LANES), 1) % t_new) // SEL_BLOCK
        forced = (blk == 0) | (blk == jq) | (blk == jq - 1)
        score = jnp.where(blk <= jq, imp_t + jnp.where(forced, FORCE, 0.0), NEG)
        blk_f = blk.astype(f32)
        sel_t = jnp.zeros((n_bpad, LANES), f32)
        for _ in range(N_SEL):
            m = jnp.max(score, axis=0, keepdims=True)
            idx = jnp.min(jnp.where(score == m, blk_f, float(n_bpad)), axis=0, keepdims=True)
            hit = blk_f == idx
            sel_t = jnp.where(hit & (m > 0.5 * NEG), 1.0, sel_t)
            score = jnp.where(hit, REMOVED, score)
        sel = jnp.concatenate([sel_t[k * LANES:(k + 1) * LANES].T for k in range(n_bpad // LANES)], axis=1)
        sel_scr[...] = sel[0:g2 * t_new]
        exp_scr[...] = (lax.broadcasted_iota(jnp.int32, (LANES, tile), 0)
                        == lax.broadcasted_iota(jnp.int32, (LANES, tile), 1) // SEL_BLOCK).astype(bf16)
        m_scr[...] = jnp.full(m_scr.shape, NEG, f32)
        l_scr[...] = jnp.zeros(l_scr.shape, f32)
        acc_scr[...] = jnp.zeros(acc_scr.shape, f32)

    def online_update(s_t, mk, v, v_feature_major):
        n = s_t.shape[-1]
        s4 = jnp.where(mk, s_t.reshape(g2, r4, t_new, n), NEG)
        m_run = m_scr[...].reshape(g2, r4, t_new, 1)
        m_new = jnp.maximum(m_run, jnp.max(s4, axis=-1, keepdims=True))
        alpha = jnp.exp2(m_run - m_new)
        p = jnp.exp2(s4 - m_new)
        l_new = alpha * l_scr[...].reshape(g2, r4, t_new, 1) + jnp.sum(p, axis=-1, keepdims=True)
        pb = p.reshape(n_rows, n).astype(bf16)
        pv = _dot_nt(pb, v) if v_feature_major else jnp.dot(pb, v, preferred_element_type=f32)
        m_scr[...] = m_new.reshape(n_rows, 1)
        l_scr[...] = l_new.reshape(n_rows, 1)
        acc_scr[...] = alpha.reshape(n_rows, 1) * acc_scr[...] + pv

    kt = jnp.concatenate([r[0:LANES, :] for r in pages], axis=1).astype(bf16)
    vt = jnp.concatenate([r[LANES:2 * LANES, :] for r in pages], axis=1).astype(bf16)
    shifted = pltpu.roll(sel_scr[...], (n_bpad - s * (tile // SEL_BLOCK)) % n_bpad, axis=1)
    picked = jnp.dot(shifted[:, 0:LANES].astype(bf16), exp_scr[...], preferred_element_type=f32)
    online_update(jnp.dot(qall, kt, preferred_element_type=f32), grouped(picked) > 0.5, vt, True)

    @pl.when(s == pl.num_programs(1) - 1)
    def _():
        new_blk = past_len // SEL_BLOCK
        kidx = lax.broadcasted_iota(jnp.int32, (n_rows, NEW_PAD), 1)
        causal = ((past_len + kidx) <= qpos) & (kidx < t_new)
        picked_new = sel_scr[:, new_blk:new_blk + 1]
        mk = (grouped(picked_new) > 0.5) & causal.reshape(g2, r4, t_new, NEW_PAD)
        online_update(_dot_nt(qall, ksn_ref[...]), mk, vsn_ref[...], False)
        o_s = acc_scr[...] * jnp.where(m_scr[...] > 0.5 * NEG, 1.0 / l_scr[...], 0.0)

        n_win = w_buf + NEW_PAD
        kw_t = win_ref[0:LANES, :].astype(bf16)
        vw_t = win_ref[LANES:2 * LANES, :].astype(bf16)
        widx = lax.broadcasted_iota(jnp.int32, (n_rows, n_win), 1)
        wpos = past_len - w_buf + widx
        wmask = (wpos <= qpos) & (wpos > qpos - WINDOW) & (wpos >= 0) & (widx < w_buf + t_new)
        s_w = jnp.concatenate([jnp.dot(qall, kw_t, preferred_element_type=f32), _dot_nt(qall, kwn_ref[...])],
                              axis=1)
        p_w = _softmax_rows(s_w, wmask).astype(bf16)
        o_w = (_dot_nt(p_w[:, 0:w_buf], vw_t)
               + jnp.dot(p_w[:, w_buf:], vwn_ref[...], preferred_element_type=f32))
        gate = gate_ref[...]
        o_ref[...] = gate[:, 0:1] * oc_scr[...] + gate[:, 1:2] * o_s + gate[:, 2:3] * o_w


def nsa_sample(q, gates, kc, vc, pool_t, page_table, ks_new, vs_new, win, kw_new, vw_new):
    f32, bf16 = jnp.float32, jnp.bfloat16
    bsz, t_new = q.shape[0], q.shape[1]
    n_pages = page_table.shape[1]
    pp = PAGES_PER_STEP
    w_buf = win.shape[2]
    r4, g2 = HEADS_PER_GROUP, KV_GROUPS
    n_rows = g2 * r4 * t_new
    past_len = n_pages * PAGE_SIZE
    assert past_len % SEL_BLOCK == 0 and t_new <= SEL_BLOCK and past_len >= w_buf and n_pages % pp == 0
    n_sel = past_len // SEL_BLOCK + 1
    n_bpad = -(-n_sel // LANES) * LANES
    eye = jnp.eye(g2, dtype=f32)
    qg = q.reshape(bsz, t_new, g2, r4, HEAD_DIM).transpose(0, 2, 3, 1, 4) * QK_SCALE
    qall = jnp.einsum('bgrqd,gk->bgrqkd', qg, eye).reshape(bsz, n_rows, LANES).astype(bf16)
    gall = gates.reshape(bsz, t_new, g2, r4, 3).transpose(0, 2, 3, 1, 4).reshape(bsz, n_rows, 3)
    pad_rows = lambda x: jnp.pad(x, ((0, 0), (0, NEW_PAD - t_new), (0, 0))).astype(bf16)
    n_cmp = kc.shape[1]
    grid_spec = pltpu.PrefetchScalarGridSpec(
        num_scalar_prefetch=1,
        grid=(bsz, n_pages // pp),
        in_specs=[_per_seq((n_rows, LANES)), _per_seq((n_rows, 3)), _per_seq((n_cmp, LANES)),
                  _per_seq((n_cmp, LANES))]
                 + [_page_spec(i, 1) for i in range(pp)]
                 + [_per_seq((NEW_PAD, LANES)), _per_seq((NEW_PAD, LANES)), _per_seq((2 * LANES, w_buf)),
                    _per_seq((NEW_PAD, LANES)), _per_seq((NEW_PAD, LANES))],
        out_specs=_per_seq((n_rows, LANES)),
        scratch_shapes=[pltpu.VMEM((g2 * t_new, n_bpad), f32), pltpu.VMEM((LANES, pp * PAGE_SIZE), bf16),
                        pltpu.VMEM((n_rows, LANES), f32),
                        pltpu.VMEM((n_rows, 1), f32), pltpu.VMEM((n_rows, 1), f32),
                        pltpu.VMEM((n_rows, LANES), f32)],
    )
    o = pl.pallas_call(
        functools.partial(_nsa_sample_kernel, n_pages=n_pages, t_new=t_new, w_buf=w_buf),
        grid_spec=grid_spec,
        out_shape=jax.ShapeDtypeStruct((bsz, n_rows, LANES), f32),
        compiler_params=_cparams("parallel", "arbitrary"),
        name="nsa_sample",
    )(page_table, qall, gall, kc, vc, *([pool_t] * pp), pad_rows(ks_new), pad_rows(vs_new), win,
      pad_rows(kw_new), pad_rows(vw_new))
    o = jnp.einsum('bgrqkd,gk->bqgrd', o.reshape(bsz, g2, r4, t_new, g2, HEAD_DIM), eye)
    return o.reshape(bsz, t_new, NSA_Q)


def _ssd_kernel(x_ref, b_ref, c_ref, dt_ref, a_ref, za_ref, zb_ref, dskip_ref, ng_ref, y_ref, hout_ref, h_scr, *,
                chunk):
    f32, bf16 = jnp.float32, jnp.bfloat16
    n_l = chunk
    hpg = SSD_HEADS // SSD_GROUPS
    gw = hpg * SSD_HEAD_DIM
    j = pl.program_id(1)

    @pl.when(j == 0)
    def _():
        h_scr[...] = jnp.zeros(h_scr.shape, f32)

    x = x_ref[...]
    dt = dt_ref[...]
    tri_b = (lax.broadcasted_iota(jnp.int32, (n_l, n_l), 0) >= lax.broadcasted_iota(jnp.int32, (n_l, n_l), 1))
    tri = tri_b.astype(bf16)
    cum = sum(jnp.dot(tri, part, preferred_element_type=f32) for part in _split3(dt * a_ref[...]))
    cum_t = cum.T
    dt_t = dt.T
    ecum = jnp.exp(cum)
    clast = cum[n_l - 1:n_l, :]
    wt = jnp.exp(clast - cum) * dt
    elast = jnp.exp(clast)
    lane = lax.broadcasted_iota(jnp.int32, (n_l, LANES), 1)
    low = lane < SSD_HEAD_DIM

    def pair(v, h0):
        return jnp.where(low[:v.shape[0]], v[:, h0:h0 + 1], v[:, h0 + 1:h0 + 2])

    tiles = []
    for g in range(SSD_GROUPS):
        bg = b_ref[:, g * SSD_STATE:(g + 1) * SSD_STATE]
        cgb = c_ref[:, g * SSD_STATE:(g + 1) * SSD_STATE].astype(bf16)
        bgt = bg.T.astype(bf16)
        cb = jnp.dot(cgb, bgt, preferred_element_type=f32)
        hg = h_scr[g]
        y_inter = jnp.dot(cgb, hg.astype(bf16), preferred_element_type=f32)
        xw, dec = [], []
        for pr in range(hpg // 2):
            h0 = hpg * g + 2 * pr
            xt = x[:, (h0 // 2) * LANES:(h0 // 2 + 1) * LANES]
            acc = None
            for k in range(2):
                h = h0 + k
                seg = cum[:, h:h + 1] - cum_t[h:h + 1, :]
                w = cb * jnp.exp(jnp.where(tri_b, seg, NEG)) * dt_t[h:h + 1, :]
                xm = jnp.where(low if k == 0 else jnp.logical_not(low), xt, 0.0).astype(bf16)
                part = jnp.dot(w.astype(bf16), xm, preferred_element_type=f32)
                acc = part if acc is None else acc + part
            tiles.append(acc + y_inter[:, pr * LANES:(pr + 1) * LANES] * pair(ecum, h0))
            xw.append((xt * pair(wt, h0)).astype(bf16))
            dec.append(pair(elast, h0))
        h_scr[g] = (hg * jnp.concatenate(dec, axis=1)
                    + jnp.dot(bgt, jnp.concatenate(xw, axis=1), preferred_element_type=f32))
    y = jnp.concatenate(tiles, axis=1) + dskip_ref[...] * x
    zg = jnp.concatenate([za_ref[...], zb_ref[...]], axis=1)
    v = y * (zg * jax.nn.sigmoid(zg))
    outs = []
    for g in range(SSD_GROUPS):
        vg = v[:, g * gw:(g + 1) * gw]
        outs.append(vg * lax.rsqrt(jnp.mean(vg * vg, axis=-1, keepdims=True) + RMS_EPS))
    y_ref[...] = jnp.concatenate(outs, axis=1) * ng_ref[...]

    @pl.when(j == pl.num_programs(1) - 1)
    def _():
        hout_ref[...] = h_scr[...]


def ssd_prompt(xbc, dt, a, z, zg_offset, d_skip, norm_g):
    f32 = jnp.float32
    bsz, t, _ = xbc.shape
    hpg = SSD_HEADS // SSD_GROUPS
    gn = SSD_GROUPS * SSD_STATE
    half = SSD_INNER // 2
    assert zg_offset % half == 0
    dt_p = jnp.pad(dt, ((0, 0), (0, 0), (0, LANES - SSD_HEADS)))
    a_p = jnp.pad(a.astype(f32), (0, LANES - SSD_HEADS)).reshape(1, LANES)
    dsk = jnp.repeat(d_skip.astype(f32), SSD_HEAD_DIM).reshape(1, SSD_INNER)
    blk = lambda w, c: pl.BlockSpec((None, SSD_CHUNK, w), lambda b, j: (b, j, c))
    full = lambda shape: pl.BlockSpec(shape, lambda b, j: (0,) * len(shape))
    state_spec = pl.BlockSpec((None, SSD_GROUPS, SSD_STATE, hpg * SSD_HEAD_DIM), lambda b, j: (b, 0, 0, 0))
    y, h = pl.pallas_call(
        functools.partial(_ssd_kernel, chunk=SSD_CHUNK),
        grid=(bsz, t // SSD_CHUNK),
        in_specs=[blk(SSD_INNER, 0), blk(gn, SSD_INNER // gn), blk(gn, SSD_INNER // gn + 1), blk(LANES, 0),
                  full((1, LANES)), blk(half, zg_offset // half), blk(half, zg_offset // half + 1),
                  full((1, SSD_INNER)), full((1, SSD_INNER))],
        out_specs=[blk(SSD_INNER, 0), state_spec],
        out_shape=[jax.ShapeDtypeStruct((bsz, t, SSD_INNER), f32),
                   jax.ShapeDtypeStruct((bsz, SSD_GROUPS, SSD_STATE, hpg * SSD_HEAD_DIM), f32)],
        scratch_shapes=[pltpu.VMEM((SSD_GROUPS, SSD_STATE, hpg * SSD_HEAD_DIM), f32)],
        compiler_params=_cparams("parallel", "arbitrary"),
        name="ssd_prompt",
    )(xbc, xbc, xbc, dt_p, a_p, z, z, dsk, norm_g.astype(f32).reshape(1, SSD_INNER))
    h = h.reshape(bsz, SSD_GROUPS, SSD_STATE, hpg, SSD_HEAD_DIM).transpose(0, 1, 3, 4, 2)
    return y, h.reshape(bsz, SSD_HEADS, SSD_HEAD_DIM, SSD_STATE)


def _inproj_even_kernel(x_ref, w_ref, c_ref, sa_ref, sb_ref,
                        u_ref, q_ref, rows_ref, kvw_ref, kvsb_ref, kvwb_ref, g_ref):
    bf16 = jnp.bfloat16
    z = jnp.dot(x_ref[...].astype(bf16), w_ref[...], preferred_element_type=jnp.float32)
    cos, s_up, s_down = c_ref[...], sa_ref[...], sb_ref[...]

    def rot(t):
        return (t * cos + pltpu.roll(t, LANES - ROT_DIM // 2, axis=1) * s_up
                + pltpu.roll(t, ROT_DIM // 2, axis=1) * s_down)

    tile = lambda k: z[:, k * LANES:(k + 1) * LANES]
    q0 = S5_DIM // LANES
    kv0 = q0 + NSA_Q // LANES
    u_ref[...] = z[:, 0:S5_DIM]
    q_ref[...] = jnp.concatenate([rot(tile(q0 + k)) for k in range(NSA_Q // LANES)], axis=1)
    kc, vc, ks, vs, kw, vw = (rot(tile(kv0)), tile(kv0 + 1), rot(tile(kv0 + 2)), tile(kv0 + 3),
                              rot(tile(kv0 + 4)), tile(kv0 + 5))
    rows_ref[...] = jnp.concatenate([kc, vc, ks, vs], axis=1)
    kvw = jnp.concatenate([kw, vw], axis=1)
    kvw_ref[...] = kvw
    kvwb_ref[...] = kvw.astype(bf16)
    kvsb_ref[...] = jnp.concatenate([ks, vs], axis=1).astype(bf16)
    g0 = (kv0 + 6) * LANES
    g_ref[...] = jax.nn.sigmoid(z[:, g0:g0 + 3 * N_HEADS])


def rope_tables(pos):
    half = ROT_DIM // 2
    inv = ROPE_THETA ** (-jnp.arange(half, dtype=jnp.float32) * 2.0 / ROT_DIM)
    ang = pos.astype(jnp.float32)[:, None] * inv[None, :]
    d = jnp.arange(LANES) % HEAD_DIM
    cos = jnp.take(jnp.cos(ang), d % half, axis=1)
    sin = jnp.take(jnp.sin(ang), d % half, axis=1)
    return (jnp.where(d < ROT_DIM, cos, 1.0), jnp.where(d < half, -sin, 0.0),
            jnp.where((d >= half) & (d < ROT_DIM), sin, 0.0))


def inproj_even(h, w_bf16, pos):
    f32, bf16 = jnp.float32, jnp.bfloat16
    n, d = h.shape
    tile = min(ROW_TILE, n)
    row = lambda w: pl.BlockSpec((tile, w), lambda i: (i, 0))
    fixed = lambda shape: pl.BlockSpec(shape, lambda i: (0, 0), pipeline_mode=pl.Buffered(1))
    widths = [(S5_DIM, f32), (NSA_Q, f32), (2 * NSA_KV, f32), (NSA_KV, f32), (NSA_KV, bf16), (NSA_KV, bf16),
              (3 * N_HEADS, f32)]
    return pl.pallas_call(
        _inproj_even_kernel,
        grid=(pl.cdiv(n, tile),),
        in_specs=[row(d), fixed(w_bf16.shape), row(LANES), row(LANES), row(LANES)],
        out_specs=[row(w) for w, _ in widths],
        out_shape=[jax.ShapeDtypeStruct((n, w), dt) for w, dt in widths],
        compiler_params=_cparams("parallel"),
        name="inproj_even",
    )(h, w_bf16, *rope_tables(pos))


def _causal_conv_tile(x, tail, w_ref, b_ref, width):
    row = lax.broadcasted_iota(jnp.int32, (TAIL, x.shape[1]), 0)
    acc = b_ref[...] + w_ref[width - 1:width, :] * x
    for k in range(1, width):
        xs = pltpu.roll(x, k, axis=0)
        head = jnp.where(row < k, pltpu.roll(tail, k, axis=0), xs[0:TAIL])
        xs = jnp.concatenate([head, xs[TAIL:]], axis=0)
        acc = acc + w_ref[width - 1 - k:width - k, :] * xs
    return acc


def _inproj_odd_kernel(x_ref, w_ref, scw_ref, scb_ref, cvw_ref, cvb_ref, dtb_ref,
                       ysc_ref, xbc_ref, dt_ref, zg_ref, tsc_ref, tx_ref, tail_sc, tail_x, *, tiles_per_seq):
    @pl.when(pl.program_id(0) % tiles_per_seq == 0)
    def _():
        tail_sc[...] = jnp.zeros(tail_sc.shape, jnp.float32)
        tail_x[...] = jnp.zeros(tail_x.shape, jnp.float32)

    z = jnp.dot(x_ref[...].astype(jnp.bfloat16), w_ref[...], preferred_element_type=jnp.float32)
    o_zg = 3 * SC_DIM
    o_x = o_zg + SSD_INNER
    o_dt = o_x + SSD_CONV_DIM
    n = z.shape[0]
    prod = z[:, 2 * SC_DIM:3 * SC_DIM] * z[:, 0:SC_DIM]
    ysc_ref[...] = z[:, SC_DIM:2 * SC_DIM] * _causal_conv_tile(prod, tail_sc[...], scw_ref, scb_ref, SC_WIDTH)
    xbc = z[:, o_x:o_dt]
    c = _causal_conv_tile(xbc, tail_x[...], cvw_ref, cvb_ref, SSD_CONV)
    xbc_ref[...] = c * jax.nn.sigmoid(c)
    dt_ref[...] = jax.nn.softplus(z[:, o_dt:o_dt + SSD_HEADS] + dtb_ref[...])
    zg_ref[...] = z[:, o_zg:o_x]
    tail_sc[...] = prod[n - TAIL:n]
    tail_x[...] = xbc[n - TAIL:n]
    tsc_ref[...] = prod[n - TAIL:n]
    tx_ref[...] = xbc[n - TAIL:n]


def inproj_odd_prompt(h, w_bf16, bsz, sc_w, sc_b, cv_w, cv_b, dt_bias):
    f32 = jnp.float32
    n_rows, d = h.shape
    t = n_rows // bsz
    assert t % ROW_TILE == 0
    tps = t // ROW_TILE
    row = lambda w: pl.BlockSpec((ROW_TILE, w), lambda i: (i, 0))
    fixed = lambda shape: pl.BlockSpec(shape, lambda i: (0,) * len(shape), pipeline_mode=pl.Buffered(1))
    last = lambda w: pl.BlockSpec((None, TAIL, w), lambda i: (i // tps, 0, 0))
    ysc, xbc, dt, zg, tsc, tx = pl.pallas_call(
        functools.partial(_inproj_odd_kernel, tiles_per_seq=tps),
        grid=(n_rows // ROW_TILE,),
        in_specs=[row(d), fixed(w_bf16.shape), fixed((SC_WIDTH, SC_DIM)), fixed((1, SC_DIM)),
                  fixed((SSD_CONV, SSD_CONV_DIM)), fixed((1, SSD_CONV_DIM)), fixed((1, SSD_HEADS))],
        out_specs=[row(SC_DIM), row(SSD_CONV_DIM), row(SSD_HEADS), row(SSD_INNER), last(SC_DIM), last(SSD_CONV_DIM)],
        out_shape=[jax.ShapeDtypeStruct((n_rows, SC_DIM), f32), jax.ShapeDtypeStruct((n_rows, SSD_CONV_DIM), f32),
                   jax.ShapeDtypeStruct((n_rows, SSD_HEADS), f32), jax.ShapeDtypeStruct((n_rows, SSD_INNER), f32),
                   jax.ShapeDtypeStruct((bsz, TAIL, SC_DIM), f32), jax.ShapeDtypeStruct((bsz, TAIL, SSD_CONV_DIM), f32)],
        scratch_shapes=[pltpu.VMEM((TAIL, SC_DIM), f32), pltpu.VMEM((TAIL, SSD_CONV_DIM), f32)],
        compiler_params=_cparams("arbitrary"),
        name="inproj_odd",
    )(h, w_bf16, sc_w.astype(f32), sc_b.astype(f32).reshape(1, SC_DIM), cv_w.astype(f32),
      cv_b.astype(f32).reshape(1, SSD_CONV_DIM), dt_bias.astype(f32).reshape(1, SSD_HEADS))
    seq = lambda a: a.reshape(bsz, t, a.shape[-1])
    return (seq(ysc), seq(xbc), seq(dt), seq(zg),
            tsc[:, TAIL - (SC_WIDTH - 1):], tx[:, TAIL - (SSD_CONV - 1):])


def last_rows(x, n):
    t = x.shape[1]
    if t < n:
        x = jnp.pad(x, [(0, 0), (n - t, 0)] + [(0, 0)] * (x.ndim - 2))
    return x[:, x.shape[1] - n:]


def causal_conv(x, buf, w, b):
    t = x.shape[1]
    width = w.shape[0]
    xp = jnp.concatenate([buf, x], axis=1)
    y = b + sum(xp[:, j:j + t] * w[j] for j in range(width))
    return y, xp[:, xp.shape[1] - (width - 1):]


def even_prompt_mix(h, w_in_bf16, bt, s5p, cmpp, w_buf):
    t = h.shape[0] // bt
    u, q, rows, kvw, kvs_b, kvw_b, gates = inproj_even(h, w_in_bf16, jnp.arange(h.shape[0]) % t)
    seq = lambda a: a.reshape(bt, t, a.shape[-1])
    feat = KV_GROUPS * HEAD_DIM
    y_s5, s5_state = s5_scan(seq(u), jnp.zeros((bt, S5_GROUPS, S5_STATE, 2), jnp.float32), s5p, S5_CHUNK)
    rows = seq(rows)
    kc = compress_prompt(rows[..., 0:feat], compress_params(cmpp[0], cmpp[1], cmpp[2]))
    vc = compress_prompt(rows[..., feat:2 * feat], compress_params(cmpp[3], cmpp[4], cmpp[5]))
    y_nsa = nsa_prompt(seq(q), seq(gates), kc, vc, seq(kvs_b), jnp.pad(seq(kvw_b), ((0, 0), (WINDOW, 0), (0, 0))))
    new_rows = rows.reshape(bt, t, 4, KV_GROUPS, HEAD_DIM)
    return (y_s5, y_nsa), s5_state, new_rows, last_rows(seq(kvw).reshape(bt, t, 2, KV_GROUPS, HEAD_DIM), w_buf)


def even_sample_mix(h, w_in_bf16, bt, s5_h0, pool, page_table, win_buf, s5p, cmpp):
    f32 = jnp.float32
    t = h.shape[0] // bt
    pos = page_table.shape[1] * PAGE_SIZE + jnp.arange(h.shape[0]) % t
    u, q, rows, kvw, _, _, gates = inproj_even(h, w_in_bf16, pos)
    seq = lambda a: a.reshape(bt, t, a.shape[-1])
    feat = KV_GROUPS * HEAD_DIM
    y_s5, s5_state = s5_scan(seq(u), s5_h0.astype(f32), s5p, t)
    rows, kvw = seq(rows), seq(kvw)
    pool_t = pool.astype(f32).transpose(0, 2, 3, 4, 1).reshape(pool.shape[0], 4 * feat, PAGE_SIZE)
    kc, vc = compress_sample(pool_t, page_table, rows[..., 0:feat], rows[..., feat:2 * feat],
                             compress_sample_params(cmpp[0], cmpp[1], cmpp[2]),
                             compress_sample_params(cmpp[3], cmpp[4], cmpp[5]))
    w_buf = win_buf.shape[1]
    win_f = win_buf.astype(f32)
    y_nsa = nsa_sample(q.reshape(bt, t, N_HEADS, HEAD_DIM), gates.reshape(bt, t, N_HEADS, 3), kc, vc, pool_t,
                       page_table, rows[..., 2 * feat:3 * feat], rows[..., 3 * feat:4 * feat],
                       win_f.transpose(0, 2, 3, 4, 1).reshape(bt, 2 * feat, w_buf), kvw[..., 0:feat],
                       kvw[..., feat:2 * feat])
    new_rows = rows.reshape(bt, t, 4, KV_GROUPS, HEAD_DIM)
    win = jnp.concatenate([win_f, kvw.reshape(bt, t, 2, KV_GROUPS, HEAD_DIM)], axis=1)
    return (y_s5, y_nsa), s5_state, new_rows, win[:, t:]


def ssd_scan(x, dt, a, bm, cm, h0, chunk):
    bt, t, nh, p = x.shape
    nch = t // chunk
    r = nh // SSD_GROUPS
    tri = jnp.arange(chunk)[:, None] >= jnp.arange(chunk)[None, :]

    def to_chunks(v):
        return jnp.moveaxis(v.reshape((bt, nch, chunk) + v.shape[2:]), 1, 0)

    def step(h, inp):
        xc, dtc, bc, cc = inp
        cum = jnp.cumsum(dtc * a, axis=1)
        seg = cum[:, :, None, :] - cum[:, None, :, :]
        decay = jnp.exp(jnp.where(tri[None, :, :, None], seg, NEG)).reshape(bt, chunk, chunk, SSD_GROUPS, r)
        cb = jnp.einsum('btgn,bsgn->btsg', cc, bc)
        xg = xc.reshape(bt, chunk, SSD_GROUPS, r, p)
        dg = dtc.reshape(bt, chunk, SSD_GROUPS, r)
        w = cb[..., None] * decay * dg[:, None]
        y_intra = jnp.einsum('btsgr,bsgrp->btgrp', w, xg)
        hg = h.reshape(bt, SSD_GROUPS, r, p, SSD_STATE)
        y_inter = jnp.einsum('btgn,bgrpn->btgrp', cc, hg) * jnp.exp(cum).reshape(bt, chunk, SSD_GROUPS, r)[..., None]
        wt = (jnp.exp(cum[:, -1:, :] - cum) * dtc).reshape(bt, chunk, SSD_GROUPS, r)
        h_new = (hg * jnp.exp(cum[:, -1]).reshape(bt, SSD_GROUPS, r)[..., None, None]
                 + jnp.einsum('bsgr,bsgrp,bsgn->bgrpn', wt, xg, bc))
        return h_new.reshape(bt, nh, p, SSD_STATE), (y_intra + y_inter).reshape(bt, chunk, nh, p)

    h_fin, ys = lax.scan(step, h0, (to_chunks(x), to_chunks(dt), to_chunks(bm), to_chunks(cm)))
    return jnp.moveaxis(ys, 0, 1).reshape(bt, t, nh, p), h_fin


def gated_rmsnorm(y, z, g):
    v = y * jax.nn.silu(z)
    bt, t, _ = v.shape
    vg = v.reshape(bt, t, SSD_GROUPS, SSD_INNER // SSD_GROUPS)
    vg = vg * lax.rsqrt(jnp.mean(vg * vg, -1, keepdims=True) + RMS_EPS)
    return vg.reshape(bt, t, SSD_INNER) * g


def odd_prompt_mix(h, w_in_bf16, bsz, sc_w, sc_b, cv_w, cv_b, dt_bias, a_log, d_skip, norm_g):
    a = -jnp.exp(a_log.astype(jnp.float32))
    y_sc, xbc_c, dt, zg, new_sc, new_conv = inproj_odd_prompt(h, w_in_bf16, bsz, sc_w, sc_b, cv_w, cv_b, dt_bias)
    y, h_new = ssd_prompt(xbc_c, dt, a, zg, 0, d_skip, norm_g)
    return (y_sc, y), new_sc, new_conv, h_new


def odd_mix(z, sc_buf, conv_buf, h0, chunk, sc_w, sc_b, cv_w, cv_b, dt_bias, a_log, d_skip, norm_g):
    f32 = jnp.float32
    bt, t, _ = z.shape
    a = -jnp.exp(a_log.astype(f32))
    o1 = SC_DIM
    o2 = 2 * SC_DIM
    o3 = 3 * SC_DIM
    o4 = o3 + SSD_INNER
    o5 = o4 + SSD_CONV_DIM
    sc_h = z[..., :o1]
    sc_bg = z[..., o1:o2]
    sc_cg = z[..., o2:o3]
    zg = z[..., o3:o4]
    xbc = z[..., o4:o5]
    dt_raw = z[..., o5:]
    conv_sc, new_sc = causal_conv(sc_cg * sc_h, sc_buf.astype(f32), sc_w, sc_b)
    y_sc = sc_bg * conv_sc
    xbc_c, new_conv = causal_conv(xbc, conv_buf.astype(f32), cv_w, cv_b)
    xbc_c = jax.nn.silu(xbc_c)
    gn = SSD_GROUPS * SSD_STATE
    xs = xbc_c[..., :SSD_INNER].reshape(bt, t, SSD_HEADS, SSD_HEAD_DIM)
    bm = xbc_c[..., SSD_INNER:SSD_INNER + gn].reshape(bt, t, SSD_GROUPS, SSD_STATE)
    cm = xbc_c[..., SSD_INNER + gn:].reshape(bt, t, SSD_GROUPS, SSD_STATE)
    dt = jax.nn.softplus((dt_raw + dt_bias).astype(f32))
    y, h_new = ssd_scan(xs, dt, a, bm, cm, h0.astype(f32), chunk)
    y = (y + d_skip[:, None] * xs).reshape(bt, t, SSD_INNER)
    y = gated_rmsnorm(y, zg, norm_g)
    return (y_sc, y), new_sc, new_conv, h_new


def moe_ffn(x, logits, w_gu_bf16, w_down_bf16):
    n, d = x.shape
    top_v, top_i = lax.top_k(logits, TOP_K)
    gate = jax.nn.softmax(top_v, axis=-1)
    flat_e = top_i.reshape(-1)
    blk = 128
    assert (TOP_K * n) % blk == 0
    onehot = jax.nn.one_hot(flat_e, N_EXPERTS, dtype=jnp.float32).reshape(-1, blk, N_EXPERTS)
    tri = (jnp.arange(blk)[:, None] >= jnp.arange(blk)[None, :]).astype(jnp.float32)
    local = jnp.einsum('ij,bjk->bik', tri, onehot)
    block_total = local[:, -1, :]
    block_off = jnp.cumsum(block_total, axis=0) - block_total
    incl = (local + block_off[:, None, :]).reshape(-1, N_EXPERTS)
    rank = jnp.take_along_axis(incl, flat_e[:, None], axis=1)[:, 0].astype(jnp.int32) - 1
    counts = jnp.sum(block_total, axis=0).astype(jnp.int32)
    padded = ((counts + ROW_TILE - 1) // ROW_TILE) * ROW_TILE
    pad_start = jnp.cumsum(padded) - padded
    dest = (pad_start[flat_e] + rank).astype(jnp.int32)
    n_tiles = (TOP_K * n) // ROW_TILE + N_EXPERTS
    rows = n_tiles * ROW_TILE
    row_token = jnp.zeros((rows,), jnp.int32).at[dest].set(jnp.arange(TOP_K * n, dtype=jnp.int32) // TOP_K,
                                                           unique_indices=True, mode='promise_in_bounds')
    tile_end = jnp.cumsum(padded) // ROW_TILE
    tile_expert = jnp.minimum(jnp.searchsorted(tile_end, jnp.arange(n_tiles), side='right'),
                              N_EXPERTS - 1).astype(jnp.int32)
    n_used = tile_end[-1:].astype(jnp.int32)
    xs = x.at[row_token].get(mode='promise_in_bounds')
    ys = grouped_ffn(xs, w_gu_bf16, w_down_bf16, tile_expert, n_used)
    dest = dest.reshape(n, TOP_K)
    y0 = ys.at[dest[:, 0]].get(mode='promise_in_bounds')
    y1 = ys.at[dest[:, 1]].get(mode='promise_in_bounds')
    return y0, y1, gate


def kernel(x_prompt, x_sample, state_s5, cache_nsa_kv, state_win_kv, state_sc_conv, state_ssd_conv, state_ssd,
           page_table, ln_g, ln_b, w_in_even, s5_lam_re, s5_lam_im, s5_log_dt, s5_b, s5_c, s5_d, s5_w_glu,
           nsa_wk1, nsa_wk2, nsa_pe_k, nsa_wv1, nsa_wv2, nsa_pe_v, w_out_even, ffn_w_gu, ffn_w_down,
           w_in_odd, sc_conv_w, sc_conv_b, ssd_conv_w, ssd_conv_b, ssd_dt_bias, ssd_a_log, ssd_d, ssd_norm_g,
           w_out_odd, moe_router, moe_router_b, moe_w_gu, moe_w_down):
    f32 = jnp.float32
    bf16 = jnp.bfloat16
    bp, tp, d = x_prompt.shape
    bs, ts, _ = x_sample.shape
    n_p = bp * tp
    n_s = bs * ts
    w_buf = state_win_kv.shape[2]
    streams = [x_prompt.astype(f32).reshape(n_p, d), x_sample.astype(f32).reshape(n_s, d)]

    def flat(parts, n_rows):
        return [p.reshape(n_rows, p.shape[-1]) for p in parts]

    def out_proj(parts, w_out, width, h, g, b, router=None):
        w = w_out.astype(bf16)
        return matmul(flat(parts, h.shape[0]), [w[:width], w[width:]], ln=(h, g, b), router=router)

    def single_expert(n_rows):
        n_tiles = pl.cdiv(n_rows, min(ROW_TILE, n_rows))
        return jnp.zeros((n_tiles,), jnp.int32), jnp.full((1,), n_tiles, jnp.int32)

    s5p = s5_params(s5_lam_re[0], s5_lam_im[0], s5_log_dt[0], s5_b[0], s5_c[0], s5_d[0], s5_w_glu[0])
    cmpp = (nsa_wk1[0], nsa_wk2[0], nsa_pe_k[0], nsa_wv1[0], nsa_wv2[0], nsa_pe_v[0])
    w_in = w_in_even[0].astype(bf16)
    mix_p, s5_p, kv_p, win_p = even_prompt_mix(streams[0], w_in, bp, s5p, cmpp, w_buf)
    mix_s, s5_s, kv_s, win_s = even_sample_mix(streams[1], w_in, bs, state_s5[0], cache_nsa_kv[0], page_table,
                                               state_win_kv[0], s5p, cmpp)
    streams = [out_proj(mix, w_out_even[0], S5_DIM, h, ln_g[0, 0], ln_b[0, 0])
               for mix, h in zip((mix_p, mix_s), streams)]
    w_gu, w_down = to_bf16(ffn_w_gu), to_bf16(ffn_w_down)
    streams = [grouped_ffn(h, w_gu, w_down, *single_expert(h.shape[0]), ln=(ln_g[0, 1], ln_b[0, 1]))
               for h in streams]

    oddp = (sc_conv_w[0], sc_conv_b[0], ssd_conv_w[0], ssd_conv_b[0], ssd_dt_bias[0],
            ssd_a_log[0], ssd_d[0], ssd_norm_g[0])
    w_in = w_in_odd[0].astype(bf16)
    mix_p, scc_p, sdc_p, ssd_p = odd_prompt_mix(streams[0], w_in, bp, *oddp)
    zs = matmul([streams[1]], [w_in]).reshape(bs, ts, -1)
    mix_s, scc_s, sdc_s, ssd_s = odd_mix(zs, state_sc_conv[0], state_ssd_conv[0], state_ssd[0], ts, *oddp)
    outs = [out_proj(mix, w_out_odd[0], SC_DIM, h, ln_g[1, 0], ln_b[1, 0], router=(moe_router[0], moe_router_b[0]))
            for mix, h in zip((mix_p, mix_s), streams)]
    h = jnp.concatenate([o[0] for o in outs], axis=0)
    logits = jnp.concatenate([o[1] for o in outs], axis=0)[:, :N_EXPERTS]
    y0, y1, gate = moe_ffn(h, logits, to_bf16(moe_w_gu[0]), to_bf16(moe_w_down[0]))
    h = moe_combine_ln(h, y0, y1, gate, ln_g[1, 1], ln_b[1, 1])

    hp = h[:n_p].reshape(bp, tp, d)
    hs = h[n_p:].reshape(bs, ts, d)
    st = lambda a, ref: a[None].astype(ref.dtype)
    return (hp.astype(x_prompt.dtype), hs.astype(x_sample.dtype),
            st(s5_p, state_s5), st(s5_s, state_s5),
            st(kv_p, cache_nsa_kv), st(kv_s, cache_nsa_kv),
            st(win_p, state_win_kv), st(win_s, state_win_kv),
            st(scc_p, state_sc_conv), st(scc_s, state_sc_conv),
            st(sdc_p, state_ssd_conv), st(sdc_s, state_ssd_conv),
            st(ssd_p, state_ssd), st(ssd_s, state_ssd))
```

```python
import functools
import math

import jax
import jax.numpy as jnp
from jax import lax
from jax.experimental import pallas as pl
from jax.experimental.pallas import tpu as pltpu

D_MODEL = 1024
DEPTH = 2
ALPHA = (2.0 * DEPTH) ** 0.25
LN_EPS = 1e-5
RMS_EPS = 1e-5
NEG = -1e30

S5_DIM = D_MODEL // 2
S5_GROUP = 16
S5_GROUPS = S5_DIM // S5_GROUP
S5_STATE = 64

HEAD_DIM = 64
N_HEADS = (D_MODEL // 2) // HEAD_DIM
KV_GROUPS = 2
HEADS_PER_GROUP = N_HEADS // KV_GROUPS
CMP_STRIDE = 16
CMP_LEN = 2 * CMP_STRIDE
SEL_BLOCK = 64
N_SEL = 16
WINDOW = 512
Q_BLOCK = 128
ROPE_THETA = 500000.0
ROT_DIM = HEAD_DIM // 4
FORCE = 1e4
NSA_Q = N_HEADS * HEAD_DIM
NSA_KV = 2 * KV_GROUPS * HEAD_DIM

SC_DIM = D_MODEL // 2
SC_WIDTH = 3
SSD_HEAD_DIM = 64
SSD_HEADS = 16
SSD_INNER = SSD_HEADS * SSD_HEAD_DIM
SSD_GROUPS = 4
SSD_STATE = 128
SSD_CONV = 4
SSD_CONV_DIM = SSD_INNER + 2 * SSD_GROUPS * SSD_STATE
SSD_CHUNK = 128

D_FF = 2816
N_EXPERTS = 8
TOP_K = 2

VMEM_LIMIT_BYTES = 56 * 1024 * 1024
LANES = 128
S5_N = S5_GROUPS * S5_STATE
S5_LT = S5_N // LANES
S5_CHUNK = 256
SEL_TILE = 1024
QK_SCALE = HEAD_DIM ** -0.5 * math.log2(math.e)
REMOVED = -3e38
PAGE_SIZE = 128
PAGES_PER_STEP = 64
NEW_PAD = 128
CAST_ROWS = 512
CAST_SPLIT = 4
TAIL = 8
ROW_TILE = 512
FF_TILE = D_FF // 2


def _cparams(*sem):
    return pltpu.CompilerParams(dimension_semantics=sem, vmem_limit_bytes=VMEM_LIMIT_BYTES)


def _deepnorm(resid, update, g, b):
    y = ALPHA * resid + update
    mu = jnp.mean(y, axis=-1, keepdims=True)
    yc = y - mu
    var = jnp.mean(yc * yc, axis=-1, keepdims=True)
    return yc * lax.rsqrt(var + LN_EPS) * g + b


def _mm_kernel(*refs, n_in, fuse_ln, router):
    xs, ws = refs[0:n_in], refs[n_in:2 * n_in]
    o_ref = refs[-2] if router else refs[-1]
    acc = None
    for x_ref, w_ref in zip(xs, ws):
        part = jnp.dot(x_ref[...].astype(jnp.bfloat16), w_ref[...], preferred_element_type=jnp.float32)
        acc = part if acc is None else acc + part
    if fuse_ln:
        r_ref, g_ref, b_ref = refs[2 * n_in:2 * n_in + 3]
        acc = _deepnorm(r_ref[...], acc, g_ref[...], b_ref[...])
    o_ref[...] = acc
    if router:
        wr_ref, br_ref = refs[-4], refs[-3]
        xparts = _split3(acc)
        logits = br_ref[...]
        for i in range(2):
            for j in range(2 - i):
                logits = logits + jnp.dot(xparts[i], wr_ref[j], preferred_element_type=jnp.float32)
        refs[-1][...] = logits


def matmul(xs, ws_bf16, ln=None, router=None):
    m = xs[0].shape[0]
    n = ws_bf16[0].shape[1]
    tile = min(ROW_TILE, m)
    row = lambda width: pl.BlockSpec((tile, width), lambda i: (i, 0))
    fixed = lambda shape: pl.BlockSpec(shape, lambda i: (0, 0), pipeline_mode=pl.Buffered(1))
    in_specs = [row(x.shape[1]) for x in xs] + [fixed(w.shape) for w in ws_bf16]
    args = list(xs) + list(ws_bf16)
    if ln is not None:
        resid, g, b = ln
        in_specs += [row(n), fixed((1, n)), fixed((1, n))]
        args += [resid, g.reshape(1, n), b.reshape(1, n)]
    out_specs, out_shape = row(n), jax.ShapeDtypeStruct((m, n), jnp.float32)
    if router is not None:
        w_r, b_r = router
        pad = LANES - w_r.shape[1]
        w_parts = jnp.stack(_split3(jnp.pad(w_r.astype(jnp.float32), ((0, 0), (0, pad)))))
        in_specs += [pl.BlockSpec((3, n, LANES), lambda i: (0, 0, 0), pipeline_mode=pl.Buffered(1)),
                     fixed((1, LANES))]
        args += [w_parts, jnp.pad(b_r.astype(jnp.float32), (0, pad)).reshape(1, LANES)]
        out_specs, out_shape = [out_specs, row(LANES)], [out_shape, jax.ShapeDtypeStruct((m, LANES), jnp.float32)]
    return pl.pallas_call(
        functools.partial(_mm_kernel, n_in=len(xs), fuse_ln=ln is not None, router=router is not None),
        grid=(pl.cdiv(m, tile),),
        in_specs=in_specs,
        out_specs=out_specs,
        out_shape=out_shape,
        compiler_params=_cparams("parallel"),
        name="matmul",
    )(*args)


def _combine_kernel(h_ref, y0_ref, y1_ref, gate_ref, g_ref, b_ref, o_ref):
    gate = gate_ref[...]
    f = gate[:, 0:1] * y0_ref[...] + gate[:, 1:2] * y1_ref[...]
    o_ref[...] = _deepnorm(h_ref[...], f, g_ref[...], b_ref[...])


def moe_combine_ln(h, y0, y1, gate, g, b):
    m, n = h.shape
    row = lambda width: pl.BlockSpec((ROW_TILE, width), lambda i: (i, 0))
    fixed = pl.BlockSpec((1, n), lambda i: (0, 0))
    return pl.pallas_call(
        _combine_kernel,
        grid=(pl.cdiv(m, ROW_TILE),),
        in_specs=[row(n), row(n), row(n), row(TOP_K), fixed, fixed],
        out_specs=row(n),
        out_shape=jax.ShapeDtypeStruct((m, n), jnp.float32),
        compiler_params=_cparams("parallel"),
        name="moe_combine_ln",
    )(h, y0, y1, gate, g.reshape(1, n), b.reshape(1, n))


def _cast_kernel(*refs):
    o_ref = refs[-1]
    o_ref[...] = jnp.concatenate([r[...].astype(o_ref.dtype) for r in refs[:-1]], axis=1)


def to_bf16(w):
    shape = w.shape
    w2 = w.reshape(-1, shape[-1])
    rows, cols = w2.shape
    split = CAST_SPLIT if cols % (CAST_SPLIT * LANES) == 0 else 1
    out = pl.pallas_call(
        _cast_kernel,
        grid=(pl.cdiv(rows, CAST_ROWS),),
        in_specs=[pl.BlockSpec((CAST_ROWS, cols // split), lambda i, c=c: (i, c)) for c in range(split)],
        out_specs=pl.BlockSpec((CAST_ROWS, cols), lambda i: (i, 0)),
        out_shape=jax.ShapeDtypeStruct((rows, cols), jnp.bfloat16),
        compiler_params=_cparams("parallel"),
        name="to_bf16",
    )(*([w2] * split))
    return out.reshape(shape)


def _ffn_kernel(te_ref, nt_ref, x_ref, wg_ref, wu_ref, wd_ref, *rest, fuse_ln):
    o_ref = rest[-1]
    t = pl.program_id(0)
    j = pl.program_id(1)

    @pl.when(t < nt_ref[0])
    def _():
        x = x_ref[...].astype(jnp.bfloat16)
        g = jnp.dot(x, wg_ref[...], preferred_element_type=jnp.float32)
        u = jnp.dot(x, wu_ref[...], preferred_element_type=jnp.float32)
        h = (g * jax.nn.sigmoid(g) * u).astype(jnp.bfloat16)
        part = jnp.dot(h, wd_ref[...], preferred_element_type=jnp.float32)

        @pl.when(j == 0)
        def _():
            o_ref[...] = part

        @pl.when(j > 0)
        def _():
            if fuse_ln:
                o_ref[...] = _deepnorm(x_ref[...], o_ref[...] + part, rest[0][...], rest[1][...])
            else:
                o_ref[...] += part

    @pl.when(jnp.logical_and(t >= nt_ref[0], j == 0))
    def _():
        o_ref[...] = jnp.zeros_like(o_ref)


def grouped_ffn(x, w_gu_bf16, w_down_bf16, tile_expert, n_tiles_used, ln=None):
    r, d = x.shape
    nf = D_FF // FF_TILE
    assert nf == 2
    tile = min(ROW_TILE, r)
    n_tiles = pl.cdiv(r, tile)
    in_specs = [
        pl.BlockSpec((tile, d), lambda t, j, te, nt: (t, 0)),
        pl.BlockSpec((None, d, FF_TILE), lambda t, j, te, nt: (te[t], 0, j)),
        pl.BlockSpec((None, d, FF_TILE), lambda t, j, te, nt: (te[t], 0, nf + j)),
        pl.BlockSpec((None, FF_TILE, d), lambda t, j, te, nt: (te[t], j, 0)),
    ]
    args = [tile_expert, n_tiles_used, x, w_gu_bf16, w_gu_bf16, w_down_bf16]
    if ln is not None:
        in_specs += [pl.BlockSpec((1, d), lambda t, j, te, nt: (0, 0))] * 2
        args += [ln[0].reshape(1, d), ln[1].reshape(1, d)]
    grid_spec = pltpu.PrefetchScalarGridSpec(
        num_scalar_prefetch=2,
        grid=(n_tiles, nf),
        in_specs=in_specs,
        out_specs=pl.BlockSpec((tile, d), lambda t, j, te, nt: (t, 0)),
    )
    return pl.pallas_call(
        functools.partial(_ffn_kernel, fuse_ln=ln is not None),
        grid_spec=grid_spec,
        out_shape=jax.ShapeDtypeStruct((r, d), jnp.float32),
        compiler_params=_cparams("parallel", "arbitrary"),
        name="grouped_ffn",
    )(*args)


def _s5_kernel(u_ref, perm_ref, h0r_ref, h0i_ref, ar_ref, ai_ref, bbr_ref, bbi_ref, cr_ref, ci_ref, d_ref, wglu_ref,
               y_ref, hro_ref, hio_ref, bur, bui, sr, si, hr, hi, *, chains, chunk):
    j = pl.program_id(0)

    @pl.when(j == 0)
    def _():
        hr[...] = h0r_ref[...]
        hi[...] = h0i_ref[...]

    rows_n = chains * chunk
    u = u_ref[...].reshape(rows_n, S5_DIM)
    to_tc = perm_ref[...]
    ub = jnp.dot(to_tc, u.astype(jnp.bfloat16), preferred_element_type=jnp.float32).astype(jnp.bfloat16)
    hd, hn = S5_DIM // 2, S5_N // 2

    def b_proj(w_ref):
        return jnp.concatenate([jnp.dot(ub[:, h * hd:(h + 1) * hd], w_ref[h * hd:(h + 1) * hd, h * hn:(h + 1) * hn],
                                        preferred_element_type=jnp.float32) for h in range(2)], axis=1)

    bu_r = b_proj(bbr_ref)
    bu_i = b_proj(bbi_ref)
    for k in range(S5_LT):
        bur[k] = bu_r[:, k * LANES:(k + 1) * LANES]
        bui[k] = bu_i[:, k * LANES:(k + 1) * LANES]
    ar = [jnp.broadcast_to(ar_ref[:, k * LANES:(k + 1) * LANES], (chains, LANES)) for k in range(S5_LT)]
    ai = [jnp.broadcast_to(ai_ref[:, k * LANES:(k + 1) * LANES], (chains, LANES)) for k in range(S5_LT)]

    def body(t, carry):
        rows = pl.ds(t * chains, chains)
        out = []
        for k in range(S5_LT):
            xr, xi = carry[2 * k], carry[2 * k + 1]
            nr = ar[k] * xr - ai[k] * xi + bur[k, rows, :]
            ni = ar[k] * xi + ai[k] * xr + bui[k, rows, :]
            sr[k, rows, :] = nr
            si[k, rows, :] = ni
            out += [nr, ni]
        return tuple(out)

    init = []
    for k in range(S5_LT):
        init += [hr[:, k * LANES:(k + 1) * LANES], hi[:, k * LANES:(k + 1) * LANES]]
    fin = lax.fori_loop(0, chunk, body, tuple(init), unroll=2)
    xr = jnp.concatenate(fin[0::2], axis=1)
    xi = jnp.concatenate(fin[1::2], axis=1)
    hr[...] = xr
    hi[...] = xi
    hro_ref[...] = xr
    hio_ref[...] = xi
    s_r = jnp.concatenate([sr[k] for k in range(S5_LT)], axis=1).astype(jnp.bfloat16)
    s_i = jnp.concatenate([si[k] for k in range(S5_LT)], axis=1).astype(jnp.bfloat16)
    y = jnp.concatenate(
        [jnp.dot(s_r[:, h * hn:(h + 1) * hn], cr_ref[h * hn:(h + 1) * hn, h * hd:(h + 1) * hd],
                 preferred_element_type=jnp.float32)
         - jnp.dot(s_i[:, h * hn:(h + 1) * hn], ci_ref[h * hn:(h + 1) * hn, h * hd:(h + 1) * hd],
                   preferred_element_type=jnp.float32) for h in range(2)], axis=1)
    y = sum(lax.dot_general(to_tc, part, (((0,), (0,)), ((), ())), preferred_element_type=jnp.float32)
            for part in _split3(y)) + d_ref[...] * u
    z = jax.nn.gelu(y)
    gate = jax.nn.sigmoid(jnp.dot(z.astype(jnp.bfloat16), wglu_ref[...], preferred_element_type=jnp.float32))
    y_ref[...] = (z * gate).reshape(chains, chunk, S5_DIM)


def s5_params(lam_re, lam_im, log_dt, b, c, d, w_glu):
    f32 = jnp.float32
    dt = jnp.exp(log_dt.astype(f32))[:, None]
    mag = jnp.exp(lam_re * dt)
    ang = lam_im * dt
    ab_re = mag * jnp.cos(ang)
    ab_im = mag * jnp.sin(ang)
    den = lam_re * lam_re + lam_im * lam_im
    nr = ab_re - 1.0
    coef_re = (nr * lam_re + ab_im * lam_im) / den
    coef_im = (ab_im * lam_re - nr * lam_im) / den
    b_re = b[..., 0].astype(f32)
    b_im = b[..., 1].astype(f32)
    bb_re = coef_re[..., None] * b_re - coef_im[..., None] * b_im
    bb_im = coef_re[..., None] * b_im + coef_im[..., None] * b_re
    eye = jnp.eye(S5_GROUPS, dtype=f32)
    bbr = jnp.einsum('gnk,gh->gkhn', bb_re, eye).reshape(S5_DIM, S5_N).astype(jnp.bfloat16)
    bbi = jnp.einsum('gnk,gh->gkhn', bb_im, eye).reshape(S5_DIM, S5_N).astype(jnp.bfloat16)
    cr = jnp.einsum('gkn,gh->gnhk', c[..., 0].astype(f32), eye).reshape(S5_N, S5_DIM).astype(jnp.bfloat16)
    ci = jnp.einsum('gkn,gh->gnhk', c[..., 1].astype(f32), eye).reshape(S5_N, S5_DIM).astype(jnp.bfloat16)
    return (ab_re.reshape(1, S5_N), ab_im.reshape(1, S5_N), bbr, bbi, cr, ci,
            d.astype(f32).reshape(1, S5_DIM), w_glu.astype(jnp.bfloat16))


def s5_scan(u, h0, params, chunk):
    chains, t, _ = u.shape
    ar, ai, bbr, bbi, cr, ci, d, wglu = params
    h0r = h0[..., 0].reshape(chains, S5_N)
    h0i = h0[..., 1].reshape(chains, S5_N)
    full = lambda shape: pl.BlockSpec(shape, lambda j: (0,) * len(shape))
    rows = chains * chunk
    r = jnp.arange(rows)
    to_tc = (r[None, :] == (r[:, None] % chains) * chunk + r[:, None] // chains).astype(jnp.bfloat16)
    y, hr, hi = pl.pallas_call(
        functools.partial(_s5_kernel, chains=chains, chunk=chunk),
        grid=(t // chunk,),
        in_specs=[pl.BlockSpec((chains, chunk, S5_DIM), lambda j: (0, j, 0)), full((rows, rows)),
                  full((chains, S5_N)), full((chains, S5_N)), full((1, S5_N)), full((1, S5_N)),
                  full((S5_DIM, S5_N)), full((S5_DIM, S5_N)), full((S5_N, S5_DIM)), full((S5_N, S5_DIM)),
                  full((1, S5_DIM)), full((S5_DIM, S5_DIM))],
        out_specs=[pl.BlockSpec((chains, chunk, S5_DIM), lambda j: (0, j, 0)),
                   full((chains, S5_N)), full((chains, S5_N))],
        out_shape=[jax.ShapeDtypeStruct((chains, t, S5_DIM), jnp.float32),
                   jax.ShapeDtypeStruct((chains, S5_N), jnp.float32),
                   jax.ShapeDtypeStruct((chains, S5_N), jnp.float32)],
        scratch_shapes=[pltpu.VMEM((S5_LT, rows, LANES), jnp.float32)] * 4
                       + [pltpu.VMEM((chains, S5_N), jnp.float32)] * 2,
        compiler_params=_cparams("arbitrary"),
        name="s5_scan",
    )(u, to_tc, h0r, h0i, ar, ai, bbr, bbi, cr, ci, d, wglu)
    new_state = jnp.stack([hr.reshape(chains, S5_GROUPS, S5_STATE), hi.reshape(chains, S5_GROUPS, S5_STATE)],
                          axis=-1)
    return y, new_state


def _dot_nt(a, b):
    return lax.dot_general(a, b, (((1,), (1,)), ((), ())), preferred_element_type=jnp.float32)


def _split3(x):
    hi = x.astype(jnp.bfloat16)
    rem = x - hi.astype(jnp.float32)
    mid = rem.astype(jnp.bfloat16)
    lo = (rem - mid.astype(jnp.float32)).astype(jnp.bfloat16)
    return hi, mid, lo


def _softmax_rows(s, mask):
    s = jnp.where(mask, s, NEG)
    m = jnp.max(s, axis=-1, keepdims=True)
    p = jnp.exp2(s - m)
    inv = jnp.where(m > 0.5 * NEG, 1.0 / jnp.sum(p, axis=-1, keepdims=True), 0.0)
    return p * inv


def _nsa_prompt_kernel(q_ref, gate_ref, kc_ref, vc_ref, ks_ref, vs_ref, kw_ref, vw_ref, o_ref, *, n_cmp, n_blk):
    f32, bf16 = jnp.float32, jnp.bfloat16
    r4 = HEADS_PER_GROUP
    n_cpad = kc_ref.shape[0]
    start = pl.program_id(1) * Q_BLOCK
    q = q_ref[...] * QK_SCALE
    gate = gate_ref[...]
    lane = lax.broadcasted_iota(jnp.int32, (Q_BLOCK, LANES), 1)
    qpos = start + lax.broadcasted_iota(jnp.int32, (Q_BLOCK, 1), 0)
    n_idx = lax.broadcasted_iota(jnp.int32, (Q_BLOCK, n_cpad), 1)
    cmask = (((n_idx * CMP_STRIDE + (CMP_LEN - 1)) <= qpos) & (n_idx < n_cmp))[None]
    ratio = SEL_BLOCK // CMP_STRIDE
    gsum = (lax.broadcasted_iota(jnp.int32, (n_blk, n_cpad), 1) // ratio
            == lax.broadcasted_iota(jnp.int32, (n_blk, n_cpad), 0)).astype(bf16)
    blk = lax.broadcasted_iota(jnp.int32, (n_blk, Q_BLOCK), 0)
    blk_f = blk.astype(f32)
    jq = (start + lax.broadcasted_iota(jnp.int32, (n_blk, Q_BLOCK), 1)) // SEL_BLOCK
    force = jnp.where((blk == 0) | (blk == jq) | (blk == jq - 1), FORCE, 0.0)
    qgs, o_cs, sels = [], [], []
    for g in range(KV_GROUPS):
        keep = (lane < HEAD_DIM) if g == 0 else (lane >= HEAD_DIM)
        parts = []
        for r in range(r4):
            h = r4 * g + r
            tile = q[:, (h // 2) * LANES:(h // 2 + 1) * LANES]
            if h % 2 != g:
                tile = pltpu.roll(tile, HEAD_DIM, axis=1)
            parts.append(jnp.where(keep, tile, 0.0))
        qg = jnp.concatenate(parts, axis=0).astype(bf16)
        qgs.append(qg)

        p_c = _softmax_rows(_dot_nt(qg, kc_ref[...]).reshape(r4, Q_BLOCK, n_cpad), cmask)
        o_cs.append(jnp.dot(p_c.reshape(r4 * Q_BLOCK, n_cpad).astype(bf16), vc_ref[...],
                            preferred_element_type=f32).reshape(r4, Q_BLOCK, LANES))
        psum = p_c[0] + p_c[1] + p_c[2] + p_c[3]
        imp_t = sum(_dot_nt(gsum, part) for part in _split3(psum))

        score = jnp.where(blk <= jq, imp_t + force, NEG)
        sel_t = jnp.zeros((n_blk, Q_BLOCK), f32)
        for _ in range(min(N_SEL, n_blk)):
            m = jnp.max(score, axis=0, keepdims=True)
            idx = jnp.min(jnp.where(score == m, blk_f, float(n_blk)), axis=0, keepdims=True)
            hit = blk_f == idx
            sel_t = jnp.where(hit & (m > 0.5 * NEG), 1.0, sel_t)
            score = jnp.where(hit, REMOVED, score)
        sels.append(sel_t.T)

    n_full = start // SEL_TILE
    expand0 = (lax.broadcasted_iota(jnp.int32, (n_blk, SEL_TILE), 0)
               == lax.broadcasted_iota(jnp.int32, (n_blk, SEL_TILE), 1) // SEL_BLOCK).astype(bf16)

    def tile_update(i, carry, causal):
        off = pl.multiple_of(i * SEL_TILE, SEL_TILE)
        k = ks_ref[pl.ds(off, SEL_TILE), :]
        v = vs_ref[pl.ds(off, SEL_TILE), :]
        vlane = lax.broadcasted_iota(jnp.int32, (SEL_TILE, LANES), 1)
        v_ones = [jnp.where((vlane < HEAD_DIM) == (g == 0), v, jnp.ones_like(v)) for g in range(KV_GROUPS)]
        out = []
        for g in range(KV_GROUPS):
            m_run, l_run, acc = carry[g]
            s_t = _dot_nt(qgs[g], k).reshape(r4, Q_BLOCK, SEL_TILE)
            shifted = pltpu.roll(sels[g], (n_blk - i * (SEL_TILE // SEL_BLOCK)) % n_blk, axis=1).astype(bf16)
            mk = jnp.dot(shifted, expand0, preferred_element_type=f32) > 0.5
            if causal:
                kpos = i * SEL_TILE + lax.broadcasted_iota(jnp.int32, (Q_BLOCK, SEL_TILE), 1)
                mk = mk & (kpos <= qpos)
            s_t = jnp.where(mk[None], s_t, NEG)
            m_new = jnp.maximum(m_run, jnp.max(s_t, axis=-1, keepdims=True))
            alpha = jnp.exp2(m_run - m_new)
            p = jnp.exp2((s_t - m_new).astype(bf16))
            pv = jnp.dot(p.reshape(r4 * Q_BLOCK, SEL_TILE), v_ones[g], preferred_element_type=f32)
            pv = pv.reshape(r4, Q_BLOCK, LANES)
            l_new = alpha * l_run + pv[:, :, (1 - g) * HEAD_DIM:(1 - g) * HEAD_DIM + 1]
            out.append((m_new, l_new, alpha * acc + pv))
        return tuple(out)

    init = (jnp.full((r4, Q_BLOCK, 1), NEG, f32), jnp.zeros((r4, Q_BLOCK, 1), f32),
            jnp.zeros((r4, Q_BLOCK, LANES), f32))
    carry = lax.fori_loop(0, n_full, lambda i, c: tile_update(i, c, False), (init, init))
    fin = tile_update(n_full, carry, True)

    n_win = WINDOW + Q_BLOCK
    woff = pl.multiple_of(start, Q_BLOCK)
    kwin = kw_ref[pl.ds(woff, n_win), :]
    vwin = vw_ref[pl.ds(woff, n_win), :]
    wpos = start - WINDOW + lax.broadcasted_iota(jnp.int32, (Q_BLOCK, n_win), 1)
    wmask = ((wpos <= qpos) & (wpos > qpos - WINDOW) & (wpos >= 0))[None]
    heads = [None] * N_HEADS
    for g in range(KV_GROUPS):
        m_fin, l_fin, acc = fin[g]
        o_s = acc * jnp.where(m_fin > 0.5 * NEG, 1.0 / l_fin, 0.0)
        p_w = _softmax_rows(_dot_nt(qgs[g], kwin).reshape(r4, Q_BLOCK, n_win), wmask)
        o_w = jnp.dot(p_w.reshape(r4 * Q_BLOCK, n_win).astype(bf16), vwin,
                      preferred_element_type=f32).reshape(r4, Q_BLOCK, LANES)
        for r in range(r4):
            h = r4 * g + r
            heads[h] = (gate[:, 3 * h:3 * h + 1] * o_cs[g][r] + gate[:, 3 * h + 1:3 * h + 2] * o_s[r]
                        + gate[:, 3 * h + 2:3 * h + 3] * o_w[r])

    tiles = []
    for j in range(N_HEADS // 2):
        even, odd = heads[2 * j], heads[2 * j + 1]
        if j // 2 == 0:
            tiles.append(jnp.where(lane < HEAD_DIM, even, pltpu.roll(odd, HEAD_DIM, axis=1)))
        else:
            tiles.append(jnp.where(lane < HEAD_DIM, pltpu.roll(even, HEAD_DIM, axis=1), odd))
    o_ref[...] = jnp.concatenate(tiles, axis=1)


def nsa_prompt(q, gates, kc, vc, kvs, kvw_pad):
    b, t, _ = q.shape
    n_cpad = kc.shape[1]
    kern = functools.partial(_nsa_prompt_kernel, n_cmp=t // CMP_STRIDE - 1, n_blk=t // SEL_BLOCK)
    whole = lambda rows, c=0: pl.BlockSpec((None, rows, LANES), lambda i, j: (i, 0, c))
    return pl.pallas_call(
        kern,
        grid=(b, t // Q_BLOCK),
        in_specs=[pl.BlockSpec((None, Q_BLOCK, NSA_Q), lambda i, j: (i, j, 0)),
                  pl.BlockSpec((None, Q_BLOCK, 3 * N_HEADS), lambda i, j: (i, j, 0)),
                  whole(n_cpad), whole(n_cpad), whole(t, 0), whole(t, 1), whole(t + WINDOW, 0),
                  whole(t + WINDOW, 1)],
        out_specs=pl.BlockSpec((None, Q_BLOCK, NSA_Q), lambda i, j: (i, j, 0)),
        out_shape=jax.ShapeDtypeStruct((b, t, NSA_Q), jnp.float32),
        compiler_params=_cparams("parallel", "arbitrary"),
        name="nsa_prompt",
    )(q, gates, kc, vc, kvs, kvs, kvw_pad, kvw_pad)


def _compress_kernel(ch_ref, pet_ref, peb_ref, w1t_ref, w1b_ref, w2_ref, o_ref):
    bf16 = jnp.bfloat16
    ch = ch_ref[...]
    n_ch = ch.shape[0]
    a = jnp.dot((ch + pet_ref[...]).astype(bf16), w1t_ref[...], preferred_element_type=jnp.float32)
    b = jnp.dot((ch + peb_ref[...]).astype(bf16), w1b_ref[...], preferred_element_type=jnp.float32)
    pre = a + pltpu.roll(b, n_ch - 1, axis=0)
    o_ref[...] = jnp.dot(jax.nn.gelu(pre).astype(bf16), w2_ref[...],
                         preferred_element_type=jnp.float32).astype(o_ref.dtype)


def compress_params(w1, w2, pe):
    f32 = jnp.float32
    eye = jnp.eye(KV_GROUPS, dtype=f32)
    w1r = w1.astype(f32).reshape(2, CMP_STRIDE, HEAD_DIM, HEAD_DIM)
    big = jnp.einsum('hjde,gk->hjgdke', w1r, eye).reshape(2, CMP_STRIDE * LANES, LANES).astype(jnp.bfloat16)
    w2bd = jnp.einsum('de,gk->gdke', w2.astype(f32), eye).reshape(LANES, LANES).astype(jnp.bfloat16)
    per = pe.astype(f32).reshape(2, CMP_STRIDE, 1, HEAD_DIM)
    pe_rows = jnp.broadcast_to(per, (2, CMP_STRIDE, KV_GROUPS, HEAD_DIM)).reshape(2, 1, CMP_STRIDE * LANES)
    return pe_rows[0], pe_rows[1], big[0], big[1], w2bd


def compress_prompt(x, params):
    b, t, _ = x.shape
    n_ch = t // CMP_STRIDE
    ch = x.reshape(b, n_ch, CMP_STRIDE * LANES)
    pet, peb, w1t, w1b, w2bd = params
    full = lambda shape: pl.BlockSpec(shape, lambda i: (0,) * len(shape))
    return pl.pallas_call(
        _compress_kernel,
        grid=(b,),
        in_specs=[pl.BlockSpec((None, n_ch, CMP_STRIDE * LANES), lambda i: (i, 0, 0)),
                  full((1, CMP_STRIDE * LANES)), full((1, CMP_STRIDE * LANES)),
                  full((CMP_STRIDE * LANES, LANES)), full((CMP_STRIDE * LANES, LANES)), full((LANES, LANES))],
        out_specs=pl.BlockSpec((None, n_ch, LANES), lambda i: (i, 0, 0)),
        out_shape=jax.ShapeDtypeStruct((b, n_ch, LANES), jnp.bfloat16),
        compiler_params=_cparams("parallel"),
        name="compress_prompt",
    )(ch, pet, peb, w1t, w1b, w2bd)


def _cmp_sample_kernel(pt_ref, *refs, n_pages):
    f32, bf16 = jnp.float32, jnp.bfloat16
    pp = PAGES_PER_STEP
    pages = refs[0:pp]
    (perm_ref, newk_ref, newv_ref, wk_ref, wv_ref, ck_ref, cv_ref, w2k_ref, w2v_ref,
     kc_ref, vc_ref, slab_k, slab_v) = refs[pp:]
    s = pl.program_id(1)
    cpp = PAGE_SIZE // CMP_STRIDE
    base = pl.multiple_of(s * (pp * cpp), pp * cpp)
    for half, slab in enumerate((slab_k, slab_v)):
        for i in range(pp):
            page = pages[i][half * LANES:(half + 1) * LANES, :].astype(bf16)
            rows = _dot_nt(perm_ref[...], page)
            for j in range(CMP_STRIDE):
                slab[j, pl.ds(base + i * cpp, cpp), :] = rows[j * cpp:(j + 1) * cpp, :]

    @pl.when(s == pl.num_programs(1) - 1)
    def _():
        n_ch = n_pages * (PAGE_SIZE // CMP_STRIDE)
        row = lax.broadcasted_iota(jnp.int32, (n_ch, LANES), 0)
        for slab, new_ref, w_ref, c_ref, w2_ref, o_ref in ((slab_k, newk_ref, wk_ref, ck_ref, w2k_ref, kc_ref),
                                                           (slab_v, newv_ref, wv_ref, cv_ref, w2v_ref, vc_ref)):
            ch = jnp.concatenate([slab[j] for j in range(CMP_STRIDE)], axis=1).astype(bf16)
            ab = jnp.dot(ch, w_ref[...], preferred_element_type=f32)
            b_new = jnp.dot(new_ref[...].astype(bf16), w_ref[...], preferred_element_type=f32)[0:1, LANES:]
            nxt = pltpu.roll(ab[:, LANES:], n_ch - 1, axis=0)
            nxt = jnp.where(row == n_ch - 1, b_new, nxt)
            pre = ab[:, :LANES] + nxt + c_ref[...]
            o_ref[...] = jnp.dot(jax.nn.gelu(pre).astype(bf16), w2_ref[...],
                                 preferred_element_type=f32).astype(o_ref.dtype)


def compress_sample_params(w1, w2, pe):
    pet, peb, w1t, w1b, w2bd = compress_params(w1, w2, pe)
    hp = lax.Precision.HIGHEST
    const = (jnp.dot(pet, w1t.astype(jnp.float32), precision=hp)
             + jnp.dot(peb, w1b.astype(jnp.float32), precision=hp))
    return jnp.concatenate([w1t, w1b], axis=1), const, w2bd


def _page_spec(i, pair):
    return pl.BlockSpec((None, 2 * LANES, PAGE_SIZE),
                        lambda b, s, pt: (pt[b, PAGES_PER_STEP * s + i], pair, 0))


def _per_seq(shape):
    return pl.BlockSpec((None,) + shape, lambda b, s, pt: (b, 0, 0))


def compress_sample(pool_t, page_table, new_k, new_v, pk, pv):
    bsz, n_pages = page_table.shape
    pp = PAGES_PER_STEP
    n_ch = n_pages * (PAGE_SIZE // CMP_STRIDE)
    t_new = new_k.shape[1]

    def chunk_rows(x):
        x = jnp.pad(x, ((0, 0), (0, CMP_STRIDE - t_new), (0, 0))).reshape(bsz, 1, CMP_STRIDE * LANES)
        return jnp.pad(x, ((0, 0), (0, 7), (0, 0)))

    full = lambda shape: pl.BlockSpec(shape, lambda b, s, pt: (0,) * len(shape))
    r = jnp.arange(PAGE_SIZE)
    cpp = PAGE_SIZE // CMP_STRIDE
    perm = (r[None, :] == (r[:, None] % cpp) * CMP_STRIDE + r[:, None] // cpp).astype(jnp.bfloat16)
    wk, ck, w2k = pk
    wv, cv, w2v = pv
    grid_spec = pltpu.PrefetchScalarGridSpec(
        num_scalar_prefetch=1,
        grid=(bsz, n_pages // pp),
        in_specs=[_page_spec(i, 0) for i in range(pp)]
                 + [full((PAGE_SIZE, PAGE_SIZE)), _per_seq((8, CMP_STRIDE * LANES)),
                    _per_seq((8, CMP_STRIDE * LANES)),
                    full((CMP_STRIDE * LANES, 2 * LANES)), full((CMP_STRIDE * LANES, 2 * LANES)),
                    full((1, LANES)), full((1, LANES)), full((LANES, LANES)), full((LANES, LANES))],
        out_specs=[_per_seq((n_ch, LANES)), _per_seq((n_ch, LANES))],
        scratch_shapes=[pltpu.VMEM((CMP_STRIDE, n_ch, LANES), jnp.float32)] * 2,
    )
    return pl.pallas_call(
        functools.partial(_cmp_sample_kernel, n_pages=n_pages),
        grid_spec=grid_spec,
        out_shape=[jax.ShapeDtypeStruct((bsz, n_ch, LANES), jnp.bfloat16)] * 2,
        compiler_params=_cparams("parallel", "arbitrary"),
        name="compress_sample",
    )(page_table, *([pool_t] * pp), perm, chunk_rows(new_k), chunk_rows(new_v), wk, wv, ck, cv, w2k, w2v)


def _nsa_sample_kernel(pt_ref, *refs, n_pages, t_new, w_buf):
    f32, bf16 = jnp.float32, jnp.bfloat16
    pp = PAGES_PER_STEP
    q_ref, gate_ref, kc_ref, vc_ref = refs[0:4]
    pages = refs[4:4 + pp]
    (ksn_ref, vsn_ref, win_ref, kwn_ref, vwn_ref, o_ref,
     sel_scr, exp_scr, oc_scr, m_scr, l_scr, acc_scr) = refs[4 + pp:]
    r4, g2 = HEADS_PER_GROUP, KV_GROUPS
    n_rows = g2 * r4 * t_new
    past_len = n_pages * PAGE_SIZE
    n_cmp = kc_ref.shape[0]
    n_bpad = sel_scr.shape[1]
    tile = pp * PAGE_SIZE
    s = pl.program_id(1)
    qall = q_ref[...]
    qpos = past_len + lax.broadcasted_iota(jnp.int32, (n_rows, 1), 0) % t_new

    def grouped(x):
        return x.reshape(g2, 1, t_new, x.shape[-1])

    @pl.when(s == 0)
    def _():
        s_c = _dot_nt(qall, kc_ref[...])
        n_idx = lax.broadcasted_iota(jnp.int32, (n_rows, n_cmp), 1)
        p_c = _softmax_rows(s_c, (n_idx * CMP_STRIDE + (CMP_LEN - 1)) <= qpos)
        oc_scr[...] = jnp.dot(p_c.astype(bf16), vc_ref[...], preferred_element_type=f32)
        psum = jnp.sum(p_c.reshape(g2, r4, t_new, n_cmp), axis=1).reshape(g2 * t_new, n_cmp)
        psum = jnp.concatenate([psum, jnp.zeros((LANES - g2 * t_new, n_cmp), f32)], axis=0)
        p_hi = psum.astype(bf16)
        rem = psum - p_hi.astype(f32)
        p_mid = rem.astype(bf16)
        p_lo = (rem - p_mid.astype(f32)).astype(bf16)
        ratio = SEL_BLOCK // CMP_STRIDE
        gsum = (lax.broadcasted_iota(jnp.int32, (n_bpad, n_cmp), 1) // ratio
                == lax.broadcasted_iota(jnp.int32, (n_bpad, n_cmp), 0)).astype(bf16)
        imp_t = _dot_nt(gsum, p_hi) + _dot_nt(gsum, p_mid) + _dot_nt(gsum, p_lo)
        blk = lax.broadcasted_iota(jnp.int32, (n_bpad, LANES), 0)
        jq = (past_len + lax.broadcasted_iota(jnp.int32, (n_bpad, LANES), 1) % t_new) // SEL_BLOCK
        forced = (blk == 0) | (blk == jq) | (blk == jq - 1)
        score = jnp.where(blk <= jq, imp_t + jnp.where(forced, FORCE, 0.0), NEG)
        blk_f = blk.astype(f32)
        sel_t = jnp.zeros((n_bpad, LANES), f32)
        for _ in range(N_SEL):
            m = jnp.max(score, axis=0, keepdims=True)
            idx = jnp.min(jnp.where(score == m, blk_f, float(n_bpad)), axis=0, keepdims=True)
            hit = blk_f == idx
            sel_t = jnp.where(hit & (m > 0.5 * NEG), 1.0, sel_t)
            score = jnp.where(hit, REMOVED, score)
        sel = jnp.concatenate([sel_t[k * LANES:(k + 1) * LANES].T for k in range(n_bpad // LANES)], axis=1)
        sel_scr[...] = sel[0:g2 * t_new]
        exp_scr[...] = (lax.broadcasted_iota(jnp.int32, (LANES, tile), 0)
                        == lax.broadcasted_iota(jnp.int32, (LANES, tile), 1) // SEL_BLOCK).astype(bf16)
        m_scr[...] = jnp.full(m_scr.shape, NEG, f32)
        l_scr[...] = jnp.zeros(l_scr.shape, f32)
        acc_scr[...] = jnp.zeros(acc_scr.shape, f32)

    def online_update(s_t, mk, v, v_feature_major):
        n = s_t.shape[-1]
        s4 = jnp.where(mk, s_t.reshape(g2, r4, t_new, n), NEG)
        m_run = m_scr[...].reshape(g2, r4, t_new, 1)
        m_new = jnp.maximum(m_run, jnp.max(s4, axis=-1, keepdims=True))
        alpha = jnp.exp2(m_run - m_new)
        p = jnp.exp2(s4 - m_new)
        l_new = alpha * l_scr[...].reshape(g2, r4, t_new, 1) + jnp.sum(p, axis=-1, keepdims=True)
        pb = p.reshape(n_rows, n).astype(bf16)
        pv = _dot_nt(pb, v) if v_feature_major else jnp.dot(pb, v, preferred_element_type=f32)
        m_scr[...] = m_new.reshape(n_rows, 1)
        l_scr[...] = l_new.reshape(n_rows, 1)
        acc_scr[...] = alpha.reshape(n_rows, 1) * acc_scr[...] + pv

    kt = jnp.concatenate([r[0:LANES, :] for r in pages], axis=1).astype(bf16)
    vt = jnp.concatenate([r[LANES:2 * LANES, :] for r in pages], axis=1).astype(bf16)
    shifted = pltpu.roll(sel_scr[...], (n_bpad - s * (tile // SEL_BLOCK)) % n_bpad, axis=1)
    picked = jnp.dot(shifted[:, 0:LANES].astype(bf16), exp_scr[...], preferred_element_type=f32)
    online_update(jnp.dot(qall, kt, preferred_element_type=f32), grouped(picked) > 0.5, vt, True)

    @pl.when(s == pl.num_programs(1) - 1)
    def _():
        new_blk = past_len // SEL_BLOCK
        kidx = lax.broadcasted_iota(jnp.int32, (n_rows, NEW_PAD), 1)
        causal = ((past_len + kidx) <= qpos) & (kidx < t_new)
        picked_new = sel_scr[:, new_blk:new_blk + 1]
        mk = (grouped(picked_new) > 0.5) & causal.reshape(g2, r4, t_new, NEW_PAD)
        online_update(_dot_nt(qall, ksn_ref[...]), mk, vsn_ref[...], False)
        o_s = acc_scr[...] * jnp.where(m_scr[...] > 0.5 * NEG, 1.0 / l_scr[...], 0.0)

        n_win = w_buf + NEW_PAD
        kw_t = win_ref[0:LANES, :].astype(bf16)
        vw_t = win_ref[LANES:2 * LANES, :].astype(bf16)
        widx = lax.broadcasted_iota(jnp.int32, (n_rows, n_win), 1)
        wpos = past_len - w_buf + widx
        wmask = (wpos <= qpos) & (wpos > qpos - WINDOW) & (wpos >= 0) & (widx < w_buf + t_new)
        s_w = jnp.concatenate([jnp.dot(qall, kw_t, preferred_element_type=f32), _dot_nt(qall, kwn_ref[...])],
                              axis=1)
        p_w = _softmax_rows(s_w, wmask).astype(bf16)
        o_w = (_dot_nt(p_w[:, 0:w_buf], vw_t)
               + jnp.dot(p_w[:, w_buf:], vwn_ref[...], preferred_element_type=f32))
        gate = gate_ref[...]
        o_ref[...] = gate[:, 0:1] * oc_scr[...] + gate[:, 1:2] * o_s + gate[:, 2:3] * o_w


def nsa_sample(q, gates, kc, vc, pool_t, page_table, ks_new, vs_new, win, kw_new, vw_new):
    f32, bf16 = jnp.float32, jnp.bfloat16
    bsz, t_new = q.shape[0], q.shape[1]
    n_pages = page_table.shape[1]
    pp = PAGES_PER_STEP
    w_buf = win.shape[2]
    r4, g2 = HEADS_PER_GROUP, KV_GROUPS
    n_rows = g2 * r4 * t_new
    past_len = n_pages * PAGE_SIZE
    assert past_len % SEL_BLOCK == 0 and t_new <= SEL_BLOCK and past_len >= w_buf and n_pages % pp == 0
    n_sel = past_len // SEL_BLOCK + 1
    n_bpad = -(-n_sel // LANES) * LANES
    eye = jnp.eye(g2, dtype=f32)
    qg = q.reshape(bsz, t_new, g2, r4, HEAD_DIM).transpose(0, 2, 3, 1, 4) * QK_SCALE
    qall = jnp.einsum('bgrqd,gk->bgrqkd', qg, eye).reshape(bsz, n_rows, LANES).astype(bf16)
    gall = gates.reshape(bsz, t_new, g2, r4, 3).transpose(0, 2, 3, 1, 4).reshape(bsz, n_rows, 3)
    pad_rows = lambda x: jnp.pad(x, ((0, 0), (0, NEW_PAD - t_new), (0, 0))).astype(bf16)
    n_cmp = kc.shape[1]
    grid_spec = pltpu.PrefetchScalarGridSpec(
        num_scalar_prefetch=1,
        grid=(bsz, n_pages // pp),
        in_specs=[_per_seq((n_rows, LANES)), _per_seq((n_rows, 3)), _per_seq((n_cmp, LANES)),
                  _per_seq((n_cmp, LANES))]
                 + [_page_spec(i, 1) for i in range(pp)]
                 + [_per_seq((NEW_PAD, LANES)), _per_seq((NEW_PAD, LANES)), _per_seq((2 * LANES, w_buf)),
                    _per_seq((NEW_PAD, LANES)), _per_seq((NEW_PAD, LANES))],
        out_specs=_per_seq((n_rows, LANES)),
        scratch_shapes=[pltpu.VMEM((g2 * t_new, n_bpad), f32), pltpu.VMEM((LANES, pp * PAGE_SIZE), bf16),
                        pltpu.VMEM((n_rows, LANES), f32),
                        pltpu.VMEM((n_rows, 1), f32), pltpu.VMEM((n_rows, 1), f32),
                        pltpu.VMEM((n_rows, LANES), f32)],
    )
    o = pl.pallas_call(
        functools.partial(_nsa_sample_kernel, n_pages=n_pages, t_new=t_new, w_buf=w_buf),
        grid_spec=grid_spec,
        out_shape=jax.ShapeDtypeStruct((bsz, n_rows, LANES), f32),
        compiler_params=_cparams("parallel", "arbitrary"),
        name="nsa_sample",
    )(page_table, qall, gall, kc, vc, *([pool_t] * pp), pad_rows(ks_new), pad_rows(vs_new), win,
      pad_rows(kw_new), pad_rows(vw_new))
    o = jnp.einsum('bgrqkd,gk->bqgrd', o.reshape(bsz, g2, r4, t_new, g2, HEAD_DIM), eye)
    return o.reshape(bsz, t_new, NSA_Q)


def _ssd_kernel(x_ref, b_ref, c_ref, dt_ref, a_ref, za_ref, zb_ref, dskip_ref, ng_ref, y_ref, hout_ref, h_scr, *,
                chunk):
    f32, bf16 = jnp.float32, jnp.bfloat16
    n_l = chunk
    hpg = SSD_HEADS // SSD_GROUPS
    gw = hpg * SSD_HEAD_DIM
    j = pl.program_id(1)

    @pl.when(j == 0)
    def _():
        h_scr[...] = jnp.zeros(h_scr.shape, f32)

    x = x_ref[...]
    dt = dt_ref[...]
    tri_b = (lax.broadcasted_iota(jnp.int32, (n_l, n_l), 0) >= lax.broadcasted_iota(jnp.int32, (n_l, n_l), 1))
    tri = tri_b.astype(bf16)
    cum = sum(jnp.dot(tri, part, preferred_element_type=f32) for part in _split3(dt * a_ref[...]))
    cum_t = cum.T
    dt_t = dt.T
    ecum = jnp.exp(cum)
    clast = cum[n_l - 1:n_l, :]
    wt = jnp.exp(clast - cum) * dt
    elast = jnp.exp(clast)
    lane = lax.broadcasted_iota(jnp.int32, (n_l, LANES), 1)
    low = lane < SSD_HEAD_DIM

    def pair(v, h0):
        return jnp.where(low[:v.shape[0]], v[:, h0:h0 + 1], v[:, h0 + 1:h0 + 2])

    tiles = []
    for g in range(SSD_GROUPS):
        bg = b_ref[:, g * SSD_STATE:(g + 1) * SSD_STATE]
        cgb = c_ref[:, g * SSD_STATE:(g + 1) * SSD_STATE].astype(bf16)
        bgt = bg.T.astype(bf16)
        cb = jnp.dot(cgb, bgt, preferred_element_type=f32)
        hg = h_scr[g]
        y_inter = jnp.dot(cgb, hg.astype(bf16), preferred_element_type=f32)
        xw, dec = [], []
        for pr in range(hpg // 2):
            h0 = hpg * g + 2 * pr
            xt = x[:, (h0 // 2) * LANES:(h0 // 2 + 1) * LANES]
            acc = None
            for k in range(2):
                h = h0 + k
                seg = cum[:, h:h + 1] - cum_t[h:h + 1, :]
                w = cb * jnp.exp(jnp.where(tri_b, seg, NEG)) * dt_t[h:h + 1, :]
                xm = jnp.where(low if k == 0 else jnp.logical_not(low), xt, 0.0).astype(bf16)
                part = jnp.dot(w.astype(bf16), xm, preferred_element_type=f32)
                acc = part if acc is None else acc + part
            tiles.append(acc + y_inter[:, pr * LANES:(pr + 1) * LANES] * pair(ecum, h0))
            xw.append((xt * pair(wt, h0)).astype(bf16))
            dec.append(pair(elast, h0))
        h_scr[g] = (hg * jnp.concatenate(dec, axis=1)
                    + jnp.dot(bgt, jnp.concatenate(xw, axis=1), preferred_element_type=f32))
    y = jnp.concatenate(tiles, axis=1) + dskip_ref[...] * x
    zg = jnp.concatenate([za_ref[...], zb_ref[...]], axis=1)
    v = y * (zg * jax.nn.sigmoid(zg))
    outs = []
    for g in range(SSD_GROUPS):
        vg = v[:, g * gw:(g + 1) * gw]
        outs.append(vg * lax.rsqrt(jnp.mean(vg * vg, axis=-1, keepdims=True) + RMS_EPS))
    y_ref[...] = jnp.concatenate(outs, axis=1) * ng_ref[...]

    @pl.when(j == pl.num_programs(1) - 1)
    def _():
        hout_ref[...] = h_scr[...]


def ssd_prompt(xbc, dt, a, z, zg_offset, d_skip, norm_g):
    f32 = jnp.float32
    bsz, t, _ = xbc.shape
    hpg = SSD_HEADS // SSD_GROUPS
    gn = SSD_GROUPS * SSD_STATE
    half = SSD_INNER // 2
    assert zg_offset % half == 0
    dt_p = jnp.pad(dt, ((0, 0), (0, 0), (0, LANES - SSD_HEADS)))
    a_p = jnp.pad(a.astype(f32), (0, LANES - SSD_HEADS)).reshape(1, LANES)
    dsk = jnp.repeat(d_skip.astype(f32), SSD_HEAD_DIM).reshape(1, SSD_INNER)
    blk = lambda w, c: pl.BlockSpec((None, SSD_CHUNK, w), lambda b, j: (b, j, c))
    full = lambda shape: pl.BlockSpec(shape, lambda b, j: (0,) * len(shape))
    state_spec = pl.BlockSpec((None, SSD_GROUPS, SSD_STATE, hpg * SSD_HEAD_DIM), lambda b, j: (b, 0, 0, 0))
    y, h = pl.pallas_call(
        functools.partial(_ssd_kernel, chunk=SSD_CHUNK),
        grid=(bsz, t // SSD_CHUNK),
        in_specs=[blk(SSD_INNER, 0), blk(gn, SSD_INNER // gn), blk(gn, SSD_INNER // gn + 1), blk(LANES, 0),
                  full((1, LANES)), blk(half, zg_offset // half), blk(half, zg_offset // half + 1),
                  full((1, SSD_INNER)), full((1, SSD_INNER))],
        out_specs=[blk(SSD_INNER, 0), state_spec],
        out_shape=[jax.ShapeDtypeStruct((bsz, t, SSD_INNER), f32),
                   jax.ShapeDtypeStruct((bsz, SSD_GROUPS, SSD_STATE, hpg * SSD_HEAD_DIM), f32)],
        scratch_shapes=[pltpu.VMEM((SSD_GROUPS, SSD_STATE, hpg * SSD_HEAD_DIM), f32)],
        compiler_params=_cparams("parallel", "arbitrary"),
        name="ssd_prompt",
    )(xbc, xbc, xbc, dt_p, a_p, z, z, dsk, norm_g.astype(f32).reshape(1, SSD_INNER))
    h = h.reshape(bsz, SSD_GROUPS, SSD_STATE, hpg, SSD_HEAD_DIM).transpose(0, 1, 3, 4, 2)
    return y, h.reshape(bsz, SSD_HEADS, SSD_HEAD_DIM, SSD_STATE)


def _inproj_even_kernel(x_ref, w_ref, c_ref, sa_ref, sb_ref,
                        u_ref, q_ref, rows_ref, kvw_ref, kvsb_ref, kvwb_ref, g_ref):
    bf16 = jnp.bfloat16
    z = jnp.dot(x_ref[...].astype(bf16), w_ref[...], preferred_element_type=jnp.float32)
    cos, s_up, s_down = c_ref[...], sa_ref[...], sb_ref[...]

    def rot(t):
        return (t * cos + pltpu.roll(t, LANES - ROT_DIM // 2, axis=1) * s_up
                + pltpu.roll(t, ROT_DIM // 2, axis=1) * s_down)

    tile = lambda k: z[:, k * LANES:(k + 1) * LANES]
    q0 = S5_DIM // LANES
    kv0 = q0 + NSA_Q // LANES
    u_ref[...] = z[:, 0:S5_DIM]
    q_ref[...] = jnp.concatenate([rot(tile(q0 + k)) for k in range(NSA_Q // LANES)], axis=1)
    kc, vc, ks, vs, kw, vw = (rot(tile(kv0)), tile(kv0 + 1), rot(tile(kv0 + 2)), tile(kv0 + 3),
                              rot(tile(kv0 + 4)), tile(kv0 + 5))
    rows_ref[...] = jnp.concatenate([kc, vc, ks, vs], axis=1)
    kvw = jnp.concatenate([kw, vw], axis=1)
    kvw_ref[...] = kvw
    kvwb_ref[...] = kvw.astype(bf16)
    kvsb_ref[...] = jnp.concatenate([ks, vs], axis=1).astype(bf16)
    g0 = (kv0 + 6) * LANES
    g_ref[...] = jax.nn.sigmoid(z[:, g0:g0 + 3 * N_HEADS])


def rope_tables(pos):
    half = ROT_DIM // 2
    inv = ROPE_THETA ** (-jnp.arange(half, dtype=jnp.float32) * 2.0 / ROT_DIM)
    ang = pos.astype(jnp.float32)[:, None] * inv[None, :]
    d = jnp.arange(LANES) % HEAD_DIM
    cos = jnp.take(jnp.cos(ang), d % half, axis=1)
    sin = jnp.take(jnp.sin(ang), d % half, axis=1)
    return (jnp.where(d < ROT_DIM, cos, 1.0), jnp.where(d < half, -sin, 0.0),
            jnp.where((d >= half) & (d < ROT_DIM), sin, 0.0))


def inproj_even(h, w_bf16, pos):
    f32, bf16 = jnp.float32, jnp.bfloat16
    n, d = h.shape
    tile = min(ROW_TILE, n)
    row = lambda w: pl.BlockSpec((tile, w), lambda i: (i, 0))
    fixed = lambda shape: pl.BlockSpec(shape, lambda i: (0, 0), pipeline_mode=pl.Buffered(1))
    widths = [(S5_DIM, f32), (NSA_Q, f32), (2 * NSA_KV, f32), (NSA_KV, f32), (NSA_KV, bf16), (NSA_KV, bf16),
              (3 * N_HEADS, f32)]
    return pl.pallas_call(
        _inproj_even_kernel,
        grid=(pl.cdiv(n, tile),),
        in_specs=[row(d), fixed(w_bf16.shape), row(LANES), row(LANES), row(LANES)],
        out_specs=[row(w) for w, _ in widths],
        out_shape=[jax.ShapeDtypeStruct((n, w), dt) for w, dt in widths],
        compiler_params=_cparams("parallel"),
        name="inproj_even",
    )(h, w_bf16, *rope_tables(pos))


def _causal_conv_tile(x, tail, w_ref, b_ref, width):
    row = lax.broadcasted_iota(jnp.int32, (TAIL, x.shape[1]), 0)
    acc = b_ref[...] + w_ref[width - 1:width, :] * x
    for k in range(1, width):
        xs = pltpu.roll(x, k, axis=0)
        head = jnp.where(row < k, pltpu.roll(tail, k, axis=0), xs[0:TAIL])
        xs = jnp.concatenate([head, xs[TAIL:]], axis=0)
        acc = acc + w_ref[width - 1 - k:width - k, :] * xs
    return acc


def _inproj_odd_kernel(x_ref, w_ref, scw_ref, scb_ref, cvw_ref, cvb_ref, dtb_ref,
                       ysc_ref, xbc_ref, dt_ref, zg_ref, tsc_ref, tx_ref, tail_sc, tail_x, *, tiles_per_seq):
    @pl.when(pl.program_id(0) % tiles_per_seq == 0)
    def _():
        tail_sc[...] = jnp.zeros(tail_sc.shape, jnp.float32)
        tail_x[...] = jnp.zeros(tail_x.shape, jnp.float32)

    z = jnp.dot(x_ref[...].astype(jnp.bfloat16), w_ref[...], preferred_element_type=jnp.float32)
    o_zg = 3 * SC_DIM
    o_x = o_zg + SSD_INNER
    o_dt = o_x + SSD_CONV_DIM
    n = z.shape[0]
    prod = z[:, 2 * SC_DIM:3 * SC_DIM] * z[:, 0:SC_DIM]
    ysc_ref[...] = z[:, SC_DIM:2 * SC_DIM] * _causal_conv_tile(prod, tail_sc[...], scw_ref, scb_ref, SC_WIDTH)
    xbc = z[:, o_x:o_dt]
    c = _causal_conv_tile(xbc, tail_x[...], cvw_ref, cvb_ref, SSD_CONV)
    xbc_ref[...] = c * jax.nn.sigmoid(c)
    dt_ref[...] = jax.nn.softplus(z[:, o_dt:o_dt + SSD_HEADS] + dtb_ref[...])
    zg_ref[...] = z[:, o_zg:o_x]
    tail_sc[...] = prod[n - TAIL:n]
    tail_x[...] = xbc[n - TAIL:n]
    tsc_ref[...] = prod[n - TAIL:n]
    tx_ref[...] = xbc[n - TAIL:n]


def inproj_odd_prompt(h, w_bf16, bsz, sc_w, sc_b, cv_w, cv_b, dt_bias):
    f32 = jnp.float32
    n_rows, d = h.shape
    t = n_rows // bsz
    assert t % ROW_TILE == 0
    tps = t // ROW_TILE
    row = lambda w: pl.BlockSpec((ROW_TILE, w), lambda i: (i, 0))
    fixed = lambda shape: pl.BlockSpec(shape, lambda i: (0,) * len(shape), pipeline_mode=pl.Buffered(1))
    last = lambda w: pl.BlockSpec((None, TAIL, w), lambda i: (i // tps, 0, 0))
    ysc, xbc, dt, zg, tsc, tx = pl.pallas_call(
        functools.partial(_inproj_odd_kernel, tiles_per_seq=tps),
        grid=(n_rows // ROW_TILE,),
        in_specs=[row(d), fixed(w_bf16.shape), fixed((SC_WIDTH, SC_DIM)), fixed((1, SC_DIM)),
                  fixed((SSD_CONV, SSD_CONV_DIM)), fixed((1, SSD_CONV_DIM)), fixed((1, SSD_HEADS))],
        out_specs=[row(SC_DIM), row(SSD_CONV_DIM), row(SSD_HEADS), row(SSD_INNER), last(SC_DIM), last(SSD_CONV_DIM)],
        out_shape=[jax.ShapeDtypeStruct((n_rows, SC_DIM), f32), jax.ShapeDtypeStruct((n_rows, SSD_CONV_DIM), f32),
                   jax.ShapeDtypeStruct((n_rows, SSD_HEADS), f32), jax.ShapeDtypeStruct((n_rows, SSD_INNER), f32),
                   jax.ShapeDtypeStruct((bsz, TAIL, SC_DIM), f32), jax.ShapeDtypeStruct((bsz, TAIL, SSD_CONV_DIM), f32)],
        scratch_shapes=[pltpu.VMEM((TAIL, SC_DIM), f32), pltpu.VMEM((TAIL, SSD_CONV_DIM), f32)],
        compiler_params=_cparams("arbitrary"),
        name="inproj_odd",
    )(h, w_bf16, sc_w.astype(f32), sc_b.astype(f32).reshape(1, SC_DIM), cv_w.astype(f32),
      cv_b.astype(f32).reshape(1, SSD_CONV_DIM), dt_bias.astype(f32).reshape(1, SSD_HEADS))
    seq = lambda a: a.reshape(bsz, t, a.shape[-1])
    return (seq(ysc), seq(xbc), seq(dt), seq(zg),
            tsc[:, TAIL - (SC_WIDTH - 1):], tx[:, TAIL - (SSD_CONV - 1):])


def last_rows(x, n):
    t = x.shape[1]
    if t < n:
        x = jnp.pad(x, [(0, 0), (n - t, 0)] + [(0, 0)] * (x.ndim - 2))
    return x[:, x.shape[1] - n:]


def causal_conv(x, buf, w, b):
    t = x.shape[1]
    width = w.shape[0]
    xp = jnp.concatenate([buf, x], axis=1)
    y = b + sum(xp[:, j:j + t] * w[j] for j in range(width))
    return y, xp[:, xp.shape[1] - (width - 1):]


def even_prompt_mix(h, w_in_bf16, bt, s5p, cmpp, w_buf):
    t = h.shape[0] // bt
    u, q, rows, kvw, kvs_b, kvw_b, gates = inproj_even(h, w_in_bf16, jnp.arange(h.shape[0]) % t)
    seq = lambda a: a.reshape(bt, t, a.shape[-1])
    feat = KV_GROUPS * HEAD_DIM
    y_s5, s5_state = s5_scan(seq(u), jnp.zeros((bt, S5_GROUPS, S5_STATE, 2), jnp.float32), s5p, S5_CHUNK)
    rows = seq(rows)
    kc = compress_prompt(rows[..., 0:feat], compress_params(cmpp[0], cmpp[1], cmpp[2]))
    vc = compress_prompt(rows[..., feat:2 * feat], compress_params(cmpp[3], cmpp[4], cmpp[5]))
    y_nsa = nsa_prompt(seq(q), seq(gates), kc, vc, seq(kvs_b), jnp.pad(seq(kvw_b), ((0, 0), (WINDOW, 0), (0, 0))))
    new_rows = rows.reshape(bt, t, 4, KV_GROUPS, HEAD_DIM)
    return (y_s5, y_nsa), s5_state, new_rows, last_rows(seq(kvw).reshape(bt, t, 2, KV_GROUPS, HEAD_DIM), w_buf)


def even_sample_mix(h, w_in_bf16, bt, s5_h0, pool, page_table, win_buf, s5p, cmpp):
    f32 = jnp.float32
    t = h.shape[0] // bt
    pos = page_table.shape[1] * PAGE_SIZE + jnp.arange(h.shape[0]) % t
    u, q, rows, kvw, _, _, gates = inproj_even(h, w_in_bf16, pos)
    seq = lambda a: a.reshape(bt, t, a.shape[-1])
    feat = KV_GROUPS * HEAD_DIM
    y_s5, s5_state = s5_scan(seq(u), s5_h0.astype(f32), s5p, t)
    rows, kvw = seq(rows), seq(kvw)
    pool_t = pool.astype(f32).transpose(0, 2, 3, 4, 1).reshape(pool.shape[0], 4 * feat, PAGE_SIZE)
    kc, vc = compress_sample(pool_t, page_table, rows[..., 0:feat], rows[..., feat:2 * feat],
                             compress_sample_params(cmpp[0], cmpp[1], cmpp[2]),
                             compress_sample_params(cmpp[3], cmpp[4], cmpp[5]))
    w_buf = win_buf.shape[1]
    win_f = win_buf.astype(f32)
    y_nsa = nsa_sample(q.reshape(bt, t, N_HEADS, HEAD_DIM), gates.reshape(bt, t, N_HEADS, 3), kc, vc, pool_t,
                       page_table, rows[..., 2 * feat:3 * feat], rows[..., 3 * feat:4 * feat],
                       win_f.transpose(0, 2, 3, 4, 1).reshape(bt, 2 * feat, w_buf), kvw[..., 0:feat],
                       kvw[..., feat:2 * feat])
    new_rows = rows.reshape(bt, t, 4, KV_GROUPS, HEAD_DIM)
    win = jnp.concatenate([win_f, kvw.reshape(bt, t, 2, KV_GROUPS, HEAD_DIM)], axis=1)
    return (y_s5, y_nsa), s5_state, new_rows, win[:, t:]


def ssd_scan(x, dt, a, bm, cm, h0, chunk):
    bt, t, nh, p = x.shape
    nch = t // chunk
    r = nh // SSD_GROUPS
    tri = jnp.arange(chunk)[:, None] >= jnp.arange(chunk)[None, :]

    def to_chunks(v):
        return jnp.moveaxis(v.reshape((bt, nch, chunk) + v.shape[2:]), 1, 0)

    def step(h, inp):
        xc, dtc, bc, cc = inp
        cum = jnp.cumsum(dtc * a, axis=1)
        seg = cum[:, :, None, :] - cum[:, None, :, :]
        decay = jnp.exp(jnp.where(tri[None, :, :, None], seg, NEG)).reshape(bt, chunk, chunk, SSD_GROUPS, r)
        cb = jnp.einsum('btgn,bsgn->btsg', cc, bc)
        xg = xc.reshape(bt, chunk, SSD_GROUPS, r, p)
        dg = dtc.reshape(bt, chunk, SSD_GROUPS, r)
        w = cb[..., None] * decay * dg[:, None]
        y_intra = jnp.einsum('btsgr,bsgrp->btgrp', w, xg)
        hg = h.reshape(bt, SSD_GROUPS, r, p, SSD_STATE)
        y_inter = jnp.einsum('btgn,bgrpn->btgrp', cc, hg) * jnp.exp(cum).reshape(bt, chunk, SSD_GROUPS, r)[..., None]
        wt = (jnp.exp(cum[:, -1:, :] - cum) * dtc).reshape(bt, chunk, SSD_GROUPS, r)
        h_new = (hg * jnp.exp(cum[:, -1]).reshape(bt, SSD_GROUPS, r)[..., None, None]
                 + jnp.einsum('bsgr,bsgrp,bsgn->bgrpn', wt, xg, bc))
        return h_new.reshape(bt, nh, p, SSD_STATE), (y_intra + y_inter).reshape(bt, chunk, nh, p)

    h_fin, ys = lax.scan(step, h0, (to_chunks(x), to_chunks(dt), to_chunks(bm), to_chunks(cm)))
    return jnp.moveaxis(ys, 0, 1).reshape(bt, t, nh, p), h_fin


def gated_rmsnorm(y, z, g):
    v = y * jax.nn.silu(z)
    bt, t, _ = v.shape
    vg = v.reshape(bt, t, SSD_GROUPS, SSD_INNER // SSD_GROUPS)
    vg = vg * lax.rsqrt(jnp.mean(vg * vg, -1, keepdims=True) + RMS_EPS)
    return vg.reshape(bt, t, SSD_INNER) * g


def odd_prompt_mix(h, w_in_bf16, bsz, sc_w, sc_b, cv_w, cv_b, dt_bias, a_log, d_skip, norm_g):
    a = -jnp.exp(a_log.astype(jnp.float32))
    y_sc, xbc_c, dt, zg, new_sc, new_conv = inproj_odd_prompt(h, w_in_bf16, bsz, sc_w, sc_b, cv_w, cv_b, dt_bias)
    y, h_new = ssd_prompt(xbc_c, dt, a, zg, 0, d_skip, norm_g)
    return (y_sc, y), new_sc, new_conv, h_new


def odd_mix(z, sc_buf, conv_buf, h0, chunk, sc_w, sc_b, cv_w, cv_b, dt_bias, a_log, d_skip, norm_g):
    f32 = jnp.float32
    bt, t, _ = z.shape
    a = -jnp.exp(a_log.astype(f32))
    o1 = SC_DIM
    o2 = 2 * SC_DIM
    o3 = 3 * SC_DIM
    o4 = o3 + SSD_INNER
    o5 = o4 + SSD_CONV_DIM
    sc_h = z[..., :o1]
    sc_bg = z[..., o1:o2]
    sc_cg = z[..., o2:o3]
    zg = z[..., o3:o4]
    xbc = z[..., o4:o5]
    dt_raw = z[..., o5:]
    conv_sc, new_sc = causal_conv(sc_cg * sc_h, sc_buf.astype(f32), sc_w, sc_b)
    y_sc = sc_bg * conv_sc
    xbc_c, new_conv = causal_conv(xbc, conv_buf.astype(f32), cv_w, cv_b)
    xbc_c = jax.nn.silu(xbc_c)
    gn = SSD_GROUPS * SSD_STATE
    xs = xbc_c[..., :SSD_INNER].reshape(bt, t, SSD_HEADS, SSD_HEAD_DIM)
    bm = xbc_c[..., SSD_INNER:SSD_INNER + gn].reshape(bt, t, SSD_GROUPS, SSD_STATE)
    cm = xbc_c[..., SSD_INNER + gn:].reshape(bt, t, SSD_GROUPS, SSD_STATE)
    dt = jax.nn.softplus((dt_raw + dt_bias).astype(f32))
    y, h_new = ssd_scan(xs, dt, a, bm, cm, h0.astype(f32), chunk)
    y = (y + d_skip[:, None] * xs).reshape(bt, t, SSD_INNER)
    y = gated_rmsnorm(y, zg, norm_g)
    return (y_sc, y), new_sc, new_conv, h_new


def moe_ffn(x, logits, w_gu_bf16, w_down_bf16):
    n, d = x.shape
    top_v, top_i = lax.top_k(logits, TOP_K)
    gate = jax.nn.softmax(top_v, axis=-1)
    flat_e = top_i.reshape(-1)
    blk = 128
    assert (TOP_K * n) % blk == 0
    onehot = jax.nn.one_hot(flat_e, N_EXPERTS, dtype=jnp.float32).reshape(-1, blk, N_EXPERTS)
    tri = (jnp.arange(blk)[:, None] >= jnp.arange(blk)[None, :]).astype(jnp.float32)
    local = jnp.einsum('ij,bjk->bik', tri, onehot)
    block_total = local[:, -1, :]
    block_off = jnp.cumsum(block_total, axis=0) - block_total
    incl = (local + block_off[:, None, :]).reshape(-1, N_EXPERTS)
    rank = jnp.take_along_axis(incl, flat_e[:, None], axis=1)[:, 0].astype(jnp.int32) - 1
    counts = jnp.sum(block_total, axis=0).astype(jnp.int32)
    padded = ((counts + ROW_TILE - 1) // ROW_TILE) * ROW_TILE
    pad_start = jnp.cumsum(padded) - padded
    dest = (pad_start[flat_e] + rank).astype(jnp.int32)
    n_tiles = (TOP_K * n) // ROW_TILE + N_EXPERTS
    rows = n_tiles * ROW_TILE
    row_token = jnp.zeros((rows,), jnp.int32).at[dest].set(jnp.arange(TOP_K * n, dtype=jnp.int32) // TOP_K,
                                                           unique_indices=True, mode='promise_in_bounds')
    tile_end = jnp.cumsum(padded) // ROW_TILE
    tile_expert = jnp.minimum(jnp.searchsorted(tile_end, jnp.arange(n_tiles), side='right'),
                              N_EXPERTS - 1).astype(jnp.int32)
    n_used = tile_end[-1:].astype(jnp.int32)
    xs = x.at[row_token].get(mode='promise_in_bounds')
    ys = grouped_ffn(xs, w_gu_bf16, w_down_bf16, tile_expert, n_used)
    dest = dest.reshape(n, TOP_K)
    y0 = ys.at[dest[:, 0]].get(mode='promise_in_bounds')
    y1 = ys.at[dest[:, 1]].get(mode='promise_in_bounds')
    return y0, y1, gate


def kernel(x_prompt, x_sample, state_s5, cache_nsa_kv, state_win_kv, state_sc_conv, state_ssd_conv, state_ssd,
           page_table, ln_g, ln_b, w_in_even, s5_lam_re, s5_lam_im, s5_log_dt, s5_b, s5_c, s5_d, s5_w_glu,
           nsa_wk1, nsa_wk2, nsa_pe_k, nsa_wv1, nsa_wv2, nsa_pe_v, w_out_even, ffn_w_gu, ffn_w_down,
           w_in_odd, sc_conv_w, sc_conv_b, ssd_conv_w, ssd_conv_b, ssd_dt_bias, ssd_a_log, ssd_d, ssd_norm_g,
           w_out_odd, moe_router, moe_router_b, moe_w_gu, moe_w_down):
    f32 = jnp.float32
    bf16 = jnp.bfloat16
    bp, tp, d = x_prompt.shape
    bs, ts, _ = x_sample.shape
    n_p = bp * tp
    n_s = bs * ts
    w_buf = state_win_kv.shape[2]
    streams = [x_prompt.astype(f32).reshape(n_p, d), x_sample.astype(f32).reshape(n_s, d)]

    def flat(parts, n_rows):
        return [p.reshape(n_rows, p.shape[-1]) for p in parts]

    def out_proj(parts, w_out, width, h, g, b, router=None):
        w = w_out.astype(bf16)
        return matmul(flat(parts, h.shape[0]), [w[:width], w[width:]], ln=(h, g, b), router=router)

    def single_expert(n_rows):
        n_tiles = pl.cdiv(n_rows, min(ROW_TILE, n_rows))
        return jnp.zeros((n_tiles,), jnp.int32), jnp.full((1,), n_tiles, jnp.int32)

    s5p = s5_params(s5_lam_re[0], s5_lam_im[0], s5_log_dt[0], s5_b[0], s5_c[0], s5_d[0], s5_w_glu[0])
    cmpp = (nsa_wk1[0], nsa_wk2[0], nsa_pe_k[0], nsa_wv1[0], nsa_wv2[0], nsa_pe_v[0])
    w_in = w_in_even[0].astype(bf16)
    mix_p, s5_p, kv_p, win_p = even_prompt_mix(streams[0], w_in, bp, s5p, cmpp, w_buf)
    mix_s, s5_s, kv_s, win_s = even_sample_mix(streams[1], w_in, bs, state_s5[0], cache_nsa_kv[0], page_table,
                                               state_win_kv[0], s5p, cmpp)
    streams = [out_proj(mix, w_out_even[0], S5_DIM, h, ln_g[0, 0], ln_b[0, 0])
               for mix, h in zip((mix_p, mix_s), streams)]
    w_gu, w_down = to_bf16(ffn_w_gu), to_bf16(ffn_w_down)
    streams = [grouped_ffn(h, w_gu, w_down, *single_expert(h.shape[0]), ln=(ln_g[0, 1], ln_b[0, 1]))
               for h in streams]

    oddp = (sc_conv_w[0], sc_conv_b[0], ssd_conv_w[0], ssd_conv_b[0], ssd_dt_bias[0],
            ssd_a_log[0], ssd_d[0], ssd_norm_g[0])
    w_in = w_in_odd[0].astype(bf16)
    mix_p, scc_p, sdc_p, ssd_p = odd_prompt_mix(streams[0], w_in, bp, *oddp)
    zs = matmul([streams[1]], [w_in]).reshape(bs, ts, -1)
    mix_s, scc_s, sdc_s, ssd_s = odd_mix(zs, state_sc_conv[0], state_ssd_conv[0], state_ssd[0], ts, *oddp)
    outs = [out_proj(mix, w_out_odd[0], SC_DIM, h, ln_g[1, 0], ln_b[1, 0], router=(moe_router[0], moe_router_b[0]))
            for mix, h in zip((mix_p, mix_s), streams)]
    h = jnp.concatenate([o[0] for o in outs], axis=0)
    logits = jnp.concatenate([o[1] for o in outs], axis=0)[:, :N_EXPERTS]
    y0, y1, gate = moe_ffn(h, logits, to_bf16(moe_w_gu[0]), to_bf16(moe_w_down[0]))
    h = moe_combine_ln(h, y0, y1, gate, ln_g[1, 1], ln_b[1, 1])

    hp = h[:n_p].reshape(bp, tp, d)
    hs = h[n_p:].reshape(bs, ts, d)
    st = lambda a, ref: a[None].astype(ref.dtype)
    return (hp.astype(x_prompt.dtype), hs.astype(x_sample.dtype),
            st(s5_p, state_s5), st(s5_s, state_s5),
            st(kv_p, cache_nsa_kv), st(kv_s, cache_nsa_kv),
            st(win_p, state_win_kv), st(win_s, state_win_kv),
            st(scc_p, state_sc_conv), st(scc_s, state_sc_conv),
            st(sdc_p, state_ssd_conv), st(sdc_s, state_ssd_conv),
            st(ssd_p, state_ssd), st(ssd_s, state_ssd))
```

```python
import functools
import math

import jax
import jax.numpy as jnp
from jax import lax
from jax.experimental import pallas as pl
from jax.experimental.pallas import tpu as pltpu

D_MODEL = 1024
DEPTH = 2
ALPHA = (2.0 * DEPTH) ** 0.25
LN_EPS = 1e-5
RMS_EPS = 1e-5
NEG = -1e30

S5_DIM = D_MODEL // 2
S5_GROUP = 16
S5_GROUPS = S5_DIM // S5_GROUP
S5_STATE = 64

HEAD_DIM = 64
N_HEADS = (D_MODEL // 2) // HEAD_DIM
KV_GROUPS = 2
HEADS_PER_GROUP = N_HEADS // KV_GROUPS
CMP_STRIDE = 16
CMP_LEN = 2 * CMP_STRIDE
SEL_BLOCK = 64
N_SEL = 16
WINDOW = 512
Q_BLOCK = 128
ROPE_THETA = 500000.0
ROT_DIM = HEAD_DIM // 4
FORCE = 1e4
NSA_Q = N_HEADS * HEAD_DIM
NSA_KV = 2 * KV_GROUPS * HEAD_DIM

SC_DIM = D_MODEL // 2
SC_WIDTH = 3
SSD_HEAD_DIM = 64
SSD_HEADS = 16
SSD_INNER = SSD_HEADS * SSD_HEAD_DIM
SSD_GROUPS = 4
SSD_STATE = 128
SSD_CONV = 4
SSD_CONV_DIM = SSD_INNER + 2 * SSD_GROUPS * SSD_STATE
SSD_CHUNK = 128

D_FF = 2816
N_EXPERTS = 8
TOP_K = 2

VMEM_LIMIT_BYTES = 56 * 1024 * 1024
LANES = 128
S5_N = S5_GROUPS * S5_STATE
S5_LT = S5_N // LANES
S5_CHUNK = 256
SEL_TILE = 1024
QK_SCALE = HEAD_DIM ** -0.5 * math.log2(math.e)
REMOVED = -3e38
PAGE_SIZE = 128
PAGES_PER_STEP = 64
NEW_PAD = 128
CAST_ROWS = 512
CAST_SPLIT = 4
TAIL = 8
ROW_TILE = 512
FF_TILE = D_FF // 2


def _cparams(*sem):
    return pltpu.CompilerParams(dimension_semantics=sem, vmem_limit_bytes=VMEM_LIMIT_BYTES)


def _deepnorm(resid, update, g, b):
    y = ALPHA * resid + update
    mu = jnp.mean(y, axis=-1, keepdims=True)
    yc = y - mu
    var = jnp.mean(yc * yc, axis=-1, keepdims=True)
    return yc * lax.rsqrt(var + LN_EPS) * g + b


def _mm_kernel(*refs, n_in, fuse_ln, router):
    xs, ws = refs[0:n_in], refs[n_in:2 * n_in]
    o_ref = refs[-2] if router else refs[-1]
    acc = None
    for x_ref, w_ref in zip(xs, ws):
        part = jnp.dot(x_ref[...].astype(jnp.bfloat16), w_ref[...], preferred_element_type=jnp.float32)
        acc = part if acc is None else acc + part
    if fuse_ln:
        r_ref, g_ref, b_ref = refs[2 * n_in:2 * n_in + 3]
        acc = _deepnorm(r_ref[...], acc, g_ref[...], b_ref[...])
    o_ref[...] = acc
    if router:
        wr_ref, br_ref = refs[-4], refs[-3]
        xparts = _split3(acc)
        logits = br_ref[...]
        for i in range(2):
            for j in range(2 - i):
                logits = logits + jnp.dot(xparts[i], wr_ref[j], preferred_element_type=jnp.float32)
        refs[-1][...] = logits


def matmul(xs, ws_bf16, ln=None, router=None):
    m = xs[0].shape[0]
    n = ws_bf16[0].shape[1]
    tile = min(ROW_TILE, m)
    row = lambda width: pl.BlockSpec((tile, width), lambda i: (i, 0))
    fixed = lambda shape: pl.BlockSpec(shape, lambda i: (0, 0), pipeline_mode=pl.Buffered(1))
    in_specs = [row(x.shape[1]) for x in xs] + [fixed(w.shape) for w in ws_bf16]
    args = list(xs) + list(ws_bf16)
    if ln is not None:
        resid, g, b = ln
        in_specs += [row(n), fixed((1, n)), fixed((1, n))]
        args += [resid, g.reshape(1, n), b.reshape(1, n)]
    out_specs, out_shape = row(n), jax.ShapeDtypeStruct((m, n), jnp.float32)
    if router is not None:
        w_r, b_r = router
        pad = LANES - w_r.shape[1]
        w_parts = jnp.stack(_split3(jnp.pad(w_r.astype(jnp.float32), ((0, 0), (0, pad)))))
        in_specs += [pl.BlockSpec((3, n, LANES), lambda i: (0, 0, 0), pipeline_mode=pl.Buffered(1)),
                     fixed((1, LANES))]
        args += [w_parts, jnp.pad(b_r.astype(jnp.float32), (0, pad)).reshape(1, LANES)]
        out_specs, out_shape = [out_specs, row(LANES)], [out_shape, jax.ShapeDtypeStruct((m, LANES), jnp.float32)]
    return pl.pallas_call(
        functools.partial(_mm_kernel, n_in=len(xs), fuse_ln=ln is not None, router=router is not None),
        grid=(pl.cdiv(m, tile),),
        in_specs=in_specs,
        out_specs=out_specs,
        out_shape=out_shape,
        compiler_params=_cparams("parallel"),
        name="matmul",
    )(*args)


def _combine_kernel(h_ref, y0_ref, y1_ref, gate_ref, g_ref, b_ref, o_ref):
    gate = gate_ref[...]
    f = gate[:, 0:1] * y0_ref[...] + gate[:, 1:2] * y1_ref[...]
    o_ref[...] = _deepnorm(h_ref[...], f, g_ref[...], b_ref[...])


def moe_combine_ln(h, y0, y1, gate, g, b, row0, n_rows):
    n = h.shape[1]
    tile = min(ROW_TILE, n_rows)
    assert row0 % tile == 0 and n_rows % tile == 0
    first = row0 // tile
    row = lambda width: pl.BlockSpec((tile, width), lambda i: (first + i, 0))
    fixed = pl.BlockSpec((1, n), lambda i: (0, 0))
    return pl.pallas_call(
        _combine_kernel,
        grid=(n_rows // tile,),
        in_specs=[row(n), row(n), row(n), row(TOP_K), fixed, fixed],
        out_specs=pl.BlockSpec((tile, n), lambda i: (i, 0)),
        out_shape=jax.ShapeDtypeStruct((n_rows, n), jnp.float32),
        compiler_params=_cparams("parallel"),
        name="moe_combine_ln",
    )(h, y0, y1, gate, g.reshape(1, n), b.reshape(1, n))


def _cast_kernel(*refs):
    o_ref = refs[-1]
    o_ref[...] = jnp.concatenate([r[...].astype(o_ref.dtype) for r in refs[:-1]], axis=1)


def to_bf16(w):
    shape = w.shape
    w2 = w.reshape(-1, shape[-1])
    rows, cols = w2.shape
    split = CAST_SPLIT if cols % (CAST_SPLIT * LANES) == 0 else 1
    out = pl.pallas_call(
        _cast_kernel,
        grid=(pl.cdiv(rows, CAST_ROWS),),
        in_specs=[pl.BlockSpec((CAST_ROWS, cols // split), lambda i, c=c: (i, c)) for c in range(split)],
        out_specs=pl.BlockSpec((CAST_ROWS, cols), lambda i: (i, 0)),
        out_shape=jax.ShapeDtypeStruct((rows, cols), jnp.bfloat16),
        compiler_params=_cparams("parallel"),
        name="to_bf16",
    )(*([w2] * split))
    return out.reshape(shape)


def _ffn_kernel(te_ref, nt_ref, x_ref, wg_ref, wu_ref, wd_ref, *rest, fuse_ln):
    o_ref = rest[-1]
    t = pl.program_id(0)
    j = pl.program_id(1)

    @pl.when(t < nt_ref[0])
    def _():
        x = x_ref[...].astype(jnp.bfloat16)
        g = jnp.dot(x, wg_ref[...], preferred_element_type=jnp.float32)
        u = jnp.dot(x, wu_ref[...], preferred_element_type=jnp.float32)
        h = (g * jax.nn.sigmoid(g) * u).astype(jnp.bfloat16)
        part = jnp.dot(h, wd_ref[...], preferred_element_type=jnp.float32)

        @pl.when(j == 0)
        def _():
            o_ref[...] = part

        @pl.when(j > 0)
        def _():
            if fuse_ln:
                o_ref[...] = _deepnorm(x_ref[...], o_ref[...] + part, rest[0][...], rest[1][...])
            else:
                o_ref[...] += part

    @pl.when(jnp.logical_and(t >= nt_ref[0], j == 0))
    def _():
        o_ref[...] = jnp.zeros_like(o_ref)


def grouped_ffn(x, w_gu_bf16, w_down_bf16, tile_expert, n_tiles_used, ln=None):
    r, d = x.shape
    nf = D_FF // FF_TILE
    assert nf == 2
    tile = min(ROW_TILE, r)
    n_tiles = pl.cdiv(r, tile)
    in_specs = [
        pl.BlockSpec((tile, d), lambda t, j, te, nt: (t, 0)),
        pl.BlockSpec((None, d, FF_TILE), lambda t, j, te, nt: (te[t], 0, j)),
        pl.BlockSpec((None, d, FF_TILE), lambda t, j, te, nt: (te[t], 0, nf + j)),
        pl.BlockSpec((None, FF_TILE, d), lambda t, j, te, nt: (te[t], j, 0)),
    ]
    args = [tile_expert, n_tiles_used, x, w_gu_bf16, w_gu_bf16, w_down_bf16]
    if ln is not None:
        in_specs += [pl.BlockSpec((1, d), lambda t, j, te, nt: (0, 0))] * 2
        args += [ln[0].reshape(1, d), ln[1].reshape(1, d)]
    grid_spec = pltpu.PrefetchScalarGridSpec(
        num_scalar_prefetch=2,
        grid=(n_tiles, nf),
        in_specs=in_specs,
        out_specs=pl.BlockSpec((tile, d), lambda t, j, te, nt: (t, 0)),
    )
    return pl.pallas_call(
        functools.partial(_ffn_kernel, fuse_ln=ln is not None),
        grid_spec=grid_spec,
        out_shape=jax.ShapeDtypeStruct((r, d), jnp.float32),
        compiler_params=_cparams("parallel", "arbitrary"),
        name="grouped_ffn",
    )(*args)


def _s5_kernel(u_ref, perm_ref, h0r_ref, h0i_ref, ar_ref, ai_ref, bbr_ref, bbi_ref, cr_ref, ci_ref, d_ref, wglu_ref,
               y_ref, hro_ref, hio_ref, bur, bui, sr, si, hr, hi, *, chains, chunk):
    j = pl.program_id(0)

    @pl.when(j == 0)
    def _():
        hr[...] = h0r_ref[...]
        hi[...] = h0i_ref[...]

    rows_n = chains * chunk
    u = u_ref[...].reshape(rows_n, S5_DIM)
    to_tc = perm_ref[...]
    ub = jnp.dot(to_tc, u.astype(jnp.bfloat16), preferred_element_type=jnp.float32).astype(jnp.bfloat16)
    hd, hn = S5_DIM // 2, S5_N // 2

    def b_proj(w_ref):
        return jnp.concatenate([jnp.dot(ub[:, h * hd:(h + 1) * hd], w_ref[h * hd:(h + 1) * hd, h * hn:(h + 1) * hn],
                                        preferred_element_type=jnp.float32) for h in range(2)], axis=1)

    bu_r = b_proj(bbr_ref)
    bu_i = b_proj(bbi_ref)
    for k in range(S5_LT):
        bur[k] = bu_r[:, k * LANES:(k + 1) * LANES]
        bui[k] = bu_i[:, k * LANES:(k + 1) * LANES]
    ar = [jnp.broadcast_to(ar_ref[:, k * LANES:(k + 1) * LANES], (chains, LANES)) for k in range(S5_LT)]
    ai = [jnp.broadcast_to(ai_ref[:, k * LANES:(k + 1) * LANES], (chains, LANES)) for k in range(S5_LT)]

    def body(t, carry):
        rows = pl.ds(t * chains, chains)
        out = []
        for k in range(S5_LT):
            xr, xi = carry[2 * k], carry[2 * k + 1]
            nr = ar[k] * xr - ai[k] * xi + bur[k, rows, :]
            ni = ar[k] * xi + ai[k] * xr + bui[k, rows, :]
            sr[k, rows, :] = nr
            si[k, rows, :] = ni
            out += [nr, ni]
        return tuple(out)

    init = []
    for k in range(S5_LT):
        init += [hr[:, k * LANES:(k + 1) * LANES], hi[:, k * LANES:(k + 1) * LANES]]
    fin = lax.fori_loop(0, chunk, body, tuple(init), unroll=2)
    xr = jnp.concatenate(fin[0::2], axis=1)
    xi = jnp.concatenate(fin[1::2], axis=1)
    hr[...] = xr
    hi[...] = xi
    hro_ref[...] = xr
    hio_ref[...] = xi
    s_r = jnp.concatenate([sr[k] for k in range(S5_LT)], axis=1).astype(jnp.bfloat16)
    s_i = jnp.concatenate([si[k] for k in range(S5_LT)], axis=1).astype(jnp.bfloat16)
    y = jnp.concatenate(
        [jnp.dot(s_r[:, h * hn:(h + 1) * hn], cr_ref[h * hn:(h + 1) * hn, h * hd:(h + 1) * hd],
                 preferred_element_type=jnp.float32)
         - jnp.dot(s_i[:, h * hn:(h + 1) * hn], ci_ref[h * hn:(h + 1) * hn, h * hd:(h + 1) * hd],
                   preferred_element_type=jnp.float32) for h in range(2)], axis=1)
    y = sum(lax.dot_general(to_tc, part, (((0,), (0,)), ((), ())), preferred_element_type=jnp.float32)
            for part in _split3(y)) + d_ref[...] * u
    z = jax.nn.gelu(y)
    gate = jax.nn.sigmoid(jnp.dot(z.astype(jnp.bfloat16), wglu_ref[...], preferred_element_type=jnp.float32))
    y_ref[...] = (z * gate).reshape(chains, chunk, S5_DIM)


def s5_params(lam_re, lam_im, log_dt, b, c, d, w_glu):
    f32 = jnp.float32
    dt = jnp.exp(log_dt.astype(f32))[:, None]
    mag = jnp.exp(lam_re * dt)
    ang = lam_im * dt
    ab_re = mag * jnp.cos(ang)
    ab_im = mag * jnp.sin(ang)
    den = lam_re * lam_re + lam_im * lam_im
    nr = ab_re - 1.0
    coef_re = (nr * lam_re + ab_im * lam_im) / den
    coef_im = (ab_im * lam_re - nr * lam_im) / den
    b_re = b[..., 0].astype(f32)
    b_im = b[..., 1].astype(f32)
    bb_re = coef_re[..., None] * b_re - coef_im[..., None] * b_im
    bb_im = coef_re[..., None] * b_im + coef_im[..., None] * b_re
    eye = jnp.eye(S5_GROUPS, dtype=f32)
    bbr = jnp.einsum('gnk,gh->gkhn', bb_re, eye).reshape(S5_DIM, S5_N).astype(jnp.bfloat16)
    bbi = jnp.einsum('gnk,gh->gkhn', bb_im, eye).reshape(S5_DIM, S5_N).astype(jnp.bfloat16)
    cr = jnp.einsum('gkn,gh->gnhk', c[..., 0].astype(f32), eye).reshape(S5_N, S5_DIM).astype(jnp.bfloat16)
    ci = jnp.einsum('gkn,gh->gnhk', c[..., 1].astype(f32), eye).reshape(S5_N, S5_DIM).astype(jnp.bfloat16)
    return (ab_re.reshape(1, S5_N), ab_im.reshape(1, S5_N), bbr, bbi, cr, ci,
            d.astype(f32).reshape(1, S5_DIM), w_glu.astype(jnp.bfloat16))


def s5_scan(u, h0, params, chunk):
    chains, t, _ = u.shape
    ar, ai, bbr, bbi, cr, ci, d, wglu = params
    h0r = h0[..., 0].reshape(chains, S5_N)
    h0i = h0[..., 1].reshape(chains, S5_N)
    full = lambda shape: pl.BlockSpec(shape, lambda j: (0,) * len(shape))
    rows = chains * chunk
    r = jnp.arange(rows)
    to_tc = (r[None, :] == (r[:, None] % chains) * chunk + r[:, None] // chains).astype(jnp.bfloat16)
    y, hr, hi = pl.pallas_call(
        functools.partial(_s5_kernel, chains=chains, chunk=chunk),
        grid=(t // chunk,),
        in_specs=[pl.BlockSpec((chains, chunk, S5_DIM), lambda j: (0, j, 0)), full((rows, rows)),
                  full((chains, S5_N)), full((chains, S5_N)), full((1, S5_N)), full((1, S5_N)),
                  full((S5_DIM, S5_N)), full((S5_DIM, S5_N)), full((S5_N, S5_DIM)), full((S5_N, S5_DIM)),
                  full((1, S5_DIM)), full((S5_DIM, S5_DIM))],
        out_specs=[pl.BlockSpec((chains, chunk, S5_DIM), lambda j: (0, j, 0)),
                   full((chains, S5_N)), full((chains, S5_N))],
        out_shape=[jax.ShapeDtypeStruct((chains, t, S5_DIM), jnp.float32),
                   jax.ShapeDtypeStruct((chains, S5_N), jnp.float32),
                   jax.ShapeDtypeStruct((chains, S5_N), jnp.float32)],
        scratch_shapes=[pltpu.VMEM((S5_LT, rows, LANES), jnp.float32)] * 4
                       + [pltpu.VMEM((chains, S5_N), jnp.float32)] * 2,
        compiler_params=_cparams("arbitrary"),
        name="s5_scan",
    )(u, to_tc, h0r, h0i, ar, ai, bbr, bbi, cr, ci, d, wglu)
    new_state = jnp.stack([hr.reshape(chains, S5_GROUPS, S5_STATE), hi.reshape(chains, S5_GROUPS, S5_STATE)],
                          axis=-1)
    return y, new_state


def _dot_nt(a, b):
    return lax.dot_general(a, b, (((1,), (1,)), ((), ())), preferred_element_type=jnp.float32)


def _split3(x):
    hi = x.astype(jnp.bfloat16)
    rem = x - hi.astype(jnp.float32)
    mid = rem.astype(jnp.bfloat16)
    lo = (rem - mid.astype(jnp.float32)).astype(jnp.bfloat16)
    return hi, mid, lo


def _softmax_rows(s, mask):
    s = jnp.where(mask, s, NEG)
    m = jnp.max(s, axis=-1, keepdims=True)
    p = jnp.exp2(s - m)
    inv = jnp.where(m > 0.5 * NEG, 1.0 / jnp.sum(p, axis=-1, keepdims=True), 0.0)
    return p * inv


def _nsa_prompt_kernel(q_ref, gate_ref, kc_ref, vc_ref, ks_ref, vs_ref, kw_ref, vw_ref, o_ref, *, n_cmp, n_blk):
    f32, bf16 = jnp.float32, jnp.bfloat16
    r4 = HEADS_PER_GROUP
    n_cpad = kc_ref.shape[0]
    start = pl.program_id(1) * Q_BLOCK
    q = q_ref[...] * QK_SCALE
    gate = gate_ref[...]
    lane = lax.broadcasted_iota(jnp.int32, (Q_BLOCK, LANES), 1)
    qpos = start + lax.broadcasted_iota(jnp.int32, (Q_BLOCK, 1), 0)
    n_idx = lax.broadcasted_iota(jnp.int32, (Q_BLOCK, n_cpad), 1)
    cmask = (((n_idx * CMP_STRIDE + (CMP_LEN - 1)) <= qpos) & (n_idx < n_cmp))[None]
    ratio = SEL_BLOCK // CMP_STRIDE
    gsum = (lax.broadcasted_iota(jnp.int32, (n_blk, n_cpad), 1) // ratio
            == lax.broadcasted_iota(jnp.int32, (n_blk, n_cpad), 0)).astype(bf16)
    blk = lax.broadcasted_iota(jnp.int32, (n_blk, Q_BLOCK), 0)
    blk_f = blk.astype(f32)
    jq = (start + lax.broadcasted_iota(jnp.int32, (n_blk, Q_BLOCK), 1)) // SEL_BLOCK
    force = jnp.where((blk == 0) | (blk == jq) | (blk == jq - 1), FORCE, 0.0)
    qgs, o_cs, sels = [], [], []
    for g in range(KV_GROUPS):
        keep = (lane < HEAD_DIM) if g == 0 else (lane >= HEAD_DIM)
        parts = []
        for r in range(r4):
            h = r4 * g + r
            tile = q[:, (h // 2) * LANES:(h // 2 + 1) * LANES]
            if h % 2 != g:
                tile = pltpu.roll(tile, HEAD_DIM, axis=1)
            parts.append(jnp.where(keep, tile, 0.0))
        qg = jnp.concatenate(parts, axis=0).astype(bf16)
        qgs.append(qg)

        p_c = _softmax_rows(_dot_nt(qg, kc_ref[...]).reshape(r4, Q_BLOCK, n_cpad), cmask)
        o_cs.append(jnp.dot(p_c.reshape(r4 * Q_BLOCK, n_cpad).astype(bf16), vc_ref[...],
                            preferred_element_type=f32).reshape(r4, Q_BLOCK, LANES))
        psum = p_c[0] + p_c[1] + p_c[2] + p_c[3]
        imp_t = sum(_dot_nt(gsum, part) for part in _split3(psum))

        score = jnp.where(blk <= jq, imp_t + force, NEG)
        sel_t = jnp.zeros((n_blk, Q_BLOCK), f32)
        for _ in range(min(N_SEL, n_blk)):
            m = jnp.max(score, axis=0, keepdims=True)
            idx = jnp.min(jnp.where(score == m, blk_f, float(n_blk)), axis=0, keepdims=True)
            hit = blk_f == idx
            sel_t = jnp.where(hit & (m > 0.5 * NEG), 1.0, sel_t)
            score = jnp.where(hit, REMOVED, score)
        sels.append(sel_t.T)

    n_full = start // SEL_TILE
    expand0 = (lax.broadcasted_iota(jnp.int32, (n_blk, SEL_TILE), 0)
               == lax.broadcasted_iota(jnp.int32, (n_blk, SEL_TILE), 1) // SEL_BLOCK).astype(bf16)

    def tile_update(i, carry, causal):
        off = pl.multiple_of(i * SEL_TILE, SEL_TILE)
        k = ks_ref[pl.ds(off, SEL_TILE), :]
        v = vs_ref[pl.ds(off, SEL_TILE), :]
        vlane = lax.broadcasted_iota(jnp.int32, (SEL_TILE, LANES), 1)
        v_ones = [jnp.where((vlane < HEAD_DIM) == (g == 0), v, jnp.ones_like(v)) for g in range(KV_GROUPS)]
        out = []
        for g in range(KV_GROUPS):
            m_run, l_run, acc = carry[g]
            s_t = _dot_nt(qgs[g], k).reshape(r4, Q_BLOCK, SEL_TILE)
            shifted = pltpu.roll(sels[g], (n_blk - i * (SEL_TILE // SEL_BLOCK)) % n_blk, axis=1).astype(bf16)
            mk = jnp.dot(shifted, expand0, preferred_element_type=f32) > 0.5
            if causal:
                kpos = i * SEL_TILE + lax.broadcasted_iota(jnp.int32, (Q_BLOCK, SEL_TILE), 1)
                mk = mk & (kpos <= qpos)
            s_t = jnp.where(mk[None], s_t, NEG)
            m_new = jnp.maximum(m_run, jnp.max(s_t, axis=-1, keepdims=True))
            alpha = jnp.exp2(m_run - m_new)
            p = jnp.exp2((s_t - m_new).astype(bf16))
            pv = jnp.dot(p.reshape(r4 * Q_BLOCK, SEL_TILE), v_ones[g], preferred_element_type=f32)
            pv = pv.reshape(r4, Q_BLOCK, LANES)
            l_new = alpha * l_run + pv[:, :, (1 - g) * HEAD_DIM:(1 - g) * HEAD_DIM + 1]
            out.append((m_new, l_new, alpha * acc + pv))
        return tuple(out)

    init = (jnp.full((r4, Q_BLOCK, 1), NEG, f32), jnp.zeros((r4, Q_BLOCK, 1), f32),
            jnp.zeros((r4, Q_BLOCK, LANES), f32))
    carry = lax.fori_loop(0, n_full, lambda i, c: tile_update(i, c, False), (init, init))
    fin = tile_update(n_full, carry, True)

    n_win = WINDOW + Q_BLOCK
    woff = pl.multiple_of(start, Q_BLOCK)
    kwin = kw_ref[pl.ds(woff, n_win), :]
    vwin = vw_ref[pl.ds(woff, n_win), :]
    wpos = start - WINDOW + lax.broadcasted_iota(jnp.int32, (Q_BLOCK, n_win), 1)
    wmask = ((wpos <= qpos) & (wpos > qpos - WINDOW) & (wpos >= 0))[None]
    heads = [None] * N_HEADS
    for g in range(KV_GROUPS):
        m_fin, l_fin, acc = fin[g]
        o_s = acc * jnp.where(m_fin > 0.5 * NEG, 1.0 / l_fin, 0.0)
        p_w = _softmax_rows(_dot_nt(qgs[g], kwin).reshape(r4, Q_BLOCK, n_win), wmask)
        o_w = jnp.dot(p_w.reshape(r4 * Q_BLOCK, n_win).astype(bf16), vwin,
                      preferred_element_type=f32).reshape(r4, Q_BLOCK, LANES)
        for r in range(r4):
            h = r4 * g + r
            heads[h] = (gate[:, 3 * h:3 * h + 1] * o_cs[g][r] + gate[:, 3 * h + 1:3 * h + 2] * o_s[r]
                        + gate[:, 3 * h + 2:3 * h + 3] * o_w[r])

    tiles = []
    for j in range(N_HEADS // 2):
        even, odd = heads[2 * j], heads[2 * j + 1]
        if j // 2 == 0:
            tiles.append(jnp.where(lane < HEAD_DIM, even, pltpu.roll(odd, HEAD_DIM, axis=1)))
        else:
            tiles.append(jnp.where(lane < HEAD_DIM, pltpu.roll(even, HEAD_DIM, axis=1), odd))
    o_ref[...] = jnp.concatenate(tiles, axis=1)


def nsa_prompt(q, gates, kc, vc, kvs, kvw_pad):
    b, t, _ = q.shape
    n_cpad = kc.shape[1]
    kern = functools.partial(_nsa_prompt_kernel, n_cmp=t // CMP_STRIDE - 1, n_blk=t // SEL_BLOCK)
    whole = lambda rows, c=0: pl.BlockSpec((None, rows, LANES), lambda i, j: (i, 0, c))
    return pl.pallas_call(
        kern,
        grid=(b, t // Q_BLOCK),
        in_specs=[pl.BlockSpec((None, Q_BLOCK, NSA_Q), lambda i, j: (i, j, 0)),
                  pl.BlockSpec((None, Q_BLOCK, 3 * N_HEADS), lambda i, j: (i, j, 0)),
                  whole(n_cpad), whole(n_cpad), whole(t, 0), whole(t, 1), whole(t + WINDOW, 0),
                  whole(t + WINDOW, 1)],
        out_specs=pl.BlockSpec((None, Q_BLOCK, NSA_Q), lambda i, j: (i, j, 0)),
        out_shape=jax.ShapeDtypeStruct((b, t, NSA_Q), jnp.float32),
        compiler_params=_cparams("parallel", "arbitrary"),
        name="nsa_prompt",
    )(q, gates, kc, vc, kvs, kvs, kvw_pad, kvw_pad)


def _compress_kernel(ch_ref, pet_ref, peb_ref, w1t_ref, w1b_ref, w2_ref, o_ref):
    bf16 = jnp.bfloat16
    ch = ch_ref[...]
    n_ch = ch.shape[0]
    a = jnp.dot((ch + pet_ref[...]).astype(bf16), w1t_ref[...], preferred_element_type=jnp.float32)
    b = jnp.dot((ch + peb_ref[...]).astype(bf16), w1b_ref[...], preferred_element_type=jnp.float32)
    pre = a + pltpu.roll(b, n_ch - 1, axis=0)
    o_ref[...] = jnp.dot(jax.nn.gelu(pre).astype(bf16), w2_ref[...],
                         preferred_element_type=jnp.float32).astype(o_ref.dtype)


def compress_params(w1, w2, pe):
    f32 = jnp.float32
    eye = jnp.eye(KV_GROUPS, dtype=f32)
    w1r = w1.astype(f32).reshape(2, CMP_STRIDE, HEAD_DIM, HEAD_DIM)
    big = jnp.einsum('hjde,gk->hjgdke', w1r, eye).reshape(2, CMP_STRIDE * LANES, LANES).astype(jnp.bfloat16)
    w2bd = jnp.einsum('de,gk->gdke', w2.astype(f32), eye).reshape(LANES, LANES).astype(jnp.bfloat16)
    per = pe.astype(f32).reshape(2, CMP_STRIDE, 1, HEAD_DIM)
    pe_rows = jnp.broadcast_to(per, (2, CMP_STRIDE, KV_GROUPS, HEAD_DIM)).reshape(2, 1, CMP_STRIDE * LANES)
    return pe_rows[0], pe_rows[1], big[0], big[1], w2bd


def compress_prompt(x, params):
    b, t, _ = x.shape
    n_ch = t // CMP_STRIDE
    ch = x.reshape(b, n_ch, CMP_STRIDE * LANES)
    pet, peb, w1t, w1b, w2bd = params
    full = lambda shape: pl.BlockSpec(shape, lambda i: (0,) * len(shape))
    return pl.pallas_call(
        _compress_kernel,
        grid=(b,),
        in_specs=[pl.BlockSpec((None, n_ch, CMP_STRIDE * LANES), lambda i: (i, 0, 0)),
                  full((1, CMP_STRIDE * LANES)), full((1, CMP_STRIDE * LANES)),
                  full((CMP_STRIDE * LANES, LANES)), full((CMP_STRIDE * LANES, LANES)), full((LANES, LANES))],
        out_specs=pl.BlockSpec((None, n_ch, LANES), lambda i: (i, 0, 0)),
        out_shape=jax.ShapeDtypeStruct((b, n_ch, LANES), jnp.bfloat16),
        compiler_params=_cparams("parallel"),
        name="compress_prompt",
    )(ch, pet, peb, w1t, w1b, w2bd)


def _cmp_sample_kernel(pt_ref, *refs, n_pages):
    f32, bf16 = jnp.float32, jnp.bfloat16
    pp = PAGES_PER_STEP
    pages = refs[0:pp]
    (perm_ref, newk_ref, newv_ref, wk_ref, wv_ref, ck_ref, cv_ref, w2k_ref, w2v_ref,
     kc_ref, vc_ref, slab_k, slab_v) = refs[pp:]
    s = pl.program_id(1)
    cpp = PAGE_SIZE // CMP_STRIDE
    base = pl.multiple_of(s * (pp * cpp), pp * cpp)
    for half, slab in enumerate((slab_k, slab_v)):
        for i in range(pp):
            page = pages[i][half * LANES:(half + 1) * LANES, :].astype(bf16)
            rows = _dot_nt(perm_ref[...], page)
            for j in range(CMP_STRIDE):
                slab[j, pl.ds(base + i * cpp, cpp), :] = rows[j * cpp:(j + 1) * cpp, :]

    @pl.when(s == pl.num_programs(1) - 1)
    def _():
        n_ch = n_pages * (PAGE_SIZE // CMP_STRIDE)
        row = lax.broadcasted_iota(jnp.int32, (n_ch, LANES), 0)
        for slab, new_ref, w_ref, c_ref, w2_ref, o_ref in ((slab_k, newk_ref, wk_ref, ck_ref, w2k_ref, kc_ref),
                                                           (slab_v, newv_ref, wv_ref, cv_ref, w2v_ref, vc_ref)):
            ch = jnp.concatenate([slab[j] for j in range(CMP_STRIDE)], axis=1).astype(bf16)
            ab = jnp.dot(ch, w_ref[...], preferred_element_type=f32)
            b_new = jnp.dot(new_ref[...].astype(bf16), w_ref[...], preferred_element_type=f32)[0:1, LANES:]
            nxt = pltpu.roll(ab[:, LANES:], n_ch - 1, axis=0)
            nxt = jnp.where(row == n_ch - 1, b_new, nxt)
            pre = ab[:, :LANES] + nxt + c_ref[...]
            o_ref[...] = jnp.dot(jax.nn.gelu(pre).astype(bf16), w2_ref[...],
                                 preferred_element_type=f32).astype(o_ref.dtype)


def compress_sample_params(w1, w2, pe):
    pet, peb, w1t, w1b, w2bd = compress_params(w1, w2, pe)
    hp = lax.Precision.HIGHEST
    const = (jnp.dot(pet, w1t.astype(jnp.float32), precision=hp)
             + jnp.dot(peb, w1b.astype(jnp.float32), precision=hp))
    return jnp.concatenate([w1t, w1b], axis=1), const, w2bd


def _page_spec(i, pair):
    return pl.BlockSpec((None, 2 * LANES, PAGE_SIZE),
                        lambda b, s, pt: (pt[b, PAGES_PER_STEP * s + i], pair, 0))


def _per_seq(shape):
    return pl.BlockSpec((None,) + shape, lambda b, s, pt: (b, 0, 0))


def compress_sample(pool_t, page_table, new_k, new_v, pk, pv):
    bsz, n_pages = page_table.shape
    pp = PAGES_PER_STEP
    n_ch = n_pages * (PAGE_SIZE // CMP_STRIDE)
    t_new = new_k.shape[1]

    def chunk_rows(x):
        x = jnp.pad(x, ((0, 0), (0, CMP_STRIDE - t_new), (0, 0))).reshape(bsz, 1, CMP_STRIDE * LANES)
        return jnp.pad(x, ((0, 0), (0, 7), (0, 0)))

    full = lambda shape: pl.BlockSpec(shape, lambda b, s, pt: (0,) * len(shape))
    r = jnp.arange(PAGE_SIZE)
    cpp = PAGE_SIZE // CMP_STRIDE
    perm = (r[None, :] == (r[:, None] % cpp) * CMP_STRIDE + r[:, None] // cpp).astype(jnp.bfloat16)
    wk, ck, w2k = pk
    wv, cv, w2v = pv
    grid_spec = pltpu.PrefetchScalarGridSpec(
        num_scalar_prefetch=1,
        grid=(bsz, n_pages // pp),
        in_specs=[_page_spec(i, 0) for i in range(pp)]
                 + [full((PAGE_SIZE, PAGE_SIZE)), _per_seq((8, CMP_STRIDE * LANES)),
                    _per_seq((8, CMP_STRIDE * LANES)),
                    full((CMP_STRIDE * LANES, 2 * LANES)), full((CMP_STRIDE * LANES, 2 * LANES)),
                    full((1, LANES)), full((1, LANES)), full((LANES, LANES)), full((LANES, LANES))],
        out_specs=[_per_seq((n_ch, LANES)), _per_seq((n_ch, LANES))],
        scratch_shapes=[pltpu.VMEM((CMP_STRIDE, n_ch, LANES), jnp.float32)] * 2,
    )
    return pl.pallas_call(
        functools.partial(_cmp_sample_kernel, n_pages=n_pages),
        grid_spec=grid_spec,
        out_shape=[jax.ShapeDtypeStruct((bsz, n_ch, LANES), jnp.bfloat16)] * 2,
        compiler_params=_cparams("parallel", "arbitrary"),
        name="compress_sample",
    )(page_table, *([pool_t] * pp), perm, chunk_rows(new_k), chunk_rows(new_v), wk, wv, ck, cv, w2k, w2v)


def _nsa_sample_kernel(pt_ref, *refs, n_pages, t_new, w_buf):
    f32, bf16 = jnp.float32, jnp.bfloat16
    pp = PAGES_PER_STEP
    q_ref, gate_ref, kc_ref, vc_ref = refs[0:4]
    pages = refs[4:4 + pp]
    (ksn_ref, vsn_ref, win_ref, kwn_ref, vwn_ref, o_ref,
     sel_scr, exp_scr, oc_scr, m_scr, l_scr, acc_scr) = refs[4 + pp:]
    r4, g2 = HEADS_PER_GROUP, KV_GROUPS
    n_rows = g2 * r4 * t_new
    past_len = n_pages * PAGE_SIZE
    n_cmp = kc_ref.shape[0]
    n_bpad = sel_scr.shape[1]
    tile = pp * PAGE_SIZE
    s = pl.program_id(1)
    qall = q_ref[...]
    qpos = past_len + lax.broadcasted_iota(jnp.int32, (n_rows, 1), 0) % t_new

    def grouped(x):
        return x.reshape(g2, 1, t_new, x.shape[-1])

    @pl.when(s == 0)
    def _():
        s_c = _dot_nt(qall, kc_ref[...])
        n_idx = lax.broadcasted_iota(jnp.int32, (n_rows, n_cmp), 1)
        p_c = _softmax_rows(s_c, (n_idx * CMP_STRIDE + (CMP_LEN - 1)) <= qpos)
        oc_scr[...] = jnp.dot(p_c.astype(bf16), vc_ref[...], preferred_element_type=f32)
        psum = jnp.sum(p_c.reshape(g2, r4, t_new, n_cmp), axis=1).reshape(g2 * t_new, n_cmp)
        psum = jnp.concatenate([psum, jnp.zeros((LANES - g2 * t_new, n_cmp), f32)], axis=0)
        p_hi = psum.astype(bf16)
        rem = psum - p_hi.astype(f32)
        p_mid = rem.astype(bf16)
        p_lo = (rem - p_mid.astype(f32)).astype(bf16)
        ratio = SEL_BLOCK // CMP_STRIDE
        gsum = (lax.broadcasted_iota(jnp.int32, (n_bpad, n_cmp), 1) // ratio
                == lax.broadcasted_iota(jnp.int32, (n_bpad, n_cmp), 0)).astype(bf16)
        imp_t = _dot_nt(gsum, p_hi) + _dot_nt(gsum, p_mid) + _dot_nt(gsum, p_lo)
        blk = lax.broadcasted_iota(jnp.int32, (n_bpad, LANES), 0)
        jq = (past_len + lax.broadcasted_iota(jnp.int32, (n_bpad, LANES), 1) % t_new) // SEL_BLOCK
        forced = (blk == 0) | (blk == jq) | (blk == jq - 1)
        score = jnp.where(blk <= jq, imp_t + jnp.where(forced, FORCE, 0.0), NEG)
        blk_f = blk.astype(f32)
        sel_t = jnp.zeros((n_bpad, LANES), f32)
        for _ in range(N_SEL):
            m = jnp.max(score, axis=0, keepdims=True)
            idx = jnp.min(jnp.where(score == m, blk_f, float(n_bpad)), axis=0, keepdims=True)
            hit = blk_f == idx
            sel_t = jnp.where(hit & (m > 0.5 * NEG), 1.0, sel_t)
            score = jnp.where(hit, REMOVED, score)
        sel = jnp.concatenate([sel_t[k * LANES:(k + 1) * LANES].T for k in range(n_bpad // LANES)], axis=1)
        sel_scr[...] = sel[0:g2 * t_new]
        exp_scr[...] = (lax.broadcasted_iota(jnp.int32, (LANES, tile), 0)
                        == lax.broadcasted_iota(jnp.int32, (LANES, tile), 1) // SEL_BLOCK).astype(bf16)
        m_scr[...] = jnp.full(m_scr.shape, NEG, f32)
        l_scr[...] = jnp.zeros(l_scr.shape, f32)
        acc_scr[...] = jnp.zeros(acc_scr.shape, f32)

    def online_update(s_t, mk, v, v_feature_major):
        n = s_t.shape[-1]
        s4 = jnp.where(mk, s_t.reshape(g2, r4, t_new, n), NEG)
        m_run = m_scr[...].reshape(g2, r4, t_new, 1)
        m_new = jnp.maximum(m_run, jnp.max(s4, axis=-1, keepdims=True))
        alpha = jnp.exp2(m_run - m_new)
        p = jnp.exp2(s4 - m_new)
        l_new = alpha * l_scr[...].reshape(g2, r4, t_new, 1) + jnp.sum(p, axis=-1, keepdims=True)
        pb = p.reshape(n_rows, n).astype(bf16)
        pv = _dot_nt(pb, v) if v_feature_major else jnp.dot(pb, v, preferred_element_type=f32)
        m_scr[...] = m_new.reshape(n_rows, 1)
        l_scr[...] = l_new.reshape(n_rows, 1)
        acc_scr[...] = alpha.reshape(n_rows, 1) * acc_scr[...] + pv

    kt = jnp.concatenate([r[0:LANES, :] for r in pages], axis=1).astype(bf16)
    vt = jnp.concatenate([r[LANES:2 * LANES, :] for r in pages], axis=1).astype(bf16)
    shifted = pltpu.roll(sel_scr[...], (n_bpad - s * (tile // SEL_BLOCK)) % n_bpad, axis=1)
    picked = jnp.dot(shifted[:, 0:LANES].astype(bf16), exp_scr[...], preferred_element_type=f32)
    online_update(jnp.dot(qall, kt, preferred_element_type=f32), grouped(picked) > 0.5, vt, True)

    @pl.when(s == pl.num_programs(1) - 1)
    def _():
        new_blk = past_len // SEL_BLOCK
        kidx = lax.broadcasted_iota(jnp.int32, (n_rows, NEW_PAD), 1)
        causal = ((past_len + kidx) <= qpos) & (kidx < t_new)
        picked_new = sel_scr[:, new_blk:new_blk + 1]
        mk = (grouped(picked_new) > 0.5) & causal.reshape(g2, r4, t_new, NEW_PAD)
        online_update(_dot_nt(qall, ksn_ref[...]), mk, vsn_ref[...], False)
        o_s = acc_scr[...] * jnp.where(m_scr[...] > 0.5 * NEG, 1.0 / l_scr[...], 0.0)

        n_win = w_buf + NEW_PAD
        kw_t = win_ref[0:LANES, :].astype(bf16)
        vw_t = win_ref[LANES:2 * LANES, :].astype(bf16)
        widx = lax.broadcasted_iota(jnp.int32, (n_rows, n_win), 1)
        wpos = past_len - w_buf + widx
        wmask = (wpos <= qpos) & (wpos > qpos - WINDOW) & (wpos >= 0) & (widx < w_buf + t_new)
        s_w = jnp.concatenate([jnp.dot(qall, kw_t, preferred_element_type=f32), _dot_nt(qall, kwn_ref[...])],
                              axis=1)
        p_w = _softmax_rows(s_w, wmask).astype(bf16)
        o_w = (_dot_nt(p_w[:, 0:w_buf], vw_t)
               + jnp.dot(p_w[:, w_buf:], vwn_ref[...], preferred_element_type=f32))
        gate = gate_ref[...]
        o_ref[...] = gate[:, 0:1] * oc_scr[...] + gate[:, 1:2] * o_s + gate[:, 2:3] * o_w


def nsa_sample(q, gates, kc, vc, pool_t, page_table, ks_new, vs_new, win, kw_new, vw_new):
    f32, bf16 = jnp.float32, jnp.bfloat16
    bsz, t_new = q.shape[0], q.shape[1]
    n_pages = page_table.shape[1]
    pp = PAGES_PER_STEP
    w_buf = win.shape[2]
    r4, g2 = HEADS_PER_GROUP, KV_GROUPS
    n_rows = g2 * r4 * t_new
    past_len = n_pages * PAGE_SIZE
    assert past_len % SEL_BLOCK == 0 and t_new <= SEL_BLOCK and past_len >= w_buf and n_pages % pp == 0
    n_sel = past_len // SEL_BLOCK + 1
    n_bpad = -(-n_sel // LANES) * LANES
    eye = jnp.eye(g2, dtype=f32)
    qg = q.reshape(bsz, t_new, g2, r4, HEAD_DIM).transpose(0, 2, 3, 1, 4) * QK_SCALE
    qall = jnp.einsum('bgrqd,gk->bgrqkd', qg, eye).reshape(bsz, n_rows, LANES).astype(bf16)
    gall = gates.reshape(bsz, t_new, g2, r4, 3).transpose(0, 2, 3, 1, 4).reshape(bsz, n_rows, 3)
    pad_rows = lambda x: jnp.pad(x, ((0, 0), (0, NEW_PAD - t_new), (0, 0))).astype(bf16)
    n_cmp = kc.shape[1]
    grid_spec = pltpu.PrefetchScalarGridSpec(
        num_scalar_prefetch=1,
        grid=(bsz, n_pages // pp),
        in_specs=[_per_seq((n_rows, LANES)), _per_seq((n_rows, 3)), _per_seq((n_cmp, LANES)),
                  _per_seq((n_cmp, LANES))]
                 + [_page_spec(i, 1) for i in range(pp)]
                 + [_per_seq((NEW_PAD, LANES)), _per_seq((NEW_PAD, LANES)), _per_seq((2 * LANES, w_buf)),
                    _per_seq((NEW_PAD, LANES)), _per_seq((NEW_PAD, LANES))],
        out_specs=_per_seq((n_rows, LANES)),
        scratch_shapes=[pltpu.VMEM((g2 * t_new, n_bpad), f32), pltpu.VMEM((LANES, pp * PAGE_SIZE), bf16),
                        pltpu.VMEM((n_rows, LANES), f32),
                        pltpu.VMEM((n_rows, 1), f32), pltpu.VMEM((n_rows, 1), f32),
                        pltpu.VMEM((n_rows, LANES), f32)],
    )
    o = pl.pallas_call(
        functools.partial(_nsa_sample_kernel, n_pages=n_pages, t_new=t_new, w_buf=w_buf),
        grid_spec=grid_spec,
        out_shape=jax.ShapeDtypeStruct((bsz, n_rows, LANES), f32),
        compiler_params=_cparams("parallel", "arbitrary"),
        name="nsa_sample",
    )(page_table, qall, gall, kc, vc, *([pool_t] * pp), pad_rows(ks_new), pad_rows(vs_new), win,
      pad_rows(kw_new), pad_rows(vw_new))
    o = jnp.einsum('bgrqkd,gk->bqgrd', o.reshape(bsz, g2, r4, t_new, g2, HEAD_DIM), eye)
    return o.reshape(bsz, t_new, NSA_Q)


def _ssd_kernel(x_ref, b_ref, c_ref, dt_ref, a_ref, za_ref, zb_ref, dskip_ref, ng_ref, y_ref, hout_ref, h_scr, *,
                chunk):
    f32, bf16 = jnp.float32, jnp.bfloat16
    n_l = chunk
    hpg = SSD_HEADS // SSD_GROUPS
    gw = hpg * SSD_HEAD_DIM
    j = pl.program_id(1)

    @pl.when(j == 0)
    def _():
        h_scr[...] = jnp.zeros(h_scr.shape, f32)

    x = x_ref[...]
    dt = dt_ref[...]
    tri_b = (lax.broadcasted_iota(jnp.int32, (n_l, n_l), 0) >= lax.broadcasted_iota(jnp.int32, (n_l, n_l), 1))
    tri = tri_b.astype(bf16)
    cum = sum(jnp.dot(tri, part, preferred_element_type=f32) for part in _split3(dt * a_ref[...]))
    cum_t = cum.T
    dt_t = dt.T
    ecum = jnp.exp(cum)
    clast = cum[n_l - 1:n_l, :]
    wt = jnp.exp(clast - cum) * dt
    elast = jnp.exp(clast)
    lane = lax.broadcasted_iota(jnp.int32, (n_l, LANES), 1)
    low = lane < SSD_HEAD_DIM

    def pair(v, h0):
        return jnp.where(low[:v.shape[0]], v[:, h0:h0 + 1], v[:, h0 + 1:h0 + 2])

    tiles = []
    for g in range(SSD_GROUPS):
        bg = b_ref[:, g * SSD_STATE:(g + 1) * SSD_STATE]
        cgb = c_ref[:, g * SSD_STATE:(g + 1) * SSD_STATE].astype(bf16)
        bgt = bg.T.astype(bf16)
        cb = jnp.dot(cgb, bgt, preferred_element_type=f32)
        hg = h_scr[g]
        y_inter = jnp.dot(cgb, hg.astype(bf16), preferred_element_type=f32)
        xw, dec = [], []
        for pr in range(hpg // 2):
            h0 = hpg * g + 2 * pr
            xt = x[:, (h0 // 2) * LANES:(h0 // 2 + 1) * LANES]
            acc = None
            for k in range(2):
                h = h0 + k
                seg = cum[:, h:h + 1] - cum_t[h:h + 1, :]
                w = cb * jnp.exp(jnp.where(tri_b, seg, NEG)) * dt_t[h:h + 1, :]
                xm = jnp.where(low if k == 0 else jnp.logical_not(low), xt, 0.0).astype(bf16)
                part = jnp.dot(w.astype(bf16), xm, preferred_element_type=f32)
                acc = part if acc is None else acc + part
            tiles.append(acc + y_inter[:, pr * LANES:(pr + 1) * LANES] * pair(ecum, h0))
            xw.append((xt * pair(wt, h0)).astype(bf16))
            dec.append(pair(elast, h0))
        h_scr[g] = (hg * jnp.concatenate(dec, axis=1)
                    + jnp.dot(bgt, jnp.concatenate(xw, axis=1), preferred_element_type=f32))
    y = jnp.concatenate(tiles, axis=1) + dskip_ref[...] * x
    zg = jnp.concatenate([za_ref[...], zb_ref[...]], axis=1)
    v = y * (zg * jax.nn.sigmoid(zg))
    outs = []
    for g in range(SSD_GROUPS):
        vg = v[:, g * gw:(g + 1) * gw]
        outs.append(vg * lax.rsqrt(jnp.mean(vg * vg, axis=-1, keepdims=True) + RMS_EPS))
    y_ref[...] = jnp.concatenate(outs, axis=1) * ng_ref[...]

    @pl.when(j == pl.num_programs(1) - 1)
    def _():
        hout_ref[...] = h_scr[...]


def ssd_prompt(xbc, dt, a, z, zg_offset, d_skip, norm_g):
    f32 = jnp.float32
    bsz, t, _ = xbc.shape
    hpg = SSD_HEADS // SSD_GROUPS
    gn = SSD_GROUPS * SSD_STATE
    half = SSD_INNER // 2
    assert zg_offset % half == 0
    dt_p = jnp.pad(dt, ((0, 0), (0, 0), (0, LANES - SSD_HEADS)))
    a_p = jnp.pad(a.astype(f32), (0, LANES - SSD_HEADS)).reshape(1, LANES)
    dsk = jnp.repeat(d_skip.astype(f32), SSD_HEAD_DIM).reshape(1, SSD_INNER)
    blk = lambda w, c: pl.BlockSpec((None, SSD_CHUNK, w), lambda b, j: (b, j, c))
    full = lambda shape: pl.BlockSpec(shape, lambda b, j: (0,) * len(shape))
    state_spec = pl.BlockSpec((None, SSD_GROUPS, SSD_STATE, hpg * SSD_HEAD_DIM), lambda b, j: (b, 0, 0, 0))
    y, h = pl.pallas_call(
        functools.partial(_ssd_kernel, chunk=SSD_CHUNK),
        grid=(bsz, t // SSD_CHUNK),
        in_specs=[blk(SSD_INNER, 0), blk(gn, SSD_INNER // gn), blk(gn, SSD_INNER // gn + 1), blk(LANES, 0),
                  full((1, LANES)), blk(half, zg_offset // half), blk(half, zg_offset // half + 1),
                  full((1, SSD_INNER)), full((1, SSD_INNER))],
        out_specs=[blk(SSD_INNER, 0), state_spec],
        out_shape=[jax.ShapeDtypeStruct((bsz, t, SSD_INNER), f32),
                   jax.ShapeDtypeStruct((bsz, SSD_GROUPS, SSD_STATE, hpg * SSD_HEAD_DIM), f32)],
        scratch_shapes=[pltpu.VMEM((SSD_GROUPS, SSD_STATE, hpg * SSD_HEAD_DIM), f32)],
        compiler_params=_cparams("parallel", "arbitrary"),
        name="ssd_prompt",
    )(xbc, xbc, xbc, dt_p, a_p, z, z, dsk, norm_g.astype(f32).reshape(1, SSD_INNER))
    h = h.reshape(bsz, SSD_GROUPS, SSD_STATE, hpg, SSD_HEAD_DIM).transpose(0, 1, 3, 4, 2)
    return y, h.reshape(bsz, SSD_HEADS, SSD_HEAD_DIM, SSD_STATE)


def _inproj_even_kernel(x_ref, w_ref, c_ref, sa_ref, sb_ref,
                        u_ref, q_ref, rows_ref, kvw_ref, kvsb_ref, kvwb_ref, g_ref):
    bf16 = jnp.bfloat16
    z = jnp.dot(x_ref[...].astype(bf16), w_ref[...], preferred_element_type=jnp.float32)
    cos, s_up, s_down = c_ref[...], sa_ref[...], sb_ref[...]

    def rot(t):
        return (t * cos + pltpu.roll(t, LANES - ROT_DIM // 2, axis=1) * s_up
                + pltpu.roll(t, ROT_DIM // 2, axis=1) * s_down)

    tile = lambda k: z[:, k * LANES:(k + 1) * LANES]
    q0 = S5_DIM // LANES
    kv0 = q0 + NSA_Q // LANES
    u_ref[...] = z[:, 0:S5_DIM]
    q_ref[...] = jnp.concatenate([rot(tile(q0 + k)) for k in range(NSA_Q // LANES)], axis=1)
    kc, vc, ks, vs, kw, vw = (rot(tile(kv0)), tile(kv0 + 1), rot(tile(kv0 + 2)), tile(kv0 + 3),
                              rot(tile(kv0 + 4)), tile(kv0 + 5))
    rows_ref[...] = jnp.concatenate([kc, vc, ks, vs], axis=1)
    kvw = jnp.concatenate([kw, vw], axis=1)
    kvw_ref[...] = kvw
    kvwb_ref[...] = kvw.astype(bf16)
    kvsb_ref[...] = jnp.concatenate([ks, vs], axis=1).astype(bf16)
    g0 = (kv0 + 6) * LANES
    g_ref[...] = jax.nn.sigmoid(z[:, g0:g0 + 3 * N_HEADS])


def rope_tables(pos):
    half = ROT_DIM // 2
    inv = ROPE_THETA ** (-jnp.arange(half, dtype=jnp.float32) * 2.0 / ROT_DIM)
    ang = pos.astype(jnp.float32)[:, None] * inv[None, :]
    d = jnp.arange(LANES) % HEAD_DIM
    cos = jnp.take(jnp.cos(ang), d % half, axis=1)
    sin = jnp.take(jnp.sin(ang), d % half, axis=1)
    return (jnp.where(d < ROT_DIM, cos, 1.0), jnp.where(d < half, -sin, 0.0),
            jnp.where((d >= half) & (d < ROT_DIM), sin, 0.0))


def inproj_even(h, w_bf16, pos):
    f32, bf16 = jnp.float32, jnp.bfloat16
    n, d = h.shape
    tile = min(ROW_TILE, n)
    row = lambda w: pl.BlockSpec((tile, w), lambda i: (i, 0))
    fixed = lambda shape: pl.BlockSpec(shape, lambda i: (0, 0), pipeline_mode=pl.Buffered(1))
    widths = [(S5_DIM, f32), (NSA_Q, f32), (2 * NSA_KV, f32), (NSA_KV, f32), (NSA_KV, bf16), (NSA_KV, bf16),
              (3 * N_HEADS, f32)]
    return pl.pallas_call(
        _inproj_even_kernel,
        grid=(pl.cdiv(n, tile),),
        in_specs=[row(d), fixed(w_bf16.shape), row(LANES), row(LANES), row(LANES)],
        out_specs=[row(w) for w, _ in widths],
        out_shape=[jax.ShapeDtypeStruct((n, w), dt) for w, dt in widths],
        compiler_params=_cparams("parallel"),
        name="inproj_even",
    )(h, w_bf16, *rope_tables(pos))


def _causal_conv_tile(x, tail, w_ref, b_ref, width):
    row = lax.broadcasted_iota(jnp.int32, (TAIL, x.shape[1]), 0)
    acc = b_ref[...] + w_ref[width - 1:width, :] * x
    for k in range(1, width):
        xs = pltpu.roll(x, k, axis=0)
        head = jnp.where(row < k, pltpu.roll(tail, k, axis=0), xs[0:TAIL])
        xs = jnp.concatenate([head, xs[TAIL:]], axis=0)
        acc = acc + w_ref[width - 1 - k:width - k, :] * xs
    return acc


def _inproj_odd_kernel(x_ref, w_ref, scw_ref, scb_ref, cvw_ref, cvb_ref, dtb_ref,
                       ysc_ref, xbc_ref, dt_ref, zg_ref, tsc_ref, tx_ref, tail_sc, tail_x, *, tiles_per_seq):
    @pl.when(pl.program_id(0) % tiles_per_seq == 0)
    def _():
        tail_sc[...] = jnp.zeros(tail_sc.shape, jnp.float32)
        tail_x[...] = jnp.zeros(tail_x.shape, jnp.float32)

    z = jnp.dot(x_ref[...].astype(jnp.bfloat16), w_ref[...], preferred_element_type=jnp.float32)
    o_zg = 3 * SC_DIM
    o_x = o_zg + SSD_INNER
    o_dt = o_x + SSD_CONV_DIM
    n = z.shape[0]
    prod = z[:, 2 * SC_DIM:3 * SC_DIM] * z[:, 0:SC_DIM]
    ysc_ref[...] = z[:, SC_DIM:2 * SC_DIM] * _causal_conv_tile(prod, tail_sc[...], scw_ref, scb_ref, SC_WIDTH)
    xbc = z[:, o_x:o_dt]
    c = _causal_conv_tile(xbc, tail_x[...], cvw_ref, cvb_ref, SSD_CONV)
    xbc_ref[...] = c * jax.nn.sigmoid(c)
    dt_ref[...] = jax.nn.softplus(z[:, o_dt:o_dt + SSD_HEADS] + dtb_ref[...])
    zg_ref[...] = z[:, o_zg:o_x]
    tail_sc[...] = prod[n - TAIL:n]
    tail_x[...] = xbc[n - TAIL:n]
    tsc_ref[...] = prod[n - TAIL:n]
    tx_ref[...] = xbc[n - TAIL:n]


def inproj_odd_prompt(h, w_bf16, bsz, sc_w, sc_b, cv_w, cv_b, dt_bias):
    f32 = jnp.float32
    n_rows, d = h.shape
    t = n_rows // bsz
    assert t % ROW_TILE == 0
    tps = t // ROW_TILE
    row = lambda w: pl.BlockSpec((ROW_TILE, w), lambda i: (i, 0))
    fixed = lambda shape: pl.BlockSpec(shape, lambda i: (0,) * len(shape), pipeline_mode=pl.Buffered(1))
    last = lambda w: pl.BlockSpec((None, TAIL, w), lambda i: (i // tps, 0, 0))
    ysc, xbc, dt, zg, tsc, tx = pl.pallas_call(
        functools.partial(_inproj_odd_kernel, tiles_per_seq=tps),
        grid=(n_rows // ROW_TILE,),
        in_specs=[row(d), fixed(w_bf16.shape), fixed((SC_WIDTH, SC_DIM)), fixed((1, SC_DIM)),
                  fixed((SSD_CONV, SSD_CONV_DIM)), fixed((1, SSD_CONV_DIM)), fixed((1, SSD_HEADS))],
        out_specs=[row(SC_DIM), row(SSD_CONV_DIM), row(SSD_HEADS), row(SSD_INNER), last(SC_DIM), last(SSD_CONV_DIM)],
        out_shape=[jax.ShapeDtypeStruct((n_rows, SC_DIM), f32), jax.ShapeDtypeStruct((n_rows, SSD_CONV_DIM), f32),
                   jax.ShapeDtypeStruct((n_rows, SSD_HEADS), f32), jax.ShapeDtypeStruct((n_rows, SSD_INNER), f32),
                   jax.ShapeDtypeStruct((bsz, TAIL, SC_DIM), f32), jax.ShapeDtypeStruct((bsz, TAIL, SSD_CONV_DIM), f32)],
        scratch_shapes=[pltpu.VMEM((TAIL, SC_DIM), f32), pltpu.VMEM((TAIL, SSD_CONV_DIM), f32)],
        compiler_params=_cparams("arbitrary"),
        name="inproj_odd",
    )(h, w_bf16, sc_w.astype(f32), sc_b.astype(f32).reshape(1, SC_DIM), cv_w.astype(f32),
      cv_b.astype(f32).reshape(1, SSD_CONV_DIM), dt_bias.astype(f32).reshape(1, SSD_HEADS))
    seq = lambda a: a.reshape(bsz, t, a.shape[-1])
    return (seq(ysc), seq(xbc), seq(dt), seq(zg),
            tsc[:, TAIL - (SC_WIDTH - 1):], tx[:, TAIL - (SSD_CONV - 1):])


def last_rows(x, n):
    t = x.shape[1]
    if t < n:
        x = jnp.pad(x, [(0, 0), (n - t, 0)] + [(0, 0)] * (x.ndim - 2))
    return x[:, x.shape[1] - n:]


def causal_conv(x, buf, w, b):
    t = x.shape[1]
    width = w.shape[0]
    xp = jnp.concatenate([buf, x], axis=1)
    y = b + sum(xp[:, j:j + t] * w[j] for j in range(width))
    return y, xp[:, xp.shape[1] - (width - 1):]


def even_prompt_mix(h, w_in_bf16, bt, s5p, cmpp, w_buf):
    t = h.shape[0] // bt
    u, q, rows, kvw, kvs_b, kvw_b, gates = inproj_even(h, w_in_bf16, jnp.arange(h.shape[0]) % t)
    seq = lambda a: a.reshape(bt, t, a.shape[-1])
    feat = KV_GROUPS * HEAD_DIM
    y_s5, s5_state = s5_scan(seq(u), jnp.zeros((bt, S5_GROUPS, S5_STATE, 2), jnp.float32), s5p, S5_CHUNK)
    rows = seq(rows)
    kc = compress_prompt(rows[..., 0:feat], compress_params(cmpp[0], cmpp[1], cmpp[2]))
    vc = compress_prompt(rows[..., feat:2 * feat], compress_params(cmpp[3], cmpp[4], cmpp[5]))
    y_nsa = nsa_prompt(seq(q), seq(gates), kc, vc, seq(kvs_b), jnp.pad(seq(kvw_b), ((0, 0), (WINDOW, 0), (0, 0))))
    new_rows = rows.reshape(bt, t, 4, KV_GROUPS, HEAD_DIM)
    return (y_s5, y_nsa), s5_state, new_rows, last_rows(seq(kvw).reshape(bt, t, 2, KV_GROUPS, HEAD_DIM), w_buf)


def even_sample_mix(h, w_in_bf16, bt, s5_h0, pool, page_table, win_buf, s5p, cmpp):
    f32 = jnp.float32
    t = h.shape[0] // bt
    pos = page_table.shape[1] * PAGE_SIZE + jnp.arange(h.shape[0]) % t
    u, q, rows, kvw, _, _, gates = inproj_even(h, w_in_bf16, pos)
    seq = lambda a: a.reshape(bt, t, a.shape[-1])
    feat = KV_GROUPS * HEAD_DIM
    y_s5, s5_state = s5_scan(seq(u), s5_h0.astype(f32), s5p, t)
    rows, kvw = seq(rows), seq(kvw)
    pool_t = pool.astype(f32).transpose(0, 2, 3, 4, 1).reshape(pool.shape[0], 4 * feat, PAGE_SIZE)
    kc, vc = compress_sample(pool_t, page_table, rows[..., 0:feat], rows[..., feat:2 * feat],
                             compress_sample_params(cmpp[0], cmpp[1], cmpp[2]),
                             compress_sample_params(cmpp[3], cmpp[4], cmpp[5]))
    w_buf = win_buf.shape[1]
    win_f = win_buf.astype(f32)
    y_nsa = nsa_sample(q.reshape(bt, t, N_HEADS, HEAD_DIM), gates.reshape(bt, t, N_HEADS, 3), kc, vc, pool_t,
                       page_table, rows[..., 2 * feat:3 * feat], rows[..., 3 * feat:4 * feat],
                       win_f.transpose(0, 2, 3, 4, 1).reshape(bt, 2 * feat, w_buf), kvw[..., 0:feat],
                       kvw[..., feat:2 * feat])
    new_rows = rows.reshape(bt, t, 4, KV_GROUPS, HEAD_DIM)
    win = jnp.concatenate([win_f, kvw.reshape(bt, t, 2, KV_GROUPS, HEAD_DIM)], axis=1)
    return (y_s5, y_nsa), s5_state, new_rows, win[:, t:]


def ssd_scan(x, dt, a, bm, cm, h0, chunk):
    bt, t, nh, p = x.shape
    nch = t // chunk
    r = nh // SSD_GROUPS
    tri = jnp.arange(chunk)[:, None] >= jnp.arange(chunk)[None, :]

    def to_chunks(v):
        return jnp.moveaxis(v.reshape((bt, nch, chunk) + v.shape[2:]), 1, 0)

    def step(h, inp):
        xc, dtc, bc, cc = inp
        cum = jnp.cumsum(dtc * a, axis=1)
        seg = cum[:, :, None, :] - cum[:, None, :, :]
        decay = jnp.exp(jnp.where(tri[None, :, :, None], seg, NEG)).reshape(bt, chunk, chunk, SSD_GROUPS, r)
        cb = jnp.einsum('btgn,bsgn->btsg', cc, bc)
        xg = xc.reshape(bt, chunk, SSD_GROUPS, r, p)
        dg = dtc.reshape(bt, chunk, SSD_GROUPS, r)
        w = cb[..., None] * decay * dg[:, None]
        y_intra = jnp.einsum('btsgr,bsgrp->btgrp', w, xg)
        hg = h.reshape(bt, SSD_GROUPS, r, p, SSD_STATE)
        y_inter = jnp.einsum('btgn,bgrpn->btgrp', cc, hg) * jnp.exp(cum).reshape(bt, chunk, SSD_GROUPS, r)[..., None]
        wt = (jnp.exp(cum[:, -1:, :] - cum) * dtc).reshape(bt, chunk, SSD_GROUPS, r)
        h_new = (hg * jnp.exp(cum[:, -1]).reshape(bt, SSD_GROUPS, r)[..., None, None]
                 + jnp.einsum('bsgr,bsgrp,bsgn->bgrpn', wt, xg, bc))
        return h_new.reshape(bt, nh, p, SSD_STATE), (y_intra + y_inter).reshape(bt, chunk, nh, p)

    h_fin, ys = lax.scan(step, h0, (to_chunks(x), to_chunks(dt), to_chunks(bm), to_chunks(cm)))
    return jnp.moveaxis(ys, 0, 1).reshape(bt, t, nh, p), h_fin


def gated_rmsnorm(y, z, g):
    v = y * jax.nn.silu(z)
    bt, t, _ = v.shape
    vg = v.reshape(bt, t, SSD_GROUPS, SSD_INNER // SSD_GROUPS)
    vg = vg * lax.rsqrt(jnp.mean(vg * vg, -1, keepdims=True) + RMS_EPS)
    return vg.reshape(bt, t, SSD_INNER) * g


def odd_prompt_mix(h, w_in_bf16, bsz, sc_w, sc_b, cv_w, cv_b, dt_bias, a_log, d_skip, norm_g):
    a = -jnp.exp(a_log.astype(jnp.float32))
    y_sc, xbc_c, dt, zg, new_sc, new_conv = inproj_odd_prompt(h, w_in_bf16, bsz, sc_w, sc_b, cv_w, cv_b, dt_bias)
    y, h_new = ssd_prompt(xbc_c, dt, a, zg, 0, d_skip, norm_g)
    return (y_sc, y), new_sc, new_conv, h_new


def odd_mix(z, sc_buf, conv_buf, h0, chunk, sc_w, sc_b, cv_w, cv_b, dt_bias, a_log, d_skip, norm_g):
    f32 = jnp.float32
    bt, t, _ = z.shape
    a = -jnp.exp(a_log.astype(f32))
    o1 = SC_DIM
    o2 = 2 * SC_DIM
    o3 = 3 * SC_DIM
    o4 = o3 + SSD_INNER
    o5 = o4 + SSD_CONV_DIM
    sc_h = z[..., :o1]
    sc_bg = z[..., o1:o2]
    sc_cg = z[..., o2:o3]
    zg = z[..., o3:o4]
    xbc = z[..., o4:o5]
    dt_raw = z[..., o5:]
    conv_sc, new_sc = causal_conv(sc_cg * sc_h, sc_buf.astype(f32), sc_w, sc_b)
    y_sc = sc_bg * conv_sc
    xbc_c, new_conv = causal_conv(xbc, conv_buf.astype(f32), cv_w, cv_b)
    xbc_c = jax.nn.silu(xbc_c)
    gn = SSD_GROUPS * SSD_STATE
    xs = xbc_c[..., :SSD_INNER].reshape(bt, t, SSD_HEADS, SSD_HEAD_DIM)
    bm = xbc_c[..., SSD_INNER:SSD_INNER + gn].reshape(bt, t, SSD_GROUPS, SSD_STATE)
    cm = xbc_c[..., SSD_INNER + gn:].reshape(bt, t, SSD_GROUPS, SSD_STATE)
    dt = jax.nn.softplus((dt_raw + dt_bias).astype(f32))
    y, h_new = ssd_scan(xs, dt, a, bm, cm, h0.astype(f32), chunk)
    y = (y + d_skip[:, None] * xs).reshape(bt, t, SSD_INNER)
    y = gated_rmsnorm(y, zg, norm_g)
    return (y_sc, y), new_sc, new_conv, h_new


def moe_ffn(x, logits, w_gu_bf16, w_down_bf16):
    n, d = x.shape
    top_v, top_i = lax.top_k(logits, TOP_K)
    gate = jax.nn.softmax(top_v, axis=-1)
    flat_e = top_i.reshape(-1)
    blk = 128
    assert (TOP_K * n) % blk == 0
    onehot = jax.nn.one_hot(flat_e, N_EXPERTS, dtype=jnp.float32).reshape(-1, blk, N_EXPERTS)
    tri = (jnp.arange(blk)[:, None] >= jnp.arange(blk)[None, :]).astype(jnp.float32)
    local = jnp.einsum('ij,bjk->bik', tri, onehot)
    block_total = local[:, -1, :]
    block_off = jnp.cumsum(block_total, axis=0) - block_total
    incl = (local + block_off[:, None, :]).reshape(-1, N_EXPERTS)
    rank = jnp.take_along_axis(incl, flat_e[:, None], axis=1)[:, 0].astype(jnp.int32) - 1
    counts = jnp.sum(block_total, axis=0).astype(jnp.int32)
    padded = ((counts + ROW_TILE - 1) // ROW_TILE) * ROW_TILE
    pad_start = jnp.cumsum(padded) - padded
    dest = (pad_start[flat_e] + rank).astype(jnp.int32)
    n_tiles = (TOP_K * n) // ROW_TILE + N_EXPERTS
    rows = n_tiles * ROW_TILE
    row_token = jnp.zeros((rows,), jnp.int32).at[dest].set(jnp.arange(TOP_K * n, dtype=jnp.int32) // TOP_K,
                                                           unique_indices=True, mode='promise_in_bounds')
    tile_end = jnp.cumsum(padded) // ROW_TILE
    tile_expert = jnp.minimum(jnp.searchsorted(tile_end, jnp.arange(n_tiles), side='right'),
                              N_EXPERTS - 1).astype(jnp.int32)
    n_used = tile_end[-1:].astype(jnp.int32)
    xs = x.at[row_token].get(mode='promise_in_bounds')
    ys = grouped_ffn(xs, w_gu_bf16, w_down_bf16, tile_expert, n_used)
    dest = dest.reshape(n, TOP_K)
    y0 = ys.at[dest[:, 0]].get(mode='promise_in_bounds')
    y1 = ys.at[dest[:, 1]].get(mode='promise_in_bounds')
    return y0, y1, gate


def kernel(x_prompt, x_sample, state_s5, cache_nsa_kv, state_win_kv, state_sc_conv, state_ssd_conv, state_ssd,
           page_table, ln_g, ln_b, w_in_even, s5_lam_re, s5_lam_im, s5_log_dt, s5_b, s5_c, s5_d, s5_w_glu,
           nsa_wk1, nsa_wk2, nsa_pe_k, nsa_wv1, nsa_wv2, nsa_pe_v, w_out_even, ffn_w_gu, ffn_w_down,
           w_in_odd, sc_conv_w, sc_conv_b, ssd_conv_w, ssd_conv_b, ssd_dt_bias, ssd_a_log, ssd_d, ssd_norm_g,
           w_out_odd, moe_router, moe_router_b, moe_w_gu, moe_w_down):
    f32 = jnp.float32
    bf16 = jnp.bfloat16
    bp, tp, d = x_prompt.shape
    bs, ts, _ = x_sample.shape
    n_p = bp * tp
    n_s = bs * ts
    w_buf = state_win_kv.shape[2]
    streams = [x_prompt.astype(f32).reshape(n_p, d), x_sample.astype(f32).reshape(n_s, d)]

    def flat(parts, n_rows):
        return [p.reshape(n_rows, p.shape[-1]) for p in parts]

    def out_proj(parts, w_out, width, h, g, b, router=None):
        w = w_out.astype(bf16)
        return matmul(flat(parts, h.shape[0]), [w[:width], w[width:]], ln=(h, g, b), router=router)

    def single_expert(n_rows):
        n_tiles = pl.cdiv(n_rows, min(ROW_TILE, n_rows))
        return jnp.zeros((n_tiles,), jnp.int32), jnp.full((1,), n_tiles, jnp.int32)

    s5p = s5_params(s5_lam_re[0], s5_lam_im[0], s5_log_dt[0], s5_b[0], s5_c[0], s5_d[0], s5_w_glu[0])
    cmpp = (nsa_wk1[0], nsa_wk2[0], nsa_pe_k[0], nsa_wv1[0], nsa_wv2[0], nsa_pe_v[0])
    w_in = w_in_even[0].astype(bf16)
    mix_p, s5_p, kv_p, win_p = even_prompt_mix(streams[0], w_in, bp, s5p, cmpp, w_buf)
    mix_s, s5_s, kv_s, win_s = even_sample_mix(streams[1], w_in, bs, state_s5[0], cache_nsa_kv[0], page_table,
                                               state_win_kv[0], s5p, cmpp)
    streams = [out_proj(mix, w_out_even[0], S5_DIM, h, ln_g[0, 0], ln_b[0, 0])
               for mix, h in zip((mix_p, mix_s), streams)]
    w_gu, w_down = to_bf16(ffn_w_gu), to_bf16(ffn_w_down)
    streams = [grouped_ffn(h, w_gu, w_down, *single_expert(h.shape[0]), ln=(ln_g[0, 1], ln_b[0, 1]))
               for h in streams]

    oddp = (sc_conv_w[0], sc_conv_b[0], ssd_conv_w[0], ssd_conv_b[0], ssd_dt_bias[0],
            ssd_a_log[0], ssd_d[0], ssd_norm_g[0])
    w_in = w_in_odd[0].astype(bf16)
    mix_p, scc_p, sdc_p, ssd_p = odd_prompt_mix(streams[0], w_in, bp, *oddp)
    zs = matmul([streams[1]], [w_in]).reshape(bs, ts, -1)
    mix_s, scc_s, sdc_s, ssd_s = odd_mix(zs, state_sc_conv[0], state_ssd_conv[0], state_ssd[0], ts, *oddp)
    outs = [out_proj(mix, w_out_odd[0], SC_DIM, h, ln_g[1, 0], ln_b[1, 0], router=(moe_router[0], moe_router_b[0]))
            for mix, h in zip((mix_p, mix_s), streams)]
    h = jnp.concatenate([o[0] for o in outs], axis=0)
    logits = jnp.concatenate([o[1] for o in outs], axis=0)[:, :N_EXPERTS]
    y0, y1, gate = moe_ffn(h, logits, to_bf16(moe_w_gu[0]), to_bf16(moe_w_down[0]))
    hp = moe_combine_ln(h, y0, y1, gate, ln_g[1, 1], ln_b[1, 1], 0, n_p).reshape(bp, tp, d)
    hs = moe_combine_ln(h, y0, y1, gate, ln_g[1, 1], ln_b[1, 1], n_p, n_s).reshape(bs, ts, d)
    st = lambda a, ref: a[None].astype(ref.dtype)
    return (hp.astype(x_prompt.dtype), hs.astype(x_sample.dtype),
            st(s5_p, state_s5), st(s5_s, state_s5),
            st(kv_p, cache_nsa_kv), st(kv_s, cache_nsa_kv),
            st(win_p, state_win_kv), st(win_s, state_win_kv),
            st(scc_p, state_sc_conv), st(scc_s, state_sc_conv),
            st(sdc_p, state_ssd_conv), st(sdc_s, state_ssd_conv),
            st(ssd_p, state_ssd), st(ssd_s, state_ssd))
```

```python
import functools
import math

import jax
import jax.numpy as jnp
from jax import lax
from jax.experimental import pallas as pl
from jax.experimental.pallas import tpu as pltpu

D_MODEL = 1024
DEPTH = 2
ALPHA = (2.0 * DEPTH) ** 0.25
LN_EPS = 1e-5
RMS_EPS = 1e-5
NEG = -1e30

S5_DIM = D_MODEL // 2
S5_GROUP = 16
S5_GROUPS = S5_DIM // S5_GROUP
S5_STATE = 64

HEAD_DIM = 64
N_HEADS = (D_MODEL // 2) // HEAD_DIM
KV_GROUPS = 2
HEADS_PER_GROUP = N_HEADS // KV_GROUPS
CMP_STRIDE = 16
CMP_LEN = 2 * CMP_STRIDE
SEL_BLOCK = 64
N_SEL = 16
WINDOW = 512
Q_BLOCK = 128
ROPE_THETA = 500000.0
ROT_DIM = HEAD_DIM // 4
FORCE = 1e4
NSA_Q = N_HEADS * HEAD_DIM
NSA_KV = 2 * KV_GROUPS * HEAD_DIM

SC_DIM = D_MODEL // 2
SC_WIDTH = 3
SSD_HEAD_DIM = 64
SSD_HEADS = 16
SSD_INNER = SSD_HEADS * SSD_HEAD_DIM
SSD_GROUPS = 4
SSD_STATE = 128
SSD_CONV = 4
SSD_CONV_DIM = SSD_INNER + 2 * SSD_GROUPS * SSD_STATE
SSD_CHUNK = 128

D_FF = 2816
N_EXPERTS = 8
TOP_K = 2

VMEM_LIMIT_BYTES = 56 * 1024 * 1024
LANES = 128
S5_N = S5_GROUPS * S5_STATE
S5_LT = S5_N // LANES
S5_CHUNK = 256
SEL_TILE = 1024
QK_SCALE = HEAD_DIM ** -0.5 * math.log2(math.e)
REMOVED = -3e38
PAGE_SIZE = 128
PAGES_PER_STEP = 64
NEW_PAD = 128
CAST_ROWS = 512
CAST_SPLIT = 4
TAIL = 8
ROW_TILE = 512
FF_TILE = D_FF // 2


def _cparams(*sem):
    return pltpu.CompilerParams(dimension_semantics=sem, vmem_limit_bytes=VMEM_LIMIT_BYTES)


def _deepnorm(resid, update, g, b):
    y = ALPHA * resid + update
    mu = jnp.mean(y, axis=-1, keepdims=True)
    yc = y - mu
    var = jnp.mean(yc * yc, axis=-1, keepdims=True)
    return yc * lax.rsqrt(var + LN_EPS) * g + b


def _mm_kernel(*refs, n_in, fuse_ln, router):
    xs, ws = refs[0:n_in], refs[n_in:2 * n_in]
    o_ref = refs[-2] if router else refs[-1]
    acc = None
    for x_ref, w_ref in zip(xs, ws):
        part = jnp.dot(x_ref[...].astype(jnp.bfloat16), w_ref[...], preferred_element_type=jnp.float32)
        acc = part if acc is None else acc + part
    if fuse_ln:
        r_ref, g_ref, b_ref = refs[2 * n_in:2 * n_in + 3]
        acc = _deepnorm(r_ref[...], acc, g_ref[...], b_ref[...])
    o_ref[...] = acc
    if router:
        wr_ref, br_ref = refs[-4], refs[-3]
        xparts = _split3(acc)
        logits = br_ref[...]
        for i in range(2):
            for j in range(2 - i):
                logits = logits + jnp.dot(xparts[i], wr_ref[j], preferred_element_type=jnp.float32)
        refs[-1][...] = logits


def matmul(xs, ws_bf16, ln=None, router=None):
    m = xs[0].shape[0]
    n = ws_bf16[0].shape[1]
    tile = min(ROW_TILE, m)
    row = lambda width: pl.BlockSpec((tile, width), lambda i: (i, 0))
    fixed = lambda shape: pl.BlockSpec(shape, lambda i: (0, 0), pipeline_mode=pl.Buffered(1))
    in_specs = [row(x.shape[1]) for x in xs] + [fixed(w.shape) for w in ws_bf16]
    args = list(xs) + list(ws_bf16)
    if ln is not None:
        resid, g, b = ln
        in_specs += [row(n), fixed((1, n)), fixed((1, n))]
        args += [resid, g.reshape(1, n), b.reshape(1, n)]
    out_specs, out_shape = row(n), jax.ShapeDtypeStruct((m, n), jnp.float32)
    if router is not None:
        w_r, b_r = router
        pad = LANES - w_r.shape[1]
        w_parts = jnp.stack(_split3(jnp.pad(w_r.astype(jnp.float32), ((0, 0), (0, pad)))))
        in_specs += [pl.BlockSpec((3, n, LANES), lambda i: (0, 0, 0), pipeline_mode=pl.Buffered(1)),
                     fixed((1, LANES))]
        args += [w_parts, jnp.pad(b_r.astype(jnp.float32), (0, pad)).reshape(1, LANES)]
        out_specs, out_shape = [out_specs, row(LANES)], [out_shape, jax.ShapeDtypeStruct((m, LANES), jnp.float32)]
    return pl.pallas_call(
        functools.partial(_mm_kernel, n_in=len(xs), fuse_ln=ln is not None, router=router is not None),
        grid=(pl.cdiv(m, tile),),
        in_specs=in_specs,
        out_specs=out_specs,
        out_shape=out_shape,
        compiler_params=_cparams("parallel"),
        name="matmul",
    )(*args)


def _combine_kernel(h_ref, y0_ref, y1_ref, gate_ref, g_ref, b_ref, o_ref):
    gate = gate_ref[...]
    f = gate[:, 0:1] * y0_ref[...] + gate[:, 1:2] * y1_ref[...]
    o_ref[...] = _deepnorm(h_ref[...], f, g_ref[...], b_ref[...])


def moe_combine_ln(h, y0, y1, gate, g, b, row0, n_rows):
    n = h.shape[1]
    tile = min(ROW_TILE, n_rows)
    assert row0 % tile == 0 and n_rows % tile == 0
    first = row0 // tile
    row = lambda width: pl.BlockSpec((tile, width), lambda i: (first + i, 0))
    fixed = pl.BlockSpec((1, n), lambda i: (0, 0))
    return pl.pallas_call(
        _combine_kernel,
        grid=(n_rows // tile,),
        in_specs=[row(n), row(n), row(n), row(TOP_K), fixed, fixed],
        out_specs=pl.BlockSpec((tile, n), lambda i: (i, 0)),
        out_shape=jax.ShapeDtypeStruct((n_rows, n), jnp.float32),
        compiler_params=_cparams("parallel"),
        name="moe_combine_ln",
    )(h, y0, y1, gate, g.reshape(1, n), b.reshape(1, n))


def _cast_kernel(*refs):
    o_ref = refs[-1]
    o_ref[...] = jnp.concatenate([r[...].astype(o_ref.dtype) for r in refs[:-1]], axis=1)


def to_bf16(w):
    shape = w.shape
    w2 = w.reshape(-1, shape[-1])
    rows, cols = w2.shape
    split = CAST_SPLIT if cols % (CAST_SPLIT * LANES) == 0 else 1
    out = pl.pallas_call(
        _cast_kernel,
        grid=(pl.cdiv(rows, CAST_ROWS),),
        in_specs=[pl.BlockSpec((CAST_ROWS, cols // split), lambda i, c=c: (i, c)) for c in range(split)],
        out_specs=pl.BlockSpec((CAST_ROWS, cols), lambda i: (i, 0)),
        out_shape=jax.ShapeDtypeStruct((rows, cols), jnp.bfloat16),
        compiler_params=_cparams("parallel"),
        name="to_bf16",
    )(*([w2] * split))
    return out.reshape(shape)


def _ffn_kernel(te_ref, nt_ref, x_ref, wg_ref, wu_ref, wd_ref, *rest, fuse_ln):
    o_ref = rest[-1]
    t = pl.program_id(0)
    j = pl.program_id(1)

    @pl.when(t < nt_ref[0])
    def _():
        x = x_ref[...].astype(jnp.bfloat16)
        g = jnp.dot(x, wg_ref[...], preferred_element_type=jnp.float32)
        u = jnp.dot(x, wu_ref[...], preferred_element_type=jnp.float32)
        h = (g * jax.nn.sigmoid(g) * u).astype(jnp.bfloat16)
        part = jnp.dot(h, wd_ref[...], preferred_element_type=jnp.float32)

        @pl.when(j == 0)
        def _():
            o_ref[...] = part

        @pl.when(j > 0)
        def _():
            if fuse_ln:
                o_ref[...] = _deepnorm(x_ref[...], o_ref[...] + part, rest[0][...], rest[1][...])
            else:
                o_ref[...] += part

    @pl.when(jnp.logical_and(t >= nt_ref[0], j == 0))
    def _():
        o_ref[...] = jnp.zeros_like(o_ref)


def grouped_ffn(x, w_gu_bf16, w_down_bf16, tile_expert, n_tiles_used, ln=None):
    r, d = x.shape
    nf = D_FF // FF_TILE
    assert nf == 2
    tile = min(ROW_TILE, r)
    n_tiles = pl.cdiv(r, tile)
    in_specs = [
        pl.BlockSpec((tile, d), lambda t, j, te, nt: (t, 0)),
        pl.BlockSpec((None, d, FF_TILE), lambda t, j, te, nt: (te[t], 0, j)),
        pl.BlockSpec((None, d, FF_TILE), lambda t, j, te, nt: (te[t], 0, nf + j)),
        pl.BlockSpec((None, FF_TILE, d), lambda t, j, te, nt: (te[t], j, 0)),
    ]
    args = [tile_expert, n_tiles_used, x, w_gu_bf16, w_gu_bf16, w_down_bf16]
    if ln is not None:
        in_specs += [pl.BlockSpec((1, d), lambda t, j, te, nt: (0, 0))] * 2
        args += [ln[0].reshape(1, d), ln[1].reshape(1, d)]
    grid_spec = pltpu.PrefetchScalarGridSpec(
        num_scalar_prefetch=2,
        grid=(n_tiles, nf),
        in_specs=in_specs,
        out_specs=pl.BlockSpec((tile, d), lambda t, j, te, nt: (t, 0)),
    )
    return pl.pallas_call(
        functools.partial(_ffn_kernel, fuse_ln=ln is not None),
        grid_spec=grid_spec,
        out_shape=jax.ShapeDtypeStruct((r, d), jnp.float32),
        compiler_params=_cparams("parallel", "arbitrary"),
        name="grouped_ffn",
    )(*args)


def _s5_kernel(u_ref, perm_ref, h0r_ref, h0i_ref, ar_ref, ai_ref, bbr_ref, bbi_ref, cr_ref, ci_ref, d_ref, wglu_ref,
               y_ref, hro_ref, hio_ref, bur, bui, sr, si, hr, hi, *, chains, chunk):
    j = pl.program_id(0)

    @pl.when(j == 0)
    def _():
        hr[...] = h0r_ref[...]
        hi[...] = h0i_ref[...]

    rows_n = chains * chunk
    u = u_ref[...].reshape(rows_n, S5_DIM)
    to_tc = perm_ref[...]
    ub = jnp.dot(to_tc, u.astype(jnp.bfloat16), preferred_element_type=jnp.float32).astype(jnp.bfloat16)
    hd, hn = S5_DIM // 2, S5_N // 2

    def b_proj(w_ref):
        return jnp.concatenate([jnp.dot(ub[:, h * hd:(h + 1) * hd], w_ref[h * hd:(h + 1) * hd, h * hn:(h + 1) * hn],
                                        preferred_element_type=jnp.float32) for h in range(2)], axis=1)

    bu_r = b_proj(bbr_ref)
    bu_i = b_proj(bbi_ref)
    for k in range(S5_LT):
        bur[k] = bu_r[:, k * LANES:(k + 1) * LANES]
        bui[k] = bu_i[:, k * LANES:(k + 1) * LANES]
    ar = [jnp.broadcast_to(ar_ref[:, k * LANES:(k + 1) * LANES], (chains, LANES)) for k in range(S5_LT)]
    ai = [jnp.broadcast_to(ai_ref[:, k * LANES:(k + 1) * LANES], (chains, LANES)) for k in range(S5_LT)]

    def body(t, carry):
        rows = pl.ds(t * chains, chains)
        out = []
        for k in range(S5_LT):
            xr, xi = carry[2 * k], carry[2 * k + 1]
            nr = ar[k] * xr - ai[k] * xi + bur[k, rows, :]
            ni = ar[k] * xi + ai[k] * xr + bui[k, rows, :]
            sr[k, rows, :] = nr
            si[k, rows, :] = ni
            out += [nr, ni]
        return tuple(out)

    init = []
    for k in range(S5_LT):
        init += [hr[:, k * LANES:(k + 1) * LANES], hi[:, k * LANES:(k + 1) * LANES]]
    fin = lax.fori_loop(0, chunk, body, tuple(init), unroll=2)
    xr = jnp.concatenate(fin[0::2], axis=1)
    xi = jnp.concatenate(fin[1::2], axis=1)
    hr[...] = xr
    hi[...] = xi
    hro_ref[...] = xr
    hio_ref[...] = xi
    s_r = jnp.concatenate([sr[k] for k in range(S5_LT)], axis=1).astype(jnp.bfloat16)
    s_i = jnp.concatenate([si[k] for k in range(S5_LT)], axis=1).astype(jnp.bfloat16)
    y = jnp.concatenate(
        [jnp.dot(s_r[:, h * hn:(h + 1) * hn], cr_ref[h * hn:(h + 1) * hn, h * hd:(h + 1) * hd],
                 preferred_element_type=jnp.float32)
         - jnp.dot(s_i[:, h * hn:(h + 1) * hn], ci_ref[h * hn:(h + 1) * hn, h * hd:(h + 1) * hd],
                   preferred_element_type=jnp.float32) for h in range(2)], axis=1)
    y = sum(lax.dot_general(to_tc, part, (((0,), (0,)), ((), ())), preferred_element_type=jnp.float32)
            for part in _split3(y)) + d_ref[...] * u
    z = jax.nn.gelu(y)
    gate = jax.nn.sigmoid(jnp.dot(z.astype(jnp.bfloat16), wglu_ref[...], preferred_element_type=jnp.float32))
    y_ref[...] = (z * gate).reshape(chains, chunk, S5_DIM)


def s5_params(lam_re, lam_im, log_dt, b, c, d, w_glu):
    f32 = jnp.float32
    dt = jnp.exp(log_dt.astype(f32))[:, None]
    mag = jnp.exp(lam_re * dt)
    ang = lam_im * dt
    ab_re = mag * jnp.cos(ang)
    ab_im = mag * jnp.sin(ang)
    den = lam_re * lam_re + lam_im * lam_im
    nr = ab_re - 1.0
    coef_re = (nr * lam_re + ab_im * lam_im) / den
    coef_im = (ab_im * lam_re - nr * lam_im) / den
    b_re = b[..., 0].astype(f32)
    b_im = b[..., 1].astype(f32)
    bb_re = coef_re[..., None] * b_re - coef_im[..., None] * b_im
    bb_im = coef_re[..., None] * b_im + coef_im[..., None] * b_re
    eye = jnp.eye(S5_GROUPS, dtype=f32)
    bbr = jnp.einsum('gnk,gh->gkhn', bb_re, eye).reshape(S5_DIM, S5_N).astype(jnp.bfloat16)
    bbi = jnp.einsum('gnk,gh->gkhn', bb_im, eye).reshape(S5_DIM, S5_N).astype(jnp.bfloat16)
    cr = jnp.einsum('gkn,gh->gnhk', c[..., 0].astype(f32), eye).reshape(S5_N, S5_DIM).astype(jnp.bfloat16)
    ci = jnp.einsum('gkn,gh->gnhk', c[..., 1].astype(f32), eye).reshape(S5_N, S5_DIM).astype(jnp.bfloat16)
    return (ab_re.reshape(1, S5_N), ab_im.reshape(1, S5_N), bbr, bbi, cr, ci,
            d.astype(f32).reshape(1, S5_DIM), w_glu.astype(jnp.bfloat16))


def s5_scan(u, h0, params, chunk):
    chains, t, _ = u.shape
    ar, ai, bbr, bbi, cr, ci, d, wglu = params
    h0r = h0[..., 0].reshape(chains, S5_N)
    h0i = h0[..., 1].reshape(chains, S5_N)
    full = lambda shape: pl.BlockSpec(shape, lambda j: (0,) * len(shape))
    rows = chains * chunk
    r = jnp.arange(rows)
    to_tc = (r[None, :] == (r[:, None] % chains) * chunk + r[:, None] // chains).astype(jnp.bfloat16)
    y, hr, hi = pl.pallas_call(
        functools.partial(_s5_kernel, chains=chains, chunk=chunk),
        grid=(t // chunk,),
        in_specs=[pl.BlockSpec((chains, chunk, S5_DIM), lambda j: (0, j, 0)), full((rows, rows)),
                  full((chains, S5_N)), full((chains, S5_N)), full((1, S5_N)), full((1, S5_N)),
                  full((S5_DIM, S5_N)), full((S5_DIM, S5_N)), full((S5_N, S5_DIM)), full((S5_N, S5_DIM)),
                  full((1, S5_DIM)), full((S5_DIM, S5_DIM))],
        out_specs=[pl.BlockSpec((chains, chunk, S5_DIM), lambda j: (0, j, 0)),
                   full((chains, S5_N)), full((chains, S5_N))],
        out_shape=[jax.ShapeDtypeStruct((chains, t, S5_DIM), jnp.float32),
                   jax.ShapeDtypeStruct((chains, S5_N), jnp.float32),
                   jax.ShapeDtypeStruct((chains, S5_N), jnp.float32)],
        scratch_shapes=[pltpu.VMEM((S5_LT, rows, LANES), jnp.float32)] * 4
                       + [pltpu.VMEM((chains, S5_N), jnp.float32)] * 2,
        compiler_params=_cparams("arbitrary"),
        name="s5_scan",
    )(u, to_tc, h0r, h0i, ar, ai, bbr, bbi, cr, ci, d, wglu)
    new_state = jnp.stack([hr.reshape(chains, S5_GROUPS, S5_STATE), hi.reshape(chains, S5_GROUPS, S5_STATE)],
                          axis=-1)
    return y, new_state


def _dot_nt(a, b):
    return lax.dot_general(a, b, (((1,), (1,)), ((), ())), preferred_element_type=jnp.float32)


def _split3(x):
    hi = x.astype(jnp.bfloat16)
    rem = x - hi.astype(jnp.float32)
    mid = rem.astype(jnp.bfloat16)
    lo = (rem - mid.astype(jnp.float32)).astype(jnp.bfloat16)
    return hi, mid, lo


def _softmax_rows(s, mask):
    s = jnp.where(mask, s, NEG)
    m = jnp.max(s, axis=-1, keepdims=True)
    p = jnp.exp2(s - m)
    inv = jnp.where(m > 0.5 * NEG, 1.0 / jnp.sum(p, axis=-1, keepdims=True), 0.0)
    return p * inv


def _nsa_prompt_kernel(q_ref, gate_ref, kc_ref, vc_ref, ks_ref, vs_ref, kw_ref, vw_ref, o_ref, *, n_cmp, n_blk):
    f32, bf16 = jnp.float32, jnp.bfloat16
    r4 = HEADS_PER_GROUP
    n_cpad = kc_ref.shape[0]
    start = pl.program_id(1) * Q_BLOCK
    q = q_ref[...] * QK_SCALE
    gate = gate_ref[...]
    lane = lax.broadcasted_iota(jnp.int32, (Q_BLOCK, LANES), 1)
    qpos = start + lax.broadcasted_iota(jnp.int32, (Q_BLOCK, 1), 0)
    n_idx = lax.broadcasted_iota(jnp.int32, (Q_BLOCK, n_cpad), 1)
    cmask = (((n_idx * CMP_STRIDE + (CMP_LEN - 1)) <= qpos) & (n_idx < n_cmp))[None]
    ratio = SEL_BLOCK // CMP_STRIDE
    gsum = (lax.broadcasted_iota(jnp.int32, (n_blk, n_cpad), 1) // ratio
            == lax.broadcasted_iota(jnp.int32, (n_blk, n_cpad), 0)).astype(bf16)
    blk = lax.broadcasted_iota(jnp.int32, (n_blk, Q_BLOCK), 0)
    blk_f = blk.astype(f32)
    jq = (start + lax.broadcasted_iota(jnp.int32, (n_blk, Q_BLOCK), 1)) // SEL_BLOCK
    force = jnp.where((blk == 0) | (blk == jq) | (blk == jq - 1), FORCE, 0.0)
    qgs, o_cs, sels = [], [], []
    for g in range(KV_GROUPS):
        keep = (lane < HEAD_DIM) if g == 0 else (lane >= HEAD_DIM)
        parts = []
        for r in range(r4):
            h = r4 * g + r
            tile = q[:, (h // 2) * LANES:(h // 2 + 1) * LANES]
            if h % 2 != g:
                tile = pltpu.roll(tile, HEAD_DIM, axis=1)
            parts.append(jnp.where(keep, tile, 0.0))
        qg = jnp.concatenate(parts, axis=0).astype(bf16)
        qgs.append(qg)

        p_c = _softmax_rows(_dot_nt(qg, kc_ref[...]).reshape(r4, Q_BLOCK, n_cpad), cmask)
        o_cs.append(jnp.dot(p_c.reshape(r4 * Q_BLOCK, n_cpad).astype(bf16), vc_ref[...],
                            preferred_element_type=f32).reshape(r4, Q_BLOCK, LANES))
        psum = p_c[0] + p_c[1] + p_c[2] + p_c[3]
        imp_t = sum(_dot_nt(gsum, part) for part in _split3(psum))

        score = jnp.where(blk <= jq, imp_t + force, NEG)
        sel_t = jnp.zeros((n_blk, Q_BLOCK), f32)
        for _ in range(min(N_SEL, n_blk)):
            m = jnp.max(score, axis=0, keepdims=True)
            idx = jnp.min(jnp.where(score == m, blk_f, float(n_blk)), axis=0, keepdims=True)
            hit = blk_f == idx
            sel_t = jnp.where(hit & (m > 0.5 * NEG), 1.0, sel_t)
            score = jnp.where(hit, REMOVED, score)
        sels.append(sel_t.T)

    n_full = start // SEL_TILE
    expand0 = (lax.broadcasted_iota(jnp.int32, (n_blk, SEL_TILE), 0)
               == lax.broadcasted_iota(jnp.int32, (n_blk, SEL_TILE), 1) // SEL_BLOCK).astype(bf16)

    def tile_update(i, carry, causal):
        off = pl.multiple_of(i * SEL_TILE, SEL_TILE)
        k = ks_ref[pl.ds(off, SEL_TILE), :]
        v = vs_ref[pl.ds(off, SEL_TILE), :]
        vlane = lax.broadcasted_iota(jnp.int32, (SEL_TILE, LANES), 1)
        v_ones = [jnp.where((vlane < HEAD_DIM) == (g == 0), v, jnp.ones_like(v)) for g in range(KV_GROUPS)]
        out = []
        for g in range(KV_GROUPS):
            m_run, l_run, acc = carry[g]
            s_t = _dot_nt(qgs[g], k).reshape(r4, Q_BLOCK, SEL_TILE)
            shifted = pltpu.roll(sels[g], (n_blk - i * (SEL_TILE // SEL_BLOCK)) % n_blk, axis=1).astype(bf16)
            mk = jnp.dot(shifted, expand0, preferred_element_type=f32) > 0.5
            if causal:
                kpos = i * SEL_TILE + lax.broadcasted_iota(jnp.int32, (Q_BLOCK, SEL_TILE), 1)
                mk = mk & (kpos <= qpos)
            s_t = jnp.where(mk[None], s_t, NEG)
            m_new = jnp.maximum(m_run, jnp.max(s_t, axis=-1, keepdims=True))
            alpha = jnp.exp2(m_run - m_new)
            p = jnp.exp2((s_t - m_new).astype(bf16))
            pv = jnp.dot(p.reshape(r4 * Q_BLOCK, SEL_TILE), v_ones[g], preferred_element_type=f32)
            pv = pv.reshape(r4, Q_BLOCK, LANES)
            l_new = alpha * l_run + pv[:, :, (1 - g) * HEAD_DIM:(1 - g) * HEAD_DIM + 1]
            out.append((m_new, l_new, alpha * acc + pv))
        return tuple(out)

    init = (jnp.full((r4, Q_BLOCK, 1), NEG, f32), jnp.zeros((r4, Q_BLOCK, 1), f32),
            jnp.zeros((r4, Q_BLOCK, LANES), f32))
    carry = lax.fori_loop(0, n_full, lambda i, c: tile_update(i, c, False), (init, init))
    fin = tile_update(n_full, carry, True)

    n_win = WINDOW + Q_BLOCK
    woff = pl.multiple_of(start, Q_BLOCK)
    kwin = kw_ref[pl.ds(woff, n_win), :]
    vwin = vw_ref[pl.ds(woff, n_win), :]
    wpos = start - WINDOW + lax.broadcasted_iota(jnp.int32, (Q_BLOCK, n_win), 1)
    wmask = ((wpos <= qpos) & (wpos > qpos - WINDOW) & (wpos >= 0))[None]
    heads = [None] * N_HEADS
    for g in range(KV_GROUPS):
        m_fin, l_fin, acc = fin[g]
        o_s = acc * jnp.where(m_fin > 0.5 * NEG, 1.0 / l_fin, 0.0)
        p_w = _softmax_rows(_dot_nt(qgs[g], kwin).reshape(r4, Q_BLOCK, n_win), wmask)
        o_w = jnp.dot(p_w.reshape(r4 * Q_BLOCK, n_win).astype(bf16), vwin,
                      preferred_element_type=f32).reshape(r4, Q_BLOCK, LANES)
        for r in range(r4):
            h = r4 * g + r
            heads[h] = (gate[:, 3 * h:3 * h + 1] * o_cs[g][r] + gate[:, 3 * h + 1:3 * h + 2] * o_s[r]
                        + gate[:, 3 * h + 2:3 * h + 3] * o_w[r])

    tiles = []
    for j in range(N_HEADS // 2):
        even, odd = heads[2 * j], heads[2 * j + 1]
        if j // 2 == 0:
            tiles.append(jnp.where(lane < HEAD_DIM, even, pltpu.roll(odd, HEAD_DIM, axis=1)))
        else:
            tiles.append(jnp.where(lane < HEAD_DIM, pltpu.roll(even, HEAD_DIM, axis=1), odd))
    o_ref[...] = jnp.concatenate(tiles, axis=1)


def nsa_prompt(q, gates, kc, vc, kvs, kvw_pad):
    b, t, _ = q.shape
    n_cpad = kc.shape[1]
    kern = functools.partial(_nsa_prompt_kernel, n_cmp=t // CMP_STRIDE - 1, n_blk=t // SEL_BLOCK)
    whole = lambda rows, c=0: pl.BlockSpec((None, rows, LANES), lambda i, j: (i, 0, c))
    return pl.pallas_call(
        kern,
        grid=(b, t // Q_BLOCK),
        in_specs=[pl.BlockSpec((None, Q_BLOCK, NSA_Q), lambda i, j: (i, j, 0)),
                  pl.BlockSpec((None, Q_BLOCK, 3 * N_HEADS), lambda i, j: (i, j, 0)),
                  whole(n_cpad), whole(n_cpad), whole(t, 0), whole(t, 1), whole(t + WINDOW, 0),
                  whole(t + WINDOW, 1)],
        out_specs=pl.BlockSpec((None, Q_BLOCK, NSA_Q), lambda i, j: (i, j, 0)),
        out_shape=jax.ShapeDtypeStruct((b, t, NSA_Q), jnp.float32),
        compiler_params=_cparams("parallel", "arbitrary"),
        name="nsa_prompt",
    )(q, gates, kc, vc, kvs, kvs, kvw_pad, kvw_pad)


def _compress_kernel(ch_ref, pet_ref, peb_ref, w1t_ref, w1b_ref, w2_ref, o_ref):
    bf16 = jnp.bfloat16
    ch = ch_ref[...]
    n_ch = ch.shape[0]
    a = jnp.dot((ch + pet_ref[...]).astype(bf16), w1t_ref[...], preferred_element_type=jnp.float32)
    b = jnp.dot((ch + peb_ref[...]).astype(bf16), w1b_ref[...], preferred_element_type=jnp.float32)
    pre = a + pltpu.roll(b, n_ch - 1, axis=0)
    o_ref[...] = jnp.dot(jax.nn.gelu(pre).astype(bf16), w2_ref[...],
                         preferred_element_type=jnp.float32).astype(o_ref.dtype)


def compress_params(w1, w2, pe):
    f32 = jnp.float32
    eye = jnp.eye(KV_GROUPS, dtype=f32)
    w1r = w1.astype(f32).reshape(2, CMP_STRIDE, HEAD_DIM, HEAD_DIM)
    big = jnp.einsum('hjde,gk->hjgdke', w1r, eye).reshape(2, CMP_STRIDE * LANES, LANES).astype(jnp.bfloat16)
    w2bd = jnp.einsum('de,gk->gdke', w2.astype(f32), eye).reshape(LANES, LANES).astype(jnp.bfloat16)
    per = pe.astype(f32).reshape(2, CMP_STRIDE, 1, HEAD_DIM)
    pe_rows = jnp.broadcast_to(per, (2, CMP_STRIDE, KV_GROUPS, HEAD_DIM)).reshape(2, 1, CMP_STRIDE * LANES)
    return pe_rows[0], pe_rows[1], big[0], big[1], w2bd


def compress_prompt(x, params):
    b, t, _ = x.shape
    n_ch = t // CMP_STRIDE
    ch = x.reshape(b, n_ch, CMP_STRIDE * LANES)
    pet, peb, w1t, w1b, w2bd = params
    full = lambda shape: pl.BlockSpec(shape, lambda i: (0,) * len(shape))
    return pl.pallas_call(
        _compress_kernel,
        grid=(b,),
        in_specs=[pl.BlockSpec((None, n_ch, CMP_STRIDE * LANES), lambda i: (i, 0, 0)),
                  full((1, CMP_STRIDE * LANES)), full((1, CMP_STRIDE * LANES)),
                  full((CMP_STRIDE * LANES, LANES)), full((CMP_STRIDE * LANES, LANES)), full((LANES, LANES))],
        out_specs=pl.BlockSpec((None, n_ch, LANES), lambda i: (i, 0, 0)),
        out_shape=jax.ShapeDtypeStruct((b, n_ch, LANES), jnp.bfloat16),
        compiler_params=_cparams("parallel"),
        name="compress_prompt",
    )(ch, pet, peb, w1t, w1b, w2bd)


def _cmp_sample_kernel(pt_ref, *refs, n_pages):
    f32, bf16 = jnp.float32, jnp.bfloat16
    pp = PAGES_PER_STEP
    pages = refs[0:pp]
    (perm_ref, newk_ref, newv_ref, wk_ref, wv_ref, ck_ref, cv_ref, w2k_ref, w2v_ref,
     kc_ref, vc_ref, slab_k, slab_v) = refs[pp:]
    s = pl.program_id(1)
    cpp = PAGE_SIZE // CMP_STRIDE
    base = pl.multiple_of(s * (pp * cpp), pp * cpp)
    for i in range(pp):
        rows = _dot_nt(perm_ref[...], pages[i][...].astype(bf16))
        for half, slab in enumerate((slab_k, slab_v)):
            for j in range(CMP_STRIDE):
                slab[j, pl.ds(base + i * cpp, cpp), :] = rows[j * cpp:(j + 1) * cpp, half * LANES:(half + 1) * LANES]

    @pl.when(s == pl.num_programs(1) - 1)
    def _():
        n_ch = n_pages * (PAGE_SIZE // CMP_STRIDE)
        row = lax.broadcasted_iota(jnp.int32, (n_ch, LANES), 0)
        for slab, new_ref, w_ref, c_ref, w2_ref, o_ref in ((slab_k, newk_ref, wk_ref, ck_ref, w2k_ref, kc_ref),
                                                           (slab_v, newv_ref, wv_ref, cv_ref, w2v_ref, vc_ref)):
            ch = jnp.concatenate([slab[j] for j in range(CMP_STRIDE)], axis=1).astype(bf16)
            ab = jnp.dot(ch, w_ref[...], preferred_element_type=f32)
            b_new = jnp.dot(new_ref[...].astype(bf16), w_ref[...], preferred_element_type=f32)[0:1, LANES:]
            nxt = pltpu.roll(ab[:, LANES:], n_ch - 1, axis=0)
            nxt = jnp.where(row == n_ch - 1, b_new, nxt)
            pre = ab[:, :LANES] + nxt + c_ref[...]
            o_ref[...] = jnp.dot(jax.nn.gelu(pre).astype(bf16), w2_ref[...],
                                 preferred_element_type=f32).astype(o_ref.dtype)


def compress_sample_params(w1, w2, pe):
    pet, peb, w1t, w1b, w2bd = compress_params(w1, w2, pe)
    hp = lax.Precision.HIGHEST
    const = (jnp.dot(pet, w1t.astype(jnp.float32), precision=hp)
             + jnp.dot(peb, w1b.astype(jnp.float32), precision=hp))
    return jnp.concatenate([w1t, w1b], axis=1), const, w2bd


def _page_spec(i, pair):
    return pl.BlockSpec((None, 2 * LANES, PAGE_SIZE),
                        lambda b, s, pt: (pt[b, PAGES_PER_STEP * s + i], pair, 0))


def _per_seq(shape):
    return pl.BlockSpec((None,) + shape, lambda b, s, pt: (b, 0, 0))


def compress_sample(pool_t, page_table, new_k, new_v, pk, pv):
    bsz, n_pages = page_table.shape
    pp = PAGES_PER_STEP
    n_ch = n_pages * (PAGE_SIZE // CMP_STRIDE)
    t_new = new_k.shape[1]

    def chunk_rows(x):
        x = jnp.pad(x, ((0, 0), (0, CMP_STRIDE - t_new), (0, 0))).reshape(bsz, 1, CMP_STRIDE * LANES)
        return jnp.pad(x, ((0, 0), (0, 7), (0, 0)))

    full = lambda shape: pl.BlockSpec(shape, lambda b, s, pt: (0,) * len(shape))
    r = jnp.arange(PAGE_SIZE)
    cpp = PAGE_SIZE // CMP_STRIDE
    perm = (r[None, :] == (r[:, None] % cpp) * CMP_STRIDE + r[:, None] // cpp).astype(jnp.bfloat16)
    wk, ck, w2k = pk
    wv, cv, w2v = pv
    grid_spec = pltpu.PrefetchScalarGridSpec(
        num_scalar_prefetch=1,
        grid=(bsz, n_pages // pp),
        in_specs=[_page_spec(i, 0) for i in range(pp)]
                 + [full((PAGE_SIZE, PAGE_SIZE)), _per_seq((8, CMP_STRIDE * LANES)),
                    _per_seq((8, CMP_STRIDE * LANES)),
                    full((CMP_STRIDE * LANES, 2 * LANES)), full((CMP_STRIDE * LANES, 2 * LANES)),
                    full((1, LANES)), full((1, LANES)), full((LANES, LANES)), full((LANES, LANES))],
        out_specs=[_per_seq((n_ch, LANES)), _per_seq((n_ch, LANES))],
        scratch_shapes=[pltpu.VMEM((CMP_STRIDE, n_ch, LANES), jnp.float32)] * 2,
    )
    return pl.pallas_call(
        functools.partial(_cmp_sample_kernel, n_pages=n_pages),
        grid_spec=grid_spec,
        out_shape=[jax.ShapeDtypeStruct((bsz, n_ch, LANES), jnp.bfloat16)] * 2,
        compiler_params=_cparams("parallel", "arbitrary"),
        name="compress_sample",
    )(page_table, *([pool_t] * pp), perm, chunk_rows(new_k), chunk_rows(new_v), wk, wv, ck, cv, w2k, w2v)


def _nsa_sample_kernel(pt_ref, *refs, n_pages, t_new, w_buf):
    f32, bf16 = jnp.float32, jnp.bfloat16
    pp = PAGES_PER_STEP
    q_ref, gate_ref, kc_ref, vc_ref = refs[0:4]
    pages = refs[4:4 + pp]
    (ksn_ref, vsn_ref, win_ref, kwn_ref, vwn_ref, o_ref,
     sel_scr, exp_scr, oc_scr, m_scr, l_scr, acc_scr) = refs[4 + pp:]
    r4, g2 = HEADS_PER_GROUP, KV_GROUPS
    n_rows = g2 * r4 * t_new
    past_len = n_pages * PAGE_SIZE
    n_cmp = kc_ref.shape[0]
    n_bpad = sel_scr.shape[1]
    tile = pp * PAGE_SIZE
    s = pl.program_id(1)
    qall = q_ref[...]
    qpos = past_len + lax.broadcasted_iota(jnp.int32, (n_rows, 1), 0) % t_new

    def grouped(x):
        return x.reshape(g2, 1, t_new, x.shape[-1])

    @pl.when(s == 0)
    def _():
        s_c = _dot_nt(qall, kc_ref[...])
        n_idx = lax.broadcasted_iota(jnp.int32, (n_rows, n_cmp), 1)
        p_c = _softmax_rows(s_c, (n_idx * CMP_STRIDE + (CMP_LEN - 1)) <= qpos)
        oc_scr[...] = jnp.dot(p_c.astype(bf16), vc_ref[...], preferred_element_type=f32)
        psum = jnp.sum(p_c.reshape(g2, r4, t_new, n_cmp), axis=1).reshape(g2 * t_new, n_cmp)
        psum = jnp.concatenate([psum, jnp.zeros((LANES - g2 * t_new, n_cmp), f32)], axis=0)
        p_hi = psum.astype(bf16)
        rem = psum - p_hi.astype(f32)
        p_mid = rem.astype(bf16)
        p_lo = (rem - p_mid.astype(f32)).astype(bf16)
        ratio = SEL_BLOCK // CMP_STRIDE
        gsum = (lax.broadcasted_iota(jnp.int32, (n_bpad, n_cmp), 1) // ratio
                == lax.broadcasted_iota(jnp.int32, (n_bpad, n_cmp), 0)).astype(bf16)
        imp_t = _dot_nt(gsum, p_hi) + _dot_nt(gsum, p_mid) + _dot_nt(gsum, p_lo)
        blk = lax.broadcasted_iota(jnp.int32, (n_bpad, LANES), 0)
        jq = (past_len + lax.broadcasted_iota(jnp.int32, (n_bpad, LANES), 1) % t_new) // SEL_BLOCK
        forced = (blk == 0) | (blk == jq) | (blk == jq - 1)
        score = jnp.where(blk <= jq, imp_t + jnp.where(forced, FORCE, 0.0), NEG)
        blk_f = blk.astype(f32)
        sel_t = jnp.zeros((n_bpad, LANES), f32)
        for _ in range(N_SEL):
            m = jnp.max(score, axis=0, keepdims=True)
            idx = jnp.min(jnp.where(score == m, blk_f, float(n_bpad)), axis=0, keepdims=True)
            hit = blk_f == idx
            sel_t = jnp.where(hit & (m > 0.5 * NEG), 1.0, sel_t)
            score = jnp.where(hit, REMOVED, score)
        sel = jnp.concatenate([sel_t[k * LANES:(k + 1) * LANES].T for k in range(n_bpad // LANES)], axis=1)
        sel_scr[...] = sel[0:g2 * t_new]
        exp_scr[...] = (lax.broadcasted_iota(jnp.int32, (LANES, tile), 0)
                        == lax.broadcasted_iota(jnp.int32, (LANES, tile), 1) // SEL_BLOCK).astype(bf16)
        m_scr[...] = jnp.full(m_scr.shape, NEG, f32)
        l_scr[...] = jnp.zeros(l_scr.shape, f32)
        acc_scr[...] = jnp.zeros(acc_scr.shape, f32)

    def online_update(s_t, mk, v, v_feature_major):
        n = s_t.shape[-1]
        s4 = jnp.where(mk, s_t.reshape(g2, r4, t_new, n), NEG)
        m_run = m_scr[...].reshape(g2, r4, t_new, 1)
        m_new = jnp.maximum(m_run, jnp.max(s4, axis=-1, keepdims=True))
        alpha = jnp.exp2(m_run - m_new)
        p = jnp.exp2(s4 - m_new)
        l_new = alpha * l_scr[...].reshape(g2, r4, t_new, 1) + jnp.sum(p, axis=-1, keepdims=True)
        pb = p.reshape(n_rows, n).astype(bf16)
        pv = _dot_nt(pb, v) if v_feature_major else jnp.dot(pb, v, preferred_element_type=f32)
        m_scr[...] = m_new.reshape(n_rows, 1)
        l_scr[...] = l_new.reshape(n_rows, 1)
        acc_scr[...] = alpha.reshape(n_rows, 1) * acc_scr[...] + pv

    kt = jnp.concatenate([r[0:LANES, :] for r in pages], axis=1).astype(bf16)
    vt = jnp.concatenate([r[LANES:2 * LANES, :] for r in pages], axis=1).astype(bf16)
    shifted = pltpu.roll(sel_scr[...], (n_bpad - s * (tile // SEL_BLOCK)) % n_bpad, axis=1)
    picked = jnp.dot(shifted[:, 0:LANES].astype(bf16), exp_scr[...], preferred_element_type=f32)
    online_update(jnp.dot(qall, kt, preferred_element_type=f32), grouped(picked) > 0.5, vt, True)

    @pl.when(s == pl.num_programs(1) - 1)
    def _():
        new_blk = past_len // SEL_BLOCK
        kidx = lax.broadcasted_iota(jnp.int32, (n_rows, NEW_PAD), 1)
        causal = ((past_len + kidx) <= qpos) & (kidx < t_new)
        picked_new = sel_scr[:, new_blk:new_blk + 1]
        mk = (grouped(picked_new) > 0.5) & causal.reshape(g2, r4, t_new, NEW_PAD)
        online_update(_dot_nt(qall, ksn_ref[...]), mk, vsn_ref[...], False)
        o_s = acc_scr[...] * jnp.where(m_scr[...] > 0.5 * NEG, 1.0 / l_scr[...], 0.0)

        n_win = w_buf + NEW_PAD
        kw_t = win_ref[0:LANES, :].astype(bf16)
        vw_t = win_ref[LANES:2 * LANES, :].astype(bf16)
        widx = lax.broadcasted_iota(jnp.int32, (n_rows, n_win), 1)
        wpos = past_len - w_buf + widx
        wmask = (wpos <= qpos) & (wpos > qpos - WINDOW) & (wpos >= 0) & (widx < w_buf + t_new)
        s_w = jnp.concatenate([jnp.dot(qall, kw_t, preferred_element_type=f32), _dot_nt(qall, kwn_ref[...])],
                              axis=1)
        p_w = _softmax_rows(s_w, wmask).astype(bf16)
        o_w = (_dot_nt(p_w[:, 0:w_buf], vw_t)
               + jnp.dot(p_w[:, w_buf:], vwn_ref[...], preferred_element_type=f32))
        gate = gate_ref[...]
        o_ref[...] = gate[:, 0:1] * oc_scr[...] + gate[:, 1:2] * o_s + gate[:, 2:3] * o_w


def nsa_sample(q, gates, kc, vc, pool_t, page_table, ks_new, vs_new, win, kw_new, vw_new):
    f32, bf16 = jnp.float32, jnp.bfloat16
    bsz, t_new = q.shape[0], q.shape[1]
    n_pages = page_table.shape[1]
    pp = PAGES_PER_STEP
    w_buf = win.shape[2]
    r4, g2 = HEADS_PER_GROUP, KV_GROUPS
    n_rows = g2 * r4 * t_new
    past_len = n_pages * PAGE_SIZE
    assert past_len % SEL_BLOCK == 0 and t_new <= SEL_BLOCK and past_len >= w_buf and n_pages % pp == 0
    n_sel = past_len // SEL_BLOCK + 1
    n_bpad = -(-n_sel // LANES) * LANES
    eye = jnp.eye(g2, dtype=f32)
    qg = q.reshape(bsz, t_new, g2, r4, HEAD_DIM).transpose(0, 2, 3, 1, 4) * QK_SCALE
    qall = jnp.einsum('bgrqd,gk->bgrqkd', qg, eye).reshape(bsz, n_rows, LANES).astype(bf16)
    gall = gates.reshape(bsz, t_new, g2, r4, 3).transpose(0, 2, 3, 1, 4).reshape(bsz, n_rows, 3)
    pad_rows = lambda x: jnp.pad(x, ((0, 0), (0, NEW_PAD - t_new), (0, 0))).astype(bf16)
    n_cmp = kc.shape[1]
    grid_spec = pltpu.PrefetchScalarGridSpec(
        num_scalar_prefetch=1,
        grid=(bsz, n_pages // pp),
        in_specs=[_per_seq((n_rows, LANES)), _per_seq((n_rows, 3)), _per_seq((n_cmp, LANES)),
                  _per_seq((n_cmp, LANES))]
                 + [_page_spec(i, 1) for i in range(pp)]
                 + [_per_seq((NEW_PAD, LANES)), _per_seq((NEW_PAD, LANES)), _per_seq((2 * LANES, w_buf)),
                    _per_seq((NEW_PAD, LANES)), _per_seq((NEW_PAD, LANES))],
        out_specs=_per_seq((n_rows, LANES)),
        scratch_shapes=[pltpu.VMEM((g2 * t_new, n_bpad), f32), pltpu.VMEM((LANES, pp * PAGE_SIZE), bf16),
                        pltpu.VMEM((n_rows, LANES), f32),
                        pltpu.VMEM((n_rows, 1), f32), pltpu.VMEM((n_rows, 1), f32),
                        pltpu.VMEM((n_rows, LANES), f32)],
    )
    o = pl.pallas_call(
        functools.partial(_nsa_sample_kernel, n_pages=n_pages, t_new=t_new, w_buf=w_buf),
        grid_spec=grid_spec,
        out_shape=jax.ShapeDtypeStruct((bsz, n_rows, LANES), f32),
        compiler_params=_cparams("parallel", "arbitrary"),
        name="nsa_sample",
    )(page_table, qall, gall, kc, vc, *([pool_t] * pp), pad_rows(ks_new), pad_rows(vs_new), win,
      pad_rows(kw_new), pad_rows(vw_new))
    o = jnp.einsum('bgrqkd,gk->bqgrd', o.reshape(bsz, g2, r4, t_new, g2, HEAD_DIM), eye)
    return o.reshape(bsz, t_new, NSA_Q)


def _ssd_kernel(x_ref, b_ref, c_ref, dt_ref, a_ref, za_ref, zb_ref, dskip_ref, ng_ref, y_ref, hout_ref, h_scr, *,
                chunk):
    f32, bf16 = jnp.float32, jnp.bfloat16
    n_l = chunk
    hpg = SSD_HEADS // SSD_GROUPS
    gw = hpg * SSD_HEAD_DIM
    j = pl.program_id(1)

    @pl.when(j == 0)
    def _():
        h_scr[...] = jnp.zeros(h_scr.shape, f32)

    x = x_ref[...]
    dt = dt_ref[...]
    tri_b = (lax.broadcasted_iota(jnp.int32, (n_l, n_l), 0) >= lax.broadcasted_iota(jnp.int32, (n_l, n_l), 1))
    tri = tri_b.astype(bf16)
    cum = sum(jnp.dot(tri, part, preferred_element_type=f32) for part in _split3(dt * a_ref[...]))
    cum_t = cum.T
    dt_t = dt.T
    ecum = jnp.exp(cum)
    clast = cum[n_l - 1:n_l, :]
    wt = jnp.exp(clast - cum) * dt
    elast = jnp.exp(clast)
    lane = lax.broadcasted_iota(jnp.int32, (n_l, LANES), 1)
    low = lane < SSD_HEAD_DIM

    def pair(v, h0):
        return jnp.where(low[:v.shape[0]], v[:, h0:h0 + 1], v[:, h0 + 1:h0 + 2])

    tiles = []
    for g in range(SSD_GROUPS):
        bg = b_ref[:, g * SSD_STATE:(g + 1) * SSD_STATE]
        cgb = c_ref[:, g * SSD_STATE:(g + 1) * SSD_STATE].astype(bf16)
        bgt = bg.T.astype(bf16)
        cb = jnp.dot(cgb, bgt, preferred_element_type=f32)
        hg = h_scr[g]
        y_inter = jnp.dot(cgb, hg.astype(bf16), preferred_element_type=f32)
        xw, dec = [], []
        for pr in range(hpg // 2):
            h0 = hpg * g + 2 * pr
            xt = x[:, (h0 // 2) * LANES:(h0 // 2 + 1) * LANES]
            acc = None
            for k in range(2):
                h = h0 + k
                seg = cum[:, h:h + 1] - cum_t[h:h + 1, :]
                w = cb * jnp.exp(jnp.where(tri_b, seg, NEG)) * dt_t[h:h + 1, :]
                xm = jnp.where(low if k == 0 else jnp.logical_not(low), xt, 0.0).astype(bf16)
                part = jnp.dot(w.astype(bf16), xm, preferred_element_type=f32)
                acc = part if acc is None else acc + part
            tiles.append(acc + y_inter[:, pr * LANES:(pr + 1) * LANES] * pair(ecum, h0))
            xw.append((xt * pair(wt, h0)).astype(bf16))
            dec.append(pair(elast, h0))
        h_scr[g] = (hg * jnp.concatenate(dec, axis=1)
                    + jnp.dot(bgt, jnp.concatenate(xw, axis=1), preferred_element_type=f32))
    y = jnp.concatenate(tiles, axis=1) + dskip_ref[...] * x
    zg = jnp.concatenate([za_ref[...], zb_ref[...]], axis=1)
    v = y * (zg * jax.nn.sigmoid(zg))
    outs = []
    for g in range(SSD_GROUPS):
        vg = v[:, g * gw:(g + 1) * gw]
        outs.append(vg * lax.rsqrt(jnp.mean(vg * vg, axis=-1, keepdims=True) + RMS_EPS))
    y_ref[...] = jnp.concatenate(outs, axis=1) * ng_ref[...]

    @pl.when(j == pl.num_programs(1) - 1)
    def _():
        hout_ref[...] = h_scr[...]


def ssd_prompt(xbc, dt, a, z, zg_offset, d_skip, norm_g):
    f32 = jnp.float32
    bsz, t, _ = xbc.shape
    hpg = SSD_HEADS // SSD_GROUPS
    gn = SSD_GROUPS * SSD_STATE
    half = SSD_INNER // 2
    assert zg_offset % half == 0
    dt_p = jnp.pad(dt, ((0, 0), (0, 0), (0, LANES - SSD_HEADS)))
    a_p = jnp.pad(a.astype(f32), (0, LANES - SSD_HEADS)).reshape(1, LANES)
    dsk = jnp.repeat(d_skip.astype(f32), SSD_HEAD_DIM).reshape(1, SSD_INNER)
    blk = lambda w, c: pl.BlockSpec((None, SSD_CHUNK, w), lambda b, j: (b, j, c))
    full = lambda shape: pl.BlockSpec(shape, lambda b, j: (0,) * len(shape))
    state_spec = pl.BlockSpec((None, SSD_GROUPS, SSD_STATE, hpg * SSD_HEAD_DIM), lambda b, j: (b, 0, 0, 0))
    y, h = pl.pallas_call(
        functools.partial(_ssd_kernel, chunk=SSD_CHUNK),
        grid=(bsz, t // SSD_CHUNK),
        in_specs=[blk(SSD_INNER, 0), blk(gn, SSD_INNER // gn), blk(gn, SSD_INNER // gn + 1), blk(LANES, 0),
                  full((1, LANES)), blk(half, zg_offset // half), blk(half, zg_offset // half + 1),
                  full((1, SSD_INNER)), full((1, SSD_INNER))],
        out_specs=[blk(SSD_INNER, 0), state_spec],
        out_shape=[jax.ShapeDtypeStruct((bsz, t, SSD_INNER), f32),
                   jax.ShapeDtypeStruct((bsz, SSD_GROUPS, SSD_STATE, hpg * SSD_HEAD_DIM), f32)],
        scratch_shapes=[pltpu.VMEM((SSD_GROUPS, SSD_STATE, hpg * SSD_HEAD_DIM), f32)],
        compiler_params=_cparams("parallel", "arbitrary"),
        name="ssd_prompt",
    )(xbc, xbc, xbc, dt_p, a_p, z, z, dsk, norm_g.astype(f32).reshape(1, SSD_INNER))
    h = h.reshape(bsz, SSD_GROUPS, SSD_STATE, hpg, SSD_HEAD_DIM).transpose(0, 1, 3, 4, 2)
    return y, h.reshape(bsz, SSD_HEADS, SSD_HEAD_DIM, SSD_STATE)


def _inproj_even_kernel(x_ref, w_ref, c_ref, sa_ref, sb_ref,
                        u_ref, q_ref, rows_ref, kvw_ref, kvsb_ref, kvwb_ref, g_ref):
    bf16 = jnp.bfloat16
    z = jnp.dot(x_ref[...].astype(bf16), w_ref[...], preferred_element_type=jnp.float32)
    cos, s_up, s_down = c_ref[...], sa_ref[...], sb_ref[...]

    def rot(t):
        return (t * cos + pltpu.roll(t, LANES - ROT_DIM // 2, axis=1) * s_up
                + pltpu.roll(t, ROT_DIM // 2, axis=1) * s_down)

    tile = lambda k: z[:, k * LANES:(k + 1) * LANES]
    q0 = S5_DIM // LANES
    kv0 = q0 + NSA_Q // LANES
    u_ref[...] = z[:, 0:S5_DIM]
    q_ref[...] = jnp.concatenate([rot(tile(q0 + k)) for k in range(NSA_Q // LANES)], axis=1)
    kc, vc, ks, vs, kw, vw = (rot(tile(kv0)), tile(kv0 + 1), rot(tile(kv0 + 2)), tile(kv0 + 3),
                              rot(tile(kv0 + 4)), tile(kv0 + 5))
    rows_ref[...] = jnp.concatenate([kc, vc, ks, vs], axis=1)
    kvw = jnp.concatenate([kw, vw], axis=1)
    kvw_ref[...] = kvw
    kvwb_ref[...] = kvw.astype(bf16)
    kvsb_ref[...] = jnp.concatenate([ks, vs], axis=1).astype(bf16)
    g0 = (kv0 + 6) * LANES
    g_ref[...] = jax.nn.sigmoid(z[:, g0:g0 + 3 * N_HEADS])


def rope_tables(pos):
    half = ROT_DIM // 2
    inv = ROPE_THETA ** (-jnp.arange(half, dtype=jnp.float32) * 2.0 / ROT_DIM)
    ang = pos.astype(jnp.float32)[:, None] * inv[None, :]
    d = jnp.arange(LANES) % HEAD_DIM
    cos = jnp.take(jnp.cos(ang), d % half, axis=1)
    sin = jnp.take(jnp.sin(ang), d % half, axis=1)
    return (jnp.where(d < ROT_DIM, cos, 1.0), jnp.where(d < half, -sin, 0.0),
            jnp.where((d >= half) & (d < ROT_DIM), sin, 0.0))


def inproj_even(h, w_bf16, pos):
    f32, bf16 = jnp.float32, jnp.bfloat16
    n, d = h.shape
    tile = min(ROW_TILE, n)
    row = lambda w: pl.BlockSpec((tile, w), lambda i: (i, 0))
    fixed = lambda shape: pl.BlockSpec(shape, lambda i: (0, 0), pipeline_mode=pl.Buffered(1))
    widths = [(S5_DIM, f32), (NSA_Q, f32), (2 * NSA_KV, f32), (NSA_KV, f32), (NSA_KV, bf16), (NSA_KV, bf16),
              (3 * N_HEADS, f32)]
    return pl.pallas_call(
        _inproj_even_kernel,
        grid=(pl.cdiv(n, tile),),
        in_specs=[row(d), fixed(w_bf16.shape), row(LANES), row(LANES), row(LANES)],
        out_specs=[row(w) for w, _ in widths],
        out_shape=[jax.ShapeDtypeStruct((n, w), dt) for w, dt in widths],
        compiler_params=_cparams("parallel"),
        name="inproj_even",
    )(h, w_bf16, *rope_tables(pos))


def _causal_conv_tile(x, tail, w_ref, b_ref, width):
    row = lax.broadcasted_iota(jnp.int32, (TAIL, x.shape[1]), 0)
    acc = b_ref[...] + w_ref[width - 1:width, :] * x
    for k in range(1, width):
        xs = pltpu.roll(x, k, axis=0)
        head = jnp.where(row < k, pltpu.roll(tail, k, axis=0), xs[0:TAIL])
        xs = jnp.concatenate([head, xs[TAIL:]], axis=0)
        acc = acc + w_ref[width - 1 - k:width - k, :] * xs
    return acc


def _inproj_odd_kernel(x_ref, w_ref, scw_ref, scb_ref, cvw_ref, cvb_ref, dtb_ref,
                       ysc_ref, xbc_ref, dt_ref, zg_ref, tsc_ref, tx_ref, tail_sc, tail_x, *, tiles_per_seq):
    @pl.when(pl.program_id(0) % tiles_per_seq == 0)
    def _():
        tail_sc[...] = jnp.zeros(tail_sc.shape, jnp.float32)
        tail_x[...] = jnp.zeros(tail_x.shape, jnp.float32)

    z = jnp.dot(x_ref[...].astype(jnp.bfloat16), w_ref[...], preferred_element_type=jnp.float32)
    o_zg = 3 * SC_DIM
    o_x = o_zg + SSD_INNER
    o_dt = o_x + SSD_CONV_DIM
    n = z.shape[0]
    prod = z[:, 2 * SC_DIM:3 * SC_DIM] * z[:, 0:SC_DIM]
    ysc_ref[...] = z[:, SC_DIM:2 * SC_DIM] * _causal_conv_tile(prod, tail_sc[...], scw_ref, scb_ref, SC_WIDTH)
    xbc = z[:, o_x:o_dt]
    c = _causal_conv_tile(xbc, tail_x[...], cvw_ref, cvb_ref, SSD_CONV)
    xbc_ref[...] = c * jax.nn.sigmoid(c)
    dt_ref[...] = jax.nn.softplus(z[:, o_dt:o_dt + SSD_HEADS] + dtb_ref[...])
    zg_ref[...] = z[:, o_zg:o_x]
    tail_sc[...] = prod[n - TAIL:n]
    tail_x[...] = xbc[n - TAIL:n]
    tsc_ref[...] = prod[n - TAIL:n]
    tx_ref[...] = xbc[n - TAIL:n]


def inproj_odd_prompt(h, w_bf16, bsz, sc_w, sc_b, cv_w, cv_b, dt_bias):
    f32 = jnp.float32
    n_rows, d = h.shape
    t = n_rows // bsz
    assert t % ROW_TILE == 0
    tps = t // ROW_TILE
    row = lambda w: pl.BlockSpec((ROW_TILE, w), lambda i: (i, 0))
    fixed = lambda shape: pl.BlockSpec(shape, lambda i: (0,) * len(shape), pipeline_mode=pl.Buffered(1))
    last = lambda w: pl.BlockSpec((None, TAIL, w), lambda i: (i // tps, 0, 0))
    ysc, xbc, dt, zg, tsc, tx = pl.pallas_call(
        functools.partial(_inproj_odd_kernel, tiles_per_seq=tps),
        grid=(n_rows // ROW_TILE,),
        in_specs=[row(d), fixed(w_bf16.shape), fixed((SC_WIDTH, SC_DIM)), fixed((1, SC_DIM)),
                  fixed((SSD_CONV, SSD_CONV_DIM)), fixed((1, SSD_CONV_DIM)), fixed((1, SSD_HEADS))],
        out_specs=[row(SC_DIM), row(SSD_CONV_DIM), row(SSD_HEADS), row(SSD_INNER), last(SC_DIM), last(SSD_CONV_DIM)],
        out_shape=[jax.ShapeDtypeStruct((n_rows, SC_DIM), f32), jax.ShapeDtypeStruct((n_rows, SSD_CONV_DIM), f32),
                   jax.ShapeDtypeStruct((n_rows, SSD_HEADS), f32), jax.ShapeDtypeStruct((n_rows, SSD_INNER), f32),
                   jax.ShapeDtypeStruct((bsz, TAIL, SC_DIM), f32), jax.ShapeDtypeStruct((bsz, TAIL, SSD_CONV_DIM), f32)],
        scratch_shapes=[pltpu.VMEM((TAIL, SC_DIM), f32), pltpu.VMEM((TAIL, SSD_CONV_DIM), f32)],
        compiler_params=_cparams("arbitrary"),
        name="inproj_odd",
    )(h, w_bf16, sc_w.astype(f32), sc_b.astype(f32).reshape(1, SC_DIM), cv_w.astype(f32),
      cv_b.astype(f32).reshape(1, SSD_CONV_DIM), dt_bias.astype(f32).reshape(1, SSD_HEADS))
    seq = lambda a: a.reshape(bsz, t, a.shape[-1])
    return (seq(ysc), seq(xbc), seq(dt), seq(zg),
            tsc[:, TAIL - (SC_WIDTH - 1):], tx[:, TAIL - (SSD_CONV - 1):])


def last_rows(x, n):
    t = x.shape[1]
    if t < n:
        x = jnp.pad(x, [(0, 0), (n - t, 0)] + [(0, 0)] * (x.ndim - 2))
    return x[:, x.shape[1] - n:]


def causal_conv(x, buf, w, b):
    t = x.shape[1]
    width = w.shape[0]
    xp = jnp.concatenate([buf, x], axis=1)
    y = b + sum(xp[:, j:j + t] * w[j] for j in range(width))
    return y, xp[:, xp.shape[1] - (width - 1):]


def even_prompt_mix(h, w_in_bf16, bt, s5p, cmpp, w_buf):
    t = h.shape[0] // bt
    u, q, rows, kvw, kvs_b, kvw_b, gates = inproj_even(h, w_in_bf16, jnp.arange(h.shape[0]) % t)
    seq = lambda a: a.reshape(bt, t, a.shape[-1])
    feat = KV_GROUPS * HEAD_DIM
    y_s5, s5_state = s5_scan(seq(u), jnp.zeros((bt, S5_GROUPS, S5_STATE, 2), jnp.float32), s5p, S5_CHUNK)
    rows = seq(rows)
    kc = compress_prompt(rows[..., 0:feat], compress_params(cmpp[0], cmpp[1], cmpp[2]))
    vc = compress_prompt(rows[..., feat:2 * feat], compress_params(cmpp[3], cmpp[4], cmpp[5]))
    y_nsa = nsa_prompt(seq(q), seq(gates), kc, vc, seq(kvs_b), jnp.pad(seq(kvw_b), ((0, 0), (WINDOW, 0), (0, 0))))
    new_rows = rows.reshape(bt, t, 4, KV_GROUPS, HEAD_DIM)
    return (y_s5, y_nsa), s5_state, new_rows, last_rows(seq(kvw).reshape(bt, t, 2, KV_GROUPS, HEAD_DIM), w_buf)


def even_sample_mix(h, w_in_bf16, bt, s5_h0, pool, page_table, win_buf, s5p, cmpp):
    f32 = jnp.float32
    t = h.shape[0] // bt
    pos = page_table.shape[1] * PAGE_SIZE + jnp.arange(h.shape[0]) % t
    u, q, rows, kvw, _, _, gates = inproj_even(h, w_in_bf16, pos)
    seq = lambda a: a.reshape(bt, t, a.shape[-1])
    feat = KV_GROUPS * HEAD_DIM
    y_s5, s5_state = s5_scan(seq(u), s5_h0.astype(f32), s5p, t)
    rows, kvw = seq(rows), seq(kvw)
    pool_t = pool.astype(f32).transpose(0, 2, 3, 4, 1).reshape(pool.shape[0], 4 * feat, PAGE_SIZE)
    kc, vc = compress_sample(pool_t, page_table, rows[..., 0:feat], rows[..., feat:2 * feat],
                             compress_sample_params(cmpp[0], cmpp[1], cmpp[2]),
                             compress_sample_params(cmpp[3], cmpp[4], cmpp[5]))
    w_buf = win_buf.shape[1]
    win_f = win_buf.astype(f32)
    y_nsa = nsa_sample(q.reshape(bt, t, N_HEADS, HEAD_DIM), gates.reshape(bt, t, N_HEADS, 3), kc, vc, pool_t,
                       page_table, rows[..., 2 * feat:3 * feat], rows[..., 3 * feat:4 * feat],
                       win_f.transpose(0, 2, 3, 4, 1).reshape(bt, 2 * feat, w_buf), kvw[..., 0:feat],
                       kvw[..., feat:2 * feat])
    new_rows = rows.reshape(bt, t, 4, KV_GROUPS, HEAD_DIM)
    win = jnp.concatenate([win_f, kvw.reshape(bt, t, 2, KV_GROUPS, HEAD_DIM)], axis=1)
    return (y_s5, y_nsa), s5_state, new_rows, win[:, t:]


def ssd_scan(x, dt, a, bm, cm, h0, chunk):
    bt, t, nh, p = x.shape
    nch = t // chunk
    r = nh // SSD_GROUPS
    tri = jnp.arange(chunk)[:, None] >= jnp.arange(chunk)[None, :]

    def to_chunks(v):
        return jnp.moveaxis(v.reshape((bt, nch, chunk) + v.shape[2:]), 1, 0)

    def step(h, inp):
        xc, dtc, bc, cc = inp
        cum = jnp.cumsum(dtc * a, axis=1)
        seg = cum[:, :, None, :] - cum[:, None, :, :]
        decay = jnp.exp(jnp.where(tri[None, :, :, None], seg, NEG)).reshape(bt, chunk, chunk, SSD_GROUPS, r)
        cb = jnp.einsum('btgn,bsgn->btsg', cc, bc)
        xg = xc.reshape(bt, chunk, SSD_GROUPS, r, p)
        dg = dtc.reshape(bt, chunk, SSD_GROUPS, r)
        w = cb[..., None] * decay * dg[:, None]
        y_intra = jnp.einsum('btsgr,bsgrp->btgrp', w, xg)
        hg = h.reshape(bt, SSD_GROUPS, r, p, SSD_STATE)
        y_inter = jnp.einsum('btgn,bgrpn->btgrp', cc, hg) * jnp.exp(cum).reshape(bt, chunk, SSD_GROUPS, r)[..., None]
        wt = (jnp.exp(cum[:, -1:, :] - cum) * dtc).reshape(bt, chunk, SSD_GROUPS, r)
        h_new = (hg * jnp.exp(cum[:, -1]).reshape(bt, SSD_GROUPS, r)[..., None, None]
                 + jnp.einsum('bsgr,bsgrp,bsgn->bgrpn', wt, xg, bc))
        return h_new.reshape(bt, nh, p, SSD_STATE), (y_intra + y_inter).reshape(bt, chunk, nh, p)

    h_fin, ys = lax.scan(step, h0, (to_chunks(x), to_chunks(dt), to_chunks(bm), to_chunks(cm)))
    return jnp.moveaxis(ys, 0, 1).reshape(bt, t, nh, p), h_fin


def gated_rmsnorm(y, z, g):
    v = y * jax.nn.silu(z)
    bt, t, _ = v.shape
    vg = v.reshape(bt, t, SSD_GROUPS, SSD_INNER // SSD_GROUPS)
    vg = vg * lax.rsqrt(jnp.mean(vg * vg, -1, keepdims=True) + RMS_EPS)
    return vg.reshape(bt, t, SSD_INNER) * g


def odd_prompt_mix(h, w_in_bf16, bsz, sc_w, sc_b, cv_w, cv_b, dt_bias, a_log, d_skip, norm_g):
    a = -jnp.exp(a_log.astype(jnp.float32))
    y_sc, xbc_c, dt, zg, new_sc, new_conv = inproj_odd_prompt(h, w_in_bf16, bsz, sc_w, sc_b, cv_w, cv_b, dt_bias)
    y, h_new = ssd_prompt(xbc_c, dt, a, zg, 0, d_skip, norm_g)
    return (y_sc, y), new_sc, new_conv, h_new


def odd_mix(z, sc_buf, conv_buf, h0, chunk, sc_w, sc_b, cv_w, cv_b, dt_bias, a_log, d_skip, norm_g):
    f32 = jnp.float32
    bt, t, _ = z.shape
    a = -jnp.exp(a_log.astype(f32))
    o1 = SC_DIM
    o2 = 2 * SC_DIM
    o3 = 3 * SC_DIM
    o4 = o3 + SSD_INNER
    o5 = o4 + SSD_CONV_DIM
    sc_h = z[..., :o1]
    sc_bg = z[..., o1:o2]
    sc_cg = z[..., o2:o3]
    zg = z[..., o3:o4]
    xbc = z[..., o4:o5]
    dt_raw = z[..., o5:]
    conv_sc, new_sc = causal_conv(sc_cg * sc_h, sc_buf.astype(f32), sc_w, sc_b)
    y_sc = sc_bg * conv_sc
    xbc_c, new_conv = causal_conv(xbc, conv_buf.astype(f32), cv_w, cv_b)
    xbc_c = jax.nn.silu(xbc_c)
    gn = SSD_GROUPS * SSD_STATE
    xs = xbc_c[..., :SSD_INNER].reshape(bt, t, SSD_HEADS, SSD_HEAD_DIM)
    bm = xbc_c[..., SSD_INNER:SSD_INNER + gn].reshape(bt, t, SSD_GROUPS, SSD_STATE)
    cm = xbc_c[..., SSD_INNER + gn:].reshape(bt, t, SSD_GROUPS, SSD_STATE)
    dt = jax.nn.softplus((dt_raw + dt_bias).astype(f32))
    y, h_new = ssd_scan(xs, dt, a, bm, cm, h0.astype(f32), chunk)
    y = (y + d_skip[:, None] * xs).reshape(bt, t, SSD_INNER)
    y = gated_rmsnorm(y, zg, norm_g)
    return (y_sc, y), new_sc, new_conv, h_new


def moe_ffn(x, logits, w_gu_bf16, w_down_bf16):
    n, d = x.shape
    top_v, top_i = lax.top_k(logits, TOP_K)
    gate = jax.nn.softmax(top_v, axis=-1)
    flat_e = top_i.reshape(-1)
    blk = 128
    assert (TOP_K * n) % blk == 0
    onehot = jax.nn.one_hot(flat_e, N_EXPERTS, dtype=jnp.float32).reshape(-1, blk, N_EXPERTS)
    tri = (jnp.arange(blk)[:, None] >= jnp.arange(blk)[None, :]).astype(jnp.float32)
    local = jnp.einsum('ij,bjk->bik', tri, onehot)
    block_total = local[:, -1, :]
    block_off = jnp.cumsum(block_total, axis=0) - block_total
    incl = (local + block_off[:, None, :]).reshape(-1, N_EXPERTS)
    rank = jnp.take_along_axis(incl, flat_e[:, None], axis=1)[:, 0].astype(jnp.int32) - 1
    counts = jnp.sum(block_total, axis=0).astype(jnp.int32)
    padded = ((counts + ROW_TILE - 1) // ROW_TILE) * ROW_TILE
    pad_start = jnp.cumsum(padded) - padded
    dest = (pad_start[flat_e] + rank).astype(jnp.int32)
    n_tiles = (TOP_K * n) // ROW_TILE + N_EXPERTS
    rows = n_tiles * ROW_TILE
    row_token = jnp.zeros((rows,), jnp.int32).at[dest].set(jnp.arange(TOP_K * n, dtype=jnp.int32) // TOP_K,
                                                           unique_indices=True, mode='promise_in_bounds')
    tile_end = jnp.cumsum(padded) // ROW_TILE
    tile_expert = jnp.minimum(jnp.searchsorted(tile_end, jnp.arange(n_tiles), side='right'),
                              N_EXPERTS - 1).astype(jnp.int32)
    n_used = tile_end[-1:].astype(jnp.int32)
    xs = x.at[row_token].get(mode='promise_in_bounds')
    ys = grouped_ffn(xs, w_gu_bf16, w_down_bf16, tile_expert, n_used)
    dest = dest.reshape(n, TOP_K)
    y0 = ys.at[dest[:, 0]].get(mode='promise_in_bounds')
    y1 = ys.at[dest[:, 1]].get(mode='promise_in_bounds')
    return y0, y1, gate


def kernel(x_prompt, x_sample, state_s5, cache_nsa_kv, state_win_kv, state_sc_conv, state_ssd_conv, state_ssd,
           page_table, ln_g, ln_b, w_in_even, s5_lam_re, s5_lam_im, s5_log_dt, s5_b, s5_c, s5_d, s5_w_glu,
           nsa_wk1, nsa_wk2, nsa_pe_k, nsa_wv1, nsa_wv2, nsa_pe_v, w_out_even, ffn_w_gu, ffn_w_down,
           w_in_odd, sc_conv_w, sc_conv_b, ssd_conv_w, ssd_conv_b, ssd_dt_bias, ssd_a_log, ssd_d, ssd_norm_g,
           w_out_odd, moe_router, moe_router_b, moe_w_gu, moe_w_down):
    f32 = jnp.float32
    bf16 = jnp.bfloat16
    bp, tp, d = x_prompt.shape
    bs, ts, _ = x_sample.shape
    n_p = bp * tp
    n_s = bs * ts
    w_buf = state_win_kv.shape[2]
    streams = [x_prompt.astype(f32).reshape(n_p, d), x_sample.astype(f32).reshape(n_s, d)]

    def flat(parts, n_rows):
        return [p.reshape(n_rows, p.shape[-1]) for p in parts]

    def out_proj(parts, w_out, width, h, g, b, router=None):
        w = w_out.astype(bf16)
        return matmul(flat(parts, h.shape[0]), [w[:width], w[width:]], ln=(h, g, b), router=router)

    def single_expert(n_rows):
        n_tiles = pl.cdiv(n_rows, min(ROW_TILE, n_rows))
        return jnp.zeros((n_tiles,), jnp.int32), jnp.full((1,), n_tiles, jnp.int32)

    s5p = s5_params(s5_lam_re[0], s5_lam_im[0], s5_log_dt[0], s5_b[0], s5_c[0], s5_d[0], s5_w_glu[0])
    cmpp = (nsa_wk1[0], nsa_wk2[0], nsa_pe_k[0], nsa_wv1[0], nsa_wv2[0], nsa_pe_v[0])
    w_in = w_in_even[0].astype(bf16)
    mix_p, s5_p, kv_p, win_p = even_prompt_mix(streams[0], w_in, bp, s5p, cmpp, w_buf)
    mix_s, s5_s, kv_s, win_s = even_sample_mix(streams[1], w_in, bs, state_s5[0], cache_nsa_kv[0], page_table,
                                               state_win_kv[0], s5p, cmpp)
    streams = [out_proj(mix, w_out_even[0], S5_DIM, h, ln_g[0, 0], ln_b[0, 0])
               for mix, h in zip((mix_p, mix_s), streams)]
    w_gu, w_down = to_bf16(ffn_w_gu), to_bf16(ffn_w_down)
    streams = [grouped_ffn(h, w_gu, w_down, *single_expert(h.shape[0]), ln=(ln_g[0, 1], ln_b[0, 1]))
               for h in streams]

    oddp = (sc_conv_w[0], sc_conv_b[0], ssd_conv_w[0], ssd_conv_b[0], ssd_dt_bias[0],
            ssd_a_log[0], ssd_d[0], ssd_norm_g[0])
    w_in = w_in_odd[0].astype(bf16)
    mix_p, scc_p, sdc_p, ssd_p = odd_prompt_mix(streams[0], w_in, bp, *oddp)
    zs = matmul([streams[1]], [w_in]).reshape(bs, ts, -1)
    mix_s, scc_s, sdc_s, ssd_s = odd_mix(zs, state_sc_conv[0], state_ssd_conv[0], state_ssd[0], ts, *oddp)
    outs = [out_proj(mix, w_out_odd[0], SC_DIM, h, ln_g[1, 0], ln_b[1, 0], router=(moe_router[0], moe_router_b[0]))
            for mix, h in zip((mix_p, mix_s), streams)]
    h = jnp.concatenate([o[0] for o in outs], axis=0)
    logits = jnp.concatenate([o[1] for o in outs], axis=0)[:, :N_EXPERTS]
    y0, y1, gate = moe_ffn(h, logits, to_bf16(moe_w_gu[0]), to_bf16(moe_w_down[0]))
    hp = moe_combine_ln(h, y0, y1, gate, ln_g[1, 1], ln_b[1, 1], 0, n_p).reshape(bp, tp, d)
    hs = moe_combine_ln(h, y0, y1, gate, ln_g[1, 1], ln_b[1, 1], n_p, n_s).reshape(bs, ts, d)
    st = lambda a, ref: a[None].astype(ref.dtype)
    return (hp.astype(x_prompt.dtype), hs.astype(x_sample.dtype),
            st(s5_p, state_s5), st(s5_s, state_s5),
            st(kv_p, cache_nsa_kv), st(kv_s, cache_nsa_kv),
            st(win_p, state_win_kv), st(win_s, state_win_kv),
            st(scc_p, state_sc_conv), st(scc_s, state_sc_conv),
            st(sdc_p, state_ssd_conv), st(sdc_s, state_ssd_conv),
            st(ssd_p, state_ssd), st(ssd_s, state_ssd))
```

```python
import functools
import math

import jax
import jax.numpy as jnp
from jax import lax
from jax.experimental import pallas as pl
from jax.experimental.pallas import tpu as pltpu

D_MODEL = 1024
DEPTH = 2
ALPHA = (2.0 * DEPTH) ** 0.25
LN_EPS = 1e-5
RMS_EPS = 1e-5
NEG = -1e30

S5_DIM = D_MODEL // 2
S5_GROUP = 16
S5_GROUPS = S5_DIM // S5_GROUP
S5_STATE = 64

HEAD_DIM = 64
N_HEADS = (D_MODEL // 2) // HEAD_DIM
KV_GROUPS = 2
HEADS_PER_GROUP = N_HEADS // KV_GROUPS
CMP_STRIDE = 16
CMP_LEN = 2 * CMP_STRIDE
SEL_BLOCK = 64
N_SEL = 16
WINDOW = 512
Q_BLOCK = 128
ROPE_THETA = 500000.0
ROT_DIM = HEAD_DIM // 4
FORCE = 1e4
NSA_Q = N_HEADS * HEAD_DIM
NSA_KV = 2 * KV_GROUPS * HEAD_DIM

SC_DIM = D_MODEL // 2
SC_WIDTH = 3
SSD_HEAD_DIM = 64
SSD_HEADS = 16
SSD_INNER = SSD_HEADS * SSD_HEAD_DIM
SSD_GROUPS = 4
SSD_STATE = 128
SSD_CONV = 4
SSD_CONV_DIM = SSD_INNER + 2 * SSD_GROUPS * SSD_STATE
SSD_CHUNK = 128

D_FF = 2816
N_EXPERTS = 8
TOP_K = 2

VMEM_LIMIT_BYTES = 56 * 1024 * 1024
LANES = 128
S5_N = S5_GROUPS * S5_STATE
S5_LT = S5_N // LANES
S5_CHUNK = 256
SEL_TILE = 1024
QK_SCALE = HEAD_DIM ** -0.5 * math.log2(math.e)
REMOVED = -3e38
PAGE_SIZE = 128
PAGES_PER_STEP = 64
NEW_PAD = 128
TAIL = 8
ROW_TILE = 512
FF_TILE = D_FF // 2


def _cparams(*sem):
    return pltpu.CompilerParams(dimension_semantics=sem, vmem_limit_bytes=VMEM_LIMIT_BYTES)


def _deepnorm(resid, update, g, b):
    y = ALPHA * resid + update
    mu = jnp.mean(y, axis=-1, keepdims=True)
    yc = y - mu
    var = jnp.mean(yc * yc, axis=-1, keepdims=True)
    return yc * lax.rsqrt(var + LN_EPS) * g + b


def _mm_kernel(*refs, n_in, fuse_ln, router):
    xs, ws = refs[0:n_in], refs[n_in:2 * n_in]
    o_ref = refs[-2] if router else refs[-1]
    acc = None
    for x_ref, w_ref in zip(xs, ws):
        part = jnp.dot(x_ref[...].astype(jnp.bfloat16), w_ref[...], preferred_element_type=jnp.float32)
        acc = part if acc is None else acc + part
    if fuse_ln:
        r_ref, g_ref, b_ref = refs[2 * n_in:2 * n_in + 3]
        acc = _deepnorm(r_ref[...], acc, g_ref[...], b_ref[...])
    o_ref[...] = acc
    if router:
        wr_ref, br_ref = refs[-4], refs[-3]
        xparts = _split3(acc)
        logits = br_ref[...]
        for i in range(2):
            for j in range(2 - i):
                logits = logits + jnp.dot(xparts[i], wr_ref[j], preferred_element_type=jnp.float32)
        refs[-1][...] = logits


def matmul(xs, ws_bf16, ln=None, router=None):
    m = xs[0].shape[0]
    n = ws_bf16[0].shape[1]
    tile = min(ROW_TILE, m)
    row = lambda width: pl.BlockSpec((tile, width), lambda i: (i, 0))
    fixed = lambda shape: pl.BlockSpec(shape, lambda i: (0, 0), pipeline_mode=pl.Buffered(1))
    in_specs = [row(x.shape[1]) for x in xs] + [fixed(w.shape) for w in ws_bf16]
    args = list(xs) + list(ws_bf16)
    if ln is not None:
        resid, g, b = ln
        in_specs += [row(n), fixed((1, n)), fixed((1, n))]
        args += [resid, g.reshape(1, n), b.reshape(1, n)]
    out_specs, out_shape = row(n), jax.ShapeDtypeStruct((m, n), jnp.float32)
    if router is not None:
        w_r, b_r = router
        pad = LANES - w_r.shape[1]
        w_parts = jnp.stack(_split3(jnp.pad(w_r.astype(jnp.float32), ((0, 0), (0, pad)))))
        in_specs += [pl.BlockSpec((3, n, LANES), lambda i: (0, 0, 0), pipeline_mode=pl.Buffered(1)),
                     fixed((1, LANES))]
        args += [w_parts, jnp.pad(b_r.astype(jnp.float32), (0, pad)).reshape(1, LANES)]
        out_specs, out_shape = [out_specs, row(LANES)], [out_shape, jax.ShapeDtypeStruct((m, LANES), jnp.float32)]
    return pl.pallas_call(
        functools.partial(_mm_kernel, n_in=len(xs), fuse_ln=ln is not None, router=router is not None),
        grid=(pl.cdiv(m, tile),),
        in_specs=in_specs,
        out_specs=out_specs,
        out_shape=out_shape,
        compiler_params=_cparams("parallel"),
        name="matmul",
    )(*args)


def _combine_kernel(h_ref, y0_ref, y1_ref, gate_ref, g_ref, b_ref, o_ref):
    gate = gate_ref[...]
    f = gate[:, 0:1] * y0_ref[...] + gate[:, 1:2] * y1_ref[...]
    o_ref[...] = _deepnorm(h_ref[...], f, g_ref[...], b_ref[...])


def moe_combine_ln(h, y0, y1, gate, g, b, row0, n_rows):
    n = h.shape[1]
    tile = min(ROW_TILE, n_rows)
    assert row0 % tile == 0 and n_rows % tile == 0
    first = row0 // tile
    row = lambda width: pl.BlockSpec((tile, width), lambda i: (first + i, 0))
    fixed = pl.BlockSpec((1, n), lambda i: (0, 0))
    return pl.pallas_call(
        _combine_kernel,
        grid=(n_rows // tile,),
        in_specs=[row(n), row(n), row(n), row(TOP_K), fixed, fixed],
        out_specs=pl.BlockSpec((tile, n), lambda i: (i, 0)),
        out_shape=jax.ShapeDtypeStruct((n_rows, n), jnp.float32),
        compiler_params=_cparams("parallel"),
        name="moe_combine_ln",
    )(h, y0, y1, gate, g.reshape(1, n), b.reshape(1, n))


def _ffn_kernel(te_ref, nt_ref, x_ref, wg_ref, wu_ref, wd_ref, *rest, fuse_ln):
    o_ref = rest[-1]
    t = pl.program_id(0)
    j = pl.program_id(1)

    @pl.when(t < nt_ref[0])
    def _():
        x = x_ref[...].astype(jnp.bfloat16)
        g = jnp.dot(x, wg_ref[...], preferred_element_type=jnp.float32)
        u = jnp.dot(x, wu_ref[...], preferred_element_type=jnp.float32)
        h = (g * jax.nn.sigmoid(g) * u).astype(jnp.bfloat16)
        part = jnp.dot(h, wd_ref[...], preferred_element_type=jnp.float32)

        @pl.when(j == 0)
        def _():
            o_ref[...] = part

        @pl.when(j > 0)
        def _():
            if fuse_ln:
                o_ref[...] = _deepnorm(x_ref[...], o_ref[...] + part, rest[0][...], rest[1][...])
            else:
                o_ref[...] += part

    @pl.when(jnp.logical_and(t >= nt_ref[0], j == 0))
    def _():
        o_ref[...] = jnp.zeros_like(o_ref)


def grouped_ffn(x, w_gu_bf16, w_down_bf16, tile_expert, n_tiles_used, ln=None):
    r, d = x.shape
    nf = D_FF // FF_TILE
    assert nf == 2
    tile = min(ROW_TILE, r)
    n_tiles = pl.cdiv(r, tile)
    in_specs = [
        pl.BlockSpec((tile, d), lambda t, j, te, nt: (t, 0)),
        pl.BlockSpec((None, d, FF_TILE), lambda t, j, te, nt: (te[t], 0, j)),
        pl.BlockSpec((None, d, FF_TILE), lambda t, j, te, nt: (te[t], 0, nf + j)),
        pl.BlockSpec((None, FF_TILE, d), lambda t, j, te, nt: (te[t], j, 0)),
    ]
    args = [tile_expert, n_tiles_used, x, w_gu_bf16, w_gu_bf16, w_down_bf16]
    if ln is not None:
        in_specs += [pl.BlockSpec((1, d), lambda t, j, te, nt: (0, 0))] * 2
        args += [ln[0].reshape(1, d), ln[1].reshape(1, d)]
    grid_spec = pltpu.PrefetchScalarGridSpec(
        num_scalar_prefetch=2,
        grid=(n_tiles, nf),
        in_specs=in_specs,
        out_specs=pl.BlockSpec((tile, d), lambda t, j, te, nt: (t, 0)),
    )
    return pl.pallas_call(
        functools.partial(_ffn_kernel, fuse_ln=ln is not None),
        grid_spec=grid_spec,
        out_shape=jax.ShapeDtypeStruct((r, d), jnp.float32),
        compiler_params=_cparams("parallel", "arbitrary"),
        name="grouped_ffn",
    )(*args)


def _s5_kernel(u_ref, perm_ref, h0r_ref, h0i_ref, ar_ref, ai_ref, bbr_ref, bbi_ref, cr_ref, ci_ref, d_ref, wglu_ref,
               y_ref, hro_ref, hio_ref, bur, bui, sr, si, hr, hi, *, chains, chunk):
    j = pl.program_id(0)

    @pl.when(j == 0)
    def _():
        hr[...] = h0r_ref[...]
        hi[...] = h0i_ref[...]

    rows_n = chains * chunk
    u = u_ref[...].reshape(rows_n, S5_DIM)
    to_tc = perm_ref[...]
    ub = jnp.dot(to_tc, u.astype(jnp.bfloat16), preferred_element_type=jnp.float32).astype(jnp.bfloat16)
    hd, hn = S5_DIM // 2, S5_N // 2

    def b_proj(w_ref):
        return jnp.concatenate([jnp.dot(ub[:, h * hd:(h + 1) * hd], w_ref[h * hd:(h + 1) * hd, h * hn:(h + 1) * hn],
                                        preferred_element_type=jnp.float32) for h in range(2)], axis=1)

    bu_r = b_proj(bbr_ref)
    bu_i = b_proj(bbi_ref)
    for k in range(S5_LT):
        bur[k] = bu_r[:, k * LANES:(k + 1) * LANES]
        bui[k] = bu_i[:, k * LANES:(k + 1) * LANES]
    ar = [jnp.broadcast_to(ar_ref[:, k * LANES:(k + 1) * LANES], (chains, LANES)) for k in range(S5_LT)]
    ai = [jnp.broadcast_to(ai_ref[:, k * LANES:(k + 1) * LANES], (chains, LANES)) for k in range(S5_LT)]

    def body(t, carry):
        rows = pl.ds(t * chains, chains)
        out = []
        for k in range(S5_LT):
            xr, xi = carry[2 * k], carry[2 * k + 1]
            nr = ar[k] * xr - ai[k] * xi + bur[k, rows, :]
            ni = ar[k] * xi + ai[k] * xr + bui[k, rows, :]
            sr[k, rows, :] = nr
            si[k, rows, :] = ni
            out += [nr, ni]
        return tuple(out)

    init = []
    for k in range(S5_LT):
        init += [hr[:, k * LANES:(k + 1) * LANES], hi[:, k * LANES:(k + 1) * LANES]]
    fin = lax.fori_loop(0, chunk, body, tuple(init), unroll=2)
    xr = jnp.concatenate(fin[0::2], axis=1)
    xi = jnp.concatenate(fin[1::2], axis=1)
    hr[...] = xr
    hi[...] = xi
    hro_ref[...] = xr
    hio_ref[...] = xi
    s_r = jnp.concatenate([sr[k] for k in range(S5_LT)], axis=1).astype(jnp.bfloat16)
    s_i = jnp.concatenate([si[k] for k in range(S5_LT)], axis=1).astype(jnp.bfloat16)
    y = jnp.concatenate(
        [jnp.dot(s_r[:, h * hn:(h + 1) * hn], cr_ref[h * hn:(h + 1) * hn, h * hd:(h + 1) * hd],
                 preferred_element_type=jnp.float32)
         - jnp.dot(s_i[:, h * hn:(h + 1) * hn], ci_ref[h * hn:(h + 1) * hn, h * hd:(h + 1) * hd],
                   preferred_element_type=jnp.float32) for h in range(2)], axis=1)
    y = sum(lax.dot_general(to_tc, part, (((0,), (0,)), ((), ())), preferred_element_type=jnp.float32)
            for part in _split3(y)) + d_ref[...] * u
    z = jax.nn.gelu(y)
    gate = jax.nn.sigmoid(jnp.dot(z.astype(jnp.bfloat16), wglu_ref[...], preferred_element_type=jnp.float32))
    y_ref[...] = (z * gate).reshape(chains, chunk, S5_DIM)


def s5_params(lam_re, lam_im, log_dt, b, c, d, w_glu):
    f32 = jnp.float32
    dt = jnp.exp(log_dt.astype(f32))[:, None]
    mag = jnp.exp(lam_re * dt)
    ang = lam_im * dt
    ab_re = mag * jnp.cos(ang)
    ab_im = mag * jnp.sin(ang)
    den = lam_re * lam_re + lam_im * lam_im
    nr = ab_re - 1.0
    coef_re = (nr * lam_re + ab_im * lam_im) / den
    coef_im = (ab_im * lam_re - nr * lam_im) / den
    b_re = b[..., 0].astype(f32)
    b_im = b[..., 1].astype(f32)
    bb_re = coef_re[..., None] * b_re - coef_im[..., None] * b_im
    bb_im = coef_re[..., None] * b_im + coef_im[..., None] * b_re
    eye = jnp.eye(S5_GROUPS, dtype=f32)
    bbr = jnp.einsum('gnk,gh->gkhn', bb_re, eye).reshape(S5_DIM, S5_N).astype(jnp.bfloat16)
    bbi = jnp.einsum('gnk,gh->gkhn', bb_im, eye).reshape(S5_DIM, S5_N).astype(jnp.bfloat16)
    cr = jnp.einsum('gkn,gh->gnhk', c[..., 0].astype(f32), eye).reshape(S5_N, S5_DIM).astype(jnp.bfloat16)
    ci = jnp.einsum('gkn,gh->gnhk', c[..., 1].astype(f32), eye).reshape(S5_N, S5_DIM).astype(jnp.bfloat16)
    return (ab_re.reshape(1, S5_N), ab_im.reshape(1, S5_N), bbr, bbi, cr, ci,
            d.astype(f32).reshape(1, S5_DIM), w_glu.astype(jnp.bfloat16))


def s5_scan(u, h0, params, chunk):
    chains, t, _ = u.shape
    ar, ai, bbr, bbi, cr, ci, d, wglu = params
    h0r = h0[..., 0].reshape(chains, S5_N)
    h0i = h0[..., 1].reshape(chains, S5_N)
    full = lambda shape: pl.BlockSpec(shape, lambda j: (0,) * len(shape))
    rows = chains * chunk
    r = jnp.arange(rows)
    to_tc = (r[None, :] == (r[:, None] % chains) * chunk + r[:, None] // chains).astype(jnp.bfloat16)
    y, hr, hi = pl.pallas_call(
        functools.partial(_s5_kernel, chains=chains, chunk=chunk),
        grid=(t // chunk,),
        in_specs=[pl.BlockSpec((chains, chunk, S5_DIM), lambda j: (0, j, 0)), full((rows, rows)),
                  full((chains, S5_N)), full((chains, S5_N)), full((1, S5_N)), full((1, S5_N)),
                  full((S5_DIM, S5_N)), full((S5_DIM, S5_N)), full((S5_N, S5_DIM)), full((S5_N, S5_DIM)),
                  full((1, S5_DIM)), full((S5_DIM, S5_DIM))],
        out_specs=[pl.BlockSpec((chains, chunk, S5_DIM), lambda j: (0, j, 0)),
                   full((chains, S5_N)), full((chains, S5_N))],
        out_shape=[jax.ShapeDtypeStruct((chains, t, S5_DIM), jnp.float32),
                   jax.ShapeDtypeStruct((chains, S5_N), jnp.float32),
                   jax.ShapeDtypeStruct((chains, S5_N), jnp.float32)],
        scratch_shapes=[pltpu.VMEM((S5_LT, rows, LANES), jnp.float32)] * 4
                       + [pltpu.VMEM((chains, S5_N), jnp.float32)] * 2,
        compiler_params=_cparams("arbitrary"),
        name="s5_scan",
    )(u, to_tc, h0r, h0i, ar, ai, bbr, bbi, cr, ci, d, wglu)
    new_state = jnp.stack([hr.reshape(chains, S5_GROUPS, S5_STATE), hi.reshape(chains, S5_GROUPS, S5_STATE)],
                          axis=-1)
    return y, new_state


def _dot_nt(a, b):
    return lax.dot_general(a, b, (((1,), (1,)), ((), ())), preferred_element_type=jnp.float32)


def _split3(x):
    hi = x.astype(jnp.bfloat16)
    rem = x - hi.astype(jnp.float32)
    mid = rem.astype(jnp.bfloat16)
    lo = (rem - mid.astype(jnp.float32)).astype(jnp.bfloat16)
    return hi, mid, lo


def _softmax_rows(s, mask):
    s = jnp.where(mask, s, NEG)
    m = jnp.max(s, axis=-1, keepdims=True)
    p = jnp.exp2(s - m)
    inv = jnp.where(m > 0.5 * NEG, 1.0 / jnp.sum(p, axis=-1, keepdims=True), 0.0)
    return p * inv


def _nsa_prompt_kernel(q_ref, gate_ref, kc_ref, vc_ref, ks_ref, vs_ref, kw_ref, vw_ref, o_ref, *, n_cmp, n_blk):
    f32, bf16 = jnp.float32, jnp.bfloat16
    r4 = HEADS_PER_GROUP
    n_cpad = kc_ref.shape[0]
    start = pl.program_id(1) * Q_BLOCK
    q = q_ref[...] * QK_SCALE
    gate = gate_ref[...]
    lane = lax.broadcasted_iota(jnp.int32, (Q_BLOCK, LANES), 1)
    qpos = start + lax.broadcasted_iota(jnp.int32, (Q_BLOCK, 1), 0)
    n_idx = lax.broadcasted_iota(jnp.int32, (Q_BLOCK, n_cpad), 1)
    cmask = (((n_idx * CMP_STRIDE + (CMP_LEN - 1)) <= qpos) & (n_idx < n_cmp))[None]
    ratio = SEL_BLOCK // CMP_STRIDE
    gsum = (lax.broadcasted_iota(jnp.int32, (n_blk, n_cpad), 1) // ratio
            == lax.broadcasted_iota(jnp.int32, (n_blk, n_cpad), 0)).astype(bf16)
    blk = lax.broadcasted_iota(jnp.int32, (n_blk, Q_BLOCK), 0)
    blk_f = blk.astype(f32)
    jq = (start + lax.broadcasted_iota(jnp.int32, (n_blk, Q_BLOCK), 1)) // SEL_BLOCK
    force = jnp.where((blk == 0) | (blk == jq) | (blk == jq - 1), FORCE, 0.0)
    qgs, o_cs, sels = [], [], []
    for g in range(KV_GROUPS):
        keep = (lane < HEAD_DIM) if g == 0 else (lane >= HEAD_DIM)
        parts = []
        for r in range(r4):
            h = r4 * g + r
            tile = q[:, (h // 2) * LANES:(h // 2 + 1) * LANES]
            if h % 2 != g:
                tile = pltpu.roll(tile, HEAD_DIM, axis=1)
            parts.append(jnp.where(keep, tile, 0.0))
        qg = jnp.concatenate(parts, axis=0).astype(bf16)
        qgs.append(qg)

        p_c = _softmax_rows(_dot_nt(qg, kc_ref[...]).reshape(r4, Q_BLOCK, n_cpad), cmask)
        o_cs.append(jnp.dot(p_c.reshape(r4 * Q_BLOCK, n_cpad).astype(bf16), vc_ref[...],
                            preferred_element_type=f32).reshape(r4, Q_BLOCK, LANES))
        psum = p_c[0] + p_c[1] + p_c[2] + p_c[3]
        imp_t = sum(_dot_nt(gsum, part) for part in _split3(psum))

        score = jnp.where(blk <= jq, imp_t + force, NEG)
        sel_t = jnp.zeros((n_blk, Q_BLOCK), f32)
        for _ in range(min(N_SEL, n_blk)):
            m = jnp.max(score, axis=0, keepdims=True)
            idx = jnp.min(jnp.where(score == m, blk_f, float(n_blk)), axis=0, keepdims=True)
            hit = blk_f == idx
            sel_t = jnp.where(hit & (m > 0.5 * NEG), 1.0, sel_t)
            score = jnp.where(hit, REMOVED, score)
        sels.append(sel_t.T)

    n_full = start // SEL_TILE
    expand0 = (lax.broadcasted_iota(jnp.int32, (n_blk, SEL_TILE), 0)
               == lax.broadcasted_iota(jnp.int32, (n_blk, SEL_TILE), 1) // SEL_BLOCK).astype(bf16)

    def tile_update(i, carry, causal):
        off = pl.multiple_of(i * SEL_TILE, SEL_TILE)
        k = ks_ref[pl.ds(off, SEL_TILE), :]
        v = vs_ref[pl.ds(off, SEL_TILE), :]
        vlane = lax.broadcasted_iota(jnp.int32, (SEL_TILE, LANES), 1)
        v_ones = [jnp.where((vlane < HEAD_DIM) == (g == 0), v, jnp.ones_like(v)) for g in range(KV_GROUPS)]
        out = []
        for g in range(KV_GROUPS):
            m_run, l_run, acc = carry[g]
            s_t = _dot_nt(qgs[g], k).reshape(r4, Q_BLOCK, SEL_TILE)
            shifted = pltpu.roll(sels[g], (n_blk - i * (SEL_TILE // SEL_BLOCK)) % n_blk, axis=1).astype(bf16)
            mk = jnp.dot(shifted, expand0, preferred_element_type=f32) > 0.5
            if causal:
                kpos = i * SEL_TILE + lax.broadcasted_iota(jnp.int32, (Q_BLOCK, SEL_TILE), 1)
                mk = mk & (kpos <= qpos)
            s_t = jnp.where(mk[None], s_t, NEG)
            m_new = jnp.maximum(m_run, jnp.max(s_t, axis=-1, keepdims=True))
            alpha = jnp.exp2(m_run - m_new)
            p = jnp.exp2((s_t - m_new).astype(bf16))
            pv = jnp.dot(p.reshape(r4 * Q_BLOCK, SEL_TILE), v_ones[g], preferred_element_type=f32)
            pv = pv.reshape(r4, Q_BLOCK, LANES)
            l_new = alpha * l_run + pv[:, :, (1 - g) * HEAD_DIM:(1 - g) * HEAD_DIM + 1]
            out.append((m_new, l_new, alpha * acc + pv))
        return tuple(out)

    init = (jnp.full((r4, Q_BLOCK, 1), NEG, f32), jnp.zeros((r4, Q_BLOCK, 1), f32),
            jnp.zeros((r4, Q_BLOCK, LANES), f32))
    carry = lax.fori_loop(0, n_full, lambda i, c: tile_update(i, c, False), (init, init))
    fin = tile_update(n_full, carry, True)

    n_win = WINDOW + Q_BLOCK
    woff = pl.multiple_of(start, Q_BLOCK)
    kwin = kw_ref[pl.ds(woff, n_win), :]
    vwin = vw_ref[pl.ds(woff, n_win), :]
    wpos = start - WINDOW + lax.broadcasted_iota(jnp.int32, (Q_BLOCK, n_win), 1)
    wmask = ((wpos <= qpos) & (wpos > qpos - WINDOW) & (wpos >= 0))[None]
    heads = [None] * N_HEADS
    for g in range(KV_GROUPS):
        m_fin, l_fin, acc = fin[g]
        o_s = acc * jnp.where(m_fin > 0.5 * NEG, 1.0 / l_fin, 0.0)
        p_w = _softmax_rows(_dot_nt(qgs[g], kwin).reshape(r4, Q_BLOCK, n_win), wmask)
        o_w = jnp.dot(p_w.reshape(r4 * Q_BLOCK, n_win).astype(bf16), vwin,
                      preferred_element_type=f32).reshape(r4, Q_BLOCK, LANES)
        for r in range(r4):
            h = r4 * g + r
            heads[h] = (gate[:, 3 * h:3 * h + 1] * o_cs[g][r] + gate[:, 3 * h + 1:3 * h + 2] * o_s[r]
                        + gate[:, 3 * h + 2:3 * h + 3] * o_w[r])

    tiles = []
    for j in range(N_HEADS // 2):
        even, odd = heads[2 * j], heads[2 * j + 1]
        if j // 2 == 0:
            tiles.append(jnp.where(lane < HEAD_DIM, even, pltpu.roll(odd, HEAD_DIM, axis=1)))
        else:
            tiles.append(jnp.where(lane < HEAD_DIM, pltpu.roll(even, HEAD_DIM, axis=1), odd))
    o_ref[...] = jnp.concatenate(tiles, axis=1)


def nsa_prompt(q, gates, kc, vc, kvs, kvw_pad):
    b, t, _ = q.shape
    n_cpad = kc.shape[1]
    kern = functools.partial(_nsa_prompt_kernel, n_cmp=t // CMP_STRIDE - 1, n_blk=t // SEL_BLOCK)
    whole = lambda rows, c=0: pl.BlockSpec((None, rows, LANES), lambda i, j: (i, 0, c))
    return pl.pallas_call(
        kern,
        grid=(b, t // Q_BLOCK),
        in_specs=[pl.BlockSpec((None, Q_BLOCK, NSA_Q), lambda i, j: (i, j, 0)),
                  pl.BlockSpec((None, Q_BLOCK, 3 * N_HEADS), lambda i, j: (i, j, 0)),
                  whole(n_cpad), whole(n_cpad), whole(t, 0), whole(t, 1), whole(t + WINDOW, 0),
                  whole(t + WINDOW, 1)],
        out_specs=pl.BlockSpec((None, Q_BLOCK, NSA_Q), lambda i, j: (i, j, 0)),
        out_shape=jax.ShapeDtypeStruct((b, t, NSA_Q), jnp.float32),
        compiler_params=_cparams("parallel", "arbitrary"),
        name="nsa_prompt",
    )(q, gates, kc, vc, kvs, kvs, kvw_pad, kvw_pad)


def _compress_kernel(ch_ref, pet_ref, peb_ref, w1t_ref, w1b_ref, w2_ref, o_ref):
    bf16 = jnp.bfloat16
    ch = ch_ref[...]
    n_ch = ch.shape[0]
    a = jnp.dot((ch + pet_ref[...]).astype(bf16), w1t_ref[...], preferred_element_type=jnp.float32)
    b = jnp.dot((ch + peb_ref[...]).astype(bf16), w1b_ref[...], preferred_element_type=jnp.float32)
    pre = a + pltpu.roll(b, n_ch - 1, axis=0)
    o_ref[...] = jnp.dot(jax.nn.gelu(pre).astype(bf16), w2_ref[...],
                         preferred_element_type=jnp.float32).astype(o_ref.dtype)


def compress_params(w1, w2, pe):
    f32 = jnp.float32
    eye = jnp.eye(KV_GROUPS, dtype=f32)
    w1r = w1.astype(f32).reshape(2, CMP_STRIDE, HEAD_DIM, HEAD_DIM)
    big = jnp.einsum('hjde,gk->hjgdke', w1r, eye).reshape(2, CMP_STRIDE * LANES, LANES).astype(jnp.bfloat16)
    w2bd = jnp.einsum('de,gk->gdke', w2.astype(f32), eye).reshape(LANES, LANES).astype(jnp.bfloat16)
    per = pe.astype(f32).reshape(2, CMP_STRIDE, 1, HEAD_DIM)
    pe_rows = jnp.broadcast_to(per, (2, CMP_STRIDE, KV_GROUPS, HEAD_DIM)).reshape(2, 1, CMP_STRIDE * LANES)
    return pe_rows[0], pe_rows[1], big[0], big[1], w2bd


def compress_prompt(x, params):
    b, t, _ = x.shape
    n_ch = t // CMP_STRIDE
    ch = x.reshape(b, n_ch, CMP_STRIDE * LANES)
    pet, peb, w1t, w1b, w2bd = params
    full = lambda shape: pl.BlockSpec(shape, lambda i: (0,) * len(shape))
    return pl.pallas_call(
        _compress_kernel,
        grid=(b,),
        in_specs=[pl.BlockSpec((None, n_ch, CMP_STRIDE * LANES), lambda i: (i, 0, 0)),
                  full((1, CMP_STRIDE * LANES)), full((1, CMP_STRIDE * LANES)),
                  full((CMP_STRIDE * LANES, LANES)), full((CMP_STRIDE * LANES, LANES)), full((LANES, LANES))],
        out_specs=pl.BlockSpec((None, n_ch, LANES), lambda i: (i, 0, 0)),
        out_shape=jax.ShapeDtypeStruct((b, n_ch, LANES), jnp.bfloat16),
        compiler_params=_cparams("parallel"),
        name="compress_prompt",
    )(ch, pet, peb, w1t, w1b, w2bd)


def _cmp_sample_kernel(pt_ref, *refs, n_pages):
    f32, bf16 = jnp.float32, jnp.bfloat16
    pp = PAGES_PER_STEP
    pages = refs[0:pp]
    (perm_ref, newk_ref, newv_ref, wk_ref, wv_ref, ck_ref, cv_ref, w2k_ref, w2v_ref,
     kc_ref, vc_ref, slab_k, slab_v) = refs[pp:]
    s = pl.program_id(1)
    cpp = PAGE_SIZE // CMP_STRIDE
    base = pl.multiple_of(s * (pp * cpp), pp * cpp)
    for i in range(pp):
        rows = _dot_nt(perm_ref[...], pages[i][...].astype(bf16))
        for half, slab in enumerate((slab_k, slab_v)):
            for j in range(CMP_STRIDE):
                slab[j, pl.ds(base + i * cpp, cpp), :] = rows[j * cpp:(j + 1) * cpp, half * LANES:(half + 1) * LANES]

    @pl.when(s == pl.num_programs(1) - 1)
    def _():
        n_ch = n_pages * (PAGE_SIZE // CMP_STRIDE)
        row = lax.broadcasted_iota(jnp.int32, (n_ch, LANES), 0)
        for slab, new_ref, w_ref, c_ref, w2_ref, o_ref in ((slab_k, newk_ref, wk_ref, ck_ref, w2k_ref, kc_ref),
                                                           (slab_v, newv_ref, wv_ref, cv_ref, w2v_ref, vc_ref)):
            ch = jnp.concatenate([slab[j] for j in range(CMP_STRIDE)], axis=1).astype(bf16)
            ab = jnp.dot(ch, w_ref[...], preferred_element_type=f32)
            b_new = jnp.dot(new_ref[...].astype(bf16), w_ref[...], preferred_element_type=f32)[0:1, LANES:]
            nxt = pltpu.roll(ab[:, LANES:], n_ch - 1, axis=0)
            nxt = jnp.where(row == n_ch - 1, b_new, nxt)
            pre = ab[:, :LANES] + nxt + c_ref[...]
            o_ref[...] = jnp.dot(jax.nn.gelu(pre).astype(bf16), w2_ref[...],
                                 preferred_element_type=f32).astype(o_ref.dtype)


def compress_sample_params(w1, w2, pe):
    pet, peb, w1t, w1b, w2bd = compress_params(w1, w2, pe)
    hp = lax.Precision.HIGHEST
    const = (jnp.dot(pet, w1t.astype(jnp.float32), precision=hp)
             + jnp.dot(peb, w1b.astype(jnp.float32), precision=hp))
    return jnp.concatenate([w1t, w1b], axis=1), const, w2bd


def _page_spec(i, pair):
    return pl.BlockSpec((None, 2 * LANES, PAGE_SIZE),
                        lambda b, s, pt: (pt[b, PAGES_PER_STEP * s + i], pair, 0))


def _per_seq(shape):
    return pl.BlockSpec((None,) + shape, lambda b, s, pt: (b, 0, 0))


def compress_sample(pool_t, page_table, new_k, new_v, pk, pv):
    bsz, n_pages = page_table.shape
    pp = PAGES_PER_STEP
    n_ch = n_pages * (PAGE_SIZE // CMP_STRIDE)
    t_new = new_k.shape[1]

    def chunk_rows(x):
        x = jnp.pad(x, ((0, 0), (0, CMP_STRIDE - t_new), (0, 0))).reshape(bsz, 1, CMP_STRIDE * LANES)
        return jnp.pad(x, ((0, 0), (0, 7), (0, 0)))

    full = lambda shape: pl.BlockSpec(shape, lambda b, s, pt: (0,) * len(shape))
    r = jnp.arange(PAGE_SIZE)
    cpp = PAGE_SIZE // CMP_STRIDE
    perm = (r[None, :] == (r[:, None] % cpp) * CMP_STRIDE + r[:, None] // cpp).astype(jnp.bfloat16)
    wk, ck, w2k = pk
    wv, cv, w2v = pv
    grid_spec = pltpu.PrefetchScalarGridSpec(
        num_scalar_prefetch=1,
        grid=(bsz, n_pages // pp),
        in_specs=[_page_spec(i, 0) for i in range(pp)]
                 + [full((PAGE_SIZE, PAGE_SIZE)), _per_seq((8, CMP_STRIDE * LANES)),
                    _per_seq((8, CMP_STRIDE * LANES)),
                    full((CMP_STRIDE * LANES, 2 * LANES)), full((CMP_STRIDE * LANES, 2 * LANES)),
                    full((1, LANES)), full((1, LANES)), full((LANES, LANES)), full((LANES, LANES))],
        out_specs=[_per_seq((n_ch, LANES)), _per_seq((n_ch, LANES))],
        scratch_shapes=[pltpu.VMEM((CMP_STRIDE, n_ch, LANES), jnp.float32)] * 2,
    )
    return pl.pallas_call(
        functools.partial(_cmp_sample_kernel, n_pages=n_pages),
        grid_spec=grid_spec,
        out_shape=[jax.ShapeDtypeStruct((bsz, n_ch, LANES), jnp.bfloat16)] * 2,
        compiler_params=_cparams("parallel", "arbitrary"),
        name="compress_sample",
    )(page_table, *([pool_t] * pp), perm, chunk_rows(new_k), chunk_rows(new_v), wk, wv, ck, cv, w2k, w2v)


def _nsa_sample_kernel(pt_ref, *refs, n_pages, t_new, w_buf):
    f32, bf16 = jnp.float32, jnp.bfloat16
    pp = PAGES_PER_STEP
    q_ref, gate_ref, kc_ref, vc_ref = refs[0:4]
    pages = refs[4:4 + pp]
    (ksn_ref, vsn_ref, win_ref, kwn_ref, vwn_ref, o_ref,
     sel_scr, exp_scr, oc_scr, m_scr, l_scr, acc_scr) = refs[4 + pp:]
    r4, g2 = HEADS_PER_GROUP, KV_GROUPS
    n_rows = g2 * r4 * t_new
    past_len = n_pages * PAGE_SIZE
    n_cmp = kc_ref.shape[0]
    n_bpad = sel_scr.shape[1]
    tile = pp * PAGE_SIZE
    s = pl.program_id(1)
    qall = q_ref[...]
    qpos = past_len + lax.broadcasted_iota(jnp.int32, (n_rows, 1), 0) % t_new

    def grouped(x):
        return x.reshape(g2, 1, t_new, x.shape[-1])

    @pl.when(s == 0)
    def _():
        s_c = _dot_nt(qall, kc_ref[...])
        n_idx = lax.broadcasted_iota(jnp.int32, (n_rows, n_cmp), 1)
        p_c = _softmax_rows(s_c, (n_idx * CMP_STRIDE + (CMP_LEN - 1)) <= qpos)
        oc_scr[...] = jnp.dot(p_c.astype(bf16), vc_ref[...], preferred_element_type=f32)
        psum = jnp.sum(p_c.reshape(g2, r4, t_new, n_cmp), axis=1).reshape(g2 * t_new, n_cmp)
        psum = jnp.concatenate([psum, jnp.zeros((LANES - g2 * t_new, n_cmp), f32)], axis=0)
        p_hi = psum.astype(bf16)
        rem = psum - p_hi.astype(f32)
        p_mid = rem.astype(bf16)
        p_lo = (rem - p_mid.astype(f32)).astype(bf16)
        ratio = SEL_BLOCK // CMP_STRIDE
        gsum = (lax.broadcasted_iota(jnp.int32, (n_bpad, n_cmp), 1) // ratio
                == lax.broadcasted_iota(jnp.int32, (n_bpad, n_cmp), 0)).astype(bf16)
        imp_t = _dot_nt(gsum, p_hi) + _dot_nt(gsum, p_mid) + _dot_nt(gsum, p_lo)
        blk = lax.broadcasted_iota(jnp.int32, (n_bpad, LANES), 0)
        jq = (past_len + lax.broadcasted_iota(jnp.int32, (n_bpad, LANES), 1) % t_new) // SEL_BLOCK
        forced = (blk == 0) | (blk == jq) | (blk == jq - 1)
        score = jnp.where(blk <= jq, imp_t + jnp.where(forced, FORCE, 0.0), NEG)
        blk_f = blk.astype(f32)
        sel_t = jnp.zeros((n_bpad, LANES), f32)
        for _ in range(N_SEL):
            m = jnp.max(score, axis=0, keepdims=True)
            idx = jnp.min(jnp.where(score == m, blk_f, float(n_bpad)), axis=0, keepdims=True)
            hit = blk_f == idx
            sel_t = jnp.where(hit & (m > 0.5 * NEG), 1.0, sel_t)
            score = jnp.where(hit, REMOVED, score)
        sel = jnp.concatenate([sel_t[k * LANES:(k + 1) * LANES].T for k in range(n_bpad // LANES)], axis=1)
        sel_scr[...] = sel[0:g2 * t_new]
        exp_scr[...] = (lax.broadcasted_iota(jnp.int32, (LANES, tile), 0)
                        == lax.broadcasted_iota(jnp.int32, (LANES, tile), 1) // SEL_BLOCK).astype(bf16)
        m_scr[...] = jnp.full(m_scr.shape, NEG, f32)
        l_scr[...] = jnp.zeros(l_scr.shape, f32)
        acc_scr[...] = jnp.zeros(acc_scr.shape, f32)

    def online_update(s_t, mk, v, v_feature_major):
        n = s_t.shape[-1]
        s4 = jnp.where(mk, s_t.reshape(g2, r4, t_new, n), NEG)
        m_run = m_scr[...].reshape(g2, r4, t_new, 1)
        m_new = jnp.maximum(m_run, jnp.max(s4, axis=-1, keepdims=True))
        alpha = jnp.exp2(m_run - m_new)
        p = jnp.exp2(s4 - m_new)
        l_new = alpha * l_scr[...].reshape(g2, r4, t_new, 1) + jnp.sum(p, axis=-1, keepdims=True)
        pb = p.reshape(n_rows, n).astype(bf16)
        pv = _dot_nt(pb, v) if v_feature_major else jnp.dot(pb, v, preferred_element_type=f32)
        m_scr[...] = m_new.reshape(n_rows, 1)
        l_scr[...] = l_new.reshape(n_rows, 1)
        acc_scr[...] = alpha.reshape(n_rows, 1) * acc_scr[...] + pv

    kt = jnp.concatenate([r[0:LANES, :] for r in pages], axis=1).astype(bf16)
    vt = jnp.concatenate([r[LANES:2 * LANES, :] for r in pages], axis=1).astype(bf16)
    shifted = pltpu.roll(sel_scr[...], (n_bpad - s * (tile // SEL_BLOCK)) % n_bpad, axis=1)
    picked = jnp.dot(shifted[:, 0:LANES].astype(bf16), exp_scr[...], preferred_element_type=f32)
    online_update(jnp.dot(qall, kt, preferred_element_type=f32), grouped(picked) > 0.5, vt, True)

    @pl.when(s == pl.num_programs(1) - 1)
    def _():
        new_blk = past_len // SEL_BLOCK
        kidx = lax.broadcasted_iota(jnp.int32, (n_rows, NEW_PAD), 1)
        causal = ((past_len + kidx) <= qpos) & (kidx < t_new)
        picked_new = sel_scr[:, new_blk:new_blk + 1]
        mk = (grouped(picked_new) > 0.5) & causal.reshape(g2, r4, t_new, NEW_PAD)
        online_update(_dot_nt(qall, ksn_ref[...]), mk, vsn_ref[...], False)
        o_s = acc_scr[...] * jnp.where(m_scr[...] > 0.5 * NEG, 1.0 / l_scr[...], 0.0)

        n_win = w_buf + NEW_PAD
        kw_t = win_ref[0:LANES, :].astype(bf16)
        vw_t = win_ref[LANES:2 * LANES, :].astype(bf16)
        widx = lax.broadcasted_iota(jnp.int32, (n_rows, n_win), 1)
        wpos = past_len - w_buf + widx
        wmask = (wpos <= qpos) & (wpos > qpos - WINDOW) & (wpos >= 0) & (widx < w_buf + t_new)
        s_w = jnp.concatenate([jnp.dot(qall, kw_t, preferred_element_type=f32), _dot_nt(qall, kwn_ref[...])],
                              axis=1)
        p_w = _softmax_rows(s_w, wmask).astype(bf16)
        o_w = (_dot_nt(p_w[:, 0:w_buf], vw_t)
               + jnp.dot(p_w[:, w_buf:], vwn_ref[...], preferred_element_type=f32))
        gate = gate_ref[...]
        o_ref[...] = gate[:, 0:1] * oc_scr[...] + gate[:, 1:2] * o_s + gate[:, 2:3] * o_w


def nsa_sample(q, gates, kc, vc, pool_t, page_table, ks_new, vs_new, win, kw_new, vw_new):
    f32, bf16 = jnp.float32, jnp.bfloat16
    bsz, t_new = q.shape[0], q.shape[1]
    n_pages = page_table.shape[1]
    pp = PAGES_PER_STEP
    w_buf = win.shape[2]
    r4, g2 = HEADS_PER_GROUP, KV_GROUPS
    n_rows = g2 * r4 * t_new
    past_len = n_pages * PAGE_SIZE
    assert past_len % SEL_BLOCK == 0 and t_new <= SEL_BLOCK and past_len >= w_buf and n_pages % pp == 0
    n_sel = past_len // SEL_BLOCK + 1
    n_bpad = -(-n_sel // LANES) * LANES
    eye = jnp.eye(g2, dtype=f32)
    qg = q.reshape(bsz, t_new, g2, r4, HEAD_DIM).transpose(0, 2, 3, 1, 4) * QK_SCALE
    qall = jnp.einsum('bgrqd,gk->bgrqkd', qg, eye).reshape(bsz, n_rows, LANES).astype(bf16)
    gall = gates.reshape(bsz, t_new, g2, r4, 3).transpose(0, 2, 3, 1, 4).reshape(bsz, n_rows, 3)
    pad_rows = lambda x: jnp.pad(x, ((0, 0), (0, NEW_PAD - t_new), (0, 0))).astype(bf16)
    n_cmp = kc.shape[1]
    grid_spec = pltpu.PrefetchScalarGridSpec(
        num_scalar_prefetch=1,
        grid=(bsz, n_pages // pp),
        in_specs=[_per_seq((n_rows, LANES)), _per_seq((n_rows, 3)), _per_seq((n_cmp, LANES)),
                  _per_seq((n_cmp, LANES))]
                 + [_page_spec(i, 1) for i in range(pp)]
                 + [_per_seq((NEW_PAD, LANES)), _per_seq((NEW_PAD, LANES)), _per_seq((2 * LANES, w_buf)),
                    _per_seq((NEW_PAD, LANES)), _per_seq((NEW_PAD, LANES))],
        out_specs=_per_seq((n_rows, LANES)),
        scratch_shapes=[pltpu.VMEM((g2 * t_new, n_bpad), f32), pltpu.VMEM((LANES, pp * PAGE_SIZE), bf16),
                        pltpu.VMEM((n_rows, LANES), f32),
                        pltpu.VMEM((n_rows, 1), f32), pltpu.VMEM((n_rows, 1), f32),
                        pltpu.VMEM((n_rows, LANES), f32)],
    )
    o = pl.pallas_call(
        functools.partial(_nsa_sample_kernel, n_pages=n_pages, t_new=t_new, w_buf=w_buf),
        grid_spec=grid_spec,
        out_shape=jax.ShapeDtypeStruct((bsz, n_rows, LANES), f32),
        compiler_params=_cparams("parallel", "arbitrary"),
        name="nsa_sample",
    )(page_table, qall, gall, kc, vc, *([pool_t] * pp), pad_rows(ks_new), pad_rows(vs_new), win,
      pad_rows(kw_new), pad_rows(vw_new))
    o = jnp.einsum('bgrqkd,gk->bqgrd', o.reshape(bsz, g2, r4, t_new, g2, HEAD_DIM), eye)
    return o.reshape(bsz, t_new, NSA_Q)


def _ssd_kernel(x_ref, b_ref, c_ref, dt_ref, a_ref, za_ref, zb_ref, dskip_ref, ng_ref, y_ref, hout_ref, h_scr, *,
                chunk):
    f32, bf16 = jnp.float32, jnp.bfloat16
    n_l = chunk
    hpg = SSD_HEADS // SSD_GROUPS
    gw = hpg * SSD_HEAD_DIM
    j = pl.program_id(1)

    @pl.when(j == 0)
    def _():
        h_scr[...] = jnp.zeros(h_scr.shape, f32)

    x = x_ref[...]
    dt = dt_ref[...]
    tri_b = (lax.broadcasted_iota(jnp.int32, (n_l, n_l), 0) >= lax.broadcasted_iota(jnp.int32, (n_l, n_l), 1))
    tri = tri_b.astype(bf16)
    cum = sum(jnp.dot(tri, part, preferred_element_type=f32) for part in _split3(dt * a_ref[...]))
    cum_t = cum.T
    dt_t = dt.T
    ecum = jnp.exp(cum)
    clast = cum[n_l - 1:n_l, :]
    wt = jnp.exp(clast - cum) * dt
    elast = jnp.exp(clast)
    lane = lax.broadcasted_iota(jnp.int32, (n_l, LANES), 1)
    low = lane < SSD_HEAD_DIM

    def pair(v, h0):
        return jnp.where(low[:v.shape[0]], v[:, h0:h0 + 1], v[:, h0 + 1:h0 + 2])

    tiles = []
    for g in range(SSD_GROUPS):
        bg = b_ref[:, g * SSD_STATE:(g + 1) * SSD_STATE]
        cgb = c_ref[:, g * SSD_STATE:(g + 1) * SSD_STATE].astype(bf16)
        bgt = bg.T.astype(bf16)
        cb = jnp.dot(cgb, bgt, preferred_element_type=f32)
        hg = h_scr[g]
        y_inter = jnp.dot(cgb, hg.astype(bf16), preferred_element_type=f32)
        xw, dec = [], []
        for pr in range(hpg // 2):
            h0 = hpg * g + 2 * pr
            xt = x[:, (h0 // 2) * LANES:(h0 // 2 + 1) * LANES]
            acc = None
            for k in range(2):
                h = h0 + k
                seg = cum[:, h:h + 1] - cum_t[h:h + 1, :]
                w = cb * jnp.exp(jnp.where(tri_b, seg, NEG)) * dt_t[h:h + 1, :]
                xm = jnp.where(low if k == 0 else jnp.logical_not(low), xt, 0.0).astype(bf16)
                part = jnp.dot(w.astype(bf16), xm, preferred_element_type=f32)
                acc = part if acc is None else acc + part
            tiles.append(acc + y_inter[:, pr * LANES:(pr + 1) * LANES] * pair(ecum, h0))
            xw.append((xt * pair(wt, h0)).astype(bf16))
            dec.append(pair(elast, h0))
        h_scr[g] = (hg * jnp.concatenate(dec, axis=1)
                    + jnp.dot(bgt, jnp.concatenate(xw, axis=1), preferred_element_type=f32))
    y = jnp.concatenate(tiles, axis=1) + dskip_ref[...] * x
    zg = jnp.concatenate([za_ref[...], zb_ref[...]], axis=1)
    v = y * (zg * jax.nn.sigmoid(zg))
    outs = []
    for g in range(SSD_GROUPS):
        vg = v[:, g * gw:(g + 1) * gw]
        outs.append(vg * lax.rsqrt(jnp.mean(vg * vg, axis=-1, keepdims=True) + RMS_EPS))
    y_ref[...] = jnp.concatenate(outs, axis=1) * ng_ref[...]

    @pl.when(j == pl.num_programs(1) - 1)
    def _():
        hout_ref[...] = h_scr[...]


def ssd_prompt(xbc, dt, a, z, zg_offset, d_skip, norm_g):
    f32 = jnp.float32
    bsz, t, _ = xbc.shape
    hpg = SSD_HEADS // SSD_GROUPS
    gn = SSD_GROUPS * SSD_STATE
    half = SSD_INNER // 2
    assert zg_offset % half == 0
    dt_p = jnp.pad(dt, ((0, 0), (0, 0), (0, LANES - SSD_HEADS)))
    a_p = jnp.pad(a.astype(f32), (0, LANES - SSD_HEADS)).reshape(1, LANES)
    dsk = jnp.repeat(d_skip.astype(f32), SSD_HEAD_DIM).reshape(1, SSD_INNER)
    blk = lambda w, c: pl.BlockSpec((None, SSD_CHUNK, w), lambda b, j: (b, j, c))
    full = lambda shape: pl.BlockSpec(shape, lambda b, j: (0,) * len(shape))
    state_spec = pl.BlockSpec((None, SSD_GROUPS, SSD_STATE, hpg * SSD_HEAD_DIM), lambda b, j: (b, 0, 0, 0))
    y, h = pl.pallas_call(
        functools.partial(_ssd_kernel, chunk=SSD_CHUNK),
        grid=(bsz, t // SSD_CHUNK),
        in_specs=[blk(SSD_INNER, 0), blk(gn, SSD_INNER // gn), blk(gn, SSD_INNER // gn + 1), blk(LANES, 0),
                  full((1, LANES)), blk(half, zg_offset // half), blk(half, zg_offset // half + 1),
                  full((1, SSD_INNER)), full((1, SSD_INNER))],
        out_specs=[blk(SSD_INNER, 0), state_spec],
        out_shape=[jax.ShapeDtypeStruct((bsz, t, SSD_INNER), f32),
                   jax.ShapeDtypeStruct((bsz, SSD_GROUPS, SSD_STATE, hpg * SSD_HEAD_DIM), f32)],
        scratch_shapes=[pltpu.VMEM((SSD_GROUPS, SSD_STATE, hpg * SSD_HEAD_DIM), f32)],
        compiler_params=_cparams("parallel", "arbitrary"),
        name="ssd_prompt",
    )(xbc, xbc, xbc, dt_p, a_p, z, z, dsk, norm_g.astype(f32).reshape(1, SSD_INNER))
    h = h.reshape(bsz, SSD_GROUPS, SSD_STATE, hpg, SSD_HEAD_DIM).transpose(0, 1, 3, 4, 2)
    return y, h.reshape(bsz, SSD_HEADS, SSD_HEAD_DIM, SSD_STATE)


def _inproj_even_kernel(x_ref, w_ref, c_ref, sa_ref, sb_ref,
                        u_ref, q_ref, rows_ref, kvw_ref, kvsb_ref, kvwb_ref, g_ref):
    bf16 = jnp.bfloat16
    z = jnp.dot(x_ref[...].astype(bf16), w_ref[...], preferred_element_type=jnp.float32)
    cos, s_up, s_down = c_ref[...], sa_ref[...], sb_ref[...]

    def rot(t):
        return (t * cos + pltpu.roll(t, LANES - ROT_DIM // 2, axis=1) * s_up
                + pltpu.roll(t, ROT_DIM // 2, axis=1) * s_down)

    tile = lambda k: z[:, k * LANES:(k + 1) * LANES]
    q0 = S5_DIM // LANES
    kv0 = q0 + NSA_Q // LANES
    u_ref[...] = z[:, 0:S5_DIM]
    q_ref[...] = jnp.concatenate([rot(tile(q0 + k)) for k in range(NSA_Q // LANES)], axis=1)
    kc, vc, ks, vs, kw, vw = (rot(tile(kv0)), tile(kv0 + 1), rot(tile(kv0 + 2)), tile(kv0 + 3),
                              rot(tile(kv0 + 4)), tile(kv0 + 5))
    rows_ref[...] = jnp.concatenate([kc, vc, ks, vs], axis=1)
    kvw = jnp.concatenate([kw, vw], axis=1)
    kvw_ref[...] = kvw
    kvwb_ref[...] = kvw.astype(bf16)
    kvsb_ref[...] = jnp.concatenate([ks, vs], axis=1).astype(bf16)
    g0 = (kv0 + 6) * LANES
    g_ref[...] = jax.nn.sigmoid(z[:, g0:g0 + 3 * N_HEADS])


def rope_tables(pos):
    half = ROT_DIM // 2
    inv = ROPE_THETA ** (-jnp.arange(half, dtype=jnp.float32) * 2.0 / ROT_DIM)
    ang = pos.astype(jnp.float32)[:, None] * inv[None, :]
    d = jnp.arange(LANES) % HEAD_DIM
    cos = jnp.take(jnp.cos(ang), d % half, axis=1)
    sin = jnp.take(jnp.sin(ang), d % half, axis=1)
    return (jnp.where(d < ROT_DIM, cos, 1.0), jnp.where(d < half, -sin, 0.0),
            jnp.where((d >= half) & (d < ROT_DIM), sin, 0.0))


def inproj_even(h, w_bf16, pos):
    f32, bf16 = jnp.float32, jnp.bfloat16
    n, d = h.shape
    tile = min(ROW_TILE, n)
    row = lambda w: pl.BlockSpec((tile, w), lambda i: (i, 0))
    fixed = lambda shape: pl.BlockSpec(shape, lambda i: (0, 0), pipeline_mode=pl.Buffered(1))
    widths = [(S5_DIM, f32), (NSA_Q, f32), (2 * NSA_KV, f32), (NSA_KV, f32), (NSA_KV, bf16), (NSA_KV, bf16),
              (3 * N_HEADS, f32)]
    return pl.pallas_call(
        _inproj_even_kernel,
        grid=(pl.cdiv(n, tile),),
        in_specs=[row(d), fixed(w_bf16.shape), row(LANES), row(LANES), row(LANES)],
        out_specs=[row(w) for w, _ in widths],
        out_shape=[jax.ShapeDtypeStruct((n, w), dt) for w, dt in widths],
        compiler_params=_cparams("parallel"),
        name="inproj_even",
    )(h, w_bf16, *rope_tables(pos))


def _causal_conv_tile(x, tail, w_ref, b_ref, width):
    row = lax.broadcasted_iota(jnp.int32, (TAIL, x.shape[1]), 0)
    acc = b_ref[...] + w_ref[width - 1:width, :] * x
    for k in range(1, width):
        xs = pltpu.roll(x, k, axis=0)
        head = jnp.where(row < k, pltpu.roll(tail, k, axis=0), xs[0:TAIL])
        xs = jnp.concatenate([head, xs[TAIL:]], axis=0)
        acc = acc + w_ref[width - 1 - k:width - k, :] * xs
    return acc


def _inproj_odd_kernel(x_ref, w_ref, scw_ref, scb_ref, cvw_ref, cvb_ref, dtb_ref,
                       ysc_ref, xbc_ref, dt_ref, zg_ref, tsc_ref, tx_ref, tail_sc, tail_x, *, tiles_per_seq):
    @pl.when(pl.program_id(0) % tiles_per_seq == 0)
    def _():
        tail_sc[...] = jnp.zeros(tail_sc.shape, jnp.float32)
        tail_x[...] = jnp.zeros(tail_x.shape, jnp.float32)

    z = jnp.dot(x_ref[...].astype(jnp.bfloat16), w_ref[...], preferred_element_type=jnp.float32)
    o_zg = 3 * SC_DIM
    o_x = o_zg + SSD_INNER
    o_dt = o_x + SSD_CONV_DIM
    n = z.shape[0]
    prod = z[:, 2 * SC_DIM:3 * SC_DIM] * z[:, 0:SC_DIM]
    ysc_ref[...] = z[:, SC_DIM:2 * SC_DIM] * _causal_conv_tile(prod, tail_sc[...], scw_ref, scb_ref, SC_WIDTH)
    xbc = z[:, o_x:o_dt]
    c = _causal_conv_tile(xbc, tail_x[...], cvw_ref, cvb_ref, SSD_CONV)
    xbc_ref[...] = c * jax.nn.sigmoid(c)
    dt_ref[...] = jax.nn.softplus(z[:, o_dt:o_dt + SSD_HEADS] + dtb_ref[...])
    zg_ref[...] = z[:, o_zg:o_x]
    tail_sc[...] = prod[n - TAIL:n]
    tail_x[...] = xbc[n - TAIL:n]
    tsc_ref[...] = prod[n - TAIL:n]
    tx_ref[...] = xbc[n - TAIL:n]


def inproj_odd_prompt(h, w_bf16, bsz, sc_w, sc_b, cv_w, cv_b, dt_bias):
    f32 = jnp.float32
    n_rows, d = h.shape
    t = n_rows // bsz
    assert t % ROW_TILE == 0
    tps = t // ROW_TILE
    row = lambda w: pl.BlockSpec((ROW_TILE, w), lambda i: (i, 0))
    fixed = lambda shape: pl.BlockSpec(shape, lambda i: (0,) * len(shape), pipeline_mode=pl.Buffered(1))
    last = lambda w: pl.BlockSpec((None, TAIL, w), lambda i: (i // tps, 0, 0))
    ysc, xbc, dt, zg, tsc, tx = pl.pallas_call(
        functools.partial(_inproj_odd_kernel, tiles_per_seq=tps),
        grid=(n_rows // ROW_TILE,),
        in_specs=[row(d), fixed(w_bf16.shape), fixed((SC_WIDTH, SC_DIM)), fixed((1, SC_DIM)),
                  fixed((SSD_CONV, SSD_CONV_DIM)), fixed((1, SSD_CONV_DIM)), fixed((1, SSD_HEADS))],
        out_specs=[row(SC_DIM), row(SSD_CONV_DIM), row(SSD_HEADS), row(SSD_INNER), last(SC_DIM), last(SSD_CONV_DIM)],
        out_shape=[jax.ShapeDtypeStruct((n_rows, SC_DIM), f32), jax.ShapeDtypeStruct((n_rows, SSD_CONV_DIM), f32),
                   jax.ShapeDtypeStruct((n_rows, SSD_HEADS), f32), jax.ShapeDtypeStruct((n_rows, SSD_INNER), f32),
                   jax.ShapeDtypeStruct((bsz, TAIL, SC_DIM), f32), jax.ShapeDtypeStruct((bsz, TAIL, SSD_CONV_DIM), f32)],
        scratch_shapes=[pltpu.VMEM((TAIL, SC_DIM), f32), pltpu.VMEM((TAIL, SSD_CONV_DIM), f32)],
        compiler_params=_cparams("arbitrary"),
        name="inproj_odd",
    )(h, w_bf16, sc_w.astype(f32), sc_b.astype(f32).reshape(1, SC_DIM), cv_w.astype(f32),
      cv_b.astype(f32).reshape(1, SSD_CONV_DIM), dt_bias.astype(f32).reshape(1, SSD_HEADS))
    seq = lambda a: a.reshape(bsz, t, a.shape[-1])
    return (seq(ysc), seq(xbc), seq(dt), seq(zg),
            tsc[:, TAIL - (SC_WIDTH - 1):], tx[:, TAIL - (SSD_CONV - 1):])


def last_rows(x, n):
    t = x.shape[1]
    if t < n:
        x = jnp.pad(x, [(0, 0), (n - t, 0)] + [(0, 0)] * (x.ndim - 2))
    return x[:, x.shape[1] - n:]


def causal_conv(x, buf, w, b):
    t = x.shape[1]
    width = w.shape[0]
    xp = jnp.concatenate([buf, x], axis=1)
    y = b + sum(xp[:, j:j + t] * w[j] for j in range(width))
    return y, xp[:, xp.shape[1] - (width - 1):]


def even_prompt_mix(h, w_in_bf16, bt, s5p, cmpp, w_buf):
    t = h.shape[0] // bt
    u, q, rows, kvw, kvs_b, kvw_b, gates = inproj_even(h, w_in_bf16, jnp.arange(h.shape[0]) % t)
    seq = lambda a: a.reshape(bt, t, a.shape[-1])
    feat = KV_GROUPS * HEAD_DIM
    y_s5, s5_state = s5_scan(seq(u), jnp.zeros((bt, S5_GROUPS, S5_STATE, 2), jnp.float32), s5p, S5_CHUNK)
    rows = seq(rows)
    kc = compress_prompt(rows[..., 0:feat], compress_params(cmpp[0], cmpp[1], cmpp[2]))
    vc = compress_prompt(rows[..., feat:2 * feat], compress_params(cmpp[3], cmpp[4], cmpp[5]))
    y_nsa = nsa_prompt(seq(q), seq(gates), kc, vc, seq(kvs_b), jnp.pad(seq(kvw_b), ((0, 0), (WINDOW, 0), (0, 0))))
    new_rows = rows.reshape(bt, t, 4, KV_GROUPS, HEAD_DIM)
    return (y_s5, y_nsa), s5_state, new_rows, last_rows(seq(kvw).reshape(bt, t, 2, KV_GROUPS, HEAD_DIM), w_buf)


def even_sample_mix(h, w_in_bf16, bt, s5_h0, pool, page_table, win_buf, s5p, cmpp):
    f32 = jnp.float32
    t = h.shape[0] // bt
    pos = page_table.shape[1] * PAGE_SIZE + jnp.arange(h.shape[0]) % t
    u, q, rows, kvw, _, _, gates = inproj_even(h, w_in_bf16, pos)
    seq = lambda a: a.reshape(bt, t, a.shape[-1])
    feat = KV_GROUPS * HEAD_DIM
    y_s5, s5_state = s5_scan(seq(u), s5_h0.astype(f32), s5p, t)
    rows, kvw = seq(rows), seq(kvw)
    pool_t = pool.astype(f32).transpose(0, 2, 3, 4, 1).reshape(pool.shape[0], 4 * feat, PAGE_SIZE)
    kc, vc = compress_sample(pool_t, page_table, rows[..., 0:feat], rows[..., feat:2 * feat],
                             compress_sample_params(cmpp[0], cmpp[1], cmpp[2]),
                             compress_sample_params(cmpp[3], cmpp[4], cmpp[5]))
    w_buf = win_buf.shape[1]
    win_f = win_buf.astype(f32)
    y_nsa = nsa_sample(q.reshape(bt, t, N_HEADS, HEAD_DIM), gates.reshape(bt, t, N_HEADS, 3), kc, vc, pool_t,
                       page_table, rows[..., 2 * feat:3 * feat], rows[..., 3 * feat:4 * feat],
                       win_f.transpose(0, 2, 3, 4, 1).reshape(bt, 2 * feat, w_buf), kvw[..., 0:feat],
                       kvw[..., feat:2 * feat])
    new_rows = rows.reshape(bt, t, 4, KV_GROUPS, HEAD_DIM)
    win = jnp.concatenate([win_f, kvw.reshape(bt, t, 2, KV_GROUPS, HEAD_DIM)], axis=1)
    return (y_s5, y_nsa), s5_state, new_rows, win[:, t:]


def ssd_scan(x, dt, a, bm, cm, h0, chunk):
    bt, t, nh, p = x.shape
    nch = t // chunk
    r = nh // SSD_GROUPS
    tri = jnp.arange(chunk)[:, None] >= jnp.arange(chunk)[None, :]

    def to_chunks(v):
        return jnp.moveaxis(v.reshape((bt, nch, chunk) + v.shape[2:]), 1, 0)

    def step(h, inp):
        xc, dtc, bc, cc = inp
        cum = jnp.cumsum(dtc * a, axis=1)
        seg = cum[:, :, None, :] - cum[:, None, :, :]
        decay = jnp.exp(jnp.where(tri[None, :, :, None], seg, NEG)).reshape(bt, chunk, chunk, SSD_GROUPS, r)
        cb = jnp.einsum('btgn,bsgn->btsg', cc, bc)
        xg = xc.reshape(bt, chunk, SSD_GROUPS, r, p)
        dg = dtc.reshape(bt, chunk, SSD_GROUPS, r)
        w = cb[..., None] * decay * dg[:, None]
        y_intra = jnp.einsum('btsgr,bsgrp->btgrp', w, xg)
        hg = h.reshape(bt, SSD_GROUPS, r, p, SSD_STATE)
        y_inter = jnp.einsum('btgn,bgrpn->btgrp', cc, hg) * jnp.exp(cum).reshape(bt, chunk, SSD_GROUPS, r)[..., None]
        wt = (jnp.exp(cum[:, -1:, :] - cum) * dtc).reshape(bt, chunk, SSD_GROUPS, r)
        h_new = (hg * jnp.exp(cum[:, -1]).reshape(bt, SSD_GROUPS, r)[..., None, None]
                 + jnp.einsum('bsgr,bsgrp,bsgn->bgrpn', wt, xg, bc))
        return h_new.reshape(bt, nh, p, SSD_STATE), (y_intra + y_inter).reshape(bt, chunk, nh, p)

    h_fin, ys = lax.scan(step, h0, (to_chunks(x), to_chunks(dt), to_chunks(bm), to_chunks(cm)))
    return jnp.moveaxis(ys, 0, 1).reshape(bt, t, nh, p), h_fin


def gated_rmsnorm(y, z, g):
    v = y * jax.nn.silu(z)
    bt, t, _ = v.shape
    vg = v.reshape(bt, t, SSD_GROUPS, SSD_INNER // SSD_GROUPS)
    vg = vg * lax.rsqrt(jnp.mean(vg * vg, -1, keepdims=True) + RMS_EPS)
    return vg.reshape(bt, t, SSD_INNER) * g


def odd_prompt_mix(h, w_in_bf16, bsz, sc_w, sc_b, cv_w, cv_b, dt_bias, a_log, d_skip, norm_g):
    a = -jnp.exp(a_log.astype(jnp.float32))
    y_sc, xbc_c, dt, zg, new_sc, new_conv = inproj_odd_prompt(h, w_in_bf16, bsz, sc_w, sc_b, cv_w, cv_b, dt_bias)
    y, h_new = ssd_prompt(xbc_c, dt, a, zg, 0, d_skip, norm_g)
    return (y_sc, y), new_sc, new_conv, h_new


def odd_mix(z, sc_buf, conv_buf, h0, chunk, sc_w, sc_b, cv_w, cv_b, dt_bias, a_log, d_skip, norm_g):
    f32 = jnp.float32
    bt, t, _ = z.shape
    a = -jnp.exp(a_log.astype(f32))
    o1 = SC_DIM
    o2 = 2 * SC_DIM
    o3 = 3 * SC_DIM
    o4 = o3 + SSD_INNER
    o5 = o4 + SSD_CONV_DIM
    sc_h = z[..., :o1]
    sc_bg = z[..., o1:o2]
    sc_cg = z[..., o2:o3]
    zg = z[..., o3:o4]
    xbc = z[..., o4:o5]
    dt_raw = z[..., o5:]
    conv_sc, new_sc = causal_conv(sc_cg * sc_h, sc_buf.astype(f32), sc_w, sc_b)
    y_sc = sc_bg * conv_sc
    xbc_c, new_conv = causal_conv(xbc, conv_buf.astype(f32), cv_w, cv_b)
    xbc_c = jax.nn.silu(xbc_c)
    gn = SSD_GROUPS * SSD_STATE
    xs = xbc_c[..., :SSD_INNER].reshape(bt, t, SSD_HEADS, SSD_HEAD_DIM)
    bm = xbc_c[..., SSD_INNER:SSD_INNER + gn].reshape(bt, t, SSD_GROUPS, SSD_STATE)
    cm = xbc_c[..., SSD_INNER + gn:].reshape(bt, t, SSD_GROUPS, SSD_STATE)
    dt = jax.nn.softplus((dt_raw + dt_bias).astype(f32))
    y, h_new = ssd_scan(xs, dt, a, bm, cm, h0.astype(f32), chunk)
    y = (y + d_skip[:, None] * xs).reshape(bt, t, SSD_INNER)
    y = gated_rmsnorm(y, zg, norm_g)
    return (y_sc, y), new_sc, new_conv, h_new


def moe_ffn(x, logits, w_gu_bf16, w_down_bf16):
    n, d = x.shape
    top_v, top_i = lax.top_k(logits, TOP_K)
    gate = jax.nn.softmax(top_v, axis=-1)
    flat_e = top_i.reshape(-1)
    blk = 128
    assert (TOP_K * n) % blk == 0
    onehot = jax.nn.one_hot(flat_e, N_EXPERTS, dtype=jnp.float32).reshape(-1, blk, N_EXPERTS)
    tri = (jnp.arange(blk)[:, None] >= jnp.arange(blk)[None, :]).astype(jnp.float32)
    local = jnp.einsum('ij,bjk->bik', tri, onehot)
    block_total = local[:, -1, :]
    block_off = jnp.cumsum(block_total, axis=0) - block_total
    incl = (local + block_off[:, None, :]).reshape(-1, N_EXPERTS)
    rank = jnp.take_along_axis(incl, flat_e[:, None], axis=1)[:, 0].astype(jnp.int32) - 1
    counts = jnp.sum(block_total, axis=0).astype(jnp.int32)
    padded = ((counts + ROW_TILE - 1) // ROW_TILE) * ROW_TILE
    pad_start = jnp.cumsum(padded) - padded
    dest = (pad_start[flat_e] + rank).astype(jnp.int32)
    n_tiles = (TOP_K * n) // ROW_TILE + N_EXPERTS
    rows = n_tiles * ROW_TILE
    row_token = jnp.zeros((rows,), jnp.int32).at[dest].set(jnp.arange(TOP_K * n, dtype=jnp.int32) // TOP_K,
                                                           unique_indices=True, mode='promise_in_bounds')
    tile_end = jnp.cumsum(padded) // ROW_TILE
    tile_expert = jnp.minimum(jnp.searchsorted(tile_end, jnp.arange(n_tiles), side='right'),
                              N_EXPERTS - 1).astype(jnp.int32)
    n_used = tile_end[-1:].astype(jnp.int32)
    xs = x.at[row_token].get(mode='promise_in_bounds')
    ys = grouped_ffn(xs, w_gu_bf16, w_down_bf16, tile_expert, n_used)
    dest = dest.reshape(n, TOP_K)
    y0 = ys.at[dest[:, 0]].get(mode='promise_in_bounds')
    y1 = ys.at[dest[:, 1]].get(mode='promise_in_bounds')
    return y0, y1, gate


def kernel(x_prompt, x_sample, state_s5, cache_nsa_kv, state_win_kv, state_sc_conv, state_ssd_conv, state_ssd,
           page_table, ln_g, ln_b, w_in_even, s5_lam_re, s5_lam_im, s5_log_dt, s5_b, s5_c, s5_d, s5_w_glu,
           nsa_wk1, nsa_wk2, nsa_pe_k, nsa_wv1, nsa_wv2, nsa_pe_v, w_out_even, ffn_w_gu, ffn_w_down,
           w_in_odd, sc_conv_w, sc_conv_b, ssd_conv_w, ssd_conv_b, ssd_dt_bias, ssd_a_log, ssd_d, ssd_norm_g,
           w_out_odd, moe_router, moe_router_b, moe_w_gu, moe_w_down):
    f32 = jnp.float32
    bf16 = jnp.bfloat16
    bp, tp, d = x_prompt.shape
    bs, ts, _ = x_sample.shape
    n_p = bp * tp
    n_s = bs * ts
    w_buf = state_win_kv.shape[2]
    streams = [x_prompt.astype(f32).reshape(n_p, d), x_sample.astype(f32).reshape(n_s, d)]

    def flat(parts, n_rows):
        return [p.reshape(n_rows, p.shape[-1]) for p in parts]

    def out_proj(parts, w_out, width, h, g, b, router=None):
        w = w_out.astype(bf16)
        return matmul(flat(parts, h.shape[0]), [w[:width], w[width:]], ln=(h, g, b), router=router)

    def single_expert(n_rows):
        n_tiles = pl.cdiv(n_rows, min(ROW_TILE, n_rows))
        return jnp.zeros((n_tiles,), jnp.int32), jnp.full((1,), n_tiles, jnp.int32)

    s5p = s5_params(s5_lam_re[0], s5_lam_im[0], s5_log_dt[0], s5_b[0], s5_c[0], s5_d[0], s5_w_glu[0])
    cmpp = (nsa_wk1[0], nsa_wk2[0], nsa_pe_k[0], nsa_wv1[0], nsa_wv2[0], nsa_pe_v[0])
    w_in = w_in_even[0].astype(bf16)
    mix_p, s5_p, kv_p, win_p = even_prompt_mix(streams[0], w_in, bp, s5p, cmpp, w_buf)
    mix_s, s5_s, kv_s, win_s = even_sample_mix(streams[1], w_in, bs, state_s5[0], cache_nsa_kv[0], page_table,
                                               state_win_kv[0], s5p, cmpp)
    streams = [out_proj(mix, w_out_even[0], S5_DIM, h, ln_g[0, 0], ln_b[0, 0])
               for mix, h in zip((mix_p, mix_s), streams)]
    w_gu, w_down = ffn_w_gu.astype(bf16), ffn_w_down.astype(bf16)
    streams = [grouped_ffn(h, w_gu, w_down, *single_expert(h.shape[0]), ln=(ln_g[0, 1], ln_b[0, 1]))
               for h in streams]

    oddp = (sc_conv_w[0], sc_conv_b[0], ssd_conv_w[0], ssd_conv_b[0], ssd_dt_bias[0],
            ssd_a_log[0], ssd_d[0], ssd_norm_g[0])
    w_in = w_in_odd[0].astype(bf16)
    mix_p, scc_p, sdc_p, ssd_p = odd_prompt_mix(streams[0], w_in, bp, *oddp)
    zs = matmul([streams[1]], [w_in]).reshape(bs, ts, -1)
    mix_s, scc_s, sdc_s, ssd_s = odd_mix(zs, state_sc_conv[0], state_ssd_conv[0], state_ssd[0], ts, *oddp)
    outs = [out_proj(mix, w_out_odd[0], SC_DIM, h, ln_g[1, 0], ln_b[1, 0], router=(moe_router[0], moe_router_b[0]))
            for mix, h in zip((mix_p, mix_s), streams)]
    h = jnp.concatenate([o[0] for o in outs], axis=0)
    logits = jnp.concatenate([o[1] for o in outs], axis=0)[:, :N_EXPERTS]
    y0, y1, gate = moe_ffn(h, logits, moe_w_gu[0].astype(bf16), moe_w_down[0].astype(bf16))
    hp = moe_combine_ln(h, y0, y1, gate, ln_g[1, 1], ln_b[1, 1], 0, n_p).reshape(bp, tp, d)
    hs = moe_combine_ln(h, y0, y1, gate, ln_g[1, 1], ln_b[1, 1], n_p, n_s).reshape(bs, ts, d)
    st = lambda a, ref: a[None].astype(ref.dtype)
    return (hp.astype(x_prompt.dtype), hs.astype(x_sample.dtype),
            st(s5_p, state_s5), st(s5_s, state_s5),
            st(kv_p, cache_nsa_kv), st(kv_s, cache_nsa_kv),
            st(win_p, state_win_kv), st(win_s, state_win_kv),
            st(scc_p, state_sc_conv), st(scc_s, state_sc_conv),
            st(sdc_p, state_ssd_conv), st(sdc_s, state_ssd_conv),
            st(ssd_p, state_ssd), st(ssd_s, state_ssd))
```
